```python
import jax, jax.numpy as jnp
from jax import lax
import numpy as np

D_MODEL = 2048
BATCH = 4
SEQ = 2048
DEPTH = 1
DEC_BATCH = 8
DEC_SEQ = 4
PAST_LEN = 16384
PAGE_SIZE = 128

HEAD_DIM = 128
CONV_DIM = D_MODEL // 4
CONV_W = 3
NSA_WIDTH = D_MODEL // 2
NSA_HEADS = NSA_WIDTH // HEAD_DIM
NSA_KV = 2
NSA_HPG = NSA_HEADS // NSA_KV
KV_WIDTH = NSA_KV * HEAD_DIM
MEM_HEADS = 4
MEM_WIDTH = MEM_HEADS * HEAD_DIM
N_MEM = 256
MIX_WIDTH = CONV_DIM + NSA_WIDTH + MEM_WIDTH
ROPE_DIM = HEAD_DIM // 4
ROPE_THETA = 500000.0
CMP_BLOCK = 64
SEL_BLOCK = 64
TOP_N = 16
WINDOW = 512
Q_BLOCK = 128
N_BRANCH = 3
N_KV_PARTS = 4
NORM_EPS = 1e-6
MASK_NEG = -1e30
FORCE = 1e9
ATTN_SCALE = HEAD_DIM ** -0.5

IN_SPLITS = (CONV_DIM, CONV_DIM, CONV_DIM, CONV_DIM,
             NSA_WIDTH, NSA_WIDTH, NSA_HEADS * N_BRANCH,
             KV_WIDTH * 6,
             MEM_WIDTH, MEM_WIDTH)
IN_WIDTH = sum(IN_SPLITS)

kernel_name = 'hymba_conv_nsa_mem_decode_step'


def rms_norm(x, g):
    xf = x.astype(jnp.float32)
    y = xf * lax.rsqrt(jnp.mean(xf * xf, axis=-1, keepdims=True) + NORM_EPS)
    return (y * g.astype(jnp.float32)).astype(x.dtype)


def rope(x, pos):
    half = ROPE_DIM // 2
    freqs = jnp.power(ROPE_THETA, -jnp.arange(half, dtype=jnp.float32) * (2.0 / ROPE_DIM))
    ang = pos.astype(jnp.float32)[:, None] * freqs[None, :]
    cos = jnp.cos(ang)[:, None, :]
    sin = jnp.sin(ang)[:, None, :]
    xf = x.astype(jnp.float32)
    x1 = xf[..., :half]
    x2 = xf[..., half:ROPE_DIM]
    out = jnp.concatenate([x1 * cos - x2 * sin, x2 * cos + x1 * sin, xf[..., ROPE_DIM:]], axis=-1)
    return out.astype(x.dtype)


def masked_softmax(s, mask, axis=-1):
    s = jnp.where(mask, s.astype(jnp.float32), MASK_NEG)
    p = jax.nn.softmax(s, axis=axis)
    return jnp.where(mask, p, 0.0)


def split_cols(z):
    idx = np.cumsum(IN_SPLITS)[:-1].tolist()
    return jnp.split(z, idx, axis=-1)


def gqa_attend(q, k, v, mask):
    B, T, H, D = q.shape
    G = k.shape[2]
    qg = q.reshape(B, T, G, H // G, D)
    s = jnp.einsum('btghd,bsgd->btghs', qg, k) * ATTN_SCALE
    if mask is None:
        p = jax.nn.softmax(s.astype(jnp.float32), axis=-1)
    else:
        p = masked_softmax(s, mask)
    o = jnp.einsum('btghs,bsgd->btghd', p.astype(v.dtype), v)
    return o.reshape(B, T, H, D)


def short_conv_mixer(h, b, c, conv_prev, w_conv):
    T = h.shape[1]
    u = c * h
    up = jnp.concatenate([conv_prev.astype(u.dtype), u], axis=1)
    y = w_conv[0] * up[:, 0:T]
    for j in range(1, CONV_W):
        y = y + w_conv[j] * up[:, j:j + T]
    return b * y, up[:, T:]


def compress(kv, a, w):
    B, L = kv.shape[:2]
    n = L // CMP_BLOCK
    blocks = kv[:, :n * CMP_BLOCK].reshape(B, n, CMP_BLOCK, NSA_KV, HEAD_DIM)
    pooled = jnp.einsum('bnjgd,jd->bngd', blocks, a)
    return jnp.einsum('bngd,de->bnge', pooled, w)


def nsa_cmp_sel(q, q_rot, q_pos, k_cmp, v_cmp, k_sel, v_sel):
    B, T = q.shape[:2]
    L = k_sel.shape[1]
    n_cmp = k_cmp.shape[1]
    n_sel = -(-L // SEL_BLOCK)
    top = min(TOP_N, n_sel)
    pad = n_sel * SEL_BLOCK - L
    padw = ((0, 0), (0, pad), (0, 0), (0, 0))
    kb = jnp.pad(k_sel, padw).reshape(B, n_sel, SEL_BLOCK, NSA_KV, HEAD_DIM).transpose(0, 3, 1, 2, 4)
    vb = jnp.pad(v_sel, padw).reshape(B, n_sel, SEL_BLOCK, NSA_KV, HEAD_DIM).transpose(0, 3, 1, 2, 4)
    cmp_end = (jnp.arange(n_cmp, dtype=jnp.int32) + 1) * CMP_BLOCK
    blk = jnp.arange(n_sel, dtype=jnp.int32)
    bidx = jnp.arange(B)[:, None, None, None]
    gidx = jnp.arange(NSA_KV)[None, :, None, None]
    qb = Q_BLOCK if T % Q_BLOCK == 0 else T
    nq = T // qb

    def one_block(args):
        qc, qrc, pc = args
        qg = qc.reshape(B, qb, NSA_KV, NSA_HPG, HEAD_DIM)
        s = jnp.einsum('btghd,bngd->btghn', qg, k_cmp) * ATTN_SCALE
        cmask = (cmp_end[None, :] <= pc[:, None] + 1)[None, :, None, None, :]
        p = masked_softmax(s, cmask)
        o_c = jnp.einsum('btghn,bngd->btghd', p.astype(v_cmp.dtype), v_cmp)
        imp = jnp.pad(p.sum(axis=3), ((0, 0), (0, 0), (0, 0), (0, n_sel - n_cmp)))
        cur = (pc // SEL_BLOCK)[:, None]
        forced = (blk[None] == 0) | (blk[None] == cur) | (blk[None] == cur - 1)
        imp = jnp.where(forced[None, :, None, :], FORCE, imp)
        imp = jnp.where((blk[None] > cur)[None, :, None, :], -1.0, imp)
        _, idx = lax.top_k(imp, top)
        idx = idx.transpose(0, 2, 1, 3)
        kg = kb[bidx, gidx, idx]
        vg = vb[bidx, gidx, idx]
        qrg = qrc.reshape(B, qb, NSA_KV, NSA_HPG, HEAD_DIM).transpose(0, 2, 1, 3, 4)
        s2 = jnp.einsum('bgthd,bgtkjd->bgthkj', qrg, kg) * ATTN_SCALE
        kpos = idx[..., None] * SEL_BLOCK + jnp.arange(SEL_BLOCK, dtype=jnp.int32)
        smask = (kpos <= pc[None, None, :, None, None])[:, :, :, None]
        p2 = masked_softmax(s2, smask, axis=(-2, -1))
        o_s = jnp.einsum('bgthkj,bgtkjd->bgthd', p2.astype(vg.dtype), vg)
        o_c = o_c.reshape(B, qb, NSA_HEADS, HEAD_DIM)
        o_s = o_s.transpose(0, 2, 1, 3, 4).reshape(B, qb, NSA_HEADS, HEAD_DIM)
        return o_c, o_s

    def split(a):
        return a.reshape(B, nq, qb, *a.shape[2:]).swapaxes(0, 1)

    o_c, o_s = lax.map(one_block, (split(q), split(q_rot), q_pos.reshape(nq, qb)))

    def merge(a):
        return a.swapaxes(0, 1).reshape(B, T, NSA_HEADS, HEAD_DIM)

    return merge(o_c), merge(o_s)


def window_banded(q_rot, k, v):
    B, T = q_rot.shape[:2]
    nb = T // Q_BLOCK
    nw = WINDOW // Q_BLOCK
    padw = ((0, 0), (WINDOW, 0), (0, 0), (0, 0))
    kp = jnp.pad(k, padw).reshape(B, nb + nw, Q_BLOCK, NSA_KV, HEAD_DIM)
    vp = jnp.pad(v, padw).reshape(B, nb + nw, Q_BLOCK, NSA_KV, HEAD_DIM)
    ks = jnp.concatenate([kp[:, j:j + nb] for j in range(nw + 1)], axis=2)
    vs = jnp.concatenate([vp[:, j:j + nb] for j in range(nw + 1)], axis=2)
    qpos = jnp.arange(T, dtype=jnp.int32).reshape(nb, Q_BLOCK)
    kpos = (jnp.arange(nb, dtype=jnp.int32)[:, None] * Q_BLOCK - WINDOW) + jnp.arange((nw + 1) * Q_BLOCK, dtype=jnp.int32)[None]
    kq = kpos[:, None, :]
    mask = (kq >= 0) & (kq <= qpos[:, :, None]) & (kq > qpos[:, :, None] - WINDOW)
    qg = q_rot.reshape(B, nb, Q_BLOCK, NSA_KV, NSA_HPG, HEAD_DIM)
    s = jnp.einsum('bnqghd,bnsgd->bnqghs', qg, ks) * ATTN_SCALE
    p = masked_softmax(s, mask[None, :, :, None, None, :])
    o = jnp.einsum('bnqghs,bnsgd->bnqghd', p.astype(vs.dtype), vs)
    return o.reshape(B, T, NSA_HEADS, HEAD_DIM)


def decoder_layer(x, past_kv, win_prev, conv_prev, mem_kv, norm_g, w_in, w_conv, a_cmp, w_cmp, w_out):
    B, T, _ = x.shape
    start = 0 if past_kv is None else past_kv.shape[1]
    pos = start + jnp.arange(T, dtype=jnp.int32)
    h = rms_norm(x, norm_g)
    c_h, c_b, c_c, c_g, q, n_g, b_g, kv6, m_q, m_g = split_cols(h @ w_in)

    y_a, conv_new = short_conv_mixer(c_h, c_b, c_c, conv_prev, w_conv)
    y_a = jax.nn.silu(c_g) * y_a

    q = q.reshape(B, T, NSA_HEADS, HEAD_DIM)
    q_rot = rope(q, pos)
    kv6 = kv6.reshape(B, T, 6, NSA_KV, HEAD_DIM)
    kv_new = jnp.stack([kv6[:, :, 0], kv6[:, :, 1], rope(kv6[:, :, 2], pos), kv6[:, :, 3]], axis=2)
    win_new = jnp.stack([rope(kv6[:, :, 4], pos), kv6[:, :, 5]], axis=2)
    kv_all = kv_new if past_kv is None else jnp.concatenate([past_kv.astype(kv_new.dtype), kv_new], axis=1)
    k_cmp = compress(kv_all[:, :, 0], a_cmp[0], w_cmp[0])
    v_cmp = compress(kv_all[:, :, 1], a_cmp[1], w_cmp[1])
    o_cmp, o_sel = nsa_cmp_sel(q, q_rot, pos, k_cmp, v_cmp, kv_all[:, :, 2], kv_all[:, :, 3])
    if win_prev is None:
        o_win = window_banded(q_rot, win_new[:, :, 0], win_new[:, :, 1])
        win_state = win_new[:, T - min(WINDOW, T):]
    else:
        win_all = jnp.concatenate([win_prev.astype(win_new.dtype), win_new], axis=1)
        S = win_all.shape[1]
        kpos = start + T - S + jnp.arange(S, dtype=jnp.int32)
        wmask = (kpos[None, :] <= pos[:, None]) & (kpos[None, :] > pos[:, None] - WINDOW)
        o_win = gqa_attend(q_rot, win_all[:, :, 0], win_all[:, :, 1], wmask[None, :, None, None, :])
        win_state = win_all[:, S - win_prev.shape[1]:]
    g = jax.nn.sigmoid(b_g.astype(jnp.float32)).reshape(B, T, NSA_HEADS, N_BRANCH).astype(x.dtype)
    o_nsa = g[..., 0:1] * o_cmp + g[..., 1:2] * o_sel + g[..., 2:3] * o_win
    y_b = jax.nn.silu(n_g) * o_nsa.reshape(B, T, NSA_WIDTH)

    o_m = gqa_attend(m_q.reshape(B, T, MEM_HEADS, HEAD_DIM), mem_kv[:, :, 0], mem_kv[:, :, 1], None)
    y_m = jax.nn.silu(m_g) * o_m.reshape(B, T, MEM_WIDTH)

    mixed = jnp.concatenate([y_a, y_b, y_m], axis=-1)
    return x + mixed @ w_out, kv_new, win_state, conv_new


def setup_inputs(seed: int = 0) -> dict:
    key = jax.random.key(seed)
    ks = jax.random.split(key, 18)
    f32 = jnp.float32
    n_pages = PAST_LEN // PAGE_SIZE
    n_pool = (DEC_BATCH * n_pages * 5) // 4
    w_buf = min(WINDOW, PAST_LEN)

    def nrm(k, shape, s=1.0):
        return jax.random.normal(k, shape, f32) * s

    perm = jax.random.permutation(ks[6], n_pool)
    return {
        'x_prompt': nrm(ks[0], (BATCH, SEQ, D_MODEL)),
        'x_sample': nrm(ks[1], (DEC_BATCH, DEC_SEQ, D_MODEL)),
        'cache_kv': nrm(ks[2], (DEPTH, n_pool, PAGE_SIZE, N_KV_PARTS, NSA_KV, HEAD_DIM)),
        'cache_win': nrm(ks[3], (DEPTH, DEC_BATCH, w_buf, 2, NSA_KV, HEAD_DIM)),
        'state_conv': nrm(ks[4], (DEPTH, DEC_BATCH, CONV_W - 1, CONV_DIM)),
        'cache_mem': nrm(ks[5], (DEPTH, DEC_BATCH, N_MEM, 2, MEM_HEADS, HEAD_DIM)),
        'page_table': perm[:DEC_BATCH * n_pages].reshape(DEC_BATCH, n_pages).astype(jnp.int32),
        'mem_prompt': nrm(ks[7], (BATCH, N_MEM, D_MODEL)),
        'norm_g': 1.0 + nrm(ks[8], (DEPTH, D_MODEL), 0.02),
        'w_in': nrm(ks[9], (DEPTH, D_MODEL, IN_WIDTH), D_MODEL ** -0.5),
        'w_conv': nrm(ks[10], (DEPTH, CONV_W, CONV_DIM), CONV_W ** -0.5),
        'a_cmp': nrm(ks[11], (DEPTH, 2, CMP_BLOCK, HEAD_DIM), CMP_BLOCK ** -0.5),
        'w_cmp': nrm(ks[12], (DEPTH, 2, HEAD_DIM, HEAD_DIM), HEAD_DIM ** -0.5),
        'mem_norm_g': 1.0 + nrm(ks[13], (DEPTH, D_MODEL), 0.02),
        'w_mem_kv': nrm(ks[14], (DEPTH, D_MODEL, 2 * MEM_WIDTH), D_MODEL ** -0.5),
        'w_out': nrm(ks[15], (DEPTH, MIX_WIDTH, D_MODEL), MIX_WIDTH ** -0.5),
        'final_g': 1.0 + nrm(ks[16], (D_MODEL,), 0.02),
    }


def reference(x_prompt, x_sample, cache_kv, cache_win, state_conv, cache_mem, page_table, mem_prompt,
              norm_g, w_in, w_conv, a_cmp, w_cmp, mem_norm_g, w_mem_kv, w_out, final_g):
    xp, xs = x_prompt, x_sample
    kv_p, win_p, conv_p, mem_p, kv_s, win_s, conv_s = [], [], [], [], [], [], []
    for l in range(DEPTH):
        lw = (norm_g[l], w_in[l], w_conv[l], a_cmp[l], w_cmp[l], w_out[l])
        mkv = (rms_norm(mem_prompt, mem_norm_g[l]) @ w_mem_kv[l]).reshape(
            mem_prompt.shape[0], N_MEM, 2, MEM_HEADS, HEAD_DIM)
        conv0 = jnp.zeros((xp.shape[0], CONV_W - 1, CONV_DIM), xp.dtype)
        xp, kvn, winn, convn = decoder_layer(xp, None, None, conv0, mkv, *lw)
        kv_p.append(kvn)
        win_p.append(winn)
        conv_p.append(convn)
        mem_p.append(mkv)
        past = cache_kv[l][page_table]
        past = past.reshape(past.shape[0], past.shape[1] * past.shape[2], *past.shape[3:])
        xs, kvn, winn, convn = decoder_layer(xs, past, cache_win[l], state_conv[l], cache_mem[l], *lw)
        kv_s.append(kvn)
        win_s.append(winn)
        conv_s.append(convn)
    y_prompt = rms_norm(xp, final_g)
    y_sample = rms_norm(xs, final_g)
    return (y_prompt, y_sample, jnp.stack(kv_p), jnp.stack(win_p), jnp.stack(conv_p), jnp.stack(mem_p),
            jnp.stack(kv_s), jnp.stack(win_s), jnp.stack(conv_s))
```

```python
import functools

import jax
import jax.numpy as jnp
from jax import lax
from jax.experimental import pallas as pl
from jax.experimental.pallas import tpu as pltpu

F32 = jnp.float32
BF16 = jnp.bfloat16

HEAD_DIM = 128
CONV_DIM = 512
CONV_W = 3
NSA_HEADS = 8
NSA_KV = 2
NSA_HPG = NSA_HEADS // NSA_KV
MEM_HEADS = 4
N_BRANCH = 3
ROPE_DIM = HEAD_DIM // 4
ROPE_HALF = ROPE_DIM // 2
ROPE_THETA = 500000.0
CMP_BLOCK = 64
SEL_BLOCK = 64
TOP_N = 16
WINDOW = 512
NORM_EPS = 1e-6
MASK_NEG = -1e30
FORCE = 1e9
ATTN_SCALE = HEAD_DIM ** -0.5

C_H, C_B, C_C, C_G = 0, 512, 1024, 1536
Q_OFF, NG_OFF, KV_OFF, MQ_OFF, MG_OFF = 2048, 3072, 4096, 5632, 6144
Z_WIDTH = 6656
BG_SRC = 4096
BG_N = NSA_HEADS * N_BRANCH
LANE = 128
SAMPLE_T_PAD = 8
VMEM_LIMIT = 48 * 1024 * 1024


def _nt(a, b):
    return lax.dot_general(a, b, (((1,), (1,)), ((), ())), preferred_element_type=F32)


def _nn(a, b):
    return jnp.dot(a, b, preferred_element_type=F32)


def _params(sem, vmem=VMEM_LIMIT):
    return pltpu.CompilerParams(dimension_semantics=sem, vmem_limit_bytes=vmem)


def _rope(x, c, s1, s2):
    return x * c + pltpu.roll(x, ROPE_HALF, 1) * s1 + pltpu.roll(x, LANE - ROPE_HALF, 1) * s2


def _silu(x):
    return x * jax.nn.sigmoid(x)


def _norm_matmul_kernel(x_ref, g_ref, w_ref, *rest, with_gate):
    if with_gate:
        wbg_ref, z_ref, bg_ref, h_scr = rest
    else:
        z_ref, h_scr = rest

    @pl.when(pl.program_id(1) == 0)
    def _():
        x = x_ref[...]
        y = x * lax.rsqrt(jnp.mean(x * x, axis=-1, keepdims=True) + NORM_EPS) * g_ref[...]
        h = y.astype(BF16)
        h_scr[...] = h
        if with_gate:
            bg_ref[...] = _nn(h, wbg_ref[...])

    z_ref[...] = _nn(h_scr[...], w_ref[...])


def _norm_matmul(x, g, w, wbg=None, *, tm, tn):
    m, d = x.shape
    n = w.shape[1]
    with_gate = wbg is not None
    in_specs = [pl.BlockSpec((tm, d), lambda i, j: (i, 0)),
                pl.BlockSpec((1, d), lambda i, j: (0, 0)),
                pl.BlockSpec((d, tn), lambda i, j: (0, j))]
    out_shape = [jax.ShapeDtypeStruct((m, n), F32)]
    out_specs = [pl.BlockSpec((tm, tn), lambda i, j: (i, j))]
    args = [x, g.reshape(1, d), w]
    if with_gate:
        nb = wbg.shape[1]
        in_specs.append(pl.BlockSpec((d, nb), lambda i, j: (0, 0)))
        out_shape.append(jax.ShapeDtypeStruct((m, nb), F32))
        out_specs.append(pl.BlockSpec((tm, nb), lambda i, j: (i, 0)))
        args.append(wbg)
    res = pl.pallas_call(
        functools.partial(_norm_matmul_kernel, with_gate=with_gate),
        out_shape=out_shape, grid=(m // tm, n // tn), in_specs=in_specs, out_specs=out_specs,
        scratch_shapes=[pltpu.VMEM((tm, d), BF16)],
        compiler_params=_params(("parallel", "arbitrary")),
        name="norm_matmul_gate" if with_gate else "norm_matmul",
    )(*args)
    return res if with_gate else res[0]


def _kv_export_kernel(kv1_ref, kv2_ref, kv0_ref, c_ref, s1_ref, s2_ref, kvn_ref, kvb_ref, win_ref):
    c, s1, s2 = c_ref[...], s1_ref[...], s2_ref[...]
    kv0 = kv0_ref[0]
    kv1 = kv1_ref[0]
    kv2 = kv2_ref[0]
    kvn_ref[0, :, 0:512] = kv0
    for g in range(NSA_KV):
        lo, hi = g * LANE, (g + 1) * LANE
        ks = _rope(kv1[:, lo:hi], c, s1, s2)
        kw = _rope(kv2[:, lo:hi], c, s1, s2)
        kvn_ref[0, :, 512 + lo:512 + hi] = ks
        kvb_ref[0, :, lo:hi] = ks.astype(BF16)
        kvb_ref[0, :, 512 + lo:512 + hi] = kw.astype(BF16)
        win_ref[0, :, lo:hi] = kw
    kvn_ref[0, :, 768:1024] = kv1[:, 256:512]
    kvb_ref[0, :, 256:512] = kv1[:, 256:512].astype(BF16)
    kvb_ref[0, :, 768:1024] = kv2[:, 256:512].astype(BF16)
    win_ref[0, :, 256:512] = kv2[:, 256:512]


def _kv_export(z3, tabs, *, tr):
    b, t, _ = z3.shape
    nt = t // tr
    kvblk = KV_OFF // 512
    zspec = lambda k: pl.BlockSpec((1, tr, 512), lambda bi, ti, k=k: (bi, ti, kvblk + k))
    tspec = pl.BlockSpec((tr, LANE), lambda bi, ti: (ti, 0))
    return pl.pallas_call(
        _kv_export_kernel,
        out_shape=[jax.ShapeDtypeStruct((b, t, 1024), F32),
                   jax.ShapeDtypeStruct((b, t, 1024), BF16),
                   jax.ShapeDtypeStruct((b, t, 512), F32)],
        grid=(b, nt),
        in_specs=[zspec(1), zspec(2), zspec(0), tspec, tspec, tspec],
        out_specs=[pl.BlockSpec((1, tr, 1024), lambda bi, ti: (bi, ti, 0)),
                   pl.BlockSpec((1, tr, 1024), lambda bi, ti: (bi, ti, 0)),
                   pl.BlockSpec((1, tr, 512), lambda bi, ti: (bi, ti, 0))],
        compiler_params=_params(("parallel", "parallel")),
        name="kv_export",
    )(z3, z3, z3, *tabs)


def _pool_kernel(*refs, n_prefetch):
    x_ref, a_ref, o_ref = refs[n_prefetch:]
    x = x_ref[0]
    rows = x.shape[0]
    x3 = x.reshape(rows // CMP_BLOCK, CMP_BLOCK, x.shape[1])
    pooled = jnp.sum(x3 * a_ref[...][None], axis=1)
    o_ref[...] = pooled.reshape(o_ref.shape)


def _pool_prompt(z3, a4):
    b, t, _ = z3.shape
    n = t // CMP_BLOCK
    return pl.pallas_call(
        functools.partial(_pool_kernel, n_prefetch=0),
        out_shape=jax.ShapeDtypeStruct((b, n, 512), F32),
        grid=(b,),
        in_specs=[pl.BlockSpec((1, t, 512), lambda bi: (bi, 0, KV_OFF // 512)),
                  pl.BlockSpec((CMP_BLOCK, 512), lambda bi: (0, 0))],
        out_specs=pl.BlockSpec((1, n, 512), lambda bi: (bi, 0, 0)),
        compiler_params=_params(("parallel",)),
        name="pool_prompt",
    )(z3, a4)


def _pool_pages(cache3, page_flat, a4, *, bs, n_pages, page_base):
    page = cache3.shape[1]
    per = page // CMP_BLOCK
    grid_spec = pltpu.PrefetchScalarGridSpec(
        num_scalar_prefetch=1, grid=(bs, n_pages),
        in_specs=[pl.BlockSpec((1, page, 512), lambda bi, pi, pt: (pt[bi * n_pages + pi] + page_base, 0, 0)),
                  pl.BlockSpec((CMP_BLOCK, 512), lambda bi, pi, pt: (0, 0))],
        out_specs=pl.BlockSpec((1, 1, per, 512), lambda bi, pi, pt: (bi, pi, 0, 0)))
    out = pl.pallas_call(
        functools.partial(_pool_kernel, n_prefetch=1),
        out_shape=jax.ShapeDtypeStruct((bs, n_pages, per, 512), F32),
        grid_spec=grid_spec,
        compiler_params=_params(("parallel", "arbitrary")),
        name="pool_pages",
    )(page_flat, cache3, a4)
    return out.reshape(bs, n_pages * per, 512)


def _cmp_proj_kernel(p_ref, w_ref, kc_ref, vc_ref):
    pooled = p_ref[0]
    n = pooled.shape[0]
    n_pad = kc_ref.shape[2]
    for c in range(4):
        r = _nn(pooled[:, c * LANE:(c + 1) * LANE].astype(BF16), w_ref[c]).astype(BF16)
        dst = kc_ref if c < 2 else vc_ref
        if n_pad > n:
            dst[0, c % 2] = jnp.zeros((n_pad, LANE), BF16)
        dst[0, c % 2, 0:n, :] = r


def _cmp_proj(pooled, w4):
    b, n, _ = pooled.shape
    n_pad = -(-n // LANE) * LANE
    spec = pl.BlockSpec((1, NSA_KV, n_pad, LANE), lambda bi: (bi, 0, 0, 0))
    return pl.pallas_call(
        _cmp_proj_kernel,
        out_shape=[jax.ShapeDtypeStruct((b, NSA_KV, n_pad, LANE), BF16)] * 2,
        grid=(b,),
        in_specs=[pl.BlockSpec((1, n, 512), lambda bi: (bi, 0, 0)),
                  pl.BlockSpec((4, LANE, LANE), lambda bi: (0, 0, 0))],
        out_specs=[spec, spec],
        compiler_params=_params(("parallel",)),
        name="cmp_proj",
    )(pooled, w4)


def _nsa_prompt_kernel(q_ref, ng_ref, bg_ref, ksel_ref, vsel_ref, kwin_ref, vwin_ref, kc_ref, vc_ref,
                       c_ref, s1_ref, s2_ref, o_ref, m_scr, l_scr, acc_scr, *, t_len, tq, tc, tw, n_sel, top):
    i = pl.program_id(2)
    hq = NSA_HPG
    q = q_ref[0]
    c, s1, s2 = c_ref[...], s1_ref[...], s2_ref[...]
    qc_l, qr_l = [], []
    for h in range(hq):
        qh = q[:, h * LANE:(h + 1) * LANE]
        qc_l.append(qh * ATTN_SCALE)
        qr_l.append(_rope(qh, c, s1, s2) * ATTN_SCALE)
    qc = jnp.concatenate(qc_l, axis=0).astype(BF16)
    qr = jnp.concatenate(qr_l, axis=0).astype(BF16)

    kc = kc_ref[0, 0]
    npad = kc.shape[0]
    s = _nt(qc, kc).reshape(hq, tq, npad)
    tpos = i * tq + lax.broadcasted_iota(jnp.int32, (tq, npad), 0)
    ncol = lax.broadcasted_iota(jnp.int32, (tq, npad), 1)
    cmask = ((ncol + 1) * CMP_BLOCK <= tpos + 1)[None]
    s = jnp.where(cmask, s, MASK_NEG)
    e = jnp.where(cmask, jnp.exp(s - jnp.max(s, axis=-1, keepdims=True)), 0.0)
    p = e / jnp.maximum(jnp.sum(e, axis=-1, keepdims=True), 1e-30)
    o_c = _nn(p.reshape(hq * tq, npad).astype(BF16), vc_ref[0, 0])
    imp = jnp.sum(p, axis=0)

    rows = min(npad, -(-n_sel // 8) * 8)
    imp_t = imp.T[0:rows]
    blk = lax.broadcasted_iota(jnp.int32, (rows, tq), 0)
    cur = (i * tq + lax.broadcasted_iota(jnp.int32, (rows, tq), 1)) // SEL_BLOCK
    imp_t = jnp.where((blk == 0) | (blk == cur) | (blk == cur - 1), FORCE, imp_t)
    imp_t = jnp.where(blk > cur, -1.0, imp_t)
    imp_t = jnp.where(blk >= n_sel, -2.0, imp_t)
    rank = jnp.zeros((rows, tq), F32)
    for j in range(n_sel):
        a = imp_t[j:j + 1, :]
        ahead = (a > imp_t) | ((a == imp_t) & (blk > j))
        rank = rank + jnp.where(ahead, 1.0, 0.0)
    sel_t = jnp.where((rank < top) & (blk < n_sel), 1.0, 0.0)
    if npad > rows:
        sel_t = jnp.concatenate([sel_t, jnp.zeros((npad - rows, tq), F32)], axis=0)
    sel = sel_t.T.astype(BF16)

    m_scr[...] = jnp.full(m_scr.shape, MASK_NEG, F32)
    l_scr[...] = jnp.zeros(l_scr.shape, F32)
    acc_scr[...] = jnp.zeros(acc_scr.shape, F32)
    n_chunks = (i * tq + tq + tc - 1) // tc

    def chunk(ci, carry):
        k0 = pl.multiple_of(ci * tc, tc)
        k = ksel_ref[0, pl.ds(k0, tc), :]
        v = vsel_ref[0, pl.ds(k0, tc), :]
        sc = _nt(qr, k)
        jb = lax.broadcasted_iota(jnp.int32, (npad, tc), 0)
        kb = (k0 + lax.broadcasted_iota(jnp.int32, (npad, tc), 1)) // SEL_BLOCK
        expand = jnp.where(jb == kb, 1.0, 0.0).astype(BF16)
        chosen = _nn(sel, expand)
        kp = k0 + lax.broadcasted_iota(jnp.int32, (tq, tc), 1)
        tp = i * tq + lax.broadcasted_iota(jnp.int32, (tq, tc), 0)
        bias = jnp.where((chosen > 0.5) & (kp <= tp), 0.0, MASK_NEG)
        sc = (sc.reshape(hq, tq, tc) + bias[None]).reshape(hq * tq, tc)
        m_prev = m_scr[...]
        m_new = jnp.maximum(m_prev, jnp.max(sc, axis=-1, keepdims=True))
        alpha = jnp.exp(m_prev - m_new)
        pe = jnp.exp(sc - m_new)
        l_scr[...] = alpha * l_scr[...] + jnp.sum(pe, axis=-1, keepdims=True)
        acc_scr[...] = alpha * acc_scr[...] + _nn(pe.astype(BF16), v)
        m_scr[...] = m_new
        return carry

    lax.fori_loop(0, n_chunks, chunk, 0)
    o_s = acc_scr[...] / l_scr[...]

    w0 = pl.multiple_of(jnp.clip(i * tq + tq - tw, 0, t_len - tw), LANE)
    kw = kwin_ref[0, pl.ds(w0, tw), :]
    vw = vwin_ref[0, pl.ds(w0, tw), :]
    sw = _nt(qr, kw)
    kp = w0 + lax.broadcasted_iota(jnp.int32, (tq, tw), 1)
    tp = i * tq + lax.broadcasted_iota(jnp.int32, (tq, tw), 0)
    wbias = jnp.where((kp <= tp) & (kp > tp - WINDOW), 0.0, MASK_NEG)
    sw = (sw.reshape(hq, tq, tw) + wbias[None]).reshape(hq * tq, tw)
    ew = jnp.exp(sw - jnp.max(sw, axis=-1, keepdims=True))
    o_w = _nn(ew.astype(BF16), vw) / jnp.sum(ew, axis=-1, keepdims=True)

    gate = jax.nn.sigmoid(bg_ref[0])
    ng = ng_ref[0]
    for h in range(hq):
        r = slice(h * tq, (h + 1) * tq)
        o = (gate[:, 3 * h:3 * h + 1] * o_c[r] + gate[:, 3 * h + 1:3 * h + 2] * o_s[r]
             + gate[:, 3 * h + 2:3 * h + 3] * o_w[r])
        o_ref[0, :, h * LANE:(h + 1) * LANE] = (_silu(ng[:, h * LANE:(h + 1) * LANE]) * o).astype(o_ref.dtype)


def _nsa_prompt(z3, bg3, kvb, kc, vc, tabs, *, tq=128, tc=512):
    b, t, _ = z3.shape
    nq = t // tq
    tc = min(tc, t)
    tw = min(WINDOW + tq, t)
    n_sel = t // SEL_BLOCK
    top = min(TOP_N, n_sel)
    npad = kc.shape[2]
    gw = NSA_HPG * LANE
    kvspec = lambda k: pl.BlockSpec((1, t, LANE), lambda bi, gi, qi, k=k: (bi, 0, k + gi))
    cspec = pl.BlockSpec((1, 1, npad, LANE), lambda bi, gi, qi: (bi, gi, 0, 0))
    tspec = pl.BlockSpec((tq, LANE), lambda bi, gi, qi: (qi, 0))
    kern = functools.partial(_nsa_prompt_kernel, t_len=t, tq=tq, tc=tc, tw=tw, n_sel=n_sel, top=top)
    return pl.pallas_call(
        kern,
        out_shape=jax.ShapeDtypeStruct((b, t, NSA_HEADS * LANE), BF16),
        grid=(b, NSA_KV, nq),
        in_specs=[pl.BlockSpec((1, tq, gw), lambda bi, gi, qi: (bi, qi, Q_OFF // gw + gi)),
                  pl.BlockSpec((1, tq, gw), lambda bi, gi, qi: (bi, qi, NG_OFF // gw + gi)),
                  pl.BlockSpec((1, tq, LANE), lambda bi, gi, qi: (bi, qi, gi)),
                  kvspec(0), kvspec(2), kvspec(4), kvspec(6), cspec, cspec, tspec, tspec, tspec],
        out_specs=pl.BlockSpec((1, tq, gw), lambda bi, gi, qi: (bi, qi, gi)),
        scratch_shapes=[pltpu.VMEM((NSA_HPG * tq, 1), F32), pltpu.VMEM((NSA_HPG * tq, 1), F32),
                        pltpu.VMEM((NSA_HPG * tq, LANE), F32)],
        compiler_params=_params(("parallel", "parallel", "arbitrary")),
        name="nsa_prompt",
    )(z3, z3, bg3, kvb, kvb, kvb, kvb, kc, vc, *tabs)


def _conv_kernel(h_ref, b_ref, c_ref, g_ref, prev_ref, w_ref, y_ref, st_ref, up_scr, *, t_real):
    u = c_ref[0] * h_ref[0]
    t = u.shape[0]
    up_scr[pl.ds(8 - (CONV_W - 1), CONV_W - 1), :] = prev_ref[0]
    up_scr[pl.ds(8, t), :] = u
    w = w_ref[...]
    y = w[0:1, :] * up_scr[pl.ds(6, t), :]
    y = y + w[1:2, :] * up_scr[pl.ds(7, t), :]
    y = y + w[2:3, :] * u
    y = b_ref[0] * y
    y_ref[0] = (_silu(g_ref[0]) * y).astype(y_ref.dtype)
    st_ref[0] = up_scr[pl.ds(6 + t_real, CONV_W - 1), :]


def _conv_mixer(z3, prev, w_conv, *, t_real):
    b, t, _ = z3.shape
    nc = CONV_DIM // LANE
    zspec = lambda off: pl.BlockSpec((1, t, LANE), lambda bi, ci, off=off: (bi, 0, off // LANE + ci))
    return pl.pallas_call(
        functools.partial(_conv_kernel, t_real=t_real),
        out_shape=[jax.ShapeDtypeStruct((b, t, CONV_DIM), BF16),
                   jax.ShapeDtypeStruct((b, CONV_W - 1, CONV_DIM), F32)],
        grid=(b, nc),
        in_specs=[zspec(C_H), zspec(C_B), zspec(C_C), zspec(C_G),
                  pl.BlockSpec((1, CONV_W - 1, LANE), lambda bi, ci: (bi, 0, ci)),
                  pl.BlockSpec((CONV_W, LANE), lambda bi, ci: (0, ci))],
        out_specs=[pl.BlockSpec((1, t, LANE), lambda bi, ci: (bi, 0, ci)),
                   pl.BlockSpec((1, CONV_W - 1, LANE), lambda bi, ci: (bi, 0, ci))],
        scratch_shapes=[pltpu.VMEM((t + 8, LANE), F32)],
        compiler_params=_params(("parallel", "parallel")),
        name="conv_mixer",
    )(z3, z3, z3, z3, prev, w_conv)


def _mem_attn_kernel(q_ref, mg_ref, kv_ref, o_ref):
    q = q_ref[0]
    mg = mg_ref[0]
    half = MEM_HEADS * LANE
    for h in range(MEM_HEADS):
        lo, hi = h * LANE, (h + 1) * LANE
        k = kv_ref[0, :, lo:hi].astype(BF16)
        v = kv_ref[0, :, half + lo:half + hi].astype(BF16)
        s = _nt((q[:, lo:hi] * ATTN_SCALE).astype(BF16), k)
        e = jnp.exp(s - jnp.max(s, axis=-1, keepdims=True))
        o = _nn(e.astype(BF16), v) / jnp.sum(e, axis=-1, keepdims=True)
        o_ref[0, :, lo:hi] = (_silu(mg[:, lo:hi]) * o).astype(o_ref.dtype)


def _mem_attn(z3, mkv, *, tq):
    b, t, _ = z3.shape
    nm, wkv = mkv.shape[1], mkv.shape[2]
    wq = MEM_HEADS * LANE
    return pl.pallas_call(
        _mem_attn_kernel,
        out_shape=jax.ShapeDtypeStruct((b, t, wq), BF16),
        grid=(b, t // tq),
        in_specs=[pl.BlockSpec((1, tq, wq), lambda bi, ti: (bi, ti, MQ_OFF // wq)),
                  pl.BlockSpec((1, tq, wq), lambda bi, ti: (bi, ti, MG_OFF // wq)),
                  pl.BlockSpec((1, nm, wkv), lambda bi, ti: (bi, 0, 0))],
        out_specs=pl.BlockSpec((1, tq, wq), lambda bi, ti: (bi, ti, 0)),
        compiler_params=_params(("parallel", "parallel")),
        name="mem_attn",
    )(z3, z3, mkv)


def _out_proj_kernel(x_ref, ya_ref, yb_ref, ym_ref, w_ref, fg_ref, o_ref, *, final):
    a, bw = CONV_DIM, CONV_DIM + NSA_HEADS * LANE
    acc = _nn(ya_ref[...], w_ref[0:a, :])
    acc = acc + _nn(yb_ref[...], w_ref[a:bw, :])
    acc = acc + _nn(ym_ref[...], w_ref[bw:, :])
    r = x_ref[...] + acc
    if final:
        r = r * lax.rsqrt(jnp.mean(r * r, axis=-1, keepdims=True) + NORM_EPS) * fg_ref[...]
    o_ref[...] = r


def _out_proj(x, ya, yb, ym, w, fg, *, tm, final):
    m, d = x.shape
    row = lambda width: pl.BlockSpec((tm, width), lambda i: (i, 0))
    return pl.pallas_call(
        functools.partial(_out_proj_kernel, final=final),
        out_shape=jax.ShapeDtypeStruct((m, d), F32),
        grid=(m // tm,),
        in_specs=[row(d), row(ya.shape[1]), row(yb.shape[1]), row(ym.shape[1]),
                  pl.BlockSpec(w.shape, lambda i: (0, 0)),
                  pl.BlockSpec((1, d), lambda i: (0, 0))],
        out_specs=row(d),
        compiler_params=_params(("parallel",)),
        name="out_proj",
    )(x, ya, yb, ym, w, fg.reshape(1, d))


def _sample_pre_kernel(q_ref, ng_ref, bg_ref, kv0_ref, kv1_ref, kv2_ref, c_ref, s1_ref, s2_ref, kc_ref, vc_ref,
                       kvn_ref, wn_ref, qr_ref, oc_ref, ngo_ref, gate_ref, idx_ref, *, past, n_top):
    tp = SAMPLE_T_PAD
    hq = NSA_HPG
    c, s1, s2 = c_ref[...], s1_ref[...], s2_ref[...]
    kv0, kv1, kv2 = kv0_ref[0], kv1_ref[0], kv2_ref[0]
    kvn_ref[0, :, 0:512] = kv0
    kvn_ref[0, :, 768:1024] = kv1[:, 256:512]
    wn_ref[0, :, 256:512] = kv2[:, 256:512]
    for g in range(NSA_KV):
        lo, hi = g * LANE, (g + 1) * LANE
        kvn_ref[0, :, 512 + lo:512 + hi] = _rope(kv1[:, lo:hi], c, s1, s2)
        wn_ref[0, :, lo:hi] = _rope(kv2[:, lo:hi], c, s1, s2)

    q = q_ref[0]
    ng = ng_ref[0]
    gates = jax.nn.sigmoid(bg_ref[0])
    for g in range(NSA_KV):
        qc_l, qr_l = [], []
        for h in range(hq):
            lo = (g * hq + h) * LANE
            qh = q[:, lo:lo + LANE]
            qc_l.append(qh * ATTN_SCALE)
            qr_l.append(_rope(qh, c, s1, s2) * ATTN_SCALE)
            ngo_ref[0, g, h * tp:(h + 1) * tp, :] = ng[:, lo:lo + LANE]
            for br in range(N_BRANCH):
                col = g * LANE + h * N_BRANCH + br
                gate_ref[0, g, br, h * tp:(h + 1) * tp, :] = jnp.broadcast_to(gates[:, col:col + 1], (tp, LANE))
        qc = jnp.concatenate(qc_l, axis=0)
        qr_ref[0, g] = jnp.concatenate(qr_l, axis=0)

        kc = kc_ref[0, g]
        npad = kc.shape[0]
        s = _nt(qc.astype(BF16), kc)
        trow = lax.broadcasted_iota(jnp.int32, (hq * tp, npad), 0) % tp
        ncol = lax.broadcasted_iota(jnp.int32, (hq * tp, npad), 1)
        cmask = (ncol + 1) * CMP_BLOCK <= past + trow + 1
        s = jnp.where(cmask, s, MASK_NEG)
        e = jnp.where(cmask, jnp.exp(s - jnp.max(s, axis=-1, keepdims=True)), 0.0)
        p = e / jnp.maximum(jnp.sum(e, axis=-1, keepdims=True), 1e-30)
        oc_ref[0, g] = _nn(p.astype(BF16), vc_ref[0, g])
        imp = jnp.sum(p.reshape(hq, tp, npad), axis=0)

        blk = lax.broadcasted_iota(jnp.int32, (tp, npad), 1)
        cur = (past + lax.broadcasted_iota(jnp.int32, (tp, npad), 0)) // SEL_BLOCK
        val = jnp.where((blk == 0) | (blk == cur) | (blk == cur - 1), FORCE, imp)
        val = jnp.where(blk > cur, -1.0, val)
        val = jnp.where(blk >= past // SEL_BLOCK, -2.0, val)
        lane = lax.broadcasted_iota(jnp.int32, (tp, LANE), 1)
        idx = jnp.zeros((tp, LANE), jnp.int32)
        for r in range(n_top):
            best = jnp.max(val, axis=-1, keepdims=True)
            j = jnp.min(jnp.where(val == best, blk, npad), axis=-1, keepdims=True)
            idx = jnp.where(lane == r, j, idx)
            val = jnp.where(blk == j, -3e38, val)
        idx_ref[0, g] = idx


def _sample_pre(z3, bg3, tabs, kc, vc, *, past, n_top):
    bs, tp, _ = z3.shape
    npad = kc.shape[2]
    qw = NSA_HEADS * LANE
    kvblk = KV_OFF // 512
    zspec = lambda k: pl.BlockSpec((1, tp, 512), lambda bi, k=k: (bi, 0, kvblk + k))
    tspec = pl.BlockSpec((tp, LANE), lambda bi: (0, 0))
    cspec = pl.BlockSpec((1, NSA_KV, npad, LANE), lambda bi: (bi, 0, 0, 0))
    rows = NSA_HPG * tp
    gspec = pl.BlockSpec((1, NSA_KV, rows, LANE), lambda bi: (bi, 0, 0, 0))
    gshape = jax.ShapeDtypeStruct((bs, NSA_KV, rows, LANE), F32)
    return pl.pallas_call(
        functools.partial(_sample_pre_kernel, past=past, n_top=n_top),
        out_shape=[jax.ShapeDtypeStruct((bs, tp, 1024), F32),
                   jax.ShapeDtypeStruct((bs, tp, 512), F32),
                   gshape, gshape, gshape,
                   jax.ShapeDtypeStruct((bs, NSA_KV, N_BRANCH, rows, LANE), F32),
                   jax.ShapeDtypeStruct((bs, NSA_KV, tp, LANE), jnp.int32)],
        grid=(bs,),
        in_specs=[pl.BlockSpec((1, tp, qw), lambda bi: (bi, 0, Q_OFF // qw)),
                  pl.BlockSpec((1, tp, qw), lambda bi: (bi, 0, NG_OFF // qw)),
                  pl.BlockSpec((1, tp, NSA_KV * LANE), lambda bi: (bi, 0, 0)),
                  zspec(0), zspec(1), zspec(2), tspec, tspec, tspec, cspec, cspec],
        out_specs=[pl.BlockSpec((1, tp, 1024), lambda bi: (bi, 0, 0)),
                   pl.BlockSpec((1, tp, 512), lambda bi: (bi, 0, 0)),
                   gspec, gspec, gspec,
                   pl.BlockSpec((1, NSA_KV, N_BRANCH, rows, LANE), lambda bi: (bi, 0, 0, 0, 0)),
                   pl.BlockSpec((1, NSA_KV, tp, LANE), lambda bi: (bi, 0, 0, 0))],
        compiler_params=_params(("parallel",)),
        name="sample_pre",
    )(z3, z3, bg3, z3, z3, z3, *tabs, kc, vc)


def _sample_attn_kernel(idx_sm, pt_sm, qr_ref, oc_ref, ng_ref, gate_ref, ksn_ref, vsn_ref,
                        kwc_ref, vwc_ref, kwn_ref, vwn_ref, cache_ref, o_ref,
                        kbuf, vbuf, kw_scr, vw_scr, sem, *, ts, n_top, n_pages, page_base, per_page, wb):
    tp = SAMPLE_T_PAD
    hq = NSA_HPG
    b = pl.program_id(0)
    g = pl.program_id(1)
    n_gath = n_top * SEL_BLOCK
    ks_rows = kbuf.shape[1]

    def gather_copies(t, r):
        blk = idx_sm[((b * NSA_KV + g) * ts + t) * n_top + r]
        page = pt_sm[b * n_pages + blk // per_page] + page_base
        row0 = (blk % per_page) * SEL_BLOCK
        src_k = cache_ref.at[page, pl.ds(row0, SEL_BLOCK), 2 * NSA_KV + g]
        src_v = cache_ref.at[page, pl.ds(row0, SEL_BLOCK), 3 * NSA_KV + g]
        dst = pl.ds(r * SEL_BLOCK, SEL_BLOCK)
        return (pltpu.make_async_copy(src_k, kbuf.at[t, dst], sem.at[0]),
                pltpu.make_async_copy(src_v, vbuf.at[t, dst], sem.at[1]))

    for t in range(ts):
        for r in range(n_top):
            ck, cv = gather_copies(t, r)
            ck.start()
            cv.start()

    qr = qr_ref[0, 0].astype(BF16)
    trow = lax.broadcasted_iota(jnp.int32, (hq * tp, 1), 0) % tp

    ww = kw_scr.shape[0]
    kw_scr[pl.ds(0, wb), :] = kwc_ref[0]
    vw_scr[pl.ds(0, wb), :] = vwc_ref[0]
    kw_scr[pl.ds(wb, tp), :] = kwn_ref[0]
    vw_scr[pl.ds(wb, tp), :] = vwn_ref[0]
    kw_scr[pl.ds(wb + tp, ww - wb - tp), :] = jnp.zeros((ww - wb - tp, LANE), F32)
    vw_scr[pl.ds(wb + tp, ww - wb - tp), :] = jnp.zeros((ww - wb - tp, LANE), F32)
    sw = _nt(qr, kw_scr[...].astype(BF16))
    jw = lax.broadcasted_iota(jnp.int32, (hq * tp, ww), 1)
    rel = jw - wb
    okw = (rel <= trow) & (rel > trow - WINDOW) & (jw < wb + ts)
    sw = jnp.where(okw, sw, MASK_NEG)
    ew = jnp.where(okw, jnp.exp(sw - jnp.max(sw, axis=-1, keepdims=True)), 0.0)
    o_w = _nn(ew.astype(BF16), vw_scr[...].astype(BF16)) / jnp.sum(ew, axis=-1, keepdims=True)

    for t in range(ts):
        for r in range(n_top):
            ck, cv = gather_copies(t, r)
            ck.wait()
            cv.wait()

    js = lax.broadcasted_iota(jnp.int32, (hq * tp, ks_rows), 1)
    o_s = jnp.zeros((hq * tp, LANE), F32)
    for t in range(ts):
        kbuf[t, pl.ds(n_gath, tp), :] = ksn_ref[0]
        vbuf[t, pl.ds(n_gath, tp), :] = vsn_ref[0]
        kbuf[t, pl.ds(n_gath + tp, ks_rows - n_gath - tp), :] = jnp.zeros((ks_rows - n_gath - tp, LANE), F32)
        vbuf[t, pl.ds(n_gath + tp, ks_rows - n_gath - tp), :] = jnp.zeros((ks_rows - n_gath - tp, LANE), F32)
        ss = _nt(qr, kbuf[t].astype(BF16))
        oks = (js < n_gath) | ((js - n_gath <= t) & (js < n_gath + ts))
        ss = jnp.where(oks, ss, MASK_NEG)
        es = jnp.where(oks, jnp.exp(ss - jnp.max(ss, axis=-1, keepdims=True)), 0.0)
        ot = _nn(es.astype(BF16), vbuf[t].astype(BF16)) / jnp.sum(es, axis=-1, keepdims=True)
        o_s = jnp.where(trow == t, ot, o_s)

    o = gate_ref[0, 0, 0] * oc_ref[0, 0] + gate_ref[0, 0, 1] * o_s + gate_ref[0, 0, 2] * o_w
    y = _silu(ng_ref[0, 0]) * o
    for h in range(hq):
        o_ref[0, :, h * LANE:(h + 1) * LANE] = y[h * tp:(h + 1) * tp].astype(o_ref.dtype)


def _sample_attn(idx_flat, page_flat, qr, oc, ngo, gates, kvn, cache_win3, wn, cache4, *,
                 ts, n_top, n_pages, page_base, win_base):
    bs = qr.shape[0]
    tp = SAMPLE_T_PAD
    rows = NSA_HPG * tp
    wb = cache_win3.shape[1]
    per_page = cache4.shape[1] // SEL_BLOCK
    ks_rows = -(-(n_top * SEL_BLOCK + tp) // LANE) * LANE
    ww = -(-(wb + tp) // LANE) * LANE
    gspec = pl.BlockSpec((1, 1, rows, LANE), lambda bi, gi, *_: (bi, gi, 0, 0))
    newspec = lambda k: pl.BlockSpec((1, tp, LANE), lambda bi, gi, *_, k=k: (bi, 0, k + gi))
    wcspec = lambda k: pl.BlockSpec((1, wb, LANE), lambda bi, gi, *_, k=k: (bi + win_base, 0, k + gi))
    grid_spec = pltpu.PrefetchScalarGridSpec(
        num_scalar_prefetch=2, grid=(bs, NSA_KV),
        in_specs=[gspec, gspec, gspec,
                  pl.BlockSpec((1, 1, N_BRANCH, rows, LANE), lambda bi, gi, *_: (bi, gi, 0, 0, 0)),
                  newspec(2 * NSA_KV), newspec(3 * NSA_KV),
                  wcspec(0), wcspec(NSA_KV), newspec(0), newspec(NSA_KV),
                  pl.BlockSpec(memory_space=pl.ANY)],
        out_specs=pl.BlockSpec((1, tp, NSA_HPG * LANE), lambda bi, gi, *_: (bi, 0, gi)),
        scratch_shapes=[pltpu.VMEM((ts, ks_rows, LANE), F32), pltpu.VMEM((ts, ks_rows, LANE), F32),
                        pltpu.VMEM((ww, LANE), F32), pltpu.VMEM((ww, LANE), F32),
                        pltpu.SemaphoreType.DMA((2,))])
    kern = functools.partial(_sample_attn_kernel, ts=ts, n_top=n_top, n_pages=n_pages,
                             page_base=page_base, per_page=per_page, wb=wb)
    return pl.pallas_call(
        kern,
        out_shape=jax.ShapeDtypeStruct((bs, tp, NSA_HEADS * LANE), BF16),
        grid_spec=grid_spec,
        compiler_params=_params(("arbitrary", "arbitrary")),
        name="sample_attn",
    )(idx_flat, page_flat, qr, oc, ngo, gates, kvn, kvn, cache_win3, cache_win3, wn, wn, cache4)


def _rope_tables(pos, rows):
    freqs = jnp.power(ROPE_THETA, -jnp.arange(ROPE_HALF, dtype=F32) * (2.0 / ROPE_DIM))
    ang = pos.astype(F32)[:, None] * freqs[None, :]
    cos, sin = jnp.cos(ang), jnp.sin(ang)
    n = pos.shape[0]
    z16 = jnp.zeros((n, ROPE_HALF), F32)
    rest = LANE - ROPE_DIM
    c = jnp.concatenate([cos, cos, jnp.ones((n, rest), F32)], axis=1)
    s1 = jnp.concatenate([z16, sin, jnp.zeros((n, rest), F32)], axis=1)
    s2 = jnp.concatenate([-sin, z16, jnp.zeros((n, rest), F32)], axis=1)
    pad = lambda a: jnp.pad(a, ((0, rows - n), (0, 0)))
    return pad(c), pad(s1), pad(s2)


def _layer_weights(norm_g, w_in, w_conv, a_cmp, w_cmp, w_out):
    w_main = jnp.concatenate([w_in[:, :BG_SRC], w_in[:, BG_SRC + BG_N:]], axis=1).astype(BF16)
    per_group = NSA_HPG * N_BRANCH
    wbg = jnp.concatenate(
        [jnp.pad(w_in[:, BG_SRC + g * per_group:BG_SRC + (g + 1) * per_group], ((0, 0), (0, LANE - per_group)))
         for g in range(NSA_KV)], axis=1).astype(BF16)
    a4 = jnp.concatenate([a_cmp[0], a_cmp[0], a_cmp[1], a_cmp[1]], axis=1)
    w4 = jnp.stack([w_cmp[0], w_cmp[0], w_cmp[1], w_cmp[1]]).astype(BF16)
    return dict(norm_g=norm_g, w_main=w_main, wbg=wbg, w_conv=w_conv, a4=a4, w4=w4, w_out=w_out.astype(BF16))


def _prompt_layer(xp, mem_prompt, mem_norm_g, w_mem, lw, final_g, final):
    b, t, d = xp.shape
    m = b * t
    z, bg = _norm_matmul(xp.reshape(m, d), lw["norm_g"], lw["w_main"], lw["wbg"], tm=min(512, m), tn=512)
    z3 = z.reshape(b, t, Z_WIDTH)
    bg3 = bg.reshape(b, t, NSA_KV * LANE)
    nm = mem_prompt.shape[1]
    mkv = _norm_matmul(mem_prompt.reshape(b * nm, d), mem_norm_g, w_mem, tm=min(512, b * nm), tn=512)
    mkv3 = mkv.reshape(b, nm, 2 * MEM_HEADS * LANE)
    tabs = _rope_tables(jnp.arange(t, dtype=jnp.int32), t)
    wb = min(WINDOW, t)
    kvn, kvb, win_full = _kv_export(z3, tabs, tr=min(512, t))
    kc, vc = _cmp_proj(_pool_prompt(z3, lw["a4"]), lw["w4"])
    yb = _nsa_prompt(z3, bg3, kvb, kc, vc, tabs, tq=min(128, t))
    conv0 = jnp.zeros((b, CONV_W - 1, CONV_DIM), F32)
    ya, conv_new = _conv_mixer(z3, conv0, lw["w_conv"], t_real=t)
    ym = _mem_attn(z3, mkv3, tq=min(512, t))
    out = _out_proj(xp.reshape(m, d), ya.reshape(m, -1), yb.reshape(m, -1), ym.reshape(m, -1),
                    lw["w_out"], final_g, tm=min(256, m), final=final)
    kv_new = kvn.reshape(b, t, 4, NSA_KV, HEAD_DIM)
    win_new = win_full[:, t - wb:].reshape(b, wb, 2, NSA_KV, HEAD_DIM)
    mem_kv = mkv.reshape(b, nm, 2, MEM_HEADS, HEAD_DIM)
    return out.reshape(b, t, d), kv_new, win_new, conv_new, mem_kv


def _sample_layer(xs_p, ts, layer, cache_kv, cache_win, state_conv, cache_mem, page_table, lw, final_g, final):
    bs, tp, d = xs_p.shape
    depth, pool, page = cache_kv.shape[0], cache_kv.shape[1], cache_kv.shape[2]
    n_pages = page_table.shape[1]
    past = n_pages * page
    assert past % SEL_BLOCK == 0 and ts <= SEL_BLOCK and ts <= tp
    n_past = past // SEL_BLOCK
    n_top = min(TOP_N, n_past + 1) - 1
    m = bs * tp
    z, bg = _norm_matmul(xs_p.reshape(m, d), lw["norm_g"], lw["w_main"], lw["wbg"], tm=m, tn=512)
    z3 = z.reshape(bs, tp, Z_WIDTH)
    bg3 = bg.reshape(bs, tp, NSA_KV * LANE)
    tabs = _rope_tables(past + jnp.arange(tp, dtype=jnp.int32), tp)
    page_flat = page_table.reshape(-1).astype(jnp.int32)
    cache3 = cache_kv.reshape(depth * pool, page, 4 * NSA_KV * HEAD_DIM)
    cache4 = cache_kv.reshape(depth * pool, page, 4 * NSA_KV, HEAD_DIM)
    pooled = _pool_pages(cache3, page_flat, lw["a4"], bs=bs, n_pages=n_pages, page_base=layer * pool)
    kc, vc = _cmp_proj(pooled, lw["w4"])
    kvn, wn, qr, oc, ngo, gates, idx = _sample_pre(z3, bg3, tabs, kc, vc, past=past, n_top=n_top)
    idx_flat = idx[:, :, :ts, :n_top].reshape(-1)
    wbuf = cache_win.shape[2]
    cache_win3 = cache_win.reshape(depth * bs, wbuf, 2 * NSA_KV * HEAD_DIM)
    yb = _sample_attn(idx_flat, page_flat, qr, oc, ngo, gates, kvn, cache_win3, wn, cache4,
                      ts=ts, n_top=n_top, n_pages=n_pages, page_base=layer * pool, win_base=layer * bs)
    ya, conv_new = _conv_mixer(z3, state_conv[layer], lw["w_conv"], t_real=ts)
    nm = cache_mem.shape[2]
    mem3 = cache_mem.reshape(depth * bs, nm, 2 * MEM_HEADS * HEAD_DIM)[layer * bs:(layer + 1) * bs]
    ym = _mem_attn(z3, mem3, tq=tp)
    out = _out_proj(xs_p.reshape(m, d), ya.reshape(m, -1), yb.reshape(m, -1), ym.reshape(m, -1),
                    lw["w_out"], final_g, tm=m, final=final)
    kv_new = kvn[:, :ts].reshape(bs, ts, 4, NSA_KV, HEAD_DIM)
    win_rows = wn[:, :ts].reshape(bs, ts, 2, NSA_KV, HEAD_DIM)
    win_state = jnp.concatenate([cache_win[layer], win_rows], axis=1)[:, ts:]
    return out.reshape(bs, tp, d), kv_new, win_state, conv_new


def kernel(x_prompt, x_sample, cache_kv, cache_win, state_conv, cache_mem, page_table, mem_prompt,
           norm_g, w_in, w_conv, a_cmp, w_cmp, mem_norm_g, w_mem_kv, w_out, final_g):
    depth = w_in.shape[0]
    ts = x_sample.shape[1]
    xp = x_prompt
    xs = jnp.pad(x_sample, ((0, 0), (0, SAMPLE_T_PAD - ts), (0, 0)))
    kv_p, win_p, conv_p, mem_p, kv_s, win_s, conv_s = [], [], [], [], [], [], []
    for l in range(depth):
        lw = _layer_weights(norm_g[l], w_in[l], w_conv[l], a_cmp[l], w_cmp[l], w_out[l])
        final = l == depth - 1
        xp, kvn, winn, convn, mkv = _prompt_layer(xp, mem_prompt, mem_norm_g[l], w_mem_kv[l].astype(BF16),
                                                  lw, final_g, final)
        kv_p.append(kvn)
        win_p.append(winn)
        conv_p.append(convn)
        mem_p.append(mkv)
        xs, kvn, winn, convn = _sample_layer(xs, ts, l, cache_kv, cache_win, state_conv, cache_mem,
                                             page_table, lw, final_g, final)
        kv_s.append(kvn)
        win_s.append(winn)
        conv_s.append(convn)
    return (xp, xs[:, :ts], jnp.stack(kv_p), jnp.stack(win_p), jnp.stack(conv_p), jnp.stack(mem_p),
            jnp.stack(kv_s), jnp.stack(win_s), jnp.stack(conv_s))
```

```python
import functools

import jax
import jax.numpy as jnp
from jax import lax
from jax.experimental import pallas as pl
from jax.experimental.pallas import tpu as pltpu

F32 = jnp.float32
BF16 = jnp.bfloat16

HEAD_DIM = 128
CONV_DIM = 512
CONV_W = 3
NSA_HEADS = 8
NSA_KV = 2
NSA_HPG = NSA_HEADS // NSA_KV
MEM_HEADS = 4
N_BRANCH = 3
ROPE_DIM = HEAD_DIM // 4
ROPE_HALF = ROPE_DIM // 2
ROPE_THETA = 500000.0
CMP_BLOCK = 64
SEL_BLOCK = 64
TOP_N = 16
WINDOW = 512
NORM_EPS = 1e-6
MASK_NEG = -1e30
FORCE = 1e9
ATTN_SCALE = HEAD_DIM ** -0.5

C_H, C_B, C_C, C_G = 0, 512, 1024, 1536
Q_OFF, NG_OFF, KV_OFF, MQ_OFF, MG_OFF = 2048, 3072, 4096, 5632, 6144
Z_WIDTH = 6656
BG_SRC = 4096
BG_N = NSA_HEADS * N_BRANCH
LANE = 128
SAMPLE_T_PAD = 8
VMEM_LIMIT = 48 * 1024 * 1024


def _nt(a, b):
    return lax.dot_general(a, b, (((1,), (1,)), ((), ())), preferred_element_type=F32)


def _nn(a, b):
    return jnp.dot(a, b, preferred_element_type=F32)


def _params(sem, vmem=VMEM_LIMIT):
    return pltpu.CompilerParams(dimension_semantics=sem, vmem_limit_bytes=vmem)


def _rope(x, c, s1, s2):
    return x * c + pltpu.roll(x, ROPE_HALF, 1) * s1 + pltpu.roll(x, LANE - ROPE_HALF, 1) * s2


def _silu(x):
    return x * jax.nn.sigmoid(x)


def _norm_matmul_kernel(x_ref, g_ref, w_ref, *rest, with_gate):
    if with_gate:
        wbg_ref, z_ref, bg_ref, h_scr = rest
    else:
        z_ref, h_scr = rest

    @pl.when(pl.program_id(1) == 0)
    def _():
        x = x_ref[...]
        y = x * lax.rsqrt(jnp.mean(x * x, axis=-1, keepdims=True) + NORM_EPS) * g_ref[...]
        h = y.astype(BF16)
        h_scr[...] = h
        if with_gate:
            bg_ref[...] = _nn(h, wbg_ref[...])

    z_ref[...] = _nn(h_scr[...], w_ref[...])


def _norm_matmul(x, g, w, wbg=None, *, tm, tn):
    m, d = x.shape
    n = w.shape[1]
    with_gate = wbg is not None
    in_specs = [pl.BlockSpec((tm, d), lambda i, j: (i, 0)),
                pl.BlockSpec((1, d), lambda i, j: (0, 0)),
                pl.BlockSpec((d, tn), lambda i, j: (0, j))]
    out_shape = [jax.ShapeDtypeStruct((m, n), F32)]
    out_specs = [pl.BlockSpec((tm, tn), lambda i, j: (i, j))]
    args = [x, g.reshape(1, d), w]
    if with_gate:
        nb = wbg.shape[1]
        in_specs.append(pl.BlockSpec((d, nb), lambda i, j: (0, 0)))
        out_shape.append(jax.ShapeDtypeStruct((m, nb), F32))
        out_specs.append(pl.BlockSpec((tm, nb), lambda i, j: (i, 0)))
        args.append(wbg)
    res = pl.pallas_call(
        functools.partial(_norm_matmul_kernel, with_gate=with_gate),
        out_shape=out_shape, grid=(m // tm, n // tn), in_specs=in_specs, out_specs=out_specs,
        scratch_shapes=[pltpu.VMEM((tm, d), BF16)],
        compiler_params=_params(("parallel", "arbitrary")),
        name="norm_matmul_gate" if with_gate else "norm_matmul",
    )(*args)
    return res if with_gate else res[0]


def _kv_export_kernel(kv1_ref, kv2_ref, kv0_ref, c_ref, s1_ref, s2_ref, kvn_ref, kvb_ref, win_ref):
    c, s1, s2 = c_ref[...], s1_ref[...], s2_ref[...]
    kv0 = kv0_ref[0]
    kv1 = kv1_ref[0]
    kv2 = kv2_ref[0]
    kvn_ref[0, :, 0:512] = kv0
    for g in range(NSA_KV):
        lo, hi = g * LANE, (g + 1) * LANE
        ks = _rope(kv1[:, lo:hi], c, s1, s2)
        kw = _rope(kv2[:, lo:hi], c, s1, s2)
        kvn_ref[0, :, 512 + lo:512 + hi] = ks
        kvb_ref[0, :, lo:hi] = ks.astype(BF16)
        kvb_ref[0, :, 512 + lo:512 + hi] = kw.astype(BF16)
        win_ref[0, :, lo:hi] = kw
    kvn_ref[0, :, 768:1024] = kv1[:, 256:512]
    kvb_ref[0, :, 256:512] = kv1[:, 256:512].astype(BF16)
    kvb_ref[0, :, 768:1024] = kv2[:, 256:512].astype(BF16)
    win_ref[0, :, 256:512] = kv2[:, 256:512]


def _kv_export(z3, tabs, *, tr):
    b, t, _ = z3.shape
    nt = t // tr
    kvblk = KV_OFF // 512
    zspec = lambda k: pl.BlockSpec((1, tr, 512), lambda bi, ti, k=k: (bi, ti, kvblk + k))
    tspec = pl.BlockSpec((tr, LANE), lambda bi, ti: (ti, 0))
    return pl.pallas_call(
        _kv_export_kernel,
        out_shape=[jax.ShapeDtypeStruct((b, t, 1024), F32),
                   jax.ShapeDtypeStruct((b, t, 1024), BF16),
                   jax.ShapeDtypeStruct((b, t, 512), F32)],
        grid=(b, nt),
        in_specs=[zspec(1), zspec(2), zspec(0), tspec, tspec, tspec],
        out_specs=[pl.BlockSpec((1, tr, 1024), lambda bi, ti: (bi, ti, 0)),
                   pl.BlockSpec((1, tr, 1024), lambda bi, ti: (bi, ti, 0)),
                   pl.BlockSpec((1, tr, 512), lambda bi, ti: (bi, ti, 0))],
        compiler_params=_params(("parallel", "parallel")),
        name="kv_export",
    )(z3, z3, z3, *tabs)


def _pool_kernel(*refs, n_prefetch):
    x_ref, a_ref, o_ref = refs[n_prefetch:]
    x = x_ref[0]
    rows = x.shape[0]
    x3 = x.reshape(rows // CMP_BLOCK, CMP_BLOCK, x.shape[1])
    pooled = jnp.sum(x3 * a_ref[...][None], axis=1)
    o_ref[...] = pooled.reshape(o_ref.shape)


def _pool_prompt(z3, a4):
    b, t, _ = z3.shape
    n = t // CMP_BLOCK
    return pl.pallas_call(
        functools.partial(_pool_kernel, n_prefetch=0),
        out_shape=jax.ShapeDtypeStruct((b, n, 512), F32),
        grid=(b,),
        in_specs=[pl.BlockSpec((1, t, 512), lambda bi: (bi, 0, KV_OFF // 512)),
                  pl.BlockSpec((CMP_BLOCK, 512), lambda bi: (0, 0))],
        out_specs=pl.BlockSpec((1, n, 512), lambda bi: (bi, 0, 0)),
        compiler_params=_params(("parallel",)),
        name="pool_prompt",
    )(z3, a4)


def _pool_pages_kernel(pt_sm, a_ref, cache_ref, o_ref, buf, sem, *, pages_per_step, page_base):
    step = pl.program_id(0)
    n_steps = pl.num_programs(0)
    slot = step % 2
    n_cols = buf.shape[3]

    def page_copy(step_idx, p, to_slot):
        page = pt_sm[step_idx * pages_per_step + p] + page_base
        return pltpu.make_async_copy(cache_ref.at[page, :, pl.ds(0, n_cols), :], buf.at[to_slot, p], sem.at[to_slot])

    @pl.when(step == 0)
    def _():
        for p in range(pages_per_step):
            page_copy(0, p, 0).start()

    @pl.when(step + 1 < n_steps)
    def _():
        for p in range(pages_per_step):
            page_copy(step + 1, p, 1 - slot).start()

    for p in range(pages_per_step):
        page_copy(step, p, slot).wait()

    a = a_ref[...]
    per = buf.shape[2] // CMP_BLOCK
    for p in range(pages_per_step):
        for k in range(per):
            x = buf[slot, p, pl.ds(k * CMP_BLOCK, CMP_BLOCK)]
            o_ref[0, p, k] = jnp.sum(x * a, axis=0)


def _pool_pages(cache4, page_flat, a3, *, bs, n_pages, page_base, pages_per_step=16):
    page = cache4.shape[1]
    per = page // CMP_BLOCK
    n_cols = 2 * NSA_KV
    total = bs * n_pages
    pages_per_step = min(pages_per_step, total)
    assert total % pages_per_step == 0
    n_steps = total // pages_per_step
    grid_spec = pltpu.PrefetchScalarGridSpec(
        num_scalar_prefetch=1, grid=(n_steps,),
        in_specs=[pl.BlockSpec((CMP_BLOCK, n_cols, LANE), lambda si, pt: (0, 0, 0)),
                  pl.BlockSpec(memory_space=pl.ANY)],
        out_specs=pl.BlockSpec((1, pages_per_step, per, n_cols, LANE), lambda si, pt: (si, 0, 0, 0, 0)),
        scratch_shapes=[pltpu.VMEM((2, pages_per_step, page, n_cols, LANE), F32),
                        pltpu.SemaphoreType.DMA((2,))])
    out = pl.pallas_call(
        functools.partial(_pool_pages_kernel, pages_per_step=pages_per_step, page_base=page_base),
        out_shape=jax.ShapeDtypeStruct((n_steps, pages_per_step, per, n_cols, LANE), F32),
        grid_spec=grid_spec,
        compiler_params=_params(("arbitrary",)),
        name="pool_pages",
    )(page_flat, a3, cache4)
    return out.reshape(bs, n_pages * per, n_cols * LANE)


def _cmp_proj_kernel(p_ref, w_ref, kc_ref, vc_ref):
    pooled = p_ref[0]
    n = pooled.shape[0]
    n_pad = kc_ref.shape[2]
    for c in range(4):
        r = _nn(pooled[:, c * LANE:(c + 1) * LANE].astype(BF16), w_ref[c]).astype(BF16)
        dst = kc_ref if c < 2 else vc_ref
        if n_pad > n:
            dst[0, c % 2] = jnp.zeros((n_pad, LANE), BF16)
        dst[0, c % 2, 0:n, :] = r


def _cmp_proj(pooled, w4):
    b, n, _ = pooled.shape
    n_pad = -(-n // LANE) * LANE
    spec = pl.BlockSpec((1, NSA_KV, n_pad, LANE), lambda bi: (bi, 0, 0, 0))
    return pl.pallas_call(
        _cmp_proj_kernel,
        out_shape=[jax.ShapeDtypeStruct((b, NSA_KV, n_pad, LANE), BF16)] * 2,
        grid=(b,),
        in_specs=[pl.BlockSpec((1, n, 512), lambda bi: (bi, 0, 0)),
                  pl.BlockSpec((4, LANE, LANE), lambda bi: (0, 0, 0))],
        out_specs=[spec, spec],
        compiler_params=_params(("parallel",)),
        name="cmp_proj",
    )(pooled, w4)


def _nsa_prompt_kernel(q_ref, ng_ref, bg_ref, ksel_ref, vsel_ref, kwin_ref, vwin_ref, kc_ref, vc_ref,
                       c_ref, s1_ref, s2_ref, o_ref, m_scr, l_scr, acc_scr, *, t_len, tq, tc, tw, n_sel, top):
    i = pl.program_id(2)
    hq = NSA_HPG
    q = q_ref[0]
    c, s1, s2 = c_ref[...], s1_ref[...], s2_ref[...]
    qc_l, qr_l = [], []
    for h in range(hq):
        qh = q[:, h * LANE:(h + 1) * LANE]
        qc_l.append(qh * ATTN_SCALE)
        qr_l.append(_rope(qh, c, s1, s2) * ATTN_SCALE)
    qc = jnp.concatenate(qc_l, axis=0).astype(BF16)
    qr = jnp.concatenate(qr_l, axis=0).astype(BF16)

    kc = kc_ref[0, 0]
    npad = kc.shape[0]
    s = _nt(qc, kc).reshape(hq, tq, npad)
    tpos = i * tq + lax.broadcasted_iota(jnp.int32, (tq, npad), 0)
    ncol = lax.broadcasted_iota(jnp.int32, (tq, npad), 1)
    cmask = ((ncol + 1) * CMP_BLOCK <= tpos + 1)[None]
    s = jnp.where(cmask, s, MASK_NEG)
    e = jnp.where(cmask, jnp.exp(s - jnp.max(s, axis=-1, keepdims=True)), 0.0)
    p = e / jnp.maximum(jnp.sum(e, axis=-1, keepdims=True), 1e-30)
    o_c = _nn(p.reshape(hq * tq, npad).astype(BF16), vc_ref[0, 0])
    imp = jnp.sum(p, axis=0)

    rows = min(npad, -(-n_sel // 8) * 8)
    imp_t = imp.T[0:rows]
    blk = lax.broadcasted_iota(jnp.int32, (rows, tq), 0)
    cur = (i * tq + lax.broadcasted_iota(jnp.int32, (rows, tq), 1)) // SEL_BLOCK
    imp_t = jnp.where((blk == 0) | (blk == cur) | (blk == cur - 1), FORCE, imp_t)
    imp_t = jnp.where(blk > cur, -1.0, imp_t)
    imp_t = jnp.where(blk >= n_sel, -2.0, imp_t)
    rank = jnp.zeros((rows, tq), F32)
    for j in range(n_sel):
        a = imp_t[j:j + 1, :]
        ahead = (a > imp_t) | ((a == imp_t) & (blk > j))
        rank = rank + jnp.where(ahead, 1.0, 0.0)
    sel_t = jnp.where((rank < top) & (blk < n_sel), 1.0, 0.0)
    if npad > rows:
        sel_t = jnp.concatenate([sel_t, jnp.zeros((npad - rows, tq), F32)], axis=0)
    sel = sel_t.T.astype(BF16)

    m_scr[...] = jnp.full(m_scr.shape, MASK_NEG, F32)
    l_scr[...] = jnp.zeros(l_scr.shape, F32)
    acc_scr[...] = jnp.zeros(acc_scr.shape, F32)
    n_chunks = (i * tq + tq + tc - 1) // tc

    def chunk(ci, carry):
        k0 = pl.multiple_of(ci * tc, tc)
        k = ksel_ref[0, pl.ds(k0, tc), :]
        v = vsel_ref[0, pl.ds(k0, tc), :]
        sc = _nt(qr, k)
        jb = lax.broadcasted_iota(jnp.int32, (npad, tc), 0)
        kb = (k0 + lax.broadcasted_iota(jnp.int32, (npad, tc), 1)) // SEL_BLOCK
        expand = jnp.where(jb == kb, 1.0, 0.0).astype(BF16)
        chosen = _nn(sel, expand)
        kp = k0 + lax.broadcasted_iota(jnp.int32, (tq, tc), 1)
        tp = i * tq + lax.broadcasted_iota(jnp.int32, (tq, tc), 0)
        bias = jnp.where((chosen > 0.5) & (kp <= tp), 0.0, MASK_NEG)
        sc = (sc.reshape(hq, tq, tc) + bias[None]).reshape(hq * tq, tc)
        m_prev = m_scr[...]
        m_new = jnp.maximum(m_prev, jnp.max(sc, axis=-1, keepdims=True))
        alpha = jnp.exp(m_prev - m_new)
        pe = jnp.exp(sc - m_new)
        l_scr[...] = alpha * l_scr[...] + jnp.sum(pe, axis=-1, keepdims=True)
        acc_scr[...] = alpha * acc_scr[...] + _nn(pe.astype(BF16), v)
        m_scr[...] = m_new
        return carry

    lax.fori_loop(0, n_chunks, chunk, 0)
    o_s = acc_scr[...] / l_scr[...]

    w0 = pl.multiple_of(jnp.clip(i * tq + tq - tw, 0, t_len - tw), LANE)
    kw = kwin_ref[0, pl.ds(w0, tw), :]
    vw = vwin_ref[0, pl.ds(w0, tw), :]
    sw = _nt(qr, kw)
    kp = w0 + lax.broadcasted_iota(jnp.int32, (tq, tw), 1)
    tp = i * tq + lax.broadcasted_iota(jnp.int32, (tq, tw), 0)
    wbias = jnp.where((kp <= tp) & (kp > tp - WINDOW), 0.0, MASK_NEG)
    sw = (sw.reshape(hq, tq, tw) + wbias[None]).reshape(hq * tq, tw)
    ew = jnp.exp(sw - jnp.max(sw, axis=-1, keepdims=True))
    o_w = _nn(ew.astype(BF16), vw) / jnp.sum(ew, axis=-1, keepdims=True)

    gate = jax.nn.sigmoid(bg_ref[0])
    ng = ng_ref[0]
    for h in range(hq):
        r = slice(h * tq, (h + 1) * tq)
        o = (gate[:, 3 * h:3 * h + 1] * o_c[r] + gate[:, 3 * h + 1:3 * h + 2] * o_s[r]
             + gate[:, 3 * h + 2:3 * h + 3] * o_w[r])
        o_ref[0, :, h * LANE:(h + 1) * LANE] = (_silu(ng[:, h * LANE:(h + 1) * LANE]) * o).astype(o_ref.dtype)


def _nsa_prompt(z3, bg3, kvb, kc, vc, tabs, *, tq=128, tc=512):
    b, t, _ = z3.shape
    nq = t // tq
    tc = min(tc, t)
    tw = min(WINDOW + tq, t)
    n_sel = t // SEL_BLOCK
    top = min(TOP_N, n_sel)
    npad = kc.shape[2]
    gw = NSA_HPG * LANE
    kvspec = lambda k: pl.BlockSpec((1, t, LANE), lambda bi, gi, qi, k=k: (bi, 0, k + gi))
    cspec = pl.BlockSpec((1, 1, npad, LANE), lambda bi, gi, qi: (bi, gi, 0, 0))
    tspec = pl.BlockSpec((tq, LANE), lambda bi, gi, qi: (qi, 0))
    kern = functools.partial(_nsa_prompt_kernel, t_len=t, tq=tq, tc=tc, tw=tw, n_sel=n_sel, top=top)
    return pl.pallas_call(
        kern,
        out_shape=jax.ShapeDtypeStruct((b, t, NSA_HEADS * LANE), BF16),
        grid=(b, NSA_KV, nq),
        in_specs=[pl.BlockSpec((1, tq, gw), lambda bi, gi, qi: (bi, qi, Q_OFF // gw + gi)),
                  pl.BlockSpec((1, tq, gw), lambda bi, gi, qi: (bi, qi, NG_OFF // gw + gi)),
                  pl.BlockSpec((1, tq, LANE), lambda bi, gi, qi: (bi, qi, gi)),
                  kvspec(0), kvspec(2), kvspec(4), kvspec(6), cspec, cspec, tspec, tspec, tspec],
        out_specs=pl.BlockSpec((1, tq, gw), lambda bi, gi, qi: (bi, qi, gi)),
        scratch_shapes=[pltpu.VMEM((NSA_HPG * tq, 1), F32), pltpu.VMEM((NSA_HPG * tq, 1), F32),
                        pltpu.VMEM((NSA_HPG * tq, LANE), F32)],
        compiler_params=_params(("parallel", "parallel", "arbitrary")),
        name="nsa_prompt",
    )(z3, z3, bg3, kvb, kvb, kvb, kvb, kc, vc, *tabs)


def _conv_kernel(h_ref, b_ref, c_ref, g_ref, prev_ref, w_ref, y_ref, st_ref, up_scr, *, t_real):
    u = c_ref[0] * h_ref[0]
    t = u.shape[0]
    up_scr[pl.ds(8 - (CONV_W - 1), CONV_W - 1), :] = prev_ref[0]
    up_scr[pl.ds(8, t), :] = u
    w = w_ref[...]
    y = w[0:1, :] * up_scr[pl.ds(6, t), :]
    y = y + w[1:2, :] * up_scr[pl.ds(7, t), :]
    y = y + w[2:3, :] * u
    y = b_ref[0] * y
    y_ref[0] = (_silu(g_ref[0]) * y).astype(y_ref.dtype)
    st_ref[0] = up_scr[pl.ds(6 + t_real, CONV_W - 1), :]


def _conv_mixer(z3, prev, w_conv, *, t_real):
    b, t, _ = z3.shape
    nc = CONV_DIM // LANE
    zspec = lambda off: pl.BlockSpec((1, t, LANE), lambda bi, ci, off=off: (bi, 0, off // LANE + ci))
    return pl.pallas_call(
        functools.partial(_conv_kernel, t_real=t_real),
        out_shape=[jax.ShapeDtypeStruct((b, t, CONV_DIM), BF16),
                   jax.ShapeDtypeStruct((b, CONV_W - 1, CONV_DIM), F32)],
        grid=(b, nc),
        in_specs=[zspec(C_H), zspec(C_B), zspec(C_C), zspec(C_G),
                  pl.BlockSpec((1, CONV_W - 1, LANE), lambda bi, ci: (bi, 0, ci)),
                  pl.BlockSpec((CONV_W, LANE), lambda bi, ci: (0, ci))],
        out_specs=[pl.BlockSpec((1, t, LANE), lambda bi, ci: (bi, 0, ci)),
                   pl.BlockSpec((1, CONV_W - 1, LANE), lambda bi, ci: (bi, 0, ci))],
        scratch_shapes=[pltpu.VMEM((t + 8, LANE), F32)],
        compiler_params=_params(("parallel", "parallel")),
        name="conv_mixer",
    )(z3, z3, z3, z3, prev, w_conv)


def _mem_attn_kernel(q_ref, mg_ref, kv_ref, o_ref):
    q = q_ref[0]
    mg = mg_ref[0]
    half = MEM_HEADS * LANE
    for h in range(MEM_HEADS):
        lo, hi = h * LANE, (h + 1) * LANE
        k = kv_ref[0, :, lo:hi].astype(BF16)
        v = kv_ref[0, :, half + lo:half + hi].astype(BF16)
        s = _nt((q[:, lo:hi] * ATTN_SCALE).astype(BF16), k)
        e = jnp.exp(s - jnp.max(s, axis=-1, keepdims=True))
        o = _nn(e.astype(BF16), v) / jnp.sum(e, axis=-1, keepdims=True)
        o_ref[0, :, lo:hi] = (_silu(mg[:, lo:hi]) * o).astype(o_ref.dtype)


def _mem_attn(z3, mkv, *, tq):
    b, t, _ = z3.shape
    nm, wkv = mkv.shape[1], mkv.shape[2]
    wq = MEM_HEADS * LANE
    return pl.pallas_call(
        _mem_attn_kernel,
        out_shape=jax.ShapeDtypeStruct((b, t, wq), BF16),
        grid=(b, t // tq),
        in_specs=[pl.BlockSpec((1, tq, wq), lambda bi, ti: (bi, ti, MQ_OFF // wq)),
                  pl.BlockSpec((1, tq, wq), lambda bi, ti: (bi, ti, MG_OFF // wq)),
                  pl.BlockSpec((1, nm, wkv), lambda bi, ti: (bi, 0, 0))],
        out_specs=pl.BlockSpec((1, tq, wq), lambda bi, ti: (bi, ti, 0)),
        compiler_params=_params(("parallel", "parallel")),
        name="mem_attn",
    )(z3, z3, mkv)


def _out_proj_kernel(x_ref, ya_ref, yb_ref, ym_ref, w_ref, fg_ref, o_ref, *, final):
    a, bw = CONV_DIM, CONV_DIM + NSA_HEADS * LANE
    acc = _nn(ya_ref[...], w_ref[0:a, :])
    acc = acc + _nn(yb_ref[...], w_ref[a:bw, :])
    acc = acc + _nn(ym_ref[...], w_ref[bw:, :])
    r = x_ref[...] + acc
    if final:
        r = r * lax.rsqrt(jnp.mean(r * r, axis=-1, keepdims=True) + NORM_EPS) * fg_ref[...]
    o_ref[...] = r


def _out_proj(x, ya, yb, ym, w, fg, *, tm, final):
    m, d = x.shape
    row = lambda width: pl.BlockSpec((tm, width), lambda i: (i, 0))
    return pl.pallas_call(
        functools.partial(_out_proj_kernel, final=final),
        out_shape=jax.ShapeDtypeStruct((m, d), F32),
        grid=(m // tm,),
        in_specs=[row(d), row(ya.shape[1]), row(yb.shape[1]), row(ym.shape[1]),
                  pl.BlockSpec(w.shape, lambda i: (0, 0)),
                  pl.BlockSpec((1, d), lambda i: (0, 0))],
        out_specs=row(d),
        compiler_params=_params(("parallel",)),
        name="out_proj",
    )(x, ya, yb, ym, w, fg.reshape(1, d))


def _sample_pre_kernel(q_ref, ng_ref, bg_ref, kv0_ref, kv1_ref, kv2_ref, c_ref, s1_ref, s2_ref, kc_ref, vc_ref,
                       kvn_ref, wn_ref, qr_ref, oc_ref, ngo_ref, gate_ref, idx_ref, *, past, n_top):
    tp = SAMPLE_T_PAD
    hq = NSA_HPG
    c, s1, s2 = c_ref[...], s1_ref[...], s2_ref[...]
    kv0, kv1, kv2 = kv0_ref[0], kv1_ref[0], kv2_ref[0]
    kvn_ref[0, :, 0:512] = kv0
    kvn_ref[0, :, 768:1024] = kv1[:, 256:512]
    wn_ref[0, :, 256:512] = kv2[:, 256:512]
    for g in range(NSA_KV):
        lo, hi = g * LANE, (g + 1) * LANE
        kvn_ref[0, :, 512 + lo:512 + hi] = _rope(kv1[:, lo:hi], c, s1, s2)
        wn_ref[0, :, lo:hi] = _rope(kv2[:, lo:hi], c, s1, s2)

    q = q_ref[0]
    ng = ng_ref[0]
    gates = jax.nn.sigmoid(bg_ref[0])
    for g in range(NSA_KV):
        qc_l, qr_l = [], []
        for h in range(hq):
            lo = (g * hq + h) * LANE
            qh = q[:, lo:lo + LANE]
            qc_l.append(qh * ATTN_SCALE)
            qr_l.append(_rope(qh, c, s1, s2) * ATTN_SCALE)
            ngo_ref[0, g, h * tp:(h + 1) * tp, :] = ng[:, lo:lo + LANE]
            for br in range(N_BRANCH):
                col = g * LANE + h * N_BRANCH + br
                gate_ref[0, g, br, h * tp:(h + 1) * tp, :] = jnp.broadcast_to(gates[:, col:col + 1], (tp, LANE))
        qc = jnp.concatenate(qc_l, axis=0)
        qr_ref[0, g] = jnp.concatenate(qr_l, axis=0)

        kc = kc_ref[0, g]
        npad = kc.shape[0]
        s = _nt(qc.astype(BF16), kc)
        trow = lax.broadcasted_iota(jnp.int32, (hq * tp, npad), 0) % tp
        ncol = lax.broadcasted_iota(jnp.int32, (hq * tp, npad), 1)
        cmask = (ncol + 1) * CMP_BLOCK <= past + trow + 1
        s = jnp.where(cmask, s, MASK_NEG)
        e = jnp.where(cmask, jnp.exp(s - jnp.max(s, axis=-1, keepdims=True)), 0.0)
        p = e / jnp.maximum(jnp.sum(e, axis=-1, keepdims=True), 1e-30)
        oc_ref[0, g] = _nn(p.astype(BF16), vc_ref[0, g])
        imp = jnp.sum(p.reshape(hq, tp, npad), axis=0)

        blk = lax.broadcasted_iota(jnp.int32, (tp, npad), 1)
        cur = (past + lax.broadcasted_iota(jnp.int32, (tp, npad), 0)) // SEL_BLOCK
        val = jnp.where((blk == 0) | (blk == cur) | (blk == cur - 1), FORCE, imp)
        val = jnp.where(blk > cur, -1.0, val)
        val = jnp.where(blk >= past // SEL_BLOCK, -2.0, val)
        lane = lax.broadcasted_iota(jnp.int32, (tp, LANE), 1)
        idx = jnp.zeros((tp, LANE), jnp.int32)
        for r in range(n_top):
            best = jnp.max(val, axis=-1, keepdims=True)
            j = jnp.min(jnp.where(val == best, blk, npad), axis=-1, keepdims=True)
            idx = jnp.where(lane == r, j, idx)
            val = jnp.where(blk == j, -3e38, val)
        idx_ref[0, g] = idx


def _sample_pre(z3, bg3, tabs, kc, vc, *, past, n_top):
    bs, tp, _ = z3.shape
    npad = kc.shape[2]
    qw = NSA_HEADS * LANE
    kvblk = KV_OFF // 512
    zspec = lambda k: pl.BlockSpec((1, tp, 512), lambda bi, k=k: (bi, 0, kvblk + k))
    tspec = pl.BlockSpec((tp, LANE), lambda bi: (0, 0))
    cspec = pl.BlockSpec((1, NSA_KV, npad, LANE), lambda bi: (bi, 0, 0, 0))
    rows = NSA_HPG * tp
    gspec = pl.BlockSpec((1, NSA_KV, rows, LANE), lambda bi: (bi, 0, 0, 0))
    gshape = jax.ShapeDtypeStruct((bs, NSA_KV, rows, LANE), F32)
    return pl.pallas_call(
        functools.partial(_sample_pre_kernel, past=past, n_top=n_top),
        out_shape=[jax.ShapeDtypeStruct((bs, tp, 1024), F32),
                   jax.ShapeDtypeStruct((bs, tp, 512), F32),
                   gshape, gshape, gshape,
                   jax.ShapeDtypeStruct((bs, NSA_KV, N_BRANCH, rows, LANE), F32),
                   jax.ShapeDtypeStruct((bs, NSA_KV, tp, LANE), jnp.int32)],
        grid=(bs,),
        in_specs=[pl.BlockSpec((1, tp, qw), lambda bi: (bi, 0, Q_OFF // qw)),
                  pl.BlockSpec((1, tp, qw), lambda bi: (bi, 0, NG_OFF // qw)),
                  pl.BlockSpec((1, tp, NSA_KV * LANE), lambda bi: (bi, 0, 0)),
                  zspec(0), zspec(1), zspec(2), tspec, tspec, tspec, cspec, cspec],
        out_specs=[pl.BlockSpec((1, tp, 1024), lambda bi: (bi, 0, 0)),
                   pl.BlockSpec((1, tp, 512), lambda bi: (bi, 0, 0)),
                   gspec, gspec, gspec,
                   pl.BlockSpec((1, NSA_KV, N_BRANCH, rows, LANE), lambda bi: (bi, 0, 0, 0, 0)),
                   pl.BlockSpec((1, NSA_KV, tp, LANE), lambda bi: (bi, 0, 0, 0))],
        compiler_params=_params(("parallel",)),
        name="sample_pre",
    )(z3, z3, bg3, z3, z3, z3, *tabs, kc, vc)


def _sample_attn_kernel(idx_sm, pt_sm, qr_ref, oc_ref, ng_ref, gate_ref, ksn_ref, vsn_ref,
                        kwc_ref, vwc_ref, kwn_ref, vwn_ref, cache_ref, o_ref,
                        kbuf, vbuf, kw_scr, vw_scr, sem, *, ts, n_top, n_pages, page_base, per_page, wb):
    tp = SAMPLE_T_PAD
    hq = NSA_HPG
    b = pl.program_id(0)
    g = pl.program_id(1)
    n_gath = n_top * SEL_BLOCK
    ks_rows = kbuf.shape[1]

    def gather_copies(t, r):
        blk = idx_sm[((b * NSA_KV + g) * ts + t) * n_top + r]
        page = pt_sm[b * n_pages + blk // per_page] + page_base
        row0 = (blk % per_page) * SEL_BLOCK
        src_k = cache_ref.at[page, pl.ds(row0, SEL_BLOCK), 2 * NSA_KV + g]
        src_v = cache_ref.at[page, pl.ds(row0, SEL_BLOCK), 3 * NSA_KV + g]
        dst = pl.ds(r * SEL_BLOCK, SEL_BLOCK)
        return (pltpu.make_async_copy(src_k, kbuf.at[t, dst], sem.at[0]),
                pltpu.make_async_copy(src_v, vbuf.at[t, dst], sem.at[1]))

    for t in range(ts):
        for r in range(n_top):
            ck, cv = gather_copies(t, r)
            ck.start()
            cv.start()

    qr = qr_ref[0, 0].astype(BF16)
    trow = lax.broadcasted_iota(jnp.int32, (hq * tp, 1), 0) % tp

    ww = kw_scr.shape[0]
    kw_scr[pl.ds(0, wb), :] = kwc_ref[0]
    vw_scr[pl.ds(0, wb), :] = vwc_ref[0]
    kw_scr[pl.ds(wb, tp), :] = kwn_ref[0]
    vw_scr[pl.ds(wb, tp), :] = vwn_ref[0]
    kw_scr[pl.ds(wb + tp, ww - wb - tp), :] = jnp.zeros((ww - wb - tp, LANE), F32)
    vw_scr[pl.ds(wb + tp, ww - wb - tp), :] = jnp.zeros((ww - wb - tp, LANE), F32)
    sw = _nt(qr, kw_scr[...].astype(BF16))
    jw = lax.broadcasted_iota(jnp.int32, (hq * tp, ww), 1)
    rel = jw - wb
    okw = (rel <= trow) & (rel > trow - WINDOW) & (jw < wb + ts)
    sw = jnp.where(okw, sw, MASK_NEG)
    ew = jnp.where(okw, jnp.exp(sw - jnp.max(sw, axis=-1, keepdims=True)), 0.0)
    o_w = _nn(ew.astype(BF16), vw_scr[...].astype(BF16)) / jnp.sum(ew, axis=-1, keepdims=True)

    for t in range(ts):
        for r in range(n_top):
            ck, cv = gather_copies(t, r)
            ck.wait()
            cv.wait()

    js = lax.broadcasted_iota(jnp.int32, (hq * tp, ks_rows), 1)
    o_s = jnp.zeros((hq * tp, LANE), F32)
    for t in range(ts):
        kbuf[t, pl.ds(n_gath, tp), :] = ksn_ref[0]
        vbuf[t, pl.ds(n_gath, tp), :] = vsn_ref[0]
        kbuf[t, pl.ds(n_gath + tp, ks_rows - n_gath - tp), :] = jnp.zeros((ks_rows - n_gath - tp, LANE), F32)
        vbuf[t, pl.ds(n_gath + tp, ks_rows - n_gath - tp), :] = jnp.zeros((ks_rows - n_gath - tp, LANE), F32)
        ss = _nt(qr, kbuf[t].astype(BF16))
        oks = (js < n_gath) | ((js - n_gath <= t) & (js < n_gath + ts))
        ss = jnp.where(oks, ss, MASK_NEG)
        es = jnp.where(oks, jnp.exp(ss - jnp.max(ss, axis=-1, keepdims=True)), 0.0)
        ot = _nn(es.astype(BF16), vbuf[t].astype(BF16)) / jnp.sum(es, axis=-1, keepdims=True)
        o_s = jnp.where(trow == t, ot, o_s)

    o = gate_ref[0, 0, 0] * oc_ref[0, 0] + gate_ref[0, 0, 1] * o_s + gate_ref[0, 0, 2] * o_w
    y = _silu(ng_ref[0, 0]) * o
    for h in range(hq):
        o_ref[0, :, h * LANE:(h + 1) * LANE] = y[h * tp:(h + 1) * tp].astype(o_ref.dtype)


def _sample_attn(idx_flat, page_flat, qr, oc, ngo, gates, kvn, cache_win3, wn, cache4, *,
                 ts, n_top, n_pages, page_base, win_base):
    bs = qr.shape[0]
    tp = SAMPLE_T_PAD
    rows = NSA_HPG * tp
    wb = cache_win3.shape[1]
    per_page = cache4.shape[1] // SEL_BLOCK
    ks_rows = -(-(n_top * SEL_BLOCK + tp) // LANE) * LANE
    ww = -(-(wb + tp) // LANE) * LANE
    gspec = pl.BlockSpec((1, 1, rows, LANE), lambda bi, gi, *_: (bi, gi, 0, 0))
    newspec = lambda k: pl.BlockSpec((1, tp, LANE), lambda bi, gi, *_, k=k: (bi, 0, k + gi))
    wcspec = lambda k: pl.BlockSpec((1, wb, LANE), lambda bi, gi, *_, k=k: (bi + win_base, 0, k + gi))
    grid_spec = pltpu.PrefetchScalarGridSpec(
        num_scalar_prefetch=2, grid=(bs, NSA_KV),
        in_specs=[gspec, gspec, gspec,
                  pl.BlockSpec((1, 1, N_BRANCH, rows, LANE), lambda bi, gi, *_: (bi, gi, 0, 0, 0)),
                  newspec(2 * NSA_KV), newspec(3 * NSA_KV),
                  wcspec(0), wcspec(NSA_KV), newspec(0), newspec(NSA_KV),
                  pl.BlockSpec(memory_space=pl.ANY)],
        out_specs=pl.BlockSpec((1, tp, NSA_HPG * LANE), lambda bi, gi, *_: (bi, 0, gi)),
        scratch_shapes=[pltpu.VMEM((ts, ks_rows, LANE), F32), pltpu.VMEM((ts, ks_rows, LANE), F32),
                        pltpu.VMEM((ww, LANE), F32), pltpu.VMEM((ww, LANE), F32),
                        pltpu.SemaphoreType.DMA((2,))])
    kern = functools.partial(_sample_attn_kernel, ts=ts, n_top=n_top, n_pages=n_pages,
                             page_base=page_base, per_page=per_page, wb=wb)
    return pl.pallas_call(
        kern,
        out_shape=jax.ShapeDtypeStruct((bs, tp, NSA_HEADS * LANE), BF16),
        grid_spec=grid_spec,
        compiler_params=_params(("arbitrary", "arbitrary")),
        name="sample_attn",
    )(idx_flat, page_flat, qr, oc, ngo, gates, kvn, kvn, cache_win3, cache_win3, wn, wn, cache4)


def _rope_tables(pos, rows):
    freqs = jnp.power(ROPE_THETA, -jnp.arange(ROPE_HALF, dtype=F32) * (2.0 / ROPE_DIM))
    ang = pos.astype(F32)[:, None] * freqs[None, :]
    cos, sin = jnp.cos(ang), jnp.sin(ang)
    n = pos.shape[0]
    z16 = jnp.zeros((n, ROPE_HALF), F32)
    rest = LANE - ROPE_DIM
    c = jnp.concatenate([cos, cos, jnp.ones((n, rest), F32)], axis=1)
    s1 = jnp.concatenate([z16, sin, jnp.zeros((n, rest), F32)], axis=1)
    s2 = jnp.concatenate([-sin, z16, jnp.zeros((n, rest), F32)], axis=1)
    pad = lambda a: jnp.pad(a, ((0, rows - n), (0, 0)))
    return pad(c), pad(s1), pad(s2)


def _layer_weights(norm_g, w_in, w_conv, a_cmp, w_cmp, w_out):
    w_main = jnp.concatenate([w_in[:, :BG_SRC].astype(BF16), w_in[:, BG_SRC + BG_N:].astype(BF16)], axis=1)
    per_group = NSA_HPG * N_BRANCH
    wbg = jnp.concatenate(
        [jnp.pad(w_in[:, BG_SRC + g * per_group:BG_SRC + (g + 1) * per_group], ((0, 0), (0, LANE - per_group)))
         for g in range(NSA_KV)], axis=1).astype(BF16)
    a4 = jnp.concatenate([a_cmp[0], a_cmp[0], a_cmp[1], a_cmp[1]], axis=1)
    a3 = jnp.stack([a_cmp[0], a_cmp[0], a_cmp[1], a_cmp[1]], axis=1)
    w4 = jnp.stack([w_cmp[0], w_cmp[0], w_cmp[1], w_cmp[1]]).astype(BF16)
    return dict(norm_g=norm_g, w_main=w_main, wbg=wbg, w_conv=w_conv, a4=a4, a3=a3, w4=w4,
                w_out=w_out.astype(BF16))


def _prompt_layer(xp, mem_prompt, mem_norm_g, w_mem, lw, final_g, final):
    b, t, d = xp.shape
    m = b * t
    z, bg = _norm_matmul(xp.reshape(m, d), lw["norm_g"], lw["w_main"], lw["wbg"], tm=min(512, m), tn=512)
    z3 = z.reshape(b, t, Z_WIDTH)
    bg3 = bg.reshape(b, t, NSA_KV * LANE)
    nm = mem_prompt.shape[1]
    mkv = _norm_matmul(mem_prompt.reshape(b * nm, d), mem_norm_g, w_mem, tm=min(512, b * nm), tn=512)
    mkv3 = mkv.reshape(b, nm, 2 * MEM_HEADS * LANE)
    tabs = _rope_tables(jnp.arange(t, dtype=jnp.int32), t)
    wb = min(WINDOW, t)
    kvn, kvb, win_full = _kv_export(z3, tabs, tr=min(512, t))
    kc, vc = _cmp_proj(_pool_prompt(z3, lw["a4"]), lw["w4"])
    yb = _nsa_prompt(z3, bg3, kvb, kc, vc, tabs, tq=min(128, t))
    conv0 = jnp.zeros((b, CONV_W - 1, CONV_DIM), F32)
    ya, conv_new = _conv_mixer(z3, conv0, lw["w_conv"], t_real=t)
    ym = _mem_attn(z3, mkv3, tq=min(512, t))
    out = _out_proj(xp.reshape(m, d), ya.reshape(m, -1), yb.reshape(m, -1), ym.reshape(m, -1),
                    lw["w_out"], final_g, tm=min(256, m), final=final)
    kv_new = kvn.reshape(b, t, 4, NSA_KV, HEAD_DIM)
    win_new = win_full[:, t - wb:].reshape(b, wb, 2, NSA_KV, HEAD_DIM)
    mem_kv = mkv.reshape(b, nm, 2, MEM_HEADS, HEAD_DIM)
    return out.reshape(b, t, d), kv_new, win_new, conv_new, mem_kv


def _sample_layer(xs_p, ts, layer, cache_kv, cache_win, state_conv, cache_mem, page_table, lw, final_g, final):
    bs, tp, d = xs_p.shape
    depth, pool, page = cache_kv.shape[0], cache_kv.shape[1], cache_kv.shape[2]
    n_pages = page_table.shape[1]
    past = n_pages * page
    assert past % SEL_BLOCK == 0 and ts <= SEL_BLOCK and ts <= tp
    n_past = past // SEL_BLOCK
    n_top = min(TOP_N, n_past + 1) - 1
    m = bs * tp
    z, bg = _norm_matmul(xs_p.reshape(m, d), lw["norm_g"], lw["w_main"], lw["wbg"], tm=m, tn=512)
    z3 = z.reshape(bs, tp, Z_WIDTH)
    bg3 = bg.reshape(bs, tp, NSA_KV * LANE)
    tabs = _rope_tables(past + jnp.arange(tp, dtype=jnp.int32), tp)
    page_flat = page_table.reshape(-1).astype(jnp.int32)
    cache4 = cache_kv.reshape(depth * pool, page, 4 * NSA_KV, HEAD_DIM)
    pooled = _pool_pages(cache4, page_flat, lw["a3"], bs=bs, n_pages=n_pages, page_base=layer * pool)
    kc, vc = _cmp_proj(pooled, lw["w4"])
    kvn, wn, qr, oc, ngo, gates, idx = _sample_pre(z3, bg3, tabs, kc, vc, past=past, n_top=n_top)
    idx_flat = idx[:, :, :ts, :n_top].reshape(-1)
    wbuf = cache_win.shape[2]
    cache_win3 = cache_win.reshape(depth * bs, wbuf, 2 * NSA_KV * HEAD_DIM)
    yb = _sample_attn(idx_flat, page_flat, qr, oc, ngo, gates, kvn, cache_win3, wn, cache4,
                      ts=ts, n_top=n_top, n_pages=n_pages, page_base=layer * pool, win_base=layer * bs)
    ya, conv_new = _conv_mixer(z3, state_conv[layer], lw["w_conv"], t_real=ts)
    nm = cache_mem.shape[2]
    mem3 = cache_mem.reshape(depth * bs, nm, 2 * MEM_HEADS * HEAD_DIM)[layer * bs:(layer + 1) * bs]
    ym = _mem_attn(z3, mem3, tq=tp)
    out = _out_proj(xs_p.reshape(m, d), ya.reshape(m, -1), yb.reshape(m, -1), ym.reshape(m, -1),
                    lw["w_out"], final_g, tm=m, final=final)
    kv_new = kvn[:, :ts].reshape(bs, ts, 4, NSA_KV, HEAD_DIM)
    win_rows = wn[:, :ts].reshape(bs, ts, 2, NSA_KV, HEAD_DIM)
    win_state = jnp.concatenate([cache_win[layer], win_rows], axis=1)[:, ts:]
    return out.reshape(bs, tp, d), kv_new, win_state, conv_new


def kernel(x_prompt, x_sample, cache_kv, cache_win, state_conv, cache_mem, page_table, mem_prompt,
           norm_g, w_in, w_conv, a_cmp, w_cmp, mem_norm_g, w_mem_kv, w_out, final_g):
    depth = w_in.shape[0]
    ts = x_sample.shape[1]
    xp = x_prompt
    xs = jnp.pad(x_sample, ((0, 0), (0, SAMPLE_T_PAD - ts), (0, 0)))
    kv_p, win_p, conv_p, mem_p, kv_s, win_s, conv_s = [], [], [], [], [], [], []
    for l in range(depth):
        lw = _layer_weights(norm_g[l], w_in[l], w_conv[l], a_cmp[l], w_cmp[l], w_out[l])
        final = l == depth - 1
        xp, kvn, winn, convn, mkv = _prompt_layer(xp, mem_prompt, mem_norm_g[l], w_mem_kv[l].astype(BF16),
                                                  lw, final_g, final)
        kv_p.append(kvn)
        win_p.append(winn)
        conv_p.append(convn)
        mem_p.append(mkv)
        xs, kvn, winn, convn = _sample_layer(xs, ts, l, cache_kv, cache_win, state_conv, cache_mem,
                                             page_table, lw, final_g, final)
        kv_s.append(kvn)
        win_s.append(winn)
        conv_s.append(convn)
    return (xp, xs[:, :ts], jnp.stack(kv_p), jnp.stack(win_p), jnp.stack(conv_p), jnp.stack(mem_p),
            jnp.stack(kv_s), jnp.stack(win_s), jnp.stack(conv_s))
```

```python
import functools

import jax
import jax.numpy as jnp
from jax import lax
from jax.experimental import pallas as pl
from jax.experimental.pallas import tpu as pltpu

F32 = jnp.float32
BF16 = jnp.bfloat16

HEAD_DIM = 128
CONV_DIM = 512
CONV_W = 3
NSA_HEADS = 8
NSA_KV = 2
NSA_HPG = NSA_HEADS // NSA_KV
MEM_HEADS = 4
N_BRANCH = 3
ROPE_DIM = HEAD_DIM // 4
ROPE_HALF = ROPE_DIM // 2
ROPE_THETA = 500000.0
CMP_BLOCK = 64
SEL_BLOCK = 64
TOP_N = 16
WINDOW = 512
NORM_EPS = 1e-6
MASK_NEG = -1e30
FORCE = 1e9
ATTN_SCALE = HEAD_DIM ** -0.5
SCALE_LOG2 = ATTN_SCALE * 1.4426950408889634

C_H, C_B, C_C, C_G = 0, 512, 1024, 1536
Q_OFF, NG_OFF, KV_OFF, MQ_OFF, MG_OFF = 2048, 3072, 4096, 5632, 6144
Z_WIDTH = 6656
BG_SRC = 4096
BG_N = NSA_HEADS * N_BRANCH
LANE = 128
SAMPLE_T_PAD = 8
VMEM_LIMIT = 48 * 1024 * 1024


def _nt(a, b):
    return lax.dot_general(a, b, (((1,), (1,)), ((), ())), preferred_element_type=F32)


def _nn(a, b):
    return jnp.dot(a, b, preferred_element_type=F32)


def _params(sem, vmem=VMEM_LIMIT):
    return pltpu.CompilerParams(dimension_semantics=sem, vmem_limit_bytes=vmem)


def _rope(x, c, s1, s2):
    return x * c + pltpu.roll(x, ROPE_HALF, 1) * s1 + pltpu.roll(x, LANE - ROPE_HALF, 1) * s2


def _silu(x):
    return x * jax.nn.sigmoid(x)


def _norm_matmul_kernel(x_ref, g_ref, w_ref, *rest, with_gate):
    if with_gate:
        wbg_ref, z_ref, bg_ref, h_scr = rest
    else:
        z_ref, h_scr = rest

    @pl.when(pl.program_id(1) == 0)
    def _():
        x = x_ref[...]
        y = x * lax.rsqrt(jnp.mean(x * x, axis=-1, keepdims=True) + NORM_EPS) * g_ref[...]
        h = y.astype(BF16)
        h_scr[...] = h
        if with_gate:
            bg_ref[...] = _nn(h, wbg_ref[...])

    z_ref[...] = _nn(h_scr[...], w_ref[...])


def _norm_matmul(x, g, w, wbg=None, *, tm, tn):
    m, d = x.shape
    n = w.shape[1]
    with_gate = wbg is not None
    in_specs = [pl.BlockSpec((tm, d), lambda i, j: (i, 0)),
                pl.BlockSpec((1, d), lambda i, j: (0, 0)),
                pl.BlockSpec((d, tn), lambda i, j: (0, j))]
    out_shape = [jax.ShapeDtypeStruct((m, n), F32)]
    out_specs = [pl.BlockSpec((tm, tn), lambda i, j: (i, j))]
    args = [x, g.reshape(1, d), w]
    if with_gate:
        nb = wbg.shape[1]
        in_specs.append(pl.BlockSpec((d, nb), lambda i, j: (0, 0)))
        out_shape.append(jax.ShapeDtypeStruct((m, nb), F32))
        out_specs.append(pl.BlockSpec((tm, nb), lambda i, j: (i, 0)))
        args.append(wbg)
    res = pl.pallas_call(
        functools.partial(_norm_matmul_kernel, with_gate=with_gate),
        out_shape=out_shape, grid=(m // tm, n // tn), in_specs=in_specs, out_specs=out_specs,
        scratch_shapes=[pltpu.VMEM((tm, d), BF16)],
        compiler_params=_params(("parallel", "arbitrary")),
        name="norm_matmul_gate" if with_gate else "norm_matmul",
    )(*args)
    return res if with_gate else res[0]


def _kv_export_kernel(kv1_ref, kv2_ref, kv0_ref, c_ref, s1_ref, s2_ref, kvn_ref, kvb_ref, win_ref):
    c, s1, s2 = c_ref[...], s1_ref[...], s2_ref[...]
    kv0 = kv0_ref[0]
    kv1 = kv1_ref[0]
    kv2 = kv2_ref[0]
    kvn_ref[0, :, 0:512] = kv0
    for g in range(NSA_KV):
        lo, hi = g * LANE, (g + 1) * LANE
        ks = _rope(kv1[:, lo:hi], c, s1, s2)
        kw = _rope(kv2[:, lo:hi], c, s1, s2)
        kvn_ref[0, :, 512 + lo:512 + hi] = ks
        kvb_ref[0, :, lo:hi] = ks.astype(BF16)
        kvb_ref[0, :, 512 + lo:512 + hi] = kw.astype(BF16)
        win_ref[0, :, lo:hi] = kw
    kvn_ref[0, :, 768:1024] = kv1[:, 256:512]
    kvb_ref[0, :, 256:512] = kv1[:, 256:512].astype(BF16)
    kvb_ref[0, :, 768:1024] = kv2[:, 256:512].astype(BF16)
    win_ref[0, :, 256:512] = kv2[:, 256:512]


def _kv_export(z3, tabs, *, tr):
    b, t, _ = z3.shape
    nt = t // tr
    kvblk = KV_OFF // 512
    zspec = lambda k: pl.BlockSpec((1, tr, 512), lambda bi, ti, k=k: (bi, ti, kvblk + k))
    tspec = pl.BlockSpec((tr, LANE), lambda bi, ti: (ti, 0))
    return pl.pallas_call(
        _kv_export_kernel,
        out_shape=[jax.ShapeDtypeStruct((b, t, 1024), F32),
                   jax.ShapeDtypeStruct((b, t, 1024), BF16),
                   jax.ShapeDtypeStruct((b, t, 512), F32)],
        grid=(b, nt),
        in_specs=[zspec(1), zspec(2), zspec(0), tspec, tspec, tspec],
        out_specs=[pl.BlockSpec((1, tr, 1024), lambda bi, ti: (bi, ti, 0)),
                   pl.BlockSpec((1, tr, 1024), lambda bi, ti: (bi, ti, 0)),
                   pl.BlockSpec((1, tr, 512), lambda bi, ti: (bi, ti, 0))],
        compiler_params=_params(("parallel", "parallel")),
        name="kv_export",
    )(z3, z3, z3, *tabs)


def _pool_kernel(*refs, n_prefetch):
    x_ref, a_ref, o_ref = refs[n_prefetch:]
    x = x_ref[0]
    rows = x.shape[0]
    x3 = x.reshape(rows // CMP_BLOCK, CMP_BLOCK, x.shape[1])
    pooled = jnp.sum(x3 * a_ref[...][None], axis=1)
    o_ref[...] = pooled.reshape(o_ref.shape)


def _pool_prompt(z3, a4):
    b, t, _ = z3.shape
    n = t // CMP_BLOCK
    return pl.pallas_call(
        functools.partial(_pool_kernel, n_prefetch=0),
        out_shape=jax.ShapeDtypeStruct((b, n, 512), F32),
        grid=(b,),
        in_specs=[pl.BlockSpec((1, t, 512), lambda bi: (bi, 0, KV_OFF // 512)),
                  pl.BlockSpec((CMP_BLOCK, 512), lambda bi: (0, 0))],
        out_specs=pl.BlockSpec((1, n, 512), lambda bi: (bi, 0, 0)),
        compiler_params=_params(("parallel",)),
        name="pool_prompt",
    )(z3, a4)


def _pool_pages_kernel(pt_sm, a_ref, cache_ref, o_ref, buf, sem, *, pages_per_step, page_base):
    step = pl.program_id(0)
    n_steps = pl.num_programs(0)
    slot = step % 2
    n_cols = buf.shape[3]

    def page_copy(step_idx, p, to_slot):
        page = pt_sm[step_idx * pages_per_step + p] + page_base
        return pltpu.make_async_copy(cache_ref.at[page, :, pl.ds(0, n_cols), :], buf.at[to_slot, p], sem.at[to_slot])

    @pl.when(step == 0)
    def _():
        for p in range(pages_per_step):
            page_copy(0, p, 0).start()

    @pl.when(step + 1 < n_steps)
    def _():
        for p in range(pages_per_step):
            page_copy(step + 1, p, 1 - slot).start()

    for p in range(pages_per_step):
        page_copy(step, p, slot).wait()

    a = a_ref[...]
    per = buf.shape[2] // CMP_BLOCK
    for p in range(pages_per_step):
        for k in range(per):
            x = buf[slot, p, pl.ds(k * CMP_BLOCK, CMP_BLOCK)]
            o_ref[0, p, k] = jnp.sum(x * a, axis=0)


def _pool_pages(cache4, page_flat, a3, *, bs, n_pages, page_base, pages_per_step=16):
    page = cache4.shape[1]
    per = page // CMP_BLOCK
    n_cols = 2 * NSA_KV
    total = bs * n_pages
    pages_per_step = min(pages_per_step, total)
    assert total % pages_per_step == 0
    n_steps = total // pages_per_step
    grid_spec = pltpu.PrefetchScalarGridSpec(
        num_scalar_prefetch=1, grid=(n_steps,),
        in_specs=[pl.BlockSpec((CMP_BLOCK, n_cols, LANE), lambda si, pt: (0, 0, 0)),
                  pl.BlockSpec(memory_space=pl.ANY)],
        out_specs=pl.BlockSpec((1, pages_per_step, per, n_cols, LANE), lambda si, pt: (si, 0, 0, 0, 0)),
        scratch_shapes=[pltpu.VMEM((2, pages_per_step, page, n_cols, LANE), F32),
                        pltpu.SemaphoreType.DMA((2,))])
    out = pl.pallas_call(
        functools.partial(_pool_pages_kernel, pages_per_step=pages_per_step, page_base=page_base),
        out_shape=jax.ShapeDtypeStruct((n_steps, pages_per_step, per, n_cols, LANE), F32),
        grid_spec=grid_spec,
        compiler_params=_params(("arbitrary",)),
        name="pool_pages",
    )(page_flat, a3, cache4)
    return out.reshape(bs, n_pages * per, n_cols * LANE)


def _cmp_proj_kernel(p_ref, w_ref, kc_ref, vc_ref):
    pooled = p_ref[0]
    n = pooled.shape[0]
    n_pad = kc_ref.shape[2]
    for c in range(4):
        r = _nn(pooled[:, c * LANE:(c + 1) * LANE].astype(BF16), w_ref[c]).astype(BF16)
        dst = kc_ref if c < 2 else vc_ref
        if n_pad > n:
            dst[0, c % 2] = jnp.zeros((n_pad, LANE), BF16)
        dst[0, c % 2, 0:n, :] = r


def _cmp_proj(pooled, w4):
    b, n, _ = pooled.shape
    n_pad = -(-n // LANE) * LANE
    spec = pl.BlockSpec((1, NSA_KV, n_pad, LANE), lambda bi: (bi, 0, 0, 0))
    return pl.pallas_call(
        _cmp_proj_kernel,
        out_shape=[jax.ShapeDtypeStruct((b, NSA_KV, n_pad, LANE), BF16)] * 2,
        grid=(b,),
        in_specs=[pl.BlockSpec((1, n, 512), lambda bi: (bi, 0, 0)),
                  pl.BlockSpec((4, LANE, LANE), lambda bi: (0, 0, 0))],
        out_specs=[spec, spec],
        compiler_params=_params(("parallel",)),
        name="cmp_proj",
    )(pooled, w4)


def _lane_parts(x):
    return [x[:, j * LANE:(j + 1) * LANE] for j in range(x.shape[1] // LANE)]


def _nsa_prompt_kernel(q_ref, ng_ref, bg_ref, ksel_ref, vsel_ref, kwin_ref, vwin_ref, kc_ref, vc_ref,
                       c_ref, s1_ref, s2_ref, o_ref, qc_scr, qr_scr, s_scr, p_scr, a_scr, m_scr, l_scr, acc_scr, oc_scr,
                       *, t_len, tq, tc, tw, n_sel, top):
    i = pl.program_id(2)
    hq = NSA_HPG
    q = q_ref[0]
    c, s1, s2 = c_ref[...], s1_ref[...], s2_ref[...]
    for h in range(hq):
        qh = q[:, h * LANE:(h + 1) * LANE]
        qc_scr[pl.ds(h * tq, tq), :] = (qh * SCALE_LOG2).astype(BF16)
        qr_scr[pl.ds(h * tq, tq), :] = (_rope(qh, c, s1, s2) * SCALE_LOG2).astype(BF16)

    kc = kc_ref[0, 0]
    npad = kc.shape[0]
    s_scr[:, 0:npad] = _nt(qc_scr[...], kc)
    tpos = i * tq + lax.broadcasted_iota(jnp.int32, (tq, npad), 0)
    ncol = lax.broadcasted_iota(jnp.int32, (tq, npad), 1)
    cmask = (ncol + 1) * CMP_BLOCK <= tpos + 1
    imp = jnp.zeros((tq, npad), F32)
    for h in range(hq):
        r = pl.ds(h * tq, tq)
        s = jnp.where(cmask, s_scr[r, 0:npad], MASK_NEG)
        e = jnp.where(cmask, jnp.exp2(s - jnp.max(s, axis=-1, keepdims=True)), 0.0)
        p = e / jnp.maximum(jnp.sum(e, axis=-1, keepdims=True), 1e-30)
        p_scr[r, 0:npad] = p.astype(BF16)
        imp = imp + p
    oc_scr[...] = _nn(p_scr[:, 0:npad], vc_ref[0, 0])

    rows = min(npad, -(-n_sel // 8) * 8)
    imp_t = imp.T[0:rows]
    blk = lax.broadcasted_iota(jnp.int32, (rows, tq), 0)
    cur = (i * tq + lax.broadcasted_iota(jnp.int32, (rows, tq), 1)) // SEL_BLOCK
    imp_t = jnp.where((blk == 0) | (blk == cur) | (blk == cur - 1), FORCE, imp_t)
    imp_t = jnp.where(blk > cur, -1.0, imp_t)
    imp_t = jnp.where(blk >= n_sel, -2.0, imp_t)
    rank = jnp.zeros((rows, tq), F32)
    for j in range(n_sel):
        a = imp_t[j:j + 1, :]
        ahead = (a > imp_t) | ((a == imp_t) & (blk > j))
        rank = rank + jnp.where(ahead, 1.0, 0.0)
    sel_t = jnp.where((rank < top) & (blk < n_sel), 1.0, 0.0)
    if npad > rows:
        sel_t = jnp.concatenate([sel_t, jnp.zeros((npad - rows, tq), F32)], axis=0)
    sel = sel_t.T.astype(BF16)

    m_scr[...] = jnp.full(m_scr.shape, MASK_NEG, F32)
    l_scr[...] = jnp.zeros(l_scr.shape, F32)
    acc_scr[...] = jnp.zeros(acc_scr.shape, F32)
    n_chunks = (i * tq + tq + tc - 1) // tc

    def chunk(ci, carry):
        k0 = pl.multiple_of(ci * tc, tc)
        k = ksel_ref[0, pl.ds(k0, tc), :]
        v = vsel_ref[0, pl.ds(k0, tc), :]
        jb = lax.broadcasted_iota(jnp.int32, (npad, tc), 0)
        kb = (k0 + lax.broadcasted_iota(jnp.int32, (npad, tc), 1)) // SEL_BLOCK
        expand = jnp.where(jb == kb, 1.0, 0.0).astype(BF16)
        chosen = _nn(sel, expand)
        kp = k0 + lax.broadcasted_iota(jnp.int32, (tq, tc), 1)
        tp = i * tq + lax.broadcasted_iota(jnp.int32, (tq, tc), 0)
        bias = jnp.where((chosen > 0.5) & (kp <= tp), 0.0, MASK_NEG)
        s_scr[:, 0:tc] = _nt(qr_scr[...], k)
        for h in range(hq):
            r = pl.ds(h * tq, tq)
            parts = _lane_parts(s_scr[r, 0:tc] + bias)
            m_prev = m_scr[r, :]
            m_new = jnp.maximum(m_prev, jnp.max(functools.reduce(jnp.maximum, parts), axis=-1, keepdims=True))
            alpha = jnp.exp2(m_prev - m_new)
            pes = [jnp.exp2(x - m_new) for x in parts]
            l_scr[r, :] = alpha * l_scr[r, :] + functools.reduce(jnp.add, pes)
            p_scr[r, 0:tc] = jnp.concatenate(pes, axis=1).astype(BF16)
            a_scr[r, :] = alpha
            m_scr[r, :] = m_new
        acc_scr[...] = a_scr[...] * acc_scr[...] + _nn(p_scr[:, 0:tc], v)
        return carry

    lax.fori_loop(0, n_chunks, chunk, 0)

    w0 = pl.multiple_of(jnp.clip(i * tq + tq - tw, 0, t_len - tw), LANE)
    kp = w0 + lax.broadcasted_iota(jnp.int32, (tq, tw), 1)
    tp = i * tq + lax.broadcasted_iota(jnp.int32, (tq, tw), 0)
    wbias = jnp.where((kp <= tp) & (kp > tp - WINDOW), 0.0, MASK_NEG)
    s_scr[:, 0:tw] = _nt(qr_scr[...], kwin_ref[0, pl.ds(w0, tw), :])
    for h in range(hq):
        r = pl.ds(h * tq, tq)
        parts = _lane_parts(s_scr[r, 0:tw] + wbias)
        m = jnp.max(functools.reduce(jnp.maximum, parts), axis=-1, keepdims=True)
        ews = [jnp.exp2(x - m) for x in parts]
        a_scr[r, :] = jnp.broadcast_to(jnp.sum(functools.reduce(jnp.add, ews), axis=-1, keepdims=True), (tq, LANE))
        p_scr[r, 0:tw] = jnp.concatenate(ews, axis=1).astype(BF16)
    o_w = _nn(p_scr[:, 0:tw], vwin_ref[0, pl.ds(w0, tw), :])

    gate = jax.nn.sigmoid(bg_ref[0])
    ng = ng_ref[0]
    for h in range(hq):
        r = slice(h * tq, (h + 1) * tq)
        o_s = acc_scr[r, :] / jnp.sum(l_scr[r, :], axis=-1, keepdims=True)
        o = (gate[:, 3 * h:3 * h + 1] * oc_scr[r, :] + gate[:, 3 * h + 1:3 * h + 2] * o_s
             + gate[:, 3 * h + 2:3 * h + 3] * (o_w[r] / a_scr[r, :]))
        o_ref[0, :, h * LANE:(h + 1) * LANE] = (_silu(ng[:, h * LANE:(h + 1) * LANE]) * o).astype(o_ref.dtype)


def _nsa_prompt(z3, bg3, kvb, kc, vc, tabs, *, tq=128, tc=512):
    b, t, _ = z3.shape
    nq = t // tq
    tc = min(tc, t)
    tw = min(WINDOW + tq, t)
    n_sel = t // SEL_BLOCK
    top = min(TOP_N, n_sel)
    npad = kc.shape[2]
    gw = NSA_HPG * LANE
    rows = NSA_HPG * tq
    wide = max(tc, tw, npad)
    kvspec = lambda k: pl.BlockSpec((1, t, LANE), lambda bi, gi, qi, k=k: (bi, 0, k + gi))
    cspec = pl.BlockSpec((1, 1, npad, LANE), lambda bi, gi, qi: (bi, gi, 0, 0))
    tspec = pl.BlockSpec((tq, LANE), lambda bi, gi, qi: (qi, 0))
    kern = functools.partial(_nsa_prompt_kernel, t_len=t, tq=tq, tc=tc, tw=tw, n_sel=n_sel, top=top)
    return pl.pallas_call(
        kern,
        out_shape=jax.ShapeDtypeStruct((b, t, NSA_HEADS * LANE), BF16),
        grid=(b, NSA_KV, nq),
        in_specs=[pl.BlockSpec((1, tq, gw), lambda bi, gi, qi: (bi, qi, Q_OFF // gw + gi)),
                  pl.BlockSpec((1, tq, gw), lambda bi, gi, qi: (bi, qi, NG_OFF // gw + gi)),
                  pl.BlockSpec((1, tq, LANE), lambda bi, gi, qi: (bi, qi, gi)),
                  kvspec(0), kvspec(2), kvspec(4), kvspec(6), cspec, cspec, tspec, tspec, tspec],
        out_specs=pl.BlockSpec((1, tq, gw), lambda bi, gi, qi: (bi, qi, gi)),
        scratch_shapes=([pltpu.VMEM((rows, LANE), BF16)] * 2
                        + [pltpu.VMEM((rows, wide), F32), pltpu.VMEM((rows, wide), BF16)]
                        + [pltpu.VMEM((rows, LANE), F32)] * 5),
        compiler_params=_params(("parallel", "parallel", "arbitrary")),
        name="nsa_prompt",
    )(z3, z3, bg3, kvb, kvb, kvb, kvb, kc, vc, *tabs)


def _conv_kernel(h_ref, b_ref, c_ref, g_ref, prev_ref, w_ref, y_ref, st_ref, up_scr, *, t_real):
    u = c_ref[0] * h_ref[0]
    t = u.shape[0]
    up_scr[pl.ds(8 - (CONV_W - 1), CONV_W - 1), :] = prev_ref[0]
    up_scr[pl.ds(8, t), :] = u
    w = w_ref[...]
    y = w[0:1, :] * up_scr[pl.ds(6, t), :]
    y = y + w[1:2, :] * up_scr[pl.ds(7, t), :]
    y = y + w[2:3, :] * u
    y = b_ref[0] * y
    y_ref[0] = (_silu(g_ref[0]) * y).astype(y_ref.dtype)
    st_ref[0] = up_scr[pl.ds(6 + t_real, CONV_W - 1), :]


def _conv_mixer(z3, prev, w_conv, *, t_real):
    b, t, _ = z3.shape
    nc = CONV_DIM // LANE
    zspec = lambda off: pl.BlockSpec((1, t, LANE), lambda bi, ci, off=off: (bi, 0, off // LANE + ci))
    return pl.pallas_call(
        functools.partial(_conv_kernel, t_real=t_real),
        out_shape=[jax.ShapeDtypeStruct((b, t, CONV_DIM), BF16),
                   jax.ShapeDtypeStruct((b, CONV_W - 1, CONV_DIM), F32)],
        grid=(b, nc),
        in_specs=[zspec(C_H), zspec(C_B), zspec(C_C), zspec(C_G),
                  pl.BlockSpec((1, CONV_W - 1, LANE), lambda bi, ci: (bi, 0, ci)),
                  pl.BlockSpec((CONV_W, LANE), lambda bi, ci: (0, ci))],
        out_specs=[pl.BlockSpec((1, t, LANE), lambda bi, ci: (bi, 0, ci)),
                   pl.BlockSpec((1, CONV_W - 1, LANE), lambda bi, ci: (bi, 0, ci))],
        scratch_shapes=[pltpu.VMEM((t + 8, LANE), F32)],
        compiler_params=_params(("parallel", "parallel")),
        name="conv_mixer",
    )(z3, z3, z3, z3, prev, w_conv)


def _mem_attn_kernel(q_ref, mg_ref, kv_ref, o_ref):
    q = q_ref[0]
    mg = mg_ref[0]
    half = MEM_HEADS * LANE
    for h in range(MEM_HEADS):
        lo, hi = h * LANE, (h + 1) * LANE
        k = kv_ref[0, :, lo:hi].astype(BF16)
        v = kv_ref[0, :, half + lo:half + hi].astype(BF16)
        s = _nt((q[:, lo:hi] * ATTN_SCALE).astype(BF16), k)
        e = jnp.exp(s - jnp.max(s, axis=-1, keepdims=True))
        o = _nn(e.astype(BF16), v) / jnp.sum(e, axis=-1, keepdims=True)
        o_ref[0, :, lo:hi] = (_silu(mg[:, lo:hi]) * o).astype(o_ref.dtype)


def _mem_attn(z3, mkv, *, tq):
    b, t, _ = z3.shape
    nm, wkv = mkv.shape[1], mkv.shape[2]
    wq = MEM_HEADS * LANE
    return pl.pallas_call(
        _mem_attn_kernel,
        out_shape=jax.ShapeDtypeStruct((b, t, wq), BF16),
        grid=(b, t // tq),
        in_specs=[pl.BlockSpec((1, tq, wq), lambda bi, ti: (bi, ti, MQ_OFF // wq)),
                  pl.BlockSpec((1, tq, wq), lambda bi, ti: (bi, ti, MG_OFF // wq)),
                  pl.BlockSpec((1, nm, wkv), lambda bi, ti: (bi, 0, 0))],
        out_specs=pl.BlockSpec((1, tq, wq), lambda bi, ti: (bi, ti, 0)),
        compiler_params=_params(("parallel", "parallel")),
        name="mem_attn",
    )(z3, z3, mkv)


def _out_proj_kernel(x_ref, ya_ref, yb_ref, ym_ref, w_ref, fg_ref, o_ref, *, final):
    a, bw = CONV_DIM, CONV_DIM + NSA_HEADS * LANE
    acc = _nn(ya_ref[...], w_ref[0:a, :])
    acc = acc + _nn(yb_ref[...], w_ref[a:bw, :])
    acc = acc + _nn(ym_ref[...], w_ref[bw:, :])
    r = x_ref[...] + acc
    if final:
        r = r * lax.rsqrt(jnp.mean(r * r, axis=-1, keepdims=True) + NORM_EPS) * fg_ref[...]
    o_ref[...] = r


def _out_proj(x, ya, yb, ym, w, fg, *, tm, final):
    m, d = x.shape
    row = lambda width: pl.BlockSpec((tm, width), lambda i: (i, 0))
    return pl.pallas_call(
        functools.partial(_out_proj_kernel, final=final),
        out_shape=jax.ShapeDtypeStruct((m, d), F32),
        grid=(m // tm,),
        in_specs=[row(d), row(ya.shape[1]), row(yb.shape[1]), row(ym.shape[1]),
                  pl.BlockSpec(w.shape, lambda i: (0, 0)),
                  pl.BlockSpec((1, d), lambda i: (0, 0))],
        out_specs=row(d),
        compiler_params=_params(("parallel",)),
        name="out_proj",
    )(x, ya, yb, ym, w, fg.reshape(1, d))


def _sample_pre_kernel(q_ref, ng_ref, bg_ref, kv0_ref, kv1_ref, kv2_ref, c_ref, s1_ref, s2_ref, kc_ref, vc_ref,
                       kvn_ref, wn_ref, qr_ref, oc_ref, ngo_ref, gate_ref, idx_ref, *, past, n_top):
    tp = SAMPLE_T_PAD
    hq = NSA_HPG
    c, s1, s2 = c_ref[...], s1_ref[...], s2_ref[...]
    kv0, kv1, kv2 = kv0_ref[0], kv1_ref[0], kv2_ref[0]
    kvn_ref[0, :, 0:512] = kv0
    kvn_ref[0, :, 768:1024] = kv1[:, 256:512]
    wn_ref[0, :, 256:512] = kv2[:, 256:512]
    for g in range(NSA_KV):
        lo, hi = g * LANE, (g + 1) * LANE
        kvn_ref[0, :, 512 + lo:512 + hi] = _rope(kv1[:, lo:hi], c, s1, s2)
        wn_ref[0, :, lo:hi] = _rope(kv2[:, lo:hi], c, s1, s2)

    q = q_ref[0]
    ng = ng_ref[0]
    gates = jax.nn.sigmoid(bg_ref[0])
    for g in range(NSA_KV):
        qc_l, qr_l = [], []
        for h in range(hq):
            lo = (g * hq + h) * LANE
            qh = q[:, lo:lo + LANE]
            qc_l.append(qh * ATTN_SCALE)
            qr_l.append(_rope(qh, c, s1, s2) * ATTN_SCALE)
            ngo_ref[0, g, h * tp:(h + 1) * tp, :] = ng[:, lo:lo + LANE]
            for br in range(N_BRANCH):
                col = g * LANE + h * N_BRANCH + br
                gate_ref[0, g, br, h * tp:(h + 1) * tp, :] = jnp.broadcast_to(gates[:, col:col + 1], (tp, LANE))
        qc = jnp.concatenate(qc_l, axis=0)
        qr_ref[0, g] = jnp.concatenate(qr_l, axis=0)

        kc = kc_ref[0, g]
        npad = kc.shape[0]
        s = _nt(qc.astype(BF16), kc)
        trow = lax.broadcasted_iota(jnp.int32, (hq * tp, npad), 0) % tp
        ncol = lax.broadcasted_iota(jnp.int32, (hq * tp, npad), 1)
        cmask = (ncol + 1) * CMP_BLOCK <= past + trow + 1
        s = jnp.where(cmask, s, MASK_NEG)
        e = jnp.where(cmask, jnp.exp(s - jnp.max(s, axis=-1, keepdims=True)), 0.0)
        p = e / jnp.maximum(jnp.sum(e, axis=-1, keepdims=True), 1e-30)
        oc_ref[0, g] = _nn(p.astype(BF16), vc_ref[0, g])
        imp = jnp.sum(p.reshape(hq, tp, npad), axis=0)

        blk = lax.broadcasted_iota(jnp.int32, (tp, npad), 1)
        cur = (past + lax.broadcasted_iota(jnp.int32, (tp, npad), 0)) // SEL_BLOCK
        val = jnp.where((blk == 0) | (blk == cur) | (blk == cur - 1), FORCE, imp)
        val = jnp.where(blk > cur, -1.0, val)
        val = jnp.where(blk >= past // SEL_BLOCK, -2.0, val)
        lane = lax.broadcasted_iota(jnp.int32, (tp, LANE), 1)
        idx = jnp.zeros((tp, LANE), jnp.int32)
        for r in range(n_top):
            best = jnp.max(val, axis=-1, keepdims=True)
            j = jnp.min(jnp.where(val == best, blk, npad), axis=-1, keepdims=True)
            idx = jnp.where(lane == r, j, idx)
            val = jnp.where(blk == j, -3e38, val)
        idx_ref[0, g] = idx


def _sample_pre(z3, bg3, tabs, kc, vc, *, past, n_top):
    bs, tp, _ = z3.shape
    npad = kc.shape[2]
    qw = NSA_HEADS * LANE
    kvblk = KV_OFF // 512
    zspec = lambda k: pl.BlockSpec((1, tp, 512), lambda bi, k=k: (bi, 0, kvblk + k))
    tspec = pl.BlockSpec((tp, LANE), lambda bi: (0, 0))
    cspec = pl.BlockSpec((1, NSA_KV, npad, LANE), lambda bi: (bi, 0, 0, 0))
    rows = NSA_HPG * tp
    gspec = pl.BlockSpec((1, NSA_KV, rows, LANE), lambda bi: (bi, 0, 0, 0))
    gshape = jax.ShapeDtypeStruct((bs, NSA_KV, rows, LANE), F32)
    return pl.pallas_call(
        functools.partial(_sample_pre_kernel, past=past, n_top=n_top),
        out_shape=[jax.ShapeDtypeStruct((bs, tp, 1024), F32),
                   jax.ShapeDtypeStruct((bs, tp, 512), F32),
                   gshape, gshape, gshape,
                   jax.ShapeDtypeStruct((bs, NSA_KV, N_BRANCH, rows, LANE), F32),
                   jax.ShapeDtypeStruct((bs, NSA_KV, tp, LANE), jnp.int32)],
        grid=(bs,),
        in_specs=[pl.BlockSpec((1, tp, qw), lambda bi: (bi, 0, Q_OFF // qw)),
                  pl.BlockSpec((1, tp, qw), lambda bi: (bi, 0, NG_OFF // qw)),
                  pl.BlockSpec((1, tp, NSA_KV * LANE), lambda bi: (bi, 0, 0)),
                  zspec(0), zspec(1), zspec(2), tspec, tspec, tspec, cspec, cspec],
        out_specs=[pl.BlockSpec((1, tp, 1024), lambda bi: (bi, 0, 0)),
                   pl.BlockSpec((1, tp, 512), lambda bi: (bi, 0, 0)),
                   gspec, gspec, gspec,
                   pl.BlockSpec((1, NSA_KV, N_BRANCH, rows, LANE), lambda bi: (bi, 0, 0, 0, 0)),
                   pl.BlockSpec((1, NSA_KV, tp, LANE), lambda bi: (bi, 0, 0, 0))],
        compiler_params=_params(("parallel",)),
        name="sample_pre",
    )(z3, z3, bg3, z3, z3, z3, *tabs, kc, vc)


def _sample_attn_kernel(idx_sm, pt_sm, qr_ref, oc_ref, ng_ref, gate_ref, ksn_ref, vsn_ref,
                        kwc_ref, vwc_ref, kwn_ref, vwn_ref, cache_ref, o_ref,
                        kbuf, vbuf, kw_scr, vw_scr, sem, *, ts, n_top, n_pages, page_base, per_page, wb):
    tp = SAMPLE_T_PAD
    hq = NSA_HPG
    b = pl.program_id(0)
    g = pl.program_id(1)
    n_gath = n_top * SEL_BLOCK
    ks_rows = kbuf.shape[1]

    def gather_copies(t, r):
        blk = idx_sm[((b * NSA_KV + g) * ts + t) * n_top + r]
        page = pt_sm[b * n_pages + blk // per_page] + page_base
        row0 = (blk % per_page) * SEL_BLOCK
        src_k = cache_ref.at[page, pl.ds(row0, SEL_BLOCK), 2 * NSA_KV + g]
        src_v = cache_ref.at[page, pl.ds(row0, SEL_BLOCK), 3 * NSA_KV + g]
        dst = pl.ds(r * SEL_BLOCK, SEL_BLOCK)
        return (pltpu.make_async_copy(src_k, kbuf.at[t, dst], sem.at[0]),
                pltpu.make_async_copy(src_v, vbuf.at[t, dst], sem.at[1]))

    for t in range(ts):
        for r in range(n_top):
            ck, cv = gather_copies(t, r)
            ck.start()
            cv.start()

    qr = qr_ref[0, 0].astype(BF16)
    trow = lax.broadcasted_iota(jnp.int32, (hq * tp, 1), 0) % tp

    ww = kw_scr.shape[0]
    kw_scr[pl.ds(0, wb), :] = kwc_ref[0]
    vw_scr[pl.ds(0, wb), :] = vwc_ref[0]
    kw_scr[pl.ds(wb, tp), :] = kwn_ref[0]
    vw_scr[pl.ds(wb, tp), :] = vwn_ref[0]
    kw_scr[pl.ds(wb + tp, ww - wb - tp), :] = jnp.zeros((ww - wb - tp, LANE), F32)
    vw_scr[pl.ds(wb + tp, ww - wb - tp), :] = jnp.zeros((ww - wb - tp, LANE), F32)
    sw = _nt(qr, kw_scr[...].astype(BF16))
    jw = lax.broadcasted_iota(jnp.int32, (hq * tp, ww), 1)
    rel = jw - wb
    okw = (rel <= trow) & (rel > trow - WINDOW) & (jw < wb + ts)
    sw = jnp.where(okw, sw, MASK_NEG)
    ew = jnp.where(okw, jnp.exp(sw - jnp.max(sw, axis=-1, keepdims=True)), 0.0)
    o_w = _nn(ew.astype(BF16), vw_scr[...].astype(BF16)) / jnp.sum(ew, axis=-1, keepdims=True)

    for t in range(ts):
        for r in range(n_top):
            ck, cv = gather_copies(t, r)
            ck.wait()
            cv.wait()

    js = lax.broadcasted_iota(jnp.int32, (hq * tp, ks_rows), 1)
    o_s = jnp.zeros((hq * tp, LANE), F32)
    for t in range(ts):
        kbuf[t, pl.ds(n_gath, tp), :] = ksn_ref[0]
        vbuf[t, pl.ds(n_gath, tp), :] = vsn_ref[0]
        kbuf[t, pl.ds(n_gath + tp, ks_rows - n_gath - tp), :] = jnp.zeros((ks_rows - n_gath - tp, LANE), F32)
        vbuf[t, pl.ds(n_gath + tp, ks_rows - n_gath - tp), :] = jnp.zeros((ks_rows - n_gath - tp, LANE), F32)
        ss = _nt(qr, kbuf[t].astype(BF16))
        oks = (js < n_gath) | ((js - n_gath <= t) & (js < n_gath + ts))
        ss = jnp.where(oks, ss, MASK_NEG)
        es = jnp.where(oks, jnp.exp(ss - jnp.max(ss, axis=-1, keepdims=True)), 0.0)
        ot = _nn(es.astype(BF16), vbuf[t].astype(BF16)) / jnp.sum(es, axis=-1, keepdims=True)
        o_s = jnp.where(trow == t, ot, o_s)

    o = gate_ref[0, 0, 0] * oc_ref[0, 0] + gate_ref[0, 0, 1] * o_s + gate_ref[0, 0, 2] * o_w
    y = _silu(ng_ref[0, 0]) * o
    for h in range(hq):
        o_ref[0, :, h * LANE:(h + 1) * LANE] = y[h * tp:(h + 1) * tp].astype(o_ref.dtype)


def _sample_attn(idx_flat, page_flat, qr, oc, ngo, gates, kvn, cache_win3, wn, cache4, *,
                 ts, n_top, n_pages, page_base, win_base):
    bs = qr.shape[0]
    tp = SAMPLE_T_PAD
    rows = NSA_HPG * tp
    wb = cache_win3.shape[1]
    per_page = cache4.shape[1] // SEL_BLOCK
    ks_rows = -(-(n_top * SEL_BLOCK + tp) // LANE) * LANE
    ww = -(-(wb + tp) // LANE) * LANE
    gspec = pl.BlockSpec((1, 1, rows, LANE), lambda bi, gi, *_: (bi, gi, 0, 0))
    newspec = lambda k: pl.BlockSpec((1, tp, LANE), lambda bi, gi, *_, k=k: (bi, 0, k + gi))
    wcspec = lambda k: pl.BlockSpec((1, wb, LANE), lambda bi, gi, *_, k=k: (bi + win_base, 0, k + gi))
    grid_spec = pltpu.PrefetchScalarGridSpec(
        num_scalar_prefetch=2, grid=(bs, NSA_KV),
        in_specs=[gspec, gspec, gspec,
                  pl.BlockSpec((1, 1, N_BRANCH, rows, LANE), lambda bi, gi, *_: (bi, gi, 0, 0, 0)),
                  newspec(2 * NSA_KV), newspec(3 * NSA_KV),
                  wcspec(0), wcspec(NSA_KV), newspec(0), newspec(NSA_KV),
                  pl.BlockSpec(memory_space=pl.ANY)],
        out_specs=pl.BlockSpec((1, tp, NSA_HPG * LANE), lambda bi, gi, *_: (bi, 0, gi)),
        scratch_shapes=[pltpu.VMEM((ts, ks_rows, LANE), F32), pltpu.VMEM((ts, ks_rows, LANE), F32),
                        pltpu.VMEM((ww, LANE), F32), pltpu.VMEM((ww, LANE), F32),
                        pltpu.SemaphoreType.DMA((2,))])
    kern = functools.partial(_sample_attn_kernel, ts=ts, n_top=n_top, n_pages=n_pages,
                             page_base=page_base, per_page=per_page, wb=wb)
    return pl.pallas_call(
        kern,
        out_shape=jax.ShapeDtypeStruct((bs, tp, NSA_HEADS * LANE), BF16),
        grid_spec=grid_spec,
        compiler_params=_params(("arbitrary", "arbitrary")),
        name="sample_attn",
    )(idx_flat, page_flat, qr, oc, ngo, gates, kvn, kvn, cache_win3, cache_win3, wn, wn, cache4)


def _rope_tables(pos, rows):
    freqs = jnp.power(ROPE_THETA, -jnp.arange(ROPE_HALF, dtype=F32) * (2.0 / ROPE_DIM))
    ang = pos.astype(F32)[:, None] * freqs[None, :]
    cos, sin = jnp.cos(ang), jnp.sin(ang)
    n = pos.shape[0]
    z16 = jnp.zeros((n, ROPE_HALF), F32)
    rest = LANE - ROPE_DIM
    c = jnp.concatenate([cos, cos, jnp.ones((n, rest), F32)], axis=1)
    s1 = jnp.concatenate([z16, sin, jnp.zeros((n, rest), F32)], axis=1)
    s2 = jnp.concatenate([-sin, z16, jnp.zeros((n, rest), F32)], axis=1)
    pad = lambda a: jnp.pad(a, ((0, rows - n), (0, 0)))
    return pad(c), pad(s1), pad(s2)


def _layer_weights(norm_g, w_in, w_conv, a_cmp, w_cmp, w_out):
    w_main = jnp.concatenate([w_in[:, :BG_SRC].astype(BF16), w_in[:, BG_SRC + BG_N:].astype(BF16)], axis=1)
    per_group = NSA_HPG * N_BRANCH
    wbg = jnp.concatenate(
        [jnp.pad(w_in[:, BG_SRC + g * per_group:BG_SRC + (g + 1) * per_group], ((0, 0), (0, LANE - per_group)))
         for g in range(NSA_KV)], axis=1).astype(BF16)
    a4 = jnp.concatenate([a_cmp[0], a_cmp[0], a_cmp[1], a_cmp[1]], axis=1)
    a3 = jnp.stack([a_cmp[0], a_cmp[0], a_cmp[1], a_cmp[1]], axis=1)
    w4 = jnp.stack([w_cmp[0], w_cmp[0], w_cmp[1], w_cmp[1]]).astype(BF16)
    return dict(norm_g=norm_g, w_main=w_main, wbg=wbg, w_conv=w_conv, a4=a4, a3=a3, w4=w4,
                w_out=w_out.astype(BF16))


def _prompt_layer(xp, mem_prompt, mem_norm_g, w_mem, lw, final_g, final):
    b, t, d = xp.shape
    m = b * t
    z, bg = _norm_matmul(xp.reshape(m, d), lw["norm_g"], lw["w_main"], lw["wbg"], tm=min(1024, m), tn=512)
    z3 = z.reshape(b, t, Z_WIDTH)
    bg3 = bg.reshape(b, t, NSA_KV * LANE)
    nm = mem_prompt.shape[1]
    mkv = _norm_matmul(mem_prompt.reshape(b * nm, d), mem_norm_g, w_mem, tm=min(512, b * nm), tn=512)
    mkv3 = mkv.reshape(b, nm, 2 * MEM_HEADS * LANE)
    tabs = _rope_tables(jnp.arange(t, dtype=jnp.int32), t)
    wb = min(WINDOW, t)
    kvn, kvb, win_full = _kv_export(z3, tabs, tr=min(512, t))
    kc, vc = _cmp_proj(_pool_prompt(z3, lw["a4"]), lw["w4"])
    yb = _nsa_prompt(z3, bg3, kvb, kc, vc, tabs, tq=min(256, t))
    conv0 = jnp.zeros((b, CONV_W - 1, CONV_DIM), F32)
    ya, conv_new = _conv_mixer(z3, conv0, lw["w_conv"], t_real=t)
    ym = _mem_attn(z3, mkv3, tq=min(512, t))
    out = _out_proj(xp.reshape(m, d), ya.reshape(m, -1), yb.reshape(m, -1), ym.reshape(m, -1),
                    lw["w_out"], final_g, tm=min(256, m), final=final)
    kv_new = kvn.reshape(b, t, 4, NSA_KV, HEAD_DIM)
    win_new = win_full[:, t - wb:].reshape(b, wb, 2, NSA_KV, HEAD_DIM)
    mem_kv = mkv.reshape(b, nm, 2, MEM_HEADS, HEAD_DIM)
    return out.reshape(b, t, d), kv_new, win_new, conv_new, mem_kv


def _sample_layer(xs_p, ts, layer, cache_kv, cache_win, state_conv, cache_mem, page_table, lw, final_g, final):
    bs, tp, d = xs_p.shape
    depth, pool, page = cache_kv.shape[0], cache_kv.shape[1], cache_kv.shape[2]
    n_pages = page_table.shape[1]
    past = n_pages * page
    assert past % SEL_BLOCK == 0 and ts <= SEL_BLOCK and ts <= tp
    n_past = past // SEL_BLOCK
    n_top = min(TOP_N, n_past + 1) - 1
    m = bs * tp
    z, bg = _norm_matmul(xs_p.reshape(m, d), lw["norm_g"], lw["w_main"], lw["wbg"], tm=m, tn=512)
    z3 = z.reshape(bs, tp, Z_WIDTH)
    bg3 = bg.reshape(bs, tp, NSA_KV * LANE)
    tabs = _rope_tables(past + jnp.arange(tp, dtype=jnp.int32), tp)
    page_flat = page_table.reshape(-1).astype(jnp.int32)
    cache4 = cache_kv.reshape(depth * pool, page, 4 * NSA_KV, HEAD_DIM)
    pooled = _pool_pages(cache4, page_flat, lw["a3"], bs=bs, n_pages=n_pages, page_base=layer * pool)
    kc, vc = _cmp_proj(pooled, lw["w4"])
    kvn, wn, qr, oc, ngo, gates, idx = _sample_pre(z3, bg3, tabs, kc, vc, past=past, n_top=n_top)
    idx_flat = idx[:, :, :ts, :n_top].reshape(-1)
    wbuf = cache_win.shape[2]
    cache_win3 = cache_win.reshape(depth * bs, wbuf, 2 * NSA_KV * HEAD_DIM)
    yb = _sample_attn(idx_flat, page_flat, qr, oc, ngo, gates, kvn, cache_win3, wn, cache4,
                      ts=ts, n_top=n_top, n_pages=n_pages, page_base=layer * pool, win_base=layer * bs)
    ya, conv_new = _conv_mixer(z3, state_conv[layer], lw["w_conv"], t_real=ts)
    nm = cache_mem.shape[2]
    mem3 = cache_mem.reshape(depth * bs, nm, 2 * MEM_HEADS * HEAD_DIM)[layer * bs:(layer + 1) * bs]
    ym = _mem_attn(z3, mem3, tq=tp)
    out = _out_proj(xs_p.reshape(m, d), ya.reshape(m, -1), yb.reshape(m, -1), ym.reshape(m, -1),
                    lw["w_out"], final_g, tm=m, final=final)
    kv_new = kvn[:, :ts].reshape(bs, ts, 4, NSA_KV, HEAD_DIM)
    win_rows = wn[:, :ts].reshape(bs, ts, 2, NSA_KV, HEAD_DIM)
    win_state = jnp.concatenate([cache_win[layer], win_rows], axis=1)[:, ts:]
    return out.reshape(bs, tp, d), kv_new, win_state, conv_new


def kernel(x_prompt, x_sample, cache_kv, cache_win, state_conv, cache_mem, page_table, mem_prompt,
           norm_g, w_in, w_conv, a_cmp, w_cmp, mem_norm_g, w_mem_kv, w_out, final_g):
    depth = w_in.shape[0]
    ts = x_sample.shape[1]
    xp = x_prompt
    xs = jnp.pad(x_sample, ((0, 0), (0, SAMPLE_T_PAD - ts), (0, 0)))
    kv_p, win_p, conv_p, mem_p, kv_s, win_s, conv_s = [], [], [], [], [], [], []
    for l in range(depth):
        lw = _layer_weights(norm_g[l], w_in[l], w_conv[l], a_cmp[l], w_cmp[l], w_out[l])
        final = l == depth - 1
        xp, kvn, winn, convn, mkv = _prompt_layer(xp, mem_prompt, mem_norm_g[l], w_mem_kv[l].astype(BF16),
                                                  lw, final_g, final)
        kv_p.append(kvn)
        win_p.append(winn)
        conv_p.append(convn)
        mem_p.append(mkv)
        xs, kvn, winn, convn = _sample_layer(xs, ts, l, cache_kv, cache_win, state_conv, cache_mem,
                                             page_table, lw, final_g, final)
        kv_s.append(kvn)
        win_s.append(winn)
        conv_s.append(convn)
    return (xp, xs[:, :ts], jnp.stack(kv_p), jnp.stack(win_p), jnp.stack(conv_p), jnp.stack(mem_p),
            jnp.stack(kv_s), jnp.stack(win_s), jnp.stack(conv_s))
```

```python
import functools

import jax
import jax.numpy as jnp
from jax import lax
from jax.experimental import pallas as pl
from jax.experimental.pallas import tpu as pltpu

F32 = jnp.float32
BF16 = jnp.bfloat16

HEAD_DIM = 128
CONV_DIM = 512
CONV_W = 3
NSA_HEADS = 8
NSA_KV = 2
NSA_HPG = NSA_HEADS // NSA_KV
MEM_HEADS = 4
N_BRANCH = 3
ROPE_DIM = HEAD_DIM // 4
ROPE_HALF = ROPE_DIM // 2
ROPE_THETA = 500000.0
CMP_BLOCK = 64
SEL_BLOCK = 64
TOP_N = 16
WINDOW = 512
NORM_EPS = 1e-6
MASK_NEG = -1e30
FORCE = 1e9
ATTN_SCALE = HEAD_DIM ** -0.5
SCALE_LOG2 = ATTN_SCALE * 1.4426950408889634

C_H, C_B, C_C, C_G, Q_OFF, NG_OFF = 0, 512, 1024, 1536, 2048, 3072
ZA_WIDTH = 4096
KV_OFF, MQ_OFF, MG_OFF = 0, 1536, 2048
ZB_WIDTH = 2560
BG_SRC = 4096
BG_N = NSA_HEADS * N_BRANCH
LANE = 128
SAMPLE_T_PAD = 8
VMEM_LIMIT = 48 * 1024 * 1024


def _nt(a, b):
    return lax.dot_general(a, b, (((1,), (1,)), ((), ())), preferred_element_type=F32)


def _nn(a, b):
    return jnp.dot(a, b, preferred_element_type=F32)


def _params(sem, vmem=VMEM_LIMIT):
    return pltpu.CompilerParams(dimension_semantics=sem, vmem_limit_bytes=vmem)


def _rope(x, c, s1, s2):
    return x * c + pltpu.roll(x, ROPE_HALF, 1) * s1 + pltpu.roll(x, LANE - ROPE_HALF, 1) * s2


def _silu(x):
    return x * jax.nn.sigmoid(x)


def _norm_matmul_kernel(x_ref, g_ref, w_ref, *rest, with_gate):
    if with_gate:
        wbg_ref, z_ref, bg_ref, h_scr = rest
    else:
        z_ref, h_scr = rest

    @pl.when(pl.program_id(1) == 0)
    def _():
        x = x_ref[...]
        y = x * lax.rsqrt(jnp.mean(x * x, axis=-1, keepdims=True) + NORM_EPS) * g_ref[...]
        h = y.astype(BF16)
        h_scr[...] = h
        if with_gate:
            bg_ref[...] = _nn(h, wbg_ref[...])

    z_ref[...] = _nn(h_scr[...], w_ref[...])


def _norm_matmul(x, g, w, wbg=None, *, tm, tn):
    m, d = x.shape
    n = w.shape[1]
    with_gate = wbg is not None
    in_specs = [pl.BlockSpec((tm, d), lambda i, j: (i, 0)),
                pl.BlockSpec((1, d), lambda i, j: (0, 0)),
                pl.BlockSpec((d, tn), lambda i, j: (0, j))]
    out_shape = [jax.ShapeDtypeStruct((m, n), F32)]
    out_specs = [pl.BlockSpec((tm, tn), lambda i, j: (i, j))]
    args = [x, g.reshape(1, d), w]
    if with_gate:
        nb = wbg.shape[1]
        in_specs.append(pl.BlockSpec((d, nb), lambda i, j: (0, 0)))
        out_shape.append(jax.ShapeDtypeStruct((m, nb), F32))
        out_specs.append(pl.BlockSpec((tm, nb), lambda i, j: (i, 0)))
        args.append(wbg)
    res = pl.pallas_call(
        functools.partial(_norm_matmul_kernel, with_gate=with_gate),
        out_shape=out_shape, grid=(m // tm, n // tn), in_specs=in_specs, out_specs=out_specs,
        scratch_shapes=[pltpu.VMEM((tm, d), BF16)],
        compiler_params=_params(("parallel", "arbitrary")),
        name="norm_matmul_gate" if with_gate else "norm_matmul",
    )(*args)
    return res if with_gate else res[0]


def _kv_export_kernel(kv1_ref, kv2_ref, kv0_ref, c_ref, s1_ref, s2_ref, kvn_ref, kvb_ref, win_ref):
    c, s1, s2 = c_ref[...], s1_ref[...], s2_ref[...]
    kv0 = kv0_ref[0]
    kv1 = kv1_ref[0]
    kv2 = kv2_ref[0]
    tr = kv0.shape[0]
    n_kv, n_w, half = 4 * NSA_KV, 2 * NSA_KV, NSA_KV * LANE
    last = pl.program_id(1) == pl.num_programs(1) - 1
    for g in range(NSA_KV):
        lo, hi = g * LANE, (g + 1) * LANE
        ks = _rope(kv1[:, lo:hi], c, s1, s2)
        kw = _rope(kv2[:, lo:hi], c, s1, s2)
        vs = kv1[:, half + lo:half + hi]
        vw = kv2[:, half + lo:half + hi]
        kvn_ref[pl.ds(g, tr, stride=n_kv), :] = kv0[:, lo:hi]
        kvn_ref[pl.ds(NSA_KV + g, tr, stride=n_kv), :] = kv0[:, half + lo:half + hi]
        kvn_ref[pl.ds(2 * NSA_KV + g, tr, stride=n_kv), :] = ks
        kvn_ref[pl.ds(3 * NSA_KV + g, tr, stride=n_kv), :] = vs
        kvb_ref[0, :, lo:hi] = ks.astype(BF16)
        kvb_ref[0, :, half + lo:half + hi] = vs.astype(BF16)
        kvb_ref[0, :, 2 * half + lo:2 * half + hi] = kw.astype(BF16)
        kvb_ref[0, :, 3 * half + lo:3 * half + hi] = vw.astype(BF16)

        @pl.when(last)
        def _():
            win_ref[pl.ds(g, tr, stride=n_w), :] = kw
            win_ref[pl.ds(NSA_KV + g, tr, stride=n_w), :] = vw


def _kv_export(zb3, tabs, *, tr):
    b, t, _ = zb3.shape
    nt = t // tr
    kvblk = KV_OFF // 512
    n_kv, n_w = 4 * NSA_KV, 2 * NSA_KV
    zspec = lambda k: pl.BlockSpec((1, tr, 512), lambda bi, ti, k=k: (bi, ti, kvblk + k))
    tspec = pl.BlockSpec((tr, LANE), lambda bi, ti: (ti, 0))
    return pl.pallas_call(
        _kv_export_kernel,
        out_shape=[jax.ShapeDtypeStruct((b * t * n_kv, LANE), F32),
                   jax.ShapeDtypeStruct((b, t, n_kv * LANE), BF16),
                   jax.ShapeDtypeStruct((b * tr * n_w, LANE), F32)],
        grid=(b, nt),
        in_specs=[zspec(1), zspec(2), zspec(0), tspec, tspec, tspec],
        out_specs=[pl.BlockSpec((tr * n_kv, LANE), lambda bi, ti: (bi * nt + ti, 0)),
                   pl.BlockSpec((1, tr, n_kv * LANE), lambda bi, ti: (bi, ti, 0)),
                   pl.BlockSpec((tr * n_w, LANE), lambda bi, ti: (bi, 0))],
        compiler_params=_params(("parallel", "arbitrary")),
        name="kv_export",
    )(zb3, zb3, zb3, *tabs)


def _pool_kernel(*refs, n_prefetch):
    x_ref, a_ref, o_ref = refs[n_prefetch:]
    x = x_ref[0]
    rows = x.shape[0]
    x3 = x.reshape(rows // CMP_BLOCK, CMP_BLOCK, x.shape[1])
    pooled = jnp.sum(x3 * a_ref[...][None], axis=1)
    o_ref[...] = pooled.reshape(o_ref.shape)


def _pool_prompt(z3, a4):
    b, t, _ = z3.shape
    n = t // CMP_BLOCK
    return pl.pallas_call(
        functools.partial(_pool_kernel, n_prefetch=0),
        out_shape=jax.ShapeDtypeStruct((b, n, 512), F32),
        grid=(b,),
        in_specs=[pl.BlockSpec((1, t, 512), lambda bi: (bi, 0, KV_OFF // 512)),
                  pl.BlockSpec((CMP_BLOCK, 512), lambda bi: (0, 0))],
        out_specs=pl.BlockSpec((1, n, 512), lambda bi: (bi, 0, 0)),
        compiler_params=_params(("parallel",)),
        name="pool_prompt",
    )(z3, a4)


def _pool_pages_kernel(pt_sm, a_ref, cache_ref, o_ref, buf, sem, *, pages_per_step, page_base):
    step = pl.program_id(0)
    n_steps = pl.num_programs(0)
    slot = step % 2
    n_cols = buf.shape[3]

    def page_copy(step_idx, p, to_slot):
        page = pt_sm[step_idx * pages_per_step + p] + page_base
        return pltpu.make_async_copy(cache_ref.at[page, :, pl.ds(0, n_cols), :], buf.at[to_slot, p], sem.at[to_slot])

    @pl.when(step == 0)
    def _():
        for p in range(pages_per_step):
            page_copy(0, p, 0).start()

    @pl.when(step + 1 < n_steps)
    def _():
        for p in range(pages_per_step):
            page_copy(step + 1, p, 1 - slot).start()

    for p in range(pages_per_step):
        page_copy(step, p, slot).wait()

    a = a_ref[...]
    per = buf.shape[2] // CMP_BLOCK
    for p in range(pages_per_step):
        for k in range(per):
            x = buf[slot, p, pl.ds(k * CMP_BLOCK, CMP_BLOCK)]
            o_ref[0, p, k] = jnp.sum(x * a, axis=0)


def _pool_pages(cache4, page_flat, a3, *, bs, n_pages, page_base, pages_per_step=16):
    page = cache4.shape[1]
    per = page // CMP_BLOCK
    n_cols = 2 * NSA_KV
    total = bs * n_pages
    pages_per_step = min(pages_per_step, total)
    assert total % pages_per_step == 0
    n_steps = total // pages_per_step
    grid_spec = pltpu.PrefetchScalarGridSpec(
        num_scalar_prefetch=1, grid=(n_steps,),
        in_specs=[pl.BlockSpec((CMP_BLOCK, n_cols, LANE), lambda si, pt: (0, 0, 0)),
                  pl.BlockSpec(memory_space=pl.ANY)],
        out_specs=pl.BlockSpec((1, pages_per_step, per, n_cols, LANE), lambda si, pt: (si, 0, 0, 0, 0)),
        scratch_shapes=[pltpu.VMEM((2, pages_per_step, page, n_cols, LANE), F32),
                        pltpu.SemaphoreType.DMA((2,))])
    out = pl.pallas_call(
        functools.partial(_pool_pages_kernel, pages_per_step=pages_per_step, page_base=page_base),
        out_shape=jax.ShapeDtypeStruct((n_steps, pages_per_step, per, n_cols, LANE), F32),
        grid_spec=grid_spec,
        compiler_params=_params(("arbitrary",)),
        name="pool_pages",
    )(page_flat, a3, cache4)
    return out.reshape(bs, n_pages * per, n_cols * LANE)


def _cmp_proj_kernel(p_ref, w_ref, kc_ref, vc_ref):
    pooled = p_ref[0]
    n = pooled.shape[0]
    n_pad = kc_ref.shape[2]
    for c in range(4):
        r = _nn(pooled[:, c * LANE:(c + 1) * LANE].astype(BF16), w_ref[c]).astype(BF16)
        dst = kc_ref if c < 2 else vc_ref
        if n_pad > n:
            dst[0, c % 2] = jnp.zeros((n_pad, LANE), BF16)
        dst[0, c % 2, 0:n, :] = r


def _cmp_proj(pooled, w4):
    b, n, _ = pooled.shape
    n_pad = -(-n // LANE) * LANE
    spec = pl.BlockSpec((1, NSA_KV, n_pad, LANE), lambda bi: (bi, 0, 0, 0))
    return pl.pallas_call(
        _cmp_proj_kernel,
        out_shape=[jax.ShapeDtypeStruct((b, NSA_KV, n_pad, LANE), BF16)] * 2,
        grid=(b,),
        in_specs=[pl.BlockSpec((1, n, 512), lambda bi: (bi, 0, 0)),
                  pl.BlockSpec((4, LANE, LANE), lambda bi: (0, 0, 0))],
        out_specs=[spec, spec],
        compiler_params=_params(("parallel",)),
        name="cmp_proj",
    )(pooled, w4)


def _lane_parts(x):
    return [x[:, j * LANE:(j + 1) * LANE] for j in range(x.shape[1] // LANE)]


def _nsa_prompt_kernel(q_ref, ng_ref, bg_ref, ksel_ref, vsel_ref, kwin_ref, vwin_ref, kc_ref, vc_ref,
                       c_ref, s1_ref, s2_ref, o_ref, qc_scr, qr_scr, s_scr, p_scr, a_scr, m_scr, l_scr, acc_scr, oc_scr,
                       *, t_len, tq, tc, tw, n_sel, top):
    i = pl.program_id(2)
    hq = NSA_HPG
    q = q_ref[0]
    c, s1, s2 = c_ref[...], s1_ref[...], s2_ref[...]
    for h in range(hq):
        qh = q[:, h * LANE:(h + 1) * LANE]
        qc_scr[pl.ds(h * tq, tq), :] = (qh * SCALE_LOG2).astype(BF16)
        qr_scr[pl.ds(h * tq, tq), :] = (_rope(qh, c, s1, s2) * SCALE_LOG2).astype(BF16)

    kc = kc_ref[0, 0]
    npad = kc.shape[0]
    s_scr[:, 0:npad] = _nt(qc_scr[...], kc)
    tpos = i * tq + lax.broadcasted_iota(jnp.int32, (tq, npad), 0)
    ncol = lax.broadcasted_iota(jnp.int32, (tq, npad), 1)
    cmask = (ncol + 1) * CMP_BLOCK <= tpos + 1
    imp = jnp.zeros((tq, npad), F32)
    for h in range(hq):
        r = pl.ds(h * tq, tq)
        s = jnp.where(cmask, s_scr[r, 0:npad], MASK_NEG)
        e = jnp.where(cmask, jnp.exp2(s - jnp.max(s, axis=-1, keepdims=True)), 0.0)
        p = e / jnp.maximum(jnp.sum(e, axis=-1, keepdims=True), 1e-30)
        p_scr[r, 0:npad] = p.astype(BF16)
        imp = imp + p
    oc_scr[...] = _nn(p_scr[:, 0:npad], vc_ref[0, 0])

    rows = min(npad, -(-n_sel // 8) * 8)
    imp_t = imp.T[0:rows]
    blk = lax.broadcasted_iota(jnp.int32, (rows, tq), 0)
    cur = (i * tq + lax.broadcasted_iota(jnp.int32, (rows, tq), 1)) // SEL_BLOCK
    imp_t = jnp.where((blk == 0) | (blk == cur) | (blk == cur - 1), FORCE, imp_t)
    imp_t = jnp.where(blk > cur, -1.0, imp_t)
    imp_t = jnp.where(blk >= n_sel, -2.0, imp_t)
    rank = jnp.zeros((rows, tq), F32)
    for j in range(n_sel):
        a = imp_t[j:j + 1, :]
        ahead = (a > imp_t) | ((a == imp_t) & (blk > j))
        rank = rank + jnp.where(ahead, 1.0, 0.0)
    sel_t = jnp.where((rank < top) & (blk < n_sel), 1.0, 0.0)
    if npad > rows:
        sel_t = jnp.concatenate([sel_t, jnp.zeros((npad - rows, tq), F32)], axis=0)
    sel = sel_t.T.astype(BF16)

    m_scr[...] = jnp.full(m_scr.shape, MASK_NEG, F32)
    l_scr[...] = jnp.zeros(l_scr.shape, F32)
    acc_scr[...] = jnp.zeros(acc_scr.shape, F32)
    n_chunks = (i * tq + tq + tc - 1) // tc

    def chunk(ci, carry):
        k0 = pl.multiple_of(ci * tc, tc)
        k = ksel_ref[0, pl.ds(k0, tc), :]
        v = vsel_ref[0, pl.ds(k0, tc), :]
        jb = lax.broadcasted_iota(jnp.int32, (npad, tc), 0)
        kb = (k0 + lax.broadcasted_iota(jnp.int32, (npad, tc), 1)) // SEL_BLOCK
        expand = jnp.where(jb == kb, 1.0, 0.0).astype(BF16)
        chosen = _nn(sel, expand)
        kp = k0 + lax.broadcasted_iota(jnp.int32, (tq, tc), 1)
        tp = i * tq + lax.broadcasted_iota(jnp.int32, (tq, tc), 0)
        bias = jnp.where((chosen > 0.5) & (kp <= tp), 0.0, MASK_NEG)
        s_scr[:, 0:tc] = _nt(qr_scr[...], k)
        for h in range(hq):
            r = pl.ds(h * tq, tq)
            parts = _lane_parts(s_scr[r, 0:tc] + bias)
            m_prev = m_scr[r, :]
            m_new = jnp.maximum(m_prev, jnp.max(functools.reduce(jnp.maximum, parts), axis=-1, keepdims=True))
            alpha = jnp.exp2(m_prev - m_new)
            pes = [jnp.exp2(x - m_new) for x in parts]
            l_scr[r, :] = alpha * l_scr[r, :] + functools.reduce(jnp.add, pes)
            p_scr[r, 0:tc] = jnp.concatenate(pes, axis=1).astype(BF16)
            a_scr[r, :] = alpha
            m_scr[r, :] = m_new
        acc_scr[...] = a_scr[...] * acc_scr[...] + _nn(p_scr[:, 0:tc], v)
        return carry

    lax.fori_loop(0, n_chunks, chunk, 0)

    w0 = pl.multiple_of(jnp.clip(i * tq + tq - tw, 0, t_len - tw), LANE)
    kp = w0 + lax.broadcasted_iota(jnp.int32, (tq, tw), 1)
    tp = i * tq + lax.broadcasted_iota(jnp.int32, (tq, tw), 0)
    wbias = jnp.where((kp <= tp) & (kp > tp - WINDOW), 0.0, MASK_NEG)
    s_scr[:, 0:tw] = _nt(qr_scr[...], kwin_ref[0, pl.ds(w0, tw), :])
    for h in range(hq):
        r = pl.ds(h * tq, tq)
        parts = _lane_parts(s_scr[r, 0:tw] + wbias)
        m = jnp.max(functools.reduce(jnp.maximum, parts), axis=-1, keepdims=True)
        ews = [jnp.exp2(x - m) for x in parts]
        a_scr[r, :] = jnp.broadcast_to(jnp.sum(functools.reduce(jnp.add, ews), axis=-1, keepdims=True), (tq, LANE))
        p_scr[r, 0:tw] = jnp.concatenate(ews, axis=1).astype(BF16)
    o_w = _nn(p_scr[:, 0:tw], vwin_ref[0, pl.ds(w0, tw), :])

    gate = jax.nn.sigmoid(bg_ref[0])
    ng = ng_ref[0]
    for h in range(hq):
        r = slice(h * tq, (h + 1) * tq)
        o_s = acc_scr[r, :] / jnp.sum(l_scr[r, :], axis=-1, keepdims=True)
        o = (gate[:, 3 * h:3 * h + 1] * oc_scr[r, :] + gate[:, 3 * h + 1:3 * h + 2] * o_s
             + gate[:, 3 * h + 2:3 * h + 3] * (o_w[r] / a_scr[r, :]))
        o_ref[0, :, h * LANE:(h + 1) * LANE] = (_silu(ng[:, h * LANE:(h + 1) * LANE]) * o).astype(o_ref.dtype)


def _nsa_prompt(z3, bg3, kvb, kc, vc, tabs, *, tq=128, tc=512):
    b, t, _ = z3.shape
    nq = t // tq
    tc = min(tc, t)
    tw = min(WINDOW + tq, t)
    n_sel = t // SEL_BLOCK
    top = min(TOP_N, n_sel)
    npad = kc.shape[2]
    gw = NSA_HPG * LANE
    rows = NSA_HPG * tq
    wide = max(tc, tw, npad)
    kvspec = lambda k: pl.BlockSpec((1, t, LANE), lambda bi, gi, qi, k=k: (bi, 0, k + gi))
    cspec = pl.BlockSpec((1, 1, npad, LANE), lambda bi, gi, qi: (bi, gi, 0, 0))
    tspec = pl.BlockSpec((tq, LANE), lambda bi, gi, qi: (qi, 0))
    kern = functools.partial(_nsa_prompt_kernel, t_len=t, tq=tq, tc=tc, tw=tw, n_sel=n_sel, top=top)
    return pl.pallas_call(
        kern,
        out_shape=jax.ShapeDtypeStruct((b, t, NSA_HEADS * LANE), BF16),
        grid=(b, NSA_KV, nq),
        in_specs=[pl.BlockSpec((1, tq, gw), lambda bi, gi, qi: (bi, qi, Q_OFF // gw + gi)),
                  pl.BlockSpec((1, tq, gw), lambda bi, gi, qi: (bi, qi, NG_OFF // gw + gi)),
                  pl.BlockSpec((1, tq, LANE), lambda bi, gi, qi: (bi, qi, gi)),
                  kvspec(0), kvspec(2), kvspec(4), kvspec(6), cspec, cspec, tspec, tspec, tspec],
        out_specs=pl.BlockSpec((1, tq, gw), lambda bi, gi, qi: (bi, qi, gi)),
        scratch_shapes=([pltpu.VMEM((rows, LANE), BF16)] * 2
                        + [pltpu.VMEM((rows, wide), F32), pltpu.VMEM((rows, wide), BF16)]
                        + [pltpu.VMEM((rows, LANE), F32)] * 5),
        compiler_params=_params(("parallel", "parallel", "arbitrary")),
        name="nsa_prompt",
    )(z3, z3, bg3, kvb, kvb, kvb, kvb, kc, vc, *tabs)


def _conv_kernel(h_ref, b_ref, c_ref, g_ref, prev_ref, w_ref, y_ref, st_ref, up_scr, *, t_real):
    u = c_ref[0] * h_ref[0]
    t = u.shape[0]
    up_scr[pl.ds(8 - (CONV_W - 1), CONV_W - 1), :] = prev_ref[0]
    up_scr[pl.ds(8, t), :] = u
    w = w_ref[...]
    y = w[0:1, :] * up_scr[pl.ds(6, t), :]
    y = y + w[1:2, :] * up_scr[pl.ds(7, t), :]
    y = y + w[2:3, :] * u
    y = b_ref[0] * y
    y_ref[0] = (_silu(g_ref[0]) * y).astype(y_ref.dtype)
    st_ref[0] = up_scr[pl.ds(6 + t_real, CONV_W - 1), :]


def _conv_mixer(z3, prev, w_conv, *, t_real):
    b, t, _ = z3.shape
    nc = CONV_DIM // LANE
    zspec = lambda off: pl.BlockSpec((1, t, LANE), lambda bi, ci, off=off: (bi, 0, off // LANE + ci))
    return pl.pallas_call(
        functools.partial(_conv_kernel, t_real=t_real),
        out_shape=[jax.ShapeDtypeStruct((b, t, CONV_DIM), BF16),
                   jax.ShapeDtypeStruct((b, CONV_W - 1, CONV_DIM), F32)],
        grid=(b, nc),
        in_specs=[zspec(C_H), zspec(C_B), zspec(C_C), zspec(C_G),
                  pl.BlockSpec((1, CONV_W - 1, LANE), lambda bi, ci: (bi, 0, ci)),
                  pl.BlockSpec((CONV_W, LANE), lambda bi, ci: (0, ci))],
        out_specs=[pl.BlockSpec((1, t, LANE), lambda bi, ci: (bi, 0, ci)),
                   pl.BlockSpec((1, CONV_W - 1, LANE), lambda bi, ci: (bi, 0, ci))],
        scratch_shapes=[pltpu.VMEM((t + 8, LANE), F32)],
        compiler_params=_params(("parallel", "parallel")),
        name="conv_mixer",
    )(z3, z3, z3, z3, prev, w_conv)


def _mem_attn_kernel(q_ref, mg_ref, kv_ref, o_ref, *, interleaved):
    q = q_ref[0]
    mg = mg_ref[0]
    half = MEM_HEADS * LANE
    for h in range(MEM_HEADS):
        lo, hi = h * LANE, (h + 1) * LANE
        if interleaved:
            nm = kv_ref.shape[1] // (2 * MEM_HEADS)
            k = kv_ref[0, pl.ds(h, nm, stride=2 * MEM_HEADS), :].astype(BF16)
            v = kv_ref[0, pl.ds(MEM_HEADS + h, nm, stride=2 * MEM_HEADS), :].astype(BF16)
        else:
            k = kv_ref[0, :, lo:hi].astype(BF16)
            v = kv_ref[0, :, half + lo:half + hi].astype(BF16)
        s = _nt((q[:, lo:hi] * ATTN_SCALE).astype(BF16), k)
        e = jnp.exp(s - jnp.max(s, axis=-1, keepdims=True))
        o = _nn(e.astype(BF16), v) / jnp.sum(e, axis=-1, keepdims=True)
        o_ref[0, :, lo:hi] = (_silu(mg[:, lo:hi]) * o).astype(o_ref.dtype)


def _mem_attn(zb3, mkv, *, tq, interleaved, kv_base=0):
    b, t, _ = zb3.shape
    wq = MEM_HEADS * LANE
    return pl.pallas_call(
        functools.partial(_mem_attn_kernel, interleaved=interleaved),
        out_shape=jax.ShapeDtypeStruct((b, t, wq), BF16),
        grid=(b, t // tq),
        in_specs=[pl.BlockSpec((1, tq, wq), lambda bi, ti: (bi, ti, MQ_OFF // wq)),
                  pl.BlockSpec((1, tq, wq), lambda bi, ti: (bi, ti, MG_OFF // wq)),
                  pl.BlockSpec((1,) + mkv.shape[1:], lambda bi, ti: (bi + kv_base, 0, 0))],
        out_specs=pl.BlockSpec((1, tq, wq), lambda bi, ti: (bi, ti, 0)),
        compiler_params=_params(("parallel", "parallel")),
        name="mem_attn",
    )(zb3, zb3, mkv)


def _out_proj_kernel(x_ref, ya_ref, yb_ref, ym_ref, w_ref, fg_ref, o_ref, *, final):
    a, bw = CONV_DIM, CONV_DIM + NSA_HEADS * LANE
    acc = _nn(ya_ref[...], w_ref[0:a, :])
    acc = acc + _nn(yb_ref[...], w_ref[a:bw, :])
    acc = acc + _nn(ym_ref[...], w_ref[bw:, :])
    r = x_ref[...] + acc
    if final:
        r = r * lax.rsqrt(jnp.mean(r * r, axis=-1, keepdims=True) + NORM_EPS) * fg_ref[...]
    o_ref[...] = r


def _out_proj(x, ya, yb, ym, w, fg, *, tm, final):
    m, d = x.shape
    row = lambda width: pl.BlockSpec((tm, width), lambda i: (i, 0))
    return pl.pallas_call(
        functools.partial(_out_proj_kernel, final=final),
        out_shape=jax.ShapeDtypeStruct((m, d), F32),
        grid=(m // tm,),
        in_specs=[row(d), row(ya.shape[1]), row(yb.shape[1]), row(ym.shape[1]),
                  pl.BlockSpec(w.shape, lambda i: (0, 0)),
                  pl.BlockSpec((1, d), lambda i: (0, 0))],
        out_specs=row(d),
        compiler_params=_params(("parallel",)),
        name="out_proj",
    )(x, ya, yb, ym, w, fg.reshape(1, d))


def _sample_pre_kernel(q_ref, ng_ref, bg_ref, kv0_ref, kv1_ref, kv2_ref, c_ref, s1_ref, s2_ref, kc_ref, vc_ref,
                       kvn_ref, wn_ref, qr_ref, oc_ref, ngo_ref, gate_ref, val_ref, *, past):
    tp = SAMPLE_T_PAD
    hq = NSA_HPG
    c, s1, s2 = c_ref[...], s1_ref[...], s2_ref[...]
    kv0, kv1, kv2 = kv0_ref[0], kv1_ref[0], kv2_ref[0]
    kvn_ref[0, :, 0:512] = kv0
    kvn_ref[0, :, 768:1024] = kv1[:, 256:512]
    wn_ref[0, :, 256:512] = kv2[:, 256:512]
    for g in range(NSA_KV):
        lo, hi = g * LANE, (g + 1) * LANE
        kvn_ref[0, :, 512 + lo:512 + hi] = _rope(kv1[:, lo:hi], c, s1, s2)
        wn_ref[0, :, lo:hi] = _rope(kv2[:, lo:hi], c, s1, s2)

    q = q_ref[0]
    ng = ng_ref[0]
    gates = jax.nn.sigmoid(bg_ref[0])
    for g in range(NSA_KV):
        qc_l, qr_l = [], []
        for h in range(hq):
            lo = (g * hq + h) * LANE
            qh = q[:, lo:lo + LANE]
            qc_l.append(qh * ATTN_SCALE)
            qr_l.append(_rope(qh, c, s1, s2) * ATTN_SCALE)
            ngo_ref[0, g, h * tp:(h + 1) * tp, :] = ng[:, lo:lo + LANE]
            for br in range(N_BRANCH):
                col = g * LANE + h * N_BRANCH + br
                gate_ref[0, g, br, h * tp:(h + 1) * tp, :] = jnp.broadcast_to(gates[:, col:col + 1], (tp, LANE))
        qc = jnp.concatenate(qc_l, axis=0)
        qr_ref[0, g] = jnp.concatenate(qr_l, axis=0)

        kc = kc_ref[0, g]
        npad = kc.shape[0]
        s = _nt(qc.astype(BF16), kc)
        trow = lax.broadcasted_iota(jnp.int32, (hq * tp, npad), 0) % tp
        ncol = lax.broadcasted_iota(jnp.int32, (hq * tp, npad), 1)
        cmask = (ncol + 1) * CMP_BLOCK <= past + trow + 1
        s = jnp.where(cmask, s, MASK_NEG)
        e = jnp.where(cmask, jnp.exp(s - jnp.max(s, axis=-1, keepdims=True)), 0.0)
        p = e / jnp.maximum(jnp.sum(e, axis=-1, keepdims=True), 1e-30)
        oc_ref[0, g] = _nn(p.astype(BF16), vc_ref[0, g])
        imp = jnp.sum(p.reshape(hq, tp, npad), axis=0)

        blk = lax.broadcasted_iota(jnp.int32, (tp, npad), 1)
        cur = (past + lax.broadcasted_iota(jnp.int32, (tp, npad), 0)) // SEL_BLOCK
        val = jnp.where((blk == 0) | (blk == cur) | (blk == cur - 1), FORCE, imp)
        val = jnp.where(blk > cur, -1.0, val)
        val = jnp.where(blk >= past // SEL_BLOCK, -2.0, val)
        val_ref[0, g] = val


def _topk_kernel(val_ref, idx_ref, *, n_top):
    val = val_ref[...]
    rows, n = val.shape
    blk = lax.broadcasted_iota(jnp.int32, (rows, n), 1)
    lane = lax.broadcasted_iota(jnp.int32, (rows, LANE), 1)
    idx = jnp.zeros((rows, LANE), jnp.int32)
    for r in range(n_top):
        best = jnp.max(val, axis=-1, keepdims=True)
        j = jnp.min(jnp.where(val == best, blk, n), axis=-1, keepdims=True)
        idx = jnp.where(lane == r, j, idx)
        val = jnp.where(blk == j, -3e38, val)
    idx_ref[...] = idx


def _topk(val2, *, n_top):
    rows, n = val2.shape
    return pl.pallas_call(
        functools.partial(_topk_kernel, n_top=n_top),
        out_shape=jax.ShapeDtypeStruct((rows, LANE), jnp.int32),
        grid=(1,),
        in_specs=[pl.BlockSpec((rows, n), lambda i: (0, 0))],
        out_specs=pl.BlockSpec((rows, LANE), lambda i: (0, 0)),
        compiler_params=_params(("arbitrary",)),
        name="sample_topk",
    )(val2)


def _sample_pre(za3, zb3, bg3, tabs, kc, vc, *, past):
    bs, tp, _ = za3.shape
    npad = kc.shape[2]
    qw = NSA_HEADS * LANE
    kvblk = KV_OFF // 512
    zspec = lambda k: pl.BlockSpec((1, tp, 512), lambda bi, k=k: (bi, 0, kvblk + k))
    tspec = pl.BlockSpec((tp, LANE), lambda bi: (0, 0))
    cspec = pl.BlockSpec((1, NSA_KV, npad, LANE), lambda bi: (bi, 0, 0, 0))
    rows = NSA_HPG * tp
    gspec = pl.BlockSpec((1, NSA_KV, rows, LANE), lambda bi: (bi, 0, 0, 0))
    gshape = jax.ShapeDtypeStruct((bs, NSA_KV, rows, LANE), F32)
    return pl.pallas_call(
        functools.partial(_sample_pre_kernel, past=past),
        out_shape=[jax.ShapeDtypeStruct((bs, tp, 1024), F32),
                   jax.ShapeDtypeStruct((bs, tp, 512), F32),
                   gshape, gshape, gshape,
                   jax.ShapeDtypeStruct((bs, NSA_KV, N_BRANCH, rows, LANE), F32),
                   jax.ShapeDtypeStruct((bs, NSA_KV, tp, npad), F32)],
        grid=(bs,),
        in_specs=[pl.BlockSpec((1, tp, qw), lambda bi: (bi, 0, Q_OFF // qw)),
                  pl.BlockSpec((1, tp, qw), lambda bi: (bi, 0, NG_OFF // qw)),
                  pl.BlockSpec((1, tp, NSA_KV * LANE), lambda bi: (bi, 0, 0)),
                  zspec(0), zspec(1), zspec(2), tspec, tspec, tspec, cspec, cspec],
        out_specs=[pl.BlockSpec((1, tp, 1024), lambda bi: (bi, 0, 0)),
                   pl.BlockSpec((1, tp, 512), lambda bi: (bi, 0, 0)),
                   gspec, gspec, gspec,
                   pl.BlockSpec((1, NSA_KV, N_BRANCH, rows, LANE), lambda bi: (bi, 0, 0, 0, 0)),
                   pl.BlockSpec((1, NSA_KV, tp, npad), lambda bi: (bi, 0, 0, 0))],
        compiler_params=_params(("parallel",)),
        name="sample_pre",
    )(za3, za3, bg3, zb3, zb3, zb3, *tabs, kc, vc)


def _sample_attn_kernel(idx_sm, pt_sm, qr_ref, oc_ref, ng_ref, gate_ref, ksn_ref, vsn_ref,
                        wc_ref, kwn_ref, vwn_ref, cache_ref, o_ref,
                        kbuf, vbuf, kw_scr, vw_scr, sem, *, ts, n_top, n_pages, page_base, per_page, wb):
    tp = SAMPLE_T_PAD
    hq = NSA_HPG
    b = pl.program_id(0)
    g = pl.program_id(1)
    n_gath = n_top * SEL_BLOCK
    ks_rows = kbuf.shape[1]

    def gather_copies(t, r):
        blk = idx_sm[((b * NSA_KV + g) * ts + t) * n_top + r]
        page = pt_sm[b * n_pages + blk // per_page] + page_base
        row0 = (blk % per_page) * SEL_BLOCK
        src_k = cache_ref.at[page, pl.ds(row0, SEL_BLOCK), 2 * NSA_KV + g]
        src_v = cache_ref.at[page, pl.ds(row0, SEL_BLOCK), 3 * NSA_KV + g]
        dst = pl.ds(r * SEL_BLOCK, SEL_BLOCK)
        return (pltpu.make_async_copy(src_k, kbuf.at[t, dst], sem.at[0]),
                pltpu.make_async_copy(src_v, vbuf.at[t, dst], sem.at[1]))

    for t in range(ts):
        for r in range(n_top):
            ck, cv = gather_copies(t, r)
            ck.start()
            cv.start()

    qr = qr_ref[0, 0].astype(BF16)
    trow = lax.broadcasted_iota(jnp.int32, (hq * tp, 1), 0) % tp

    ww = kw_scr.shape[0]
    kw_scr[pl.ds(0, wb), :] = wc_ref[0, pl.ds(g, wb, stride=2 * NSA_KV), :]
    vw_scr[pl.ds(0, wb), :] = wc_ref[0, pl.ds(NSA_KV + g, wb, stride=2 * NSA_KV), :]
    kw_scr[pl.ds(wb, tp), :] = kwn_ref[0]
    vw_scr[pl.ds(wb, tp), :] = vwn_ref[0]
    kw_scr[pl.ds(wb + tp, ww - wb - tp), :] = jnp.zeros((ww - wb - tp, LANE), F32)
    vw_scr[pl.ds(wb + tp, ww - wb - tp), :] = jnp.zeros((ww - wb - tp, LANE), F32)
    sw = _nt(qr, kw_scr[...].astype(BF16))
    jw = lax.broadcasted_iota(jnp.int32, (hq * tp, ww), 1)
    rel = jw - wb
    okw = (rel <= trow) & (rel > trow - WINDOW) & (jw < wb + ts)
    sw = jnp.where(okw, sw, MASK_NEG)
    ew = jnp.where(okw, jnp.exp(sw - jnp.max(sw, axis=-1, keepdims=True)), 0.0)
    o_w = _nn(ew.astype(BF16), vw_scr[...].astype(BF16)) / jnp.sum(ew, axis=-1, keepdims=True)

    for t in range(ts):
        for r in range(n_top):
            ck, cv = gather_copies(t, r)
            ck.wait()
            cv.wait()

    js = lax.broadcasted_iota(jnp.int32, (hq * tp, ks_rows), 1)
    o_s = jnp.zeros((hq * tp, LANE), F32)
    for t in range(ts):
        kbuf[t, pl.ds(n_gath, tp), :] = ksn_ref[0]
        vbuf[t, pl.ds(n_gath, tp), :] = vsn_ref[0]
        kbuf[t, pl.ds(n_gath + tp, ks_rows - n_gath - tp), :] = jnp.zeros((ks_rows - n_gath - tp, LANE), F32)
        vbuf[t, pl.ds(n_gath + tp, ks_rows - n_gath - tp), :] = jnp.zeros((ks_rows - n_gath - tp, LANE), F32)
        ss = _nt(qr, kbuf[t].astype(BF16))
        oks = (js < n_gath) | ((js - n_gath <= t) & (js < n_gath + ts))
        ss = jnp.where(oks, ss, MASK_NEG)
        es = jnp.where(oks, jnp.exp(ss - jnp.max(ss, axis=-1, keepdims=True)), 0.0)
        ot = _nn(es.astype(BF16), vbuf[t].astype(BF16)) / jnp.sum(es, axis=-1, keepdims=True)
        o_s = jnp.where(trow == t, ot, o_s)

    o = gate_ref[0, 0, 0] * oc_ref[0, 0] + gate_ref[0, 0, 1] * o_s + gate_ref[0, 0, 2] * o_w
    y = _silu(ng_ref[0, 0]) * o
    for h in range(hq):
        o_ref[0, :, h * LANE:(h + 1) * LANE] = y[h * tp:(h + 1) * tp].astype(o_ref.dtype)


def _sample_attn(idx_flat, page_flat, qr, oc, ngo, gates, kvn, cache_win_rows, wn, cache4, *,
                 ts, n_top, n_pages, page_base, win_base):
    bs = qr.shape[0]
    tp = SAMPLE_T_PAD
    rows = NSA_HPG * tp
    wb = cache_win_rows.shape[1] // (2 * NSA_KV)
    per_page = cache4.shape[1] // SEL_BLOCK
    ks_rows = -(-(n_top * SEL_BLOCK + tp) // LANE) * LANE
    ww = -(-(wb + tp) // LANE) * LANE
    gspec = pl.BlockSpec((1, 1, rows, LANE), lambda bi, gi, *_: (bi, gi, 0, 0))
    newspec = lambda k: pl.BlockSpec((1, tp, LANE), lambda bi, gi, *_, k=k: (bi, 0, k + gi))
    grid_spec = pltpu.PrefetchScalarGridSpec(
        num_scalar_prefetch=2, grid=(bs, NSA_KV),
        in_specs=[gspec, gspec, gspec,
                  pl.BlockSpec((1, 1, N_BRANCH, rows, LANE), lambda bi, gi, *_: (bi, gi, 0, 0, 0)),
                  newspec(2 * NSA_KV), newspec(3 * NSA_KV),
                  pl.BlockSpec((1,) + cache_win_rows.shape[1:], lambda bi, gi, *_: (bi + win_base, 0, 0)),
                  newspec(0), newspec(NSA_KV),
                  pl.BlockSpec(memory_space=pl.ANY)],
        out_specs=pl.BlockSpec((1, tp, NSA_HPG * LANE), lambda bi, gi, *_: (bi, 0, gi)),
        scratch_shapes=[pltpu.VMEM((ts, ks_rows, LANE), F32), pltpu.VMEM((ts, ks_rows, LANE), F32),
                        pltpu.VMEM((ww, LANE), F32), pltpu.VMEM((ww, LANE), F32),
                        pltpu.SemaphoreType.DMA((2,))])
    kern = functools.partial(_sample_attn_kernel, ts=ts, n_top=n_top, n_pages=n_pages,
                             page_base=page_base, per_page=per_page, wb=wb)
    return pl.pallas_call(
        kern,
        out_shape=jax.ShapeDtypeStruct((bs, tp, NSA_HEADS * LANE), BF16),
        grid_spec=grid_spec,
        compiler_params=_params(("arbitrary", "arbitrary")),
        name="sample_attn",
    )(idx_flat, page_flat, qr, oc, ngo, gates, kvn, kvn, cache_win_rows, wn, wn, cache4)


def _rope_tables(pos, rows):
    freqs = jnp.power(ROPE_THETA, -jnp.arange(ROPE_HALF, dtype=F32) * (2.0 / ROPE_DIM))
    ang = pos.astype(F32)[:, None] * freqs[None, :]
    cos, sin = jnp.cos(ang), jnp.sin(ang)
    n = pos.shape[0]
    z16 = jnp.zeros((n, ROPE_HALF), F32)
    rest = LANE - ROPE_DIM
    c = jnp.concatenate([cos, cos, jnp.ones((n, rest), F32)], axis=1)
    s1 = jnp.concatenate([z16, sin, jnp.zeros((n, rest), F32)], axis=1)
    s2 = jnp.concatenate([-sin, z16, jnp.zeros((n, rest), F32)], axis=1)
    pad = lambda a: jnp.pad(a, ((0, rows - n), (0, 0)))
    return pad(c), pad(s1), pad(s2)


def _layer_weights(norm_g, w_in, w_conv, a_cmp, w_cmp, w_out):
    w_a = w_in[:, :BG_SRC].astype(BF16)
    w_b = w_in[:, BG_SRC + BG_N:].astype(BF16)
    per_group = NSA_HPG * N_BRANCH
    wbg = jnp.concatenate(
        [jnp.pad(w_in[:, BG_SRC + g * per_group:BG_SRC + (g + 1) * per_group], ((0, 0), (0, LANE - per_group)))
         for g in range(NSA_KV)], axis=1).astype(BF16)
    a4 = jnp.concatenate([a_cmp[0], a_cmp[0], a_cmp[1], a_cmp[1]], axis=1)
    a3 = jnp.stack([a_cmp[0], a_cmp[0], a_cmp[1], a_cmp[1]], axis=1)
    w4 = jnp.stack([w_cmp[0], w_cmp[0], w_cmp[1], w_cmp[1]]).astype(BF16)
    return dict(norm_g=norm_g, w_a=w_a, w_b=w_b, wbg=wbg, w_conv=w_conv, a4=a4, a3=a3, w4=w4,
                w_out=w_out.astype(BF16))


def _in_proj(x2, lw, *, tm):
    za, bg = _norm_matmul(x2, lw["norm_g"], lw["w_a"], lw["wbg"], tm=tm, tn=1024)
    zb = _norm_matmul(x2, lw["norm_g"], lw["w_b"], tm=tm, tn=1280)
    return za, zb, bg


def _prompt_layer(xp, mem_prompt, mem_norm_g, w_mem, lw, final_g, final):
    b, t, d = xp.shape
    m = b * t
    za, zb, bg = _in_proj(xp.reshape(m, d), lw, tm=min(1024, m))
    za3 = za.reshape(b, t, ZA_WIDTH)
    zb3 = zb.reshape(b, t, ZB_WIDTH)
    bg3 = bg.reshape(b, t, NSA_KV * LANE)
    nm = mem_prompt.shape[1]
    mkv = _norm_matmul(mem_prompt.reshape(b * nm, d), mem_norm_g, w_mem, tm=min(512, b * nm), tn=512)
    mkv3 = mkv.reshape(b, nm, 2 * MEM_HEADS * LANE)
    tabs = _rope_tables(jnp.arange(t, dtype=jnp.int32), t)
    wb = min(WINDOW, t)
    kvn, kvb, win_rows = _kv_export(zb3, tabs, tr=wb)
    kc, vc = _cmp_proj(_pool_prompt(zb3, lw["a4"]), lw["w4"])
    yb = _nsa_prompt(za3, bg3, kvb, kc, vc, tabs, tq=min(256, t))
    conv0 = jnp.zeros((b, CONV_W - 1, CONV_DIM), F32)
    ya, conv_new = _conv_mixer(za3, conv0, lw["w_conv"], t_real=t)
    ym = _mem_attn(zb3, mkv3, tq=min(512, t), interleaved=False)
    out = _out_proj(xp.reshape(m, d), ya.reshape(m, -1), yb.reshape(m, -1), ym.reshape(m, -1),
                    lw["w_out"], final_g, tm=min(512, m), final=final)
    kv_new = kvn.reshape(b, t, 4, NSA_KV, HEAD_DIM)
    win_new = win_rows.reshape(b, wb, 2, NSA_KV, HEAD_DIM)
    mem_kv = mkv.reshape(b, nm, 2, MEM_HEADS, HEAD_DIM)
    return out.reshape(b, t, d), kv_new, win_new, conv_new, mem_kv


def _sample_layer(xs_p, ts, layer, cache_kv, cache_win, state_conv, cache_mem, page_table, lw, final_g, final):
    bs, tp, d = xs_p.shape
    depth, pool, page = cache_kv.shape[0], cache_kv.shape[1], cache_kv.shape[2]
    n_pages = page_table.shape[1]
    past = n_pages * page
    assert past % SEL_BLOCK == 0 and ts <= SEL_BLOCK and ts <= tp
    n_past = past // SEL_BLOCK
    n_top = min(TOP_N, n_past + 1) - 1
    m = bs * tp
    za, zb, bg = _in_proj(xs_p.reshape(m, d), lw, tm=m)
    za3 = za.reshape(bs, tp, ZA_WIDTH)
    zb3 = zb.reshape(bs, tp, ZB_WIDTH)
    bg3 = bg.reshape(bs, tp, NSA_KV * LANE)
    tabs = _rope_tables(past + jnp.arange(tp, dtype=jnp.int32), tp)
    page_flat = page_table.reshape(-1).astype(jnp.int32)
    cache4 = cache_kv.reshape(depth * pool, page, 4 * NSA_KV, HEAD_DIM)
    pooled = _pool_pages(cache4, page_flat, lw["a3"], bs=bs, n_pages=n_pages, page_base=layer * pool)
    kc, vc = _cmp_proj(pooled, lw["w4"])
    kvn, wn, qr, oc, ngo, gates, val = _sample_pre(za3, zb3, bg3, tabs, kc, vc, past=past)
    idx = _topk(val.reshape(bs * NSA_KV * tp, val.shape[-1]), n_top=n_top)
    idx_flat = idx.reshape(bs, NSA_KV, tp, LANE)[:, :, :ts, :n_top].reshape(-1)
    wbuf = cache_win.shape[2]
    cache_win_rows = cache_win.reshape(depth * bs, wbuf * 2 * NSA_KV, HEAD_DIM)
    yb = _sample_attn(idx_flat, page_flat, qr, oc, ngo, gates, kvn, cache_win_rows, wn, cache4,
                      ts=ts, n_top=n_top, n_pages=n_pages, page_base=layer * pool, win_base=layer * bs)
    ya, conv_new = _conv_mixer(za3, state_conv[layer], lw["w_conv"], t_real=ts)
    nm = cache_mem.shape[2]
    mem_rows = cache_mem.reshape(depth * bs, nm * 2 * MEM_HEADS, HEAD_DIM)
    ym = _mem_attn(zb3, mem_rows, tq=tp, interleaved=True, kv_base=layer * bs)
    out = _out_proj(xs_p.reshape(m, d), ya.reshape(m, -1), yb.reshape(m, -1), ym.reshape(m, -1),
                    lw["w_out"], final_g, tm=m, final=final)
    kv_new = kvn[:, :ts].reshape(bs, ts, 4, NSA_KV, HEAD_DIM)
    win_rows = wn[:, :ts].reshape(bs, ts, 2, NSA_KV, HEAD_DIM)
    win_state = jnp.concatenate([cache_win[layer], win_rows], axis=1)[:, ts:]
    return out.reshape(bs, tp, d), kv_new, win_state, conv_new


def kernel(x_prompt, x_sample, cache_kv, cache_win, state_conv, cache_mem, page_table, mem_prompt,
           norm_g, w_in, w_conv, a_cmp, w_cmp, mem_norm_g, w_mem_kv, w_out, final_g):
    depth = w_in.shape[0]
    ts = x_sample.shape[1]
    xp = x_prompt
    xs = jnp.pad(x_sample, ((0, 0), (0, SAMPLE_T_PAD - ts), (0, 0)))
    kv_p, win_p, conv_p, mem_p, kv_s, win_s, conv_s = [], [], [], [], [], [], []
    for l in range(depth):
        lw = _layer_weights(norm_g[l], w_in[l], w_conv[l], a_cmp[l], w_cmp[l], w_out[l])
        final = l == depth - 1
        xp, kvn, winn, convn, mkv = _prompt_layer(xp, mem_prompt, mem_norm_g[l], w_mem_kv[l].astype(BF16),
                                                  lw, final_g, final)
        kv_p.append(kvn)
        win_p.append(winn)
        conv_p.append(convn)
        mem_p.append(mkv)
        xs, kvn, winn, convn = _sample_layer(xs, ts, l, cache_kv, cache_win, state_conv, cache_mem,
                                             page_table, lw, final_g, final)
        kv_s.append(kvn)
        win_s.append(winn)
        conv_s.append(convn)
    return (xp, xs[:, :ts], jnp.stack(kv_p), jnp.stack(win_p), jnp.stack(conv_p), jnp.stack(mem_p),
            jnp.stack(kv_s), jnp.stack(win_s), jnp.stack(conv_s))
```

```python
import functools

import jax
import jax.numpy as jnp
from jax import lax
from jax.experimental import pallas as pl
from jax.experimental.pallas import tpu as pltpu

F32 = jnp.float32
BF16 = jnp.bfloat16

HEAD_DIM = 128
CONV_DIM = 512
CONV_W = 3
NSA_HEADS = 8
NSA_KV = 2
NSA_HPG = NSA_HEADS // NSA_KV
MEM_HEADS = 4
N_BRANCH = 3
ROPE_DIM = HEAD_DIM // 4
ROPE_HALF = ROPE_DIM // 2
ROPE_THETA = 500000.0
CMP_BLOCK = 64
SEL_BLOCK = 64
TOP_N = 16
WINDOW = 512
NORM_EPS = 1e-6
MASK_NEG = -1e30
FORCE = 1e9
ATTN_SCALE = HEAD_DIM ** -0.5
SCALE_LOG2 = ATTN_SCALE * 1.4426950408889634

C_H, C_B, C_C, C_G, Q_OFF, NG_OFF = 0, 512, 1024, 1536, 2048, 3072
ZA_WIDTH = 4096
KV_OFF, MQ_OFF, MG_OFF = 0, 1536, 2048
ZB_WIDTH = 2560
BG_SRC = 4096
BG_N = NSA_HEADS * N_BRANCH
LANE = 128
SAMPLE_T_PAD = 8
VMEM_LIMIT = 56 * 1024 * 1024


def _nt(a, b):
    return lax.dot_general(a, b, (((1,), (1,)), ((), ())), preferred_element_type=F32)


def _nn(a, b):
    return jnp.dot(a, b, preferred_element_type=F32)


def _params(sem, vmem=VMEM_LIMIT):
    return pltpu.CompilerParams(dimension_semantics=sem, vmem_limit_bytes=vmem)


def _rope(x, c, s1, s2):
    return x * c + pltpu.roll(x, ROPE_HALF, 1) * s1 + pltpu.roll(x, LANE - ROPE_HALF, 1) * s2


def _silu(x):
    return x * jax.nn.sigmoid(x)


def _norm_matmul_kernel(x_ref, g_ref, w_ref, *rest, with_gate, tn):
    if with_gate:
        wbg_ref, z_ref, bg_ref, h_scr = rest
    else:
        z_ref, h_scr = rest
    x = x_ref[...]
    y = x * lax.rsqrt(jnp.mean(x * x, axis=-1, keepdims=True) + NORM_EPS) * g_ref[...]
    h_scr[...] = y.astype(BF16)
    if with_gate:
        bg_ref[...] = _nn(h_scr[...], wbg_ref[...])
    for j in range(w_ref.shape[1] // tn):
        z_ref[:, j * tn:(j + 1) * tn] = _nn(h_scr[...], w_ref[:, j * tn:(j + 1) * tn])


def _norm_matmul(x, g, w, wbg=None, *, tm, tn):
    m, d = x.shape
    n = w.shape[1]
    assert n % tn == 0 and m % tm == 0
    with_gate = wbg is not None
    resident = lambda shape: pl.BlockSpec(shape, lambda i: (0, 0), pipeline_mode=pl.Buffered(1))
    in_specs = [pl.BlockSpec((tm, d), lambda i: (i, 0)), resident((1, d)), resident((d, n))]
    out_shape = [jax.ShapeDtypeStruct((m, n), F32)]
    out_specs = [pl.BlockSpec((tm, n), lambda i: (i, 0))]
    args = [x, g.reshape(1, d), w]
    if with_gate:
        nb = wbg.shape[1]
        in_specs.append(resident((d, nb)))
        out_shape.append(jax.ShapeDtypeStruct((m, nb), F32))
        out_specs.append(pl.BlockSpec((tm, nb), lambda i: (i, 0)))
        args.append(wbg)
    res = pl.pallas_call(
        functools.partial(_norm_matmul_kernel, with_gate=with_gate, tn=tn),
        out_shape=out_shape, grid=(m // tm,), in_specs=in_specs, out_specs=out_specs,
        scratch_shapes=[pltpu.VMEM((tm, d), BF16)],
        compiler_params=_params(("parallel",)),
        name="norm_matmul_gate" if with_gate else "norm_matmul",
    )(*args)
    return res if with_gate else res[0]


def _pack_a_kernel(w_ref, o_ref):
    o_ref[...] = w_ref[0].astype(BF16)


def _pack_b_kernel(w_ref, nxt_ref, o_ref, g_ref, *, shift, per_group):
    tiles = [w_ref[0, :, k * LANE:(k + 1) * LANE] for k in range(w_ref.shape[2] // LANE)] + [nxt_ref[0]]
    lane = lax.broadcasted_iota(jnp.int32, tiles[0].shape, 1)
    rolled = [pltpu.roll(t, LANE - shift, 1) for t in tiles]
    for k in range(len(tiles) - 1):
        o_ref[:, k * LANE:(k + 1) * LANE] = jnp.where(lane < LANE - shift, rolled[k], rolled[k + 1]).astype(BF16)

    @pl.when(pl.program_id(0) == 0)
    def _():
        for g in range(NSA_KV):
            src = tiles[0] if g == 0 else pltpu.roll(tiles[0], LANE - g * per_group, 1)
            g_ref[:, g * LANE:(g + 1) * LANE] = jnp.where(lane < per_group, src, 0.0).astype(BF16)


def _pack_in_proj_weights(w_in3, layer):
    _, d, n_in = w_in3.shape
    tn = 512
    wa = pl.pallas_call(
        _pack_a_kernel,
        out_shape=jax.ShapeDtypeStruct((d, ZA_WIDTH), BF16),
        grid=(ZA_WIDTH // tn,),
        in_specs=[pl.BlockSpec((1, d, tn), lambda j: (layer, 0, j))],
        out_specs=pl.BlockSpec((d, tn), lambda j: (0, j)),
        compiler_params=_params(("parallel",)),
        name="pack_w_a",
    )(w_in3)
    assert n_in == BG_SRC + BG_N + ZB_WIDTH and BG_SRC % tn == 0 and ZB_WIDTH % tn == 0
    per_group = NSA_HPG * N_BRANCH
    wb, wbg = pl.pallas_call(
        functools.partial(_pack_b_kernel, shift=BG_N, per_group=per_group),
        out_shape=[jax.ShapeDtypeStruct((d, ZB_WIDTH), BF16), jax.ShapeDtypeStruct((d, NSA_KV * LANE), BF16)],
        grid=(ZB_WIDTH // tn,),
        in_specs=[pl.BlockSpec((1, d, tn), lambda j: (layer, 0, BG_SRC // tn + j)),
                  pl.BlockSpec((1, d, LANE), lambda j: (layer, 0, (BG_SRC + tn) // LANE + j * (tn // LANE)))],
        out_specs=[pl.BlockSpec((d, tn), lambda j: (0, j)),
                   pl.BlockSpec((d, NSA_KV * LANE), lambda j: (0, 0))],
        compiler_params=_params(("arbitrary",)),
        name="pack_w_b",
    )(w_in3, w_in3)
    return wa, wb, wbg


def _kv_export_kernel(kv1_ref, kv2_ref, kv0_ref, c_ref, s1_ref, s2_ref, kvn_ref, kvb_ref, win_ref):
    c, s1, s2 = c_ref[...], s1_ref[...], s2_ref[...]
    kv0 = kv0_ref[0]
    kv1 = kv1_ref[0]
    kv2 = kv2_ref[0]
    tr = kv0.shape[0]
    n_kv, n_w, half = 4 * NSA_KV, 2 * NSA_KV, NSA_KV * LANE
    last = pl.program_id(1) == pl.num_programs(1) - 1
    for g in range(NSA_KV):
        lo, hi = g * LANE, (g + 1) * LANE
        ks = _rope(kv1[:, lo:hi], c, s1, s2)
        kw = _rope(kv2[:, lo:hi], c, s1, s2)
        vs = kv1[:, half + lo:half + hi]
        vw = kv2[:, half + lo:half + hi]
        kvn_ref[pl.ds(g, tr, stride=n_kv), :] = kv0[:, lo:hi]
        kvn_ref[pl.ds(NSA_KV + g, tr, stride=n_kv), :] = kv0[:, half + lo:half + hi]
        kvn_ref[pl.ds(2 * NSA_KV + g, tr, stride=n_kv), :] = ks
        kvn_ref[pl.ds(3 * NSA_KV + g, tr, stride=n_kv), :] = vs
        kvb_ref[0, :, lo:hi] = ks.astype(BF16)
        kvb_ref[0, :, half + lo:half + hi] = vs.astype(BF16)
        kvb_ref[0, :, 2 * half + lo:2 * half + hi] = kw.astype(BF16)
        kvb_ref[0, :, 3 * half + lo:3 * half + hi] = vw.astype(BF16)

        @pl.when(last)
        def _():
            win_ref[pl.ds(g, tr, stride=n_w), :] = kw
            win_ref[pl.ds(NSA_KV + g, tr, stride=n_w), :] = vw


def _kv_export(zb3, tabs, *, tr):
    b, t, _ = zb3.shape
    nt = t // tr
    kvblk = KV_OFF // 512
    n_kv, n_w = 4 * NSA_KV, 2 * NSA_KV
    zspec = lambda k: pl.BlockSpec((1, tr, 512), lambda bi, ti, k=k: (bi, ti, kvblk + k))
    tspec = pl.BlockSpec((tr, LANE), lambda bi, ti: (ti, 0))
    return pl.pallas_call(
        _kv_export_kernel,
        out_shape=[jax.ShapeDtypeStruct((b * t * n_kv, LANE), F32),
                   jax.ShapeDtypeStruct((b, t, n_kv * LANE), BF16),
                   jax.ShapeDtypeStruct((b * tr * n_w, LANE), F32)],
        grid=(b, nt),
        in_specs=[zspec(1), zspec(2), zspec(0), tspec, tspec, tspec],
        out_specs=[pl.BlockSpec((tr * n_kv, LANE), lambda bi, ti: (bi * nt + ti, 0)),
                   pl.BlockSpec((1, tr, n_kv * LANE), lambda bi, ti: (bi, ti, 0)),
                   pl.BlockSpec((tr * n_w, LANE), lambda bi, ti: (bi, 0))],
        compiler_params=_params(("parallel", "arbitrary")),
        name="kv_export",
    )(zb3, zb3, zb3, *tabs)


def _pool_kernel(*refs, n_prefetch):
    x_ref, a_ref, o_ref = refs[n_prefetch:]
    x = x_ref[0]
    rows = x.shape[0]
    x3 = x.reshape(rows // CMP_BLOCK, CMP_BLOCK, x.shape[1])
    pooled = jnp.sum(x3 * a_ref[...][None], axis=1)
    o_ref[...] = pooled.reshape(o_ref.shape)


def _pool_prompt(z3, a4):
    b, t, _ = z3.shape
    n = t // CMP_BLOCK
    return pl.pallas_call(
        functools.partial(_pool_kernel, n_prefetch=0),
        out_shape=jax.ShapeDtypeStruct((b, n, 512), F32),
        grid=(b,),
        in_specs=[pl.BlockSpec((1, t, 512), lambda bi: (bi, 0, KV_OFF // 512)),
                  pl.BlockSpec((CMP_BLOCK, 512), lambda bi: (0, 0))],
        out_specs=pl.BlockSpec((1, n, 512), lambda bi: (bi, 0, 0)),
        compiler_params=_params(("parallel",)),
        name="pool_prompt",
    )(z3, a4)


def _pool_pages_kernel(pt_sm, a_ref, cache_ref, o_ref, buf, sem, *, pages_per_step, page_base):
    step = pl.program_id(0)
    n_steps = pl.num_programs(0)
    slot = step % 2
    n_cols = buf.shape[3]

    def page_copy(step_idx, p, to_slot):
        page = pt_sm[step_idx * pages_per_step + p] + page_base
        return pltpu.make_async_copy(cache_ref.at[page, :, pl.ds(0, n_cols), :], buf.at[to_slot, p], sem.at[to_slot])

    @pl.when(step == 0)
    def _():
        for p in range(pages_per_step):
            page_copy(0, p, 0).start()

    @pl.when(step + 1 < n_steps)
    def _():
        for p in range(pages_per_step):
            page_copy(step + 1, p, 1 - slot).start()

    for p in range(pages_per_step):
        page_copy(step, p, slot).wait()

    a = a_ref[...]
    per = buf.shape[2] // CMP_BLOCK
    for p in range(pages_per_step):
        for k in range(per):
            x = buf[slot, p, pl.ds(k * CMP_BLOCK, CMP_BLOCK)]
            o_ref[0, p, k] = jnp.sum(x * a, axis=0)


def _pool_pages(cache4, page_flat, a3, *, bs, n_pages, page_base, pages_per_step=16):
    page = cache4.shape[1]
    per = page // CMP_BLOCK
    n_cols = 2 * NSA_KV
    total = bs * n_pages
    pages_per_step = min(pages_per_step, total)
    assert total % pages_per_step == 0
    n_steps = total // pages_per_step
    grid_spec = pltpu.PrefetchScalarGridSpec(
        num_scalar_prefetch=1, grid=(n_steps,),
        in_specs=[pl.BlockSpec((CMP_BLOCK, n_cols, LANE), lambda si, pt: (0, 0, 0)),
                  pl.BlockSpec(memory_space=pl.ANY)],
        out_specs=pl.BlockSpec((1, pages_per_step, per, n_cols, LANE), lambda si, pt: (si, 0, 0, 0, 0)),
        scratch_shapes=[pltpu.VMEM((2, pages_per_step, page, n_cols, LANE), F32),
                        pltpu.SemaphoreType.DMA((2,))])
    out = pl.pallas_call(
        functools.partial(_pool_pages_kernel, pages_per_step=pages_per_step, page_base=page_base),
        out_shape=jax.ShapeDtypeStruct((n_steps, pages_per_step, per, n_cols, LANE), F32),
        grid_spec=grid_spec,
        compiler_params=_params(("arbitrary",)),
        name="pool_pages",
    )(page_flat, a3, cache4)
    return out.reshape(bs, n_pages * per, n_cols * LANE)


def _cmp_proj_kernel(p_ref, w_ref, kc_ref, vc_ref):
    pooled = p_ref[0]
    n = pooled.shape[0]
    n_pad = kc_ref.shape[2]
    for c in range(4):
        r = _nn(pooled[:, c * LANE:(c + 1) * LANE].astype(BF16), w_ref[c]).astype(BF16)
        dst = kc_ref if c < 2 else vc_ref
        if n_pad > n:
            dst[0, c % 2] = jnp.zeros((n_pad, LANE), BF16)
        dst[0, c % 2, 0:n, :] = r


def _cmp_proj(pooled, w4):
    b, n, _ = pooled.shape
    n_pad = -(-n // LANE) * LANE
    spec = pl.BlockSpec((1, NSA_KV, n_pad, LANE), lambda bi: (bi, 0, 0, 0))
    return pl.pallas_call(
        _cmp_proj_kernel,
        out_shape=[jax.ShapeDtypeStruct((b, NSA_KV, n_pad, LANE), BF16)] * 2,
        grid=(b,),
        in_specs=[pl.BlockSpec((1, n, 512), lambda bi: (bi, 0, 0)),
                  pl.BlockSpec((4, LANE, LANE), lambda bi: (0, 0, 0))],
        out_specs=[spec, spec],
        compiler_params=_params(("parallel",)),
        name="cmp_proj",
    )(pooled, w4)


def _lane_parts(x):
    return [x[:, j * LANE:(j + 1) * LANE] for j in range(x.shape[1] // LANE)]


def _nsa_prompt_kernel(q_ref, ng_ref, bg_ref, ksel_ref, vsel_ref, kwin_ref, vwin_ref, kc_ref, vc_ref,
                       c_ref, s1_ref, s2_ref, o_ref, qc_scr, qr_scr, s_scr, p_scr, a_scr, m_scr, l_scr, acc_scr, oc_scr,
                       *, t_len, tq, tc, tw, n_sel, top):
    i = pl.program_id(2)
    hq = NSA_HPG
    q = q_ref[0]
    c, s1, s2 = c_ref[...], s1_ref[...], s2_ref[...]
    for h in range(hq):
        qh = q[:, h * LANE:(h + 1) * LANE]
        qc_scr[pl.ds(h * tq, tq), :] = (qh * SCALE_LOG2).astype(BF16)
        qr_scr[pl.ds(h * tq, tq), :] = (_rope(qh, c, s1, s2) * SCALE_LOG2).astype(BF16)

    kc = kc_ref[0, 0]
    npad = kc.shape[0]
    s_scr[:, 0:npad] = _nt(qc_scr[...], kc)
    tpos = i * tq + lax.broadcasted_iota(jnp.int32, (tq, npad), 0)
    ncol = lax.broadcasted_iota(jnp.int32, (tq, npad), 1)
    cmask = (ncol + 1) * CMP_BLOCK <= tpos + 1
    imp = jnp.zeros((tq, npad), F32)
    for h in range(hq):
        r = pl.ds(h * tq, tq)
        s = jnp.where(cmask, s_scr[r, 0:npad], MASK_NEG)
        e = jnp.where(cmask, jnp.exp2(s - jnp.max(s, axis=-1, keepdims=True)), 0.0)
        p = e / jnp.maximum(jnp.sum(e, axis=-1, keepdims=True), 1e-30)
        p_scr[r, 0:npad] = p.astype(BF16)
        imp = imp + p
    oc_scr[...] = _nn(p_scr[:, 0:npad], vc_ref[0, 0])

    rows = min(npad, -(-n_sel // 8) * 8)
    imp_t = imp.T[0:rows]
    blk = lax.broadcasted_iota(jnp.int32, (rows, tq), 0)
    cur = (i * tq + lax.broadcasted_iota(jnp.int32, (rows, tq), 1)) // SEL_BLOCK
    imp_t = jnp.where((blk == 0) | (blk == cur) | (blk == cur - 1), FORCE, imp_t)
    imp_t = jnp.where(blk > cur, -1.0, imp_t)
    imp_t = jnp.where(blk >= n_sel, -2.0, imp_t)
    rank = jnp.zeros((rows, tq), F32)
    for j in range(n_sel):
        a = imp_t[j:j + 1, :]
        ahead = (a > imp_t) | ((a == imp_t) & (blk > j))
        rank = rank + jnp.where(ahead, 1.0, 0.0)
    sel_t = jnp.where((rank < top) & (blk < n_sel), 1.0, 0.0)
    if npad > rows:
        sel_t = jnp.concatenate([sel_t, jnp.zeros((npad - rows, tq), F32)], axis=0)
    sel = sel_t.T.astype(BF16)

    m_scr[...] = jnp.full(m_scr.shape, MASK_NEG, F32)
    l_scr[...] = jnp.zeros(l_scr.shape, F32)
    acc_scr[...] = jnp.zeros(acc_scr.shape, F32)
    n_chunks = (i * tq + tq + tc - 1) // tc

    def chunk(ci, carry):
        k0 = pl.multiple_of(ci * tc, tc)
        k = ksel_ref[0, pl.ds(k0, tc), :]
        v = vsel_ref[0, pl.ds(k0, tc), :]
        jb = lax.broadcasted_iota(jnp.int32, (npad, tc), 0)
        kb = (k0 + lax.broadcasted_iota(jnp.int32, (npad, tc), 1)) // SEL_BLOCK
        expand = jnp.where(jb == kb, 1.0, 0.0).astype(BF16)
        chosen = _nn(sel, expand)
        kp = k0 + lax.broadcasted_iota(jnp.int32, (tq, tc), 1)
        tp = i * tq + lax.broadcasted_iota(jnp.int32, (tq, tc), 0)
        bias = jnp.where((chosen > 0.5) & (kp <= tp), 0.0, MASK_NEG)
        s_scr[:, 0:tc] = _nt(qr_scr[...], k)
        for h in range(hq):
            r = pl.ds(h * tq, tq)
            parts = _lane_parts(s_scr[r, 0:tc] + bias)
            m_prev = m_scr[r, :]
            m_new = jnp.maximum(m_prev, jnp.max(functools.reduce(jnp.maximum, parts), axis=-1, keepdims=True))
            alpha = jnp.exp2(m_prev - m_new)
            pes = [jnp.exp2(x - m_new) for x in parts]
            l_scr[r, :] = alpha * l_scr[r, :] + functools.reduce(jnp.add, pes)
            p_scr[r, 0:tc] = jnp.concatenate(pes, axis=1).astype(BF16)
            a_scr[r, :] = alpha
            m_scr[r, :] = m_new
        acc_scr[...] = a_scr[...] * acc_scr[...] + _nn(p_scr[:, 0:tc], v)
        return carry

    lax.fori_loop(0, n_chunks, chunk, 0)

    w0 = pl.multiple_of(jnp.clip(i * tq + tq - tw, 0, t_len - tw), LANE)
    kp = w0 + lax.broadcasted_iota(jnp.int32, (tq, tw), 1)
    tp = i * tq + lax.broadcasted_iota(jnp.int32, (tq, tw), 0)
    wbias = jnp.where((kp <= tp) & (kp > tp - WINDOW), 0.0, MASK_NEG)
    s_scr[:, 0:tw] = _nt(qr_scr[...], kwin_ref[0, pl.ds(w0, tw), :])
    for h in range(hq):
        r = pl.ds(h * tq, tq)
        parts = _lane_parts(s_scr[r, 0:tw] + wbias)
        m = jnp.max(functools.reduce(jnp.maximum, parts), axis=-1, keepdims=True)
        ews = [jnp.exp2(x - m) for x in parts]
        a_scr[r, :] = jnp.broadcast_to(jnp.sum(functools.reduce(jnp.add, ews), axis=-1, keepdims=True), (tq, LANE))
        p_scr[r, 0:tw] = jnp.concatenate(ews, axis=1).astype(BF16)
    o_w = _nn(p_scr[:, 0:tw], vwin_ref[0, pl.ds(w0, tw), :])

    gate = jax.nn.sigmoid(bg_ref[0])
    ng = ng_ref[0]
    for h in range(hq):
        r = slice(h * tq, (h + 1) * tq)
        o_s = acc_scr[r, :] / jnp.sum(l_scr[r, :], axis=-1, keepdims=True)
        o = (gate[:, 3 * h:3 * h + 1] * oc_scr[r, :] + gate[:, 3 * h + 1:3 * h + 2] * o_s
             + gate[:, 3 * h + 2:3 * h + 3] * (o_w[r] / a_scr[r, :]))
        o_ref[0, :, h * LANE:(h + 1) * LANE] = (_silu(ng[:, h * LANE:(h + 1) * LANE]) * o).astype(o_ref.dtype)


def _nsa_prompt(z3, bg3, kvb, kc, vc, tabs, *, tq=128, tc=512):
    b, t, _ = z3.shape
    nq = t // tq
    tc = min(tc, t)
    tw = min(WINDOW + tq, t)
    n_sel = t // SEL_BLOCK
    top = min(TOP_N, n_sel)
    npad = kc.shape[2]
    gw = NSA_HPG * LANE
    rows = NSA_HPG * tq
    wide = max(tc, tw, npad)
    kvspec = lambda k: pl.BlockSpec((1, t, LANE), lambda bi, gi, qi, k=k: (bi, 0, k + gi))
    cspec = pl.BlockSpec((1, 1, npad, LANE), lambda bi, gi, qi: (bi, gi, 0, 0))
    tspec = pl.BlockSpec((tq, LANE), lambda bi, gi, qi: (qi, 0))
    kern = functools.partial(_nsa_prompt_kernel, t_len=t, tq=tq, tc=tc, tw=tw, n_sel=n_sel, top=top)
    return pl.pallas_call(
        kern,
        out_shape=jax.ShapeDtypeStruct((b, t, NSA_HEADS * LANE), BF16),
        grid=(b, NSA_KV, nq),
        in_specs=[pl.BlockSpec((1, tq, gw), lambda bi, gi, qi: (bi, qi, Q_OFF // gw + gi)),
                  pl.BlockSpec((1, tq, gw), lambda bi, gi, qi: (bi, qi, NG_OFF // gw + gi)),
                  pl.BlockSpec((1, tq, LANE), lambda bi, gi, qi: (bi, qi, gi)),
                  kvspec(0), kvspec(2), kvspec(4), kvspec(6), cspec, cspec, tspec, tspec, tspec],
        out_specs=pl.BlockSpec((1, tq, gw), lambda bi, gi, qi: (bi, qi, gi)),
        scratch_shapes=([pltpu.VMEM((rows, LANE), BF16)] * 2
                        + [pltpu.VMEM((rows, wide), F32), pltpu.VMEM((rows, wide), BF16)]
                        + [pltpu.VMEM((rows, LANE), F32)] * 5),
        compiler_params=_params(("parallel", "parallel", "arbitrary")),
        name="nsa_prompt",
    )(z3, z3, bg3, kvb, kvb, kvb, kvb, kc, vc, *tabs)


def _conv_kernel(h_ref, b_ref, c_ref, g_ref, prev_ref, w_ref, y_ref, st_ref, up_scr, *, t_real):
    u = c_ref[0] * h_ref[0]
    t = u.shape[0]
    up_scr[pl.ds(8 - (CONV_W - 1), CONV_W - 1), :] = prev_ref[0]
    up_scr[pl.ds(8, t), :] = u
    w = w_ref[...]
    y = w[0:1, :] * up_scr[pl.ds(6, t), :]
    y = y + w[1:2, :] * up_scr[pl.ds(7, t), :]
    y = y + w[2:3, :] * u
    y = b_ref[0] * y
    y_ref[0] = (_silu(g_ref[0]) * y).astype(y_ref.dtype)
    st_ref[0] = up_scr[pl.ds(6 + t_real, CONV_W - 1), :]


def _conv_mixer(z3, prev, w_conv, *, t_real):
    b, t, _ = z3.shape
    nc = CONV_DIM // LANE
    zspec = lambda off: pl.BlockSpec((1, t, LANE), lambda bi, ci, off=off: (bi, 0, off // LANE + ci))
    return pl.pallas_call(
        functools.partial(_conv_kernel, t_real=t_real),
        out_shape=[jax.ShapeDtypeStruct((b, t, CONV_DIM), BF16),
                   jax.ShapeDtypeStruct((b, CONV_W - 1, CONV_DIM), F32)],
        grid=(b, nc),
        in_specs=[zspec(C_H), zspec(C_B), zspec(C_C), zspec(C_G),
                  pl.BlockSpec((1, CONV_W - 1, LANE), lambda bi, ci: (bi, 0, ci)),
                  pl.BlockSpec((CONV_W, LANE), lambda bi, ci: (0, ci))],
        out_specs=[pl.BlockSpec((1, t, LANE), lambda bi, ci: (bi, 0, ci)),
                   pl.BlockSpec((1, CONV_W - 1, LANE), lambda bi, ci: (bi, 0, ci))],
        scratch_shapes=[pltpu.VMEM((t + 8, LANE), F32)],
        compiler_params=_params(("parallel", "parallel")),
        name="conv_mixer",
    )(z3, z3, z3, z3, prev, w_conv)


def _mem_attn_kernel(q_ref, mg_ref, kv_ref, o_ref, *, interleaved):
    q = q_ref[0]
    mg = mg_ref[0]
    half = MEM_HEADS * LANE
    for h in range(MEM_HEADS):
        lo, hi = h * LANE, (h + 1) * LANE
        if interleaved:
            nm = kv_ref.shape[1] // (2 * MEM_HEADS)
            k = kv_ref[0, pl.ds(h, nm, stride=2 * MEM_HEADS), :].astype(BF16)
            v = kv_ref[0, pl.ds(MEM_HEADS + h, nm, stride=2 * MEM_HEADS), :].astype(BF16)
        else:
            k = kv_ref[0, :, lo:hi].astype(BF16)
            v = kv_ref[0, :, half + lo:half + hi].astype(BF16)
        s = _nt((q[:, lo:hi] * ATTN_SCALE).astype(BF16), k)
        e = jnp.exp(s - jnp.max(s, axis=-1, keepdims=True))
        o = _nn(e.astype(BF16), v) / jnp.sum(e, axis=-1, keepdims=True)
        o_ref[0, :, lo:hi] = (_silu(mg[:, lo:hi]) * o).astype(o_ref.dtype)


def _mem_attn(zb3, mkv, *, tq, interleaved, kv_base=0):
    b, t, _ = zb3.shape
    wq = MEM_HEADS * LANE
    return pl.pallas_call(
        functools.partial(_mem_attn_kernel, interleaved=interleaved),
        out_shape=jax.ShapeDtypeStruct((b, t, wq), BF16),
        grid=(b, t // tq),
        in_specs=[pl.BlockSpec((1, tq, wq), lambda bi, ti: (bi, ti, MQ_OFF // wq)),
                  pl.BlockSpec((1, tq, wq), lambda bi, ti: (bi, ti, MG_OFF // wq)),
                  pl.BlockSpec((1,) + mkv.shape[1:], lambda bi, ti: (bi + kv_base, 0, 0))],
        out_specs=pl.BlockSpec((1, tq, wq), lambda bi, ti: (bi, ti, 0)),
        compiler_params=_params(("parallel", "parallel")),
        name="mem_attn",
    )(zb3, zb3, mkv)


def _out_proj_kernel(x_ref, ya_ref, yb_ref, ym_ref, w_ref, fg_ref, o_ref, *, final):
    a, bw = CONV_DIM, CONV_DIM + NSA_HEADS * LANE
    acc = _nn(ya_ref[...], w_ref[0:a, :])
    acc = acc + _nn(yb_ref[...], w_ref[a:bw, :])
    acc = acc + _nn(ym_ref[...], w_ref[bw:, :])
    r = x_ref[...] + acc
    if final:
        r = r * lax.rsqrt(jnp.mean(r * r, axis=-1, keepdims=True) + NORM_EPS) * fg_ref[...]
    o_ref[...] = r


def _out_proj(x, ya, yb, ym, w, fg, *, tm, final):
    m, d = x.shape
    row = lambda width: pl.BlockSpec((tm, width), lambda i: (i, 0))
    return pl.pallas_call(
        functools.partial(_out_proj_kernel, final=final),
        out_shape=jax.ShapeDtypeStruct((m, d), F32),
        grid=(m // tm,),
        in_specs=[row(d), row(ya.shape[1]), row(yb.shape[1]), row(ym.shape[1]),
                  pl.BlockSpec(w.shape, lambda i: (0, 0)),
                  pl.BlockSpec((1, d), lambda i: (0, 0))],
        out_specs=row(d),
        compiler_params=_params(("parallel",)),
        name="out_proj",
    )(x, ya, yb, ym, w, fg.reshape(1, d))


def _sample_pre_kernel(q_ref, ng_ref, bg_ref, kv0_ref, kv1_ref, kv2_ref, c_ref, s1_ref, s2_ref, kc_ref, vc_ref,
                       kvn_ref, wn_ref, qr_ref, oc_ref, ngo_ref, gate_ref, val_ref, *, past):
    tp = SAMPLE_T_PAD
    hq = NSA_HPG
    c, s1, s2 = c_ref[...], s1_ref[...], s2_ref[...]
    kv0, kv1, kv2 = kv0_ref[0], kv1_ref[0], kv2_ref[0]
    kvn_ref[0, :, 0:512] = kv0
    kvn_ref[0, :, 768:1024] = kv1[:, 256:512]
    wn_ref[0, :, 256:512] = kv2[:, 256:512]
    for g in range(NSA_KV):
        lo, hi = g * LANE, (g + 1) * LANE
        kvn_ref[0, :, 512 + lo:512 + hi] = _rope(kv1[:, lo:hi], c, s1, s2)
        wn_ref[0, :, lo:hi] = _rope(kv2[:, lo:hi], c, s1, s2)

    q = q_ref[0]
    ng = ng_ref[0]
    gates = jax.nn.sigmoid(bg_ref[0])
    for g in range(NSA_KV):
        qc_l, qr_l = [], []
        for h in range(hq):
            lo = (g * hq + h) * LANE
            qh = q[:, lo:lo + LANE]
            qc_l.append(qh * ATTN_SCALE)
            qr_l.append(_rope(qh, c, s1, s2) * ATTN_SCALE)
            ngo_ref[0, g, h * tp:(h + 1) * tp, :] = ng[:, lo:lo + LANE]
            for br in range(N_BRANCH):
                col = g * LANE + h * N_BRANCH + br
                gate_ref[0, g, br, h * tp:(h + 1) * tp, :] = jnp.broadcast_to(gates[:, col:col + 1], (tp, LANE))
        qc = jnp.concatenate(qc_l, axis=0)
        qr_ref[0, g] = jnp.concatenate(qr_l, axis=0)

        kc = kc_ref[0, g]
        npad = kc.shape[0]
        s = _nt(qc.astype(BF16), kc)
        trow = lax.broadcasted_iota(jnp.int32, (hq * tp, npad), 0) % tp
        ncol = lax.broadcasted_iota(jnp.int32, (hq * tp, npad), 1)
        cmask = (ncol + 1) * CMP_BLOCK <= past + trow + 1
        s = jnp.where(cmask, s, MASK_NEG)
        e = jnp.where(cmask, jnp.exp(s - jnp.max(s, axis=-1, keepdims=True)), 0.0)
        p = e / jnp.maximum(jnp.sum(e, axis=-1, keepdims=True), 1e-30)
        oc_ref[0, g] = _nn(p.astype(BF16), vc_ref[0, g])
        imp = jnp.sum(p.reshape(hq, tp, npad), axis=0)

        blk = lax.broadcasted_iota(jnp.int32, (tp, npad), 1)
        cur = (past + lax.broadcasted_iota(jnp.int32, (tp, npad), 0)) // SEL_BLOCK
        val = jnp.where((blk == 0) | (blk == cur) | (blk == cur - 1), FORCE, imp)
        val = jnp.where(blk > cur, -1.0, val)
        val = jnp.where(blk >= past // SEL_BLOCK, -2.0, val)
        val_ref[0, g] = val


def _topk_kernel(val_ref, idx_ref, *, n_top):
    val = val_ref[...]
    rows, n = val.shape
    blk = lax.broadcasted_iota(jnp.int32, (rows, n), 1)
    lane = lax.broadcasted_iota(jnp.int32, (rows, LANE), 1)
    idx = jnp.zeros((rows, LANE), jnp.int32)
    for r in range(n_top):
        best = jnp.max(val, axis=-1, keepdims=True)
        j = jnp.min(jnp.where(val == best, blk, n), axis=-1, keepdims=True)
        idx = jnp.where(lane == r, j, idx)
        val = jnp.where(blk == j, -3e38, val)
    idx_ref[...] = idx


def _topk(val2, *, n_top):
    rows, n = val2.shape
    return pl.pallas_call(
        functools.partial(_topk_kernel, n_top=n_top),
        out_shape=jax.ShapeDtypeStruct((rows, LANE), jnp.int32),
        grid=(1,),
        in_specs=[pl.BlockSpec((rows, n), lambda i: (0, 0))],
        out_specs=pl.BlockSpec((rows, LANE), lambda i: (0, 0)),
        compiler_params=_params(("arbitrary",)),
        name="sample_topk",
    )(val2)


def _sample_pre(za3, zb3, bg3, tabs, kc, vc, *, past):
    bs, tp, _ = za3.shape
    npad = kc.shape[2]
    qw = NSA_HEADS * LANE
    kvblk = KV_OFF // 512
    zspec = lambda k: pl.BlockSpec((1, tp, 512), lambda bi, k=k: (bi, 0, kvblk + k))
    tspec = pl.BlockSpec((tp, LANE), lambda bi: (0, 0))
    cspec = pl.BlockSpec((1, NSA_KV, npad, LANE), lambda bi: (bi, 0, 0, 0))
    rows = NSA_HPG * tp
    gspec = pl.BlockSpec((1, NSA_KV, rows, LANE), lambda bi: (bi, 0, 0, 0))
    gshape = jax.ShapeDtypeStruct((bs, NSA_KV, rows, LANE), F32)
    return pl.pallas_call(
        functools.partial(_sample_pre_kernel, past=past),
        out_shape=[jax.ShapeDtypeStruct((bs, tp, 1024), F32),
                   jax.ShapeDtypeStruct((bs, tp, 512), F32),
                   gshape, gshape, gshape,
                   jax.ShapeDtypeStruct((bs, NSA_KV, N_BRANCH, rows, LANE), F32),
                   jax.ShapeDtypeStruct((bs, NSA_KV, tp, npad), F32)],
        grid=(bs,),
        in_specs=[pl.BlockSpec((1, tp, qw), lambda bi: (bi, 0, Q_OFF // qw)),
                  pl.BlockSpec((1, tp, qw), lambda bi: (bi, 0, NG_OFF // qw)),
                  pl.BlockSpec((1, tp, NSA_KV * LANE), lambda bi: (bi, 0, 0)),
                  zspec(0), zspec(1), zspec(2), tspec, tspec, tspec, cspec, cspec],
        out_specs=[pl.BlockSpec((1, tp, 1024), lambda bi: (bi, 0, 0)),
                   pl.BlockSpec((1, tp, 512), lambda bi: (bi, 0, 0)),
                   gspec, gspec, gspec,
                   pl.BlockSpec((1, NSA_KV, N_BRANCH, rows, LANE), lambda bi: (bi, 0, 0, 0, 0)),
                   pl.BlockSpec((1, NSA_KV, tp, npad), lambda bi: (bi, 0, 0, 0))],
        compiler_params=_params(("parallel",)),
        name="sample_pre",
    )(za3, za3, bg3, zb3, zb3, zb3, *tabs, kc, vc)


def _sample_attn_kernel(idx_sm, pt_sm, qr_ref, oc_ref, ng_ref, gate_ref, ksn_ref, vsn_ref,
                        wc_ref, kwn_ref, vwn_ref, cache_ref, o_ref,
                        kbuf, vbuf, kw_scr, vw_scr, sem, *, ts, n_top, n_pages, page_base, per_page, wb):
    tp = SAMPLE_T_PAD
    hq = NSA_HPG
    b = pl.program_id(0)
    g = pl.program_id(1)
    n_gath = n_top * SEL_BLOCK
    ks_rows = kbuf.shape[1]

    def gather_copies(t, r):
        blk = idx_sm[((b * NSA_KV + g) * ts + t) * n_top + r]
        page = pt_sm[b * n_pages + blk // per_page] + page_base
        row0 = (blk % per_page) * SEL_BLOCK
        src_k = cache_ref.at[page, pl.ds(row0, SEL_BLOCK), 2 * NSA_KV + g]
        src_v = cache_ref.at[page, pl.ds(row0, SEL_BLOCK), 3 * NSA_KV + g]
        dst = pl.ds(r * SEL_BLOCK, SEL_BLOCK)
        return (pltpu.make_async_copy(src_k, kbuf.at[t, dst], sem.at[0]),
                pltpu.make_async_copy(src_v, vbuf.at[t, dst], sem.at[1]))

    for t in range(ts):
        for r in range(n_top):
            ck, cv = gather_copies(t, r)
            ck.start()
            cv.start()

    qr = qr_ref[0, 0].astype(BF16)
    trow = lax.broadcasted_iota(jnp.int32, (hq * tp, 1), 0) % tp

    ww = kw_scr.shape[0]
    kw_scr[pl.ds(0, wb), :] = wc_ref[0, pl.ds(g, wb, stride=2 * NSA_KV), :]
    vw_scr[pl.ds(0, wb), :] = wc_ref[0, pl.ds(NSA_KV + g, wb, stride=2 * NSA_KV), :]
    kw_scr[pl.ds(wb, tp), :] = kwn_ref[0]
    vw_scr[pl.ds(wb, tp), :] = vwn_ref[0]
    kw_scr[pl.ds(wb + tp, ww - wb - tp), :] = jnp.zeros((ww - wb - tp, LANE), F32)
    vw_scr[pl.ds(wb + tp, ww - wb - tp), :] = jnp.zeros((ww - wb - tp, LANE), F32)
    sw = _nt(qr, kw_scr[...].astype(BF16))
    jw = lax.broadcasted_iota(jnp.int32, (hq * tp, ww), 1)
    rel = jw - wb
    okw = (rel <= trow) & (rel > trow - WINDOW) & (jw < wb + ts)
    sw = jnp.where(okw, sw, MASK_NEG)
    ew = jnp.where(okw, jnp.exp(sw - jnp.max(sw, axis=-1, keepdims=True)), 0.0)
    o_w = _nn(ew.astype(BF16), vw_scr[...].astype(BF16)) / jnp.sum(ew, axis=-1, keepdims=True)

    for t in range(ts):
        for r in range(n_top):
            ck, cv = gather_copies(t, r)
            ck.wait()
            cv.wait()

    js = lax.broadcasted_iota(jnp.int32, (hq * tp, ks_rows), 1)
    o_s = jnp.zeros((hq * tp, LANE), F32)
    for t in range(ts):
        kbuf[t, pl.ds(n_gath, tp), :] = ksn_ref[0]
        vbuf[t, pl.ds(n_gath, tp), :] = vsn_ref[0]
        kbuf[t, pl.ds(n_gath + tp, ks_rows - n_gath - tp), :] = jnp.zeros((ks_rows - n_gath - tp, LANE), F32)
        vbuf[t, pl.ds(n_gath + tp, ks_rows - n_gath - tp), :] = jnp.zeros((ks_rows - n_gath - tp, LANE), F32)
        ss = _nt(qr, kbuf[t].astype(BF16))
        oks = (js < n_gath) | ((js - n_gath <= t) & (js < n_gath + ts))
        ss = jnp.where(oks, ss, MASK_NEG)
        es = jnp.where(oks, jnp.exp(ss - jnp.max(ss, axis=-1, keepdims=True)), 0.0)
        ot = _nn(es.astype(BF16), vbuf[t].astype(BF16)) / jnp.sum(es, axis=-1, keepdims=True)
        o_s = jnp.where(trow == t, ot, o_s)

    o = gate_ref[0, 0, 0] * oc_ref[0, 0] + gate_ref[0, 0, 1] * o_s + gate_ref[0, 0, 2] * o_w
    y = _silu(ng_ref[0, 0]) * o
    for h in range(hq):
        o_ref[0, :, h * LANE:(h + 1) * LANE] = y[h * tp:(h + 1) * tp].astype(o_ref.dtype)


def _sample_attn(idx_flat, page_flat, qr, oc, ngo, gates, kvn, cache_win_rows, wn, cache4, *,
                 ts, n_top, n_pages, page_base, win_base):
    bs = qr.shape[0]
    tp = SAMPLE_T_PAD
    rows = NSA_HPG * tp
    wb = cache_win_rows.shape[1] // (2 * NSA_KV)
    per_page = cache4.shape[1] // SEL_BLOCK
    ks_rows = -(-(n_top * SEL_BLOCK + tp) // LANE) * LANE
    ww = -(-(wb + tp) // LANE) * LANE
    gspec = pl.BlockSpec((1, 1, rows, LANE), lambda bi, gi, *_: (bi, gi, 0, 0))
    newspec = lambda k: pl.BlockSpec((1, tp, LANE), lambda bi, gi, *_, k=k: (bi, 0, k + gi))
    grid_spec = pltpu.PrefetchScalarGridSpec(
        num_scalar_prefetch=2, grid=(bs, NSA_KV),
        in_specs=[gspec, gspec, gspec,
                  pl.BlockSpec((1, 1, N_BRANCH, rows, LANE), lambda bi, gi, *_: (bi, gi, 0, 0, 0)),
                  newspec(2 * NSA_KV), newspec(3 * NSA_KV),
                  pl.BlockSpec((1,) + cache_win_rows.shape[1:], lambda bi, gi, *_: (bi + win_base, 0, 0)),
                  newspec(0), newspec(NSA_KV),
                  pl.BlockSpec(memory_space=pl.ANY)],
        out_specs=pl.BlockSpec((1, tp, NSA_HPG * LANE), lambda bi, gi, *_: (bi, 0, gi)),
        scratch_shapes=[pltpu.VMEM((ts, ks_rows, LANE), F32), pltpu.VMEM((ts, ks_rows, LANE), F32),
                        pltpu.VMEM((ww, LANE), F32), pltpu.VMEM((ww, LANE), F32),
                        pltpu.SemaphoreType.DMA((2,))])
    kern = functools.partial(_sample_attn_kernel, ts=ts, n_top=n_top, n_pages=n_pages,
                             page_base=page_base, per_page=per_page, wb=wb)
    return pl.pallas_call(
        kern,
        out_shape=jax.ShapeDtypeStruct((bs, tp, NSA_HEADS * LANE), BF16),
        grid_spec=grid_spec,
        compiler_params=_params(("arbitrary", "arbitrary")),
        name="sample_attn",
    )(idx_flat, page_flat, qr, oc, ngo, gates, kvn, kvn, cache_win_rows, wn, wn, cache4)


def _rope_tables(pos, rows):
    freqs = jnp.power(ROPE_THETA, -jnp.arange(ROPE_HALF, dtype=F32) * (2.0 / ROPE_DIM))
    ang = pos.astype(F32)[:, None] * freqs[None, :]
    cos, sin = jnp.cos(ang), jnp.sin(ang)
    n = pos.shape[0]
    z16 = jnp.zeros((n, ROPE_HALF), F32)
    rest = LANE - ROPE_DIM
    c = jnp.concatenate([cos, cos, jnp.ones((n, rest), F32)], axis=1)
    s1 = jnp.concatenate([z16, sin, jnp.zeros((n, rest), F32)], axis=1)
    s2 = jnp.concatenate([-sin, z16, jnp.zeros((n, rest), F32)], axis=1)
    pad = lambda a: jnp.pad(a, ((0, rows - n), (0, 0)))
    return pad(c), pad(s1), pad(s2)


def _layer_weights(norm_g, w_in3, layer, w_conv, a_cmp, w_cmp, w_out):
    w_a, w_b, wbg = _pack_in_proj_weights(w_in3, layer)
    a4 = jnp.concatenate([a_cmp[0], a_cmp[0], a_cmp[1], a_cmp[1]], axis=1)
    a3 = jnp.stack([a_cmp[0], a_cmp[0], a_cmp[1], a_cmp[1]], axis=1)
    w4 = jnp.stack([w_cmp[0], w_cmp[0], w_cmp[1], w_cmp[1]]).astype(BF16)
    return dict(norm_g=norm_g, w_a=w_a, w_b=w_b, wbg=wbg, w_conv=w_conv, a4=a4, a3=a3, w4=w4,
                w_out=w_out.astype(BF16))


def _in_proj(x2, lw, *, tm):
    za, bg = _norm_matmul(x2, lw["norm_g"], lw["w_a"], lw["wbg"], tm=tm, tn=512)
    zb = _norm_matmul(x2, lw["norm_g"], lw["w_b"], tm=tm, tn=512)
    return za, zb, bg


def _prompt_layer(xp, mem_prompt, mem_norm_g, w_mem, lw, final_g, final):
    b, t, d = xp.shape
    m = b * t
    za, zb, bg = _in_proj(xp.reshape(m, d), lw, tm=min(512, m))
    za3 = za.reshape(b, t, ZA_WIDTH)
    zb3 = zb.reshape(b, t, ZB_WIDTH)
    bg3 = bg.reshape(b, t, NSA_KV * LANE)
    nm = mem_prompt.shape[1]
    mkv = _norm_matmul(mem_prompt.reshape(b * nm, d), mem_norm_g, w_mem, tm=min(512, b * nm), tn=512)
    mkv3 = mkv.reshape(b, nm, 2 * MEM_HEADS * LANE)
    tabs = _rope_tables(jnp.arange(t, dtype=jnp.int32), t)
    wb = min(WINDOW, t)
    kvn, kvb, win_rows = _kv_export(zb3, tabs, tr=wb)
    kc, vc = _cmp_proj(_pool_prompt(zb3, lw["a4"]), lw["w4"])
    yb = _nsa_prompt(za3, bg3, kvb, kc, vc, tabs, tq=min(256, t))
    conv0 = jnp.zeros((b, CONV_W - 1, CONV_DIM), F32)
    ya, conv_new = _conv_mixer(za3, conv0, lw["w_conv"], t_real=t)
    ym = _mem_attn(zb3, mkv3, tq=min(512, t), interleaved=False)
    out = _out_proj(xp.reshape(m, d), ya.reshape(m, -1), yb.reshape(m, -1), ym.reshape(m, -1),
                    lw["w_out"], final_g, tm=min(512, m), final=final)
    kv_new = kvn.reshape(b, t, 4, NSA_KV, HEAD_DIM)
    win_new = win_rows.reshape(b, wb, 2, NSA_KV, HEAD_DIM)
    mem_kv = mkv.reshape(b, nm, 2, MEM_HEADS, HEAD_DIM)
    return out.reshape(b, t, d), kv_new, win_new, conv_new, mem_kv


def _sample_layer(xs_p, ts, layer, cache_kv, cache_win, state_conv, cache_mem, page_table, lw, final_g, final):
    bs, tp, d = xs_p.shape
    depth, pool, page = cache_kv.shape[0], cache_kv.shape[1], cache_kv.shape[2]
    n_pages = page_table.shape[1]
    past = n_pages * page
    assert past % SEL_BLOCK == 0 and ts <= SEL_BLOCK and ts <= tp
    n_past = past // SEL_BLOCK
    n_top = min(TOP_N, n_past + 1) - 1
    m = bs * tp
    za, zb, bg = _in_proj(xs_p.reshape(m, d), lw, tm=m)
    za3 = za.reshape(bs, tp, ZA_WIDTH)
    zb3 = zb.reshape(bs, tp, ZB_WIDTH)
    bg3 = bg.reshape(bs, tp, NSA_KV * LANE)
    tabs = _rope_tables(past + jnp.arange(tp, dtype=jnp.int32), tp)
    page_flat = page_table.reshape(-1).astype(jnp.int32)
    cache4 = cache_kv.reshape(depth * pool, page, 4 * NSA_KV, HEAD_DIM)
    pooled = _pool_pages(cache4, page_flat, lw["a3"], bs=bs, n_pages=n_pages, page_base=layer * pool)
    kc, vc = _cmp_proj(pooled, lw["w4"])
    kvn, wn, qr, oc, ngo, gates, val = _sample_pre(za3, zb3, bg3, tabs, kc, vc, past=past)
    idx = _topk(val.reshape(bs * NSA_KV * tp, val.shape[-1]), n_top=n_top)
    idx_flat = idx.reshape(bs, NSA_KV, tp, LANE)[:, :, :ts, :n_top].reshape(-1)
    wbuf = cache_win.shape[2]
    cache_win_rows = cache_win.reshape(depth * bs, wbuf * 2 * NSA_KV, HEAD_DIM)
    yb = _sample_attn(idx_flat, page_flat, qr, oc, ngo, gates, kvn, cache_win_rows, wn, cache4,
                      ts=ts, n_top=n_top, n_pages=n_pages, page_base=layer * pool, win_base=layer * bs)
    ya, conv_new = _conv_mixer(za3, state_conv[layer], lw["w_conv"], t_real=ts)
    nm = cache_mem.shape[2]
    mem_rows = cache_mem.reshape(depth * bs, nm * 2 * MEM_HEADS, HEAD_DIM)
    ym = _mem_attn(zb3, mem_rows, tq=tp, interleaved=True, kv_base=layer * bs)
    out = _out_proj(xs_p.reshape(m, d), ya.reshape(m, -1), yb.reshape(m, -1), ym.reshape(m, -1),
                    lw["w_out"], final_g, tm=m, final=final)
    kv_new = kvn[:, :ts].reshape(bs, ts, 4, NSA_KV, HEAD_DIM)
    win_rows = wn[:, :ts].reshape(bs, ts, 2, NSA_KV, HEAD_DIM)
    win_state = jnp.concatenate([cache_win[layer], win_rows], axis=1)[:, ts:]
    return out.reshape(bs, tp, d), kv_new, win_state, conv_new


def kernel(x_prompt, x_sample, cache_kv, cache_win, state_conv, cache_mem, page_table, mem_prompt,
           norm_g, w_in, w_conv, a_cmp, w_cmp, mem_norm_g, w_mem_kv, w_out, final_g):
    depth = w_in.shape[0]
    ts = x_sample.shape[1]
    xp = x_prompt
    xs = jnp.pad(x_sample, ((0, 0), (0, SAMPLE_T_PAD - ts), (0, 0)))
    kv_p, win_p, conv_p, mem_p, kv_s, win_s, conv_s = [], [], [], [], [], [], []
    for l in range(depth):
        lw = _layer_weights(norm_g[l], w_in, l, w_conv[l], a_cmp[l], w_cmp[l], w_out[l])
        final = l == depth - 1
        xp, kvn, winn, convn, mkv = _prompt_layer(xp, mem_prompt, mem_norm_g[l], w_mem_kv[l].astype(BF16),
                                                  lw, final_g, final)
        kv_p.append(kvn)
        win_p.append(winn)
        conv_p.append(convn)
        mem_p.append(mkv)
        xs, kvn, winn, convn = _sample_layer(xs, ts, l, cache_kv, cache_win, state_conv, cache_mem,
                                             page_table, lw, final_g, final)
        kv_s.append(kvn)
        win_s.append(winn)
        conv_s.append(convn)
    return (xp, xs[:, :ts], jnp.stack(kv_p), jnp.stack(win_p), jnp.stack(conv_p), jnp.stack(mem_p),
            jnp.stack(kv_s), jnp.stack(win_s), jnp.stack(conv_s))
```

```python
import functools

import jax
import jax.numpy as jnp
from jax import lax
from jax.experimental import pallas as pl
from jax.experimental.pallas import tpu as pltpu

F32 = jnp.float32
BF16 = jnp.bfloat16

HEAD_DIM = 128
CONV_DIM = 512
CONV_W = 3
NSA_HEADS = 8
NSA_KV = 2
NSA_HPG = NSA_HEADS // NSA_KV
MEM_HEADS = 4
N_BRANCH = 3
ROPE_DIM = HEAD_DIM // 4
ROPE_HALF = ROPE_DIM // 2
ROPE_THETA = 500000.0
CMP_BLOCK = 64
SEL_BLOCK = 64
TOP_N = 16
WINDOW = 512
NORM_EPS = 1e-6
MASK_NEG = -1e30
FORCE = 1e9
ATTN_SCALE = HEAD_DIM ** -0.5
SCALE_LOG2 = ATTN_SCALE * 1.4426950408889634

C_H, C_B, C_C, C_G, Q_OFF, NG_OFF = 0, 512, 1024, 1536, 2048, 3072
ZA_WIDTH = 4096
KV_OFF, MQ_OFF, MG_OFF = 0, 1536, 2048
ZB_WIDTH = 2560
BG_SRC = 4096
BG_N = NSA_HEADS * N_BRANCH
LANE = 128
SAMPLE_T_PAD = 8
VMEM_LIMIT = 56 * 1024 * 1024


def _nt(a, b):
    return lax.dot_general(a, b, (((1,), (1,)), ((), ())), preferred_element_type=F32)


def _nn(a, b):
    return jnp.dot(a, b, preferred_element_type=F32)


def _params(sem, vmem=VMEM_LIMIT):
    return pltpu.CompilerParams(dimension_semantics=sem, vmem_limit_bytes=vmem)


def _rope(x, c, s1, s2):
    return x * c + pltpu.roll(x, ROPE_HALF, 1) * s1 + pltpu.roll(x, LANE - ROPE_HALF, 1) * s2


def _silu(x):
    return x * jax.nn.sigmoid(x)


def _norm_matmul_kernel(x_ref, g_ref, w_ref, *rest, with_gate, tn):
    if with_gate:
        wbg_ref, z_ref, bg_ref, h_scr = rest
    else:
        z_ref, h_scr = rest
    x = x_ref[...]
    y = x * lax.rsqrt(jnp.mean(x * x, axis=-1, keepdims=True) + NORM_EPS) * g_ref[...]
    h_scr[...] = y.astype(BF16)
    if with_gate:
        bg_ref[...] = _nt(h_scr[...], wbg_ref[...])
    for j in range(w_ref.shape[0] // tn):
        z_ref[:, j * tn:(j + 1) * tn] = _nt(h_scr[...], w_ref[j * tn:(j + 1) * tn, :])


def _norm_matmul(x, g, w, wbg=None, *, tm, tn):
    m, d = x.shape
    n = w.shape[0]
    assert n % tn == 0 and m % tm == 0 and w.shape[1] == d
    with_gate = wbg is not None
    resident = lambda shape: pl.BlockSpec(shape, lambda i: (0, 0), pipeline_mode=pl.Buffered(1))
    in_specs = [pl.BlockSpec((tm, d), lambda i: (i, 0)), resident((1, d)), resident((n, d))]
    out_shape = [jax.ShapeDtypeStruct((m, n), F32)]
    out_specs = [pl.BlockSpec((tm, n), lambda i: (i, 0))]
    args = [x, g.reshape(1, d), w]
    if with_gate:
        nb = wbg.shape[0]
        in_specs.append(resident((nb, d)))
        out_shape.append(jax.ShapeDtypeStruct((m, nb), F32))
        out_specs.append(pl.BlockSpec((tm, nb), lambda i: (i, 0)))
        args.append(wbg)
    res = pl.pallas_call(
        functools.partial(_norm_matmul_kernel, with_gate=with_gate, tn=tn),
        out_shape=out_shape, grid=(m // tm,), in_specs=in_specs, out_specs=out_specs,
        scratch_shapes=[pltpu.VMEM((tm, d), BF16)],
        compiler_params=_params(("parallel",)),
        name="norm_matmul_gate" if with_gate else "norm_matmul",
    )(*args)
    return res if with_gate else res[0]


def _cast_rows_kernel(w_ref, o_ref):
    o_ref[...] = w_ref[0].astype(BF16)


def _pack_gate_kernel(w_ref, o_ref):
    o_ref[...] = jnp.zeros(o_ref.shape, BF16)
    o_ref[0:w_ref.shape[1], :] = w_ref[0].astype(BF16)


def _pack_in_proj_weights(w_t3, layer):
    _, n_in, d = w_t3.shape
    tn = 512
    assert n_in == BG_SRC + BG_N + ZB_WIDTH and BG_SRC % tn == 0 and ZB_WIDTH % tn == 0 and BG_N % 8 == 0

    def cast_rows(first_row, n_rows, name):
        return pl.pallas_call(
            _cast_rows_kernel,
            out_shape=jax.ShapeDtypeStruct((n_rows, d), BF16),
            grid=(n_rows // tn,),
            in_specs=[pl.BlockSpec((pl.Element(1), pl.Element(tn), pl.Element(d)),
                                   lambda j: (layer, pl.multiple_of(first_row + j * tn, 8), 0))],
            out_specs=pl.BlockSpec((tn, d), lambda j: (j, 0)),
            compiler_params=_params(("parallel",)),
            name=name,
        )(w_t3)

    wa = cast_rows(0, ZA_WIDTH, "pack_w_a")
    wb = cast_rows(BG_SRC + BG_N, ZB_WIDTH, "pack_w_b")
    wbg = pl.pallas_call(
        _pack_gate_kernel,
        out_shape=jax.ShapeDtypeStruct((LANE, d), BF16),
        grid=(1,),
        in_specs=[pl.BlockSpec((pl.Element(1), pl.Element(BG_N), pl.Element(d)), lambda j: (layer, BG_SRC, 0))],
        out_specs=pl.BlockSpec((LANE, d), lambda j: (0, 0)),
        compiler_params=_params(("arbitrary",)),
        name="pack_w_gate",
    )(w_t3)
    return wa, wb, wbg


def _kv_export_kernel(kv1_ref, kv2_ref, kv0_ref, c_ref, s1_ref, s2_ref, kvn_ref, kvb_ref, win_ref):
    c, s1, s2 = c_ref[...], s1_ref[...], s2_ref[...]
    kv0 = kv0_ref[0]
    kv1 = kv1_ref[0]
    kv2 = kv2_ref[0]
    tr = kv0.shape[0]
    n_kv, n_w, half = 4 * NSA_KV, 2 * NSA_KV, NSA_KV * LANE
    last = pl.program_id(1) == pl.num_programs(1) - 1
    for g in range(NSA_KV):
        lo, hi = g * LANE, (g + 1) * LANE
        ks = _rope(kv1[:, lo:hi], c, s1, s2)
        kw = _rope(kv2[:, lo:hi], c, s1, s2)
        vs = kv1[:, half + lo:half + hi]
        vw = kv2[:, half + lo:half + hi]
        kvn_ref[pl.ds(g, tr, stride=n_kv), :] = kv0[:, lo:hi]
        kvn_ref[pl.ds(NSA_KV + g, tr, stride=n_kv), :] = kv0[:, half + lo:half + hi]
        kvn_ref[pl.ds(2 * NSA_KV + g, tr, stride=n_kv), :] = ks
        kvn_ref[pl.ds(3 * NSA_KV + g, tr, stride=n_kv), :] = vs
        kvb_ref[0, :, lo:hi] = ks.astype(BF16)
        kvb_ref[0, :, half + lo:half + hi] = vs.astype(BF16)
        kvb_ref[0, :, 2 * half + lo:2 * half + hi] = kw.astype(BF16)
        kvb_ref[0, :, 3 * half + lo:3 * half + hi] = vw.astype(BF16)

        @pl.when(last)
        def _():
            win_ref[pl.ds(g, tr, stride=n_w), :] = kw
            win_ref[pl.ds(NSA_KV + g, tr, stride=n_w), :] = vw


def _kv_export(zb3, tabs, *, tr):
    b, t, _ = zb3.shape
    nt = t // tr
    kvblk = KV_OFF // 512
    n_kv, n_w = 4 * NSA_KV, 2 * NSA_KV
    zspec = lambda k: pl.BlockSpec((1, tr, 512), lambda bi, ti, k=k: (bi, ti, kvblk + k))
    tspec = pl.BlockSpec((tr, LANE), lambda bi, ti: (ti, 0))
    return pl.pallas_call(
        _kv_export_kernel,
        out_shape=[jax.ShapeDtypeStruct((b * t * n_kv, LANE), F32),
                   jax.ShapeDtypeStruct((b, t, n_kv * LANE), BF16),
                   jax.ShapeDtypeStruct((b * tr * n_w, LANE), F32)],
        grid=(b, nt),
        in_specs=[zspec(1), zspec(2), zspec(0), tspec, tspec, tspec],
        out_specs=[pl.BlockSpec((tr * n_kv, LANE), lambda bi, ti: (bi * nt + ti, 0)),
                   pl.BlockSpec((1, tr, n_kv * LANE), lambda bi, ti: (bi, ti, 0)),
                   pl.BlockSpec((tr * n_w, LANE), lambda bi, ti: (bi, 0))],
        compiler_params=_params(("parallel", "arbitrary")),
        name="kv_export",
    )(zb3, zb3, zb3, *tabs)


def _pool_kernel(*refs, n_prefetch):
    x_ref, a_ref, o_ref = refs[n_prefetch:]
    x = x_ref[0]
    rows = x.shape[0]
    x3 = x.reshape(rows // CMP_BLOCK, CMP_BLOCK, x.shape[1])
    pooled = jnp.sum(x3 * a_ref[...][None], axis=1)
    o_ref[...] = pooled.reshape(o_ref.shape)


def _pool_prompt(z3, a4):
    b, t, _ = z3.shape
    n = t // CMP_BLOCK
    return pl.pallas_call(
        functools.partial(_pool_kernel, n_prefetch=0),
        out_shape=jax.ShapeDtypeStruct((b, n, 512), F32),
        grid=(b,),
        in_specs=[pl.BlockSpec((1, t, 512), lambda bi: (bi, 0, KV_OFF // 512)),
                  pl.BlockSpec((CMP_BLOCK, 512), lambda bi: (0, 0))],
        out_specs=pl.BlockSpec((1, n, 512), lambda bi: (bi, 0, 0)),
        compiler_params=_params(("parallel",)),
        name="pool_prompt",
    )(z3, a4)


def _pool_pages_kernel(pt_sm, a_ref, cache_ref, o_ref, buf, sem, *, pages_per_step, page_base):
    step = pl.program_id(0)
    n_steps = pl.num_programs(0)
    slot = step % 2
    n_cols = buf.shape[3]

    def page_copy(step_idx, p, to_slot):
        page = pt_sm[step_idx * pages_per_step + p] + page_base
        return pltpu.make_async_copy(cache_ref.at[page, :, pl.ds(0, n_cols), :], buf.at[to_slot, p], sem.at[to_slot])

    @pl.when(step == 0)
    def _():
        for p in range(pages_per_step):
            page_copy(0, p, 0).start()

    @pl.when(step + 1 < n_steps)
    def _():
        for p in range(pages_per_step):
            page_copy(step + 1, p, 1 - slot).start()

    for p in range(pages_per_step):
        page_copy(step, p, slot).wait()

    a = a_ref[...]
    per = buf.shape[2] // CMP_BLOCK
    for p in range(pages_per_step):
        for k in range(per):
            x = buf[slot, p, pl.ds(k * CMP_BLOCK, CMP_BLOCK)]
            o_ref[0, p, k] = jnp.sum(x * a, axis=0)


def _pool_pages(cache4, page_flat, a3, *, bs, n_pages, page_base, pages_per_step=16):
    page = cache4.shape[1]
    per = page // CMP_BLOCK
    n_cols = 2 * NSA_KV
    total = bs * n_pages
    pages_per_step = min(pages_per_step, total)
    assert total % pages_per_step == 0
    n_steps = total // pages_per_step
    grid_spec = pltpu.PrefetchScalarGridSpec(
        num_scalar_prefetch=1, grid=(n_steps,),
        in_specs=[pl.BlockSpec((CMP_BLOCK, n_cols, LANE), lambda si, pt: (0, 0, 0)),
                  pl.BlockSpec(memory_space=pl.ANY)],
        out_specs=pl.BlockSpec((1, pages_per_step, per, n_cols, LANE), lambda si, pt: (si, 0, 0, 0, 0)),
        scratch_shapes=[pltpu.VMEM((2, pages_per_step, page, n_cols, LANE), F32),
                        pltpu.SemaphoreType.DMA((2,))])
    out = pl.pallas_call(
        functools.partial(_pool_pages_kernel, pages_per_step=pages_per_step, page_base=page_base),
        out_shape=jax.ShapeDtypeStruct((n_steps, pages_per_step, per, n_cols, LANE), F32),
        grid_spec=grid_spec,
        compiler_params=_params(("arbitrary",)),
        name="pool_pages",
    )(page_flat, a3, cache4)
    return out.reshape(bs, n_pages * per, n_cols * LANE)


def _cmp_proj_kernel(p_ref, w_ref, kc_ref, vc_ref):
    pooled = p_ref[0]
    n = pooled.shape[0]
    n_pad = kc_ref.shape[2]
    for c in range(4):
        r = _nn(pooled[:, c * LANE:(c + 1) * LANE].astype(BF16), w_ref[c]).astype(BF16)
        dst = kc_ref if c < 2 else vc_ref
        if n_pad > n:
            dst[0, c % 2] = jnp.zeros((n_pad, LANE), BF16)
        dst[0, c % 2, 0:n, :] = r


def _cmp_proj(pooled, w4):
    b, n, _ = pooled.shape
    n_pad = -(-n // LANE) * LANE
    spec = pl.BlockSpec((1, NSA_KV, n_pad, LANE), lambda bi: (bi, 0, 0, 0))
    return pl.pallas_call(
        _cmp_proj_kernel,
        out_shape=[jax.ShapeDtypeStruct((b, NSA_KV, n_pad, LANE), BF16)] * 2,
        grid=(b,),
        in_specs=[pl.BlockSpec((1, n, 512), lambda bi: (bi, 0, 0)),
                  pl.BlockSpec((4, LANE, LANE), lambda bi: (0, 0, 0))],
        out_specs=[spec, spec],
        compiler_params=_params(("parallel",)),
        name="cmp_proj",
    )(pooled, w4)


def _lane_parts(x):
    return [x[:, j * LANE:(j + 1) * LANE] for j in range(x.shape[1] // LANE)]


def _nsa_prompt_kernel(q_ref, ng_ref, bg_ref, ksel_ref, vsel_ref, kwin_ref, vwin_ref, kc_ref, vc_ref,
                       c_ref, s1_ref, s2_ref, o_ref, qc_scr, qr_scr, s_scr, p_scr, a_scr, m_scr, l_scr, acc_scr, oc_scr,
                       *, t_len, tq, tc, tw, n_sel, top):
    i = pl.program_id(2)
    hq = NSA_HPG
    q = q_ref[0]
    c, s1, s2 = c_ref[...], s1_ref[...], s2_ref[...]
    for h in range(hq):
        qh = q[:, h * LANE:(h + 1) * LANE]
        qc_scr[pl.ds(h * tq, tq), :] = (qh * SCALE_LOG2).astype(BF16)
        qr_scr[pl.ds(h * tq, tq), :] = (_rope(qh, c, s1, s2) * SCALE_LOG2).astype(BF16)

    kc = kc_ref[0, 0]
    npad = kc.shape[0]
    s_scr[:, 0:npad] = _nt(qc_scr[...], kc)
    tpos = i * tq + lax.broadcasted_iota(jnp.int32, (tq, npad), 0)
    ncol = lax.broadcasted_iota(jnp.int32, (tq, npad), 1)
    cmask = (ncol + 1) * CMP_BLOCK <= tpos + 1
    imp = jnp.zeros((tq, npad), F32)
    for h in range(hq):
        r = pl.ds(h * tq, tq)
        s = jnp.where(cmask, s_scr[r, 0:npad], MASK_NEG)
        e = jnp.where(cmask, jnp.exp2(s - jnp.max(s, axis=-1, keepdims=True)), 0.0)
        p = e / jnp.maximum(jnp.sum(e, axis=-1, keepdims=True), 1e-30)
        p_scr[r, 0:npad] = p.astype(BF16)
        imp = imp + p
    oc_scr[...] = _nn(p_scr[:, 0:npad], vc_ref[0, 0])

    rows = min(npad, -(-n_sel // 8) * 8)
    imp_t = imp.T[0:rows]
    blk = lax.broadcasted_iota(jnp.int32, (rows, tq), 0)
    cur = (i * tq + lax.broadcasted_iota(jnp.int32, (rows, tq), 1)) // SEL_BLOCK
    imp_t = jnp.where((blk == 0) | (blk == cur) | (blk == cur - 1), FORCE, imp_t)
    imp_t = jnp.where(blk > cur, -1.0, imp_t)
    imp_t = jnp.where(blk >= n_sel, -2.0, imp_t)
    rank = jnp.zeros((rows, tq), F32)
    for j in range(n_sel):
        a = imp_t[j:j + 1, :]
        ahead = (a > imp_t) | ((a == imp_t) & (blk > j))
        rank = rank + jnp.where(ahead, 1.0, 0.0)
    sel_t = jnp.where((rank < top) & (blk < n_sel), 1.0, 0.0)
    if npad > rows:
        sel_t = jnp.concatenate([sel_t, jnp.zeros((npad - rows, tq), F32)], axis=0)
    sel = sel_t.T.astype(BF16)

    m_scr[...] = jnp.full(m_scr.shape, MASK_NEG, F32)
    l_scr[...] = jnp.zeros(l_scr.shape, F32)
    acc_scr[...] = jnp.zeros(acc_scr.shape, F32)
    n_chunks = (i * tq + tq + tc - 1) // tc

    def chunk(ci, carry):
        k0 = pl.multiple_of(ci * tc, tc)
        k = ksel_ref[0, pl.ds(k0, tc), :]
        v = vsel_ref[0, pl.ds(k0, tc), :]
        jb = lax.broadcasted_iota(jnp.int32, (npad, tc), 0)
        kb = (k0 + lax.broadcasted_iota(jnp.int32, (npad, tc), 1)) // SEL_BLOCK
        expand = jnp.where(jb == kb, 1.0, 0.0).astype(BF16)
        chosen = _nn(sel, expand)
        kp = k0 + lax.broadcasted_iota(jnp.int32, (tq, tc), 1)
        tp = i * tq + lax.broadcasted_iota(jnp.int32, (tq, tc), 0)
        bias = jnp.where((chosen > 0.5) & (kp <= tp), 0.0, MASK_NEG)
        s_scr[:, 0:tc] = _nt(qr_scr[...], k)
        for h in range(hq):
            r = pl.ds(h * tq, tq)
            parts = _lane_parts(s_scr[r, 0:tc] + bias)
            m_prev = m_scr[r, :]
            m_new = jnp.maximum(m_prev, jnp.max(functools.reduce(jnp.maximum, parts), axis=-1, keepdims=True))
            alpha = jnp.exp2(m_prev - m_new)
            pes = [jnp.exp2(x - m_new) for x in parts]
            l_scr[r, :] = alpha * l_scr[r, :] + functools.reduce(jnp.add, pes)
            p_scr[r, 0:tc] = jnp.concatenate(pes, axis=1).astype(BF16)
            a_scr[r, :] = alpha
            m_scr[r, :] = m_new
        acc_scr[...] = a_scr[...] * acc_scr[...] + _nn(p_scr[:, 0:tc], v)
        return carry

    lax.fori_loop(0, n_chunks, chunk, 0)

    w0 = pl.multiple_of(jnp.clip(i * tq + tq - tw, 0, t_len - tw), LANE)
    kp = w0 + lax.broadcasted_iota(jnp.int32, (tq, tw), 1)
    tp = i * tq + lax.broadcasted_iota(jnp.int32, (tq, tw), 0)
    wbias = jnp.where((kp <= tp) & (kp > tp - WINDOW), 0.0, MASK_NEG)
    s_scr[:, 0:tw] = _nt(qr_scr[...], kwin_ref[0, pl.ds(w0, tw), :])
    for h in range(hq):
        r = pl.ds(h * tq, tq)
        parts = _lane_parts(s_scr[r, 0:tw] + wbias)
        m = jnp.max(functools.reduce(jnp.maximum, parts), axis=-1, keepdims=True)
        ews = [jnp.exp2(x - m) for x in parts]
        a_scr[r, :] = jnp.broadcast_to(jnp.sum(functools.reduce(jnp.add, ews), axis=-1, keepdims=True), (tq, LANE))
        p_scr[r, 0:tw] = jnp.concatenate(ews, axis=1).astype(BF16)
    o_w = _nn(p_scr[:, 0:tw], vwin_ref[0, pl.ds(w0, tw), :])

    gate = jax.nn.sigmoid(bg_ref[0])
    for g in range(1, NSA_KV):
        gate = jnp.where(pl.program_id(1) == g, pltpu.roll(gate, LANE - g * hq * N_BRANCH, 1), gate)
    ng = ng_ref[0]
    for h in range(hq):
        r = slice(h * tq, (h + 1) * tq)
        o_s = acc_scr[r, :] / jnp.sum(l_scr[r, :], axis=-1, keepdims=True)
        o = (gate[:, 3 * h:3 * h + 1] * oc_scr[r, :] + gate[:, 3 * h + 1:3 * h + 2] * o_s
             + gate[:, 3 * h + 2:3 * h + 3] * (o_w[r] / a_scr[r, :]))
        o_ref[0, :, h * LANE:(h + 1) * LANE] = (_silu(ng[:, h * LANE:(h + 1) * LANE]) * o).astype(o_ref.dtype)


def _nsa_prompt(z3, bg3, kvb, kc, vc, tabs, *, tq=128, tc=512):
    b, t, _ = z3.shape
    nq = t // tq
    tc = min(tc, t)
    tw = min(WINDOW + tq, t)
    n_sel = t // SEL_BLOCK
    top = min(TOP_N, n_sel)
    npad = kc.shape[2]
    gw = NSA_HPG * LANE
    rows = NSA_HPG * tq
    wide = max(tc, tw, npad)
    kvspec = lambda k: pl.BlockSpec((1, t, LANE), lambda bi, gi, qi, k=k: (bi, 0, k + gi))
    cspec = pl.BlockSpec((1, 1, npad, LANE), lambda bi, gi, qi: (bi, gi, 0, 0))
    tspec = pl.BlockSpec((tq, LANE), lambda bi, gi, qi: (qi, 0))
    kern = functools.partial(_nsa_prompt_kernel, t_len=t, tq=tq, tc=tc, tw=tw, n_sel=n_sel, top=top)
    return pl.pallas_call(
        kern,
        out_shape=jax.ShapeDtypeStruct((b, t, NSA_HEADS * LANE), BF16),
        grid=(b, NSA_KV, nq),
        in_specs=[pl.BlockSpec((1, tq, gw), lambda bi, gi, qi: (bi, qi, Q_OFF // gw + gi)),
                  pl.BlockSpec((1, tq, gw), lambda bi, gi, qi: (bi, qi, NG_OFF // gw + gi)),
                  pl.BlockSpec((1, tq, LANE), lambda bi, gi, qi: (bi, qi, 0)),
                  kvspec(0), kvspec(2), kvspec(4), kvspec(6), cspec, cspec, tspec, tspec, tspec],
        out_specs=pl.BlockSpec((1, tq, gw), lambda bi, gi, qi: (bi, qi, gi)),
        scratch_shapes=([pltpu.VMEM((rows, LANE), BF16)] * 2
                        + [pltpu.VMEM((rows, wide), F32), pltpu.VMEM((rows, wide), BF16)]
                        + [pltpu.VMEM((rows, LANE), F32)] * 5),
        compiler_params=_params(("parallel", "parallel", "arbitrary")),
        name="nsa_prompt",
    )(z3, z3, bg3, kvb, kvb, kvb, kvb, kc, vc, *tabs)


def _conv_kernel(h_ref, b_ref, c_ref, g_ref, prev_ref, w_ref, y_ref, st_ref, up_scr, *, t_real):
    u = c_ref[0] * h_ref[0]
    t = u.shape[0]
    up_scr[pl.ds(8 - (CONV_W - 1), CONV_W - 1), :] = prev_ref[0]
    up_scr[pl.ds(8, t), :] = u
    w = w_ref[...]
    y = w[0:1, :] * up_scr[pl.ds(6, t), :]
    y = y + w[1:2, :] * up_scr[pl.ds(7, t), :]
    y = y + w[2:3, :] * u
    y = b_ref[0] * y
    y_ref[0] = (_silu(g_ref[0]) * y).astype(y_ref.dtype)
    st_ref[0] = up_scr[pl.ds(6 + t_real, CONV_W - 1), :]


def _conv_mixer(z3, prev, w_conv, *, t_real):
    b, t, _ = z3.shape
    nc = CONV_DIM // LANE
    zspec = lambda off: pl.BlockSpec((1, t, LANE), lambda bi, ci, off=off: (bi, 0, off // LANE + ci))
    return pl.pallas_call(
        functools.partial(_conv_kernel, t_real=t_real),
        out_shape=[jax.ShapeDtypeStruct((b, t, CONV_DIM), BF16),
                   jax.ShapeDtypeStruct((b, CONV_W - 1, CONV_DIM), F32)],
        grid=(b, nc),
        in_specs=[zspec(C_H), zspec(C_B), zspec(C_C), zspec(C_G),
                  pl.BlockSpec((1, CONV_W - 1, LANE), lambda bi, ci: (bi, 0, ci)),
                  pl.BlockSpec((CONV_W, LANE), lambda bi, ci: (0, ci))],
        out_specs=[pl.BlockSpec((1, t, LANE), lambda bi, ci: (bi, 0, ci)),
                   pl.BlockSpec((1, CONV_W - 1, LANE), lambda bi, ci: (bi, 0, ci))],
        scratch_shapes=[pltpu.VMEM((t + 8, LANE), F32)],
        compiler_params=_params(("parallel", "parallel")),
        name="conv_mixer",
    )(z3, z3, z3, z3, prev, w_conv)


def _mem_attn_kernel(q_ref, mg_ref, kv_ref, o_ref, *, interleaved):
    q = q_ref[0]
    mg = mg_ref[0]
    half = MEM_HEADS * LANE
    for h in range(MEM_HEADS):
        lo, hi = h * LANE, (h + 1) * LANE
        if interleaved:
            nm = kv_ref.shape[1] // (2 * MEM_HEADS)
            k = kv_ref[0, pl.ds(h, nm, stride=2 * MEM_HEADS), :].astype(BF16)
            v = kv_ref[0, pl.ds(MEM_HEADS + h, nm, stride=2 * MEM_HEADS), :].astype(BF16)
        else:
            k = kv_ref[0, :, lo:hi].astype(BF16)
            v = kv_ref[0, :, half + lo:half + hi].astype(BF16)
        s = _nt((q[:, lo:hi] * ATTN_SCALE).astype(BF16), k)
        e = jnp.exp(s - jnp.max(s, axis=-1, keepdims=True))
        o = _nn(e.astype(BF16), v) / jnp.sum(e, axis=-1, keepdims=True)
        o_ref[0, :, lo:hi] = (_silu(mg[:, lo:hi]) * o).astype(o_ref.dtype)


def _mem_attn(zb3, mkv, *, tq, interleaved, kv_base=0):
    b, t, _ = zb3.shape
    wq = MEM_HEADS * LANE
    return pl.pallas_call(
        functools.partial(_mem_attn_kernel, interleaved=interleaved),
        out_shape=jax.ShapeDtypeStruct((b, t, wq), BF16),
        grid=(b, t // tq),
        in_specs=[pl.BlockSpec((1, tq, wq), lambda bi, ti: (bi, ti, MQ_OFF // wq)),
                  pl.BlockSpec((1, tq, wq), lambda bi, ti: (bi, ti, MG_OFF // wq)),
                  pl.BlockSpec((1,) + mkv.shape[1:], lambda bi, ti: (bi + kv_base, 0, 0))],
        out_specs=pl.BlockSpec((1, tq, wq), lambda bi, ti: (bi, ti, 0)),
        compiler_params=_params(("parallel", "parallel")),
        name="mem_attn",
    )(zb3, zb3, mkv)


def _out_proj_kernel(x_ref, ya_ref, yb_ref, ym_ref, w_ref, fg_ref, o_ref, *, final):
    a, bw = CONV_DIM, CONV_DIM + NSA_HEADS * LANE
    acc = _nn(ya_ref[...], w_ref[0:a, :])
    acc = acc + _nn(yb_ref[...], w_ref[a:bw, :])
    acc = acc + _nn(ym_ref[...], w_ref[bw:, :])
    r = x_ref[...] + acc
    if final:
        r = r * lax.rsqrt(jnp.mean(r * r, axis=-1, keepdims=True) + NORM_EPS) * fg_ref[...]
    o_ref[...] = r


def _out_proj(x, ya, yb, ym, w, fg, *, tm, final):
    m, d = x.shape
    row = lambda width: pl.BlockSpec((tm, width), lambda i: (i, 0))
    return pl.pallas_call(
        functools.partial(_out_proj_kernel, final=final),
        out_shape=jax.ShapeDtypeStruct((m, d), F32),
        grid=(m // tm,),
        in_specs=[row(d), row(ya.shape[1]), row(yb.shape[1]), row(ym.shape[1]),
                  pl.BlockSpec(w.shape, lambda i: (0, 0)),
                  pl.BlockSpec((1, d), lambda i: (0, 0))],
        out_specs=row(d),
        compiler_params=_params(("parallel",)),
        name="out_proj",
    )(x, ya, yb, ym, w, fg.reshape(1, d))


def _sample_pre_kernel(q_ref, ng_ref, bg_ref, kv0_ref, kv1_ref, kv2_ref, c_ref, s1_ref, s2_ref, kc_ref, vc_ref,
                       kvn_ref, wn_ref, qr_ref, oc_ref, ngo_ref, gate_ref, val_ref, *, past):
    tp = SAMPLE_T_PAD
    hq = NSA_HPG
    c, s1, s2 = c_ref[...], s1_ref[...], s2_ref[...]
    kv0, kv1, kv2 = kv0_ref[0], kv1_ref[0], kv2_ref[0]
    kvn_ref[0, :, 0:512] = kv0
    kvn_ref[0, :, 768:1024] = kv1[:, 256:512]
    wn_ref[0, :, 256:512] = kv2[:, 256:512]
    for g in range(NSA_KV):
        lo, hi = g * LANE, (g + 1) * LANE
        kvn_ref[0, :, 512 + lo:512 + hi] = _rope(kv1[:, lo:hi], c, s1, s2)
        wn_ref[0, :, lo:hi] = _rope(kv2[:, lo:hi], c, s1, s2)

    q = q_ref[0]
    ng = ng_ref[0]
    gates = jax.nn.sigmoid(bg_ref[0])
    for g in range(NSA_KV):
        qc_l, qr_l = [], []
        for h in range(hq):
            lo = (g * hq + h) * LANE
            qh = q[:, lo:lo + LANE]
            qc_l.append(qh * ATTN_SCALE)
            qr_l.append(_rope(qh, c, s1, s2) * ATTN_SCALE)
            ngo_ref[0, g, h * tp:(h + 1) * tp, :] = ng[:, lo:lo + LANE]
            for br in range(N_BRANCH):
                col = (g * hq + h) * N_BRANCH + br
                gate_ref[0, g, br, h * tp:(h + 1) * tp, :] = jnp.broadcast_to(gates[:, col:col + 1], (tp, LANE))
        qc = jnp.concatenate(qc_l, axis=0)
        qr_ref[0, g] = jnp.concatenate(qr_l, axis=0)

        kc = kc_ref[0, g]
        npad = kc.shape[0]
        s = _nt(qc.astype(BF16), kc)
        trow = lax.broadcasted_iota(jnp.int32, (hq * tp, npad), 0) % tp
        ncol = lax.broadcasted_iota(jnp.int32, (hq * tp, npad), 1)
        cmask = (ncol + 1) * CMP_BLOCK <= past + trow + 1
        s = jnp.where(cmask, s, MASK_NEG)
        e = jnp.where(cmask, jnp.exp(s - jnp.max(s, axis=-1, keepdims=True)), 0.0)
        p = e / jnp.maximum(jnp.sum(e, axis=-1, keepdims=True), 1e-30)
        oc_ref[0, g] = _nn(p.astype(BF16), vc_ref[0, g])
        imp = jnp.sum(p.reshape(hq, tp, npad), axis=0)

        blk = lax.broadcasted_iota(jnp.int32, (tp, npad), 1)
        cur = (past + lax.broadcasted_iota(jnp.int32, (tp, npad), 0)) // SEL_BLOCK
        val = jnp.where((blk == 0) | (blk == cur) | (blk == cur - 1), FORCE, imp)
        val = jnp.where(blk > cur, -1.0, val)
        val = jnp.where(blk >= past // SEL_BLOCK, -2.0, val)
        val_ref[0, g] = val


def _topk_kernel(val_ref, idx_ref, *, n_top):
    val = val_ref[...]
    rows, n = val.shape
    blk = lax.broadcasted_iota(jnp.int32, (rows, n), 1)
    lane = lax.broadcasted_iota(jnp.int32, (rows, LANE), 1)
    idx = jnp.zeros((rows, LANE), jnp.int32)
    for r in range(n_top):
        best = jnp.max(val, axis=-1, keepdims=True)
        j = jnp.min(jnp.where(val == best, blk, n), axis=-1, keepdims=True)
        idx = jnp.where(lane == r, j, idx)
        val = jnp.where(blk == j, -3e38, val)
    idx_ref[...] = idx


def _topk(val2, *, n_top):
    rows, n = val2.shape
    return pl.pallas_call(
        functools.partial(_topk_kernel, n_top=n_top),
        out_shape=jax.ShapeDtypeStruct((rows, LANE), jnp.int32),
        grid=(1,),
        in_specs=[pl.BlockSpec((rows, n), lambda i: (0, 0))],
        out_specs=pl.BlockSpec((rows, LANE), lambda i: (0, 0)),
        compiler_params=_params(("arbitrary",)),
        name="sample_topk",
    )(val2)


def _sample_pre(za3, zb3, bg3, tabs, kc, vc, *, past):
    bs, tp, _ = za3.shape
    npad = kc.shape[2]
    qw = NSA_HEADS * LANE
    kvblk = KV_OFF // 512
    zspec = lambda k: pl.BlockSpec((1, tp, 512), lambda bi, k=k: (bi, 0, kvblk + k))
    tspec = pl.BlockSpec((tp, LANE), lambda bi: (0, 0))
    cspec = pl.BlockSpec((1, NSA_KV, npad, LANE), lambda bi: (bi, 0, 0, 0))
    rows = NSA_HPG * tp
    gspec = pl.BlockSpec((1, NSA_KV, rows, LANE), lambda bi: (bi, 0, 0, 0))
    gshape = jax.ShapeDtypeStruct((bs, NSA_KV, rows, LANE), F32)
    return pl.pallas_call(
        functools.partial(_sample_pre_kernel, past=past),
        out_shape=[jax.ShapeDtypeStruct((bs, tp, 1024), F32),
                   jax.ShapeDtypeStruct((bs, tp, 512), F32),
                   gshape, gshape, gshape,
                   jax.ShapeDtypeStruct((bs, NSA_KV, N_BRANCH, rows, LANE), F32),
                   jax.ShapeDtypeStruct((bs, NSA_KV, tp, npad), F32)],
        grid=(bs,),
        in_specs=[pl.BlockSpec((1, tp, qw), lambda bi: (bi, 0, Q_OFF // qw)),
                  pl.BlockSpec((1, tp, qw), lambda bi: (bi, 0, NG_OFF // qw)),
                  pl.BlockSpec((1, tp, LANE), lambda bi: (bi, 0, 0)),
                  zspec(0), zspec(1), zspec(2), tspec, tspec, tspec, cspec, cspec],
        out_specs=[pl.BlockSpec((1, tp, 1024), lambda bi: (bi, 0, 0)),
                   pl.BlockSpec((1, tp, 512), lambda bi: (bi, 0, 0)),
                   gspec, gspec, gspec,
                   pl.BlockSpec((1, NSA_KV, N_BRANCH, rows, LANE), lambda bi: (bi, 0, 0, 0, 0)),
                   pl.BlockSpec((1, NSA_KV, tp, npad), lambda bi: (bi, 0, 0, 0))],
        compiler_params=_params(("parallel",)),
        name="sample_pre",
    )(za3, za3, bg3, zb3, zb3, zb3, *tabs, kc, vc)


def _sample_attn_kernel(idx_sm, pt_sm, qr_ref, oc_ref, ng_ref, gate_ref, ksn_ref, vsn_ref,
                        wc_ref, kwn_ref, vwn_ref, cache_ref, o_ref,
                        kbuf, vbuf, kw_scr, vw_scr, sem, *, ts, n_top, n_pages, page_base, per_page, wb):
    tp = SAMPLE_T_PAD
    hq = NSA_HPG
    b = pl.program_id(0)
    g = pl.program_id(1)
    n_gath = n_top * SEL_BLOCK
    ks_rows = kbuf.shape[1]

    def gather_copies(t, r):
        blk = idx_sm[((b * NSA_KV + g) * ts + t) * n_top + r]
        page = pt_sm[b * n_pages + blk // per_page] + page_base
        row0 = (blk % per_page) * SEL_BLOCK
        src_k = cache_ref.at[page, pl.ds(row0, SEL_BLOCK), 2 * NSA_KV + g]
        src_v = cache_ref.at[page, pl.ds(row0, SEL_BLOCK), 3 * NSA_KV + g]
        dst = pl.ds(r * SEL_BLOCK, SEL_BLOCK)
        return (pltpu.make_async_copy(src_k, kbuf.at[t, dst], sem.at[0]),
                pltpu.make_async_copy(src_v, vbuf.at[t, dst], sem.at[1]))

    for t in range(ts):
        for r in range(n_top):
            ck, cv = gather_copies(t, r)
            ck.start()
            cv.start()

    qr = qr_ref[0, 0].astype(BF16)
    trow = lax.broadcasted_iota(jnp.int32, (hq * tp, 1), 0) % tp

    ww = kw_scr.shape[0]
    kw_scr[pl.ds(0, wb), :] = wc_ref[0, pl.ds(g, wb, stride=2 * NSA_KV), :]
    vw_scr[pl.ds(0, wb), :] = wc_ref[0, pl.ds(NSA_KV + g, wb, stride=2 * NSA_KV), :]
    kw_scr[pl.ds(wb, tp), :] = kwn_ref[0]
    vw_scr[pl.ds(wb, tp), :] = vwn_ref[0]
    kw_scr[pl.ds(wb + tp, ww - wb - tp), :] = jnp.zeros((ww - wb - tp, LANE), F32)
    vw_scr[pl.ds(wb + tp, ww - wb - tp), :] = jnp.zeros((ww - wb - tp, LANE), F32)
    sw = _nt(qr, kw_scr[...].astype(BF16))
    jw = lax.broadcasted_iota(jnp.int32, (hq * tp, ww), 1)
    rel = jw - wb
    okw = (rel <= trow) & (rel > trow - WINDOW) & (jw < wb + ts)
    sw = jnp.where(okw, sw, MASK_NEG)
    ew = jnp.where(okw, jnp.exp(sw - jnp.max(sw, axis=-1, keepdims=True)), 0.0)
    o_w = _nn(ew.astype(BF16), vw_scr[...].astype(BF16)) / jnp.sum(ew, axis=-1, keepdims=True)

    for t in range(ts):
        for r in range(n_top):
            ck, cv = gather_copies(t, r)
            ck.wait()
            cv.wait()

    js = lax.broadcasted_iota(jnp.int32, (hq * tp, ks_rows), 1)
    o_s = jnp.zeros((hq * tp, LANE), F32)
    for t in range(ts):
        kbuf[t, pl.ds(n_gath, tp), :] = ksn_ref[0]
        vbuf[t, pl.ds(n_gath, tp), :] = vsn_ref[0]
        kbuf[t, pl.ds(n_gath + tp, ks_rows - n_gath - tp), :] = jnp.zeros((ks_rows - n_gath - tp, LANE), F32)
        vbuf[t, pl.ds(n_gath + tp, ks_rows - n_gath - tp), :] = jnp.zeros((ks_rows - n_gath - tp, LANE), F32)
        ss = _nt(qr, kbuf[t].astype(BF16))
        oks = (js < n_gath) | ((js - n_gath <= t) & (js < n_gath + ts))
        ss = jnp.where(oks, ss, MASK_NEG)
        es = jnp.where(oks, jnp.exp(ss - jnp.max(ss, axis=-1, keepdims=True)), 0.0)
        ot = _nn(es.astype(BF16), vbuf[t].astype(BF16)) / jnp.sum(es, axis=-1, keepdims=True)
        o_s = jnp.where(trow == t, ot, o_s)

    o = gate_ref[0, 0, 0] * oc_ref[0, 0] + gate_ref[0, 0, 1] * o_s + gate_ref[0, 0, 2] * o_w
    y = _silu(ng_ref[0, 0]) * o
    for h in range(hq):
        o_ref[0, :, h * LANE:(h + 1) * LANE] = y[h * tp:(h + 1) * tp].astype(o_ref.dtype)


def _sample_attn(idx_flat, page_flat, qr, oc, ngo, gates, kvn, cache_win_rows, wn, cache4, *,
                 ts, n_top, n_pages, page_base, win_base):
    bs = qr.shape[0]
    tp = SAMPLE_T_PAD
    rows = NSA_HPG * tp
    wb = cache_win_rows.shape[1] // (2 * NSA_KV)
    per_page = cache4.shape[1] // SEL_BLOCK
    ks_rows = -(-(n_top * SEL_BLOCK + tp) // LANE) * LANE
    ww = -(-(wb + tp) // LANE) * LANE
    gspec = pl.BlockSpec((1, 1, rows, LANE), lambda bi, gi, *_: (bi, gi, 0, 0))
    newspec = lambda k: pl.BlockSpec((1, tp, LANE), lambda bi, gi, *_, k=k: (bi, 0, k + gi))
    grid_spec = pltpu.PrefetchScalarGridSpec(
        num_scalar_prefetch=2, grid=(bs, NSA_KV),
        in_specs=[gspec, gspec, gspec,
                  pl.BlockSpec((1, 1, N_BRANCH, rows, LANE), lambda bi, gi, *_: (bi, gi, 0, 0, 0)),
                  newspec(2 * NSA_KV), newspec(3 * NSA_KV),
                  pl.BlockSpec((1,) + cache_win_rows.shape[1:], lambda bi, gi, *_: (bi + win_base, 0, 0)),
                  newspec(0), newspec(NSA_KV),
                  pl.BlockSpec(memory_space=pl.ANY)],
        out_specs=pl.BlockSpec((1, tp, NSA_HPG * LANE), lambda bi, gi, *_: (bi, 0, gi)),
        scratch_shapes=[pltpu.VMEM((ts, ks_rows, LANE), F32), pltpu.VMEM((ts, ks_rows, LANE), F32),
                        pltpu.VMEM((ww, LANE), F32), pltpu.VMEM((ww, LANE), F32),
                        pltpu.SemaphoreType.DMA((2,))])
    kern = functools.partial(_sample_attn_kernel, ts=ts, n_top=n_top, n_pages=n_pages,
                             page_base=page_base, per_page=per_page, wb=wb)
    return pl.pallas_call(
        kern,
        out_shape=jax.ShapeDtypeStruct((bs, tp, NSA_HEADS * LANE), BF16),
        grid_spec=grid_spec,
        compiler_params=_params(("arbitrary", "arbitrary")),
        name="sample_attn",
    )(idx_flat, page_flat, qr, oc, ngo, gates, kvn, kvn, cache_win_rows, wn, wn, cache4)


def _rope_tables(pos, rows):
    freqs = jnp.power(ROPE_THETA, -jnp.arange(ROPE_HALF, dtype=F32) * (2.0 / ROPE_DIM))
    ang = pos.astype(F32)[:, None] * freqs[None, :]
    cos, sin = jnp.cos(ang), jnp.sin(ang)
    n = pos.shape[0]
    z16 = jnp.zeros((n, ROPE_HALF), F32)
    rest = LANE - ROPE_DIM
    c = jnp.concatenate([cos, cos, jnp.ones((n, rest), F32)], axis=1)
    s1 = jnp.concatenate([z16, sin, jnp.zeros((n, rest), F32)], axis=1)
    s2 = jnp.concatenate([-sin, z16, jnp.zeros((n, rest), F32)], axis=1)
    pad = lambda a: jnp.pad(a, ((0, rows - n), (0, 0)))
    return pad(c), pad(s1), pad(s2)


def _layer_weights(norm_g, w_in3, layer, w_conv, a_cmp, w_cmp, w_out):
    w_a, w_b, wbg = _pack_in_proj_weights(jnp.swapaxes(w_in3, 1, 2), layer)
    a4 = jnp.concatenate([a_cmp[0], a_cmp[0], a_cmp[1], a_cmp[1]], axis=1)
    a3 = jnp.stack([a_cmp[0], a_cmp[0], a_cmp[1], a_cmp[1]], axis=1)
    w4 = jnp.stack([w_cmp[0], w_cmp[0], w_cmp[1], w_cmp[1]]).astype(BF16)
    return dict(norm_g=norm_g, w_a=w_a, w_b=w_b, wbg=wbg, w_conv=w_conv, a4=a4, a3=a3, w4=w4,
                w_out=w_out.astype(BF16))


def _in_proj(x2, lw, *, tm):
    za, bg = _norm_matmul(x2, lw["norm_g"], lw["w_a"], lw["wbg"], tm=tm, tn=512)
    zb = _norm_matmul(x2, lw["norm_g"], lw["w_b"], tm=tm, tn=512)
    return za, zb, bg


def _prompt_layer(xp, mem_prompt, mem_norm_g, w_mem, lw, final_g, final):
    b, t, d = xp.shape
    m = b * t
    za, zb, bg = _in_proj(xp.reshape(m, d), lw, tm=min(512, m))
    za3 = za.reshape(b, t, ZA_WIDTH)
    zb3 = zb.reshape(b, t, ZB_WIDTH)
    bg3 = bg.reshape(b, t, LANE)
    nm = mem_prompt.shape[1]
    mkv = _norm_matmul(mem_prompt.reshape(b * nm, d), mem_norm_g, w_mem, tm=min(512, b * nm), tn=512)
    mkv3 = mkv.reshape(b, nm, 2 * MEM_HEADS * LANE)
    tabs = _rope_tables(jnp.arange(t, dtype=jnp.int32), t)
    wb = min(WINDOW, t)
    kvn, kvb, win_rows = _kv_export(zb3, tabs, tr=wb)
    kc, vc = _cmp_proj(_pool_prompt(zb3, lw["a4"]), lw["w4"])
    yb = _nsa_prompt(za3, bg3, kvb, kc, vc, tabs, tq=min(256, t))
    conv0 = jnp.zeros((b, CONV_W - 1, CONV_DIM), F32)
    ya, conv_new = _conv_mixer(za3, conv0, lw["w_conv"], t_real=t)
    ym = _mem_attn(zb3, mkv3, tq=min(512, t), interleaved=False)
    out = _out_proj(xp.reshape(m, d), ya.reshape(m, -1), yb.reshape(m, -1), ym.reshape(m, -1),
                    lw["w_out"], final_g, tm=min(512, m), final=final)
    kv_new = kvn.reshape(b, t, 4, NSA_KV, HEAD_DIM)
    win_new = win_rows.reshape(b, wb, 2, NSA_KV, HEAD_DIM)
    mem_kv = mkv.reshape(b, nm, 2, MEM_HEADS, HEAD_DIM)
    return out.reshape(b, t, d), kv_new, win_new, conv_new, mem_kv


def _sample_layer(xs_p, ts, layer, cache_kv, cache_win, state_conv, cache_mem, page_table, lw, final_g, final):
    bs, tp, d = xs_p.shape
    depth, pool, page = cache_kv.shape[0], cache_kv.shape[1], cache_kv.shape[2]
    n_pages = page_table.shape[1]
    past = n_pages * page
    assert past % SEL_BLOCK == 0 and ts <= SEL_BLOCK and ts <= tp
    n_past = past // SEL_BLOCK
    n_top = min(TOP_N, n_past + 1) - 1
    m = bs * tp
    za, zb, bg = _in_proj(xs_p.reshape(m, d), lw, tm=m)
    za3 = za.reshape(bs, tp, ZA_WIDTH)
    zb3 = zb.reshape(bs, tp, ZB_WIDTH)
    bg3 = bg.reshape(bs, tp, LANE)
    tabs = _rope_tables(past + jnp.arange(tp, dtype=jnp.int32), tp)
    page_flat = page_table.reshape(-1).astype(jnp.int32)
    cache4 = cache_kv.reshape(depth * pool, page, 4 * NSA_KV, HEAD_DIM)
    pooled = _pool_pages(cache4, page_flat, lw["a3"], bs=bs, n_pages=n_pages, page_base=layer * pool)
    kc, vc = _cmp_proj(pooled, lw["w4"])
    kvn, wn, qr, oc, ngo, gates, val = _sample_pre(za3, zb3, bg3, tabs, kc, vc, past=past)
    idx = _topk(val.reshape(bs * NSA_KV * tp, val.shape[-1]), n_top=n_top)
    idx_flat = idx.reshape(bs, NSA_KV, tp, LANE)[:, :, :ts, :n_top].reshape(-1)
    wbuf = cache_win.shape[2]
    cache_win_rows = cache_win.reshape(depth * bs, wbuf * 2 * NSA_KV, HEAD_DIM)
    yb = _sample_attn(idx_flat, page_flat, qr, oc, ngo, gates, kvn, cache_win_rows, wn, cache4,
                      ts=ts, n_top=n_top, n_pages=n_pages, page_base=layer * pool, win_base=layer * bs)
    ya, conv_new = _conv_mixer(za3, state_conv[layer], lw["w_conv"], t_real=ts)
    nm = cache_mem.shape[2]
    mem_rows = cache_mem.reshape(depth * bs, nm * 2 * MEM_HEADS, HEAD_DIM)
    ym = _mem_attn(zb3, mem_rows, tq=tp, interleaved=True, kv_base=layer * bs)
    out = _out_proj(xs_p.reshape(m, d), ya.reshape(m, -1), yb.reshape(m, -1), ym.reshape(m, -1),
                    lw["w_out"], final_g, tm=m, final=final)
    kv_new = kvn[:, :ts].reshape(bs, ts, 4, NSA_KV, HEAD_DIM)
    win_rows = wn[:, :ts].reshape(bs, ts, 2, NSA_KV, HEAD_DIM)
    win_state = jnp.concatenate([cache_win[layer], win_rows], axis=1)[:, ts:]
    return out.reshape(bs, tp, d), kv_new, win_state, conv_new


def kernel(x_prompt, x_sample, cache_kv, cache_win, state_conv, cache_mem, page_table, mem_prompt,
           norm_g, w_in, w_conv, a_cmp, w_cmp, mem_norm_g, w_mem_kv, w_out, final_g):
    depth = w_in.shape[0]
    ts = x_sample.shape[1]
    xp = x_prompt
    xs = jnp.pad(x_sample, ((0, 0), (0, SAMPLE_T_PAD - ts), (0, 0)))
    kv_p, win_p, conv_p, mem_p, kv_s, win_s, conv_s = [], [], [], [], [], [], []
    for l in range(depth):
        lw = _layer_weights(norm_g[l], w_in, l, w_conv[l], a_cmp[l], w_cmp[l], w_out[l])
        final = l == depth - 1
        xp, kvn, winn, convn, mkv = _prompt_layer(xp, mem_prompt, mem_norm_g[l], w_mem_kv[l].T.astype(BF16),
                                                  lw, final_g, final)
        kv_p.append(kvn)
        win_p.append(winn)
        conv_p.append(convn)
        mem_p.append(mkv)
        xs, kvn, winn, convn = _sample_layer(xs, ts, l, cache_kv, cache_win, state_conv, cache_mem,
                                             page_table, lw, final_g, final)
        kv_s.append(kvn)
        win_s.append(winn)
        conv_s.append(convn)
    return (xp, xs[:, :ts], jnp.stack(kv_p), jnp.stack(win_p), jnp.stack(conv_p), jnp.stack(mem_p),
            jnp.stack(kv_s), jnp.stack(win_s), jnp.stack(conv_s))
```

```python
import functools

import jax
import jax.numpy as jnp
from jax import lax
from jax.experimental import pallas as pl
from jax.experimental.pallas import tpu as pltpu

F32 = jnp.float32
BF16 = jnp.bfloat16

HEAD_DIM = 128
CONV_DIM = 512
CONV_W = 3
NSA_HEADS = 8
NSA_KV = 2
NSA_HPG = NSA_HEADS // NSA_KV
MEM_HEADS = 4
N_BRANCH = 3
ROPE_DIM = HEAD_DIM // 4
ROPE_HALF = ROPE_DIM // 2
ROPE_THETA = 500000.0
CMP_BLOCK = 64
SEL_BLOCK = 64
TOP_N = 16
WINDOW = 512
NORM_EPS = 1e-6
MASK_NEG = -1e30
FORCE = 1e9
ATTN_SCALE = HEAD_DIM ** -0.5
SCALE_LOG2 = ATTN_SCALE * 1.4426950408889634

C_H, C_B, C_C, C_G, Q_OFF, NG_OFF = 0, 512, 1024, 1536, 2048, 3072
ZA_WIDTH = 4096
KV_OFF, MQ_OFF, MG_OFF = 0, 1536, 2048
ZB_WIDTH = 2560
BG_SRC = 4096
BG_N = NSA_HEADS * N_BRANCH
LANE = 128
SAMPLE_T_PAD = 8
VMEM_LIMIT = 56 * 1024 * 1024


def _nt(a, b):
    return lax.dot_general(a, b, (((1,), (1,)), ((), ())), preferred_element_type=F32)


def _nn(a, b):
    return jnp.dot(a, b, preferred_element_type=F32)


def _params(sem, vmem=VMEM_LIMIT):
    return pltpu.CompilerParams(dimension_semantics=sem, vmem_limit_bytes=vmem)


def _rope(x, c, s1, s2):
    return x * c + pltpu.roll(x, ROPE_HALF, 1) * s1 + pltpu.roll(x, LANE - ROPE_HALF, 1) * s2


def _silu(x):
    return x * jax.nn.sigmoid(x)


def _norm_matmul_kernel(x_ref, g_ref, w_ref, *rest, with_gate, tn):
    if with_gate:
        wbg_ref, z_ref, bg_ref, h_scr = rest
    else:
        z_ref, h_scr = rest
    x = x_ref[...]
    y = x * lax.rsqrt(jnp.mean(x * x, axis=-1, keepdims=True) + NORM_EPS) * g_ref[...]
    h_scr[...] = y.astype(BF16)
    if with_gate:
        bg_ref[...] = _nt(h_scr[...], wbg_ref[...])
    for j in range(w_ref.shape[0] // tn):
        z_ref[:, j * tn:(j + 1) * tn] = _nt(h_scr[...], w_ref[j * tn:(j + 1) * tn, :])


def _norm_matmul(x, g, w, wbg=None, *, tm, tn):
    m, d = x.shape
    n = w.shape[0]
    assert n % tn == 0 and m % tm == 0 and w.shape[1] == d
    with_gate = wbg is not None
    resident = lambda shape: pl.BlockSpec(shape, lambda i: (0, 0), pipeline_mode=pl.Buffered(1))
    in_specs = [pl.BlockSpec((tm, d), lambda i: (i, 0)), resident((1, d)), resident((n, d))]
    out_shape = [jax.ShapeDtypeStruct((m, n), F32)]
    out_specs = [pl.BlockSpec((tm, n), lambda i: (i, 0))]
    args = [x, g.reshape(1, d), w]
    if with_gate:
        nb = wbg.shape[0]
        in_specs.append(resident((nb, d)))
        out_shape.append(jax.ShapeDtypeStruct((m, nb), F32))
        out_specs.append(pl.BlockSpec((tm, nb), lambda i: (i, 0)))
        args.append(wbg)
    res = pl.pallas_call(
        functools.partial(_norm_matmul_kernel, with_gate=with_gate, tn=tn),
        out_shape=out_shape, grid=(m // tm,), in_specs=in_specs, out_specs=out_specs,
        scratch_shapes=[pltpu.VMEM((tm, d), BF16)],
        compiler_params=_params(("parallel",)),
        name="norm_matmul_gate" if with_gate else "norm_matmul",
    )(*args)
    return res if with_gate else res[0]


def _cast_rows_kernel(w_ref, o_ref):
    o_ref[...] = w_ref[0].astype(BF16)


def _pack_gate_kernel(w_ref, o_ref):
    o_ref[...] = jnp.zeros(o_ref.shape, BF16)
    o_ref[0:w_ref.shape[1], :] = w_ref[0].astype(BF16)


def _pack_in_proj_weights(w_t3, layer):
    _, n_in, d = w_t3.shape
    tn = 512
    assert n_in == BG_SRC + BG_N + ZB_WIDTH and BG_SRC % tn == 0 and ZB_WIDTH % tn == 0 and BG_N % 8 == 0

    def cast_rows(first_row, n_rows, name):
        return pl.pallas_call(
            _cast_rows_kernel,
            out_shape=jax.ShapeDtypeStruct((n_rows, d), BF16),
            grid=(n_rows // tn,),
            in_specs=[pl.BlockSpec((pl.Element(1), pl.Element(tn), pl.Element(d)),
                                   lambda j: (layer, pl.multiple_of(first_row + j * tn, 8), 0))],
            out_specs=pl.BlockSpec((tn, d), lambda j: (j, 0)),
            compiler_params=_params(("parallel",)),
            name=name,
        )(w_t3)

    wa = cast_rows(0, ZA_WIDTH, "pack_w_a")
    wb = cast_rows(BG_SRC + BG_N, ZB_WIDTH, "pack_w_b")
    wbg = pl.pallas_call(
        _pack_gate_kernel,
        out_shape=jax.ShapeDtypeStruct((LANE, d), BF16),
        grid=(1,),
        in_specs=[pl.BlockSpec((pl.Element(1), pl.Element(BG_N), pl.Element(d)), lambda j: (layer, BG_SRC, 0))],
        out_specs=pl.BlockSpec((LANE, d), lambda j: (0, 0)),
        compiler_params=_params(("arbitrary",)),
        name="pack_w_gate",
    )(w_t3)
    return wa, wb, wbg


def _kv_export_kernel(kv1_ref, kv2_ref, kv0_ref, c_ref, s1_ref, s2_ref, kvn_ref, kvb_ref, win_ref):
    c, s1, s2 = c_ref[...], s1_ref[...], s2_ref[...]
    kv0 = kv0_ref[0]
    kv1 = kv1_ref[0]
    kv2 = kv2_ref[0]
    tr = kv0.shape[0]
    n_kv, n_w, half = 4 * NSA_KV, 2 * NSA_KV, NSA_KV * LANE
    last = pl.program_id(1) == pl.num_programs(1) - 1
    for g in range(NSA_KV):
        lo, hi = g * LANE, (g + 1) * LANE
        ks = _rope(kv1[:, lo:hi], c, s1, s2)
        kw = _rope(kv2[:, lo:hi], c, s1, s2)
        vs = kv1[:, half + lo:half + hi]
        vw = kv2[:, half + lo:half + hi]
        kvn_ref[pl.ds(g, tr, stride=n_kv), :] = kv0[:, lo:hi]
        kvn_ref[pl.ds(NSA_KV + g, tr, stride=n_kv), :] = kv0[:, half + lo:half + hi]
        kvn_ref[pl.ds(2 * NSA_KV + g, tr, stride=n_kv), :] = ks
        kvn_ref[pl.ds(3 * NSA_KV + g, tr, stride=n_kv), :] = vs
        kvb_ref[0, :, lo:hi] = ks.astype(BF16)
        kvb_ref[0, :, half + lo:half + hi] = vs.astype(BF16)
        kvb_ref[0, :, 2 * half + lo:2 * half + hi] = kw.astype(BF16)
        kvb_ref[0, :, 3 * half + lo:3 * half + hi] = vw.astype(BF16)

        @pl.when(last)
        def _():
            win_ref[pl.ds(g, tr, stride=n_w), :] = kw
            win_ref[pl.ds(NSA_KV + g, tr, stride=n_w), :] = vw


def _kv_export(zb3, tabs, *, tr):
    b, t, _ = zb3.shape
    nt = t // tr
    kvblk = KV_OFF // 512
    n_kv, n_w = 4 * NSA_KV, 2 * NSA_KV
    zspec = lambda k: pl.BlockSpec((1, tr, 512), lambda bi, ti, k=k: (bi, ti, kvblk + k))
    tspec = pl.BlockSpec((tr, LANE), lambda bi, ti: (ti, 0))
    return pl.pallas_call(
        _kv_export_kernel,
        out_shape=[jax.ShapeDtypeStruct((b * t * n_kv, LANE), F32),
                   jax.ShapeDtypeStruct((b, t, n_kv * LANE), BF16),
                   jax.ShapeDtypeStruct((b * tr * n_w, LANE), F32)],
        grid=(b, nt),
        in_specs=[zspec(1), zspec(2), zspec(0), tspec, tspec, tspec],
        out_specs=[pl.BlockSpec((tr * n_kv, LANE), lambda bi, ti: (bi * nt + ti, 0)),
                   pl.BlockSpec((1, tr, n_kv * LANE), lambda bi, ti: (bi, ti, 0)),
                   pl.BlockSpec((tr * n_w, LANE), lambda bi, ti: (bi, 0))],
        compiler_params=_params(("parallel", "arbitrary")),
        name="kv_export",
    )(zb3, zb3, zb3, *tabs)


def _pool_kernel(*refs, n_prefetch):
    x_ref, a_ref, o_ref = refs[n_prefetch:]
    x = x_ref[0]
    rows = x.shape[0]
    x3 = x.reshape(rows // CMP_BLOCK, CMP_BLOCK, x.shape[1])
    pooled = jnp.sum(x3 * a_ref[...][None], axis=1)
    o_ref[...] = pooled.reshape(o_ref.shape)


def _pool_prompt(z3, a4):
    b, t, _ = z3.shape
    n = t // CMP_BLOCK
    return pl.pallas_call(
        functools.partial(_pool_kernel, n_prefetch=0),
        out_shape=jax.ShapeDtypeStruct((b, n, 512), F32),
        grid=(b,),
        in_specs=[pl.BlockSpec((1, t, 512), lambda bi: (bi, 0, KV_OFF // 512)),
                  pl.BlockSpec((CMP_BLOCK, 512), lambda bi: (0, 0))],
        out_specs=pl.BlockSpec((1, n, 512), lambda bi: (bi, 0, 0)),
        compiler_params=_params(("parallel",)),
        name="pool_prompt",
    )(z3, a4)


def _pool_pages_kernel(pt_sm, a_ref, cache_ref, o_ref, buf, sem, *, pages_per_step, page_base):
    step = pl.program_id(0)
    n_steps = pl.num_programs(0)
    slot = step % 2
    n_cols = buf.shape[3]

    def page_copy(step_idx, p, to_slot):
        page = pt_sm[step_idx * pages_per_step + p] + page_base
        return pltpu.make_async_copy(cache_ref.at[page, :, pl.ds(0, n_cols), :], buf.at[to_slot, p], sem.at[to_slot])

    @pl.when(step == 0)
    def _():
        for p in range(pages_per_step):
            page_copy(0, p, 0).start()

    @pl.when(step + 1 < n_steps)
    def _():
        for p in range(pages_per_step):
            page_copy(step + 1, p, 1 - slot).start()

    for p in range(pages_per_step):
        page_copy(step, p, slot).wait()

    a = a_ref[...]
    per = buf.shape[2] // CMP_BLOCK
    for p in range(pages_per_step):
        for k in range(per):
            x = buf[slot, p, pl.ds(k * CMP_BLOCK, CMP_BLOCK)]
            o_ref[0, p, k] = jnp.sum(x * a, axis=0)


def _pool_pages(cache4, page_flat, a3, *, bs, n_pages, page_base, pages_per_step=16):
    page = cache4.shape[1]
    per = page // CMP_BLOCK
    n_cols = 2 * NSA_KV
    total = bs * n_pages
    pages_per_step = min(pages_per_step, total)
    assert total % pages_per_step == 0
    n_steps = total // pages_per_step
    grid_spec = pltpu.PrefetchScalarGridSpec(
        num_scalar_prefetch=1, grid=(n_steps,),
        in_specs=[pl.BlockSpec((CMP_BLOCK, n_cols, LANE), lambda si, pt: (0, 0, 0)),
                  pl.BlockSpec(memory_space=pl.ANY)],
        out_specs=pl.BlockSpec((1, pages_per_step, per, n_cols, LANE), lambda si, pt: (si, 0, 0, 0, 0)),
        scratch_shapes=[pltpu.VMEM((2, pages_per_step, page, n_cols, LANE), F32),
                        pltpu.SemaphoreType.DMA((2,))])
    out = pl.pallas_call(
        functools.partial(_pool_pages_kernel, pages_per_step=pages_per_step, page_base=page_base),
        out_shape=jax.ShapeDtypeStruct((n_steps, pages_per_step, per, n_cols, LANE), F32),
        grid_spec=grid_spec,
        compiler_params=_params(("arbitrary",)),
        name="pool_pages",
    )(page_flat, a3, cache4)
    return out.reshape(bs, n_pages * per, n_cols * LANE)


def _cmp_proj_kernel(p_ref, w_ref, kc_ref, vc_ref):
    pooled = p_ref[0]
    n = pooled.shape[0]
    n_pad = kc_ref.shape[2]
    for c in range(4):
        r = _nn(pooled[:, c * LANE:(c + 1) * LANE].astype(BF16), w_ref[c]).astype(BF16)
        dst = kc_ref if c < 2 else vc_ref
        if n_pad > n:
            dst[0, c % 2] = jnp.zeros((n_pad, LANE), BF16)
        dst[0, c % 2, 0:n, :] = r


def _cmp_proj(pooled, w4):
    b, n, _ = pooled.shape
    n_pad = -(-n // LANE) * LANE
    spec = pl.BlockSpec((1, NSA_KV, n_pad, LANE), lambda bi: (bi, 0, 0, 0))
    return pl.pallas_call(
        _cmp_proj_kernel,
        out_shape=[jax.ShapeDtypeStruct((b, NSA_KV, n_pad, LANE), BF16)] * 2,
        grid=(b,),
        in_specs=[pl.BlockSpec((1, n, 512), lambda bi: (bi, 0, 0)),
                  pl.BlockSpec((4, LANE, LANE), lambda bi: (0, 0, 0))],
        out_specs=[spec, spec],
        compiler_params=_params(("parallel",)),
        name="cmp_proj",
    )(pooled, w4)


def _lane_parts(x):
    return [x[:, j * LANE:(j + 1) * LANE] for j in range(x.shape[1] // LANE)]


def _nsa_prompt_kernel(*refs, t_len, tq, tc, tw, n_sel, top, pool_pages, page_base):
    if pool_pages:
        (pt_sm, q_ref, ng_ref, bg_ref, ksel_ref, vsel_ref, kwin_ref, vwin_ref, kc_ref, vc_ref, c_ref, s1_ref, s2_ref,
         pa_ref, cache_ref, o_ref, pool_ref,
         qc_scr, qr_scr, s_scr, p_scr, a_scr, m_scr, l_scr, acc_scr, oc_scr, pbuf, psem) = refs
        step = (pl.program_id(0) * pl.num_programs(1) + pl.program_id(1)) * pl.num_programs(2) + pl.program_id(2)
        n_steps = pl.num_programs(0) * pl.num_programs(1) * pl.num_programs(2)
        slot = step % 2
        half_rows, n_cols = pbuf.shape[2], 2 * NSA_KV

        def page_copies(step_idx, p, to_slot):
            page = pt_sm[step_idx * pool_pages + p] + page_base
            return [pltpu.make_async_copy(cache_ref.at[page, pl.ds(hh * half_rows, half_rows), pl.ds(0, n_cols), :],
                                          pbuf.at[to_slot, p, :, pl.ds(hh * n_cols, n_cols), :], psem.at[to_slot])
                    for hh in range(2)]

        @pl.when(step == 0)
        def _():
            for p in range(pool_pages):
                for cp in page_copies(0, p, 0):
                    cp.start()

        @pl.when(step + 1 < n_steps)
        def _():
            for p in range(pool_pages):
                for cp in page_copies(step + 1, p, 1 - slot):
                    cp.start()
    else:
        (q_ref, ng_ref, bg_ref, ksel_ref, vsel_ref, kwin_ref, vwin_ref, kc_ref, vc_ref, c_ref, s1_ref, s2_ref,
         o_ref, qc_scr, qr_scr, s_scr, p_scr, a_scr, m_scr, l_scr, acc_scr, oc_scr) = refs
    i = pl.program_id(2)
    hq = NSA_HPG
    q = q_ref[0]
    c, s1, s2 = c_ref[...], s1_ref[...], s2_ref[...]
    for h in range(hq):
        qh = q[:, h * LANE:(h + 1) * LANE]
        qc_scr[pl.ds(h * tq, tq), :] = (qh * SCALE_LOG2).astype(BF16)
        qr_scr[pl.ds(h * tq, tq), :] = (_rope(qh, c, s1, s2) * SCALE_LOG2).astype(BF16)

    kc = kc_ref[0, 0]
    npad = kc.shape[0]
    s_scr[:, 0:npad] = _nt(qc_scr[...], kc)
    tpos = i * tq + lax.broadcasted_iota(jnp.int32, (tq, npad), 0)
    ncol = lax.broadcasted_iota(jnp.int32, (tq, npad), 1)
    cmask = (ncol + 1) * CMP_BLOCK <= tpos + 1
    imp = jnp.zeros((tq, npad), F32)
    for h in range(hq):
        r = pl.ds(h * tq, tq)
        s = jnp.where(cmask, s_scr[r, 0:npad], MASK_NEG)
        e = jnp.where(cmask, jnp.exp2(s - jnp.max(s, axis=-1, keepdims=True)), 0.0)
        p = e / jnp.maximum(jnp.sum(e, axis=-1, keepdims=True), 1e-30)
        p_scr[r, 0:npad] = p.astype(BF16)
        imp = imp + p
    oc_scr[...] = _nn(p_scr[:, 0:npad], vc_ref[0, 0])

    rows = min(npad, -(-n_sel // 8) * 8)
    imp_t = imp.T[0:rows]
    blk = lax.broadcasted_iota(jnp.int32, (rows, tq), 0)
    cur = (i * tq + lax.broadcasted_iota(jnp.int32, (rows, tq), 1)) // SEL_BLOCK
    imp_t = jnp.where((blk == 0) | (blk == cur) | (blk == cur - 1), FORCE, imp_t)
    imp_t = jnp.where(blk > cur, -1.0, imp_t)
    imp_t = jnp.where(blk >= n_sel, -2.0, imp_t)
    rank = jnp.zeros((rows, tq), F32)
    for j in range(n_sel):
        a = imp_t[j:j + 1, :]
        ahead = (a > imp_t) | ((a == imp_t) & (blk > j))
        rank = rank + jnp.where(ahead, 1.0, 0.0)
    sel_t = jnp.where((rank < top) & (blk < n_sel), 1.0, 0.0)
    if npad > rows:
        sel_t = jnp.concatenate([sel_t, jnp.zeros((npad - rows, tq), F32)], axis=0)
    sel = sel_t.T.astype(BF16)

    m_scr[...] = jnp.full(m_scr.shape, MASK_NEG, F32)
    l_scr[...] = jnp.zeros(l_scr.shape, F32)
    acc_scr[...] = jnp.zeros(acc_scr.shape, F32)
    n_chunks = (i * tq + tq + tc - 1) // tc

    def chunk(ci, carry):
        k0 = pl.multiple_of(ci * tc, tc)
        k = ksel_ref[0, pl.ds(k0, tc), :]
        v = vsel_ref[0, pl.ds(k0, tc), :]
        jb = lax.broadcasted_iota(jnp.int32, (npad, tc), 0)
        kb = (k0 + lax.broadcasted_iota(jnp.int32, (npad, tc), 1)) // SEL_BLOCK
        expand = jnp.where(jb == kb, 1.0, 0.0).astype(BF16)
        chosen = _nn(sel, expand)
        kp = k0 + lax.broadcasted_iota(jnp.int32, (tq, tc), 1)
        tp = i * tq + lax.broadcasted_iota(jnp.int32, (tq, tc), 0)
        bias = jnp.where((chosen > 0.5) & (kp <= tp), 0.0, MASK_NEG)
        s_scr[:, 0:tc] = _nt(qr_scr[...], k)
        for h in range(hq):
            r = pl.ds(h * tq, tq)
            parts = _lane_parts(s_scr[r, 0:tc] + bias)
            m_prev = m_scr[r, :]
            m_new = jnp.maximum(m_prev, jnp.max(functools.reduce(jnp.maximum, parts), axis=-1, keepdims=True))
            alpha = jnp.exp2(m_prev - m_new)
            pes = [jnp.exp2(x - m_new) for x in parts]
            l_scr[r, :] = alpha * l_scr[r, :] + functools.reduce(jnp.add, pes)
            p_scr[r, 0:tc] = jnp.concatenate(pes, axis=1).astype(BF16)
            a_scr[r, :] = alpha
            m_scr[r, :] = m_new
        acc_scr[...] = a_scr[...] * acc_scr[...] + _nn(p_scr[:, 0:tc], v)
        return carry

    lax.fori_loop(0, n_chunks, chunk, 0)

    w0 = pl.multiple_of(jnp.clip(i * tq + tq - tw, 0, t_len - tw), LANE)
    kp = w0 + lax.broadcasted_iota(jnp.int32, (tq, tw), 1)
    tp = i * tq + lax.broadcasted_iota(jnp.int32, (tq, tw), 0)
    wbias = jnp.where((kp <= tp) & (kp > tp - WINDOW), 0.0, MASK_NEG)
    s_scr[:, 0:tw] = _nt(qr_scr[...], kwin_ref[0, pl.ds(w0, tw), :])
    for h in range(hq):
        r = pl.ds(h * tq, tq)
        parts = _lane_parts(s_scr[r, 0:tw] + wbias)
        m = jnp.max(functools.reduce(jnp.maximum, parts), axis=-1, keepdims=True)
        ews = [jnp.exp2(x - m) for x in parts]
        a_scr[r, :] = jnp.broadcast_to(jnp.sum(functools.reduce(jnp.add, ews), axis=-1, keepdims=True), (tq, LANE))
        p_scr[r, 0:tw] = jnp.concatenate(ews, axis=1).astype(BF16)
    o_w = _nn(p_scr[:, 0:tw], vwin_ref[0, pl.ds(w0, tw), :])

    gate = jax.nn.sigmoid(bg_ref[0])
    for g in range(1, NSA_KV):
        gate = jnp.where(pl.program_id(1) == g, pltpu.roll(gate, LANE - g * hq * N_BRANCH, 1), gate)
    ng = ng_ref[0]
    for h in range(hq):
        r = slice(h * tq, (h + 1) * tq)
        o_s = acc_scr[r, :] / jnp.sum(l_scr[r, :], axis=-1, keepdims=True)
        o = (gate[:, 3 * h:3 * h + 1] * oc_scr[r, :] + gate[:, 3 * h + 1:3 * h + 2] * o_s
             + gate[:, 3 * h + 2:3 * h + 3] * (o_w[r] / a_scr[r, :]))
        o_ref[0, :, h * LANE:(h + 1) * LANE] = (_silu(ng[:, h * LANE:(h + 1) * LANE]) * o).astype(o_ref.dtype)

    if pool_pages:
        for p in range(pool_pages):
            for cp in page_copies(step, p, slot):
                cp.wait()
        pa = pa_ref[...]
        for p in range(pool_pages):
            pool_ref[0, p] = jnp.sum(pbuf[slot, p] * pa, axis=0)


def _nsa_prompt(z3, bg3, kvb, kc, vc, tabs, *, tq=128, tc=512, pool=None):
    b, t, _ = z3.shape
    nq = t // tq
    tc = min(tc, t)
    tw = min(WINDOW + tq, t)
    n_sel = t // SEL_BLOCK
    top = min(TOP_N, n_sel)
    npad = kc.shape[2]
    gw = NSA_HPG * LANE
    rows = NSA_HPG * tq
    wide = max(tc, tw, npad)
    kvspec = lambda k: pl.BlockSpec((1, t, LANE), lambda bi, gi, qi, *_, k=k: (bi, 0, k + gi))
    cspec = pl.BlockSpec((1, 1, npad, LANE), lambda bi, gi, qi, *_: (bi, gi, 0, 0))
    tspec = pl.BlockSpec((tq, LANE), lambda bi, gi, qi, *_: (qi, 0))
    in_specs = [pl.BlockSpec((1, tq, gw), lambda bi, gi, qi, *_: (bi, qi, Q_OFF // gw + gi)),
                pl.BlockSpec((1, tq, gw), lambda bi, gi, qi, *_: (bi, qi, NG_OFF // gw + gi)),
                pl.BlockSpec((1, tq, LANE), lambda bi, gi, qi, *_: (bi, qi, 0)),
                kvspec(0), kvspec(2), kvspec(4), kvspec(6), cspec, cspec, tspec, tspec, tspec]
    out_shape = [jax.ShapeDtypeStruct((b, t, NSA_HEADS * LANE), BF16)]
    out_specs = [pl.BlockSpec((1, tq, gw), lambda bi, gi, qi, *_: (bi, qi, gi))]
    scratch = ([pltpu.VMEM((rows, LANE), BF16)] * 2
               + [pltpu.VMEM((rows, wide), F32), pltpu.VMEM((rows, wide), BF16)]
               + [pltpu.VMEM((rows, LANE), F32)] * 5)
    args = [z3, z3, bg3, kvb, kvb, kvb, kvb, kc, vc, *tabs]
    n_steps = b * NSA_KV * nq
    pool_pages, page_base, prefetch = 0, 0, []
    if pool is not None:
        cache4, page_flat, a3, page_base = pool
        page, n_cols = cache4.shape[1], 2 * NSA_KV
        if page_flat.shape[0] % n_steps == 0 and page == 2 * CMP_BLOCK and 2 * n_cols == 8:
            pool_pages = page_flat.shape[0] // n_steps
            prefetch = [page_flat]
            in_specs += [pl.BlockSpec((CMP_BLOCK, 2 * n_cols, LANE), lambda bi, gi, qi, *_: (0, 0, 0)),
                         pl.BlockSpec(memory_space=pl.ANY)]
            args += [jnp.concatenate([a3, a3], axis=1), cache4]
            out_shape.append(jax.ShapeDtypeStruct((n_steps, pool_pages, 2 * n_cols, LANE), F32))
            out_specs.append(pl.BlockSpec((1, pool_pages, 2 * n_cols, LANE),
                                          lambda bi, gi, qi, *_: ((bi * NSA_KV + gi) * nq + qi, 0, 0, 0)))
            scratch += [pltpu.VMEM((2, pool_pages, CMP_BLOCK, 2 * n_cols, LANE), F32), pltpu.SemaphoreType.DMA((2,))]
    kern = functools.partial(_nsa_prompt_kernel, t_len=t, tq=tq, tc=tc, tw=tw, n_sel=n_sel, top=top,
                             pool_pages=pool_pages, page_base=page_base)
    res = pl.pallas_call(
        kern,
        out_shape=out_shape,
        grid_spec=pltpu.PrefetchScalarGridSpec(
            num_scalar_prefetch=len(prefetch), grid=(b, NSA_KV, nq),
            in_specs=in_specs, out_specs=out_specs, scratch_shapes=scratch),
        compiler_params=_params(("arbitrary", "arbitrary", "arbitrary")),
        name="nsa_prompt",
    )(*prefetch, *args)
    return (res[0], res[1]) if pool_pages else (res[0], None)


def _conv_kernel(h_ref, b_ref, c_ref, g_ref, prev_ref, w_ref, y_ref, st_ref, up_scr, *, t_real):
    u = c_ref[0] * h_ref[0]
    t = u.shape[0]
    up_scr[pl.ds(8 - (CONV_W - 1), CONV_W - 1), :] = prev_ref[0]
    up_scr[pl.ds(8, t), :] = u
    w = w_ref[...]
    y = w[0:1, :] * up_scr[pl.ds(6, t), :]
    y = y + w[1:2, :] * up_scr[pl.ds(7, t), :]
    y = y + w[2:3, :] * u
    y = b_ref[0] * y
    y_ref[0] = (_silu(g_ref[0]) * y).astype(y_ref.dtype)
    st_ref[0] = up_scr[pl.ds(6 + t_real, CONV_W - 1), :]


def _conv_mixer(z3, prev, w_conv, *, t_real):
    b, t, _ = z3.shape
    nc = CONV_DIM // LANE
    zspec = lambda off: pl.BlockSpec((1, t, LANE), lambda bi, ci, off=off: (bi, 0, off // LANE + ci))
    return pl.pallas_call(
        functools.partial(_conv_kernel, t_real=t_real),
        out_shape=[jax.ShapeDtypeStruct((b, t, CONV_DIM), BF16),
                   jax.ShapeDtypeStruct((b, CONV_W - 1, CONV_DIM), F32)],
        grid=(b, nc),
        in_specs=[zspec(C_H), zspec(C_B), zspec(C_C), zspec(C_G),
                  pl.BlockSpec((1, CONV_W - 1, LANE), lambda bi, ci: (bi, 0, ci)),
                  pl.BlockSpec((CONV_W, LANE), lambda bi, ci: (0, ci))],
        out_specs=[pl.BlockSpec((1, t, LANE), lambda bi, ci: (bi, 0, ci)),
                   pl.BlockSpec((1, CONV_W - 1, LANE), lambda bi, ci: (bi, 0, ci))],
        scratch_shapes=[pltpu.VMEM((t + 8, LANE), F32)],
        compiler_params=_params(("parallel", "parallel")),
        name="conv_mixer",
    )(z3, z3, z3, z3, prev, w_conv)


def _mem_attn_kernel(q_ref, mg_ref, kv_ref, o_ref, *, interleaved):
    q = q_ref[0]
    mg = mg_ref[0]
    half = MEM_HEADS * LANE
    for h in range(MEM_HEADS):
        lo, hi = h * LANE, (h + 1) * LANE
        if interleaved:
            nm = kv_ref.shape[1] // (2 * MEM_HEADS)
            k = kv_ref[0, pl.ds(h, nm, stride=2 * MEM_HEADS), :].astype(BF16)
            v = kv_ref[0, pl.ds(MEM_HEADS + h, nm, stride=2 * MEM_HEADS), :].astype(BF16)
        else:
            k = kv_ref[0, :, lo:hi].astype(BF16)
            v = kv_ref[0, :, half + lo:half + hi].astype(BF16)
        s = _nt((q[:, lo:hi] * ATTN_SCALE).astype(BF16), k)
        e = jnp.exp(s - jnp.max(s, axis=-1, keepdims=True))
        o = _nn(e.astype(BF16), v) / jnp.sum(e, axis=-1, keepdims=True)
        o_ref[0, :, lo:hi] = (_silu(mg[:, lo:hi]) * o).astype(o_ref.dtype)


def _mem_attn(zb3, mkv, *, tq, interleaved, kv_base=0):
    b, t, _ = zb3.shape
    wq = MEM_HEADS * LANE
    return pl.pallas_call(
        functools.partial(_mem_attn_kernel, interleaved=interleaved),
        out_shape=jax.ShapeDtypeStruct((b, t, wq), BF16),
        grid=(b, t // tq),
        in_specs=[pl.BlockSpec((1, tq, wq), lambda bi, ti: (bi, ti, MQ_OFF // wq)),
                  pl.BlockSpec((1, tq, wq), lambda bi, ti: (bi, ti, MG_OFF // wq)),
                  pl.BlockSpec((1,) + mkv.shape[1:], lambda bi, ti: (bi + kv_base, 0, 0))],
        out_specs=pl.BlockSpec((1, tq, wq), lambda bi, ti: (bi, ti, 0)),
        compiler_params=_params(("parallel", "parallel")),
        name="mem_attn",
    )(zb3, zb3, mkv)


def _out_proj_kernel(x_ref, ya_ref, yb_ref, ym_ref, w_ref, fg_ref, o_ref, *, final):
    a, bw = CONV_DIM, CONV_DIM + NSA_HEADS * LANE
    acc = _nn(ya_ref[...], w_ref[0:a, :])
    acc = acc + _nn(yb_ref[...], w_ref[a:bw, :])
    acc = acc + _nn(ym_ref[...], w_ref[bw:, :])
    r = x_ref[...] + acc
    if final:
        r = r * lax.rsqrt(jnp.mean(r * r, axis=-1, keepdims=True) + NORM_EPS) * fg_ref[...]
    o_ref[...] = r


def _out_proj(x, ya, yb, ym, w, fg, *, tm, final):
    m, d = x.shape
    row = lambda width: pl.BlockSpec((tm, width), lambda i: (i, 0))
    return pl.pallas_call(
        functools.partial(_out_proj_kernel, final=final),
        out_shape=jax.ShapeDtypeStruct((m, d), F32),
        grid=(m // tm,),
        in_specs=[row(d), row(ya.shape[1]), row(yb.shape[1]), row(ym.shape[1]),
                  pl.BlockSpec(w.shape, lambda i: (0, 0)),
                  pl.BlockSpec((1, d), lambda i: (0, 0))],
        out_specs=row(d),
        compiler_params=_params(("parallel",)),
        name="out_proj",
    )(x, ya, yb, ym, w, fg.reshape(1, d))


def _sample_pre_kernel(q_ref, ng_ref, bg_ref, kv0_ref, kv1_ref, kv2_ref, c_ref, s1_ref, s2_ref, kc_ref, vc_ref,
                       kvn_ref, wn_ref, qr_ref, oc_ref, ngo_ref, gate_ref, val_ref, *, past):
    tp = SAMPLE_T_PAD
    hq = NSA_HPG
    c, s1, s2 = c_ref[...], s1_ref[...], s2_ref[...]
    kv0, kv1, kv2 = kv0_ref[0], kv1_ref[0], kv2_ref[0]
    kvn_ref[0, :, 0:512] = kv0
    kvn_ref[0, :, 768:1024] = kv1[:, 256:512]
    wn_ref[0, :, 256:512] = kv2[:, 256:512]
    for g in range(NSA_KV):
        lo, hi = g * LANE, (g + 1) * LANE
        kvn_ref[0, :, 512 + lo:512 + hi] = _rope(kv1[:, lo:hi], c, s1, s2)
        wn_ref[0, :, lo:hi] = _rope(kv2[:, lo:hi], c, s1, s2)

    q = q_ref[0]
    ng = ng_ref[0]
    gates = jax.nn.sigmoid(bg_ref[0])
    for g in range(NSA_KV):
        qc_l, qr_l = [], []
        for h in range(hq):
            lo = (g * hq + h) * LANE
            qh = q[:, lo:lo + LANE]
            qc_l.append(qh * ATTN_SCALE)
            qr_l.append(_rope(qh, c, s1, s2) * ATTN_SCALE)
            ngo_ref[0, g, h * tp:(h + 1) * tp, :] = ng[:, lo:lo + LANE]
            for br in range(N_BRANCH):
                col = (g * hq + h) * N_BRANCH + br
                gate_ref[0, g, br, h * tp:(h + 1) * tp, :] = jnp.broadcast_to(gates[:, col:col + 1], (tp, LANE))
        qc = jnp.concatenate(qc_l, axis=0)
        qr_ref[0, g] = jnp.concatenate(qr_l, axis=0)

        kc = kc_ref[0, g]
        npad = kc.shape[0]
        s = _nt(qc.astype(BF16), kc)
        trow = lax.broadcasted_iota(jnp.int32, (hq * tp, npad), 0) % tp
        ncol = lax.broadcasted_iota(jnp.int32, (hq * tp, npad), 1)
        cmask = (ncol + 1) * CMP_BLOCK <= past + trow + 1
        s = jnp.where(cmask, s, MASK_NEG)
        e = jnp.where(cmask, jnp.exp(s - jnp.max(s, axis=-1, keepdims=True)), 0.0)
        p = e / jnp.maximum(jnp.sum(e, axis=-1, keepdims=True), 1e-30)
        oc_ref[0, g] = _nn(p.astype(BF16), vc_ref[0, g])
        imp = jnp.sum(p.reshape(hq, tp, npad), axis=0)

        blk = lax.broadcasted_iota(jnp.int32, (tp, npad), 1)
        cur = (past + lax.broadcasted_iota(jnp.int32, (tp, npad), 0)) // SEL_BLOCK
        val = jnp.where((blk == 0) | (blk == cur) | (blk == cur - 1), FORCE, imp)
        val = jnp.where(blk > cur, -1.0, val)
        val = jnp.where(blk >= past // SEL_BLOCK, -2.0, val)
        val_ref[0, g] = val


def _topk_kernel(val_ref, idx_ref, *, n_top):
    val = val_ref[...]
    rows, n = val.shape
    blk = lax.broadcasted_iota(jnp.int32, (rows, n), 1)
    lane = lax.broadcasted_iota(jnp.int32, (rows, LANE), 1)
    idx = jnp.zeros((rows, LANE), jnp.int32)
    for r in range(n_top):
        best = jnp.max(val, axis=-1, keepdims=True)
        j = jnp.min(jnp.where(val == best, blk, n), axis=-1, keepdims=True)
        idx = jnp.where(lane == r, j, idx)
        val = jnp.where(blk == j, -3e38, val)
    idx_ref[...] = idx


def _topk(val2, *, n_top):
    rows, n = val2.shape
    return pl.pallas_call(
        functools.partial(_topk_kernel, n_top=n_top),
        out_shape=jax.ShapeDtypeStruct((rows, LANE), jnp.int32),
        grid=(1,),
        in_specs=[pl.BlockSpec((rows, n), lambda i: (0, 0))],
        out_specs=pl.BlockSpec((rows, LANE), lambda i: (0, 0)),
        compiler_params=_params(("arbitrary",)),
        name="sample_topk",
    )(val2)


def _sample_pre(za3, zb3, bg3, tabs, kc, vc, *, past):
    bs, tp, _ = za3.shape
    npad = kc.shape[2]
    qw = NSA_HEADS * LANE
    kvblk = KV_OFF // 512
    zspec = lambda k: pl.BlockSpec((1, tp, 512), lambda bi, k=k: (bi, 0, kvblk + k))
    tspec = pl.BlockSpec((tp, LANE), lambda bi: (0, 0))
    cspec = pl.BlockSpec((1, NSA_KV, npad, LANE), lambda bi: (bi, 0, 0, 0))
    rows = NSA_HPG * tp
    gspec = pl.BlockSpec((1, NSA_KV, rows, LANE), lambda bi: (bi, 0, 0, 0))
    gshape = jax.ShapeDtypeStruct((bs, NSA_KV, rows, LANE), F32)
    return pl.pallas_call(
        functools.partial(_sample_pre_kernel, past=past),
        out_shape=[jax.ShapeDtypeStruct((bs, tp, 1024), F32),
                   jax.ShapeDtypeStruct((bs, tp, 512), F32),
                   gshape, gshape, gshape,
                   jax.ShapeDtypeStruct((bs, NSA_KV, N_BRANCH, rows, LANE), F32),
                   jax.ShapeDtypeStruct((bs, NSA_KV, tp, npad), F32)],
        grid=(bs,),
        in_specs=[pl.BlockSpec((1, tp, qw), lambda bi: (bi, 0, Q_OFF // qw)),
                  pl.BlockSpec((1, tp, qw), lambda bi: (bi, 0, NG_OFF // qw)),
                  pl.BlockSpec((1, tp, LANE), lambda bi: (bi, 0, 0)),
                  zspec(0), zspec(1), zspec(2), tspec, tspec, tspec, cspec, cspec],
        out_specs=[pl.BlockSpec((1, tp, 1024), lambda bi: (bi, 0, 0)),
                   pl.BlockSpec((1, tp, 512), lambda bi: (bi, 0, 0)),
                   gspec, gspec, gspec,
                   pl.BlockSpec((1, NSA_KV, N_BRANCH, rows, LANE), lambda bi: (bi, 0, 0, 0, 0)),
                   pl.BlockSpec((1, NSA_KV, tp, npad), lambda bi: (bi, 0, 0, 0))],
        compiler_params=_params(("parallel",)),
        name="sample_pre",
    )(za3, za3, bg3, zb3, zb3, zb3, *tabs, kc, vc)


def _sample_attn_kernel(idx_sm, pt_sm, qr_ref, oc_ref, ng_ref, gate_ref, ksn_ref, vsn_ref,
                        wc_ref, kwn_ref, vwn_ref, cache_ref, o_ref,
                        kbuf, vbuf, kw_scr, vw_scr, sem, *, ts, n_top, n_pages, page_base, per_page, wb):
    tp = SAMPLE_T_PAD
    hq = NSA_HPG
    b = pl.program_id(0)
    g = pl.program_id(1)
    n_gath = n_top * SEL_BLOCK
    ks_rows = kbuf.shape[1]

    def gather_copies(t, r):
        blk = idx_sm[((b * NSA_KV + g) * ts + t) * n_top + r]
        page = pt_sm[b * n_pages + blk // per_page] + page_base
        row0 = (blk % per_page) * SEL_BLOCK
        src_k = cache_ref.at[page, pl.ds(row0, SEL_BLOCK), 2 * NSA_KV + g]
        src_v = cache_ref.at[page, pl.ds(row0, SEL_BLOCK), 3 * NSA_KV + g]
        dst = pl.ds(r * SEL_BLOCK, SEL_BLOCK)
        return (pltpu.make_async_copy(src_k, kbuf.at[t, dst], sem.at[0]),
                pltpu.make_async_copy(src_v, vbuf.at[t, dst], sem.at[1]))

    for t in range(ts):
        for r in range(n_top):
            ck, cv = gather_copies(t, r)
            ck.start()
            cv.start()

    qr = qr_ref[0, 0].astype(BF16)
    trow = lax.broadcasted_iota(jnp.int32, (hq * tp, 1), 0) % tp

    ww = kw_scr.shape[0]
    kw_scr[pl.ds(0, wb), :] = wc_ref[0, pl.ds(g, wb, stride=2 * NSA_KV), :]
    vw_scr[pl.ds(0, wb), :] = wc_ref[0, pl.ds(NSA_KV + g, wb, stride=2 * NSA_KV), :]
    kw_scr[pl.ds(wb, tp), :] = kwn_ref[0]
    vw_scr[pl.ds(wb, tp), :] = vwn_ref[0]
    kw_scr[pl.ds(wb + tp, ww - wb - tp), :] = jnp.zeros((ww - wb - tp, LANE), F32)
    vw_scr[pl.ds(wb + tp, ww - wb - tp), :] = jnp.zeros((ww - wb - tp, LANE), F32)
    sw = _nt(qr, kw_scr[...].astype(BF16))
    jw = lax.broadcasted_iota(jnp.int32, (hq * tp, ww), 1)
    rel = jw - wb
    okw = (rel <= trow) & (rel > trow - WINDOW) & (jw < wb + ts)
    sw = jnp.where(okw, sw, MASK_NEG)
    ew = jnp.where(okw, jnp.exp(sw - jnp.max(sw, axis=-1, keepdims=True)), 0.0)
    o_w = _nn(ew.astype(BF16), vw_scr[...].astype(BF16)) / jnp.sum(ew, axis=-1, keepdims=True)

    for t in range(ts):
        for r in range(n_top):
            ck, cv = gather_copies(t, r)
            ck.wait()
            cv.wait()

    js = lax.broadcasted_iota(jnp.int32, (hq * tp, ks_rows), 1)
    o_s = jnp.zeros((hq * tp, LANE), F32)
    for t in range(ts):
        kbuf[t, pl.ds(n_gath, tp), :] = ksn_ref[0]
        vbuf[t, pl.ds(n_gath, tp), :] = vsn_ref[0]
        kbuf[t, pl.ds(n_gath + tp, ks_rows - n_gath - tp), :] = jnp.zeros((ks_rows - n_gath - tp, LANE), F32)
        vbuf[t, pl.ds(n_gath + tp, ks_rows - n_gath - tp), :] = jnp.zeros((ks_rows - n_gath - tp, LANE), F32)
        ss = _nt(qr, kbuf[t].astype(BF16))
        oks = (js < n_gath) | ((js - n_gath <= t) & (js < n_gath + ts))
        ss = jnp.where(oks, ss, MASK_NEG)
        es = jnp.where(oks, jnp.exp(ss - jnp.max(ss, axis=-1, keepdims=True)), 0.0)
        ot = _nn(es.astype(BF16), vbuf[t].astype(BF16)) / jnp.sum(es, axis=-1, keepdims=True)
        o_s = jnp.where(trow == t, ot, o_s)

    o = gate_ref[0, 0, 0] * oc_ref[0, 0] + gate_ref[0, 0, 1] * o_s + gate_ref[0, 0, 2] * o_w
    y = _silu(ng_ref[0, 0]) * o
    for h in range(hq):
        o_ref[0, :, h * LANE:(h + 1) * LANE] = y[h * tp:(h + 1) * tp].astype(o_ref.dtype)


def _sample_attn(idx_flat, page_flat, qr, oc, ngo, gates, kvn, cache_win_rows, wn, cache4, *,
                 ts, n_top, n_pages, page_base, win_base):
    bs = qr.shape[0]
    tp = SAMPLE_T_PAD
    rows = NSA_HPG * tp
    wb = cache_win_rows.shape[1] // (2 * NSA_KV)
    per_page = cache4.shape[1] // SEL_BLOCK
    ks_rows = -(-(n_top * SEL_BLOCK + tp) // LANE) * LANE
    ww = -(-(wb + tp) // LANE) * LANE
    gspec = pl.BlockSpec((1, 1, rows, LANE), lambda bi, gi, *_: (bi, gi, 0, 0))
    newspec = lambda k: pl.BlockSpec((1, tp, LANE), lambda bi, gi, *_, k=k: (bi, 0, k + gi))
    grid_spec = pltpu.PrefetchScalarGridSpec(
        num_scalar_prefetch=2, grid=(bs, NSA_KV),
        in_specs=[gspec, gspec, gspec,
                  pl.BlockSpec((1, 1, N_BRANCH, rows, LANE), lambda bi, gi, *_: (bi, gi, 0, 0, 0)),
                  newspec(2 * NSA_KV), newspec(3 * NSA_KV),
                  pl.BlockSpec((1,) + cache_win_rows.shape[1:], lambda bi, gi, *_: (bi + win_base, 0, 0)),
                  newspec(0), newspec(NSA_KV),
                  pl.BlockSpec(memory_space=pl.ANY)],
        out_specs=pl.BlockSpec((1, tp, NSA_HPG * LANE), lambda bi, gi, *_: (bi, 0, gi)),
        scratch_shapes=[pltpu.VMEM((ts, ks_rows, LANE), F32), pltpu.VMEM((ts, ks_rows, LANE), F32),
                        pltpu.VMEM((ww, LANE), F32), pltpu.VMEM((ww, LANE), F32),
                        pltpu.SemaphoreType.DMA((2,))])
    kern = functools.partial(_sample_attn_kernel, ts=ts, n_top=n_top, n_pages=n_pages,
                             page_base=page_base, per_page=per_page, wb=wb)
    return pl.pallas_call(
        kern,
        out_shape=jax.ShapeDtypeStruct((bs, tp, NSA_HEADS * LANE), BF16),
        grid_spec=grid_spec,
        compiler_params=_params(("arbitrary", "arbitrary")),
        name="sample_attn",
    )(idx_flat, page_flat, qr, oc, ngo, gates, kvn, kvn, cache_win_rows, wn, wn, cache4)


def _rope_tables(pos, rows):
    freqs = jnp.power(ROPE_THETA, -jnp.arange(ROPE_HALF, dtype=F32) * (2.0 / ROPE_DIM))
    ang = pos.astype(F32)[:, None] * freqs[None, :]
    cos, sin = jnp.cos(ang), jnp.sin(ang)
    n = pos.shape[0]
    z16 = jnp.zeros((n, ROPE_HALF), F32)
    rest = LANE - ROPE_DIM
    c = jnp.concatenate([cos, cos, jnp.ones((n, rest), F32)], axis=1)
    s1 = jnp.concatenate([z16, sin, jnp.zeros((n, rest), F32)], axis=1)
    s2 = jnp.concatenate([-sin, z16, jnp.zeros((n, rest), F32)], axis=1)
    pad = lambda a: jnp.pad(a, ((0, rows - n), (0, 0)))
    return pad(c), pad(s1), pad(s2)


def _layer_weights(norm_g, w_in3, layer, w_conv, a_cmp, w_cmp, w_out):
    w_a, w_b, wbg = _pack_in_proj_weights(jnp.swapaxes(w_in3, 1, 2), layer)
    a4 = jnp.concatenate([a_cmp[0], a_cmp[0], a_cmp[1], a_cmp[1]], axis=1)
    a3 = jnp.stack([a_cmp[0], a_cmp[0], a_cmp[1], a_cmp[1]], axis=1)
    w4 = jnp.stack([w_cmp[0], w_cmp[0], w_cmp[1], w_cmp[1]]).astype(BF16)
    return dict(norm_g=norm_g, w_a=w_a, w_b=w_b, wbg=wbg, w_conv=w_conv, a4=a4, a3=a3, w4=w4,
                w_out=w_out.astype(BF16))


def _in_proj(x2, lw, *, tm):
    za, bg = _norm_matmul(x2, lw["norm_g"], lw["w_a"], lw["wbg"], tm=tm, tn=512)
    zb = _norm_matmul(x2, lw["norm_g"], lw["w_b"], tm=tm, tn=512)
    return za, zb, bg


def _prompt_layer(xp, mem_prompt, mem_norm_g, w_mem, lw, final_g, final, pool):
    b, t, d = xp.shape
    m = b * t
    za, zb, bg = _in_proj(xp.reshape(m, d), lw, tm=min(512, m))
    za3 = za.reshape(b, t, ZA_WIDTH)
    zb3 = zb.reshape(b, t, ZB_WIDTH)
    bg3 = bg.reshape(b, t, LANE)
    nm = mem_prompt.shape[1]
    mkv = _norm_matmul(mem_prompt.reshape(b * nm, d), mem_norm_g, w_mem, tm=min(512, b * nm), tn=512)
    mkv3 = mkv.reshape(b, nm, 2 * MEM_HEADS * LANE)
    tabs = _rope_tables(jnp.arange(t, dtype=jnp.int32), t)
    wb = min(WINDOW, t)
    kvn, kvb, win_rows = _kv_export(zb3, tabs, tr=wb)
    kc, vc = _cmp_proj(_pool_prompt(zb3, lw["a4"]), lw["w4"])
    yb, pooled = _nsa_prompt(za3, bg3, kvb, kc, vc, tabs, tq=min(256, t), pool=pool)
    conv0 = jnp.zeros((b, CONV_W - 1, CONV_DIM), F32)
    ya, conv_new = _conv_mixer(za3, conv0, lw["w_conv"], t_real=t)
    ym = _mem_attn(zb3, mkv3, tq=min(512, t), interleaved=False)
    out = _out_proj(xp.reshape(m, d), ya.reshape(m, -1), yb.reshape(m, -1), ym.reshape(m, -1),
                    lw["w_out"], final_g, tm=min(512, m), final=final)
    kv_new = kvn.reshape(b, t, 4, NSA_KV, HEAD_DIM)
    win_new = win_rows.reshape(b, wb, 2, NSA_KV, HEAD_DIM)
    mem_kv = mkv.reshape(b, nm, 2, MEM_HEADS, HEAD_DIM)
    return out.reshape(b, t, d), kv_new, win_new, conv_new, mem_kv, pooled


def _sample_layer(xs_p, ts, layer, cache4, page_flat, pooled, cache_win, state_conv, cache_mem, lw, final_g, final):
    bs, tp, d = xs_p.shape
    depth = cache_win.shape[0]
    pool, page = cache4.shape[0] // depth, cache4.shape[1]
    n_pages = page_flat.shape[0] // bs
    past = n_pages * page
    assert past % SEL_BLOCK == 0 and ts <= SEL_BLOCK and ts <= tp
    n_past = past // SEL_BLOCK
    n_top = min(TOP_N, n_past + 1) - 1
    m = bs * tp
    za, zb, bg = _in_proj(xs_p.reshape(m, d), lw, tm=m)
    za3 = za.reshape(bs, tp, ZA_WIDTH)
    zb3 = zb.reshape(bs, tp, ZB_WIDTH)
    bg3 = bg.reshape(bs, tp, LANE)
    tabs = _rope_tables(past + jnp.arange(tp, dtype=jnp.int32), tp)
    if pooled is None:
        pooled = _pool_pages(cache4, page_flat, lw["a3"], bs=bs, n_pages=n_pages, page_base=layer * pool)
    else:
        pooled = pooled.reshape(bs, n_pages * (page // CMP_BLOCK), 2 * NSA_KV * LANE)
    kc, vc = _cmp_proj(pooled, lw["w4"])
    kvn, wn, qr, oc, ngo, gates, val = _sample_pre(za3, zb3, bg3, tabs, kc, vc, past=past)
    idx = _topk(val.reshape(bs * NSA_KV * tp, val.shape[-1]), n_top=n_top)
    idx_flat = idx.reshape(bs, NSA_KV, tp, LANE)[:, :, :ts, :n_top].reshape(-1)
    wbuf = cache_win.shape[2]
    cache_win_rows = cache_win.reshape(depth * bs, wbuf * 2 * NSA_KV, HEAD_DIM)
    yb = _sample_attn(idx_flat, page_flat, qr, oc, ngo, gates, kvn, cache_win_rows, wn, cache4,
                      ts=ts, n_top=n_top, n_pages=n_pages, page_base=layer * pool, win_base=layer * bs)
    ya, conv_new = _conv_mixer(za3, state_conv[layer], lw["w_conv"], t_real=ts)
    nm = cache_mem.shape[2]
    mem_rows = cache_mem.reshape(depth * bs, nm * 2 * MEM_HEADS, HEAD_DIM)
    ym = _mem_attn(zb3, mem_rows, tq=tp, interleaved=True, kv_base=layer * bs)
    out = _out_proj(xs_p.reshape(m, d), ya.reshape(m, -1), yb.reshape(m, -1), ym.reshape(m, -1),
                    lw["w_out"], final_g, tm=m, final=final)
    kv_new = kvn[:, :ts].reshape(bs, ts, 4, NSA_KV, HEAD_DIM)
    win_rows = wn[:, :ts].reshape(bs, ts, 2, NSA_KV, HEAD_DIM)
    win_state = jnp.concatenate([cache_win[layer], win_rows], axis=1)[:, ts:]
    return out.reshape(bs, tp, d), kv_new, win_state, conv_new


def kernel(x_prompt, x_sample, cache_kv, cache_win, state_conv, cache_mem, page_table, mem_prompt,
           norm_g, w_in, w_conv, a_cmp, w_cmp, mem_norm_g, w_mem_kv, w_out, final_g):
    depth = w_in.shape[0]
    ts = x_sample.shape[1]
    xp = x_prompt
    xs = jnp.pad(x_sample, ((0, 0), (0, SAMPLE_T_PAD - ts), (0, 0)))
    pool_size, page = cache_kv.shape[1], cache_kv.shape[2]
    cache4 = cache_kv.reshape(depth * pool_size, page, 4 * NSA_KV, HEAD_DIM)
    page_flat = page_table.reshape(-1).astype(jnp.int32)
    kv_p, win_p, conv_p, mem_p, kv_s, win_s, conv_s = [], [], [], [], [], [], []
    for l in range(depth):
        lw = _layer_weights(norm_g[l], w_in, l, w_conv[l], a_cmp[l], w_cmp[l], w_out[l])
        final = l == depth - 1
        xp, kvn, winn, convn, mkv, pooled = _prompt_layer(
            xp, mem_prompt, mem_norm_g[l], w_mem_kv[l].T.astype(BF16), lw, final_g, final,
            pool=(cache4, page_flat, lw["a3"], l * pool_size))
        kv_p.append(kvn)
        win_p.append(winn)
        conv_p.append(convn)
        mem_p.append(mkv)
        xs, kvn, winn, convn = _sample_layer(xs, ts, l, cache4, page_flat, pooled, cache_win, state_conv,
                                             cache_mem, lw, final_g, final)
        kv_s.append(kvn)
        win_s.append(winn)
        conv_s.append(convn)
    return (xp, xs[:, :ts], jnp.stack(kv_p), jnp.stack(win_p), jnp.stack(conv_p), jnp.stack(mem_p),
            jnp.stack(kv_s), jnp.stack(win_s), jnp.stack(conv_s))
```

```python
import functools

import jax
import jax.numpy as jnp
from jax import lax
from jax.experimental import pallas as pl
from jax.experimental.pallas import tpu as pltpu

F32 = jnp.float32
BF16 = jnp.bfloat16

HEAD_DIM = 128
CONV_DIM = 512
CONV_W = 3
NSA_HEADS = 8
NSA_KV = 2
NSA_HPG = NSA_HEADS // NSA_KV
MEM_HEADS = 4
N_BRANCH = 3
ROPE_DIM = HEAD_DIM // 4
ROPE_HALF = ROPE_DIM // 2
ROPE_THETA = 500000.0
CMP_BLOCK = 64
SEL_BLOCK = 64
TOP_N = 16
WINDOW = 512
NORM_EPS = 1e-6
MASK_NEG = -1e30
FORCE = 1e9
ATTN_SCALE = HEAD_DIM ** -0.5
SCALE_LOG2 = ATTN_SCALE * 1.4426950408889634

C_H, C_B, C_C, C_G, Q_OFF, NG_OFF = 0, 512, 1024, 1536, 2048, 3072
ZA_WIDTH = 4096
KV_OFF, MQ_OFF, MG_OFF = 0, 1536, 2048
ZB_WIDTH = 2560
BG_SRC = 4096
BG_N = NSA_HEADS * N_BRANCH
LANE = 128
SAMPLE_T_PAD = 8
VMEM_LIMIT = 56 * 1024 * 1024


def _nt(a, b):
    return lax.dot_general(a, b, (((1,), (1,)), ((), ())), preferred_element_type=F32)


def _nn(a, b):
    return jnp.dot(a, b, preferred_element_type=F32)


def _params(sem, vmem=VMEM_LIMIT):
    return pltpu.CompilerParams(dimension_semantics=sem, vmem_limit_bytes=vmem)


def _rope(x, c, s1, s2):
    return x * c + pltpu.roll(x, ROPE_HALF, 1) * s1 + pltpu.roll(x, LANE - ROPE_HALF, 1) * s2


def _silu(x):
    return x * jax.nn.sigmoid(x)


def _norm_matmul_kernel(x_ref, g_ref, w_ref, *rest, with_gate, tn):
    if with_gate:
        wbg_ref, z_ref, bg_ref, h_scr = rest
    else:
        z_ref, h_scr = rest
    x = x_ref[...]
    y = x * lax.rsqrt(jnp.mean(x * x, axis=-1, keepdims=True) + NORM_EPS) * g_ref[...]
    h_scr[...] = y.astype(BF16)
    if with_gate:
        bg_ref[...] = _nt(h_scr[...], wbg_ref[...])
    for j in range(w_ref.shape[0] // tn):
        z_ref[:, j * tn:(j + 1) * tn] = _nt(h_scr[...], w_ref[j * tn:(j + 1) * tn, :])


def _norm_matmul(x, g, w, wbg=None, *, tm, tn):
    m, d = x.shape
    n = w.shape[0]
    assert n % tn == 0 and m % tm == 0 and w.shape[1] == d
    with_gate = wbg is not None
    resident = lambda shape: pl.BlockSpec(shape, lambda i: (0, 0), pipeline_mode=pl.Buffered(1))
    in_specs = [pl.BlockSpec((tm, d), lambda i: (i, 0)), resident((1, d)), resident((n, d))]
    out_shape = [jax.ShapeDtypeStruct((m, n), F32)]
    out_specs = [pl.BlockSpec((tm, n), lambda i: (i, 0))]
    args = [x, g.reshape(1, d), w]
    if with_gate:
        nb = wbg.shape[0]
        in_specs.append(resident((nb, d)))
        out_shape.append(jax.ShapeDtypeStruct((m, nb), F32))
        out_specs.append(pl.BlockSpec((tm, nb), lambda i: (i, 0)))
        args.append(wbg)
    res = pl.pallas_call(
        functools.partial(_norm_matmul_kernel, with_gate=with_gate, tn=tn),
        out_shape=out_shape, grid=(m // tm,), in_specs=in_specs, out_specs=out_specs,
        scratch_shapes=[pltpu.VMEM((tm, d), BF16)],
        compiler_params=_params(("parallel",)),
        name="norm_matmul_gate" if with_gate else "norm_matmul",
    )(*args)
    return res if with_gate else res[0]


def _rms_rows(x_ref, g_ref):
    x = x_ref[...]
    return (x * lax.rsqrt(jnp.mean(x * x, axis=-1, keepdims=True) + NORM_EPS) * g_ref[...]).astype(BF16)


def _proj_conv_kernel(x_ref, g_ref, w_ref, wbg_ref, wc_ref, zqn_ref, bg_ref, ya_ref, st_ref, h_scr, up_scr,
                      *, blocks_per_seq):
    tm = x_ref.shape[0]
    cw = CONV_DIM
    i = pl.program_id(0)
    first = i % blocks_per_seq == 0
    h_scr[...] = _rms_rows(x_ref, g_ref)
    bg_ref[...] = _nt(h_scr[...], wbg_ref[...])
    chunk = lambda off: _nt(h_scr[...], w_ref[off:off + cw, :])

    @pl.when(first)
    def _():
        up_scr[pl.ds(8 - (CONV_W - 1), CONV_W - 1), :] = jnp.zeros((CONV_W - 1, cw), F32)

    @pl.when(jnp.logical_not(first))
    def _():
        up_scr[pl.ds(8 - (CONV_W - 1), CONV_W - 1), :] = up_scr[pl.ds(8 + tm - (CONV_W - 1), CONV_W - 1), :]

    u = chunk(C_C) * chunk(C_H)
    up_scr[pl.ds(8, tm), :] = u
    wc = wc_ref[...]
    y = wc[0:1, :] * up_scr[pl.ds(6, tm), :]
    y = y + wc[1:2, :] * up_scr[pl.ds(7, tm), :]
    y = y + wc[2:3, :] * u
    y = chunk(C_B) * y
    ya_ref[...] = (_silu(chunk(C_G)) * y).astype(ya_ref.dtype)

    @pl.when(i % blocks_per_seq == blocks_per_seq - 1)
    def _():
        st_ref[0] = up_scr[pl.ds(8 + tm - (CONV_W - 1), CONV_W - 1), :]

    for j in range(zqn_ref.shape[1] // cw):
        zqn_ref[:, j * cw:(j + 1) * cw] = chunk(Q_OFF + j * cw)


def _proj_conv(x2, g, w_a, wbg, w_conv, *, tm, seq_len):
    m, d = x2.shape
    assert seq_len % tm == 0 and CONV_W == 3
    blocks_per_seq = seq_len // tm
    nq = ZA_WIDTH - Q_OFF
    resident = lambda shape: pl.BlockSpec(shape, lambda i: (0,) * len(shape), pipeline_mode=pl.Buffered(1))
    row = lambda width: pl.BlockSpec((tm, width), lambda i: (i, 0))
    return pl.pallas_call(
        functools.partial(_proj_conv_kernel, blocks_per_seq=blocks_per_seq),
        out_shape=[jax.ShapeDtypeStruct((m, nq), F32), jax.ShapeDtypeStruct((m, LANE), F32),
                   jax.ShapeDtypeStruct((m, CONV_DIM), BF16),
                   jax.ShapeDtypeStruct((m // seq_len, CONV_W - 1, CONV_DIM), F32)],
        grid=(m // tm,),
        in_specs=[row(d), resident((1, d)), resident(w_a.shape), resident(wbg.shape), resident(w_conv.shape)],
        out_specs=[row(nq), row(LANE), row(CONV_DIM),
                   pl.BlockSpec((1, CONV_W - 1, CONV_DIM), lambda i: (i // blocks_per_seq, 0, 0))],
        scratch_shapes=[pltpu.VMEM((tm, d), BF16), pltpu.VMEM((tm + 8, CONV_DIM), F32)],
        compiler_params=_params(("arbitrary",)),
        name="proj_conv",
    )(x2, g.reshape(1, d), w_a, wbg, w_conv)


def _proj_kv_kernel(x_ref, g_ref, w_ref, c_ref, s1_ref, s2_ref, a_ref, kvn_ref, kvb_ref, win_ref, pool_ref, zm_ref,
                    h_scr, *, blocks_per_seq):
    tm = x_ref.shape[0]
    cw = 2 * NSA_KV * LANE
    h_scr[...] = _rms_rows(x_ref, g_ref)
    chunk = lambda off: _nt(h_scr[...], w_ref[off:off + cw, :])
    c, s1, s2 = c_ref[...], s1_ref[...], s2_ref[...]
    kv0 = chunk(KV_OFF)
    kv1 = chunk(KV_OFF + cw)
    kv2 = chunk(KV_OFF + 2 * cw)
    n_kv, n_w, half = 4 * NSA_KV, 2 * NSA_KV, NSA_KV * LANE
    last = pl.program_id(0) % blocks_per_seq == blocks_per_seq - 1
    for g in range(NSA_KV):
        lo, hi = g * LANE, (g + 1) * LANE
        ks = _rope(kv1[:, lo:hi], c, s1, s2)
        kw = _rope(kv2[:, lo:hi], c, s1, s2)
        vs = kv1[:, half + lo:half + hi]
        vw = kv2[:, half + lo:half + hi]
        kvn_ref[pl.ds(g, tm, stride=n_kv), :] = kv0[:, lo:hi]
        kvn_ref[pl.ds(NSA_KV + g, tm, stride=n_kv), :] = kv0[:, half + lo:half + hi]
        kvn_ref[pl.ds(2 * NSA_KV + g, tm, stride=n_kv), :] = ks
        kvn_ref[pl.ds(3 * NSA_KV + g, tm, stride=n_kv), :] = vs
        kvb_ref[:, lo:hi] = ks.astype(BF16)
        kvb_ref[:, half + lo:half + hi] = vs.astype(BF16)
        kvb_ref[:, 2 * half + lo:2 * half + hi] = kw.astype(BF16)
        kvb_ref[:, 3 * half + lo:3 * half + hi] = vw.astype(BF16)

        @pl.when(last)
        def _():
            win_ref[pl.ds(g, tm, stride=n_w), :] = kw
            win_ref[pl.ds(NSA_KV + g, tm, stride=n_w), :] = vw

    pool_ref[...] = jnp.sum(kv0.reshape(tm // CMP_BLOCK, CMP_BLOCK, cw) * a_ref[...][None], axis=1)
    for j in range(zm_ref.shape[1] // cw):
        zm_ref[:, j * cw:(j + 1) * cw] = chunk(MQ_OFF + j * cw)


def _proj_kv(x2, g, w_b, tabs, a4, *, tm, seq_len):
    m, d = x2.shape
    assert seq_len % tm == 0
    blocks_per_seq = seq_len // tm
    n_kv, n_w = 4 * NSA_KV, 2 * NSA_KV
    nm = ZB_WIDTH - MQ_OFF
    resident = lambda shape: pl.BlockSpec(shape, lambda i: (0,) * len(shape), pipeline_mode=pl.Buffered(1))
    row = lambda width: pl.BlockSpec((tm, width), lambda i: (i, 0))
    tspec = pl.BlockSpec((tm, LANE), lambda i: (i % blocks_per_seq, 0))
    return pl.pallas_call(
        functools.partial(_proj_kv_kernel, blocks_per_seq=blocks_per_seq),
        out_shape=[jax.ShapeDtypeStruct((m * n_kv, LANE), F32),
                   jax.ShapeDtypeStruct((m, n_kv * LANE), BF16),
                   jax.ShapeDtypeStruct((m // seq_len * tm * n_w, LANE), F32),
                   jax.ShapeDtypeStruct((m // CMP_BLOCK, 2 * NSA_KV * LANE), F32),
                   jax.ShapeDtypeStruct((m, nm), F32)],
        grid=(m // tm,),
        in_specs=[row(d), resident((1, d)), resident(w_b.shape), tspec, tspec, tspec, resident(a4.shape)],
        out_specs=[pl.BlockSpec((tm * n_kv, LANE), lambda i: (i, 0)), row(n_kv * LANE),
                   pl.BlockSpec((tm * n_w, LANE), lambda i: (i // blocks_per_seq, 0)),
                   pl.BlockSpec((tm // CMP_BLOCK, 2 * NSA_KV * LANE), lambda i: (i, 0)), row(nm)],
        scratch_shapes=[pltpu.VMEM((tm, d), BF16)],
        compiler_params=_params(("arbitrary",)),
        name="proj_kv",
    )(x2, g.reshape(1, d), w_b, *tabs, a4)


def _cast_rows_kernel(w_ref, o_ref):
    o_ref[...] = w_ref[0].astype(BF16)


def _pack_gate_kernel(w_ref, o_ref):
    o_ref[...] = jnp.zeros(o_ref.shape, BF16)
    o_ref[0:w_ref.shape[1], :] = w_ref[0].astype(BF16)


def _pack_in_proj_weights(w_t3, layer):
    _, n_in, d = w_t3.shape
    tn = 512
    assert n_in == BG_SRC + BG_N + ZB_WIDTH and BG_SRC % tn == 0 and ZB_WIDTH % tn == 0 and BG_N % 8 == 0

    def cast_rows(first_row, n_rows, name):
        return pl.pallas_call(
            _cast_rows_kernel,
            out_shape=jax.ShapeDtypeStruct((n_rows, d), BF16),
            grid=(n_rows // tn,),
            in_specs=[pl.BlockSpec((pl.Element(1), pl.Element(tn), pl.Element(d)),
                                   lambda j: (layer, pl.multiple_of(first_row + j * tn, 8), 0))],
            out_specs=pl.BlockSpec((tn, d), lambda j: (j, 0)),
            compiler_params=_params(("parallel",)),
            name=name,
        )(w_t3)

    wa = cast_rows(0, ZA_WIDTH, "pack_w_a")
    wb = cast_rows(BG_SRC + BG_N, ZB_WIDTH, "pack_w_b")
    wbg = pl.pallas_call(
        _pack_gate_kernel,
        out_shape=jax.ShapeDtypeStruct((LANE, d), BF16),
        grid=(1,),
        in_specs=[pl.BlockSpec((pl.Element(1), pl.Element(BG_N), pl.Element(d)), lambda j: (layer, BG_SRC, 0))],
        out_specs=pl.BlockSpec((LANE, d), lambda j: (0, 0)),
        compiler_params=_params(("arbitrary",)),
        name="pack_w_gate",
    )(w_t3)
    return wa, wb, wbg


def _kv_export_kernel(kv1_ref, kv2_ref, kv0_ref, c_ref, s1_ref, s2_ref, kvn_ref, kvb_ref, win_ref):
    c, s1, s2 = c_ref[...], s1_ref[...], s2_ref[...]
    kv0 = kv0_ref[0]
    kv1 = kv1_ref[0]
    kv2 = kv2_ref[0]
    tr = kv0.shape[0]
    n_kv, n_w, half = 4 * NSA_KV, 2 * NSA_KV, NSA_KV * LANE
    last = pl.program_id(1) == pl.num_programs(1) - 1
    for g in range(NSA_KV):
        lo, hi = g * LANE, (g + 1) * LANE
        ks = _rope(kv1[:, lo:hi], c, s1, s2)
        kw = _rope(kv2[:, lo:hi], c, s1, s2)
        vs = kv1[:, half + lo:half + hi]
        vw = kv2[:, half + lo:half + hi]
        kvn_ref[pl.ds(g, tr, stride=n_kv), :] = kv0[:, lo:hi]
        kvn_ref[pl.ds(NSA_KV + g, tr, stride=n_kv), :] = kv0[:, half + lo:half + hi]
        kvn_ref[pl.ds(2 * NSA_KV + g, tr, stride=n_kv), :] = ks
        kvn_ref[pl.ds(3 * NSA_KV + g, tr, stride=n_kv), :] = vs
        kvb_ref[0, :, lo:hi] = ks.astype(BF16)
        kvb_ref[0, :, half + lo:half + hi] = vs.astype(BF16)
        kvb_ref[0, :, 2 * half + lo:2 * half + hi] = kw.astype(BF16)
        kvb_ref[0, :, 3 * half + lo:3 * half + hi] = vw.astype(BF16)

        @pl.when(last)
        def _():
            win_ref[pl.ds(g, tr, stride=n_w), :] = kw
            win_ref[pl.ds(NSA_KV + g, tr, stride=n_w), :] = vw


def _kv_export(zb3, tabs, *, tr):
    b, t, _ = zb3.shape
    nt = t // tr
    kvblk = KV_OFF // 512
    n_kv, n_w = 4 * NSA_KV, 2 * NSA_KV
    zspec = lambda k: pl.BlockSpec((1, tr, 512), lambda bi, ti, k=k: (bi, ti, kvblk + k))
    tspec = pl.BlockSpec((tr, LANE), lambda bi, ti: (ti, 0))
    return pl.pallas_call(
        _kv_export_kernel,
        out_shape=[jax.ShapeDtypeStruct((b * t * n_kv, LANE), F32),
                   jax.ShapeDtypeStruct((b, t, n_kv * LANE), BF16),
                   jax.ShapeDtypeStruct((b * tr * n_w, LANE), F32)],
        grid=(b, nt),
        in_specs=[zspec(1), zspec(2), zspec(0), tspec, tspec, tspec],
        out_specs=[pl.BlockSpec((tr * n_kv, LANE), lambda bi, ti: (bi * nt + ti, 0)),
                   pl.BlockSpec((1, tr, n_kv * LANE), lambda bi, ti: (bi, ti, 0)),
                   pl.BlockSpec((tr * n_w, LANE), lambda bi, ti: (bi, 0))],
        compiler_params=_params(("parallel", "arbitrary")),
        name="kv_export",
    )(zb3, zb3, zb3, *tabs)


def _pool_kernel(*refs, n_prefetch):
    x_ref, a_ref, o_ref = refs[n_prefetch:]
    x = x_ref[0]
    rows = x.shape[0]
    x3 = x.reshape(rows // CMP_BLOCK, CMP_BLOCK, x.shape[1])
    pooled = jnp.sum(x3 * a_ref[...][None], axis=1)
    o_ref[...] = pooled.reshape(o_ref.shape)


def _pool_prompt(z3, a4):
    b, t, _ = z3.shape
    n = t // CMP_BLOCK
    return pl.pallas_call(
        functools.partial(_pool_kernel, n_prefetch=0),
        out_shape=jax.ShapeDtypeStruct((b, n, 512), F32),
        grid=(b,),
        in_specs=[pl.BlockSpec((1, t, 512), lambda bi: (bi, 0, KV_OFF // 512)),
                  pl.BlockSpec((CMP_BLOCK, 512), lambda bi: (0, 0))],
        out_specs=pl.BlockSpec((1, n, 512), lambda bi: (bi, 0, 0)),
        compiler_params=_params(("parallel",)),
        name="pool_prompt",
    )(z3, a4)


def _pool_pages_kernel(pt_sm, a_ref, cache_ref, o_ref, buf, sem, *, pages_per_step, page_base):
    step = pl.program_id(0)
    n_steps = pl.num_programs(0)
    slot = step % 2
    n_cols = buf.shape[3]

    def page_copy(step_idx, p, to_slot):
        page = pt_sm[step_idx * pages_per_step + p] + page_base
        return pltpu.make_async_copy(cache_ref.at[page, :, pl.ds(0, n_cols), :], buf.at[to_slot, p], sem.at[to_slot])

    @pl.when(step == 0)
    def _():
        for p in range(pages_per_step):
            page_copy(0, p, 0).start()

    @pl.when(step + 1 < n_steps)
    def _():
        for p in range(pages_per_step):
            page_copy(step + 1, p, 1 - slot).start()

    for p in range(pages_per_step):
        page_copy(step, p, slot).wait()

    a = a_ref[...]
    per = buf.shape[2] // CMP_BLOCK
    for p in range(pages_per_step):
        for k in range(per):
            x = buf[slot, p, pl.ds(k * CMP_BLOCK, CMP_BLOCK)]
            o_ref[0, p, k] = jnp.sum(x * a, axis=0)


def _pool_pages(cache4, page_flat, a3, *, bs, n_pages, page_base, pages_per_step=16):
    page = cache4.shape[1]
    per = page // CMP_BLOCK
    n_cols = 2 * NSA_KV
    total = bs * n_pages
    pages_per_step = min(pages_per_step, total)
    assert total % pages_per_step == 0
    n_steps = total // pages_per_step
    grid_spec = pltpu.PrefetchScalarGridSpec(
        num_scalar_prefetch=1, grid=(n_steps,),
        in_specs=[pl.BlockSpec((CMP_BLOCK, n_cols, LANE), lambda si, pt: (0, 0, 0)),
                  pl.BlockSpec(memory_space=pl.ANY)],
        out_specs=pl.BlockSpec((1, pages_per_step, per, n_cols, LANE), lambda si, pt: (si, 0, 0, 0, 0)),
        scratch_shapes=[pltpu.VMEM((2, pages_per_step, page, n_cols, LANE), F32),
                        pltpu.SemaphoreType.DMA((2,))])
    out = pl.pallas_call(
        functools.partial(_pool_pages_kernel, pages_per_step=pages_per_step, page_base=page_base),
        out_shape=jax.ShapeDtypeStruct((n_steps, pages_per_step, per, n_cols, LANE), F32),
        grid_spec=grid_spec,
        compiler_params=_params(("arbitrary",)),
        name="pool_pages",
    )(page_flat, a3, cache4)
    return out.reshape(bs, n_pages * per, n_cols * LANE)


def _cmp_proj_kernel(p_ref, w_ref, kc_ref, vc_ref):
    pooled = p_ref[0]
    n = pooled.shape[0]
    n_pad = kc_ref.shape[2]
    for c in range(4):
        r = _nn(pooled[:, c * LANE:(c + 1) * LANE].astype(BF16), w_ref[c]).astype(BF16)
        dst = kc_ref if c < 2 else vc_ref
        if n_pad > n:
            dst[0, c % 2] = jnp.zeros((n_pad, LANE), BF16)
        dst[0, c % 2, 0:n, :] = r


def _cmp_proj(pooled, w4):
    b, n, _ = pooled.shape
    n_pad = -(-n // LANE) * LANE
    spec = pl.BlockSpec((1, NSA_KV, n_pad, LANE), lambda bi: (bi, 0, 0, 0))
    return pl.pallas_call(
        _cmp_proj_kernel,
        out_shape=[jax.ShapeDtypeStruct((b, NSA_KV, n_pad, LANE), BF16)] * 2,
        grid=(b,),
        in_specs=[pl.BlockSpec((1, n, 512), lambda bi: (bi, 0, 0)),
                  pl.BlockSpec((4, LANE, LANE), lambda bi: (0, 0, 0))],
        out_specs=[spec, spec],
        compiler_params=_params(("parallel",)),
        name="cmp_proj",
    )(pooled, w4)


def _lane_parts(x):
    return [x[:, j * LANE:(j + 1) * LANE] for j in range(x.shape[1] // LANE)]


def _nsa_prompt_kernel(*refs, t_len, tq, tc, tw, n_sel, top, pool_pages, page_base):
    if pool_pages:
        (pt_sm, q_ref, ng_ref, bg_ref, ksel_ref, vsel_ref, kwin_ref, vwin_ref, kc_ref, vc_ref, c_ref, s1_ref, s2_ref,
         pa_ref, cache_ref, o_ref, pool_ref,
         qc_scr, qr_scr, s_scr, p_scr, a_scr, m_scr, l_scr, acc_scr, oc_scr, pbuf, psem) = refs
        step = (pl.program_id(0) * pl.num_programs(1) + pl.program_id(1)) * pl.num_programs(2) + pl.program_id(2)
        n_steps = pl.num_programs(0) * pl.num_programs(1) * pl.num_programs(2)
        slot = step % 2
        half_rows, n_cols = pbuf.shape[2], 2 * NSA_KV

        def page_copies(step_idx, p, to_slot):
            page = pt_sm[step_idx * pool_pages + p] + page_base
            return [pltpu.make_async_copy(cache_ref.at[page, pl.ds(hh * half_rows, half_rows), pl.ds(0, n_cols), :],
                                          pbuf.at[to_slot, p, :, pl.ds(hh * n_cols, n_cols), :], psem.at[to_slot])
                    for hh in range(2)]

        @pl.when(step == 0)
        def _():
            for p in range(pool_pages):
                for cp in page_copies(0, p, 0):
                    cp.start()

        @pl.when(step + 1 < n_steps)
        def _():
            for p in range(pool_pages):
                for cp in page_copies(step + 1, p, 1 - slot):
                    cp.start()
    else:
        (q_ref, ng_ref, bg_ref, ksel_ref, vsel_ref, kwin_ref, vwin_ref, kc_ref, vc_ref, c_ref, s1_ref, s2_ref,
         o_ref, qc_scr, qr_scr, s_scr, p_scr, a_scr, m_scr, l_scr, acc_scr, oc_scr) = refs
    i = pl.program_id(2)
    hq = NSA_HPG
    q = q_ref[0]
    c, s1, s2 = c_ref[...], s1_ref[...], s2_ref[...]
    for h in range(hq):
        qh = q[:, h * LANE:(h + 1) * LANE]
        qc_scr[pl.ds(h * tq, tq), :] = (qh * SCALE_LOG2).astype(BF16)
        qr_scr[pl.ds(h * tq, tq), :] = (_rope(qh, c, s1, s2) * SCALE_LOG2).astype(BF16)

    kc = kc_ref[0, 0]
    npad = kc.shape[0]
    s_scr[:, 0:npad] = _nt(qc_scr[...], kc)
    tpos = i * tq + lax.broadcasted_iota(jnp.int32, (tq, npad), 0)
    ncol = lax.broadcasted_iota(jnp.int32, (tq, npad), 1)
    cmask = (ncol + 1) * CMP_BLOCK <= tpos + 1
    imp = jnp.zeros((tq, npad), F32)
    for h in range(hq):
        r = pl.ds(h * tq, tq)
        s = jnp.where(cmask, s_scr[r, 0:npad], MASK_NEG)
        e = jnp.where(cmask, jnp.exp2(s - jnp.max(s, axis=-1, keepdims=True)), 0.0)
        p = e / jnp.maximum(jnp.sum(e, axis=-1, keepdims=True), 1e-30)
        p_scr[r, 0:npad] = p.astype(BF16)
        imp = imp + p
    oc_scr[...] = _nn(p_scr[:, 0:npad], vc_ref[0, 0])

    rows = min(npad, -(-n_sel // 8) * 8)
    imp_t = imp.T[0:rows]
    blk = lax.broadcasted_iota(jnp.int32, (rows, tq), 0)
    cur = (i * tq + lax.broadcasted_iota(jnp.int32, (rows, tq), 1)) // SEL_BLOCK
    imp_t = jnp.where((blk == 0) | (blk == cur) | (blk == cur - 1), FORCE, imp_t)
    imp_t = jnp.where(blk > cur, -1.0, imp_t)
    imp_t = jnp.where(blk >= n_sel, -2.0, imp_t)
    rank = jnp.zeros((rows, tq), F32)
    for j in range(n_sel):
        a = imp_t[j:j + 1, :]
        ahead = (a > imp_t) | ((a == imp_t) & (blk > j))
        rank = rank + jnp.where(ahead, 1.0, 0.0)
    sel_t = jnp.where((rank < top) & (blk < n_sel), 1.0, 0.0)
    if npad > rows:
        sel_t = jnp.concatenate([sel_t, jnp.zeros((npad - rows, tq), F32)], axis=0)
    sel = sel_t.T.astype(BF16)

    m_scr[...] = jnp.full(m_scr.shape, MASK_NEG, F32)
    l_scr[...] = jnp.zeros(l_scr.shape, F32)
    acc_scr[...] = jnp.zeros(acc_scr.shape, F32)
    n_chunks = (i * tq + tq + tc - 1) // tc

    def chunk(ci, carry):
        k0 = pl.multiple_of(ci * tc, tc)
        k = ksel_ref[0, pl.ds(k0, tc), :]
        v = vsel_ref[0, pl.ds(k0, tc), :]
        jb = lax.broadcasted_iota(jnp.int32, (npad, tc), 0)
        kb = (k0 + lax.broadcasted_iota(jnp.int32, (npad, tc), 1)) // SEL_BLOCK
        expand = jnp.where(jb == kb, 1.0, 0.0).astype(BF16)
        chosen = _nn(sel, expand)
        kp = k0 + lax.broadcasted_iota(jnp.int32, (tq, tc), 1)
        tp = i * tq + lax.broadcasted_iota(jnp.int32, (tq, tc), 0)
        bias = jnp.where((chosen > 0.5) & (kp <= tp), 0.0, MASK_NEG)
        s_scr[:, 0:tc] = _nt(qr_scr[...], k)
        for h in range(hq):
            r = pl.ds(h * tq, tq)
            parts = _lane_parts(s_scr[r, 0:tc] + bias)
            m_prev = m_scr[r, :]
            m_new = jnp.maximum(m_prev, jnp.max(functools.reduce(jnp.maximum, parts), axis=-1, keepdims=True))
            alpha = jnp.exp2(m_prev - m_new)
            pes = [jnp.exp2(x - m_new) for x in parts]
            l_scr[r, :] = alpha * l_scr[r, :] + functools.reduce(jnp.add, pes)
            p_scr[r, 0:tc] = jnp.concatenate(pes, axis=1).astype(BF16)
            a_scr[r, :] = alpha
            m_scr[r, :] = m_new
        acc_scr[...] = a_scr[...] * acc_scr[...] + _nn(p_scr[:, 0:tc], v)
        return carry

    lax.fori_loop(0, n_chunks, chunk, 0)

    w0 = pl.multiple_of(jnp.clip(i * tq + tq - tw, 0, t_len - tw), LANE)
    kp = w0 + lax.broadcasted_iota(jnp.int32, (tq, tw), 1)
    tp = i * tq + lax.broadcasted_iota(jnp.int32, (tq, tw), 0)
    wbias = jnp.where((kp <= tp) & (kp > tp - WINDOW), 0.0, MASK_NEG)
    s_scr[:, 0:tw] = _nt(qr_scr[...], kwin_ref[0, pl.ds(w0, tw), :])
    for h in range(hq):
        r = pl.ds(h * tq, tq)
        parts = _lane_parts(s_scr[r, 0:tw] + wbias)
        m = jnp.max(functools.reduce(jnp.maximum, parts), axis=-1, keepdims=True)
        ews = [jnp.exp2(x - m) for x in parts]
        a_scr[r, :] = jnp.broadcast_to(jnp.sum(functools.reduce(jnp.add, ews), axis=-1, keepdims=True), (tq, LANE))
        p_scr[r, 0:tw] = jnp.concatenate(ews, axis=1).astype(BF16)
    o_w = _nn(p_scr[:, 0:tw], vwin_ref[0, pl.ds(w0, tw), :])

    gate = jax.nn.sigmoid(bg_ref[0])
    for g in range(1, NSA_KV):
        gate = jnp.where(pl.program_id(1) == g, pltpu.roll(gate, LANE - g * hq * N_BRANCH, 1), gate)
    ng = ng_ref[0]
    for h in range(hq):
        r = slice(h * tq, (h + 1) * tq)
        o_s = acc_scr[r, :] / jnp.sum(l_scr[r, :], axis=-1, keepdims=True)
        o = (gate[:, 3 * h:3 * h + 1] * oc_scr[r, :] + gate[:, 3 * h + 1:3 * h + 2] * o_s
             + gate[:, 3 * h + 2:3 * h + 3] * (o_w[r] / a_scr[r, :]))
        o_ref[0, :, h * LANE:(h + 1) * LANE] = (_silu(ng[:, h * LANE:(h + 1) * LANE]) * o).astype(o_ref.dtype)

    if pool_pages:
        for p in range(pool_pages):
            for cp in page_copies(step, p, slot):
                cp.wait()
        pa = pa_ref[...]
        for p in range(pool_pages):
            pool_ref[0, p] = jnp.sum(pbuf[slot, p] * pa, axis=0)


def _nsa_prompt(z3, bg3, kvb, kc, vc, tabs, *, q_off, ng_off, tq=128, tc=512, pool=None):
    b, t, _ = z3.shape
    nq = t // tq
    tc = min(tc, t)
    tw = min(WINDOW + tq, t)
    n_sel = t // SEL_BLOCK
    top = min(TOP_N, n_sel)
    npad = kc.shape[2]
    gw = NSA_HPG * LANE
    rows = NSA_HPG * tq
    wide = max(tc, tw, npad)
    kvspec = lambda k: pl.BlockSpec((1, t, LANE), lambda bi, gi, qi, *_, k=k: (bi, 0, k + gi))
    cspec = pl.BlockSpec((1, 1, npad, LANE), lambda bi, gi, qi, *_: (bi, gi, 0, 0))
    tspec = pl.BlockSpec((tq, LANE), lambda bi, gi, qi, *_: (qi, 0))
    in_specs = [pl.BlockSpec((1, tq, gw), lambda bi, gi, qi, *_: (bi, qi, q_off // gw + gi)),
                pl.BlockSpec((1, tq, gw), lambda bi, gi, qi, *_: (bi, qi, ng_off // gw + gi)),
                pl.BlockSpec((1, tq, LANE), lambda bi, gi, qi, *_: (bi, qi, 0)),
                kvspec(0), kvspec(2), kvspec(4), kvspec(6), cspec, cspec, tspec, tspec, tspec]
    out_shape = [jax.ShapeDtypeStruct((b, t, NSA_HEADS * LANE), BF16)]
    out_specs = [pl.BlockSpec((1, tq, gw), lambda bi, gi, qi, *_: (bi, qi, gi))]
    scratch = ([pltpu.VMEM((rows, LANE), BF16)] * 2
               + [pltpu.VMEM((rows, wide), F32), pltpu.VMEM((rows, wide), BF16)]
               + [pltpu.VMEM((rows, LANE), F32)] * 5)
    args = [z3, z3, bg3, kvb, kvb, kvb, kvb, kc, vc, *tabs]
    n_steps = b * NSA_KV * nq
    pool_pages, page_base, prefetch = 0, 0, []
    if pool is not None:
        cache4, page_flat, a3, page_base = pool
        page, n_cols = cache4.shape[1], 2 * NSA_KV
        if page_flat.shape[0] % n_steps == 0 and page == 2 * CMP_BLOCK and 2 * n_cols == 8:
            pool_pages = page_flat.shape[0] // n_steps
            prefetch = [page_flat]
            in_specs += [pl.BlockSpec((CMP_BLOCK, 2 * n_cols, LANE), lambda bi, gi, qi, *_: (0, 0, 0)),
                         pl.BlockSpec(memory_space=pl.ANY)]
            args += [jnp.concatenate([a3, a3], axis=1), cache4]
            out_shape.append(jax.ShapeDtypeStruct((n_steps, pool_pages, 2 * n_cols, LANE), F32))
            out_specs.append(pl.BlockSpec((1, pool_pages, 2 * n_cols, LANE),
                                          lambda bi, gi, qi, *_: ((bi * NSA_KV + gi) * nq + qi, 0, 0, 0)))
            scratch += [pltpu.VMEM((2, pool_pages, CMP_BLOCK, 2 * n_cols, LANE), F32), pltpu.SemaphoreType.DMA((2,))]
    kern = functools.partial(_nsa_prompt_kernel, t_len=t, tq=tq, tc=tc, tw=tw, n_sel=n_sel, top=top,
                             pool_pages=pool_pages, page_base=page_base)
    res = pl.pallas_call(
        kern,
        out_shape=out_shape,
        grid_spec=pltpu.PrefetchScalarGridSpec(
            num_scalar_prefetch=len(prefetch), grid=(b, NSA_KV, nq),
            in_specs=in_specs, out_specs=out_specs, scratch_shapes=scratch),
        compiler_params=_params(("arbitrary", "arbitrary", "arbitrary")),
        name="nsa_prompt",
    )(*prefetch, *args)
    return (res[0], res[1]) if pool_pages else (res[0], None)


def _conv_kernel(h_ref, b_ref, c_ref, g_ref, prev_ref, w_ref, y_ref, st_ref, up_scr, *, t_real):
    u = c_ref[0] * h_ref[0]
    t = u.shape[0]
    up_scr[pl.ds(8 - (CONV_W - 1), CONV_W - 1), :] = prev_ref[0]
    up_scr[pl.ds(8, t), :] = u
    w = w_ref[...]
    y = w[0:1, :] * up_scr[pl.ds(6, t), :]
    y = y + w[1:2, :] * up_scr[pl.ds(7, t), :]
    y = y + w[2:3, :] * u
    y = b_ref[0] * y
    y_ref[0] = (_silu(g_ref[0]) * y).astype(y_ref.dtype)
    st_ref[0] = up_scr[pl.ds(6 + t_real, CONV_W - 1), :]


def _conv_mixer(z3, prev, w_conv, *, t_real):
    b, t, _ = z3.shape
    nc = CONV_DIM // LANE
    zspec = lambda off: pl.BlockSpec((1, t, LANE), lambda bi, ci, off=off: (bi, 0, off // LANE + ci))
    return pl.pallas_call(
        functools.partial(_conv_kernel, t_real=t_real),
        out_shape=[jax.ShapeDtypeStruct((b, t, CONV_DIM), BF16),
                   jax.ShapeDtypeStruct((b, CONV_W - 1, CONV_DIM), F32)],
        grid=(b, nc),
        in_specs=[zspec(C_H), zspec(C_B), zspec(C_C), zspec(C_G),
                  pl.BlockSpec((1, CONV_W - 1, LANE), lambda bi, ci: (bi, 0, ci)),
                  pl.BlockSpec((CONV_W, LANE), lambda bi, ci: (0, ci))],
        out_specs=[pl.BlockSpec((1, t, LANE), lambda bi, ci: (bi, 0, ci)),
                   pl.BlockSpec((1, CONV_W - 1, LANE), lambda bi, ci: (bi, 0, ci))],
        scratch_shapes=[pltpu.VMEM((t + 8, LANE), F32)],
        compiler_params=_params(("parallel", "parallel")),
        name="conv_mixer",
    )(z3, z3, z3, z3, prev, w_conv)


def _mem_attn_kernel(q_ref, mg_ref, kv_ref, o_ref, *, interleaved):
    q = q_ref[0]
    mg = mg_ref[0]
    half = MEM_HEADS * LANE
    for h in range(MEM_HEADS):
        lo, hi = h * LANE, (h + 1) * LANE
        if interleaved:
            nm = kv_ref.shape[1] // (2 * MEM_HEADS)
            k = kv_ref[0, pl.ds(h, nm, stride=2 * MEM_HEADS), :].astype(BF16)
            v = kv_ref[0, pl.ds(MEM_HEADS + h, nm, stride=2 * MEM_HEADS), :].astype(BF16)
        else:
            k = kv_ref[0, :, lo:hi].astype(BF16)
            v = kv_ref[0, :, half + lo:half + hi].astype(BF16)
        s = _nt((q[:, lo:hi] * ATTN_SCALE).astype(BF16), k)
        e = jnp.exp(s - jnp.max(s, axis=-1, keepdims=True))
        o = _nn(e.astype(BF16), v) / jnp.sum(e, axis=-1, keepdims=True)
        o_ref[0, :, lo:hi] = (_silu(mg[:, lo:hi]) * o).astype(o_ref.dtype)


def _mem_attn(zb3, mkv, *, mq_off, mg_off, tq, interleaved, kv_base=0):
    b, t, _ = zb3.shape
    wq = MEM_HEADS * LANE
    return pl.pallas_call(
        functools.partial(_mem_attn_kernel, interleaved=interleaved),
        out_shape=jax.ShapeDtypeStruct((b, t, wq), BF16),
        grid=(b, t // tq),
        in_specs=[pl.BlockSpec((1, tq, wq), lambda bi, ti: (bi, ti, mq_off // wq)),
                  pl.BlockSpec((1, tq, wq), lambda bi, ti: (bi, ti, mg_off // wq)),
                  pl.BlockSpec((1,) + mkv.shape[1:], lambda bi, ti: (bi + kv_base, 0, 0))],
        out_specs=pl.BlockSpec((1, tq, wq), lambda bi, ti: (bi, ti, 0)),
        compiler_params=_params(("parallel", "parallel")),
        name="mem_attn",
    )(zb3, zb3, mkv)


def _out_proj_kernel(x_ref, ya_ref, yb_ref, ym_ref, w_ref, fg_ref, o_ref, *, final):
    a, bw = CONV_DIM, CONV_DIM + NSA_HEADS * LANE
    acc = _nn(ya_ref[...], w_ref[0:a, :])
    acc = acc + _nn(yb_ref[...], w_ref[a:bw, :])
    acc = acc + _nn(ym_ref[...], w_ref[bw:, :])
    r = x_ref[...] + acc
    if final:
        r = r * lax.rsqrt(jnp.mean(r * r, axis=-1, keepdims=True) + NORM_EPS) * fg_ref[...]
    o_ref[...] = r


def _out_proj(x, ya, yb, ym, w, fg, *, tm, final):
    m, d = x.shape
    row = lambda width: pl.BlockSpec((tm, width), lambda i: (i, 0))
    return pl.pallas_call(
        functools.partial(_out_proj_kernel, final=final),
        out_shape=jax.ShapeDtypeStruct((m, d), F32),
        grid=(m // tm,),
        in_specs=[row(d), row(ya.shape[1]), row(yb.shape[1]), row(ym.shape[1]),
                  pl.BlockSpec(w.shape, lambda i: (0, 0)),
                  pl.BlockSpec((1, d), lambda i: (0, 0))],
        out_specs=row(d),
        compiler_params=_params(("parallel",)),
        name="out_proj",
    )(x, ya, yb, ym, w, fg.reshape(1, d))


def _sample_pre_kernel(q_ref, ng_ref, bg_ref, kv0_ref, kv1_ref, kv2_ref, c_ref, s1_ref, s2_ref, kc_ref, vc_ref,
                       kvn_ref, wn_ref, qr_ref, oc_ref, ngo_ref, gate_ref, val_ref, *, past):
    tp = SAMPLE_T_PAD
    hq = NSA_HPG
    c, s1, s2 = c_ref[...], s1_ref[...], s2_ref[...]
    kv0, kv1, kv2 = kv0_ref[0], kv1_ref[0], kv2_ref[0]
    kvn_ref[0, :, 0:512] = kv0
    kvn_ref[0, :, 768:1024] = kv1[:, 256:512]
    wn_ref[0, :, 256:512] = kv2[:, 256:512]
    for g in range(NSA_KV):
        lo, hi = g * LANE, (g + 1) * LANE
        kvn_ref[0, :, 512 + lo:512 + hi] = _rope(kv1[:, lo:hi], c, s1, s2)
        wn_ref[0, :, lo:hi] = _rope(kv2[:, lo:hi], c, s1, s2)

    q = q_ref[0]
    ng = ng_ref[0]
    gates = jax.nn.sigmoid(bg_ref[0])
    for g in range(NSA_KV):
        qc_l, qr_l = [], []
        for h in range(hq):
            lo = (g * hq + h) * LANE
            qh = q[:, lo:lo + LANE]
            qc_l.append(qh * ATTN_SCALE)
            qr_l.append(_rope(qh, c, s1, s2) * ATTN_SCALE)
            ngo_ref[0, g, h * tp:(h + 1) * tp, :] = ng[:, lo:lo + LANE]
            for br in range(N_BRANCH):
                col = (g * hq + h) * N_BRANCH + br
                gate_ref[0, g, br, h * tp:(h + 1) * tp, :] = jnp.broadcast_to(gates[:, col:col + 1], (tp, LANE))
        qc = jnp.concatenate(qc_l, axis=0)
        qr_ref[0, g] = jnp.concatenate(qr_l, axis=0)

        kc = kc_ref[0, g]
        npad = kc.shape[0]
        s = _nt(qc.astype(BF16), kc)
        trow = lax.broadcasted_iota(jnp.int32, (hq * tp, npad), 0) % tp
        ncol = lax.broadcasted_iota(jnp.int32, (hq * tp, npad), 1)
        cmask = (ncol + 1) * CMP_BLOCK <= past + trow + 1
        s = jnp.where(cmask, s, MASK_NEG)
        e = jnp.where(cmask, jnp.exp(s - jnp.max(s, axis=-1, keepdims=True)), 0.0)
        p = e / jnp.maximum(jnp.sum(e, axis=-1, keepdims=True), 1e-30)
        oc_ref[0, g] = _nn(p.astype(BF16), vc_ref[0, g])
        imp = jnp.sum(p.reshape(hq, tp, npad), axis=0)

        blk = lax.broadcasted_iota(jnp.int32, (tp, npad), 1)
        cur = (past + lax.broadcasted_iota(jnp.int32, (tp, npad), 0)) // SEL_BLOCK
        val = jnp.where((blk == 0) | (blk == cur) | (blk == cur - 1), FORCE, imp)
        val = jnp.where(blk > cur, -1.0, val)
        val = jnp.where(blk >= past // SEL_BLOCK, -2.0, val)
        val_ref[0, g] = val


def _topk_kernel(val_ref, idx_ref, *, n_top):
    val = val_ref[...]
    rows, n = val.shape
    blk = lax.broadcasted_iota(jnp.int32, (rows, n), 1)
    lane = lax.broadcasted_iota(jnp.int32, (rows, LANE), 1)
    idx = jnp.zeros((rows, LANE), jnp.int32)
    for r in range(n_top):
        best = jnp.max(val, axis=-1, keepdims=True)
        j = jnp.min(jnp.where(val == best, blk, n), axis=-1, keepdims=True)
        idx = jnp.where(lane == r, j, idx)
        val = jnp.where(blk == j, -3e38, val)
    idx_ref[...] = idx


def _topk(val2, *, n_top):
    rows, n = val2.shape
    return pl.pallas_call(
        functools.partial(_topk_kernel, n_top=n_top),
        out_shape=jax.ShapeDtypeStruct((rows, LANE), jnp.int32),
        grid=(1,),
        in_specs=[pl.BlockSpec((rows, n), lambda i: (0, 0))],
        out_specs=pl.BlockSpec((rows, LANE), lambda i: (0, 0)),
        compiler_params=_params(("arbitrary",)),
        name="sample_topk",
    )(val2)


def _sample_pre(za3, zb3, bg3, tabs, kc, vc, *, past):
    bs, tp, _ = za3.shape
    npad = kc.shape[2]
    qw = NSA_HEADS * LANE
    kvblk = KV_OFF // 512
    zspec = lambda k: pl.BlockSpec((1, tp, 512), lambda bi, k=k: (bi, 0, kvblk + k))
    tspec = pl.BlockSpec((tp, LANE), lambda bi: (0, 0))
    cspec = pl.BlockSpec((1, NSA_KV, npad, LANE), lambda bi: (bi, 0, 0, 0))
    rows = NSA_HPG * tp
    gspec = pl.BlockSpec((1, NSA_KV, rows, LANE), lambda bi: (bi, 0, 0, 0))
    gshape = jax.ShapeDtypeStruct((bs, NSA_KV, rows, LANE), F32)
    return pl.pallas_call(
        functools.partial(_sample_pre_kernel, past=past),
        out_shape=[jax.ShapeDtypeStruct((bs, tp, 1024), F32),
                   jax.ShapeDtypeStruct((bs, tp, 512), F32),
                   gshape, gshape, gshape,
                   jax.ShapeDtypeStruct((bs, NSA_KV, N_BRANCH, rows, LANE), F32),
                   jax.ShapeDtypeStruct((bs, NSA_KV, tp, npad), F32)],
        grid=(bs,),
        in_specs=[pl.BlockSpec((1, tp, qw), lambda bi: (bi, 0, Q_OFF // qw)),
                  pl.BlockSpec((1, tp, qw), lambda bi: (bi, 0, NG_OFF // qw)),
                  pl.BlockSpec((1, tp, LANE), lambda bi: (bi, 0, 0)),
                  zspec(0), zspec(1), zspec(2), tspec, tspec, tspec, cspec, cspec],
        out_specs=[pl.BlockSpec((1, tp, 1024), lambda bi: (bi, 0, 0)),
                   pl.BlockSpec((1, tp, 512), lambda bi: (bi, 0, 0)),
                   gspec, gspec, gspec,
                   pl.BlockSpec((1, NSA_KV, N_BRANCH, rows, LANE), lambda bi: (bi, 0, 0, 0, 0)),
                   pl.BlockSpec((1, NSA_KV, tp, npad), lambda bi: (bi, 0, 0, 0))],
        compiler_params=_params(("parallel",)),
        name="sample_pre",
    )(za3, za3, bg3, zb3, zb3, zb3, *tabs, kc, vc)


def _sample_attn_kernel(idx_sm, pt_sm, qr_ref, oc_ref, ng_ref, gate_ref, ksn_ref, vsn_ref,
                        wc_ref, kwn_ref, vwn_ref, cache_ref, o_ref,
                        kbuf, vbuf, kw_scr, vw_scr, sem, *, ts, n_top, n_pages, page_base, per_page, wb):
    tp = SAMPLE_T_PAD
    hq = NSA_HPG
    b = pl.program_id(0)
    g = pl.program_id(1)
    n_gath = n_top * SEL_BLOCK
    ks_rows = kbuf.shape[1]

    def gather_copies(t, r):
        blk = idx_sm[((b * NSA_KV + g) * ts + t) * n_top + r]
        page = pt_sm[b * n_pages + blk // per_page] + page_base
        row0 = (blk % per_page) * SEL_BLOCK
        src_k = cache_ref.at[page, pl.ds(row0, SEL_BLOCK), 2 * NSA_KV + g]
        src_v = cache_ref.at[page, pl.ds(row0, SEL_BLOCK), 3 * NSA_KV + g]
        dst = pl.ds(r * SEL_BLOCK, SEL_BLOCK)
        return (pltpu.make_async_copy(src_k, kbuf.at[t, dst], sem.at[0]),
                pltpu.make_async_copy(src_v, vbuf.at[t, dst], sem.at[1]))

    for t in range(ts):
        for r in range(n_top):
            ck, cv = gather_copies(t, r)
            ck.start()
            cv.start()

    qr = qr_ref[0, 0].astype(BF16)
    trow = lax.broadcasted_iota(jnp.int32, (hq * tp, 1), 0) % tp

    ww = kw_scr.shape[0]
    kw_scr[pl.ds(0, wb), :] = wc_ref[0, pl.ds(g, wb, stride=2 * NSA_KV), :]
    vw_scr[pl.ds(0, wb), :] = wc_ref[0, pl.ds(NSA_KV + g, wb, stride=2 * NSA_KV), :]
    kw_scr[pl.ds(wb, tp), :] = kwn_ref[0]
    vw_scr[pl.ds(wb, tp), :] = vwn_ref[0]
    kw_scr[pl.ds(wb + tp, ww - wb - tp), :] = jnp.zeros((ww - wb - tp, LANE), F32)
    vw_scr[pl.ds(wb + tp, ww - wb - tp), :] = jnp.zeros((ww - wb - tp, LANE), F32)
    sw = _nt(qr, kw_scr[...].astype(BF16))
    jw = lax.broadcasted_iota(jnp.int32, (hq * tp, ww), 1)
    rel = jw - wb
    okw = (rel <= trow) & (rel > trow - WINDOW) & (jw < wb + ts)
    sw = jnp.where(okw, sw, MASK_NEG)
    ew = jnp.where(okw, jnp.exp(sw - jnp.max(sw, axis=-1, keepdims=True)), 0.0)
    o_w = _nn(ew.astype(BF16), vw_scr[...].astype(BF16)) / jnp.sum(ew, axis=-1, keepdims=True)

    for t in range(ts):
        for r in range(n_top):
            ck, cv = gather_copies(t, r)
            ck.wait()
            cv.wait()

    js = lax.broadcasted_iota(jnp.int32, (hq * tp, ks_rows), 1)
    o_s = jnp.zeros((hq * tp, LANE), F32)
    for t in range(ts):
        kbuf[t, pl.ds(n_gath, tp), :] = ksn_ref[0]
        vbuf[t, pl.ds(n_gath, tp), :] = vsn_ref[0]
        kbuf[t, pl.ds(n_gath + tp, ks_rows - n_gath - tp), :] = jnp.zeros((ks_rows - n_gath - tp, LANE), F32)
        vbuf[t, pl.ds(n_gath + tp, ks_rows - n_gath - tp), :] = jnp.zeros((ks_rows - n_gath - tp, LANE), F32)
        ss = _nt(qr, kbuf[t].astype(BF16))
        oks = (js < n_gath) | ((js - n_gath <= t) & (js < n_gath + ts))
        ss = jnp.where(oks, ss, MASK_NEG)
        es = jnp.where(oks, jnp.exp(ss - jnp.max(ss, axis=-1, keepdims=True)), 0.0)
        ot = _nn(es.astype(BF16), vbuf[t].astype(BF16)) / jnp.sum(es, axis=-1, keepdims=True)
        o_s = jnp.where(trow == t, ot, o_s)

    o = gate_ref[0, 0, 0] * oc_ref[0, 0] + gate_ref[0, 0, 1] * o_s + gate_ref[0, 0, 2] * o_w
    y = _silu(ng_ref[0, 0]) * o
    for h in range(hq):
        o_ref[0, :, h * LANE:(h + 1) * LANE] = y[h * tp:(h + 1) * tp].astype(o_ref.dtype)


def _sample_attn(idx_flat, page_flat, qr, oc, ngo, gates, kvn, cache_win_rows, wn, cache4, *,
                 ts, n_top, n_pages, page_base, win_base):
    bs = qr.shape[0]
    tp = SAMPLE_T_PAD
    rows = NSA_HPG * tp
    wb = cache_win_rows.shape[1] // (2 * NSA_KV)
    per_page = cache4.shape[1] // SEL_BLOCK
    ks_rows = -(-(n_top * SEL_BLOCK + tp) // LANE) * LANE
    ww = -(-(wb + tp) // LANE) * LANE
    gspec = pl.BlockSpec((1, 1, rows, LANE), lambda bi, gi, *_: (bi, gi, 0, 0))
    newspec = lambda k: pl.BlockSpec((1, tp, LANE), lambda bi, gi, *_, k=k: (bi, 0, k + gi))
    grid_spec = pltpu.PrefetchScalarGridSpec(
        num_scalar_prefetch=2, grid=(bs, NSA_KV),
        in_specs=[gspec, gspec, gspec,
                  pl.BlockSpec((1, 1, N_BRANCH, rows, LANE), lambda bi, gi, *_: (bi, gi, 0, 0, 0)),
                  newspec(2 * NSA_KV), newspec(3 * NSA_KV),
                  pl.BlockSpec((1,) + cache_win_rows.shape[1:], lambda bi, gi, *_: (bi + win_base, 0, 0)),
                  newspec(0), newspec(NSA_KV),
                  pl.BlockSpec(memory_space=pl.ANY)],
        out_specs=pl.BlockSpec((1, tp, NSA_HPG * LANE), lambda bi, gi, *_: (bi, 0, gi)),
        scratch_shapes=[pltpu.VMEM((ts, ks_rows, LANE), F32), pltpu.VMEM((ts, ks_rows, LANE), F32),
                        pltpu.VMEM((ww, LANE), F32), pltpu.VMEM((ww, LANE), F32),
                        pltpu.SemaphoreType.DMA((2,))])
    kern = functools.partial(_sample_attn_kernel, ts=ts, n_top=n_top, n_pages=n_pages,
                             page_base=page_base, per_page=per_page, wb=wb)
    return pl.pallas_call(
        kern,
        out_shape=jax.ShapeDtypeStruct((bs, tp, NSA_HEADS * LANE), BF16),
        grid_spec=grid_spec,
        compiler_params=_params(("arbitrary", "arbitrary")),
        name="sample_attn",
    )(idx_flat, page_flat, qr, oc, ngo, gates, kvn, kvn, cache_win_rows, wn, wn, cache4)


def _rope_tables(pos, rows):
    freqs = jnp.power(ROPE_THETA, -jnp.arange(ROPE_HALF, dtype=F32) * (2.0 / ROPE_DIM))
    ang = pos.astype(F32)[:, None] * freqs[None, :]
    cos, sin = jnp.cos(ang), jnp.sin(ang)
    n = pos.shape[0]
    z16 = jnp.zeros((n, ROPE_HALF), F32)
    rest = LANE - ROPE_DIM
    c = jnp.concatenate([cos, cos, jnp.ones((n, rest), F32)], axis=1)
    s1 = jnp.concatenate([z16, sin, jnp.zeros((n, rest), F32)], axis=1)
    s2 = jnp.concatenate([-sin, z16, jnp.zeros((n, rest), F32)], axis=1)
    pad = lambda a: jnp.pad(a, ((0, rows - n), (0, 0)))
    return pad(c), pad(s1), pad(s2)


def _layer_weights(norm_g, w_in3, layer, w_conv, a_cmp, w_cmp, w_out):
    w_a, w_b, wbg = _pack_in_proj_weights(jnp.swapaxes(w_in3, 1, 2), layer)
    a4 = jnp.concatenate([a_cmp[0], a_cmp[0], a_cmp[1], a_cmp[1]], axis=1)
    a3 = jnp.stack([a_cmp[0], a_cmp[0], a_cmp[1], a_cmp[1]], axis=1)
    w4 = jnp.stack([w_cmp[0], w_cmp[0], w_cmp[1], w_cmp[1]]).astype(BF16)
    return dict(norm_g=norm_g, w_a=w_a, w_b=w_b, wbg=wbg, w_conv=w_conv, a4=a4, a3=a3, w4=w4,
                w_out=w_out.astype(BF16))


def _in_proj(x2, lw, *, tm):
    za, bg = _norm_matmul(x2, lw["norm_g"], lw["w_a"], lw["wbg"], tm=tm, tn=512)
    zb = _norm_matmul(x2, lw["norm_g"], lw["w_b"], tm=tm, tn=512)
    return za, zb, bg


def _prompt_layer(xp, mem_prompt, mem_norm_g, w_mem, lw, final_g, final, pool):
    b, t, d = xp.shape
    m = b * t
    x2 = xp.reshape(m, d)
    tabs = _rope_tables(jnp.arange(t, dtype=jnp.int32), t)
    wb = min(WINDOW, t)
    zqn, bg, ya, conv_new = _proj_conv(x2, lw["norm_g"], lw["w_a"], lw["wbg"], lw["w_conv"], tm=min(512, t), seq_len=t)
    kvn, kvb, win_rows, pooled_prompt, zm = _proj_kv(x2, lw["norm_g"], lw["w_b"], tabs, lw["a4"], tm=wb, seq_len=t)
    nm = mem_prompt.shape[1]
    mkv = _norm_matmul(mem_prompt.reshape(b * nm, d), mem_norm_g, w_mem, tm=min(512, b * nm), tn=512)
    mkv3 = mkv.reshape(b, nm, 2 * MEM_HEADS * LANE)
    kc, vc = _cmp_proj(pooled_prompt.reshape(b, t // CMP_BLOCK, -1), lw["w4"])
    yb, pooled = _nsa_prompt(zqn.reshape(b, t, -1), bg.reshape(b, t, LANE), kvb.reshape(b, t, -1), kc, vc, tabs,
                             q_off=0, ng_off=NG_OFF - Q_OFF, tq=min(256, t), pool=pool)
    ym = _mem_attn(zm.reshape(b, t, -1), mkv3, mq_off=0, mg_off=MG_OFF - MQ_OFF, tq=min(512, t), interleaved=False)
    out = _out_proj(x2, ya, yb.reshape(m, -1), ym.reshape(m, -1),
                    lw["w_out"], final_g, tm=min(512, m), final=final)
    kv_new = kvn.reshape(b, t, 4, NSA_KV, HEAD_DIM)
    win_new = win_rows.reshape(b, wb, 2, NSA_KV, HEAD_DIM)
    mem_kv = mkv.reshape(b, nm, 2, MEM_HEADS, HEAD_DIM)
    return out.reshape(b, t, d), kv_new, win_new, conv_new, mem_kv, pooled


def _sample_layer(xs_p, ts, layer, cache4, page_flat, pooled, cache_win, state_conv, cache_mem, lw, final_g, final):
    bs, tp, d = xs_p.shape
    depth = cache_win.shape[0]
    pool, page = cache4.shape[0] // depth, cache4.shape[1]
    n_pages = page_flat.shape[0] // bs
    past = n_pages * page
    assert past % SEL_BLOCK == 0 and ts <= SEL_BLOCK and ts <= tp
    n_past = past // SEL_BLOCK
    n_top = min(TOP_N, n_past + 1) - 1
    m = bs * tp
    za, zb, bg = _in_proj(xs_p.reshape(m, d), lw, tm=m)
    za3 = za.reshape(bs, tp, ZA_WIDTH)
    zb3 = zb.reshape(bs, tp, ZB_WIDTH)
    bg3 = bg.reshape(bs, tp, LANE)
    tabs = _rope_tables(past + jnp.arange(tp, dtype=jnp.int32), tp)
    if pooled is None:
        pooled = _pool_pages(cache4, page_flat, lw["a3"], bs=bs, n_pages=n_pages, page_base=layer * pool)
    else:
        pooled = pooled.reshape(bs, n_pages * (page // CMP_BLOCK), 2 * NSA_KV * LANE)
    kc, vc = _cmp_proj(pooled, lw["w4"])
    kvn, wn, qr, oc, ngo, gates, val = _sample_pre(za3, zb3, bg3, tabs, kc, vc, past=past)
    idx = _topk(val.reshape(bs * NSA_KV * tp, val.shape[-1]), n_top=n_top)
    idx_flat = idx.reshape(bs, NSA_KV, tp, LANE)[:, :, :ts, :n_top].reshape(-1)
    wbuf = cache_win.shape[2]
    cache_win_rows = cache_win.reshape(depth * bs, wbuf * 2 * NSA_KV, HEAD_DIM)
    yb = _sample_attn(idx_flat, page_flat, qr, oc, ngo, gates, kvn, cache_win_rows, wn, cache4,
                      ts=ts, n_top=n_top, n_pages=n_pages, page_base=layer * pool, win_base=layer * bs)
    ya, conv_new = _conv_mixer(za3, state_conv[layer], lw["w_conv"], t_real=ts)
    nm = cache_mem.shape[2]
    mem_rows = cache_mem.reshape(depth * bs, nm * 2 * MEM_HEADS, HEAD_DIM)
    ym = _mem_attn(zb3, mem_rows, mq_off=MQ_OFF, mg_off=MG_OFF, tq=tp, interleaved=True, kv_base=layer * bs)
    out = _out_proj(xs_p.reshape(m, d), ya.reshape(m, -1), yb.reshape(m, -1), ym.reshape(m, -1),
                    lw["w_out"], final_g, tm=m, final=final)
    kv_new = kvn[:, :ts].reshape(bs, ts, 4, NSA_KV, HEAD_DIM)
    win_rows = wn[:, :ts].reshape(bs, ts, 2, NSA_KV, HEAD_DIM)
    win_state = jnp.concatenate([cache_win[layer], win_rows], axis=1)[:, ts:]
    return out.reshape(bs, tp, d), kv_new, win_state, conv_new


def kernel(x_prompt, x_sample, cache_kv, cache_win, state_conv, cache_mem, page_table, mem_prompt,
           norm_g, w_in, w_conv, a_cmp, w_cmp, mem_norm_g, w_mem_kv, w_out, final_g):
    depth = w_in.shape[0]
    ts = x_sample.shape[1]
    xp = x_prompt
    xs = jnp.pad(x_sample, ((0, 0), (0, SAMPLE_T_PAD - ts), (0, 0)))
    pool_size, page = cache_kv.shape[1], cache_kv.shape[2]
    cache4 = cache_kv.reshape(depth * pool_size, page, 4 * NSA_KV, HEAD_DIM)
    page_flat = page_table.reshape(-1).astype(jnp.int32)
    kv_p, win_p, conv_p, mem_p, kv_s, win_s, conv_s = [], [], [], [], [], [], []
    for l in range(depth):
        lw = _layer_weights(norm_g[l], w_in, l, w_conv[l], a_cmp[l], w_cmp[l], w_out[l])
        final = l == depth - 1
        xp, kvn, winn, convn, mkv, pooled = _prompt_layer(
            xp, mem_prompt, mem_norm_g[l], w_mem_kv[l].T.astype(BF16), lw, final_g, final,
            pool=(cache4, page_flat, lw["a3"], l * pool_size))
        kv_p.append(kvn)
        win_p.append(winn)
        conv_p.append(convn)
        mem_p.append(mkv)
        xs, kvn, winn, convn = _sample_layer(xs, ts, l, cache4, page_flat, pooled, cache_win, state_conv,
                                             cache_mem, lw, final_g, final)
        kv_s.append(kvn)
        win_s.append(winn)
        conv_s.append(convn)
    return (xp, xs[:, :ts], jnp.stack(kv_p), jnp.stack(win_p), jnp.stack(conv_p), jnp.stack(mem_p),
            jnp.stack(kv_s), jnp.stack(win_s), jnp.stack(conv_s))
```

```python
import functools

import jax
import jax.numpy as jnp
from jax import lax
from jax.experimental import pallas as pl
from jax.experimental.pallas import tpu as pltpu

F32 = jnp.float32
BF16 = jnp.bfloat16

HEAD_DIM = 128
CONV_DIM = 512
CONV_W = 3
NSA_HEADS = 8
NSA_KV = 2
NSA_HPG = NSA_HEADS // NSA_KV
MEM_HEADS = 4
N_BRANCH = 3
ROPE_DIM = HEAD_DIM // 4
ROPE_HALF = ROPE_DIM // 2
ROPE_THETA = 500000.0
CMP_BLOCK = 64
SEL_BLOCK = 64
TOP_N = 16
WINDOW = 512
NORM_EPS = 1e-6
MASK_NEG = -1e30
FORCE = 1e9
ATTN_SCALE = HEAD_DIM ** -0.5
SCALE_LOG2 = ATTN_SCALE * 1.4426950408889634

C_H, C_B, C_C, C_G, Q_OFF, NG_OFF = 0, 512, 1024, 1536, 2048, 3072
ZA_WIDTH = 4096
KV_OFF, MQ_OFF, MG_OFF = 0, 1536, 2048
ZB_WIDTH = 2560
BG_SRC = 4096
BG_N = NSA_HEADS * N_BRANCH
LANE = 128
SAMPLE_T_PAD = 8
VMEM_LIMIT = 56 * 1024 * 1024


def _nt(a, b):
    return lax.dot_general(a, b, (((1,), (1,)), ((), ())), preferred_element_type=F32)


def _nn(a, b):
    return jnp.dot(a, b, preferred_element_type=F32)


def _params(sem, vmem=VMEM_LIMIT):
    return pltpu.CompilerParams(dimension_semantics=sem, vmem_limit_bytes=vmem)


def _rope(x, c, s1, s2):
    return x * c + pltpu.roll(x, ROPE_HALF, 1) * s1 + pltpu.roll(x, LANE - ROPE_HALF, 1) * s2


def _silu(x):
    return x * jax.nn.sigmoid(x)


def _norm_matmul_kernel(x_ref, g_ref, w_ref, *rest, with_gate, tn):
    if with_gate:
        wbg_ref, z_ref, bg_ref, h_scr = rest
    else:
        z_ref, h_scr = rest
    x = x_ref[...]
    y = x * lax.rsqrt(jnp.mean(x * x, axis=-1, keepdims=True) + NORM_EPS) * g_ref[...]
    h_scr[...] = y.astype(BF16)
    if with_gate:
        bg_ref[...] = _nt(h_scr[...], wbg_ref[...])
    for j in range(w_ref.shape[0] // tn):
        z_ref[:, j * tn:(j + 1) * tn] = _nt(h_scr[...], w_ref[j * tn:(j + 1) * tn, :])


def _norm_matmul(x, g, w, wbg=None, *, tm, tn):
    m, d = x.shape
    n = w.shape[0]
    assert n % tn == 0 and m % tm == 0 and w.shape[1] == d
    with_gate = wbg is not None
    resident = lambda shape: pl.BlockSpec(shape, lambda i: (0, 0), pipeline_mode=pl.Buffered(1))
    in_specs = [pl.BlockSpec((tm, d), lambda i: (i, 0)), resident((1, d)), resident((n, d))]
    out_shape = [jax.ShapeDtypeStruct((m, n), F32)]
    out_specs = [pl.BlockSpec((tm, n), lambda i: (i, 0))]
    args = [x, g.reshape(1, d), w]
    if with_gate:
        nb = wbg.shape[0]
        in_specs.append(resident((nb, d)))
        out_shape.append(jax.ShapeDtypeStruct((m, nb), F32))
        out_specs.append(pl.BlockSpec((tm, nb), lambda i: (i, 0)))
        args.append(wbg)
    res = pl.pallas_call(
        functools.partial(_norm_matmul_kernel, with_gate=with_gate, tn=tn),
        out_shape=out_shape, grid=(m // tm,), in_specs=in_specs, out_specs=out_specs,
        scratch_shapes=[pltpu.VMEM((tm, d), BF16)],
        compiler_params=_params(("parallel",)),
        name="norm_matmul_gate" if with_gate else "norm_matmul",
    )(*args)
    return res if with_gate else res[0]


def _rms_rows(x_ref, g_ref):
    x = x_ref[...]
    return (x * lax.rsqrt(jnp.mean(x * x, axis=-1, keepdims=True) + NORM_EPS) * g_ref[...]).astype(BF16)


def _proj_conv_kernel(x_ref, g_ref, w_ref, wbg_ref, wc_ref, c_ref, s1_ref, s2_ref,
                      qc_ref, qr_ref, ng_ref, bg_ref, ya_ref, st_ref, h_scr, up_scr, *, blocks_per_seq):
    tm = x_ref.shape[0]
    cw = CONV_DIM
    i = pl.program_id(0)
    first = i % blocks_per_seq == 0
    h_scr[...] = _rms_rows(x_ref, g_ref)
    bg_ref[...] = _nt(h_scr[...], wbg_ref[...])
    chunk = lambda off: _nt(h_scr[...], w_ref[off:off + cw, :])

    @pl.when(first)
    def _():
        up_scr[pl.ds(8 - (CONV_W - 1), CONV_W - 1), :] = jnp.zeros((CONV_W - 1, cw), F32)

    @pl.when(jnp.logical_not(first))
    def _():
        up_scr[pl.ds(8 - (CONV_W - 1), CONV_W - 1), :] = up_scr[pl.ds(8 + tm - (CONV_W - 1), CONV_W - 1), :]

    u = chunk(C_C) * chunk(C_H)
    up_scr[pl.ds(8, tm), :] = u
    wc = wc_ref[...]
    y = wc[0:1, :] * up_scr[pl.ds(6, tm), :]
    y = y + wc[1:2, :] * up_scr[pl.ds(7, tm), :]
    y = y + wc[2:3, :] * u
    y = chunk(C_B) * y
    ya_ref[...] = (_silu(chunk(C_G)) * y).astype(ya_ref.dtype)

    @pl.when(i % blocks_per_seq == blocks_per_seq - 1)
    def _():
        st_ref[0] = up_scr[pl.ds(8 + tm - (CONV_W - 1), CONV_W - 1), :]

    c, s1, s2 = c_ref[...], s1_ref[...], s2_ref[...]
    for j in range((NG_OFF - Q_OFF) // cw):
        qv = chunk(Q_OFF + j * cw)
        for k in range(cw // LANE):
            lo = j * cw + k * LANE
            qh = qv[:, k * LANE:(k + 1) * LANE]
            qc_ref[:, lo:lo + LANE] = (qh * SCALE_LOG2).astype(BF16)
            qr_ref[:, lo:lo + LANE] = (_rope(qh, c, s1, s2) * SCALE_LOG2).astype(BF16)
    for j in range((ZA_WIDTH - NG_OFF) // cw):
        ng_ref[:, j * cw:(j + 1) * cw] = chunk(NG_OFF + j * cw)


def _proj_conv(x2, g, w_a, wbg, w_conv, tabs, *, tm, seq_len):
    m, d = x2.shape
    assert seq_len % tm == 0 and CONV_W == 3
    blocks_per_seq = seq_len // tm
    nq, nng = NG_OFF - Q_OFF, ZA_WIDTH - NG_OFF
    resident = lambda shape: pl.BlockSpec(shape, lambda i: (0,) * len(shape), pipeline_mode=pl.Buffered(1))
    row = lambda width: pl.BlockSpec((tm, width), lambda i: (i, 0))
    tspec = pl.BlockSpec((tm, LANE), lambda i: (i % blocks_per_seq, 0))
    return pl.pallas_call(
        functools.partial(_proj_conv_kernel, blocks_per_seq=blocks_per_seq),
        out_shape=[jax.ShapeDtypeStruct((m, nq), BF16), jax.ShapeDtypeStruct((m, nq), BF16),
                   jax.ShapeDtypeStruct((m, nng), F32), jax.ShapeDtypeStruct((m, LANE), F32),
                   jax.ShapeDtypeStruct((m, CONV_DIM), BF16),
                   jax.ShapeDtypeStruct((m // seq_len, CONV_W - 1, CONV_DIM), F32)],
        grid=(m // tm,),
        in_specs=[row(d), resident((1, d)), resident(w_a.shape), resident(wbg.shape), resident(w_conv.shape),
                  tspec, tspec, tspec],
        out_specs=[row(nq), row(nq), row(nng), row(LANE), row(CONV_DIM),
                   pl.BlockSpec((1, CONV_W - 1, CONV_DIM), lambda i: (i // blocks_per_seq, 0, 0))],
        scratch_shapes=[pltpu.VMEM((tm, d), BF16), pltpu.VMEM((tm + 8, CONV_DIM), F32)],
        compiler_params=_params(("arbitrary",)),
        name="proj_conv",
    )(x2, g.reshape(1, d), w_a, wbg, w_conv, *tabs)


def _proj_kv_kernel(x_ref, g_ref, w_ref, c_ref, s1_ref, s2_ref, a_ref, kvn_ref, kvb_ref, win_ref, pool_ref, zm_ref,
                    h_scr, *, blocks_per_seq):
    tm = x_ref.shape[0]
    cw = 2 * NSA_KV * LANE
    h_scr[...] = _rms_rows(x_ref, g_ref)
    chunk = lambda off: _nt(h_scr[...], w_ref[off:off + cw, :])
    c, s1, s2 = c_ref[...], s1_ref[...], s2_ref[...]
    kv0 = chunk(KV_OFF)
    kv1 = chunk(KV_OFF + cw)
    kv2 = chunk(KV_OFF + 2 * cw)
    n_kv, n_w, half = 4 * NSA_KV, 2 * NSA_KV, NSA_KV * LANE
    last = pl.program_id(0) % blocks_per_seq == blocks_per_seq - 1
    for g in range(NSA_KV):
        lo, hi = g * LANE, (g + 1) * LANE
        ks = _rope(kv1[:, lo:hi], c, s1, s2)
        kw = _rope(kv2[:, lo:hi], c, s1, s2)
        vs = kv1[:, half + lo:half + hi]
        vw = kv2[:, half + lo:half + hi]
        kvn_ref[pl.ds(g, tm, stride=n_kv), :] = kv0[:, lo:hi]
        kvn_ref[pl.ds(NSA_KV + g, tm, stride=n_kv), :] = kv0[:, half + lo:half + hi]
        kvn_ref[pl.ds(2 * NSA_KV + g, tm, stride=n_kv), :] = ks
        kvn_ref[pl.ds(3 * NSA_KV + g, tm, stride=n_kv), :] = vs
        kvb_ref[:, lo:hi] = ks.astype(BF16)
        kvb_ref[:, half + lo:half + hi] = vs.astype(BF16)
        kvb_ref[:, 2 * half + lo:2 * half + hi] = kw.astype(BF16)
        kvb_ref[:, 3 * half + lo:3 * half + hi] = vw.astype(BF16)

        @pl.when(last)
        def _():
            win_ref[pl.ds(g, tm, stride=n_w), :] = kw
            win_ref[pl.ds(NSA_KV + g, tm, stride=n_w), :] = vw

    pool_ref[...] = jnp.sum(kv0.reshape(tm // CMP_BLOCK, CMP_BLOCK, cw) * a_ref[...][None], axis=1)
    for j in range(zm_ref.shape[1] // cw):
        zm_ref[:, j * cw:(j + 1) * cw] = chunk(MQ_OFF + j * cw)


def _proj_kv(x2, g, w_b, tabs, a4, *, tm, seq_len):
    m, d = x2.shape
    assert seq_len % tm == 0
    blocks_per_seq = seq_len // tm
    n_kv, n_w = 4 * NSA_KV, 2 * NSA_KV
    nm = ZB_WIDTH - MQ_OFF
    resident = lambda shape: pl.BlockSpec(shape, lambda i: (0,) * len(shape), pipeline_mode=pl.Buffered(1))
    row = lambda width: pl.BlockSpec((tm, width), lambda i: (i, 0))
    tspec = pl.BlockSpec((tm, LANE), lambda i: (i % blocks_per_seq, 0))
    return pl.pallas_call(
        functools.partial(_proj_kv_kernel, blocks_per_seq=blocks_per_seq),
        out_shape=[jax.ShapeDtypeStruct((m * n_kv, LANE), F32),
                   jax.ShapeDtypeStruct((m, n_kv * LANE), BF16),
                   jax.ShapeDtypeStruct((m // seq_len * tm * n_w, LANE), F32),
                   jax.ShapeDtypeStruct((m // CMP_BLOCK, 2 * NSA_KV * LANE), F32),
                   jax.ShapeDtypeStruct((m, nm), F32)],
        grid=(m // tm,),
        in_specs=[row(d), resident((1, d)), resident(w_b.shape), tspec, tspec, tspec, resident(a4.shape)],
        out_specs=[pl.BlockSpec((tm * n_kv, LANE), lambda i: (i, 0)), row(n_kv * LANE),
                   pl.BlockSpec((tm * n_w, LANE), lambda i: (i // blocks_per_seq, 0)),
                   pl.BlockSpec((tm // CMP_BLOCK, 2 * NSA_KV * LANE), lambda i: (i, 0)), row(nm)],
        scratch_shapes=[pltpu.VMEM((tm, d), BF16)],
        compiler_params=_params(("arbitrary",)),
        name="proj_kv",
    )(x2, g.reshape(1, d), w_b, *tabs, a4)


def _cast_rows_kernel(w_ref, o_ref):
    o_ref[...] = w_ref[0].astype(BF16)


def _pack_gate_kernel(w_ref, o_ref):
    o_ref[...] = jnp.zeros(o_ref.shape, BF16)
    o_ref[0:w_ref.shape[1], :] = w_ref[0].astype(BF16)


def _pack_in_proj_weights(w_t3, layer):
    _, n_in, d = w_t3.shape
    tn = 512
    assert n_in == BG_SRC + BG_N + ZB_WIDTH and BG_SRC % tn == 0 and ZB_WIDTH % tn == 0 and BG_N % 8 == 0

    def cast_rows(first_row, n_rows, name):
        return pl.pallas_call(
            _cast_rows_kernel,
            out_shape=jax.ShapeDtypeStruct((n_rows, d), BF16),
            grid=(n_rows // tn,),
            in_specs=[pl.BlockSpec((pl.Element(1), pl.Element(tn), pl.Element(d)),
                                   lambda j: (layer, pl.multiple_of(first_row + j * tn, 8), 0))],
            out_specs=pl.BlockSpec((tn, d), lambda j: (j, 0)),
            compiler_params=_params(("parallel",)),
            name=name,
        )(w_t3)

    wa = cast_rows(0, ZA_WIDTH, "pack_w_a")
    wb = cast_rows(BG_SRC + BG_N, ZB_WIDTH, "pack_w_b")
    wbg = pl.pallas_call(
        _pack_gate_kernel,
        out_shape=jax.ShapeDtypeStruct((LANE, d), BF16),
        grid=(1,),
        in_specs=[pl.BlockSpec((pl.Element(1), pl.Element(BG_N), pl.Element(d)), lambda j: (layer, BG_SRC, 0))],
        out_specs=pl.BlockSpec((LANE, d), lambda j: (0, 0)),
        compiler_params=_params(("arbitrary",)),
        name="pack_w_gate",
    )(w_t3)
    return wa, wb, wbg


def _pool_pages_kernel(pt_sm, a_ref, cache_ref, o_ref, buf, sem, *, pages_per_step, page_base):
    step = pl.program_id(0)
    n_steps = pl.num_programs(0)
    slot = step % 2
    n_cols = buf.shape[3]

    def page_copy(step_idx, p, to_slot):
        page = pt_sm[step_idx * pages_per_step + p] + page_base
        return pltpu.make_async_copy(cache_ref.at[page, :, pl.ds(0, n_cols), :], buf.at[to_slot, p], sem.at[to_slot])

    @pl.when(step == 0)
    def _():
        for p in range(pages_per_step):
            page_copy(0, p, 0).start()

    @pl.when(step + 1 < n_steps)
    def _():
        for p in range(pages_per_step):
            page_copy(step + 1, p, 1 - slot).start()

    for p in range(pages_per_step):
        page_copy(step, p, slot).wait()

    a = a_ref[...]
    per = buf.shape[2] // CMP_BLOCK
    for p in range(pages_per_step):
        for k in range(per):
            x = buf[slot, p, pl.ds(k * CMP_BLOCK, CMP_BLOCK)]
            o_ref[0, p, k] = jnp.sum(x * a, axis=0)


def _pool_pages(cache4, page_flat, a3, *, bs, n_pages, page_base, pages_per_step=16):
    page = cache4.shape[1]
    per = page // CMP_BLOCK
    n_cols = 2 * NSA_KV
    total = bs * n_pages
    pages_per_step = min(pages_per_step, total)
    assert total % pages_per_step == 0
    n_steps = total // pages_per_step
    grid_spec = pltpu.PrefetchScalarGridSpec(
        num_scalar_prefetch=1, grid=(n_steps,),
        in_specs=[pl.BlockSpec((CMP_BLOCK, n_cols, LANE), lambda si, pt: (0, 0, 0)),
                  pl.BlockSpec(memory_space=pl.ANY)],
        out_specs=pl.BlockSpec((1, pages_per_step, per, n_cols, LANE), lambda si, pt: (si, 0, 0, 0, 0)),
        scratch_shapes=[pltpu.VMEM((2, pages_per_step, page, n_cols, LANE), F32),
                        pltpu.SemaphoreType.DMA((2,))])
    out = pl.pallas_call(
        functools.partial(_pool_pages_kernel, pages_per_step=pages_per_step, page_base=page_base),
        out_shape=jax.ShapeDtypeStruct((n_steps, pages_per_step, per, n_cols, LANE), F32),
        grid_spec=grid_spec,
        compiler_params=_params(("arbitrary",)),
        name="pool_pages",
    )(page_flat, a3, cache4)
    return out.reshape(bs, n_pages * per, n_cols * LANE)


def _cmp_proj_kernel(p_ref, w_ref, kc_ref, vc_ref):
    pooled = p_ref[0]
    n = pooled.shape[0]
    n_pad = kc_ref.shape[2]
    for c in range(4):
        r = _nn(pooled[:, c * LANE:(c + 1) * LANE].astype(BF16), w_ref[c]).astype(BF16)
        dst = kc_ref if c < 2 else vc_ref
        if n_pad > n:
            dst[0, c % 2] = jnp.zeros((n_pad, LANE), BF16)
        dst[0, c % 2, 0:n, :] = r


def _cmp_proj(pooled, w4):
    b, n, _ = pooled.shape
    n_pad = -(-n // LANE) * LANE
    spec = pl.BlockSpec((1, NSA_KV, n_pad, LANE), lambda bi: (bi, 0, 0, 0))
    return pl.pallas_call(
        _cmp_proj_kernel,
        out_shape=[jax.ShapeDtypeStruct((b, NSA_KV, n_pad, LANE), BF16)] * 2,
        grid=(b,),
        in_specs=[pl.BlockSpec((1, n, 512), lambda bi: (bi, 0, 0)),
                  pl.BlockSpec((4, LANE, LANE), lambda bi: (0, 0, 0))],
        out_specs=[spec, spec],
        compiler_params=_params(("parallel",)),
        name="cmp_proj",
    )(pooled, w4)


def _lane_parts(x):
    return [x[:, j * LANE:(j + 1) * LANE] for j in range(x.shape[1] // LANE)]


def _nsa_prompt_kernel(*refs, t_len, tq, tc, tw, n_sel, top, sub, pool_pages, page_base):
    if pool_pages:
        (pt_sm, qc_ref, qr_ref, ng_ref, bg_ref, ksel_ref, vsel_ref, kwin_ref, vwin_ref, kc_ref, vc_ref,
         pa_ref, cache_ref, o_ref, pool_ref,
         qc_scr, qr_scr, s_scr, p_scr, a_scr, m_scr, l_scr, acc_scr, oc_scr, b_scr, pbuf, psem) = refs
        step = (pl.program_id(0) * pl.num_programs(1) + pl.program_id(1)) * pl.num_programs(2) + pl.program_id(2)
        n_steps = pl.num_programs(0) * pl.num_programs(1) * pl.num_programs(2)
        slot = step % 2
        half_rows, n_cols = pbuf.shape[2], 2 * NSA_KV

        def page_copies(step_idx, p, to_slot):
            page = pt_sm[step_idx * pool_pages + p] + page_base
            return [pltpu.make_async_copy(cache_ref.at[page, pl.ds(hh * half_rows, half_rows), pl.ds(0, n_cols), :],
                                          pbuf.at[to_slot, p, :, pl.ds(hh * n_cols, n_cols), :], psem.at[to_slot])
                    for hh in range(2)]

        @pl.when(step == 0)
        def _():
            for p in range(pool_pages):
                for cp in page_copies(0, p, 0):
                    cp.start()

        @pl.when(step + 1 < n_steps)
        def _():
            for p in range(pool_pages):
                for cp in page_copies(step + 1, p, 1 - slot):
                    cp.start()
    else:
        (qc_ref, qr_ref, ng_ref, bg_ref, ksel_ref, vsel_ref, kwin_ref, vwin_ref, kc_ref, vc_ref,
         o_ref, qc_scr, qr_scr, s_scr, p_scr, a_scr, m_scr, l_scr, acc_scr, oc_scr, b_scr) = refs
    i = pl.program_id(2)
    hq = NSA_HPG
    for h in range(hq):
        qc_scr[pl.ds(h * tq, tq), :] = qc_ref[0, :, h * LANE:(h + 1) * LANE]
        qr_scr[pl.ds(h * tq, tq), :] = qr_ref[0, :, h * LANE:(h + 1) * LANE]

    kc = kc_ref[0, 0]
    npad = kc.shape[0]
    s_scr[:, 0:npad] = _nt(qc_scr[...], kc)
    tpos = i * tq + lax.broadcasted_iota(jnp.int32, (tq, npad), 0)
    ncol = lax.broadcasted_iota(jnp.int32, (tq, npad), 1)
    cmask = (ncol + 1) * CMP_BLOCK <= tpos + 1
    imp = jnp.zeros((tq, npad), F32)
    for h in range(hq):
        r = pl.ds(h * tq, tq)
        s = jnp.where(cmask, s_scr[r, 0:npad], MASK_NEG)
        e = jnp.where(cmask, jnp.exp2(s - jnp.max(s, axis=-1, keepdims=True)), 0.0)
        p = e / jnp.maximum(jnp.sum(e, axis=-1, keepdims=True), 1e-30)
        p_scr[r, 0:npad] = p.astype(BF16)
        imp = imp + p
    oc_scr[...] = _nn(p_scr[:, 0:npad], vc_ref[0, 0])

    rows = min(npad, -(-n_sel // 8) * 8)
    imp_t = imp.T[0:rows]
    blk = lax.broadcasted_iota(jnp.int32, (rows, tq), 0)
    cur = (i * tq + lax.broadcasted_iota(jnp.int32, (rows, tq), 1)) // SEL_BLOCK
    imp_t = jnp.where((blk == 0) | (blk == cur) | (blk == cur - 1), FORCE, imp_t)
    imp_t = jnp.where(blk > cur, -1.0, imp_t)
    imp_t = jnp.where(blk >= n_sel, -2.0, imp_t)
    rank = jnp.zeros((rows, tq), F32)
    for j in range(n_sel):
        a = imp_t[j:j + 1, :]
        ahead = (a > imp_t) | ((a == imp_t) & (blk > j))
        rank = rank + jnp.where(ahead, 1.0, 0.0)
    sel_t = jnp.where((rank < top) & (blk < n_sel), 1.0, 0.0)
    if npad > rows:
        sel_t = jnp.concatenate([sel_t, jnp.zeros((npad - rows, tq), F32)], axis=0)
    sel = sel_t.T.astype(BF16)

    m_scr[...] = jnp.full(m_scr.shape, MASK_NEG, F32)
    l_scr[...] = jnp.zeros(l_scr.shape, F32)
    acc_scr[...] = jnp.zeros(acc_scr.shape, F32)
    n_chunks = (i * tq + tq + tc - 1) // tc

    def chunk(ci, carry):
        k0 = pl.multiple_of(ci * tc, tc)
        k = ksel_ref[0, pl.ds(k0, tc), :]
        v = vsel_ref[0, pl.ds(k0, tc), :]
        jb = lax.broadcasted_iota(jnp.int32, (npad, tc), 0)
        kb = (k0 + lax.broadcasted_iota(jnp.int32, (npad, tc), 1)) // SEL_BLOCK
        expand = jnp.where(jb == kb, 1.0, 0.0).astype(BF16)
        chosen = _nn(sel, expand)
        kp = k0 + lax.broadcasted_iota(jnp.int32, (tq, tc), 1)
        tp = i * tq + lax.broadcasted_iota(jnp.int32, (tq, tc), 0)
        b_scr[:, 0:tc] = jnp.where((chosen > 0.5) & (kp <= tp), 0.0, MASK_NEG)
        s_scr[:, 0:tc] = _nt(qr_scr[...], k)
        for r0 in range(0, hq * tq, sub):
            r = pl.ds(r0, sub)
            parts = _lane_parts(s_scr[r, 0:tc] + b_scr[pl.ds(r0 % tq, sub), 0:tc])
            m_prev = m_scr[r, :]
            m_new = jnp.maximum(m_prev, jnp.max(functools.reduce(jnp.maximum, parts), axis=-1, keepdims=True))
            alpha = jnp.exp2(m_prev - m_new)
            pes = [jnp.exp2(x - m_new) for x in parts]
            l_scr[r, :] = alpha * l_scr[r, :] + functools.reduce(jnp.add, pes)
            p_scr[r, 0:tc] = jnp.concatenate(pes, axis=1).astype(BF16)
            a_scr[r, :] = alpha
            m_scr[r, :] = m_new
        acc_scr[...] = a_scr[...] * acc_scr[...] + _nn(p_scr[:, 0:tc], v)
        return carry

    lax.fori_loop(0, n_chunks, chunk, 0)

    w0 = pl.multiple_of(jnp.clip(i * tq + tq - tw, 0, t_len - tw), LANE)
    kp = w0 + lax.broadcasted_iota(jnp.int32, (tq, tw), 1)
    tp = i * tq + lax.broadcasted_iota(jnp.int32, (tq, tw), 0)
    b_scr[:, 0:tw] = jnp.where((kp <= tp) & (kp > tp - WINDOW), 0.0, MASK_NEG)
    s_scr[:, 0:tw] = _nt(qr_scr[...], kwin_ref[0, pl.ds(w0, tw), :])
    for r0 in range(0, hq * tq, sub):
        r = pl.ds(r0, sub)
        parts = _lane_parts(s_scr[r, 0:tw] + b_scr[pl.ds(r0 % tq, sub), 0:tw])
        m = jnp.max(functools.reduce(jnp.maximum, parts), axis=-1, keepdims=True)
        ews = [jnp.exp2(x - m) for x in parts]
        a_scr[r, :] = jnp.broadcast_to(jnp.sum(functools.reduce(jnp.add, ews), axis=-1, keepdims=True), (sub, LANE))
        p_scr[r, 0:tw] = jnp.concatenate(ews, axis=1).astype(BF16)
    o_w = _nn(p_scr[:, 0:tw], vwin_ref[0, pl.ds(w0, tw), :])

    gate = jax.nn.sigmoid(bg_ref[0])
    for g in range(1, NSA_KV):
        gate = jnp.where(pl.program_id(1) == g, pltpu.roll(gate, LANE - g * hq * N_BRANCH, 1), gate)
    ng = ng_ref[0]
    for h in range(hq):
        r = slice(h * tq, (h + 1) * tq)
        o_s = acc_scr[r, :] / jnp.sum(l_scr[r, :], axis=-1, keepdims=True)
        o = (gate[:, 3 * h:3 * h + 1] * oc_scr[r, :] + gate[:, 3 * h + 1:3 * h + 2] * o_s
             + gate[:, 3 * h + 2:3 * h + 3] * (o_w[r] / a_scr[r, :]))
        o_ref[0, :, h * LANE:(h + 1) * LANE] = (_silu(ng[:, h * LANE:(h + 1) * LANE]) * o).astype(o_ref.dtype)

    if pool_pages:
        for p in range(pool_pages):
            for cp in page_copies(step, p, slot):
                cp.wait()
        pa = pa_ref[...]
        for p in range(pool_pages):
            pool_ref[0, p] = jnp.sum(pbuf[slot, p] * pa, axis=0)


def _nsa_prompt(qc3, qr3, ng3, bg3, kvb, kc, vc, *, tq=128, tc=512, sub=64, pool=None):
    b, t, _ = qc3.shape
    nq = t // tq
    tc = min(tc, t)
    tw = min(WINDOW + tq, t)
    n_sel = t // SEL_BLOCK
    top = min(TOP_N, n_sel)
    npad = kc.shape[2]
    gw = NSA_HPG * LANE
    rows = NSA_HPG * tq
    wide = max(tc, tw, npad)
    kvspec = lambda k: pl.BlockSpec((1, t, LANE), lambda bi, gi, qi, *_, k=k: (bi, 0, k + gi))
    cspec = pl.BlockSpec((1, 1, npad, LANE), lambda bi, gi, qi, *_: (bi, gi, 0, 0))
    gspec = pl.BlockSpec((1, tq, gw), lambda bi, gi, qi, *_: (bi, qi, gi))
    in_specs = [gspec, gspec, gspec,
                pl.BlockSpec((1, tq, LANE), lambda bi, gi, qi, *_: (bi, qi, 0)),
                kvspec(0), kvspec(2), kvspec(4), kvspec(6), cspec, cspec]
    out_shape = [jax.ShapeDtypeStruct((b, t, NSA_HEADS * LANE), BF16)]
    out_specs = [pl.BlockSpec((1, tq, gw), lambda bi, gi, qi, *_: (bi, qi, gi))]
    scratch = ([pltpu.VMEM((rows, LANE), BF16)] * 2
               + [pltpu.VMEM((rows, wide), F32), pltpu.VMEM((rows, wide), BF16)]
               + [pltpu.VMEM((rows, LANE), F32)] * 5 + [pltpu.VMEM((tq, wide), F32)])
    args = [qc3, qr3, ng3, bg3, kvb, kvb, kvb, kvb, kc, vc]
    n_steps = b * NSA_KV * nq
    pool_pages, page_base, prefetch = 0, 0, []
    if pool is not None:
        cache4, page_flat, a3, page_base = pool
        page, n_cols = cache4.shape[1], 2 * NSA_KV
        if page_flat.shape[0] % n_steps == 0 and page == 2 * CMP_BLOCK and 2 * n_cols == 8:
            pool_pages = page_flat.shape[0] // n_steps
            prefetch = [page_flat]
            in_specs += [pl.BlockSpec((CMP_BLOCK, 2 * n_cols, LANE), lambda bi, gi, qi, *_: (0, 0, 0)),
                         pl.BlockSpec(memory_space=pl.ANY)]
            args += [jnp.concatenate([a3, a3], axis=1), cache4]
            out_shape.append(jax.ShapeDtypeStruct((n_steps, pool_pages, 2 * n_cols, LANE), F32))
            out_specs.append(pl.BlockSpec((1, pool_pages, 2 * n_cols, LANE),
                                          lambda bi, gi, qi, *_: ((bi * NSA_KV + gi) * nq + qi, 0, 0, 0)))
            scratch += [pltpu.VMEM((2, pool_pages, CMP_BLOCK, 2 * n_cols, LANE), F32), pltpu.SemaphoreType.DMA((2,))]
    kern = functools.partial(_nsa_prompt_kernel, t_len=t, tq=tq, tc=tc, tw=tw, n_sel=n_sel, top=top,
                             sub=min(sub, tq), pool_pages=pool_pages, page_base=page_base)
    res = pl.pallas_call(
        kern,
        out_shape=out_shape,
        grid_spec=pltpu.PrefetchScalarGridSpec(
            num_scalar_prefetch=len(prefetch), grid=(b, NSA_KV, nq),
            in_specs=in_specs, out_specs=out_specs, scratch_shapes=scratch),
        compiler_params=_params(("arbitrary", "arbitrary", "arbitrary")),
        name="nsa_prompt",
    )(*prefetch, *args)
    return (res[0], res[1]) if pool_pages else (res[0], None)


def _conv_kernel(h_ref, b_ref, c_ref, g_ref, prev_ref, w_ref, y_ref, st_ref, up_scr, *, t_real):
    u = c_ref[0] * h_ref[0]
    t = u.shape[0]
    up_scr[pl.ds(8 - (CONV_W - 1), CONV_W - 1), :] = prev_ref[0]
    up_scr[pl.ds(8, t), :] = u
    w = w_ref[...]
    y = w[0:1, :] * up_scr[pl.ds(6, t), :]
    y = y + w[1:2, :] * up_scr[pl.ds(7, t), :]
    y = y + w[2:3, :] * u
    y = b_ref[0] * y
    y_ref[0] = (_silu(g_ref[0]) * y).astype(y_ref.dtype)
    st_ref[0] = up_scr[pl.ds(6 + t_real, CONV_W - 1), :]


def _conv_mixer(z3, prev, w_conv, *, t_real):
    b, t, _ = z3.shape
    nc = CONV_DIM // LANE
    zspec = lambda off: pl.BlockSpec((1, t, LANE), lambda bi, ci, off=off: (bi, 0, off // LANE + ci))
    return pl.pallas_call(
        functools.partial(_conv_kernel, t_real=t_real),
        out_shape=[jax.ShapeDtypeStruct((b, t, CONV_DIM), BF16),
                   jax.ShapeDtypeStruct((b, CONV_W - 1, CONV_DIM), F32)],
        grid=(b, nc),
        in_specs=[zspec(C_H), zspec(C_B), zspec(C_C), zspec(C_G),
                  pl.BlockSpec((1, CONV_W - 1, LANE), lambda bi, ci: (bi, 0, ci)),
                  pl.BlockSpec((CONV_W, LANE), lambda bi, ci: (0, ci))],
        out_specs=[pl.BlockSpec((1, t, LANE), lambda bi, ci: (bi, 0, ci)),
                   pl.BlockSpec((1, CONV_W - 1, LANE), lambda bi, ci: (bi, 0, ci))],
        scratch_shapes=[pltpu.VMEM((t + 8, LANE), F32)],
        compiler_params=_params(("parallel", "parallel")),
        name="conv_mixer",
    )(z3, z3, z3, z3, prev, w_conv)


def _mem_attn_kernel(q_ref, mg_ref, kv_ref, o_ref, *, interleaved):
    q = q_ref[0]
    mg = mg_ref[0]
    half = MEM_HEADS * LANE
    for h in range(MEM_HEADS):
        lo, hi = h * LANE, (h + 1) * LANE
        if interleaved:
            nm = kv_ref.shape[1] // (2 * MEM_HEADS)
            k = kv_ref[0, pl.ds(h, nm, stride=2 * MEM_HEADS), :].astype(BF16)
            v = kv_ref[0, pl.ds(MEM_HEADS + h, nm, stride=2 * MEM_HEADS), :].astype(BF16)
        else:
            k = kv_ref[0, :, lo:hi].astype(BF16)
            v = kv_ref[0, :, half + lo:half + hi].astype(BF16)
        s = _nt((q[:, lo:hi] * ATTN_SCALE).astype(BF16), k)
        e = jnp.exp(s - jnp.max(s, axis=-1, keepdims=True))
        o = _nn(e.astype(BF16), v) / jnp.sum(e, axis=-1, keepdims=True)
        o_ref[0, :, lo:hi] = (_silu(mg[:, lo:hi]) * o).astype(o_ref.dtype)


def _mem_attn(zb3, mkv, *, mq_off, mg_off, tq, interleaved, kv_base=0):
    b, t, _ = zb3.shape
    wq = MEM_HEADS * LANE
    return pl.pallas_call(
        functools.partial(_mem_attn_kernel, interleaved=interleaved),
        out_shape=jax.ShapeDtypeStruct((b, t, wq), BF16),
        grid=(b, t // tq),
        in_specs=[pl.BlockSpec((1, tq, wq), lambda bi, ti: (bi, ti, mq_off // wq)),
                  pl.BlockSpec((1, tq, wq), lambda bi, ti: (bi, ti, mg_off // wq)),
                  pl.BlockSpec((1,) + mkv.shape[1:], lambda bi, ti: (bi + kv_base, 0, 0))],
        out_specs=pl.BlockSpec((1, tq, wq), lambda bi, ti: (bi, ti, 0)),
        compiler_params=_params(("parallel", "parallel")),
        name="mem_attn",
    )(zb3, zb3, mkv)


def _out_proj_kernel(x_ref, ya_ref, yb_ref, ym_ref, w_ref, fg_ref, o_ref, *, final):
    a, bw = CONV_DIM, CONV_DIM + NSA_HEADS * LANE
    acc = _nn(ya_ref[...], w_ref[0:a, :])
    acc = acc + _nn(yb_ref[...], w_ref[a:bw, :])
    acc = acc + _nn(ym_ref[...], w_ref[bw:, :])
    r = x_ref[...] + acc
    if final:
        r = r * lax.rsqrt(jnp.mean(r * r, axis=-1, keepdims=True) + NORM_EPS) * fg_ref[...]
    o_ref[...] = r


def _out_proj(x, ya, yb, ym, w, fg, *, tm, final):
    m, d = x.shape
    row = lambda width: pl.BlockSpec((tm, width), lambda i: (i, 0))
    return pl.pallas_call(
        functools.partial(_out_proj_kernel, final=final),
        out_shape=jax.ShapeDtypeStruct((m, d), F32),
        grid=(m // tm,),
        in_specs=[row(d), row(ya.shape[1]), row(yb.shape[1]), row(ym.shape[1]),
                  pl.BlockSpec(w.shape, lambda i: (0, 0)),
                  pl.BlockSpec((1, d), lambda i: (0, 0))],
        out_specs=row(d),
        compiler_params=_params(("parallel",)),
        name="out_proj",
    )(x, ya, yb, ym, w, fg.reshape(1, d))


def _sample_pre_kernel(q_ref, ng_ref, bg_ref, kv0_ref, kv1_ref, kv2_ref, c_ref, s1_ref, s2_ref, kc_ref, vc_ref,
                       kvn_ref, wn_ref, qr_ref, oc_ref, ngo_ref, gate_ref, val_ref, *, past):
    tp = SAMPLE_T_PAD
    hq = NSA_HPG
    c, s1, s2 = c_ref[...], s1_ref[...], s2_ref[...]
    kv0, kv1, kv2 = kv0_ref[0], kv1_ref[0], kv2_ref[0]
    kvn_ref[0, :, 0:512] = kv0
    kvn_ref[0, :, 768:1024] = kv1[:, 256:512]
    wn_ref[0, :, 256:512] = kv2[:, 256:512]
    for g in range(NSA_KV):
        lo, hi = g * LANE, (g + 1) * LANE
        kvn_ref[0, :, 512 + lo:512 + hi] = _rope(kv1[:, lo:hi], c, s1, s2)
        wn_ref[0, :, lo:hi] = _rope(kv2[:, lo:hi], c, s1, s2)

    q = q_ref[0]
    ng = ng_ref[0]
    gates = jax.nn.sigmoid(bg_ref[0])
    for g in range(NSA_KV):
        qc_l, qr_l = [], []
        for h in range(hq):
            lo = (g * hq + h) * LANE
            qh = q[:, lo:lo + LANE]
            qc_l.append(qh * ATTN_SCALE)
            qr_l.append(_rope(qh, c, s1, s2) * ATTN_SCALE)
            ngo_ref[0, g, h * tp:(h + 1) * tp, :] = ng[:, lo:lo + LANE]
            for br in range(N_BRANCH):
                col = (g * hq + h) * N_BRANCH + br
                gate_ref[0, g, br, h * tp:(h + 1) * tp, :] = jnp.broadcast_to(gates[:, col:col + 1], (tp, LANE))
        qc = jnp.concatenate(qc_l, axis=0)
        qr_ref[0, g] = jnp.concatenate(qr_l, axis=0)

        kc = kc_ref[0, g]
        npad = kc.shape[0]
        s = _nt(qc.astype(BF16), kc)
        trow = lax.broadcasted_iota(jnp.int32, (hq * tp, npad), 0) % tp
        ncol = lax.broadcasted_iota(jnp.int32, (hq * tp, npad), 1)
        cmask = (ncol + 1) * CMP_BLOCK <= past + trow + 1
        s = jnp.where(cmask, s, MASK_NEG)
        e = jnp.where(cmask, jnp.exp(s - jnp.max(s, axis=-1, keepdims=True)), 0.0)
        p = e / jnp.maximum(jnp.sum(e, axis=-1, keepdims=True), 1e-30)
        oc_ref[0, g] = _nn(p.astype(BF16), vc_ref[0, g])
        imp = jnp.sum(p.reshape(hq, tp, npad), axis=0)

        blk = lax.broadcasted_iota(jnp.int32, (tp, npad), 1)
        cur = (past + lax.broadcasted_iota(jnp.int32, (tp, npad), 0)) // SEL_BLOCK
        val = jnp.where((blk == 0) | (blk == cur) | (blk == cur - 1), FORCE, imp)
        val = jnp.where(blk > cur, -1.0, val)
        val = jnp.where(blk >= past // SEL_BLOCK, -2.0, val)
        val_ref[0, g] = val


def _topk_kernel(val_ref, idx_ref, *, n_top):
    val = val_ref[...]
    rows, n = val.shape
    blk = lax.broadcasted_iota(jnp.int32, (rows, n), 1)
    lane = lax.broadcasted_iota(jnp.int32, (rows, LANE), 1)
    idx = jnp.zeros((rows, LANE), jnp.int32)
    for r in range(n_top):
        best = jnp.max(val, axis=-1, keepdims=True)
        j = jnp.min(jnp.where(val == best, blk, n), axis=-1, keepdims=True)
        idx = jnp.where(lane == r, j, idx)
        val = jnp.where(blk == j, -3e38, val)
    idx_ref[...] = idx


def _topk(val2, *, n_top):
    rows, n = val2.shape
    return pl.pallas_call(
        functools.partial(_topk_kernel, n_top=n_top),
        out_shape=jax.ShapeDtypeStruct((rows, LANE), jnp.int32),
        grid=(1,),
        in_specs=[pl.BlockSpec((rows, n), lambda i: (0, 0))],
        out_specs=pl.BlockSpec((rows, LANE), lambda i: (0, 0)),
        compiler_params=_params(("arbitrary",)),
        name="sample_topk",
    )(val2)


def _sample_pre(za3, zb3, bg3, tabs, kc, vc, *, past):
    bs, tp, _ = za3.shape
    npad = kc.shape[2]
    qw = NSA_HEADS * LANE
    kvblk = KV_OFF // 512
    zspec = lambda k: pl.BlockSpec((1, tp, 512), lambda bi, k=k: (bi, 0, kvblk + k))
    tspec = pl.BlockSpec((tp, LANE), lambda bi: (0, 0))
    cspec = pl.BlockSpec((1, NSA_KV, npad, LANE), lambda bi: (bi, 0, 0, 0))
    rows = NSA_HPG * tp
    gspec = pl.BlockSpec((1, NSA_KV, rows, LANE), lambda bi: (bi, 0, 0, 0))
    gshape = jax.ShapeDtypeStruct((bs, NSA_KV, rows, LANE), F32)
    return pl.pallas_call(
        functools.partial(_sample_pre_kernel, past=past),
        out_shape=[jax.ShapeDtypeStruct((bs, tp, 1024), F32),
                   jax.ShapeDtypeStruct((bs, tp, 512), F32),
                   gshape, gshape, gshape,
                   jax.ShapeDtypeStruct((bs, NSA_KV, N_BRANCH, rows, LANE), F32),
                   jax.ShapeDtypeStruct((bs, NSA_KV, tp, npad), F32)],
        grid=(bs,),
        in_specs=[pl.BlockSpec((1, tp, qw), lambda bi: (bi, 0, Q_OFF // qw)),
                  pl.BlockSpec((1, tp, qw), lambda bi: (bi, 0, NG_OFF // qw)),
                  pl.BlockSpec((1, tp, LANE), lambda bi: (bi, 0, 0)),
                  zspec(0), zspec(1), zspec(2), tspec, tspec, tspec, cspec, cspec],
        out_specs=[pl.BlockSpec((1, tp, 1024), lambda bi: (bi, 0, 0)),
                   pl.BlockSpec((1, tp, 512), lambda bi: (bi, 0, 0)),
                   gspec, gspec, gspec,
                   pl.BlockSpec((1, NSA_KV, N_BRANCH, rows, LANE), lambda bi: (bi, 0, 0, 0, 0)),
                   pl.BlockSpec((1, NSA_KV, tp, npad), lambda bi: (bi, 0, 0, 0))],
        compiler_params=_params(("parallel",)),
        name="sample_pre",
    )(za3, za3, bg3, zb3, zb3, zb3, *tabs, kc, vc)


def _sample_attn_kernel(idx_sm, pt_sm, qr_ref, oc_ref, ng_ref, gate_ref, ksn_ref, vsn_ref,
                        wc_ref, kwn_ref, vwn_ref, cache_ref, o_ref,
                        kbuf, vbuf, kw_scr, vw_scr, sem, *, ts, n_top, n_pages, page_base, per_page, wb):
    tp = SAMPLE_T_PAD
    hq = NSA_HPG
    b = pl.program_id(0)
    g = pl.program_id(1)
    n_gath = n_top * SEL_BLOCK
    ks_rows = kbuf.shape[1]

    def gather_copies(t, r):
        blk = idx_sm[((b * NSA_KV + g) * ts + t) * n_top + r]
        page = pt_sm[b * n_pages + blk // per_page] + page_base
        row0 = (blk % per_page) * SEL_BLOCK
        src_k = cache_ref.at[page, pl.ds(row0, SEL_BLOCK), 2 * NSA_KV + g]
        src_v = cache_ref.at[page, pl.ds(row0, SEL_BLOCK), 3 * NSA_KV + g]
        dst = pl.ds(r * SEL_BLOCK, SEL_BLOCK)
        return (pltpu.make_async_copy(src_k, kbuf.at[t, dst], sem.at[0]),
                pltpu.make_async_copy(src_v, vbuf.at[t, dst], sem.at[1]))

    for t in range(ts):
        for r in range(n_top):
            ck, cv = gather_copies(t, r)
            ck.start()
            cv.start()

    qr = qr_ref[0, 0].astype(BF16)
    trow = lax.broadcasted_iota(jnp.int32, (hq * tp, 1), 0) % tp

    ww = kw_scr.shape[0]
    kw_scr[pl.ds(0, wb), :] = wc_ref[0, pl.ds(g, wb, stride=2 * NSA_KV), :]
    vw_scr[pl.ds(0, wb), :] = wc_ref[0, pl.ds(NSA_KV + g, wb, stride=2 * NSA_KV), :]
    kw_scr[pl.ds(wb, tp), :] = kwn_ref[0]
    vw_scr[pl.ds(wb, tp), :] = vwn_ref[0]
    kw_scr[pl.ds(wb + tp, ww - wb - tp), :] = jnp.zeros((ww - wb - tp, LANE), F32)
    vw_scr[pl.ds(wb + tp, ww - wb - tp), :] = jnp.zeros((ww - wb - tp, LANE), F32)
    sw = _nt(qr, kw_scr[...].astype(BF16))
    jw = lax.broadcasted_iota(jnp.int32, (hq * tp, ww), 1)
    rel = jw - wb
    okw = (rel <= trow) & (rel > trow - WINDOW) & (jw < wb + ts)
    sw = jnp.where(okw, sw, MASK_NEG)
    ew = jnp.where(okw, jnp.exp(sw - jnp.max(sw, axis=-1, keepdims=True)), 0.0)
    o_w = _nn(ew.astype(BF16), vw_scr[...].astype(BF16)) / jnp.sum(ew, axis=-1, keepdims=True)

    for t in range(ts):
        for r in range(n_top):
            ck, cv = gather_copies(t, r)
            ck.wait()
            cv.wait()

    js = lax.broadcasted_iota(jnp.int32, (hq * tp, ks_rows), 1)
    o_s = jnp.zeros((hq * tp, LANE), F32)
    for t in range(ts):
        kbuf[t, pl.ds(n_gath, tp), :] = ksn_ref[0]
        vbuf[t, pl.ds(n_gath, tp), :] = vsn_ref[0]
        kbuf[t, pl.ds(n_gath + tp, ks_rows - n_gath - tp), :] = jnp.zeros((ks_rows - n_gath - tp, LANE), F32)
        vbuf[t, pl.ds(n_gath + tp, ks_rows - n_gath - tp), :] = jnp.zeros((ks_rows - n_gath - tp, LANE), F32)
        ss = _nt(qr, kbuf[t].astype(BF16))
        oks = (js < n_gath) | ((js - n_gath <= t) & (js < n_gath + ts))
        ss = jnp.where(oks, ss, MASK_NEG)
        es = jnp.where(oks, jnp.exp(ss - jnp.max(ss, axis=-1, keepdims=True)), 0.0)
        ot = _nn(es.astype(BF16), vbuf[t].astype(BF16)) / jnp.sum(es, axis=-1, keepdims=True)
        o_s = jnp.where(trow == t, ot, o_s)

    o = gate_ref[0, 0, 0] * oc_ref[0, 0] + gate_ref[0, 0, 1] * o_s + gate_ref[0, 0, 2] * o_w
    y = _silu(ng_ref[0, 0]) * o
    for h in range(hq):
        o_ref[0, :, h * LANE:(h + 1) * LANE] = y[h * tp:(h + 1) * tp].astype(o_ref.dtype)


def _sample_attn(idx_flat, page_flat, qr, oc, ngo, gates, kvn, cache_win_rows, wn, cache4, *,
                 ts, n_top, n_pages, page_base, win_base):
    bs = qr.shape[0]
    tp = SAMPLE_T_PAD
    rows = NSA_HPG * tp
    wb = cache_win_rows.shape[1] // (2 * NSA_KV)
    per_page = cache4.shape[1] // SEL_BLOCK
    ks_rows = -(-(n_top * SEL_BLOCK + tp) // LANE) * LANE
    ww = -(-(wb + tp) // LANE) * LANE
    gspec = pl.BlockSpec((1, 1, rows, LANE), lambda bi, gi, *_: (bi, gi, 0, 0))
    newspec = lambda k: pl.BlockSpec((1, tp, LANE), lambda bi, gi, *_, k=k: (bi, 0, k + gi))
    grid_spec = pltpu.PrefetchScalarGridSpec(
        num_scalar_prefetch=2, grid=(bs, NSA_KV),
        in_specs=[gspec, gspec, gspec,
                  pl.BlockSpec((1, 1, N_BRANCH, rows, LANE), lambda bi, gi, *_: (bi, gi, 0, 0, 0)),
                  newspec(2 * NSA_KV), newspec(3 * NSA_KV),
                  pl.BlockSpec((1,) + cache_win_rows.shape[1:], lambda bi, gi, *_: (bi + win_base, 0, 0)),
                  newspec(0), newspec(NSA_KV),
                  pl.BlockSpec(memory_space=pl.ANY)],
        out_specs=pl.BlockSpec((1, tp, NSA_HPG * LANE), lambda bi, gi, *_: (bi, 0, gi)),
        scratch_shapes=[pltpu.VMEM((ts, ks_rows, LANE), F32), pltpu.VMEM((ts, ks_rows, LANE), F32),
                        pltpu.VMEM((ww, LANE), F32), pltpu.VMEM((ww, LANE), F32),
                        pltpu.SemaphoreType.DMA((2,))])
    kern = functools.partial(_sample_attn_kernel, ts=ts, n_top=n_top, n_pages=n_pages,
                             page_base=page_base, per_page=per_page, wb=wb)
    return pl.pallas_call(
        kern,
        out_shape=jax.ShapeDtypeStruct((bs, tp, NSA_HEADS * LANE), BF16),
        grid_spec=grid_spec,
        compiler_params=_params(("arbitrary", "arbitrary")),
        name="sample_attn",
    )(idx_flat, page_flat, qr, oc, ngo, gates, kvn, kvn, cache_win_rows, wn, wn, cache4)


def _rope_tables(pos, rows):
    freqs = jnp.power(ROPE_THETA, -jnp.arange(ROPE_HALF, dtype=F32) * (2.0 / ROPE_DIM))
    ang = pos.astype(F32)[:, None] * freqs[None, :]
    cos, sin = jnp.cos(ang), jnp.sin(ang)
    n = pos.shape[0]
    z16 = jnp.zeros((n, ROPE_HALF), F32)
    rest = LANE - ROPE_DIM
    c = jnp.concatenate([cos, cos, jnp.ones((n, rest), F32)], axis=1)
    s1 = jnp.concatenate([z16, sin, jnp.zeros((n, rest), F32)], axis=1)
    s2 = jnp.concatenate([-sin, z16, jnp.zeros((n, rest), F32)], axis=1)
    pad = lambda a: jnp.pad(a, ((0, rows - n), (0, 0)))
    return pad(c), pad(s1), pad(s2)


def _layer_weights(norm_g, w_in3, layer, w_conv, a_cmp, w_cmp, w_out):
    w_a, w_b, wbg = _pack_in_proj_weights(jnp.swapaxes(w_in3, 1, 2), layer)
    a4 = jnp.concatenate([a_cmp[0], a_cmp[0], a_cmp[1], a_cmp[1]], axis=1)
    a3 = jnp.stack([a_cmp[0], a_cmp[0], a_cmp[1], a_cmp[1]], axis=1)
    w4 = jnp.stack([w_cmp[0], w_cmp[0], w_cmp[1], w_cmp[1]]).astype(BF16)
    return dict(norm_g=norm_g, w_a=w_a, w_b=w_b, wbg=wbg, w_conv=w_conv, a4=a4, a3=a3, w4=w4,
                w_out=w_out.astype(BF16))


def _in_proj(x2, lw, *, tm):
    za, bg = _norm_matmul(x2, lw["norm_g"], lw["w_a"], lw["wbg"], tm=tm, tn=512)
    zb = _norm_matmul(x2, lw["norm_g"], lw["w_b"], tm=tm, tn=512)
    return za, zb, bg


def _prompt_layer(xp, mem_prompt, mem_norm_g, w_mem, lw, final_g, final, pool):
    b, t, d = xp.shape
    m = b * t
    x2 = xp.reshape(m, d)
    tabs = _rope_tables(jnp.arange(t, dtype=jnp.int32), t)
    wb = min(WINDOW, t)
    qc, qr, ng, bg, ya, conv_new = _proj_conv(x2, lw["norm_g"], lw["w_a"], lw["wbg"], lw["w_conv"], tabs,
                                              tm=min(512, t), seq_len=t)
    kvn, kvb, win_rows, pooled_prompt, zm = _proj_kv(x2, lw["norm_g"], lw["w_b"], tabs, lw["a4"], tm=wb, seq_len=t)
    nm = mem_prompt.shape[1]
    mkv = _norm_matmul(mem_prompt.reshape(b * nm, d), mem_norm_g, w_mem, tm=min(512, b * nm), tn=512)
    mkv3 = mkv.reshape(b, nm, 2 * MEM_HEADS * LANE)
    kc, vc = _cmp_proj(pooled_prompt.reshape(b, t // CMP_BLOCK, -1), lw["w4"])
    r3 = lambda a: a.reshape(b, t, -1)
    yb, pooled = _nsa_prompt(r3(qc), r3(qr), r3(ng), r3(bg), r3(kvb), kc, vc, tq=min(256, t), pool=pool)
    ym = _mem_attn(zm.reshape(b, t, -1), mkv3, mq_off=0, mg_off=MG_OFF - MQ_OFF, tq=min(512, t), interleaved=False)
    out = _out_proj(x2, ya, yb.reshape(m, -1), ym.reshape(m, -1),
                    lw["w_out"], final_g, tm=min(512, m), final=final)
    kv_new = kvn.reshape(b, t, 4, NSA_KV, HEAD_DIM)
    win_new = win_rows.reshape(b, wb, 2, NSA_KV, HEAD_DIM)
    mem_kv = mkv.reshape(b, nm, 2, MEM_HEADS, HEAD_DIM)
    return out.reshape(b, t, d), kv_new, win_new, conv_new, mem_kv, pooled


def _sample_layer(xs_p, ts, layer, cache4, page_flat, pooled, cache_win, state_conv, cache_mem, lw, final_g, final):
    bs, tp, d = xs_p.shape
    depth = cache_win.shape[0]
    pool, page = cache4.shape[0] // depth, cache4.shape[1]
    n_pages = page_flat.shape[0] // bs
    past = n_pages * page
    assert past % SEL_BLOCK == 0 and ts <= SEL_BLOCK and ts <= tp
    n_past = past // SEL_BLOCK
    n_top = min(TOP_N, n_past + 1) - 1
    m = bs * tp
    za, zb, bg = _in_proj(xs_p.reshape(m, d), lw, tm=m)
    za3 = za.reshape(bs, tp, ZA_WIDTH)
    zb3 = zb.reshape(bs, tp, ZB_WIDTH)
    bg3 = bg.reshape(bs, tp, LANE)
    tabs = _rope_tables(past + jnp.arange(tp, dtype=jnp.int32), tp)
    if pooled is None:
        pooled = _pool_pages(cache4, page_flat, lw["a3"], bs=bs, n_pages=n_pages, page_base=layer * pool)
    else:
        pooled = pooled.reshape(bs, n_pages * (page // CMP_BLOCK), 2 * NSA_KV * LANE)
    kc, vc = _cmp_proj(pooled, lw["w4"])
    kvn, wn, qr, oc, ngo, gates, val = _sample_pre(za3, zb3, bg3, tabs, kc, vc, past=past)
    idx = _topk(val.reshape(bs * NSA_KV * tp, val.shape[-1]), n_top=n_top)
    idx_flat = idx.reshape(bs, NSA_KV, tp, LANE)[:, :, :ts, :n_top].reshape(-1)
    wbuf = cache_win.shape[2]
    cache_win_rows = cache_win.reshape(depth * bs, wbuf * 2 * NSA_KV, HEAD_DIM)
    yb = _sample_attn(idx_flat, page_flat, qr, oc, ngo, gates, kvn, cache_win_rows, wn, cache4,
                      ts=ts, n_top=n_top, n_pages=n_pages, page_base=layer * pool, win_base=layer * bs)
    ya, conv_new = _conv_mixer(za3, state_conv[layer], lw["w_conv"], t_real=ts)
    nm = cache_mem.shape[2]
    mem_rows = cache_mem.reshape(depth * bs, nm * 2 * MEM_HEADS, HEAD_DIM)
    ym = _mem_attn(zb3, mem_rows, mq_off=MQ_OFF, mg_off=MG_OFF, tq=tp, interleaved=True, kv_base=layer * bs)
    out = _out_proj(xs_p.reshape(m, d), ya.reshape(m, -1), yb.reshape(m, -1), ym.reshape(m, -1),
                    lw["w_out"], final_g, tm=m, final=final)
    kv_new = kvn[:, :ts].reshape(bs, ts, 4, NSA_KV, HEAD_DIM)
    win_rows = wn[:, :ts].reshape(bs, ts, 2, NSA_KV, HEAD_DIM)
    win_state = jnp.concatenate([cache_win[layer], win_rows], axis=1)[:, ts:]
    return out.reshape(bs, tp, d), kv_new, win_state, conv_new


def kernel(x_prompt, x_sample, cache_kv, cache_win, state_conv, cache_mem, page_table, mem_prompt,
           norm_g, w_in, w_conv, a_cmp, w_cmp, mem_norm_g, w_mem_kv, w_out, final_g):
    depth = w_in.shape[0]
    ts = x_sample.shape[1]
    xp = x_prompt
    xs = jnp.pad(x_sample, ((0, 0), (0, SAMPLE_T_PAD - ts), (0, 0)))
    pool_size, page = cache_kv.shape[1], cache_kv.shape[2]
    cache4 = cache_kv.reshape(depth * pool_size, page, 4 * NSA_KV, HEAD_DIM)
    page_flat = page_table.reshape(-1).astype(jnp.int32)
    kv_p, win_p, conv_p, mem_p, kv_s, win_s, conv_s = [], [], [], [], [], [], []
    for l in range(depth):
        lw = _layer_weights(norm_g[l], w_in, l, w_conv[l], a_cmp[l], w_cmp[l], w_out[l])
        final = l == depth - 1
        xp, kvn, winn, convn, mkv, pooled = _prompt_layer(
            xp, mem_prompt, mem_norm_g[l], w_mem_kv[l].T.astype(BF16), lw, final_g, final,
            pool=(cache4, page_flat, lw["a3"], l * pool_size))
        kv_p.append(kvn)
        win_p.append(winn)
        conv_p.append(convn)
        mem_p.append(mkv)
        xs, kvn, winn, convn = _sample_layer(xs, ts, l, cache4, page_flat, pooled, cache_win, state_conv,
                                             cache_mem, lw, final_g, final)
        kv_s.append(kvn)
        win_s.append(winn)
        conv_s.append(convn)
    return (xp, xs[:, :ts], jnp.stack(kv_p), jnp.stack(win_p), jnp.stack(conv_p), jnp.stack(mem_p),
            jnp.stack(kv_s), jnp.stack(win_s), jnp.stack(conv_s))
```

```python
import functools

import jax
import jax.numpy as jnp
from jax import lax
from jax.experimental import pallas as pl
from jax.experimental.pallas import tpu as pltpu

F32 = jnp.float32
BF16 = jnp.bfloat16

HEAD_DIM = 128
CONV_DIM = 512
CONV_W = 3
NSA_HEADS = 8
NSA_KV = 2
NSA_HPG = NSA_HEADS // NSA_KV
MEM_HEADS = 4
N_BRANCH = 3
ROPE_DIM = HEAD_DIM // 4
ROPE_HALF = ROPE_DIM // 2
ROPE_THETA = 500000.0
CMP_BLOCK = 64
SEL_BLOCK = 64
TOP_N = 16
WINDOW = 512
NORM_EPS = 1e-6
MASK_NEG = -1e30
FORCE = 1e9
ATTN_SCALE = HEAD_DIM ** -0.5
SCALE_LOG2 = ATTN_SCALE * 1.4426950408889634

C_H, C_B, C_C, C_G, Q_OFF, NG_OFF = 0, 512, 1024, 1536, 2048, 3072
ZA_WIDTH = 4096
KV_OFF, MQ_OFF, MG_OFF = 0, 1536, 2048
ZB_WIDTH = 2560
BG_SRC = 4096
BG_N = NSA_HEADS * N_BRANCH
LANE = 128
SAMPLE_T_PAD = 8
VMEM_LIMIT = 56 * 1024 * 1024


def _nt(a, b):
    return lax.dot_general(a, b, (((1,), (1,)), ((), ())), preferred_element_type=F32)


def _nn(a, b):
    return jnp.dot(a, b, preferred_element_type=F32)


def _params(sem, vmem=VMEM_LIMIT):
    return pltpu.CompilerParams(dimension_semantics=sem, vmem_limit_bytes=vmem)


def _rope(x, c, s1, s2):
    return x * c + pltpu.roll(x, ROPE_HALF, 1) * s1 + pltpu.roll(x, LANE - ROPE_HALF, 1) * s2


def _silu(x):
    return x * jax.nn.sigmoid(x)


def _norm_matmul_kernel(x_ref, g_ref, w_ref, *rest, with_gate, tn):
    if with_gate:
        wbg_ref, z_ref, bg_ref, h_scr = rest
    else:
        z_ref, h_scr = rest
    x = x_ref[...]
    y = x * lax.rsqrt(jnp.mean(x * x, axis=-1, keepdims=True) + NORM_EPS) * g_ref[...]
    h_scr[...] = y.astype(BF16)
    if with_gate:
        bg_ref[...] = _nt(h_scr[...], wbg_ref[...])
    for j in range(w_ref.shape[0] // tn):
        z_ref[:, j * tn:(j + 1) * tn] = _nt(h_scr[...], w_ref[j * tn:(j + 1) * tn, :])


def _norm_matmul(x, g, w, wbg=None, *, tm, tn):
    m, d = x.shape
    n = w.shape[0]
    assert n % tn == 0 and m % tm == 0 and w.shape[1] == d
    with_gate = wbg is not None
    resident = lambda shape: pl.BlockSpec(shape, lambda i: (0, 0), pipeline_mode=pl.Buffered(1))
    in_specs = [pl.BlockSpec((tm, d), lambda i: (i, 0)), resident((1, d)), resident((n, d))]
    out_shape = [jax.ShapeDtypeStruct((m, n), F32)]
    out_specs = [pl.BlockSpec((tm, n), lambda i: (i, 0))]
    args = [x, g.reshape(1, d), w]
    if with_gate:
        nb = wbg.shape[0]
        in_specs.append(resident((nb, d)))
        out_shape.append(jax.ShapeDtypeStruct((m, nb), F32))
        out_specs.append(pl.BlockSpec((tm, nb), lambda i: (i, 0)))
        args.append(wbg)
    res = pl.pallas_call(
        functools.partial(_norm_matmul_kernel, with_gate=with_gate, tn=tn),
        out_shape=out_shape, grid=(m // tm,), in_specs=in_specs, out_specs=out_specs,
        scratch_shapes=[pltpu.VMEM((tm, d), BF16)],
        compiler_params=_params(("parallel",)),
        name="norm_matmul_gate" if with_gate else "norm_matmul",
    )(*args)
    return res if with_gate else res[0]


def _rms_rows(x_ref, g_ref):
    x = x_ref[...]
    return (x * lax.rsqrt(jnp.mean(x * x, axis=-1, keepdims=True) + NORM_EPS) * g_ref[...]).astype(BF16)


def _proj_conv_kernel(x_ref, g_ref, w_ref, wbg_ref, wc_ref, c_ref, s1_ref, s2_ref, xs_ref,
                      qc_ref, qr_ref, ng_ref, bg_ref, ya_ref, st_ref, zs_ref, bgs_ref, h_scr, up_scr,
                      *, blocks_per_seq):
    tm = x_ref.shape[0]
    cw = CONV_DIM
    i = pl.program_id(0)
    first = i % blocks_per_seq == 0
    h_scr[...] = _rms_rows(x_ref, g_ref)
    bg_ref[...] = _nt(h_scr[...], wbg_ref[...])
    chunk = lambda off: _nt(h_scr[...], w_ref[off:off + cw, :])

    @pl.when(first)
    def _():
        up_scr[pl.ds(8 - (CONV_W - 1), CONV_W - 1), :] = jnp.zeros((CONV_W - 1, cw), F32)

    @pl.when(jnp.logical_not(first))
    def _():
        up_scr[pl.ds(8 - (CONV_W - 1), CONV_W - 1), :] = up_scr[pl.ds(8 + tm - (CONV_W - 1), CONV_W - 1), :]

    u = chunk(C_C) * chunk(C_H)
    up_scr[pl.ds(8, tm), :] = u
    wc = wc_ref[...]
    y = wc[0:1, :] * up_scr[pl.ds(6, tm), :]
    y = y + wc[1:2, :] * up_scr[pl.ds(7, tm), :]
    y = y + wc[2:3, :] * u
    y = chunk(C_B) * y
    ya_ref[...] = (_silu(chunk(C_G)) * y).astype(ya_ref.dtype)

    @pl.when(i % blocks_per_seq == blocks_per_seq - 1)
    def _():
        st_ref[0] = up_scr[pl.ds(8 + tm - (CONV_W - 1), CONV_W - 1), :]

    c, s1, s2 = c_ref[...], s1_ref[...], s2_ref[...]
    for j in range((NG_OFF - Q_OFF) // cw):
        qv = chunk(Q_OFF + j * cw)
        for k in range(cw // LANE):
            lo = j * cw + k * LANE
            qh = qv[:, k * LANE:(k + 1) * LANE]
            qc_ref[:, lo:lo + LANE] = (qh * SCALE_LOG2).astype(BF16)
            qr_ref[:, lo:lo + LANE] = (_rope(qh, c, s1, s2) * SCALE_LOG2).astype(BF16)
    for j in range((ZA_WIDTH - NG_OFF) // cw):
        ng_ref[:, j * cw:(j + 1) * cw] = chunk(NG_OFF + j * cw)

    @pl.when(i == pl.num_programs(0) - 1)
    def _():
        hs = _rms_rows(xs_ref, g_ref)
        bgs_ref[...] = _nt(hs, wbg_ref[...])
        for j in range(ZA_WIDTH // cw):
            zs_ref[:, j * cw:(j + 1) * cw] = _nt(hs, w_ref[j * cw:(j + 1) * cw, :])


def _proj_conv(x2, g, w_a, wbg, w_conv, tabs, xs2, *, tm, seq_len):
    m, d = x2.shape
    ms = xs2.shape[0]
    assert seq_len % tm == 0 and CONV_W == 3
    blocks_per_seq = seq_len // tm
    nq, nng = NG_OFF - Q_OFF, ZA_WIDTH - NG_OFF
    resident = lambda shape: pl.BlockSpec(shape, lambda i: (0,) * len(shape), pipeline_mode=pl.Buffered(1))
    row = lambda width: pl.BlockSpec((tm, width), lambda i: (i, 0))
    tspec = pl.BlockSpec((tm, LANE), lambda i: (i % blocks_per_seq, 0))
    return pl.pallas_call(
        functools.partial(_proj_conv_kernel, blocks_per_seq=blocks_per_seq),
        out_shape=[jax.ShapeDtypeStruct((m, nq), BF16), jax.ShapeDtypeStruct((m, nq), BF16),
                   jax.ShapeDtypeStruct((m, nng), F32), jax.ShapeDtypeStruct((m, LANE), F32),
                   jax.ShapeDtypeStruct((m, CONV_DIM), BF16),
                   jax.ShapeDtypeStruct((m // seq_len, CONV_W - 1, CONV_DIM), F32),
                   jax.ShapeDtypeStruct((ms, ZA_WIDTH), F32), jax.ShapeDtypeStruct((ms, LANE), F32)],
        grid=(m // tm,),
        in_specs=[row(d), resident((1, d)), resident(w_a.shape), resident(wbg.shape), resident(w_conv.shape),
                  tspec, tspec, tspec, resident((ms, d))],
        out_specs=[row(nq), row(nq), row(nng), row(LANE), row(CONV_DIM),
                   pl.BlockSpec((1, CONV_W - 1, CONV_DIM), lambda i: (i // blocks_per_seq, 0, 0)),
                   pl.BlockSpec((ms, ZA_WIDTH), lambda i: (0, 0)), pl.BlockSpec((ms, LANE), lambda i: (0, 0))],
        scratch_shapes=[pltpu.VMEM((tm, d), BF16), pltpu.VMEM((tm + 8, CONV_DIM), F32)],
        compiler_params=_params(("arbitrary",)),
        name="proj_conv",
    )(x2, g.reshape(1, d), w_a, wbg, w_conv, *tabs, xs2)


def _proj_kv_kernel(x_ref, g_ref, w_ref, c_ref, s1_ref, s2_ref, a_ref, xs_ref,
                    kvn_ref, kvb_ref, win_ref, pool_ref, zm_ref, zs_ref, h_scr, *, blocks_per_seq):
    tm = x_ref.shape[0]
    cw = 2 * NSA_KV * LANE
    h_scr[...] = _rms_rows(x_ref, g_ref)
    chunk = lambda off: _nt(h_scr[...], w_ref[off:off + cw, :])
    c, s1, s2 = c_ref[...], s1_ref[...], s2_ref[...]
    kv0 = chunk(KV_OFF)
    kv1 = chunk(KV_OFF + cw)
    kv2 = chunk(KV_OFF + 2 * cw)
    n_kv, n_w, half = 4 * NSA_KV, 2 * NSA_KV, NSA_KV * LANE
    last = pl.program_id(0) % blocks_per_seq == blocks_per_seq - 1
    for g in range(NSA_KV):
        lo, hi = g * LANE, (g + 1) * LANE
        ks = _rope(kv1[:, lo:hi], c, s1, s2)
        kw = _rope(kv2[:, lo:hi], c, s1, s2)
        vs = kv1[:, half + lo:half + hi]
        vw = kv2[:, half + lo:half + hi]
        kvn_ref[pl.ds(g, tm, stride=n_kv), :] = kv0[:, lo:hi]
        kvn_ref[pl.ds(NSA_KV + g, tm, stride=n_kv), :] = kv0[:, half + lo:half + hi]
        kvn_ref[pl.ds(2 * NSA_KV + g, tm, stride=n_kv), :] = ks
        kvn_ref[pl.ds(3 * NSA_KV + g, tm, stride=n_kv), :] = vs
        kvb_ref[:, lo:hi] = ks.astype(BF16)
        kvb_ref[:, half + lo:half + hi] = vs.astype(BF16)
        kvb_ref[:, 2 * half + lo:2 * half + hi] = kw.astype(BF16)
        kvb_ref[:, 3 * half + lo:3 * half + hi] = vw.astype(BF16)

        @pl.when(last)
        def _():
            win_ref[pl.ds(g, tm, stride=n_w), :] = kw
            win_ref[pl.ds(NSA_KV + g, tm, stride=n_w), :] = vw

    pool_ref[...] = jnp.sum(kv0.reshape(tm // CMP_BLOCK, CMP_BLOCK, cw) * a_ref[...][None], axis=1)
    for j in range(zm_ref.shape[1] // cw):
        zm_ref[:, j * cw:(j + 1) * cw] = chunk(MQ_OFF + j * cw)

    @pl.when(pl.program_id(0) == pl.num_programs(0) - 1)
    def _():
        hs = _rms_rows(xs_ref, g_ref)
        for j in range(ZB_WIDTH // cw):
            zs_ref[:, j * cw:(j + 1) * cw] = _nt(hs, w_ref[j * cw:(j + 1) * cw, :])


def _proj_kv(x2, g, w_b, tabs, a4, xs2, *, tm, seq_len):
    m, d = x2.shape
    ms = xs2.shape[0]
    assert seq_len % tm == 0
    blocks_per_seq = seq_len // tm
    n_kv, n_w = 4 * NSA_KV, 2 * NSA_KV
    nm = ZB_WIDTH - MQ_OFF
    resident = lambda shape: pl.BlockSpec(shape, lambda i: (0,) * len(shape), pipeline_mode=pl.Buffered(1))
    row = lambda width: pl.BlockSpec((tm, width), lambda i: (i, 0))
    tspec = pl.BlockSpec((tm, LANE), lambda i: (i % blocks_per_seq, 0))
    return pl.pallas_call(
        functools.partial(_proj_kv_kernel, blocks_per_seq=blocks_per_seq),
        out_shape=[jax.ShapeDtypeStruct((m * n_kv, LANE), F32),
                   jax.ShapeDtypeStruct((m, n_kv * LANE), BF16),
                   jax.ShapeDtypeStruct((m // seq_len * tm * n_w, LANE), F32),
                   jax.ShapeDtypeStruct((m // CMP_BLOCK, 2 * NSA_KV * LANE), F32),
                   jax.ShapeDtypeStruct((m, nm), F32), jax.ShapeDtypeStruct((ms, ZB_WIDTH), F32)],
        grid=(m // tm,),
        in_specs=[row(d), resident((1, d)), resident(w_b.shape), tspec, tspec, tspec, resident(a4.shape),
                  resident((ms, d))],
        out_specs=[pl.BlockSpec((tm * n_kv, LANE), lambda i: (i, 0)), row(n_kv * LANE),
                   pl.BlockSpec((tm * n_w, LANE), lambda i: (i // blocks_per_seq, 0)),
                   pl.BlockSpec((tm // CMP_BLOCK, 2 * NSA_KV * LANE), lambda i: (i, 0)), row(nm),
                   pl.BlockSpec((ms, ZB_WIDTH), lambda i: (0, 0))],
        scratch_shapes=[pltpu.VMEM((tm, d), BF16)],
        compiler_params=_params(("arbitrary",)),
        name="proj_kv",
    )(x2, g.reshape(1, d), w_b, *tabs, a4, xs2)


def _cast_rows_kernel(w_ref, o_ref):
    o_ref[...] = w_ref[0].astype(BF16)


def _pack_gate_kernel(w_ref, o_ref):
    o_ref[...] = jnp.zeros(o_ref.shape, BF16)
    o_ref[0:w_ref.shape[1], :] = w_ref[0].astype(BF16)


def _pack_in_proj_weights(w_t3, layer):
    _, n_in, d = w_t3.shape
    tn = 512
    assert n_in == BG_SRC + BG_N + ZB_WIDTH and BG_SRC % tn == 0 and ZB_WIDTH % tn == 0 and BG_N % 8 == 0

    def cast_rows(first_row, n_rows, name):
        return pl.pallas_call(
            _cast_rows_kernel,
            out_shape=jax.ShapeDtypeStruct((n_rows, d), BF16),
            grid=(n_rows // tn,),
            in_specs=[pl.BlockSpec((pl.Element(1), pl.Element(tn), pl.Element(d)),
                                   lambda j: (layer, pl.multiple_of(first_row + j * tn, 8), 0))],
            out_specs=pl.BlockSpec((tn, d), lambda j: (j, 0)),
            compiler_params=_params(("parallel",)),
            name=name,
        )(w_t3)

    wa = cast_rows(0, ZA_WIDTH, "pack_w_a")
    wb = cast_rows(BG_SRC + BG_N, ZB_WIDTH, "pack_w_b")
    wbg = pl.pallas_call(
        _pack_gate_kernel,
        out_shape=jax.ShapeDtypeStruct((LANE, d), BF16),
        grid=(1,),
        in_specs=[pl.BlockSpec((pl.Element(1), pl.Element(BG_N), pl.Element(d)), lambda j: (layer, BG_SRC, 0))],
        out_specs=pl.BlockSpec((LANE, d), lambda j: (0, 0)),
        compiler_params=_params(("arbitrary",)),
        name="pack_w_gate",
    )(w_t3)
    return wa, wb, wbg


def _pool_pages_kernel(pt_sm, a_ref, cache_ref, o_ref, buf, sem, *, pages_per_step, page_base):
    step = pl.program_id(0)
    n_steps = pl.num_programs(0)
    slot = step % 2
    n_cols = buf.shape[3]

    def page_copy(step_idx, p, to_slot):
        page = pt_sm[step_idx * pages_per_step + p] + page_base
        return pltpu.make_async_copy(cache_ref.at[page, :, pl.ds(0, n_cols), :], buf.at[to_slot, p], sem.at[to_slot])

    @pl.when(step == 0)
    def _():
        for p in range(pages_per_step):
            page_copy(0, p, 0).start()

    @pl.when(step + 1 < n_steps)
    def _():
        for p in range(pages_per_step):
            page_copy(step + 1, p, 1 - slot).start()

    for p in range(pages_per_step):
        page_copy(step, p, slot).wait()

    a = a_ref[...]
    per = buf.shape[2] // CMP_BLOCK
    for p in range(pages_per_step):
        for k in range(per):
            x = buf[slot, p, pl.ds(k * CMP_BLOCK, CMP_BLOCK)]
            o_ref[0, p, k] = jnp.sum(x * a, axis=0)


def _pool_pages(cache4, page_flat, a3, *, bs, n_pages, page_base, pages_per_step=16):
    page = cache4.shape[1]
    per = page // CMP_BLOCK
    n_cols = 2 * NSA_KV
    total = bs * n_pages
    pages_per_step = min(pages_per_step, total)
    assert total % pages_per_step == 0
    n_steps = total // pages_per_step
    grid_spec = pltpu.PrefetchScalarGridSpec(
        num_scalar_prefetch=1, grid=(n_steps,),
        in_specs=[pl.BlockSpec((CMP_BLOCK, n_cols, LANE), lambda si, pt: (0, 0, 0)),
                  pl.BlockSpec(memory_space=pl.ANY)],
        out_specs=pl.BlockSpec((1, pages_per_step, per, n_cols, LANE), lambda si, pt: (si, 0, 0, 0, 0)),
        scratch_shapes=[pltpu.VMEM((2, pages_per_step, page, n_cols, LANE), F32),
                        pltpu.SemaphoreType.DMA((2,))])
    out = pl.pallas_call(
        functools.partial(_pool_pages_kernel, pages_per_step=pages_per_step, page_base=page_base),
        out_shape=jax.ShapeDtypeStruct((n_steps, pages_per_step, per, n_cols, LANE), F32),
        grid_spec=grid_spec,
        compiler_params=_params(("arbitrary",)),
        name="pool_pages",
    )(page_flat, a3, cache4)
    return out.reshape(bs, n_pages * per, n_cols * LANE)


def _cmp_proj_kernel(p_ref, w_ref, kc_ref, vc_ref):
    pooled = p_ref[0]
    n = pooled.shape[0]
    n_pad = kc_ref.shape[2]
    for c in range(4):
        r = _nn(pooled[:, c * LANE:(c + 1) * LANE].astype(BF16), w_ref[c]).astype(BF16)
        dst = kc_ref if c < 2 else vc_ref
        if n_pad > n:
            dst[0, c % 2] = jnp.zeros((n_pad, LANE), BF16)
        dst[0, c % 2, 0:n, :] = r


def _cmp_proj(pooled, w4):
    b, n, _ = pooled.shape
    n_pad = -(-n // LANE) * LANE
    spec = pl.BlockSpec((1, NSA_KV, n_pad, LANE), lambda bi: (bi, 0, 0, 0))
    return pl.pallas_call(
        _cmp_proj_kernel,
        out_shape=[jax.ShapeDtypeStruct((b, NSA_KV, n_pad, LANE), BF16)] * 2,
        grid=(b,),
        in_specs=[pl.BlockSpec((1, n, 512), lambda bi: (bi, 0, 0)),
                  pl.BlockSpec((4, LANE, LANE), lambda bi: (0, 0, 0))],
        out_specs=[spec, spec],
        compiler_params=_params(("parallel",)),
        name="cmp_proj",
    )(pooled, w4)


def _lane_parts(x):
    return [x[:, j * LANE:(j + 1) * LANE] for j in range(x.shape[1] // LANE)]


def _nsa_prompt_kernel(*refs, t_len, tq, tc, tw, n_sel, top, sub, pool_pages, page_base):
    if pool_pages:
        (pt_sm, qc_ref, qr_ref, ng_ref, bg_ref, ksel_ref, vsel_ref, kwin_ref, vwin_ref, kc_ref, vc_ref,
         pa_ref, cache_ref, o_ref, pool_ref,
         qc_scr, qr_scr, s_scr, p_scr, a_scr, m_scr, l_scr, acc_scr, oc_scr, b_scr, pbuf, psem) = refs
        step = (pl.program_id(0) * pl.num_programs(1) + pl.program_id(1)) * pl.num_programs(2) + pl.program_id(2)
        n_steps = pl.num_programs(0) * pl.num_programs(1) * pl.num_programs(2)
        slot = step % 2
        half_rows, n_cols = pbuf.shape[2], 2 * NSA_KV

        def page_copies(step_idx, p, to_slot):
            page = pt_sm[step_idx * pool_pages + p] + page_base
            return [pltpu.make_async_copy(cache_ref.at[page, pl.ds(hh * half_rows, half_rows), pl.ds(0, n_cols), :],
                                          pbuf.at[to_slot, p, :, pl.ds(hh * n_cols, n_cols), :], psem.at[to_slot])
                    for hh in range(2)]

        @pl.when(step == 0)
        def _():
            for p in range(pool_pages):
                for cp in page_copies(0, p, 0):
                    cp.start()

        @pl.when(step + 1 < n_steps)
        def _():
            for p in range(pool_pages):
                for cp in page_copies(step + 1, p, 1 - slot):
                    cp.start()
    else:
        (qc_ref, qr_ref, ng_ref, bg_ref, ksel_ref, vsel_ref, kwin_ref, vwin_ref, kc_ref, vc_ref,
         o_ref, qc_scr, qr_scr, s_scr, p_scr, a_scr, m_scr, l_scr, acc_scr, oc_scr, b_scr) = refs
    i = pl.program_id(2)
    hq = NSA_HPG
    for h in range(hq):
        qc_scr[pl.ds(h * tq, tq), :] = qc_ref[0, :, h * LANE:(h + 1) * LANE]
        qr_scr[pl.ds(h * tq, tq), :] = qr_ref[0, :, h * LANE:(h + 1) * LANE]

    kc = kc_ref[0, 0]
    npad = kc.shape[0]
    s_scr[:, 0:npad] = _nt(qc_scr[...], kc)
    tpos = i * tq + lax.broadcasted_iota(jnp.int32, (tq, npad), 0)
    ncol = lax.broadcasted_iota(jnp.int32, (tq, npad), 1)
    cmask = (ncol + 1) * CMP_BLOCK <= tpos + 1
    imp = jnp.zeros((tq, npad), F32)
    for h in range(hq):
        r = pl.ds(h * tq, tq)
        s = jnp.where(cmask, s_scr[r, 0:npad], MASK_NEG)
        e = jnp.where(cmask, jnp.exp2(s - jnp.max(s, axis=-1, keepdims=True)), 0.0)
        p = e / jnp.maximum(jnp.sum(e, axis=-1, keepdims=True), 1e-30)
        p_scr[r, 0:npad] = p.astype(BF16)
        imp = imp + p
    oc_scr[...] = _nn(p_scr[:, 0:npad], vc_ref[0, 0])

    rows = min(npad, -(-n_sel // 8) * 8)
    imp_t = imp.T[0:rows]
    blk = lax.broadcasted_iota(jnp.int32, (rows, tq), 0)
    cur = (i * tq + lax.broadcasted_iota(jnp.int32, (rows, tq), 1)) // SEL_BLOCK
    imp_t = jnp.where((blk == 0) | (blk == cur) | (blk == cur - 1), FORCE, imp_t)
    imp_t = jnp.where(blk > cur, -1.0, imp_t)
    imp_t = jnp.where(blk >= n_sel, -2.0, imp_t)
    rank = jnp.zeros((rows, tq), F32)
    for j in range(n_sel):
        a = imp_t[j:j + 1, :]
        ahead = (a > imp_t) | ((a == imp_t) & (blk > j))
        rank = rank + jnp.where(ahead, 1.0, 0.0)
    sel_t = jnp.where((rank < top) & (blk < n_sel), 1.0, 0.0)
    if npad > rows:
        sel_t = jnp.concatenate([sel_t, jnp.zeros((npad - rows, tq), F32)], axis=0)
    sel = sel_t.T.astype(BF16)

    m_scr[...] = jnp.full(m_scr.shape, MASK_NEG, F32)
    l_scr[...] = jnp.zeros(l_scr.shape, F32)
    acc_scr[...] = jnp.zeros(acc_scr.shape, F32)
    n_chunks = (i * tq + tq + tc - 1) // tc

    def chunk(ci, carry):
        k0 = pl.multiple_of(ci * tc, tc)
        k = ksel_ref[0, pl.ds(k0, tc), :]
        v = vsel_ref[0, pl.ds(k0, tc), :]
        jb = lax.broadcasted_iota(jnp.int32, (npad, tc), 0)
        kb = (k0 + lax.broadcasted_iota(jnp.int32, (npad, tc), 1)) // SEL_BLOCK
        expand = jnp.where(jb == kb, 1.0, 0.0).astype(BF16)
        chosen = _nn(sel, expand)
        kp = k0 + lax.broadcasted_iota(jnp.int32, (tq, tc), 1)
        tp = i * tq + lax.broadcasted_iota(jnp.int32, (tq, tc), 0)
        b_scr[:, 0:tc] = jnp.where((chosen > 0.5) & (kp <= tp), 0.0, MASK_NEG)
        s_scr[:, 0:tc] = _nt(qr_scr[...], k)
        for r0 in range(0, hq * tq, sub):
            r = pl.ds(r0, sub)
            parts = _lane_parts(s_scr[r, 0:tc] + b_scr[pl.ds(r0 % tq, sub), 0:tc])
            m_prev = m_scr[r, :]
            m_new = jnp.maximum(m_prev, jnp.max(functools.reduce(jnp.maximum, parts), axis=-1, keepdims=True))
            alpha = jnp.exp2(m_prev - m_new)
            pes = [jnp.exp2(x - m_new) for x in parts]
            l_scr[r, :] = alpha * l_scr[r, :] + functools.reduce(jnp.add, pes)
            p_scr[r, 0:tc] = jnp.concatenate(pes, axis=1).astype(BF16)
            a_scr[r, :] = alpha
            m_scr[r, :] = m_new
        acc_scr[...] = a_scr[...] * acc_scr[...] + _nn(p_scr[:, 0:tc], v)
        return carry

    lax.fori_loop(0, n_chunks, chunk, 0)

    w0 = pl.multiple_of(jnp.clip(i * tq + tq - tw, 0, t_len - tw), LANE)
    kp = w0 + lax.broadcasted_iota(jnp.int32, (tq, tw), 1)
    tp = i * tq + lax.broadcasted_iota(jnp.int32, (tq, tw), 0)
    b_scr[:, 0:tw] = jnp.where((kp <= tp) & (kp > tp - WINDOW), 0.0, MASK_NEG)
    s_scr[:, 0:tw] = _nt(qr_scr[...], kwin_ref[0, pl.ds(w0, tw), :])
    for r0 in range(0, hq * tq, sub):
        r = pl.ds(r0, sub)
        parts = _lane_parts(s_scr[r, 0:tw] + b_scr[pl.ds(r0 % tq, sub), 0:tw])
        m = jnp.max(functools.reduce(jnp.maximum, parts), axis=-1, keepdims=True)
        ews = [jnp.exp2(x - m) for x in parts]
        a_scr[r, :] = jnp.broadcast_to(jnp.sum(functools.reduce(jnp.add, ews), axis=-1, keepdims=True), (sub, LANE))
        p_scr[r, 0:tw] = jnp.concatenate(ews, axis=1).astype(BF16)
    o_w = _nn(p_scr[:, 0:tw], vwin_ref[0, pl.ds(w0, tw), :])

    gate = jax.nn.sigmoid(bg_ref[0])
    for g in range(1, NSA_KV):
        gate = jnp.where(pl.program_id(1) == g, pltpu.roll(gate, LANE - g * hq * N_BRANCH, 1), gate)
    ng = ng_ref[0]
    for h in range(hq):
        r = slice(h * tq, (h + 1) * tq)
        o_s = acc_scr[r, :] / jnp.sum(l_scr[r, :], axis=-1, keepdims=True)
        o = (gate[:, 3 * h:3 * h + 1] * oc_scr[r, :] + gate[:, 3 * h + 1:3 * h + 2] * o_s
             + gate[:, 3 * h + 2:3 * h + 3] * (o_w[r] / a_scr[r, :]))
        o_ref[0, :, h * LANE:(h + 1) * LANE] = (_silu(ng[:, h * LANE:(h + 1) * LANE]) * o).astype(o_ref.dtype)

    if pool_pages:
        for p in range(pool_pages):
            for cp in page_copies(step, p, slot):
                cp.wait()
        pa = pa_ref[...]
        for p in range(pool_pages):
            pool_ref[0, p] = jnp.sum(pbuf[slot, p] * pa, axis=0)


def _nsa_prompt(qc3, qr3, ng3, bg3, kvb, kc, vc, *, tq=128, tc=512, sub=64, pool=None):
    b, t, _ = qc3.shape
    nq = t // tq
    tc = min(tc, t)
    tw = min(WINDOW + tq, t)
    n_sel = t // SEL_BLOCK
    top = min(TOP_N, n_sel)
    npad = kc.shape[2]
    gw = NSA_HPG * LANE
    rows = NSA_HPG * tq
    wide = max(tc, tw, npad)
    kvspec = lambda k: pl.BlockSpec((1, t, LANE), lambda bi, gi, qi, *_, k=k: (bi, 0, k + gi))
    cspec = pl.BlockSpec((1, 1, npad, LANE), lambda bi, gi, qi, *_: (bi, gi, 0, 0))
    gspec = pl.BlockSpec((1, tq, gw), lambda bi, gi, qi, *_: (bi, qi, gi))
    in_specs = [gspec, gspec, gspec,
                pl.BlockSpec((1, tq, LANE), lambda bi, gi, qi, *_: (bi, qi, 0)),
                kvspec(0), kvspec(2), kvspec(4), kvspec(6), cspec, cspec]
    out_shape = [jax.ShapeDtypeStruct((b, t, NSA_HEADS * LANE), BF16)]
    out_specs = [pl.BlockSpec((1, tq, gw), lambda bi, gi, qi, *_: (bi, qi, gi))]
    scratch = ([pltpu.VMEM((rows, LANE), BF16)] * 2
               + [pltpu.VMEM((rows, wide), F32), pltpu.VMEM((rows, wide), BF16)]
               + [pltpu.VMEM((rows, LANE), F32)] * 5 + [pltpu.VMEM((tq, wide), F32)])
    args = [qc3, qr3, ng3, bg3, kvb, kvb, kvb, kvb, kc, vc]
    n_steps = b * NSA_KV * nq
    pool_pages, page_base, prefetch = 0, 0, []
    if pool is not None:
        cache4, page_flat, a3, page_base = pool
        page, n_cols = cache4.shape[1], 2 * NSA_KV
        if page_flat.shape[0] % n_steps == 0 and page == 2 * CMP_BLOCK and 2 * n_cols == 8:
            pool_pages = page_flat.shape[0] // n_steps
            prefetch = [page_flat]
            in_specs += [pl.BlockSpec((CMP_BLOCK, 2 * n_cols, LANE), lambda bi, gi, qi, *_: (0, 0, 0)),
                         pl.BlockSpec(memory_space=pl.ANY)]
            args += [jnp.concatenate([a3, a3], axis=1), cache4]
            out_shape.append(jax.ShapeDtypeStruct((n_steps, pool_pages, 2 * n_cols, LANE), F32))
            out_specs.append(pl.BlockSpec((1, pool_pages, 2 * n_cols, LANE),
                                          lambda bi, gi, qi, *_: ((bi * NSA_KV + gi) * nq + qi, 0, 0, 0)))
            scratch += [pltpu.VMEM((2, pool_pages, CMP_BLOCK, 2 * n_cols, LANE), F32), pltpu.SemaphoreType.DMA((2,))]
    kern = functools.partial(_nsa_prompt_kernel, t_len=t, tq=tq, tc=tc, tw=tw, n_sel=n_sel, top=top,
                             sub=min(sub, tq), pool_pages=pool_pages, page_base=page_base)
    res = pl.pallas_call(
        kern,
        out_shape=out_shape,
        grid_spec=pltpu.PrefetchScalarGridSpec(
            num_scalar_prefetch=len(prefetch), grid=(b, NSA_KV, nq),
            in_specs=in_specs, out_specs=out_specs, scratch_shapes=scratch),
        compiler_params=_params(("arbitrary", "arbitrary", "arbitrary")),
        name="nsa_prompt",
    )(*prefetch, *args)
    return (res[0], res[1]) if pool_pages else (res[0], None)


def _conv_kernel(h_ref, b_ref, c_ref, g_ref, prev_ref, w_ref, y_ref, st_ref, up_scr, *, t_real):
    u = c_ref[...] * h_ref[...]
    t = u.shape[1]
    up_scr[:, pl.ds(8 - (CONV_W - 1), CONV_W - 1), :] = prev_ref[...]
    up_scr[:, pl.ds(8, t), :] = u
    w = w_ref[...]
    y = w[0:1, :][None] * up_scr[:, pl.ds(6, t), :]
    y = y + w[1:2, :][None] * up_scr[:, pl.ds(7, t), :]
    y = y + w[2:3, :][None] * u
    y = b_ref[...] * y
    y_ref[...] = (_silu(g_ref[...]) * y).astype(y_ref.dtype)
    st_ref[...] = up_scr[:, pl.ds(6 + t_real, CONV_W - 1), :]


def _conv_mixer(z3, prev, w_conv, *, t_real):
    b, t, _ = z3.shape
    nc = CONV_DIM // LANE
    zspec = lambda off: pl.BlockSpec((b, t, LANE), lambda ci, off=off: (0, 0, off // LANE + ci))
    return pl.pallas_call(
        functools.partial(_conv_kernel, t_real=t_real),
        out_shape=[jax.ShapeDtypeStruct((b, t, CONV_DIM), BF16),
                   jax.ShapeDtypeStruct((b, CONV_W - 1, CONV_DIM), F32)],
        grid=(nc,),
        in_specs=[zspec(C_H), zspec(C_B), zspec(C_C), zspec(C_G),
                  pl.BlockSpec((b, CONV_W - 1, LANE), lambda ci: (0, 0, ci)),
                  pl.BlockSpec((CONV_W, LANE), lambda ci: (0, ci))],
        out_specs=[pl.BlockSpec((b, t, LANE), lambda ci: (0, 0, ci)),
                   pl.BlockSpec((b, CONV_W - 1, LANE), lambda ci: (0, 0, ci))],
        scratch_shapes=[pltpu.VMEM((b, t + 8, LANE), F32)],
        compiler_params=_params(("parallel",)),
        name="conv_mixer",
    )(z3, z3, z3, z3, prev, w_conv)


def _mem_attn_kernel(q_ref, mg_ref, kv_ref, o_ref, *, interleaved):
    q = q_ref[0]
    mg = mg_ref[0]
    half = MEM_HEADS * LANE
    for h in range(MEM_HEADS):
        lo, hi = h * LANE, (h + 1) * LANE
        if interleaved:
            nm = kv_ref.shape[1] // (2 * MEM_HEADS)
            k = kv_ref[0, pl.ds(h, nm, stride=2 * MEM_HEADS), :].astype(BF16)
            v = kv_ref[0, pl.ds(MEM_HEADS + h, nm, stride=2 * MEM_HEADS), :].astype(BF16)
        else:
            k = kv_ref[0, :, lo:hi].astype(BF16)
            v = kv_ref[0, :, half + lo:half + hi].astype(BF16)
        s = _nt((q[:, lo:hi] * ATTN_SCALE).astype(BF16), k)
        e = jnp.exp(s - jnp.max(s, axis=-1, keepdims=True))
        o = _nn(e.astype(BF16), v) / jnp.sum(e, axis=-1, keepdims=True)
        o_ref[0, :, lo:hi] = (_silu(mg[:, lo:hi]) * o).astype(o_ref.dtype)


def _mem_attn(zb3, mkv, *, mq_off, mg_off, tq, interleaved, kv_base=0):
    b, t, _ = zb3.shape
    wq = MEM_HEADS * LANE
    return pl.pallas_call(
        functools.partial(_mem_attn_kernel, interleaved=interleaved),
        out_shape=jax.ShapeDtypeStruct((b, t, wq), BF16),
        grid=(b, t // tq),
        in_specs=[pl.BlockSpec((1, tq, wq), lambda bi, ti: (bi, ti, mq_off // wq)),
                  pl.BlockSpec((1, tq, wq), lambda bi, ti: (bi, ti, mg_off // wq)),
                  pl.BlockSpec((1,) + mkv.shape[1:], lambda bi, ti: (bi + kv_base, 0, 0))],
        out_specs=pl.BlockSpec((1, tq, wq), lambda bi, ti: (bi, ti, 0)),
        compiler_params=_params(("parallel", "parallel")),
        name="mem_attn",
    )(zb3, zb3, mkv)


def _out_proj_kernel(x_ref, ya_ref, yb_ref, ym_ref, w_ref, fg_ref, o_ref, *, final):
    a, bw = CONV_DIM, CONV_DIM + NSA_HEADS * LANE
    acc = _nn(ya_ref[...], w_ref[0:a, :])
    acc = acc + _nn(yb_ref[...], w_ref[a:bw, :])
    acc = acc + _nn(ym_ref[...], w_ref[bw:, :])
    r = x_ref[...] + acc
    if final:
        r = r * lax.rsqrt(jnp.mean(r * r, axis=-1, keepdims=True) + NORM_EPS) * fg_ref[...]
    o_ref[...] = r


def _out_proj(x, ya, yb, ym, w, fg, *, tm, final):
    m, d = x.shape
    row = lambda width: pl.BlockSpec((tm, width), lambda i: (i, 0))
    return pl.pallas_call(
        functools.partial(_out_proj_kernel, final=final),
        out_shape=jax.ShapeDtypeStruct((m, d), F32),
        grid=(m // tm,),
        in_specs=[row(d), row(ya.shape[1]), row(yb.shape[1]), row(ym.shape[1]),
                  pl.BlockSpec(w.shape, lambda i: (0, 0), pipeline_mode=pl.Buffered(1)),
                  pl.BlockSpec((1, d), lambda i: (0, 0), pipeline_mode=pl.Buffered(1))],
        out_specs=row(d),
        compiler_params=_params(("parallel",)),
        name="out_proj",
    )(x, ya, yb, ym, w, fg.reshape(1, d))


def _sample_pre_kernel(q_ref, ng_ref, bg_ref, kv0_ref, kv1_ref, kv2_ref, c_ref, s1_ref, s2_ref, kc_ref, vc_ref,
                       kvn_ref, wn_ref, qr_ref, oc_ref, ngo_ref, gate_ref, val_ref, *, past):
    tp = SAMPLE_T_PAD
    hq = NSA_HPG
    c, s1, s2 = c_ref[...], s1_ref[...], s2_ref[...]
    kv0, kv1, kv2 = kv0_ref[0], kv1_ref[0], kv2_ref[0]
    kvn_ref[0, :, 0:512] = kv0
    kvn_ref[0, :, 768:1024] = kv1[:, 256:512]
    wn_ref[0, :, 256:512] = kv2[:, 256:512]
    for g in range(NSA_KV):
        lo, hi = g * LANE, (g + 1) * LANE
        kvn_ref[0, :, 512 + lo:512 + hi] = _rope(kv1[:, lo:hi], c, s1, s2)
        wn_ref[0, :, lo:hi] = _rope(kv2[:, lo:hi], c, s1, s2)

    q = q_ref[0]
    ng = ng_ref[0]
    gates = jax.nn.sigmoid(bg_ref[0])
    for g in range(NSA_KV):
        qc_l, qr_l = [], []
        for h in range(hq):
            lo = (g * hq + h) * LANE
            qh = q[:, lo:lo + LANE]
            qc_l.append(qh * ATTN_SCALE)
            qr_l.append(_rope(qh, c, s1, s2) * ATTN_SCALE)
            ngo_ref[0, g, h * tp:(h + 1) * tp, :] = ng[:, lo:lo + LANE]
            for br in range(N_BRANCH):
                col = (g * hq + h) * N_BRANCH + br
                gate_ref[0, g, br, h * tp:(h + 1) * tp, :] = jnp.broadcast_to(gates[:, col:col + 1], (tp, LANE))
        qc = jnp.concatenate(qc_l, axis=0)
        qr_ref[0, g] = jnp.concatenate(qr_l, axis=0)

        kc = kc_ref[0, g]
        npad = kc.shape[0]
        s = _nt(qc.astype(BF16), kc)
        trow = lax.broadcasted_iota(jnp.int32, (hq * tp, npad), 0) % tp
        ncol = lax.broadcasted_iota(jnp.int32, (hq * tp, npad), 1)
        cmask = (ncol + 1) * CMP_BLOCK <= past + trow + 1
        s = jnp.where(cmask, s, MASK_NEG)
        e = jnp.where(cmask, jnp.exp(s - jnp.max(s, axis=-1, keepdims=True)), 0.0)
        p = e / jnp.maximum(jnp.sum(e, axis=-1, keepdims=True), 1e-30)
        oc_ref[0, g] = _nn(p.astype(BF16), vc_ref[0, g])
        imp = jnp.sum(p.reshape(hq, tp, npad), axis=0)

        blk = lax.broadcasted_iota(jnp.int32, (tp, npad), 1)
        cur = (past + lax.broadcasted_iota(jnp.int32, (tp, npad), 0)) // SEL_BLOCK
        val = jnp.where((blk == 0) | (blk == cur) | (blk == cur - 1), FORCE, imp)
        val = jnp.where(blk > cur, -1.0, val)
        val = jnp.where(blk >= past // SEL_BLOCK, -2.0, val)
        val_ref[0, g] = val


def _topk_kernel(val_ref, idx_ref, *, n_top):
    val = val_ref[...]
    rows, n = val.shape
    blk = lax.broadcasted_iota(jnp.int32, (rows, n), 1)
    lane = lax.broadcasted_iota(jnp.int32, (rows, LANE), 1)
    idx = jnp.zeros((rows, LANE), jnp.int32)
    for r in range(n_top):
        best = jnp.max(val, axis=-1, keepdims=True)
        j = jnp.min(jnp.where(val == best, blk, n), axis=-1, keepdims=True)
        idx = jnp.where(lane == r, j, idx)
        val = jnp.where(blk == j, -3e38, val)
    idx_ref[...] = idx


def _topk(val2, *, n_top):
    rows, n = val2.shape
    return pl.pallas_call(
        functools.partial(_topk_kernel, n_top=n_top),
        out_shape=jax.ShapeDtypeStruct((rows, LANE), jnp.int32),
        grid=(1,),
        in_specs=[pl.BlockSpec((rows, n), lambda i: (0, 0))],
        out_specs=pl.BlockSpec((rows, LANE), lambda i: (0, 0)),
        compiler_params=_params(("arbitrary",)),
        name="sample_topk",
    )(val2)


def _sample_pre(za3, zb3, bg3, tabs, kc, vc, *, past):
    bs, tp, _ = za3.shape
    npad = kc.shape[2]
    qw = NSA_HEADS * LANE
    kvblk = KV_OFF // 512
    zspec = lambda k: pl.BlockSpec((1, tp, 512), lambda bi, k=k: (bi, 0, kvblk + k))
    tspec = pl.BlockSpec((tp, LANE), lambda bi: (0, 0))
    cspec = pl.BlockSpec((1, NSA_KV, npad, LANE), lambda bi: (bi, 0, 0, 0))
    rows = NSA_HPG * tp
    gspec = pl.BlockSpec((1, NSA_KV, rows, LANE), lambda bi: (bi, 0, 0, 0))
    gshape = jax.ShapeDtypeStruct((bs, NSA_KV, rows, LANE), F32)
    return pl.pallas_call(
        functools.partial(_sample_pre_kernel, past=past),
        out_shape=[jax.ShapeDtypeStruct((bs, tp, 1024), F32),
                   jax.ShapeDtypeStruct((bs, tp, 512), F32),
                   gshape, gshape, gshape,
                   jax.ShapeDtypeStruct((bs, NSA_KV, N_BRANCH, rows, LANE), F32),
                   jax.ShapeDtypeStruct((bs, NSA_KV, tp, npad), F32)],
        grid=(bs,),
        in_specs=[pl.BlockSpec((1, tp, qw), lambda bi: (bi, 0, Q_OFF // qw)),
                  pl.BlockSpec((1, tp, qw), lambda bi: (bi, 0, NG_OFF // qw)),
                  pl.BlockSpec((1, tp, LANE), lambda bi: (bi, 0, 0)),
                  zspec(0), zspec(1), zspec(2), tspec, tspec, tspec, cspec, cspec],
        out_specs=[pl.BlockSpec((1, tp, 1024), lambda bi: (bi, 0, 0)),
                   pl.BlockSpec((1, tp, 512), lambda bi: (bi, 0, 0)),
                   gspec, gspec, gspec,
                   pl.BlockSpec((1, NSA_KV, N_BRANCH, rows, LANE), lambda bi: (bi, 0, 0, 0, 0)),
                   pl.BlockSpec((1, NSA_KV, tp, npad), lambda bi: (bi, 0, 0, 0))],
        compiler_params=_params(("parallel",)),
        name="sample_pre",
    )(za3, za3, bg3, zb3, zb3, zb3, *tabs, kc, vc)


def _sample_attn_kernel(idx_sm, pt_sm, qr_ref, oc_ref, ng_ref, gate_ref, ksn_ref, vsn_ref,
                        wc_ref, kwn_ref, vwn_ref, cache_ref, o_ref,
                        kbuf, vbuf, kw_scr, vw_scr, sem, *, ts, n_top, n_pages, page_base, per_page, wb):
    tp = SAMPLE_T_PAD
    hq = NSA_HPG
    b = pl.program_id(0)
    g = pl.program_id(1)
    n_gath = n_top * SEL_BLOCK
    ks_rows = kbuf.shape[1]

    def gather_copies(t, r):
        blk = idx_sm[((b * NSA_KV + g) * ts + t) * n_top + r]
        page = pt_sm[b * n_pages + blk // per_page] + page_base
        row0 = (blk % per_page) * SEL_BLOCK
        src_k = cache_ref.at[page, pl.ds(row0, SEL_BLOCK), 2 * NSA_KV + g]
        src_v = cache_ref.at[page, pl.ds(row0, SEL_BLOCK), 3 * NSA_KV + g]
        dst = pl.ds(r * SEL_BLOCK, SEL_BLOCK)
        return (pltpu.make_async_copy(src_k, kbuf.at[t, dst], sem.at[0]),
                pltpu.make_async_copy(src_v, vbuf.at[t, dst], sem.at[1]))

    for t in range(ts):
        for r in range(n_top):
            ck, cv = gather_copies(t, r)
            ck.start()
            cv.start()

    qr = qr_ref[0, 0].astype(BF16)
    trow = lax.broadcasted_iota(jnp.int32, (hq * tp, 1), 0) % tp

    ww = kw_scr.shape[0]
    kw_scr[pl.ds(0, wb), :] = wc_ref[0, pl.ds(g, wb, stride=2 * NSA_KV), :]
    vw_scr[pl.ds(0, wb), :] = wc_ref[0, pl.ds(NSA_KV + g, wb, stride=2 * NSA_KV), :]
    kw_scr[pl.ds(wb, tp), :] = kwn_ref[0]
    vw_scr[pl.ds(wb, tp), :] = vwn_ref[0]
    kw_scr[pl.ds(wb + tp, ww - wb - tp), :] = jnp.zeros((ww - wb - tp, LANE), F32)
    vw_scr[pl.ds(wb + tp, ww - wb - tp), :] = jnp.zeros((ww - wb - tp, LANE), F32)
    sw = _nt(qr, kw_scr[...].astype(BF16))
    jw = lax.broadcasted_iota(jnp.int32, (hq * tp, ww), 1)
    rel = jw - wb
    okw = (rel <= trow) & (rel > trow - WINDOW) & (jw < wb + ts)
    sw = jnp.where(okw, sw, MASK_NEG)
    ew = jnp.where(okw, jnp.exp(sw - jnp.max(sw, axis=-1, keepdims=True)), 0.0)
    o_w = _nn(ew.astype(BF16), vw_scr[...].astype(BF16)) / jnp.sum(ew, axis=-1, keepdims=True)

    for t in range(ts):
        for r in range(n_top):
            ck, cv = gather_copies(t, r)
            ck.wait()
            cv.wait()

    js = lax.broadcasted_iota(jnp.int32, (hq * tp, ks_rows), 1)
    o_s = jnp.zeros((hq * tp, LANE), F32)
    for t in range(ts):
        kbuf[t, pl.ds(n_gath, tp), :] = ksn_ref[0]
        vbuf[t, pl.ds(n_gath, tp), :] = vsn_ref[0]
        kbuf[t, pl.ds(n_gath + tp, ks_rows - n_gath - tp), :] = jnp.zeros((ks_rows - n_gath - tp, LANE), F32)
        vbuf[t, pl.ds(n_gath + tp, ks_rows - n_gath - tp), :] = jnp.zeros((ks_rows - n_gath - tp, LANE), F32)
        ss = _nt(qr, kbuf[t].astype(BF16))
        oks = (js < n_gath) | ((js - n_gath <= t) & (js < n_gath + ts))
        ss = jnp.where(oks, ss, MASK_NEG)
        es = jnp.where(oks, jnp.exp(ss - jnp.max(ss, axis=-1, keepdims=True)), 0.0)
        ot = _nn(es.astype(BF16), vbuf[t].astype(BF16)) / jnp.sum(es, axis=-1, keepdims=True)
        o_s = jnp.where(trow == t, ot, o_s)

    o = gate_ref[0, 0, 0] * oc_ref[0, 0] + gate_ref[0, 0, 1] * o_s + gate_ref[0, 0, 2] * o_w
    y = _silu(ng_ref[0, 0]) * o
    for h in range(hq):
        o_ref[0, :, h * LANE:(h + 1) * LANE] = y[h * tp:(h + 1) * tp].astype(o_ref.dtype)


def _sample_attn(idx_flat, page_flat, qr, oc, ngo, gates, kvn, cache_win_rows, wn, cache4, *,
                 ts, n_top, n_pages, page_base, win_base):
    bs = qr.shape[0]
    tp = SAMPLE_T_PAD
    rows = NSA_HPG * tp
    wb = cache_win_rows.shape[1] // (2 * NSA_KV)
    per_page = cache4.shape[1] // SEL_BLOCK
    ks_rows = -(-(n_top * SEL_BLOCK + tp) // LANE) * LANE
    ww = -(-(wb + tp) // LANE) * LANE
    gspec = pl.BlockSpec((1, 1, rows, LANE), lambda bi, gi, *_: (bi, gi, 0, 0))
    newspec = lambda k: pl.BlockSpec((1, tp, LANE), lambda bi, gi, *_, k=k: (bi, 0, k + gi))
    grid_spec = pltpu.PrefetchScalarGridSpec(
        num_scalar_prefetch=2, grid=(bs, NSA_KV),
        in_specs=[gspec, gspec, gspec,
                  pl.BlockSpec((1, 1, N_BRANCH, rows, LANE), lambda bi, gi, *_: (bi, gi, 0, 0, 0)),
                  newspec(2 * NSA_KV), newspec(3 * NSA_KV),
                  pl.BlockSpec((1,) + cache_win_rows.shape[1:], lambda bi, gi, *_: (bi + win_base, 0, 0)),
                  newspec(0), newspec(NSA_KV),
                  pl.BlockSpec(memory_space=pl.ANY)],
        out_specs=pl.BlockSpec((1, tp, NSA_HPG * LANE), lambda bi, gi, *_: (bi, 0, gi)),
        scratch_shapes=[pltpu.VMEM((ts, ks_rows, LANE), F32), pltpu.VMEM((ts, ks_rows, LANE), F32),
                        pltpu.VMEM((ww, LANE), F32), pltpu.VMEM((ww, LANE), F32),
                        pltpu.SemaphoreType.DMA((2,))])
    kern = functools.partial(_sample_attn_kernel, ts=ts, n_top=n_top, n_pages=n_pages,
                             page_base=page_base, per_page=per_page, wb=wb)
    return pl.pallas_call(
        kern,
        out_shape=jax.ShapeDtypeStruct((bs, tp, NSA_HEADS * LANE), BF16),
        grid_spec=grid_spec,
        compiler_params=_params(("arbitrary", "arbitrary")),
        name="sample_attn",
    )(idx_flat, page_flat, qr, oc, ngo, gates, kvn, kvn, cache_win_rows, wn, wn, cache4)


def _rope_tables(pos, rows):
    freqs = jnp.power(ROPE_THETA, -jnp.arange(ROPE_HALF, dtype=F32) * (2.0 / ROPE_DIM))
    ang = pos.astype(F32)[:, None] * freqs[None, :]
    cos, sin = jnp.cos(ang), jnp.sin(ang)
    n = pos.shape[0]
    z16 = jnp.zeros((n, ROPE_HALF), F32)
    rest = LANE - ROPE_DIM
    c = jnp.concatenate([cos, cos, jnp.ones((n, rest), F32)], axis=1)
    s1 = jnp.concatenate([z16, sin, jnp.zeros((n, rest), F32)], axis=1)
    s2 = jnp.concatenate([-sin, z16, jnp.zeros((n, rest), F32)], axis=1)
    pad = lambda a: jnp.pad(a, ((0, rows - n), (0, 0)))
    return pad(c), pad(s1), pad(s2)


def _layer_weights(norm_g, w_in3, layer, w_conv, a_cmp, w_cmp, w_out):
    w_a, w_b, wbg = _pack_in_proj_weights(jnp.swapaxes(w_in3, 1, 2), layer)
    a4 = jnp.concatenate([a_cmp[0], a_cmp[0], a_cmp[1], a_cmp[1]], axis=1)
    a3 = jnp.stack([a_cmp[0], a_cmp[0], a_cmp[1], a_cmp[1]], axis=1)
    w4 = jnp.stack([w_cmp[0], w_cmp[0], w_cmp[1], w_cmp[1]]).astype(BF16)
    return dict(norm_g=norm_g, w_a=w_a, w_b=w_b, wbg=wbg, w_conv=w_conv, a4=a4, a3=a3, w4=w4,
                w_out=w_out.astype(BF16))


def _prompt_layer(xp, xs2, mem_prompt, mem_norm_g, w_mem, lw, final_g, final, pool):
    b, t, d = xp.shape
    m = b * t
    x2 = xp.reshape(m, d)
    tabs = _rope_tables(jnp.arange(t, dtype=jnp.int32), t)
    wb = min(WINDOW, t)
    qc, qr, ng, bg, ya, conv_new, za_s, bg_s = _proj_conv(x2, lw["norm_g"], lw["w_a"], lw["wbg"], lw["w_conv"], tabs,
                                                          xs2, tm=min(512, t), seq_len=t)
    kvn, kvb, win_rows, pooled_prompt, zm, zb_s = _proj_kv(x2, lw["norm_g"], lw["w_b"], tabs, lw["a4"], xs2,
                                                           tm=wb, seq_len=t)
    nm = mem_prompt.shape[1]
    mkv = _norm_matmul(mem_prompt.reshape(b * nm, d), mem_norm_g, w_mem, tm=min(512, b * nm), tn=512)
    mkv3 = mkv.reshape(b, nm, 2 * MEM_HEADS * LANE)
    kc, vc = _cmp_proj(pooled_prompt.reshape(b, t // CMP_BLOCK, -1), lw["w4"])
    r3 = lambda a: a.reshape(b, t, -1)
    yb, pooled = _nsa_prompt(r3(qc), r3(qr), r3(ng), r3(bg), r3(kvb), kc, vc, tq=min(256, t), pool=pool)
    ym = _mem_attn(zm.reshape(b, t, -1), mkv3, mq_off=0, mg_off=MG_OFF - MQ_OFF, tq=min(512, t), interleaved=False)
    out = _out_proj(x2, ya, yb.reshape(m, -1), ym.reshape(m, -1),
                    lw["w_out"], final_g, tm=min(512, m), final=final)
    kv_new = kvn.reshape(b, t, 4, NSA_KV, HEAD_DIM)
    win_new = win_rows.reshape(b, wb, 2, NSA_KV, HEAD_DIM)
    mem_kv = mkv.reshape(b, nm, 2, MEM_HEADS, HEAD_DIM)
    return out.reshape(b, t, d), kv_new, win_new, conv_new, mem_kv, pooled, (za_s, zb_s, bg_s)


def _sample_layer(xs_p, proj, ts, layer, cache4, page_flat, pooled, cache_win, state_conv, cache_mem, lw, final_g,
                  final):
    bs, tp, d = xs_p.shape
    depth = cache_win.shape[0]
    pool, page = cache4.shape[0] // depth, cache4.shape[1]
    n_pages = page_flat.shape[0] // bs
    past = n_pages * page
    assert past % SEL_BLOCK == 0 and ts <= SEL_BLOCK and ts <= tp
    n_past = past // SEL_BLOCK
    n_top = min(TOP_N, n_past + 1) - 1
    m = bs * tp
    za, zb, bg = proj
    za3 = za.reshape(bs, tp, ZA_WIDTH)
    zb3 = zb.reshape(bs, tp, ZB_WIDTH)
    bg3 = bg.reshape(bs, tp, LANE)
    tabs = _rope_tables(past + jnp.arange(tp, dtype=jnp.int32), tp)
    if pooled is None:
        pooled = _pool_pages(cache4, page_flat, lw["a3"], bs=bs, n_pages=n_pages, page_base=layer * pool)
    else:
        pooled = pooled.reshape(bs, n_pages * (page // CMP_BLOCK), 2 * NSA_KV * LANE)
    kc, vc = _cmp_proj(pooled, lw["w4"])
    kvn, wn, qr, oc, ngo, gates, val = _sample_pre(za3, zb3, bg3, tabs, kc, vc, past=past)
    idx = _topk(val.reshape(bs * NSA_KV * tp, val.shape[-1]), n_top=n_top)
    idx_flat = idx.reshape(bs, NSA_KV, tp, LANE)[:, :, :ts, :n_top].reshape(-1)
    wbuf = cache_win.shape[2]
    cache_win_rows = cache_win.reshape(depth * bs, wbuf * 2 * NSA_KV, HEAD_DIM)
    yb = _sample_attn(idx_flat, page_flat, qr, oc, ngo, gates, kvn, cache_win_rows, wn, cache4,
                      ts=ts, n_top=n_top, n_pages=n_pages, page_base=layer * pool, win_base=layer * bs)
    ya, conv_new = _conv_mixer(za3, state_conv[layer], lw["w_conv"], t_real=ts)
    nm = cache_mem.shape[2]
    mem_rows = cache_mem.reshape(depth * bs, nm * 2 * MEM_HEADS, HEAD_DIM)
    ym = _mem_attn(zb3, mem_rows, mq_off=MQ_OFF, mg_off=MG_OFF, tq=tp, interleaved=True, kv_base=layer * bs)
    out = _out_proj(xs_p.reshape(m, d), ya.reshape(m, -1), yb.reshape(m, -1), ym.reshape(m, -1),
                    lw["w_out"], final_g, tm=m, final=final)
    kv_new = kvn[:, :ts].reshape(bs, ts, 4, NSA_KV, HEAD_DIM)
    win_rows = wn[:, :ts].reshape(bs, ts, 2, NSA_KV, HEAD_DIM)
    win_state = jnp.concatenate([cache_win[layer], win_rows], axis=1)[:, ts:]
    return out.reshape(bs, tp, d), kv_new, win_state, conv_new


def kernel(x_prompt, x_sample, cache_kv, cache_win, state_conv, cache_mem, page_table, mem_prompt,
           norm_g, w_in, w_conv, a_cmp, w_cmp, mem_norm_g, w_mem_kv, w_out, final_g):
    depth = w_in.shape[0]
    ts = x_sample.shape[1]
    xp = x_prompt
    xs = jnp.pad(x_sample, ((0, 0), (0, SAMPLE_T_PAD - ts), (0, 0)))
    pool_size, page = cache_kv.shape[1], cache_kv.shape[2]
    cache4 = cache_kv.reshape(depth * pool_size, page, 4 * NSA_KV, HEAD_DIM)
    page_flat = page_table.reshape(-1).astype(jnp.int32)
    kv_p, win_p, conv_p, mem_p, kv_s, win_s, conv_s = [], [], [], [], [], [], []
    for l in range(depth):
        lw = _layer_weights(norm_g[l], w_in, l, w_conv[l], a_cmp[l], w_cmp[l], w_out[l])
        final = l == depth - 1
        xp, kvn, winn, convn, mkv, pooled, proj_s = _prompt_layer(
            xp, xs.reshape(-1, xs.shape[-1]), mem_prompt, mem_norm_g[l], w_mem_kv[l].T.astype(BF16), lw, final_g, final,
            pool=(cache4, page_flat, lw["a3"], l * pool_size))
        kv_p.append(kvn)
        win_p.append(winn)
        conv_p.append(convn)
        mem_p.append(mkv)
        xs, kvn, winn, convn = _sample_layer(xs, proj_s, ts, l, cache4, page_flat, pooled, cache_win, state_conv,
                                             cache_mem, lw, final_g, final)
        kv_s.append(kvn)
        win_s.append(winn)
        conv_s.append(convn)
    return (xp, xs[:, :ts], jnp.stack(kv_p), jnp.stack(win_p), jnp.stack(conv_p), jnp.stack(mem_p),
            jnp.stack(kv_s), jnp.stack(win_s), jnp.stack(conv_s))
```

```python
import functools

import jax
import jax.numpy as jnp
from jax import lax
from jax.experimental import pallas as pl
from jax.experimental.pallas import tpu as pltpu

F32 = jnp.float32
BF16 = jnp.bfloat16

HEAD_DIM = 128
CONV_DIM = 512
CONV_W = 3
NSA_HEADS = 8
NSA_KV = 2
NSA_HPG = NSA_HEADS // NSA_KV
MEM_HEADS = 4
N_BRANCH = 3
ROPE_DIM = HEAD_DIM // 4
ROPE_HALF = ROPE_DIM // 2
ROPE_THETA = 500000.0
CMP_BLOCK = 64
SEL_BLOCK = 64
TOP_N = 16
WINDOW = 512
NORM_EPS = 1e-6
MASK_NEG = -1e30
FORCE = 1e9
ATTN_SCALE = HEAD_DIM ** -0.5
SCALE_LOG2 = ATTN_SCALE * 1.4426950408889634

C_H, C_B, C_C, C_G, Q_OFF, NG_OFF = 0, 512, 1024, 1536, 2048, 3072
ZA_WIDTH = 4096
KV_OFF, MQ_OFF, MG_OFF = 0, 1536, 2048
ZB_WIDTH = 2560
BG_SRC = 4096
BG_N = NSA_HEADS * N_BRANCH
LANE = 128
SAMPLE_T_PAD = 8
VMEM_LIMIT = 56 * 1024 * 1024


def _nt(a, b):
    return lax.dot_general(a, b, (((1,), (1,)), ((), ())), preferred_element_type=F32)


def _nn(a, b):
    return jnp.dot(a, b, preferred_element_type=F32)


def _params(sem, vmem=VMEM_LIMIT):
    return pltpu.CompilerParams(dimension_semantics=sem, vmem_limit_bytes=vmem)


def _rope(x, c, s1, s2):
    return x * c + pltpu.roll(x, ROPE_HALF, 1) * s1 + pltpu.roll(x, LANE - ROPE_HALF, 1) * s2


def _silu(x):
    return x * jax.nn.sigmoid(x)


def _norm_matmul_kernel(x_ref, g_ref, w_ref, *rest, with_gate, tn):
    if with_gate:
        wbg_ref, z_ref, bg_ref, h_scr = rest
    else:
        z_ref, h_scr = rest
    x = x_ref[...]
    y = x * lax.rsqrt(jnp.mean(x * x, axis=-1, keepdims=True) + NORM_EPS) * g_ref[...]
    h_scr[...] = y.astype(BF16)
    if with_gate:
        bg_ref[...] = _nt(h_scr[...], wbg_ref[...])
    for j in range(w_ref.shape[0] // tn):
        z_ref[:, j * tn:(j + 1) * tn] = _nt(h_scr[...], w_ref[j * tn:(j + 1) * tn, :])


def _norm_matmul(x, g, w, wbg=None, *, tm, tn):
    m, d = x.shape
    n = w.shape[0]
    assert n % tn == 0 and m % tm == 0 and w.shape[1] == d
    with_gate = wbg is not None
    resident = lambda shape: pl.BlockSpec(shape, lambda i: (0, 0), pipeline_mode=pl.Buffered(1))
    in_specs = [pl.BlockSpec((tm, d), lambda i: (i, 0)), resident((1, d)), resident((n, d))]
    out_shape = [jax.ShapeDtypeStruct((m, n), F32)]
    out_specs = [pl.BlockSpec((tm, n), lambda i: (i, 0))]
    args = [x, g.reshape(1, d), w]
    if with_gate:
        nb = wbg.shape[0]
        in_specs.append(resident((nb, d)))
        out_shape.append(jax.ShapeDtypeStruct((m, nb), F32))
        out_specs.append(pl.BlockSpec((tm, nb), lambda i: (i, 0)))
        args.append(wbg)
    res = pl.pallas_call(
        functools.partial(_norm_matmul_kernel, with_gate=with_gate, tn=tn),
        out_shape=out_shape, grid=(m // tm,), in_specs=in_specs, out_specs=out_specs,
        scratch_shapes=[pltpu.VMEM((tm, d), BF16)],
        compiler_params=_params(("parallel",)),
        name="norm_matmul_gate" if with_gate else "norm_matmul",
    )(*args)
    return res if with_gate else res[0]


def _rms_rows(x_ref, g_ref):
    x = x_ref[...]
    return (x * lax.rsqrt(jnp.mean(x * x, axis=-1, keepdims=True) + NORM_EPS) * g_ref[...]).astype(BF16)


def _proj_conv_kernel(x_ref, g_ref, w_ref, wbg_ref, wc_ref, c_ref, s1_ref, s2_ref, xs_ref,
                      qc_ref, qr_ref, ng_ref, bg_ref, ya_ref, st_ref, zs_ref, bgs_ref, h_scr, up_scr,
                      *, blocks_per_seq):
    tm = x_ref.shape[0]
    cw = CONV_DIM
    i = pl.program_id(0)
    first = i % blocks_per_seq == 0
    h_scr[...] = _rms_rows(x_ref, g_ref)
    bg_ref[...] = _nt(h_scr[...], wbg_ref[...])
    chunk = lambda off: _nt(h_scr[...], w_ref[off:off + cw, :])

    @pl.when(first)
    def _():
        up_scr[pl.ds(8 - (CONV_W - 1), CONV_W - 1), :] = jnp.zeros((CONV_W - 1, cw), F32)

    @pl.when(jnp.logical_not(first))
    def _():
        up_scr[pl.ds(8 - (CONV_W - 1), CONV_W - 1), :] = up_scr[pl.ds(8 + tm - (CONV_W - 1), CONV_W - 1), :]

    u = chunk(C_C) * chunk(C_H)
    up_scr[pl.ds(8, tm), :] = u
    wc = wc_ref[...]
    y = wc[0:1, :] * up_scr[pl.ds(6, tm), :]
    y = y + wc[1:2, :] * up_scr[pl.ds(7, tm), :]
    y = y + wc[2:3, :] * u
    y = chunk(C_B) * y
    ya_ref[...] = (_silu(chunk(C_G)) * y).astype(ya_ref.dtype)

    @pl.when(i % blocks_per_seq == blocks_per_seq - 1)
    def _():
        st_ref[0] = up_scr[pl.ds(8 + tm - (CONV_W - 1), CONV_W - 1), :]

    c, s1, s2 = c_ref[...], s1_ref[...], s2_ref[...]
    for j in range((NG_OFF - Q_OFF) // cw):
        qv = chunk(Q_OFF + j * cw)
        for k in range(cw // LANE):
            lo = j * cw + k * LANE
            qh = qv[:, k * LANE:(k + 1) * LANE]
            qc_ref[:, lo:lo + LANE] = (qh * SCALE_LOG2).astype(BF16)
            qr_ref[:, lo:lo + LANE] = (_rope(qh, c, s1, s2) * SCALE_LOG2).astype(BF16)
    for j in range((ZA_WIDTH - NG_OFF) // cw):
        ng_ref[:, j * cw:(j + 1) * cw] = _silu(chunk(NG_OFF + j * cw))

    @pl.when(i == pl.num_programs(0) - 1)
    def _():
        hs = _rms_rows(xs_ref, g_ref)
        bgs_ref[...] = _nt(hs, wbg_ref[...])
        for j in range(ZA_WIDTH // cw):
            zs_ref[:, j * cw:(j + 1) * cw] = _nt(hs, w_ref[j * cw:(j + 1) * cw, :])


def _proj_conv(x2, g, w_a, wbg, w_conv, tabs, xs2, *, tm, seq_len):
    m, d = x2.shape
    ms = xs2.shape[0]
    assert seq_len % tm == 0 and CONV_W == 3
    blocks_per_seq = seq_len // tm
    nq, nng = NG_OFF - Q_OFF, ZA_WIDTH - NG_OFF
    resident = lambda shape: pl.BlockSpec(shape, lambda i: (0,) * len(shape), pipeline_mode=pl.Buffered(1))
    row = lambda width: pl.BlockSpec((tm, width), lambda i: (i, 0))
    tspec = pl.BlockSpec((tm, LANE), lambda i: (i % blocks_per_seq, 0))
    return pl.pallas_call(
        functools.partial(_proj_conv_kernel, blocks_per_seq=blocks_per_seq),
        out_shape=[jax.ShapeDtypeStruct((m, nq), BF16), jax.ShapeDtypeStruct((m, nq), BF16),
                   jax.ShapeDtypeStruct((m, nng), F32), jax.ShapeDtypeStruct((m, LANE), F32),
                   jax.ShapeDtypeStruct((m, CONV_DIM), BF16),
                   jax.ShapeDtypeStruct((m // seq_len, CONV_W - 1, CONV_DIM), F32),
                   jax.ShapeDtypeStruct((ms, ZA_WIDTH), F32), jax.ShapeDtypeStruct((ms, LANE), F32)],
        grid=(m // tm,),
        in_specs=[row(d), resident((1, d)), resident(w_a.shape), resident(wbg.shape), resident(w_conv.shape),
                  tspec, tspec, tspec, resident((ms, d))],
        out_specs=[row(nq), row(nq), row(nng), row(LANE), row(CONV_DIM),
                   pl.BlockSpec((1, CONV_W - 1, CONV_DIM), lambda i: (i // blocks_per_seq, 0, 0)),
                   pl.BlockSpec((ms, ZA_WIDTH), lambda i: (0, 0)), pl.BlockSpec((ms, LANE), lambda i: (0, 0))],
        scratch_shapes=[pltpu.VMEM((tm, d), BF16), pltpu.VMEM((tm + 8, CONV_DIM), F32)],
        compiler_params=_params(("arbitrary",)),
        name="proj_conv",
    )(x2, g.reshape(1, d), w_a, wbg, w_conv, *tabs, xs2)


def _proj_kv_kernel(x_ref, g_ref, w_ref, c_ref, s1_ref, s2_ref, a_ref, xs_ref,
                    kvn_ref, kvb_ref, win_ref, pool_ref, zm_ref, zs_ref, h_scr, *, blocks_per_seq):
    tm = x_ref.shape[0]
    cw = 2 * NSA_KV * LANE
    h_scr[...] = _rms_rows(x_ref, g_ref)
    chunk = lambda off: _nt(h_scr[...], w_ref[off:off + cw, :])
    c, s1, s2 = c_ref[...], s1_ref[...], s2_ref[...]
    kv0 = chunk(KV_OFF)
    kv1 = chunk(KV_OFF + cw)
    kv2 = chunk(KV_OFF + 2 * cw)
    n_kv, n_w, half = 4 * NSA_KV, 2 * NSA_KV, NSA_KV * LANE
    last = pl.program_id(0) % blocks_per_seq == blocks_per_seq - 1
    for g in range(NSA_KV):
        lo, hi = g * LANE, (g + 1) * LANE
        ks = _rope(kv1[:, lo:hi], c, s1, s2)
        kw = _rope(kv2[:, lo:hi], c, s1, s2)
        vs = kv1[:, half + lo:half + hi]
        vw = kv2[:, half + lo:half + hi]
        kvn_ref[pl.ds(g, tm, stride=n_kv), :] = kv0[:, lo:hi]
        kvn_ref[pl.ds(NSA_KV + g, tm, stride=n_kv), :] = kv0[:, half + lo:half + hi]
        kvn_ref[pl.ds(2 * NSA_KV + g, tm, stride=n_kv), :] = ks
        kvn_ref[pl.ds(3 * NSA_KV + g, tm, stride=n_kv), :] = vs
        kvb_ref[:, lo:hi] = ks.astype(BF16)
        kvb_ref[:, half + lo:half + hi] = vs.astype(BF16)
        kvb_ref[:, 2 * half + lo:2 * half + hi] = kw.astype(BF16)
        kvb_ref[:, 3 * half + lo:3 * half + hi] = vw.astype(BF16)

        @pl.when(last)
        def _():
            win_ref[pl.ds(g, tm, stride=n_w), :] = kw
            win_ref[pl.ds(NSA_KV + g, tm, stride=n_w), :] = vw

    pool_ref[...] = jnp.sum(kv0.reshape(tm // CMP_BLOCK, CMP_BLOCK, cw) * a_ref[...][None], axis=1)
    for j in range(zm_ref.shape[1] // cw):
        zm_ref[:, j * cw:(j + 1) * cw] = chunk(MQ_OFF + j * cw)

    @pl.when(pl.program_id(0) == pl.num_programs(0) - 1)
    def _():
        hs = _rms_rows(xs_ref, g_ref)
        for j in range(ZB_WIDTH // cw):
            zs_ref[:, j * cw:(j + 1) * cw] = _nt(hs, w_ref[j * cw:(j + 1) * cw, :])


def _proj_kv(x2, g, w_b, tabs, a4, xs2, *, tm, seq_len):
    m, d = x2.shape
    ms = xs2.shape[0]
    assert seq_len % tm == 0
    blocks_per_seq = seq_len // tm
    n_kv, n_w = 4 * NSA_KV, 2 * NSA_KV
    nm = ZB_WIDTH - MQ_OFF
    resident = lambda shape: pl.BlockSpec(shape, lambda i: (0,) * len(shape), pipeline_mode=pl.Buffered(1))
    row = lambda width: pl.BlockSpec((tm, width), lambda i: (i, 0))
    tspec = pl.BlockSpec((tm, LANE), lambda i: (i % blocks_per_seq, 0))
    return pl.pallas_call(
        functools.partial(_proj_kv_kernel, blocks_per_seq=blocks_per_seq),
        out_shape=[jax.ShapeDtypeStruct((m * n_kv, LANE), F32),
                   jax.ShapeDtypeStruct((m, n_kv * LANE), BF16),
                   jax.ShapeDtypeStruct((m // seq_len * tm * n_w, LANE), F32),
                   jax.ShapeDtypeStruct((m // CMP_BLOCK, 2 * NSA_KV * LANE), F32),
                   jax.ShapeDtypeStruct((m, nm), F32), jax.ShapeDtypeStruct((ms, ZB_WIDTH), F32)],
        grid=(m // tm,),
        in_specs=[row(d), resident((1, d)), resident(w_b.shape), tspec, tspec, tspec, resident(a4.shape),
                  resident((ms, d))],
        out_specs=[pl.BlockSpec((tm * n_kv, LANE), lambda i: (i, 0)), row(n_kv * LANE),
                   pl.BlockSpec((tm * n_w, LANE), lambda i: (i // blocks_per_seq, 0)),
                   pl.BlockSpec((tm // CMP_BLOCK, 2 * NSA_KV * LANE), lambda i: (i, 0)), row(nm),
                   pl.BlockSpec((ms, ZB_WIDTH), lambda i: (0, 0))],
        scratch_shapes=[pltpu.VMEM((tm, d), BF16)],
        compiler_params=_params(("arbitrary",)),
        name="proj_kv",
    )(x2, g.reshape(1, d), w_b, *tabs, a4, xs2)


def _cast_rows_kernel(w_ref, o_ref):
    o_ref[...] = w_ref[0].astype(BF16)


def _pack_gate_kernel(w_ref, o_ref):
    o_ref[...] = jnp.zeros(o_ref.shape, BF16)
    o_ref[0:w_ref.shape[1], :] = w_ref[0].astype(BF16)


def _pack_in_proj_weights(w_t3, layer):
    _, n_in, d = w_t3.shape
    tn = 512
    assert n_in == BG_SRC + BG_N + ZB_WIDTH and BG_SRC % tn == 0 and ZB_WIDTH % tn == 0 and BG_N % 8 == 0

    def cast_rows(first_row, n_rows, name):
        return pl.pallas_call(
            _cast_rows_kernel,
            out_shape=jax.ShapeDtypeStruct((n_rows, d), BF16),
            grid=(n_rows // tn,),
            in_specs=[pl.BlockSpec((pl.Element(1), pl.Element(tn), pl.Element(d)),
                                   lambda j: (layer, pl.multiple_of(first_row + j * tn, 8), 0))],
            out_specs=pl.BlockSpec((tn, d), lambda j: (j, 0)),
            compiler_params=_params(("parallel",)),
            name=name,
        )(w_t3)

    wa = cast_rows(0, ZA_WIDTH, "pack_w_a")
    wb = cast_rows(BG_SRC + BG_N, ZB_WIDTH, "pack_w_b")
    wbg = pl.pallas_call(
        _pack_gate_kernel,
        out_shape=jax.ShapeDtypeStruct((LANE, d), BF16),
        grid=(1,),
        in_specs=[pl.BlockSpec((pl.Element(1), pl.Element(BG_N), pl.Element(d)), lambda j: (layer, BG_SRC, 0))],
        out_specs=pl.BlockSpec((LANE, d), lambda j: (0, 0)),
        compiler_params=_params(("arbitrary",)),
        name="pack_w_gate",
    )(w_t3)
    return wa, wb, wbg


def _pool_pages_kernel(pt_sm, a_ref, cache_ref, o_ref, buf, sem, *, pages_per_step, page_base):
    step = pl.program_id(0)
    n_steps = pl.num_programs(0)
    slot = step % 2
    n_cols = buf.shape[3]

    def page_copy(step_idx, p, to_slot):
        page = pt_sm[step_idx * pages_per_step + p] + page_base
        return pltpu.make_async_copy(cache_ref.at[page, :, pl.ds(0, n_cols), :], buf.at[to_slot, p], sem.at[to_slot])

    @pl.when(step == 0)
    def _():
        for p in range(pages_per_step):
            page_copy(0, p, 0).start()

    @pl.when(step + 1 < n_steps)
    def _():
        for p in range(pages_per_step):
            page_copy(step + 1, p, 1 - slot).start()

    for p in range(pages_per_step):
        page_copy(step, p, slot).wait()

    a = a_ref[...]
    per = buf.shape[2] // CMP_BLOCK
    for p in range(pages_per_step):
        for k in range(per):
            x = buf[slot, p, pl.ds(k * CMP_BLOCK, CMP_BLOCK)]
            o_ref[0, p, k] = jnp.sum(x * a, axis=0)


def _pool_pages(cache4, page_flat, a3, *, bs, n_pages, page_base, pages_per_step=16):
    page = cache4.shape[1]
    per = page // CMP_BLOCK
    n_cols = 2 * NSA_KV
    total = bs * n_pages
    pages_per_step = min(pages_per_step, total)
    assert total % pages_per_step == 0
    n_steps = total // pages_per_step
    grid_spec = pltpu.PrefetchScalarGridSpec(
        num_scalar_prefetch=1, grid=(n_steps,),
        in_specs=[pl.BlockSpec((CMP_BLOCK, n_cols, LANE), lambda si, pt: (0, 0, 0)),
                  pl.BlockSpec(memory_space=pl.ANY)],
        out_specs=pl.BlockSpec((1, pages_per_step, per, n_cols, LANE), lambda si, pt: (si, 0, 0, 0, 0)),
        scratch_shapes=[pltpu.VMEM((2, pages_per_step, page, n_cols, LANE), F32),
                        pltpu.SemaphoreType.DMA((2,))])
    out = pl.pallas_call(
        functools.partial(_pool_pages_kernel, pages_per_step=pages_per_step, page_base=page_base),
        out_shape=jax.ShapeDtypeStruct((n_steps, pages_per_step, per, n_cols, LANE), F32),
        grid_spec=grid_spec,
        compiler_params=_params(("arbitrary",)),
        name="pool_pages",
    )(page_flat, a3, cache4)
    return out.reshape(bs, n_pages * per, n_cols * LANE)


def _cmp_proj_kernel(p_ref, w_ref, kc_ref, vc_ref):
    pooled = p_ref[0]
    n = pooled.shape[0]
    n_pad = kc_ref.shape[2]
    for c in range(4):
        r = _nn(pooled[:, c * LANE:(c + 1) * LANE].astype(BF16), w_ref[c]).astype(BF16)
        dst = kc_ref if c < 2 else vc_ref
        if n_pad > n:
            dst[0, c % 2] = jnp.zeros((n_pad, LANE), BF16)
        dst[0, c % 2, 0:n, :] = r


def _cmp_proj(pooled, w4):
    b, n, _ = pooled.shape
    n_pad = -(-n // LANE) * LANE
    spec = pl.BlockSpec((1, NSA_KV, n_pad, LANE), lambda bi: (bi, 0, 0, 0))
    return pl.pallas_call(
        _cmp_proj_kernel,
        out_shape=[jax.ShapeDtypeStruct((b, NSA_KV, n_pad, LANE), BF16)] * 2,
        grid=(b,),
        in_specs=[pl.BlockSpec((1, n, 512), lambda bi: (bi, 0, 0)),
                  pl.BlockSpec((4, LANE, LANE), lambda bi: (0, 0, 0))],
        out_specs=[spec, spec],
        compiler_params=_params(("parallel",)),
        name="cmp_proj",
    )(pooled, w4)


def _lane_parts(x):
    return [x[:, j * LANE:(j + 1) * LANE] for j in range(x.shape[1] // LANE)]


def _nsa_prompt_kernel(*refs, t_len, tq, tc, tw, n_sel, top, sub, pool_pages, page_base):
    if pool_pages:
        (pt_sm, qc_ref, qr_ref, ng_ref, bg_ref, ksel_ref, vsel_ref, kwin_ref, vwin_ref, kc_ref, vc_ref,
         pa_ref, cache_ref, o_ref, pool_ref,
         qc_scr, qr_scr, s_scr, p_scr, a_scr, m_scr, l_scr, acc_scr, oc_scr, b_scr, pbuf, psem) = refs
        step = (pl.program_id(0) * pl.num_programs(1) + pl.program_id(1)) * pl.num_programs(2) + pl.program_id(2)
        n_steps = pl.num_programs(0) * pl.num_programs(1) * pl.num_programs(2)
        slot = step % 2
        half_rows, n_cols = pbuf.shape[2], 2 * NSA_KV

        def page_copies(step_idx, p, to_slot):
            page = pt_sm[step_idx * pool_pages + p] + page_base
            return [pltpu.make_async_copy(cache_ref.at[page, pl.ds(hh * half_rows, half_rows), pl.ds(0, n_cols), :],
                                          pbuf.at[to_slot, p, :, pl.ds(hh * n_cols, n_cols), :], psem.at[to_slot])
                    for hh in range(2)]

        @pl.when(step == 0)
        def _():
            for p in range(pool_pages):
                for cp in page_copies(0, p, 0):
                    cp.start()

        @pl.when(step + 1 < n_steps)
        def _():
            for p in range(pool_pages):
                for cp in page_copies(step + 1, p, 1 - slot):
                    cp.start()
    else:
        (qc_ref, qr_ref, ng_ref, bg_ref, ksel_ref, vsel_ref, kwin_ref, vwin_ref, kc_ref, vc_ref,
         o_ref, qc_scr, qr_scr, s_scr, p_scr, a_scr, m_scr, l_scr, acc_scr, oc_scr, b_scr) = refs
    i = pl.program_id(2)
    hq = NSA_HPG
    for h in range(hq):
        qc_scr[pl.ds(h * tq, tq), :] = qc_ref[0, :, h * LANE:(h + 1) * LANE]
        qr_scr[pl.ds(h * tq, tq), :] = qr_ref[0, :, h * LANE:(h + 1) * LANE]

    kc = kc_ref[0, 0]
    npad = kc.shape[0]
    s_scr[:, 0:npad] = _nt(qc_scr[...], kc)
    tpos = i * tq + lax.broadcasted_iota(jnp.int32, (tq, npad), 0)
    ncol = lax.broadcasted_iota(jnp.int32, (tq, npad), 1)
    cmask = (ncol + 1) * CMP_BLOCK <= tpos + 1
    imp = jnp.zeros((tq, npad), F32)
    for h in range(hq):
        r = pl.ds(h * tq, tq)
        s = jnp.where(cmask, s_scr[r, 0:npad], MASK_NEG)
        e = jnp.where(cmask, jnp.exp2(s - jnp.max(s, axis=-1, keepdims=True)), 0.0)
        p = e / jnp.maximum(jnp.sum(e, axis=-1, keepdims=True), 1e-30)
        p_scr[r, 0:npad] = p.astype(BF16)
        imp = imp + p
    oc_scr[...] = _nn(p_scr[:, 0:npad], vc_ref[0, 0])

    rows = min(npad, -(-n_sel // 8) * 8)
    imp_t = imp.T[0:rows]
    blk = lax.broadcasted_iota(jnp.int32, (rows, tq), 0)
    cur = (i * tq + lax.broadcasted_iota(jnp.int32, (rows, tq), 1)) // SEL_BLOCK
    imp_t = jnp.where((blk == 0) | (blk == cur) | (blk == cur - 1), FORCE, imp_t)
    imp_t = jnp.where(blk > cur, -1.0, imp_t)
    imp_t = jnp.where(blk >= n_sel, -2.0, imp_t)
    rank = jnp.zeros((rows, tq), F32)
    for j in range(n_sel):
        a = imp_t[j:j + 1, :]
        ahead = (a > imp_t) | ((a == imp_t) & (blk > j))
        rank = rank + jnp.where(ahead, 1.0, 0.0)
    sel_t = jnp.where((rank < top) & (blk < n_sel), 1.0, 0.0)
    if npad > rows:
        sel_t = jnp.concatenate([sel_t, jnp.zeros((npad - rows, tq), F32)], axis=0)
    sel = sel_t.T.astype(BF16)

    m_scr[...] = jnp.full(m_scr.shape, MASK_NEG, F32)
    l_scr[...] = jnp.zeros(l_scr.shape, F32)
    acc_scr[...] = jnp.zeros(acc_scr.shape, F32)
    n_chunks = (i * tq + tq + tc - 1) // tc

    def chunk(ci, carry):
        k0 = pl.multiple_of(ci * tc, tc)
        k = ksel_ref[0, pl.ds(k0, tc), :]
        v = vsel_ref[0, pl.ds(k0, tc), :]
        jb = lax.broadcasted_iota(jnp.int32, (npad, tc), 0)
        kb = (k0 + lax.broadcasted_iota(jnp.int32, (npad, tc), 1)) // SEL_BLOCK
        expand = jnp.where(jb == kb, 1.0, 0.0).astype(BF16)
        chosen = _nn(sel, expand)
        kp = k0 + lax.broadcasted_iota(jnp.int32, (tq, tc), 1)
        tp = i * tq + lax.broadcasted_iota(jnp.int32, (tq, tc), 0)
        b_scr[:, 0:tc] = jnp.where((chosen > 0.5) & (kp <= tp), 0.0, MASK_NEG)
        s_scr[:, 0:tc] = _nt(qr_scr[...], k)
        for r0 in range(0, hq * tq, sub):
            r = pl.ds(r0, sub)
            parts = _lane_parts(s_scr[r, 0:tc] + b_scr[pl.ds(r0 % tq, sub), 0:tc])
            m_prev = m_scr[r, :]
            m_new = jnp.maximum(m_prev, jnp.max(functools.reduce(jnp.maximum, parts), axis=-1, keepdims=True))
            alpha = jnp.exp2(m_prev - m_new)
            pes = [jnp.exp2(x - m_new) for x in parts]
            l_scr[r, :] = alpha * l_scr[r, :] + functools.reduce(jnp.add, pes)
            p_scr[r, 0:tc] = jnp.concatenate(pes, axis=1).astype(BF16)
            a_scr[r, :] = alpha
            m_scr[r, :] = m_new
        acc_scr[...] = a_scr[...] * acc_scr[...] + _nn(p_scr[:, 0:tc], v)
        return carry

    lax.fori_loop(0, n_chunks, chunk, 0)

    w0 = pl.multiple_of(jnp.clip(i * tq + tq - tw, 0, t_len - tw), LANE)
    kp = w0 + lax.broadcasted_iota(jnp.int32, (tq, tw), 1)
    tp = i * tq + lax.broadcasted_iota(jnp.int32, (tq, tw), 0)
    b_scr[:, 0:tw] = jnp.where((kp <= tp) & (kp > tp - WINDOW), 0.0, MASK_NEG)
    s_scr[:, 0:tw] = _nt(qr_scr[...], kwin_ref[0, pl.ds(w0, tw), :])
    for r0 in range(0, hq * tq, sub):
        r = pl.ds(r0, sub)
        parts = _lane_parts(s_scr[r, 0:tw] + b_scr[pl.ds(r0 % tq, sub), 0:tw])
        m = jnp.max(functools.reduce(jnp.maximum, parts), axis=-1, keepdims=True)
        ews = [jnp.exp2(x - m) for x in parts]
        row_sum = jnp.sum(functools.reduce(jnp.add, ews), axis=-1, keepdims=True)
        a_scr[r, :] = jnp.broadcast_to(1.0 / row_sum, (sub, LANE))
        p_scr[r, 0:tw] = jnp.concatenate(ews, axis=1).astype(BF16)
    o_w = _nn(p_scr[:, 0:tw], vwin_ref[0, pl.ds(w0, tw), :])

    gate = jax.nn.sigmoid(bg_ref[0])
    for g in range(1, NSA_KV):
        gate = jnp.where(pl.program_id(1) == g, pltpu.roll(gate, LANE - g * hq * N_BRANCH, 1), gate)
    ng = ng_ref[0]
    for h in range(hq):
        r = slice(h * tq, (h + 1) * tq)
        o_s = acc_scr[r, :] * (1.0 / jnp.sum(l_scr[r, :], axis=-1, keepdims=True))
        o = (gate[:, 3 * h:3 * h + 1] * oc_scr[r, :] + gate[:, 3 * h + 1:3 * h + 2] * o_s
             + gate[:, 3 * h + 2:3 * h + 3] * (o_w[r] * a_scr[r, :]))
        o_ref[0, :, h * LANE:(h + 1) * LANE] = (ng[:, h * LANE:(h + 1) * LANE] * o).astype(o_ref.dtype)

    if pool_pages:
        for p in range(pool_pages):
            for cp in page_copies(step, p, slot):
                cp.wait()
        pa = pa_ref[...]
        for p in range(pool_pages):
            pool_ref[0, p] = jnp.sum(pbuf[slot, p] * pa, axis=0)


def _nsa_prompt(qc3, qr3, ng3, bg3, kvb, kc, vc, *, tq=128, tc=512, sub=64, pool=None):
    b, t, _ = qc3.shape
    nq = t // tq
    tc = min(tc, t)
    tw = min(WINDOW + tq, t)
    n_sel = t // SEL_BLOCK
    top = min(TOP_N, n_sel)
    npad = kc.shape[2]
    gw = NSA_HPG * LANE
    rows = NSA_HPG * tq
    wide = max(tc, tw, npad)
    kvspec = lambda k: pl.BlockSpec((1, t, LANE), lambda bi, gi, qi, *_, k=k: (bi, 0, k + gi))
    cspec = pl.BlockSpec((1, 1, npad, LANE), lambda bi, gi, qi, *_: (bi, gi, 0, 0))
    gspec = pl.BlockSpec((1, tq, gw), lambda bi, gi, qi, *_: (bi, qi, gi))
    in_specs = [gspec, gspec, gspec,
                pl.BlockSpec((1, tq, LANE), lambda bi, gi, qi, *_: (bi, qi, 0)),
                kvspec(0), kvspec(2), kvspec(4), kvspec(6), cspec, cspec]
    out_shape = [jax.ShapeDtypeStruct((b, t, NSA_HEADS * LANE), BF16)]
    out_specs = [pl.BlockSpec((1, tq, gw), lambda bi, gi, qi, *_: (bi, qi, gi))]
    scratch = ([pltpu.VMEM((rows, LANE), BF16)] * 2
               + [pltpu.VMEM((rows, wide), F32), pltpu.VMEM((rows, wide), BF16)]
               + [pltpu.VMEM((rows, LANE), F32)] * 5 + [pltpu.VMEM((tq, wide), F32)])
    args = [qc3, qr3, ng3, bg3, kvb, kvb, kvb, kvb, kc, vc]
    n_steps = b * NSA_KV * nq
    pool_pages, page_base, prefetch = 0, 0, []
    if pool is not None:
        cache4, page_flat, a3, page_base = pool
        page, n_cols = cache4.shape[1], 2 * NSA_KV
        if page_flat.shape[0] % n_steps == 0 and page == 2 * CMP_BLOCK and 2 * n_cols == 8:
            pool_pages = page_flat.shape[0] // n_steps
            prefetch = [page_flat]
            in_specs += [pl.BlockSpec((CMP_BLOCK, 2 * n_cols, LANE), lambda bi, gi, qi, *_: (0, 0, 0)),
                         pl.BlockSpec(memory_space=pl.ANY)]
            args += [jnp.concatenate([a3, a3], axis=1), cache4]
            out_shape.append(jax.ShapeDtypeStruct((n_steps, pool_pages, 2 * n_cols, LANE), F32))
            out_specs.append(pl.BlockSpec((1, pool_pages, 2 * n_cols, LANE),
                                          lambda bi, gi, qi, *_: ((bi * NSA_KV + gi) * nq + qi, 0, 0, 0)))
            scratch += [pltpu.VMEM((2, pool_pages, CMP_BLOCK, 2 * n_cols, LANE), F32), pltpu.SemaphoreType.DMA((2,))]
    kern = functools.partial(_nsa_prompt_kernel, t_len=t, tq=tq, tc=tc, tw=tw, n_sel=n_sel, top=top,
                             sub=min(sub, tq), pool_pages=pool_pages, page_base=page_base)
    res = pl.pallas_call(
        kern,
        out_shape=out_shape,
        grid_spec=pltpu.PrefetchScalarGridSpec(
            num_scalar_prefetch=len(prefetch), grid=(b, NSA_KV, nq),
            in_specs=in_specs, out_specs=out_specs, scratch_shapes=scratch),
        compiler_params=_params(("arbitrary", "arbitrary", "arbitrary")),
        name="nsa_prompt",
    )(*prefetch, *args)
    return (res[0], res[1]) if pool_pages else (res[0], None)


def _conv_kernel(h_ref, b_ref, c_ref, g_ref, prev_ref, w_ref, y_ref, st_ref, up_scr, *, t_real):
    u = c_ref[...] * h_ref[...]
    t = u.shape[1]
    up_scr[:, pl.ds(8 - (CONV_W - 1), CONV_W - 1), :] = prev_ref[...]
    up_scr[:, pl.ds(8, t), :] = u
    w = w_ref[...]
    y = w[0:1, :][None] * up_scr[:, pl.ds(6, t), :]
    y = y + w[1:2, :][None] * up_scr[:, pl.ds(7, t), :]
    y = y + w[2:3, :][None] * u
    y = b_ref[...] * y
    y_ref[...] = (_silu(g_ref[...]) * y).astype(y_ref.dtype)
    st_ref[...] = up_scr[:, pl.ds(6 + t_real, CONV_W - 1), :]


def _conv_mixer(z3, prev, w_conv, *, t_real):
    b, t, _ = z3.shape
    nc = CONV_DIM // LANE
    zspec = lambda off: pl.BlockSpec((b, t, LANE), lambda ci, off=off: (0, 0, off // LANE + ci))
    return pl.pallas_call(
        functools.partial(_conv_kernel, t_real=t_real),
        out_shape=[jax.ShapeDtypeStruct((b, t, CONV_DIM), BF16),
                   jax.ShapeDtypeStruct((b, CONV_W - 1, CONV_DIM), F32)],
        grid=(nc,),
        in_specs=[zspec(C_H), zspec(C_B), zspec(C_C), zspec(C_G),
                  pl.BlockSpec((b, CONV_W - 1, LANE), lambda ci: (0, 0, ci)),
                  pl.BlockSpec((CONV_W, LANE), lambda ci: (0, ci))],
        out_specs=[pl.BlockSpec((b, t, LANE), lambda ci: (0, 0, ci)),
                   pl.BlockSpec((b, CONV_W - 1, LANE), lambda ci: (0, 0, ci))],
        scratch_shapes=[pltpu.VMEM((b, t + 8, LANE), F32)],
        compiler_params=_params(("parallel",)),
        name="conv_mixer",
    )(z3, z3, z3, z3, prev, w_conv)


def _mem_attn_kernel(q_ref, mg_ref, kv_ref, o_ref, *, interleaved):
    q = q_ref[0]
    mg = mg_ref[0]
    half = MEM_HEADS * LANE
    for h in range(MEM_HEADS):
        lo, hi = h * LANE, (h + 1) * LANE
        if interleaved:
            nm = kv_ref.shape[1] // (2 * MEM_HEADS)
            k = kv_ref[0, pl.ds(h, nm, stride=2 * MEM_HEADS), :].astype(BF16)
            v = kv_ref[0, pl.ds(MEM_HEADS + h, nm, stride=2 * MEM_HEADS), :].astype(BF16)
        else:
            k = kv_ref[0, :, lo:hi].astype(BF16)
            v = kv_ref[0, :, half + lo:half + hi].astype(BF16)
        s = _nt((q[:, lo:hi] * ATTN_SCALE).astype(BF16), k)
        e = jnp.exp(s - jnp.max(s, axis=-1, keepdims=True))
        o = _nn(e.astype(BF16), v) / jnp.sum(e, axis=-1, keepdims=True)
        o_ref[0, :, lo:hi] = (_silu(mg[:, lo:hi]) * o).astype(o_ref.dtype)


def _mem_attn(zb3, mkv, *, mq_off, mg_off, tq, interleaved, kv_base=0):
    b, t, _ = zb3.shape
    wq = MEM_HEADS * LANE
    return pl.pallas_call(
        functools.partial(_mem_attn_kernel, interleaved=interleaved),
        out_shape=jax.ShapeDtypeStruct((b, t, wq), BF16),
        grid=(b, t // tq),
        in_specs=[pl.BlockSpec((1, tq, wq), lambda bi, ti: (bi, ti, mq_off // wq)),
                  pl.BlockSpec((1, tq, wq), lambda bi, ti: (bi, ti, mg_off // wq)),
                  pl.BlockSpec((1,) + mkv.shape[1:], lambda bi, ti: (bi + kv_base, 0, 0))],
        out_specs=pl.BlockSpec((1, tq, wq), lambda bi, ti: (bi, ti, 0)),
        compiler_params=_params(("parallel", "parallel")),
        name="mem_attn",
    )(zb3, zb3, mkv)


def _out_proj_kernel(x_ref, ya_ref, yb_ref, ym_ref, w_ref, fg_ref, o_ref, *, final):
    a, bw = CONV_DIM, CONV_DIM + NSA_HEADS * LANE
    acc = _nn(ya_ref[...], w_ref[0:a, :])
    acc = acc + _nn(yb_ref[...], w_ref[a:bw, :])
    acc = acc + _nn(ym_ref[...], w_ref[bw:, :])
    r = x_ref[...] + acc
    if final:
        r = r * lax.rsqrt(jnp.mean(r * r, axis=-1, keepdims=True) + NORM_EPS) * fg_ref[...]
    o_ref[...] = r


def _out_proj(x, ya, yb, ym, w, fg, *, tm, final):
    m, d = x.shape
    row = lambda width: pl.BlockSpec((tm, width), lambda i: (i, 0))
    return pl.pallas_call(
        functools.partial(_out_proj_kernel, final=final),
        out_shape=jax.ShapeDtypeStruct((m, d), F32),
        grid=(m // tm,),
        in_specs=[row(d), row(ya.shape[1]), row(yb.shape[1]), row(ym.shape[1]),
                  pl.BlockSpec(w.shape, lambda i: (0, 0), pipeline_mode=pl.Buffered(1)),
                  pl.BlockSpec((1, d), lambda i: (0, 0), pipeline_mode=pl.Buffered(1))],
        out_specs=row(d),
        compiler_params=_params(("parallel",)),
        name="out_proj",
    )(x, ya, yb, ym, w, fg.reshape(1, d))


def _sample_pre_kernel(q_ref, ng_ref, bg_ref, kv0_ref, kv1_ref, kv2_ref, c_ref, s1_ref, s2_ref, kc_ref, vc_ref,
                       kvn_ref, wn_ref, qr_ref, oc_ref, ngo_ref, gate_ref, val_ref, *, past):
    tp = SAMPLE_T_PAD
    hq = NSA_HPG
    c, s1, s2 = c_ref[...], s1_ref[...], s2_ref[...]
    kv0, kv1, kv2 = kv0_ref[0], kv1_ref[0], kv2_ref[0]
    kvn_ref[0, :, 0:512] = kv0
    kvn_ref[0, :, 768:1024] = kv1[:, 256:512]
    wn_ref[0, :, 256:512] = kv2[:, 256:512]
    for g in range(NSA_KV):
        lo, hi = g * LANE, (g + 1) * LANE
        kvn_ref[0, :, 512 + lo:512 + hi] = _rope(kv1[:, lo:hi], c, s1, s2)
        wn_ref[0, :, lo:hi] = _rope(kv2[:, lo:hi], c, s1, s2)

    q = q_ref[0]
    ng = ng_ref[0]
    gates = jax.nn.sigmoid(bg_ref[0])
    for g in range(NSA_KV):
        qc_l, qr_l = [], []
        for h in range(hq):
            lo = (g * hq + h) * LANE
            qh = q[:, lo:lo + LANE]
            qc_l.append(qh * ATTN_SCALE)
            qr_l.append(_rope(qh, c, s1, s2) * ATTN_SCALE)
            ngo_ref[0, g, h * tp:(h + 1) * tp, :] = ng[:, lo:lo + LANE]
            for br in range(N_BRANCH):
                col = (g * hq + h) * N_BRANCH + br
                gate_ref[0, g, br, h * tp:(h + 1) * tp, :] = jnp.broadcast_to(gates[:, col:col + 1], (tp, LANE))
        qc = jnp.concatenate(qc_l, axis=0)
        qr_ref[0, g] = jnp.concatenate(qr_l, axis=0)

        kc = kc_ref[0, g]
        npad = kc.shape[0]
        s = _nt(qc.astype(BF16), kc)
        trow = lax.broadcasted_iota(jnp.int32, (hq * tp, npad), 0) % tp
        ncol = lax.broadcasted_iota(jnp.int32, (hq * tp, npad), 1)
        cmask = (ncol + 1) * CMP_BLOCK <= past + trow + 1
        s = jnp.where(cmask, s, MASK_NEG)
        e = jnp.where(cmask, jnp.exp(s - jnp.max(s, axis=-1, keepdims=True)), 0.0)
        p = e / jnp.maximum(jnp.sum(e, axis=-1, keepdims=True), 1e-30)
        oc_ref[0, g] = _nn(p.astype(BF16), vc_ref[0, g])
        imp = jnp.sum(p.reshape(hq, tp, npad), axis=0)

        blk = lax.broadcasted_iota(jnp.int32, (tp, npad), 1)
        cur = (past + lax.broadcasted_iota(jnp.int32, (tp, npad), 0)) // SEL_BLOCK
        val = jnp.where((blk == 0) | (blk == cur) | (blk == cur - 1), FORCE, imp)
        val = jnp.where(blk > cur, -1.0, val)
        val = jnp.where(blk >= past // SEL_BLOCK, -2.0, val)
        val_ref[0, g] = val


def _topk_kernel(val_ref, idx_ref, *, n_top):
    val = val_ref[...]
    rows, n = val.shape
    blk = lax.broadcasted_iota(jnp.int32, (rows, n), 1)
    lane = lax.broadcasted_iota(jnp.int32, (rows, LANE), 1)
    idx = jnp.zeros((rows, LANE), jnp.int32)
    for r in range(n_top):
        best = jnp.max(val, axis=-1, keepdims=True)
        j = jnp.min(jnp.where(val == best, blk, n), axis=-1, keepdims=True)
        idx = jnp.where(lane == r, j, idx)
        val = jnp.where(blk == j, -3e38, val)
    idx_ref[...] = idx


def _topk(val2, *, n_top):
    rows, n = val2.shape
    return pl.pallas_call(
        functools.partial(_topk_kernel, n_top=n_top),
        out_shape=jax.ShapeDtypeStruct((rows, LANE), jnp.int32),
        grid=(1,),
        in_specs=[pl.BlockSpec((rows, n), lambda i: (0, 0))],
        out_specs=pl.BlockSpec((rows, LANE), lambda i: (0, 0)),
        compiler_params=_params(("arbitrary",)),
        name="sample_topk",
    )(val2)


def _sample_pre(za3, zb3, bg3, tabs, kc, vc, *, past):
    bs, tp, _ = za3.shape
    npad = kc.shape[2]
    qw = NSA_HEADS * LANE
    kvblk = KV_OFF // 512
    zspec = lambda k: pl.BlockSpec((1, tp, 512), lambda bi, k=k: (bi, 0, kvblk + k))
    tspec = pl.BlockSpec((tp, LANE), lambda bi: (0, 0))
    cspec = pl.BlockSpec((1, NSA_KV, npad, LANE), lambda bi: (bi, 0, 0, 0))
    rows = NSA_HPG * tp
    gspec = pl.BlockSpec((1, NSA_KV, rows, LANE), lambda bi: (bi, 0, 0, 0))
    gshape = jax.ShapeDtypeStruct((bs, NSA_KV, rows, LANE), F32)
    return pl.pallas_call(
        functools.partial(_sample_pre_kernel, past=past),
        out_shape=[jax.ShapeDtypeStruct((bs, tp, 1024), F32),
                   jax.ShapeDtypeStruct((bs, tp, 512), F32),
                   gshape, gshape, gshape,
                   jax.ShapeDtypeStruct((bs, NSA_KV, N_BRANCH, rows, LANE), F32),
                   jax.ShapeDtypeStruct((bs, NSA_KV, tp, npad), F32)],
        grid=(bs,),
        in_specs=[pl.BlockSpec((1, tp, qw), lambda bi: (bi, 0, Q_OFF // qw)),
                  pl.BlockSpec((1, tp, qw), lambda bi: (bi, 0, NG_OFF // qw)),
                  pl.BlockSpec((1, tp, LANE), lambda bi: (bi, 0, 0)),
                  zspec(0), zspec(1), zspec(2), tspec, tspec, tspec, cspec, cspec],
        out_specs=[pl.BlockSpec((1, tp, 1024), lambda bi: (bi, 0, 0)),
                   pl.BlockSpec((1, tp, 512), lambda bi: (bi, 0, 0)),
                   gspec, gspec, gspec,
                   pl.BlockSpec((1, NSA_KV, N_BRANCH, rows, LANE), lambda bi: (bi, 0, 0, 0, 0)),
                   pl.BlockSpec((1, NSA_KV, tp, npad), lambda bi: (bi, 0, 0, 0))],
        compiler_params=_params(("parallel",)),
        name="sample_pre",
    )(za3, za3, bg3, zb3, zb3, zb3, *tabs, kc, vc)


def _sample_attn_kernel(idx_sm, pt_sm, qr_ref, oc_ref, ng_ref, gate_ref, ksn_ref, vsn_ref,
                        wc_ref, kwn_ref, vwn_ref, cache_ref, o_ref,
                        kbuf, vbuf, kw_scr, vw_scr, sem, *, ts, n_top, n_pages, page_base, per_page, wb):
    tp = SAMPLE_T_PAD
    hq = NSA_HPG
    b = pl.program_id(0)
    g = pl.program_id(1)
    n_gath = n_top * SEL_BLOCK
    ks_rows = kbuf.shape[2]
    step = b * NSA_KV + g
    n_steps = pl.num_programs(0) * NSA_KV
    slot = step % 2

    def gather_copies(step_idx, t, r, to_slot):
        bb, gg = step_idx // NSA_KV, step_idx % NSA_KV
        blk = idx_sm[(step_idx * ts + t) * n_top + r]
        page = pt_sm[bb * n_pages + blk // per_page] + page_base
        row0 = (blk % per_page) * SEL_BLOCK
        src_k = cache_ref.at[page, pl.ds(row0, SEL_BLOCK), 2 * NSA_KV + gg]
        src_v = cache_ref.at[page, pl.ds(row0, SEL_BLOCK), 3 * NSA_KV + gg]
        dst = pl.ds(r * SEL_BLOCK, SEL_BLOCK)
        return (pltpu.make_async_copy(src_k, kbuf.at[to_slot, t, dst], sem.at[to_slot, 0]),
                pltpu.make_async_copy(src_v, vbuf.at[to_slot, t, dst], sem.at[to_slot, 1]))

    def start_gathers(step_idx, to_slot):
        for t in range(ts):
            for r in range(n_top):
                ck, cv = gather_copies(step_idx, t, r, to_slot)
                ck.start()
                cv.start()

    @pl.when(step == 0)
    def _():
        start_gathers(0, 0)

    @pl.when(step + 1 < n_steps)
    def _():
        start_gathers(step + 1, 1 - slot)

    qr = qr_ref[0, 0].astype(BF16)
    trow = lax.broadcasted_iota(jnp.int32, (hq * tp, 1), 0) % tp

    ww = kw_scr.shape[0]
    kw_scr[pl.ds(0, wb), :] = wc_ref[0, pl.ds(g, wb, stride=2 * NSA_KV), :]
    vw_scr[pl.ds(0, wb), :] = wc_ref[0, pl.ds(NSA_KV + g, wb, stride=2 * NSA_KV), :]
    kw_scr[pl.ds(wb, tp), :] = kwn_ref[0]
    vw_scr[pl.ds(wb, tp), :] = vwn_ref[0]
    kw_scr[pl.ds(wb + tp, ww - wb - tp), :] = jnp.zeros((ww - wb - tp, LANE), F32)
    vw_scr[pl.ds(wb + tp, ww - wb - tp), :] = jnp.zeros((ww - wb - tp, LANE), F32)
    sw = _nt(qr, kw_scr[...].astype(BF16))
    jw = lax.broadcasted_iota(jnp.int32, (hq * tp, ww), 1)
    rel = jw - wb
    okw = (rel <= trow) & (rel > trow - WINDOW) & (jw < wb + ts)
    sw = jnp.where(okw, sw, MASK_NEG)
    ew = jnp.where(okw, jnp.exp(sw - jnp.max(sw, axis=-1, keepdims=True)), 0.0)
    o_w = _nn(ew.astype(BF16), vw_scr[...].astype(BF16)) / jnp.sum(ew, axis=-1, keepdims=True)

    for t in range(ts):
        for r in range(n_top):
            ck, cv = gather_copies(step, t, r, slot)
            ck.wait()
            cv.wait()

    js = lax.broadcasted_iota(jnp.int32, (hq * tp, ks_rows), 1)
    o_s = jnp.zeros((hq * tp, LANE), F32)
    for t in range(ts):
        kbuf[slot, t, pl.ds(n_gath, tp), :] = ksn_ref[0]
        vbuf[slot, t, pl.ds(n_gath, tp), :] = vsn_ref[0]
        kbuf[slot, t, pl.ds(n_gath + tp, ks_rows - n_gath - tp), :] = jnp.zeros((ks_rows - n_gath - tp, LANE), F32)
        vbuf[slot, t, pl.ds(n_gath + tp, ks_rows - n_gath - tp), :] = jnp.zeros((ks_rows - n_gath - tp, LANE), F32)
        ss = _nt(qr, kbuf[slot, t].astype(BF16))
        oks = (js < n_gath) | ((js - n_gath <= t) & (js < n_gath + ts))
        ss = jnp.where(oks, ss, MASK_NEG)
        es = jnp.where(oks, jnp.exp(ss - jnp.max(ss, axis=-1, keepdims=True)), 0.0)
        ot = _nn(es.astype(BF16), vbuf[slot, t].astype(BF16)) / jnp.sum(es, axis=-1, keepdims=True)
        o_s = jnp.where(trow == t, ot, o_s)

    o = gate_ref[0, 0, 0] * oc_ref[0, 0] + gate_ref[0, 0, 1] * o_s + gate_ref[0, 0, 2] * o_w
    y = _silu(ng_ref[0, 0]) * o
    for h in range(hq):
        o_ref[0, :, h * LANE:(h + 1) * LANE] = y[h * tp:(h + 1) * tp].astype(o_ref.dtype)


def _sample_attn(idx_flat, page_flat, qr, oc, ngo, gates, kvn, cache_win_rows, wn, cache4, *,
                 ts, n_top, n_pages, page_base, win_base):
    bs = qr.shape[0]
    tp = SAMPLE_T_PAD
    rows = NSA_HPG * tp
    wb = cache_win_rows.shape[1] // (2 * NSA_KV)
    per_page = cache4.shape[1] // SEL_BLOCK
    ks_rows = -(-(n_top * SEL_BLOCK + tp) // LANE) * LANE
    ww = -(-(wb + tp) // LANE) * LANE
    gspec = pl.BlockSpec((1, 1, rows, LANE), lambda bi, gi, *_: (bi, gi, 0, 0))
    newspec = lambda k: pl.BlockSpec((1, tp, LANE), lambda bi, gi, *_, k=k: (bi, 0, k + gi))
    grid_spec = pltpu.PrefetchScalarGridSpec(
        num_scalar_prefetch=2, grid=(bs, NSA_KV),
        in_specs=[gspec, gspec, gspec,
                  pl.BlockSpec((1, 1, N_BRANCH, rows, LANE), lambda bi, gi, *_: (bi, gi, 0, 0, 0)),
                  newspec(2 * NSA_KV), newspec(3 * NSA_KV),
                  pl.BlockSpec((1,) + cache_win_rows.shape[1:], lambda bi, gi, *_: (bi + win_base, 0, 0)),
                  newspec(0), newspec(NSA_KV),
                  pl.BlockSpec(memory_space=pl.ANY)],
        out_specs=pl.BlockSpec((1, tp, NSA_HPG * LANE), lambda bi, gi, *_: (bi, 0, gi)),
        scratch_shapes=[pltpu.VMEM((2, ts, ks_rows, LANE), F32), pltpu.VMEM((2, ts, ks_rows, LANE), F32),
                        pltpu.VMEM((ww, LANE), F32), pltpu.VMEM((ww, LANE), F32),
                        pltpu.SemaphoreType.DMA((2, 2))])
    kern = functools.partial(_sample_attn_kernel, ts=ts, n_top=n_top, n_pages=n_pages,
                             page_base=page_base, per_page=per_page, wb=wb)
    return pl.pallas_call(
        kern,
        out_shape=jax.ShapeDtypeStruct((bs, tp, NSA_HEADS * LANE), BF16),
        grid_spec=grid_spec,
        compiler_params=_params(("arbitrary", "arbitrary")),
        name="sample_attn",
    )(idx_flat, page_flat, qr, oc, ngo, gates, kvn, kvn, cache_win_rows, wn, wn, cache4)


def _rope_tables(pos, rows):
    freqs = jnp.power(ROPE_THETA, -jnp.arange(ROPE_HALF, dtype=F32) * (2.0 / ROPE_DIM))
    ang = pos.astype(F32)[:, None] * freqs[None, :]
    cos, sin = jnp.cos(ang), jnp.sin(ang)
    n = pos.shape[0]
    z16 = jnp.zeros((n, ROPE_HALF), F32)
    rest = LANE - ROPE_DIM
    c = jnp.concatenate([cos, cos, jnp.ones((n, rest), F32)], axis=1)
    s1 = jnp.concatenate([z16, sin, jnp.zeros((n, rest), F32)], axis=1)
    s2 = jnp.concatenate([-sin, z16, jnp.zeros((n, rest), F32)], axis=1)
    pad = lambda a: jnp.pad(a, ((0, rows - n), (0, 0)))
    return pad(c), pad(s1), pad(s2)


def _layer_weights(norm_g, w_in3, layer, w_conv, a_cmp, w_cmp, w_out):
    w_a, w_b, wbg = _pack_in_proj_weights(jnp.swapaxes(w_in3, 1, 2), layer)
    a4 = jnp.concatenate([a_cmp[0], a_cmp[0], a_cmp[1], a_cmp[1]], axis=1)
    a3 = jnp.stack([a_cmp[0], a_cmp[0], a_cmp[1], a_cmp[1]], axis=1)
    w4 = jnp.stack([w_cmp[0], w_cmp[0], w_cmp[1], w_cmp[1]]).astype(BF16)
    return dict(norm_g=norm_g, w_a=w_a, w_b=w_b, wbg=wbg, w_conv=w_conv, a4=a4, a3=a3, w4=w4,
                w_out=w_out.astype(BF16))


def _prompt_layer(xp, xs2, mem_prompt, mem_norm_g, w_mem, lw, final_g, final, pool):
    b, t, d = xp.shape
    m = b * t
    x2 = xp.reshape(m, d)
    tabs = _rope_tables(jnp.arange(t, dtype=jnp.int32), t)
    wb = min(WINDOW, t)
    qc, qr, ng, bg, ya, conv_new, za_s, bg_s = _proj_conv(x2, lw["norm_g"], lw["w_a"], lw["wbg"], lw["w_conv"], tabs,
                                                          xs2, tm=min(512, t), seq_len=t)
    kvn, kvb, win_rows, pooled_prompt, zm, zb_s = _proj_kv(x2, lw["norm_g"], lw["w_b"], tabs, lw["a4"], xs2,
                                                           tm=wb, seq_len=t)
    nm = mem_prompt.shape[1]
    mkv = _norm_matmul(mem_prompt.reshape(b * nm, d), mem_norm_g, w_mem, tm=min(512, b * nm), tn=512)
    mkv3 = mkv.reshape(b, nm, 2 * MEM_HEADS * LANE)
    kc, vc = _cmp_proj(pooled_prompt.reshape(b, t // CMP_BLOCK, -1), lw["w4"])
    r3 = lambda a: a.reshape(b, t, -1)
    yb, pooled = _nsa_prompt(r3(qc), r3(qr), r3(ng), r3(bg), r3(kvb), kc, vc, tq=min(256, t), pool=pool)
    ym = _mem_attn(zm.reshape(b, t, -1), mkv3, mq_off=0, mg_off=MG_OFF - MQ_OFF, tq=min(512, t), interleaved=False)
    out = _out_proj(x2, ya, yb.reshape(m, -1), ym.reshape(m, -1),
                    lw["w_out"], final_g, tm=min(512, m), final=final)
    kv_new = kvn.reshape(b, t, 4, NSA_KV, HEAD_DIM)
    win_new = win_rows.reshape(b, wb, 2, NSA_KV, HEAD_DIM)
    mem_kv = mkv.reshape(b, nm, 2, MEM_HEADS, HEAD_DIM)
    return out.reshape(b, t, d), kv_new, win_new, conv_new, mem_kv, pooled, (za_s, zb_s, bg_s)


def _sample_layer(xs_p, proj, ts, layer, cache4, page_flat, pooled, cache_win, state_conv, cache_mem, lw, final_g,
                  final):
    bs, tp, d = xs_p.shape
    depth = cache_win.shape[0]
    pool, page = cache4.shape[0] // depth, cache4.shape[1]
    n_pages = page_flat.shape[0] // bs
    past = n_pages * page
    assert past % SEL_BLOCK == 0 and ts <= SEL_BLOCK and ts <= tp
    n_past = past // SEL_BLOCK
    n_top = min(TOP_N, n_past + 1) - 1
    m = bs * tp
    za, zb, bg = proj
    za3 = za.reshape(bs, tp, ZA_WIDTH)
    zb3 = zb.reshape(bs, tp, ZB_WIDTH)
    bg3 = bg.reshape(bs, tp, LANE)
    tabs = _rope_tables(past + jnp.arange(tp, dtype=jnp.int32), tp)
    if pooled is None:
        pooled = _pool_pages(cache4, page_flat, lw["a3"], bs=bs, n_pages=n_pages, page_base=layer * pool)
    else:
        pooled = pooled.reshape(bs, n_pages * (page // CMP_BLOCK), 2 * NSA_KV * LANE)
    kc, vc = _cmp_proj(pooled, lw["w4"])
    kvn, wn, qr, oc, ngo, gates, val = _sample_pre(za3, zb3, bg3, tabs, kc, vc, past=past)
    idx = _topk(val.reshape(bs * NSA_KV * tp, val.shape[-1]), n_top=n_top)
    idx_flat = idx.reshape(bs, NSA_KV, tp, LANE)[:, :, :ts, :n_top].reshape(-1)
    wbuf = cache_win.shape[2]
    cache_win_rows = cache_win.reshape(depth * bs, wbuf * 2 * NSA_KV, HEAD_DIM)
    yb = _sample_attn(idx_flat, page_flat, qr, oc, ngo, gates, kvn, cache_win_rows, wn, cache4,
                      ts=ts, n_top=n_top, n_pages=n_pages, page_base=layer * pool, win_base=layer * bs)
    ya, conv_new = _conv_mixer(za3, state_conv[layer], lw["w_conv"], t_real=ts)
    nm = cache_mem.shape[2]
    mem_rows = cache_mem.reshape(depth * bs, nm * 2 * MEM_HEADS, HEAD_DIM)
    ym = _mem_attn(zb3, mem_rows, mq_off=MQ_OFF, mg_off=MG_OFF, tq=tp, interleaved=True, kv_base=layer * bs)
    out = _out_proj(xs_p.reshape(m, d), ya.reshape(m, -1), yb.reshape(m, -1), ym.reshape(m, -1),
                    lw["w_out"], final_g, tm=m, final=final)
    kv_new = kvn[:, :ts].reshape(bs, ts, 4, NSA_KV, HEAD_DIM)
    win_rows = wn[:, :ts].reshape(bs, ts, 2, NSA_KV, HEAD_DIM)
    win_state = jnp.concatenate([cache_win[layer], win_rows], axis=1)[:, ts:]
    return out.reshape(bs, tp, d), kv_new, win_state, conv_new


def kernel(x_prompt, x_sample, cache_kv, cache_win, state_conv, cache_mem, page_table, mem_prompt,
           norm_g, w_in, w_conv, a_cmp, w_cmp, mem_norm_g, w_mem_kv, w_out, final_g):
    depth = w_in.shape[0]
    ts = x_sample.shape[1]
    xp = x_prompt
    xs = jnp.pad(x_sample, ((0, 0), (0, SAMPLE_T_PAD - ts), (0, 0)))
    pool_size, page = cache_kv.shape[1], cache_kv.shape[2]
    cache4 = cache_kv.reshape(depth * pool_size, page, 4 * NSA_KV, HEAD_DIM)
    page_flat = page_table.reshape(-1).astype(jnp.int32)
    kv_p, win_p, conv_p, mem_p, kv_s, win_s, conv_s = [], [], [], [], [], [], []
    for l in range(depth):
        lw = _layer_weights(norm_g[l], w_in, l, w_conv[l], a_cmp[l], w_cmp[l], w_out[l])
        final = l == depth - 1
        xp, kvn, winn, convn, mkv, pooled, proj_s = _prompt_layer(
            xp, xs.reshape(-1, xs.shape[-1]), mem_prompt, mem_norm_g[l], w_mem_kv[l].T.astype(BF16), lw, final_g, final,
            pool=(cache4, page_flat, lw["a3"], l * pool_size))
        kv_p.append(kvn)
        win_p.append(winn)
        conv_p.append(convn)
        mem_p.append(mkv)
        xs, kvn, winn, convn = _sample_layer(xs, proj_s, ts, l, cache4, page_flat, pooled, cache_win, state_conv,
                                             cache_mem, lw, final_g, final)
        kv_s.append(kvn)
        win_s.append(winn)
        conv_s.append(convn)
    return (xp, xs[:, :ts], jnp.stack(kv_p), jnp.stack(win_p), jnp.stack(conv_p), jnp.stack(mem_p),
            jnp.stack(kv_s), jnp.stack(win_s), jnp.stack(conv_s))
```

```python
import functools

import jax
import jax.numpy as jnp
from jax import lax
from jax.experimental import pallas as pl
from jax.experimental.pallas import tpu as pltpu

F32 = jnp.float32
BF16 = jnp.bfloat16

HEAD_DIM = 128
CONV_DIM = 512
CONV_W = 3
NSA_HEADS = 8
NSA_KV = 2
NSA_HPG = NSA_HEADS // NSA_KV
MEM_HEADS = 4
N_BRANCH = 3
ROPE_DIM = HEAD_DIM // 4
ROPE_HALF = ROPE_DIM // 2
ROPE_THETA = 500000.0
CMP_BLOCK = 64
SEL_BLOCK = 64
SEL_SHIFT = 6
TOP_N = 16
WINDOW = 512
NORM_EPS = 1e-6
MASK_NEG = -1e30
FORCE = 1e9
ATTN_SCALE = HEAD_DIM ** -0.5
SCALE_LOG2 = ATTN_SCALE * 1.4426950408889634

C_H, C_B, C_C, C_G, Q_OFF, NG_OFF = 0, 512, 1024, 1536, 2048, 3072
ZA_WIDTH = 4096
KV_OFF, MQ_OFF, MG_OFF = 0, 1536, 2048
ZB_WIDTH = 2560
BG_SRC = 4096
BG_N = NSA_HEADS * N_BRANCH
LANE = 128
SAMPLE_T_PAD = 8
VMEM_LIMIT = 56 * 1024 * 1024


def _nt(a, b):
    return lax.dot_general(a, b, (((1,), (1,)), ((), ())), preferred_element_type=F32)


def _nn(a, b):
    return jnp.dot(a, b, preferred_element_type=F32)


def _params(sem, vmem=VMEM_LIMIT):
    return pltpu.CompilerParams(dimension_semantics=sem, vmem_limit_bytes=vmem)


def _rope(x, c, s1, s2):
    return x * c + pltpu.roll(x, ROPE_HALF, 1) * s1 + pltpu.roll(x, LANE - ROPE_HALF, 1) * s2


def _silu(x):
    return x * jax.nn.sigmoid(x)


def _norm_matmul_kernel(x_ref, g_ref, w_ref, *rest, with_gate, tn):
    if with_gate:
        wbg_ref, z_ref, bg_ref, h_scr = rest
    else:
        z_ref, h_scr = rest
    x = x_ref[...]
    y = x * lax.rsqrt(jnp.mean(x * x, axis=-1, keepdims=True) + NORM_EPS) * g_ref[...]
    h_scr[...] = y.astype(BF16)
    if with_gate:
        bg_ref[...] = _nt(h_scr[...], wbg_ref[...])
    for j in range(w_ref.shape[0] // tn):
        z_ref[:, j * tn:(j + 1) * tn] = _nt(h_scr[...], w_ref[j * tn:(j + 1) * tn, :])


def _norm_matmul(x, g, w, wbg=None, *, tm, tn):
    m, d = x.shape
    n = w.shape[0]
    assert n % tn == 0 and m % tm == 0 and w.shape[1] == d
    with_gate = wbg is not None
    resident = lambda shape: pl.BlockSpec(shape, lambda i: (0, 0), pipeline_mode=pl.Buffered(1))
    in_specs = [pl.BlockSpec((tm, d), lambda i: (i, 0)), resident((1, d)), resident((n, d))]
    out_shape = [jax.ShapeDtypeStruct((m, n), F32)]
    out_specs = [pl.BlockSpec((tm, n), lambda i: (i, 0))]
    args = [x, g.reshape(1, d), w]
    if with_gate:
        nb = wbg.shape[0]
        in_specs.append(resident((nb, d)))
        out_shape.append(jax.ShapeDtypeStruct((m, nb), F32))
        out_specs.append(pl.BlockSpec((tm, nb), lambda i: (i, 0)))
        args.append(wbg)
    res = pl.pallas_call(
        functools.partial(_norm_matmul_kernel, with_gate=with_gate, tn=tn),
        out_shape=out_shape, grid=(m // tm,), in_specs=in_specs, out_specs=out_specs,
        scratch_shapes=[pltpu.VMEM((tm, d), BF16)],
        compiler_params=_params(("parallel",)),
        name="norm_matmul_gate" if with_gate else "norm_matmul",
    )(*args)
    return res if with_gate else res[0]


def _rms_rows(x_ref, g_ref):
    x = x_ref[...]
    return (x * lax.rsqrt(jnp.mean(x * x, axis=-1, keepdims=True) + NORM_EPS) * g_ref[...]).astype(BF16)


def _proj_conv_kernel(x_ref, g_ref, w_ref, wbg_ref, wc_ref, c_ref, s1_ref, s2_ref, xs_ref,
                      qc_ref, qr_ref, ng_ref, bg_ref, ya_ref, st_ref, zs_ref, bgs_ref, h_scr, up_scr,
                      *, blocks_per_seq):
    tm = x_ref.shape[0]
    cw = CONV_DIM
    i = pl.program_id(0)
    first = i % blocks_per_seq == 0
    h_scr[...] = _rms_rows(x_ref, g_ref)
    bg_ref[...] = _nt(h_scr[...], wbg_ref[...])
    chunk = lambda off: _nt(h_scr[...], w_ref[off:off + cw, :])

    @pl.when(first)
    def _():
        up_scr[pl.ds(8 - (CONV_W - 1), CONV_W - 1), :] = jnp.zeros((CONV_W - 1, cw), F32)

    @pl.when(jnp.logical_not(first))
    def _():
        up_scr[pl.ds(8 - (CONV_W - 1), CONV_W - 1), :] = up_scr[pl.ds(8 + tm - (CONV_W - 1), CONV_W - 1), :]

    u = chunk(C_C) * chunk(C_H)
    up_scr[pl.ds(8, tm), :] = u
    wc = wc_ref[...]
    y = wc[0:1, :] * up_scr[pl.ds(6, tm), :]
    y = y + wc[1:2, :] * up_scr[pl.ds(7, tm), :]
    y = y + wc[2:3, :] * u
    y = chunk(C_B) * y
    ya_ref[...] = (_silu(chunk(C_G)) * y).astype(ya_ref.dtype)

    @pl.when(i % blocks_per_seq == blocks_per_seq - 1)
    def _():
        st_ref[0] = up_scr[pl.ds(8 + tm - (CONV_W - 1), CONV_W - 1), :]

    c, s1, s2 = c_ref[...], s1_ref[...], s2_ref[...]
    for j in range((NG_OFF - Q_OFF) // cw):
        qv = chunk(Q_OFF + j * cw)
        for k in range(cw // LANE):
            lo = j * cw + k * LANE
            qh = qv[:, k * LANE:(k + 1) * LANE]
            qc_ref[:, lo:lo + LANE] = (qh * SCALE_LOG2).astype(BF16)
            qr_ref[:, lo:lo + LANE] = (_rope(qh, c, s1, s2) * SCALE_LOG2).astype(BF16)
    for j in range((ZA_WIDTH - NG_OFF) // cw):
        ng_ref[:, j * cw:(j + 1) * cw] = _silu(chunk(NG_OFF + j * cw))

    @pl.when(i == pl.num_programs(0) - 1)
    def _():
        hs = _rms_rows(xs_ref, g_ref)
        bgs_ref[...] = _nt(hs, wbg_ref[...])
        for j in range(ZA_WIDTH // cw):
            zs_ref[:, j * cw:(j + 1) * cw] = _nt(hs, w_ref[j * cw:(j + 1) * cw, :])


def _proj_conv(x2, g, w_a, wbg, w_conv, tabs, xs2, *, tm, seq_len):
    m, d = x2.shape
    ms = xs2.shape[0]
    assert seq_len % tm == 0 and CONV_W == 3
    blocks_per_seq = seq_len // tm
    nq, nng = NG_OFF - Q_OFF, ZA_WIDTH - NG_OFF
    resident = lambda shape: pl.BlockSpec(shape, lambda i: (0,) * len(shape), pipeline_mode=pl.Buffered(1))
    row = lambda width: pl.BlockSpec((tm, width), lambda i: (i, 0))
    tspec = pl.BlockSpec((tm, LANE), lambda i: (i % blocks_per_seq, 0))
    return pl.pallas_call(
        functools.partial(_proj_conv_kernel, blocks_per_seq=blocks_per_seq),
        out_shape=[jax.ShapeDtypeStruct((m, nq), BF16), jax.ShapeDtypeStruct((m, nq), BF16),
                   jax.ShapeDtypeStruct((m, nng), F32), jax.ShapeDtypeStruct((m, LANE), F32),
                   jax.ShapeDtypeStruct((m, CONV_DIM), BF16),
                   jax.ShapeDtypeStruct((m // seq_len, CONV_W - 1, CONV_DIM), F32),
                   jax.ShapeDtypeStruct((ms, ZA_WIDTH), F32), jax.ShapeDtypeStruct((ms, LANE), F32)],
        grid=(m // tm,),
        in_specs=[row(d), resident((1, d)), resident(w_a.shape), resident(wbg.shape), resident(w_conv.shape),
                  tspec, tspec, tspec, resident((ms, d))],
        out_specs=[row(nq), row(nq), row(nng), row(LANE), row(CONV_DIM),
                   pl.BlockSpec((1, CONV_W - 1, CONV_DIM), lambda i: (i // blocks_per_seq, 0, 0)),
                   pl.BlockSpec((ms, ZA_WIDTH), lambda i: (0, 0)), pl.BlockSpec((ms, LANE), lambda i: (0, 0))],
        scratch_shapes=[pltpu.VMEM((tm, d), BF16), pltpu.VMEM((tm + 8, CONV_DIM), F32)],
        compiler_params=_params(("arbitrary",)),
        name="proj_conv",
    )(x2, g.reshape(1, d), w_a, wbg, w_conv, *tabs, xs2)


def _proj_kv_kernel(x_ref, g_ref, w_ref, c_ref, s1_ref, s2_ref, a_ref, xs_ref,
                    kvn_ref, kvb_ref, win_ref, pool_ref, zm_ref, zs_ref, h_scr, *, blocks_per_seq):
    tm = x_ref.shape[0]
    cw = 2 * NSA_KV * LANE
    h_scr[...] = _rms_rows(x_ref, g_ref)
    chunk = lambda off: _nt(h_scr[...], w_ref[off:off + cw, :])
    c, s1, s2 = c_ref[...], s1_ref[...], s2_ref[...]
    kv0 = chunk(KV_OFF)
    kv1 = chunk(KV_OFF + cw)
    kv2 = chunk(KV_OFF + 2 * cw)
    n_kv, n_w, half = 4 * NSA_KV, 2 * NSA_KV, NSA_KV * LANE
    last = pl.program_id(0) % blocks_per_seq == blocks_per_seq - 1
    for g in range(NSA_KV):
        lo, hi = g * LANE, (g + 1) * LANE
        ks = _rope(kv1[:, lo:hi], c, s1, s2)
        kw = _rope(kv2[:, lo:hi], c, s1, s2)
        vs = kv1[:, half + lo:half + hi]
        vw = kv2[:, half + lo:half + hi]
        kvn_ref[pl.ds(g, tm, stride=n_kv), :] = kv0[:, lo:hi]
        kvn_ref[pl.ds(NSA_KV + g, tm, stride=n_kv), :] = kv0[:, half + lo:half + hi]
        kvn_ref[pl.ds(2 * NSA_KV + g, tm, stride=n_kv), :] = ks
        kvn_ref[pl.ds(3 * NSA_KV + g, tm, stride=n_kv), :] = vs
        kvb_ref[:, lo:hi] = ks.astype(BF16)
        kvb_ref[:, half + lo:half + hi] = vs.astype(BF16)
        kvb_ref[:, 2 * half + lo:2 * half + hi] = kw.astype(BF16)
        kvb_ref[:, 3 * half + lo:3 * half + hi] = vw.astype(BF16)

        @pl.when(last)
        def _():
            win_ref[pl.ds(g, tm, stride=n_w), :] = kw
            win_ref[pl.ds(NSA_KV + g, tm, stride=n_w), :] = vw

    pool_ref[...] = jnp.sum(kv0.reshape(tm // CMP_BLOCK, CMP_BLOCK, cw) * a_ref[...][None], axis=1)
    for j in range(zm_ref.shape[1] // cw):
        zm_ref[:, j * cw:(j + 1) * cw] = chunk(MQ_OFF + j * cw)

    @pl.when(pl.program_id(0) == pl.num_programs(0) - 1)
    def _():
        hs = _rms_rows(xs_ref, g_ref)
        for j in range(ZB_WIDTH // cw):
            zs_ref[:, j * cw:(j + 1) * cw] = _nt(hs, w_ref[j * cw:(j + 1) * cw, :])


def _proj_kv(x2, g, w_b, tabs, a4, xs2, *, tm, seq_len):
    m, d = x2.shape
    ms = xs2.shape[0]
    assert seq_len % tm == 0
    blocks_per_seq = seq_len // tm
    n_kv, n_w = 4 * NSA_KV, 2 * NSA_KV
    nm = ZB_WIDTH - MQ_OFF
    resident = lambda shape: pl.BlockSpec(shape, lambda i: (0,) * len(shape), pipeline_mode=pl.Buffered(1))
    row = lambda width: pl.BlockSpec((tm, width), lambda i: (i, 0))
    tspec = pl.BlockSpec((tm, LANE), lambda i: (i % blocks_per_seq, 0))
    return pl.pallas_call(
        functools.partial(_proj_kv_kernel, blocks_per_seq=blocks_per_seq),
        out_shape=[jax.ShapeDtypeStruct((m * n_kv, LANE), F32),
                   jax.ShapeDtypeStruct((m, n_kv * LANE), BF16),
                   jax.ShapeDtypeStruct((m // seq_len * tm * n_w, LANE), F32),
                   jax.ShapeDtypeStruct((m // CMP_BLOCK, 2 * NSA_KV * LANE), F32),
                   jax.ShapeDtypeStruct((m, nm), F32), jax.ShapeDtypeStruct((ms, ZB_WIDTH), F32)],
        grid=(m // tm,),
        in_specs=[row(d), resident((1, d)), resident(w_b.shape), tspec, tspec, tspec, resident(a4.shape),
                  resident((ms, d))],
        out_specs=[pl.BlockSpec((tm * n_kv, LANE), lambda i: (i, 0)), row(n_kv * LANE),
                   pl.BlockSpec((tm * n_w, LANE), lambda i: (i // blocks_per_seq, 0)),
                   pl.BlockSpec((tm // CMP_BLOCK, 2 * NSA_KV * LANE), lambda i: (i, 0)), row(nm),
                   pl.BlockSpec((ms, ZB_WIDTH), lambda i: (0, 0))],
        scratch_shapes=[pltpu.VMEM((tm, d), BF16)],
        compiler_params=_params(("arbitrary",)),
        name="proj_kv",
    )(x2, g.reshape(1, d), w_b, *tabs, a4, xs2)


def _cast_rows_kernel(w_ref, o_ref):
    o_ref[...] = w_ref[0].astype(BF16)


def _pack_gate_kernel(w_ref, o_ref):
    o_ref[...] = jnp.zeros(o_ref.shape, BF16)
    o_ref[0:w_ref.shape[1], :] = w_ref[0].astype(BF16)


def _pack_in_proj_weights(w_t3, layer):
    _, n_in, d = w_t3.shape
    tn = 512
    assert n_in == BG_SRC + BG_N + ZB_WIDTH and BG_SRC % tn == 0 and ZB_WIDTH % tn == 0 and BG_N % 8 == 0

    def cast_rows(first_row, n_rows, name):
        return pl.pallas_call(
            _cast_rows_kernel,
            out_shape=jax.ShapeDtypeStruct((n_rows, d), BF16),
            grid=(n_rows // tn,),
            in_specs=[pl.BlockSpec((pl.Element(1), pl.Element(tn), pl.Element(d)),
                                   lambda j: (layer, pl.multiple_of(first_row + j * tn, 8), 0))],
            out_specs=pl.BlockSpec((tn, d), lambda j: (j, 0)),
            compiler_params=_params(("parallel",)),
            name=name,
        )(w_t3)

    wa = cast_rows(0, ZA_WIDTH, "pack_w_a")
    wb = cast_rows(BG_SRC + BG_N, ZB_WIDTH, "pack_w_b")
    wbg = pl.pallas_call(
        _pack_gate_kernel,
        out_shape=jax.ShapeDtypeStruct((LANE, d), BF16),
        grid=(1,),
        in_specs=[pl.BlockSpec((pl.Element(1), pl.Element(BG_N), pl.Element(d)), lambda j: (layer, BG_SRC, 0))],
        out_specs=pl.BlockSpec((LANE, d), lambda j: (0, 0)),
        compiler_params=_params(("arbitrary",)),
        name="pack_w_gate",
    )(w_t3)
    return wa, wb, wbg


def _pool_pages_kernel(pt_sm, a_ref, cache_ref, o_ref, buf, sem, *, pages_per_step, page_base):
    step = pl.program_id(0)
    n_steps = pl.num_programs(0)
    slot = step % 2
    n_cols = buf.shape[3]

    def page_copy(step_idx, p, to_slot):
        page = pt_sm[step_idx * pages_per_step + p] + page_base
        return pltpu.make_async_copy(cache_ref.at[page, :, pl.ds(0, n_cols), :], buf.at[to_slot, p], sem.at[to_slot])

    @pl.when(step == 0)
    def _():
        for p in range(pages_per_step):
            page_copy(0, p, 0).start()

    @pl.when(step + 1 < n_steps)
    def _():
        for p in range(pages_per_step):
            page_copy(step + 1, p, 1 - slot).start()

    for p in range(pages_per_step):
        page_copy(step, p, slot).wait()

    a = a_ref[...]
    per = buf.shape[2] // CMP_BLOCK
    for p in range(pages_per_step):
        for k in range(per):
            x = buf[slot, p, pl.ds(k * CMP_BLOCK, CMP_BLOCK)]
            o_ref[0, p, k] = jnp.sum(x * a, axis=0)


def _pool_pages(cache4, page_flat, a3, *, bs, n_pages, page_base, pages_per_step=16):
    page = cache4.shape[1]
    per = page // CMP_BLOCK
    n_cols = 2 * NSA_KV
    total = bs * n_pages
    pages_per_step = min(pages_per_step, total)
    assert total % pages_per_step == 0
    n_steps = total // pages_per_step
    grid_spec = pltpu.PrefetchScalarGridSpec(
        num_scalar_prefetch=1, grid=(n_steps,),
        in_specs=[pl.BlockSpec((CMP_BLOCK, n_cols, LANE), lambda si, pt: (0, 0, 0)),
                  pl.BlockSpec(memory_space=pl.ANY)],
        out_specs=pl.BlockSpec((1, pages_per_step, per, n_cols, LANE), lambda si, pt: (si, 0, 0, 0, 0)),
        scratch_shapes=[pltpu.VMEM((2, pages_per_step, page, n_cols, LANE), F32),
                        pltpu.SemaphoreType.DMA((2,))])
    out = pl.pallas_call(
        functools.partial(_pool_pages_kernel, pages_per_step=pages_per_step, page_base=page_base),
        out_shape=jax.ShapeDtypeStruct((n_steps, pages_per_step, per, n_cols, LANE), F32),
        grid_spec=grid_spec,
        compiler_params=_params(("arbitrary",)),
        name="pool_pages",
    )(page_flat, a3, cache4)
    return out.reshape(bs, n_pages * per, n_cols * LANE)


def _cmp_proj_kernel(p_ref, w_ref, kc_ref, vc_ref):
    pooled = p_ref[0]
    n = pooled.shape[0]
    n_pad = kc_ref.shape[2]
    for c in range(4):
        r = _nn(pooled[:, c * LANE:(c + 1) * LANE].astype(BF16), w_ref[c]).astype(BF16)
        dst = kc_ref if c < 2 else vc_ref
        if n_pad > n:
            dst[0, c % 2] = jnp.zeros((n_pad, LANE), BF16)
        dst[0, c % 2, 0:n, :] = r


def _cmp_proj(pooled, w4):
    b, n, _ = pooled.shape
    n_pad = -(-n // LANE) * LANE
    spec = pl.BlockSpec((1, NSA_KV, n_pad, LANE), lambda bi: (bi, 0, 0, 0))
    return pl.pallas_call(
        _cmp_proj_kernel,
        out_shape=[jax.ShapeDtypeStruct((b, NSA_KV, n_pad, LANE), BF16)] * 2,
        grid=(b,),
        in_specs=[pl.BlockSpec((1, n, 512), lambda bi: (bi, 0, 0)),
                  pl.BlockSpec((4, LANE, LANE), lambda bi: (0, 0, 0))],
        out_specs=[spec, spec],
        compiler_params=_params(("parallel",)),
        name="cmp_proj",
    )(pooled, w4)


def _lane_parts(x):
    return [x[:, j * LANE:(j + 1) * LANE] for j in range(x.shape[1] // LANE)]


def _nsa_prompt_kernel(*refs, t_len, tq, tc, tw, n_sel, top, sub, pool_pages, page_base):
    if pool_pages:
        (pt_sm, qc_ref, qr_ref, ng_ref, bg_ref, ksel_ref, vsel_ref, kwin_ref, vwin_ref, kc_ref, vc_ref,
         pa_ref, cache_ref, o_ref, pool_ref,
         qc_scr, qr_scr, s_scr, p_scr, a_scr, m_scr, l_scr, acc_scr, oc_scr, b_scr, pbuf, psem) = refs
        step = (pl.program_id(0) * pl.num_programs(1) + pl.program_id(1)) * pl.num_programs(2) + pl.program_id(2)
        n_steps = pl.num_programs(0) * pl.num_programs(1) * pl.num_programs(2)
        slot = step % 2
        half_rows, n_cols = pbuf.shape[2], 2 * NSA_KV

        def page_copies(step_idx, p, to_slot):
            page = pt_sm[step_idx * pool_pages + p] + page_base
            return [pltpu.make_async_copy(cache_ref.at[page, pl.ds(hh * half_rows, half_rows), pl.ds(0, n_cols), :],
                                          pbuf.at[to_slot, p, :, pl.ds(hh * n_cols, n_cols), :], psem.at[to_slot])
                    for hh in range(2)]

        @pl.when(step == 0)
        def _():
            for p in range(pool_pages):
                for cp in page_copies(0, p, 0):
                    cp.start()

        @pl.when(step + 1 < n_steps)
        def _():
            for p in range(pool_pages):
                for cp in page_copies(step + 1, p, 1 - slot):
                    cp.start()
    else:
        (qc_ref, qr_ref, ng_ref, bg_ref, ksel_ref, vsel_ref, kwin_ref, vwin_ref, kc_ref, vc_ref,
         o_ref, qc_scr, qr_scr, s_scr, p_scr, a_scr, m_scr, l_scr, acc_scr, oc_scr, b_scr) = refs
    i = pl.program_id(2)
    hq = NSA_HPG
    for h in range(hq):
        qc_scr[pl.ds(h * tq, tq), :] = qc_ref[0, :, h * LANE:(h + 1) * LANE]
        qr_scr[pl.ds(h * tq, tq), 0:LANE] = qr_ref[0, :, h * LANE:(h + 1) * LANE]

    kc = kc_ref[0, 0]
    npad = kc.shape[0]
    s_scr[:, 0:npad] = _nt(qc_scr[...], kc)
    tpos = i * tq + lax.broadcasted_iota(jnp.int32, (tq, npad), 0)
    ncol = lax.broadcasted_iota(jnp.int32, (tq, npad), 1)
    cmask = (ncol + 1) * CMP_BLOCK <= tpos + 1
    imp = jnp.zeros((tq, npad), F32)
    for h in range(hq):
        r = pl.ds(h * tq, tq)
        s = jnp.where(cmask, s_scr[r, 0:npad], MASK_NEG)
        e = jnp.where(cmask, jnp.exp2(s - jnp.max(s, axis=-1, keepdims=True)), 0.0)
        p = e / jnp.maximum(jnp.sum(e, axis=-1, keepdims=True), 1e-30)
        p_scr[r, 0:npad] = p.astype(BF16)
        imp = imp + p
    oc_scr[...] = _nn(p_scr[:, 0:npad], vc_ref[0, 0])

    rows = min(npad, -(-n_sel // 8) * 8)
    imp_t = imp.T[0:rows]
    blk = lax.broadcasted_iota(jnp.int32, (rows, tq), 0)
    cur = lax.shift_right_logical(i * tq + lax.broadcasted_iota(jnp.int32, (rows, tq), 1), SEL_SHIFT)
    imp_t = jnp.where((blk == 0) | (blk == cur) | (blk == cur - 1), FORCE, imp_t)
    imp_t = jnp.where(blk > cur, -1.0, imp_t)
    imp_t = jnp.where(blk >= n_sel, -2.0, imp_t)
    rank = jnp.zeros((rows, tq), F32)
    for j in range(n_sel):
        a = imp_t[j:j + 1, :]
        ahead = (a > imp_t) | ((a == imp_t) & (blk > j))
        rank = rank + jnp.where(ahead, 1.0, 0.0)
    neg_t = jnp.where((rank < top) & (blk < n_sel) & (blk <= cur), 0.0, MASK_NEG)
    if npad > rows:
        neg_t = jnp.concatenate([neg_t, jnp.zeros((npad - rows, tq), F32)], axis=0)
    neg = neg_t.T.astype(BF16)
    for h in range(hq):
        qr_scr[pl.ds(h * tq, tq), LANE:2 * LANE] = neg

    m_scr[...] = jnp.full(m_scr.shape, MASK_NEG, F32)
    l_scr[...] = jnp.zeros(l_scr.shape, F32)
    acc_scr[...] = jnp.zeros(acc_scr.shape, F32)
    n_chunks = (i * tq + tq + tc - 1) // tc

    def chunk(ci, diagonal):
        k0 = pl.multiple_of(ci * tc, tc)
        v = vsel_ref[0, pl.ds(k0, tc), :]
        kb = lax.shift_right_logical(k0 + lax.broadcasted_iota(jnp.int32, (tc, npad), 0), SEL_SHIFT)
        jb = lax.broadcasted_iota(jnp.int32, (tc, npad), 1)
        k_aug = jnp.concatenate([ksel_ref[0, pl.ds(k0, tc), :], jnp.where(kb == jb, 1.0, 0.0).astype(BF16)], axis=1)
        s_scr[:, 0:tc] = _nt(qr_scr[...], k_aug)
        if diagonal:
            kp = k0 + lax.broadcasted_iota(jnp.int32, (tq, tc), 1)
            tp = i * tq + lax.broadcasted_iota(jnp.int32, (tq, tc), 0)
            b_scr[:, 0:tc] = jnp.where(kp <= tp, 0.0, MASK_NEG)
        for r0 in range(0, hq * tq, sub):
            r = pl.ds(r0, sub)
            sc = s_scr[r, 0:tc]
            if diagonal:
                sc = sc + b_scr[pl.ds(r0 % tq, sub), 0:tc]
            parts = _lane_parts(sc)
            m_prev = m_scr[r, :]
            m_new = jnp.maximum(m_prev, jnp.max(functools.reduce(jnp.maximum, parts), axis=-1, keepdims=True))
            alpha = jnp.exp2(m_prev - m_new)
            pes = [jnp.exp2(x - m_new) for x in parts]
            l_scr[r, :] = alpha * l_scr[r, :] + functools.reduce(jnp.add, pes)
            p_scr[r, 0:tc] = jnp.concatenate(pes, axis=1).astype(BF16)
            a_scr[r, :] = alpha
            m_scr[r, :] = m_new
        acc_scr[...] = a_scr[...] * acc_scr[...] + _nn(p_scr[:, 0:tc], v)

    def full_chunk(ci, carry):
        chunk(ci, False)
        return carry

    lax.fori_loop(0, n_chunks - 1, full_chunk, 0)
    chunk(n_chunks - 1, True)

    w0 = pl.multiple_of(jnp.clip(i * tq + tq - tw, 0, t_len - tw), LANE)
    kp = w0 + lax.broadcasted_iota(jnp.int32, (tq, tw), 1)
    tp = i * tq + lax.broadcasted_iota(jnp.int32, (tq, tw), 0)
    b_scr[:, 0:tw] = jnp.where((kp <= tp) & (kp > tp - WINDOW), 0.0, MASK_NEG)
    s_scr[:, 0:tw] = _nt(qr_scr[:, 0:LANE], kwin_ref[0, pl.ds(w0, tw), :])
    for r0 in range(0, hq * tq, sub):
        r = pl.ds(r0, sub)
        parts = _lane_parts(s_scr[r, 0:tw] + b_scr[pl.ds(r0 % tq, sub), 0:tw])
        m = jnp.max(functools.reduce(jnp.maximum, parts), axis=-1, keepdims=True)
        ews = [jnp.exp2(x - m) for x in parts]
        row_sum = jnp.sum(functools.reduce(jnp.add, ews), axis=-1, keepdims=True)
        a_scr[r, :] = jnp.broadcast_to(1.0 / row_sum, (sub, LANE))
        p_scr[r, 0:tw] = jnp.concatenate(ews, axis=1).astype(BF16)
    o_w = _nn(p_scr[:, 0:tw], vwin_ref[0, pl.ds(w0, tw), :])

    gate = jax.nn.sigmoid(bg_ref[0])
    for g in range(1, NSA_KV):
        gate = jnp.where(pl.program_id(1) == g, pltpu.roll(gate, LANE - g * hq * N_BRANCH, 1), gate)
    ng = ng_ref[0]
    for h in range(hq):
        r = slice(h * tq, (h + 1) * tq)
        o_s = acc_scr[r, :] * (1.0 / jnp.sum(l_scr[r, :], axis=-1, keepdims=True))
        o = (gate[:, 3 * h:3 * h + 1] * oc_scr[r, :] + gate[:, 3 * h + 1:3 * h + 2] * o_s
             + gate[:, 3 * h + 2:3 * h + 3] * (o_w[r] * a_scr[r, :]))
        o_ref[0, :, h * LANE:(h + 1) * LANE] = (ng[:, h * LANE:(h + 1) * LANE] * o).astype(o_ref.dtype)

    if pool_pages:
        for p in range(pool_pages):
            for cp in page_copies(step, p, slot):
                cp.wait()
        pa = pa_ref[...]
        for p in range(pool_pages):
            pool_ref[0, p] = jnp.sum(pbuf[slot, p] * pa, axis=0)


def _nsa_prompt(qc3, qr3, ng3, bg3, kvb, kc, vc, *, tq=128, tc=512, sub=64, pool=None):
    b, t, _ = qc3.shape
    nq = t // tq
    tc = min(tc, t)
    tw = min(WINDOW + tq, t)
    n_sel = t // SEL_BLOCK
    top = min(TOP_N, n_sel)
    npad = kc.shape[2]
    gw = NSA_HPG * LANE
    rows = NSA_HPG * tq
    wide = max(tc, tw, npad)
    kvspec = lambda k: pl.BlockSpec((1, t, LANE), lambda bi, gi, qi, *_, k=k: (bi, 0, k + gi))
    cspec = pl.BlockSpec((1, 1, npad, LANE), lambda bi, gi, qi, *_: (bi, gi, 0, 0))
    gspec = pl.BlockSpec((1, tq, gw), lambda bi, gi, qi, *_: (bi, qi, gi))
    in_specs = [gspec, gspec, gspec,
                pl.BlockSpec((1, tq, LANE), lambda bi, gi, qi, *_: (bi, qi, 0)),
                kvspec(0), kvspec(2), kvspec(4), kvspec(6), cspec, cspec]
    out_shape = [jax.ShapeDtypeStruct((b, t, NSA_HEADS * LANE), BF16)]
    out_specs = [pl.BlockSpec((1, tq, gw), lambda bi, gi, qi, *_: (bi, qi, gi))]
    assert tc % tq == 0 and npad == LANE
    scratch = ([pltpu.VMEM((rows, LANE), BF16), pltpu.VMEM((rows, 2 * LANE), BF16)]
               + [pltpu.VMEM((rows, wide), F32), pltpu.VMEM((rows, wide), BF16)]
               + [pltpu.VMEM((rows, LANE), F32)] * 5 + [pltpu.VMEM((tq, wide), F32)])
    args = [qc3, qr3, ng3, bg3, kvb, kvb, kvb, kvb, kc, vc]
    n_steps = b * NSA_KV * nq
    pool_pages, page_base, prefetch = 0, 0, []
    if pool is not None:
        cache4, page_flat, a3, page_base = pool
        page, n_cols = cache4.shape[1], 2 * NSA_KV
        if page_flat.shape[0] % n_steps == 0 and page == 2 * CMP_BLOCK and 2 * n_cols == 8:
            pool_pages = page_flat.shape[0] // n_steps
            prefetch = [page_flat]
            in_specs += [pl.BlockSpec((CMP_BLOCK, 2 * n_cols, LANE), lambda bi, gi, qi, *_: (0, 0, 0)),
                         pl.BlockSpec(memory_space=pl.ANY)]
            args += [jnp.concatenate([a3, a3], axis=1), cache4]
            out_shape.append(jax.ShapeDtypeStruct((n_steps, pool_pages, 2 * n_cols, LANE), F32))
            out_specs.append(pl.BlockSpec((1, pool_pages, 2 * n_cols, LANE),
                                          lambda bi, gi, qi, *_: ((bi * NSA_KV + gi) * nq + qi, 0, 0, 0)))
            scratch += [pltpu.VMEM((2, pool_pages, CMP_BLOCK, 2 * n_cols, LANE), F32), pltpu.SemaphoreType.DMA((2,))]
    kern = functools.partial(_nsa_prompt_kernel, t_len=t, tq=tq, tc=tc, tw=tw, n_sel=n_sel, top=top,
                             sub=min(sub, tq), pool_pages=pool_pages, page_base=page_base)
    res = pl.pallas_call(
        kern,
        out_shape=out_shape,
        grid_spec=pltpu.PrefetchScalarGridSpec(
            num_scalar_prefetch=len(prefetch), grid=(b, NSA_KV, nq),
            in_specs=in_specs, out_specs=out_specs, scratch_shapes=scratch),
        compiler_params=_params(("arbitrary", "arbitrary", "arbitrary")),
        name="nsa_prompt",
    )(*prefetch, *args)
    return (res[0], res[1]) if pool_pages else (res[0], None)


def _conv_kernel(h_ref, b_ref, c_ref, g_ref, prev_ref, w_ref, y_ref, st_ref, up_scr, *, t_real):
    u = c_ref[...] * h_ref[...]
    t = u.shape[1]
    up_scr[:, pl.ds(8 - (CONV_W - 1), CONV_W - 1), :] = prev_ref[...]
    up_scr[:, pl.ds(8, t), :] = u
    w = w_ref[...]
    y = w[0:1, :][None] * up_scr[:, pl.ds(6, t), :]
    y = y + w[1:2, :][None] * up_scr[:, pl.ds(7, t), :]
    y = y + w[2:3, :][None] * u
    y = b_ref[...] * y
    y_ref[...] = (_silu(g_ref[...]) * y).astype(y_ref.dtype)
    st_ref[...] = up_scr[:, pl.ds(6 + t_real, CONV_W - 1), :]


def _conv_mixer(z3, prev, w_conv, *, t_real):
    b, t, _ = z3.shape
    nc = CONV_DIM // LANE
    zspec = lambda off: pl.BlockSpec((b, t, LANE), lambda ci, off=off: (0, 0, off // LANE + ci))
    return pl.pallas_call(
        functools.partial(_conv_kernel, t_real=t_real),
        out_shape=[jax.ShapeDtypeStruct((b, t, CONV_DIM), BF16),
                   jax.ShapeDtypeStruct((b, CONV_W - 1, CONV_DIM), F32)],
        grid=(nc,),
        in_specs=[zspec(C_H), zspec(C_B), zspec(C_C), zspec(C_G),
                  pl.BlockSpec((b, CONV_W - 1, LANE), lambda ci: (0, 0, ci)),
                  pl.BlockSpec((CONV_W, LANE), lambda ci: (0, ci))],
        out_specs=[pl.BlockSpec((b, t, LANE), lambda ci: (0, 0, ci)),
                   pl.BlockSpec((b, CONV_W - 1, LANE), lambda ci: (0, 0, ci))],
        scratch_shapes=[pltpu.VMEM((b, t + 8, LANE), F32)],
        compiler_params=_params(("parallel",)),
        name="conv_mixer",
    )(z3, z3, z3, z3, prev, w_conv)


def _mem_attn_kernel(q_ref, mg_ref, kv_ref, o_ref, *, interleaved):
    q = q_ref[0]
    mg = mg_ref[0]
    half = MEM_HEADS * LANE
    for h in range(MEM_HEADS):
        lo, hi = h * LANE, (h + 1) * LANE
        if interleaved:
            nm = kv_ref.shape[1] // (2 * MEM_HEADS)
            k = kv_ref[0, pl.ds(h, nm, stride=2 * MEM_HEADS), :].astype(BF16)
            v = kv_ref[0, pl.ds(MEM_HEADS + h, nm, stride=2 * MEM_HEADS), :].astype(BF16)
        else:
            k = kv_ref[0, :, lo:hi].astype(BF16)
            v = kv_ref[0, :, half + lo:half + hi].astype(BF16)
        s = _nt((q[:, lo:hi] * ATTN_SCALE).astype(BF16), k)
        e = jnp.exp(s - jnp.max(s, axis=-1, keepdims=True))
        o = _nn(e.astype(BF16), v) / jnp.sum(e, axis=-1, keepdims=True)
        o_ref[0, :, lo:hi] = (_silu(mg[:, lo:hi]) * o).astype(o_ref.dtype)


def _mem_attn(zb3, mkv, *, mq_off, mg_off, tq, interleaved, kv_base=0):
    b, t, _ = zb3.shape
    wq = MEM_HEADS * LANE
    return pl.pallas_call(
        functools.partial(_mem_attn_kernel, interleaved=interleaved),
        out_shape=jax.ShapeDtypeStruct((b, t, wq), BF16),
        grid=(b, t // tq),
        in_specs=[pl.BlockSpec((1, tq, wq), lambda bi, ti: (bi, ti, mq_off // wq)),
                  pl.BlockSpec((1, tq, wq), lambda bi, ti: (bi, ti, mg_off // wq)),
                  pl.BlockSpec((1,) + mkv.shape[1:], lambda bi, ti: (bi + kv_base, 0, 0))],
        out_specs=pl.BlockSpec((1, tq, wq), lambda bi, ti: (bi, ti, 0)),
        compiler_params=_params(("parallel", "parallel")),
        name="mem_attn",
    )(zb3, zb3, mkv)


def _out_proj_kernel(x_ref, ya_ref, yb_ref, ym_ref, w_ref, fg_ref, o_ref, *, final):
    a, bw = CONV_DIM, CONV_DIM + NSA_HEADS * LANE
    acc = _nn(ya_ref[...], w_ref[0:a, :])
    acc = acc + _nn(yb_ref[...], w_ref[a:bw, :])
    acc = acc + _nn(ym_ref[...], w_ref[bw:, :])
    r = x_ref[...] + acc
    if final:
        r = r * lax.rsqrt(jnp.mean(r * r, axis=-1, keepdims=True) + NORM_EPS) * fg_ref[...]
    o_ref[...] = r


def _out_proj(x, ya, yb, ym, w, fg, *, tm, final):
    m, d = x.shape
    row = lambda width: pl.BlockSpec((tm, width), lambda i: (i, 0))
    return pl.pallas_call(
        functools.partial(_out_proj_kernel, final=final),
        out_shape=jax.ShapeDtypeStruct((m, d), F32),
        grid=(m // tm,),
        in_specs=[row(d), row(ya.shape[1]), row(yb.shape[1]), row(ym.shape[1]),
                  pl.BlockSpec(w.shape, lambda i: (0, 0), pipeline_mode=pl.Buffered(1)),
                  pl.BlockSpec((1, d), lambda i: (0, 0), pipeline_mode=pl.Buffered(1))],
        out_specs=row(d),
        compiler_params=_params(("parallel",)),
        name="out_proj",
    )(x, ya, yb, ym, w, fg.reshape(1, d))


def _sample_pre_kernel(q_ref, ng_ref, bg_ref, kv0_ref, kv1_ref, kv2_ref, c_ref, s1_ref, s2_ref, kc_ref, vc_ref,
                       kvn_ref, wn_ref, qr_ref, oc_ref, ngo_ref, gate_ref, val_ref, *, past):
    tp = SAMPLE_T_PAD
    hq = NSA_HPG
    c, s1, s2 = c_ref[...], s1_ref[...], s2_ref[...]
    kv0, kv1, kv2 = kv0_ref[0], kv1_ref[0], kv2_ref[0]
    kvn_ref[0, :, 0:512] = kv0
    kvn_ref[0, :, 768:1024] = kv1[:, 256:512]
    wn_ref[0, :, 256:512] = kv2[:, 256:512]
    for g in range(NSA_KV):
        lo, hi = g * LANE, (g + 1) * LANE
        kvn_ref[0, :, 512 + lo:512 + hi] = _rope(kv1[:, lo:hi], c, s1, s2)
        wn_ref[0, :, lo:hi] = _rope(kv2[:, lo:hi], c, s1, s2)

    q = q_ref[0]
    ng = ng_ref[0]
    gates = jax.nn.sigmoid(bg_ref[0])
    for g in range(NSA_KV):
        qc_l, qr_l = [], []
        for h in range(hq):
            lo = (g * hq + h) * LANE
            qh = q[:, lo:lo + LANE]
            qc_l.append(qh * ATTN_SCALE)
            qr_l.append(_rope(qh, c, s1, s2) * ATTN_SCALE)
            ngo_ref[0, g, h * tp:(h + 1) * tp, :] = ng[:, lo:lo + LANE]
            for br in range(N_BRANCH):
                col = (g * hq + h) * N_BRANCH + br
                gate_ref[0, g, br, h * tp:(h + 1) * tp, :] = jnp.broadcast_to(gates[:, col:col + 1], (tp, LANE))
        qc = jnp.concatenate(qc_l, axis=0)
        qr_ref[0, g] = jnp.concatenate(qr_l, axis=0)

        kc = kc_ref[0, g]
        npad = kc.shape[0]
        s = _nt(qc.astype(BF16), kc)
        trow = lax.broadcasted_iota(jnp.int32, (hq * tp, npad), 0) % tp
        ncol = lax.broadcasted_iota(jnp.int32, (hq * tp, npad), 1)
        cmask = (ncol + 1) * CMP_BLOCK <= past + trow + 1
        s = jnp.where(cmask, s, MASK_NEG)
        e = jnp.where(cmask, jnp.exp(s - jnp.max(s, axis=-1, keepdims=True)), 0.0)
        p = e / jnp.maximum(jnp.sum(e, axis=-1, keepdims=True), 1e-30)
        oc_ref[0, g] = _nn(p.astype(BF16), vc_ref[0, g])
        imp = jnp.sum(p.reshape(hq, tp, npad), axis=0)

        blk = lax.broadcasted_iota(jnp.int32, (tp, npad), 1)
        cur = (past + lax.broadcasted_iota(jnp.int32, (tp, npad), 0)) // SEL_BLOCK
        val = jnp.where((blk == 0) | (blk == cur) | (blk == cur - 1), FORCE, imp)
        val = jnp.where(blk > cur, -1.0, val)
        val = jnp.where(blk >= past // SEL_BLOCK, -2.0, val)
        val_ref[0, g] = val


def _topk_kernel(val_ref, idx_ref, *, n_top):
    val = val_ref[...]
    rows, n = val.shape
    blk = lax.broadcasted_iota(jnp.int32, (rows, n), 1)
    lane = lax.broadcasted_iota(jnp.int32, (rows, LANE), 1)
    idx = jnp.zeros((rows, LANE), jnp.int32)
    for r in range(n_top):
        best = jnp.max(val, axis=-1, keepdims=True)
        j = jnp.min(jnp.where(val == best, blk, n), axis=-1, keepdims=True)
        idx = jnp.where(lane == r, j, idx)
        val = jnp.where(blk == j, -3e38, val)
    idx_ref[...] = idx


def _topk(val2, *, n_top):
    rows, n = val2.shape
    return pl.pallas_call(
        functools.partial(_topk_kernel, n_top=n_top),
        out_shape=jax.ShapeDtypeStruct((rows, LANE), jnp.int32),
        grid=(1,),
        in_specs=[pl.BlockSpec((rows, n), lambda i: (0, 0))],
        out_specs=pl.BlockSpec((rows, LANE), lambda i: (0, 0)),
        compiler_params=_params(("arbitrary",)),
        name="sample_topk",
    )(val2)


def _sample_pre(za3, zb3, bg3, tabs, kc, vc, *, past):
    bs, tp, _ = za3.shape
    npad = kc.shape[2]
    qw = NSA_HEADS * LANE
    kvblk = KV_OFF // 512
    zspec = lambda k: pl.BlockSpec((1, tp, 512), lambda bi, k=k: (bi, 0, kvblk + k))
    tspec = pl.BlockSpec((tp, LANE), lambda bi: (0, 0))
    cspec = pl.BlockSpec((1, NSA_KV, npad, LANE), lambda bi: (bi, 0, 0, 0))
    rows = NSA_HPG * tp
    gspec = pl.BlockSpec((1, NSA_KV, rows, LANE), lambda bi: (bi, 0, 0, 0))
    gshape = jax.ShapeDtypeStruct((bs, NSA_KV, rows, LANE), F32)
    return pl.pallas_call(
        functools.partial(_sample_pre_kernel, past=past),
        out_shape=[jax.ShapeDtypeStruct((bs, tp, 1024), F32),
                   jax.ShapeDtypeStruct((bs, tp, 512), F32),
                   gshape, gshape, gshape,
                   jax.ShapeDtypeStruct((bs, NSA_KV, N_BRANCH, rows, LANE), F32),
                   jax.ShapeDtypeStruct((bs, NSA_KV, tp, npad), F32)],
        grid=(bs,),
        in_specs=[pl.BlockSpec((1, tp, qw), lambda bi: (bi, 0, Q_OFF // qw)),
                  pl.BlockSpec((1, tp, qw), lambda bi: (bi, 0, NG_OFF // qw)),
                  pl.BlockSpec((1, tp, LANE), lambda bi: (bi, 0, 0)),
                  zspec(0), zspec(1), zspec(2), tspec, tspec, tspec, cspec, cspec],
        out_specs=[pl.BlockSpec((1, tp, 1024), lambda bi: (bi, 0, 0)),
                   pl.BlockSpec((1, tp, 512), lambda bi: (bi, 0, 0)),
                   gspec, gspec, gspec,
                   pl.BlockSpec((1, NSA_KV, N_BRANCH, rows, LANE), lambda bi: (bi, 0, 0, 0, 0)),
                   pl.BlockSpec((1, NSA_KV, tp, npad), lambda bi: (bi, 0, 0, 0))],
        compiler_params=_params(("parallel",)),
        name="sample_pre",
    )(za3, za3, bg3, zb3, zb3, zb3, *tabs, kc, vc)


def _sample_attn_kernel(idx_sm, pt_sm, qr_ref, oc_ref, ng_ref, gate_ref, ksn_ref, vsn_ref,
                        wc_ref, kwn_ref, vwn_ref, cache_ref, o_ref,
                        kbuf, vbuf, kw_scr, vw_scr, sem, *, ts, n_top, n_pages, page_base, per_page, wb):
    tp = SAMPLE_T_PAD
    hq = NSA_HPG
    b = pl.program_id(0)
    g = pl.program_id(1)
    n_gath = n_top * SEL_BLOCK
    ks_rows = kbuf.shape[2]
    step = b * NSA_KV + g
    n_steps = pl.num_programs(0) * NSA_KV
    slot = step % 2

    def gather_copies(step_idx, t, r, to_slot):
        bb, gg = step_idx // NSA_KV, step_idx % NSA_KV
        blk = idx_sm[(step_idx * ts + t) * n_top + r]
        page = pt_sm[bb * n_pages + blk // per_page] + page_base
        row0 = (blk % per_page) * SEL_BLOCK
        src_k = cache_ref.at[page, pl.ds(row0, SEL_BLOCK), 2 * NSA_KV + gg]
        src_v = cache_ref.at[page, pl.ds(row0, SEL_BLOCK), 3 * NSA_KV + gg]
        dst = pl.ds(r * SEL_BLOCK, SEL_BLOCK)
        return (pltpu.make_async_copy(src_k, kbuf.at[to_slot, t, dst], sem.at[to_slot, 0]),
                pltpu.make_async_copy(src_v, vbuf.at[to_slot, t, dst], sem.at[to_slot, 1]))

    def start_gathers(step_idx, to_slot):
        for t in range(ts):
            for r in range(n_top):
                ck, cv = gather_copies(step_idx, t, r, to_slot)
                ck.start()
                cv.start()

    @pl.when(step == 0)
    def _():
        start_gathers(0, 0)

    @pl.when(step + 1 < n_steps)
    def _():
        start_gathers(step + 1, 1 - slot)

    qr = qr_ref[0, 0].astype(BF16)
    trow = lax.broadcasted_iota(jnp.int32, (hq * tp, 1), 0) % tp

    ww = kw_scr.shape[0]
    kw_scr[pl.ds(0, wb), :] = wc_ref[0, pl.ds(g, wb, stride=2 * NSA_KV), :]
    vw_scr[pl.ds(0, wb), :] = wc_ref[0, pl.ds(NSA_KV + g, wb, stride=2 * NSA_KV), :]
    kw_scr[pl.ds(wb, tp), :] = kwn_ref[0]
    vw_scr[pl.ds(wb, tp), :] = vwn_ref[0]
    kw_scr[pl.ds(wb + tp, ww - wb - tp), :] = jnp.zeros((ww - wb - tp, LANE), F32)
    vw_scr[pl.ds(wb + tp, ww - wb - tp), :] = jnp.zeros((ww - wb - tp, LANE), F32)
    sw = _nt(qr, kw_scr[...].astype(BF16))
    jw = lax.broadcasted_iota(jnp.int32, (hq * tp, ww), 1)
    rel = jw - wb
    okw = (rel <= trow) & (rel > trow - WINDOW) & (jw < wb + ts)
    sw = jnp.where(okw, sw, MASK_NEG)
    ew = jnp.where(okw, jnp.exp(sw - jnp.max(sw, axis=-1, keepdims=True)), 0.0)
    o_w = _nn(ew.astype(BF16), vw_scr[...].astype(BF16)) / jnp.sum(ew, axis=-1, keepdims=True)

    for t in range(ts):
        for r in range(n_top):
            ck, cv = gather_copies(step, t, r, slot)
            ck.wait()
            cv.wait()

    js = lax.broadcasted_iota(jnp.int32, (hq * tp, ks_rows), 1)
    o_s = jnp.zeros((hq * tp, LANE), F32)
    for t in range(ts):
        kbuf[slot, t, pl.ds(n_gath, tp), :] = ksn_ref[0]
        vbuf[slot, t, pl.ds(n_gath, tp), :] = vsn_ref[0]
        kbuf[slot, t, pl.ds(n_gath + tp, ks_rows - n_gath - tp), :] = jnp.zeros((ks_rows - n_gath - tp, LANE), F32)
        vbuf[slot, t, pl.ds(n_gath + tp, ks_rows - n_gath - tp), :] = jnp.zeros((ks_rows - n_gath - tp, LANE), F32)
        ss = _nt(qr, kbuf[slot, t].astype(BF16))
        oks = (js < n_gath) | ((js - n_gath <= t) & (js < n_gath + ts))
        ss = jnp.where(oks, ss, MASK_NEG)
        es = jnp.where(oks, jnp.exp(ss - jnp.max(ss, axis=-1, keepdims=True)), 0.0)
        ot = _nn(es.astype(BF16), vbuf[slot, t].astype(BF16)) / jnp.sum(es, axis=-1, keepdims=True)
        o_s = jnp.where(trow == t, ot, o_s)

    o = gate_ref[0, 0, 0] * oc_ref[0, 0] + gate_ref[0, 0, 1] * o_s + gate_ref[0, 0, 2] * o_w
    y = _silu(ng_ref[0, 0]) * o
    for h in range(hq):
        o_ref[0, :, h * LANE:(h + 1) * LANE] = y[h * tp:(h + 1) * tp].astype(o_ref.dtype)


def _sample_attn(idx_flat, page_flat, qr, oc, ngo, gates, kvn, cache_win_rows, wn, cache4, *,
                 ts, n_top, n_pages, page_base, win_base):
    bs = qr.shape[0]
    tp = SAMPLE_T_PAD
    rows = NSA_HPG * tp
    wb = cache_win_rows.shape[1] // (2 * NSA_KV)
    per_page = cache4.shape[1] // SEL_BLOCK
    ks_rows = -(-(n_top * SEL_BLOCK + tp) // LANE) * LANE
    ww = -(-(wb + tp) // LANE) * LANE
    gspec = pl.BlockSpec((1, 1, rows, LANE), lambda bi, gi, *_: (bi, gi, 0, 0))
    newspec = lambda k: pl.BlockSpec((1, tp, LANE), lambda bi, gi, *_, k=k: (bi, 0, k + gi))
    grid_spec = pltpu.PrefetchScalarGridSpec(
        num_scalar_prefetch=2, grid=(bs, NSA_KV),
        in_specs=[gspec, gspec, gspec,
                  pl.BlockSpec((1, 1, N_BRANCH, rows, LANE), lambda bi, gi, *_: (bi, gi, 0, 0, 0)),
                  newspec(2 * NSA_KV), newspec(3 * NSA_KV),
                  pl.BlockSpec((1,) + cache_win_rows.shape[1:], lambda bi, gi, *_: (bi + win_base, 0, 0)),
                  newspec(0), newspec(NSA_KV),
                  pl.BlockSpec(memory_space=pl.ANY)],
        out_specs=pl.BlockSpec((1, tp, NSA_HPG * LANE), lambda bi, gi, *_: (bi, 0, gi)),
        scratch_shapes=[pltpu.VMEM((2, ts, ks_rows, LANE), F32), pltpu.VMEM((2, ts, ks_rows, LANE), F32),
                        pltpu.VMEM((ww, LANE), F32), pltpu.VMEM((ww, LANE), F32),
                        pltpu.SemaphoreType.DMA((2, 2))])
    kern = functools.partial(_sample_attn_kernel, ts=ts, n_top=n_top, n_pages=n_pages,
                             page_base=page_base, per_page=per_page, wb=wb)
    return pl.pallas_call(
        kern,
        out_shape=jax.ShapeDtypeStruct((bs, tp, NSA_HEADS * LANE), BF16),
        grid_spec=grid_spec,
        compiler_params=_params(("arbitrary", "arbitrary")),
        name="sample_attn",
    )(idx_flat, page_flat, qr, oc, ngo, gates, kvn, kvn, cache_win_rows, wn, wn, cache4)


def _rope_tables(pos, rows):
    freqs = jnp.power(ROPE_THETA, -jnp.arange(ROPE_HALF, dtype=F32) * (2.0 / ROPE_DIM))
    ang = pos.astype(F32)[:, None] * freqs[None, :]
    cos, sin = jnp.cos(ang), jnp.sin(ang)
    n = pos.shape[0]
    z16 = jnp.zeros((n, ROPE_HALF), F32)
    rest = LANE - ROPE_DIM
    c = jnp.concatenate([cos, cos, jnp.ones((n, rest), F32)], axis=1)
    s1 = jnp.concatenate([z16, sin, jnp.zeros((n, rest), F32)], axis=1)
    s2 = jnp.concatenate([-sin, z16, jnp.zeros((n, rest), F32)], axis=1)
    pad = lambda a: jnp.pad(a, ((0, rows - n), (0, 0)))
    return pad(c), pad(s1), pad(s2)


def _layer_weights(norm_g, w_in3, layer, w_conv, a_cmp, w_cmp, w_out):
    w_a, w_b, wbg = _pack_in_proj_weights(jnp.swapaxes(w_in3, 1, 2), layer)
    a4 = jnp.concatenate([a_cmp[0], a_cmp[0], a_cmp[1], a_cmp[1]], axis=1)
    a3 = jnp.stack([a_cmp[0], a_cmp[0], a_cmp[1], a_cmp[1]], axis=1)
    w4 = jnp.stack([w_cmp[0], w_cmp[0], w_cmp[1], w_cmp[1]]).astype(BF16)
    return dict(norm_g=norm_g, w_a=w_a, w_b=w_b, wbg=wbg, w_conv=w_conv, a4=a4, a3=a3, w4=w4,
                w_out=w_out.astype(BF16))


def _prompt_layer(xp, xs2, mem_prompt, mem_norm_g, w_mem, lw, final_g, final, pool):
    b, t, d = xp.shape
    m = b * t
    x2 = xp.reshape(m, d)
    tabs = _rope_tables(jnp.arange(t, dtype=jnp.int32), t)
    wb = min(WINDOW, t)
    qc, qr, ng, bg, ya, conv_new, za_s, bg_s = _proj_conv(x2, lw["norm_g"], lw["w_a"], lw["wbg"], lw["w_conv"], tabs,
                                                          xs2, tm=min(512, t), seq_len=t)
    kvn, kvb, win_rows, pooled_prompt, zm, zb_s = _proj_kv(x2, lw["norm_g"], lw["w_b"], tabs, lw["a4"], xs2,
                                                           tm=wb, seq_len=t)
    nm = mem_prompt.shape[1]
    mkv = _norm_matmul(mem_prompt.reshape(b * nm, d), mem_norm_g, w_mem, tm=min(512, b * nm), tn=512)
    mkv3 = mkv.reshape(b, nm, 2 * MEM_HEADS * LANE)
    kc, vc = _cmp_proj(pooled_prompt.reshape(b, t // CMP_BLOCK, -1), lw["w4"])
    r3 = lambda a: a.reshape(b, t, -1)
    yb, pooled = _nsa_prompt(r3(qc), r3(qr), r3(ng), r3(bg), r3(kvb), kc, vc, tq=min(256, t), pool=pool)
    ym = _mem_attn(zm.reshape(b, t, -1), mkv3, mq_off=0, mg_off=MG_OFF - MQ_OFF, tq=min(512, t), interleaved=False)
    out = _out_proj(x2, ya, yb.reshape(m, -1), ym.reshape(m, -1),
                    lw["w_out"], final_g, tm=min(512, m), final=final)
    kv_new = kvn.reshape(b, t, 4, NSA_KV, HEAD_DIM)
    win_new = win_rows.reshape(b, wb, 2, NSA_KV, HEAD_DIM)
    mem_kv = mkv.reshape(b, nm, 2, MEM_HEADS, HEAD_DIM)
    return out.reshape(b, t, d), kv_new, win_new, conv_new, mem_kv, pooled, (za_s, zb_s, bg_s)


def _sample_layer(xs_p, proj, ts, layer, cache4, page_flat, pooled, cache_win, state_conv, cache_mem, lw, final_g,
                  final):
    bs, tp, d = xs_p.shape
    depth = cache_win.shape[0]
    pool, page = cache4.shape[0] // depth, cache4.shape[1]
    n_pages = page_flat.shape[0] // bs
    past = n_pages * page
    assert past % SEL_BLOCK == 0 and ts <= SEL_BLOCK and ts <= tp
    n_past = past // SEL_BLOCK
    n_top = min(TOP_N, n_past + 1) - 1
    m = bs * tp
    za, zb, bg = proj
    za3 = za.reshape(bs, tp, ZA_WIDTH)
    zb3 = zb.reshape(bs, tp, ZB_WIDTH)
    bg3 = bg.reshape(bs, tp, LANE)
    tabs = _rope_tables(past + jnp.arange(tp, dtype=jnp.int32), tp)
    if pooled is None:
        pooled = _pool_pages(cache4, page_flat, lw["a3"], bs=bs, n_pages=n_pages, page_base=layer * pool)
    else:
        pooled = pooled.reshape(bs, n_pages * (page // CMP_BLOCK), 2 * NSA_KV * LANE)
    kc, vc = _cmp_proj(pooled, lw["w4"])
    kvn, wn, qr, oc, ngo, gates, val = _sample_pre(za3, zb3, bg3, tabs, kc, vc, past=past)
    idx = _topk(val.reshape(bs * NSA_KV * tp, val.shape[-1]), n_top=n_top)
    idx_flat = idx.reshape(bs, NSA_KV, tp, LANE)[:, :, :ts, :n_top].reshape(-1)
    wbuf = cache_win.shape[2]
    cache_win_rows = cache_win.reshape(depth * bs, wbuf * 2 * NSA_KV, HEAD_DIM)
    yb = _sample_attn(idx_flat, page_flat, qr, oc, ngo, gates, kvn, cache_win_rows, wn, cache4,
                      ts=ts, n_top=n_top, n_pages=n_pages, page_base=layer * pool, win_base=layer * bs)
    ya, conv_new = _conv_mixer(za3, state_conv[layer], lw["w_conv"], t_real=ts)
    nm = cache_mem.shape[2]
    mem_rows = cache_mem.reshape(depth * bs, nm * 2 * MEM_HEADS, HEAD_DIM)
    ym = _mem_attn(zb3, mem_rows, mq_off=MQ_OFF, mg_off=MG_OFF, tq=tp, interleaved=True, kv_base=layer * bs)
    out = _out_proj(xs_p.reshape(m, d), ya.reshape(m, -1), yb.reshape(m, -1), ym.reshape(m, -1),
                    lw["w_out"], final_g, tm=m, final=final)
    kv_new = kvn[:, :ts].reshape(bs, ts, 4, NSA_KV, HEAD_DIM)
    win_rows = wn[:, :ts].reshape(bs, ts, 2, NSA_KV, HEAD_DIM)
    win_state = jnp.concatenate([cache_win[layer], win_rows], axis=1)[:, ts:]
    return out.reshape(bs, tp, d), kv_new, win_state, conv_new


def kernel(x_prompt, x_sample, cache_kv, cache_win, state_conv, cache_mem, page_table, mem_prompt,
           norm_g, w_in, w_conv, a_cmp, w_cmp, mem_norm_g, w_mem_kv, w_out, final_g):
    depth = w_in.shape[0]
    ts = x_sample.shape[1]
    xp = x_prompt
    xs = jnp.pad(x_sample, ((0, 0), (0, SAMPLE_T_PAD - ts), (0, 0)))
    pool_size, page = cache_kv.shape[1], cache_kv.shape[2]
    cache4 = cache_kv.reshape(depth * pool_size, page, 4 * NSA_KV, HEAD_DIM)
    page_flat = page_table.reshape(-1).astype(jnp.int32)
    kv_p, win_p, conv_p, mem_p, kv_s, win_s, conv_s = [], [], [], [], [], [], []
    for l in range(depth):
        lw = _layer_weights(norm_g[l], w_in, l, w_conv[l], a_cmp[l], w_cmp[l], w_out[l])
        final = l == depth - 1
        xp, kvn, winn, convn, mkv, pooled, proj_s = _prompt_layer(
            xp, xs.reshape(-1, xs.shape[-1]), mem_prompt, mem_norm_g[l], w_mem_kv[l].T.astype(BF16), lw, final_g, final,
            pool=(cache4, page_flat, lw["a3"], l * pool_size))
        kv_p.append(kvn)
        win_p.append(winn)
        conv_p.append(convn)
        mem_p.append(mkv)
        xs, kvn, winn, convn = _sample_layer(xs, proj_s, ts, l, cache4, page_flat, pooled, cache_win, state_conv,
                                             cache_mem, lw, final_g, final)
        kv_s.append(kvn)
        win_s.append(winn)
        conv_s.append(convn)
    return (xp, xs[:, :ts], jnp.stack(kv_p), jnp.stack(win_p), jnp.stack(conv_p), jnp.stack(mem_p),
            jnp.stack(kv_s), jnp.stack(win_s), jnp.stack(conv_s))
```

```python
import functools

import jax
import jax.numpy as jnp
from jax import lax
from jax.experimental import pallas as pl
from jax.experimental.pallas import tpu as pltpu

F32 = jnp.float32
BF16 = jnp.bfloat16

HEAD_DIM = 128
CONV_DIM = 512
CONV_W = 3
NSA_HEADS = 8
NSA_KV = 2
NSA_HPG = NSA_HEADS // NSA_KV
MEM_HEADS = 4
N_BRANCH = 3
ROPE_DIM = HEAD_DIM // 4
ROPE_HALF = ROPE_DIM // 2
ROPE_THETA = 500000.0
CMP_BLOCK = 64
SEL_BLOCK = 64
SEL_SHIFT = 6
TOP_N = 16
WINDOW = 512
NORM_EPS = 1e-6
MASK_NEG = -1e30
FORCE = 1e9
ATTN_SCALE = HEAD_DIM ** -0.5
SCALE_LOG2 = ATTN_SCALE * 1.4426950408889634

C_H, C_B, C_C, C_G, Q_OFF, NG_OFF = 0, 512, 1024, 1536, 2048, 3072
ZA_WIDTH = 4096
KV_OFF, MQ_OFF, MG_OFF = 0, 1536, 2048
ZB_WIDTH = 2560
BG_SRC = 4096
BG_N = NSA_HEADS * N_BRANCH
LANE = 128
SAMPLE_T_PAD = 8
VMEM_LIMIT = 56 * 1024 * 1024


def _nt(a, b):
    return lax.dot_general(a, b, (((1,), (1,)), ((), ())), preferred_element_type=F32)


def _nn(a, b):
    return jnp.dot(a, b, preferred_element_type=F32)


def _params(sem, vmem=VMEM_LIMIT):
    return pltpu.CompilerParams(dimension_semantics=sem, vmem_limit_bytes=vmem)


def _rope(x, c, s1, s2):
    return x * c + pltpu.roll(x, ROPE_HALF, 1) * s1 + pltpu.roll(x, LANE - ROPE_HALF, 1) * s2


def _silu(x):
    return x * jax.nn.sigmoid(x)


def _norm_matmul_kernel(x_ref, g_ref, w_ref, *rest, with_gate, tn):
    if with_gate:
        wbg_ref, z_ref, bg_ref, h_scr = rest
    else:
        z_ref, h_scr = rest
    x = x_ref[...]
    y = x * lax.rsqrt(jnp.mean(x * x, axis=-1, keepdims=True) + NORM_EPS) * g_ref[...]
    h_scr[...] = y.astype(BF16)
    if with_gate:
        bg_ref[...] = _nt(h_scr[...], wbg_ref[...])
    for j in range(w_ref.shape[0] // tn):
        z_ref[:, j * tn:(j + 1) * tn] = _nt(h_scr[...], w_ref[j * tn:(j + 1) * tn, :])


def _norm_matmul(x, g, w, wbg=None, *, tm, tn):
    m, d = x.shape
    n = w.shape[0]
    assert n % tn == 0 and m % tm == 0 and w.shape[1] == d
    with_gate = wbg is not None
    resident = lambda shape: pl.BlockSpec(shape, lambda i: (0, 0), pipeline_mode=pl.Buffered(1))
    in_specs = [pl.BlockSpec((tm, d), lambda i: (i, 0)), resident((1, d)), resident((n, d))]
    out_shape = [jax.ShapeDtypeStruct((m, n), F32)]
    out_specs = [pl.BlockSpec((tm, n), lambda i: (i, 0))]
    args = [x, g.reshape(1, d), w]
    if with_gate:
        nb = wbg.shape[0]
        in_specs.append(resident((nb, d)))
        out_shape.append(jax.ShapeDtypeStruct((m, nb), F32))
        out_specs.append(pl.BlockSpec((tm, nb), lambda i: (i, 0)))
        args.append(wbg)
    res = pl.pallas_call(
        functools.partial(_norm_matmul_kernel, with_gate=with_gate, tn=tn),
        out_shape=out_shape, grid=(m // tm,), in_specs=in_specs, out_specs=out_specs,
        scratch_shapes=[pltpu.VMEM((tm, d), BF16)],
        compiler_params=_params(("parallel",)),
        name="norm_matmul_gate" if with_gate else "norm_matmul",
    )(*args)
    return res if with_gate else res[0]


def _rms_rows(x_ref, g_ref):
    x = x_ref[...]
    return (x * lax.rsqrt(jnp.mean(x * x, axis=-1, keepdims=True) + NORM_EPS) * g_ref[...]).astype(BF16)


def _proj_conv_kernel(x_ref, g_ref, w_ref, wbg_ref, wc_ref, c_ref, s1_ref, s2_ref, xs_ref,
                      qc_ref, qr_ref, ng_ref, bg_ref, ya_ref, st_ref, zs_ref, bgs_ref, h_scr, up_scr,
                      *, blocks_per_seq):
    tm = x_ref.shape[0]
    cw = CONV_DIM
    i = pl.program_id(0)
    first = i % blocks_per_seq == 0
    h_scr[...] = _rms_rows(x_ref, g_ref)
    bg_ref[...] = _nt(h_scr[...], wbg_ref[...])
    chunk = lambda off: _nt(h_scr[...], w_ref[off:off + cw, :])

    @pl.when(i == 0)
    def _():
        up_scr[...] = jnp.zeros(up_scr.shape, F32)

    carry = up_scr[pl.ds(8 + tm - (CONV_W - 1), CONV_W - 1), :]
    up_scr[pl.ds(8 - (CONV_W - 1), CONV_W - 1), :] = jnp.where(first, 0.0, carry)
    u = chunk(C_C) * chunk(C_H)
    up_scr[pl.ds(8, tm), :] = u
    wc = wc_ref[...]
    y = wc[0:1, :] * up_scr[pl.ds(6, tm), :]
    y = y + wc[1:2, :] * up_scr[pl.ds(7, tm), :]
    y = y + wc[2:3, :] * u
    y = chunk(C_B) * y
    ya_ref[...] = (_silu(chunk(C_G)) * y).astype(ya_ref.dtype)

    st_ref[0] = up_scr[pl.ds(8 + tm - (CONV_W - 1), CONV_W - 1), :]

    c, s1, s2 = c_ref[...], s1_ref[...], s2_ref[...]
    for j in range((NG_OFF - Q_OFF) // cw):
        qv = chunk(Q_OFF + j * cw)
        for k in range(cw // LANE):
            lo = j * cw + k * LANE
            qh = qv[:, k * LANE:(k + 1) * LANE]
            qc_ref[:, lo:lo + LANE] = (qh * SCALE_LOG2).astype(BF16)
            qr_ref[:, lo:lo + LANE] = (_rope(qh, c, s1, s2) * SCALE_LOG2).astype(BF16)
    for j in range((ZA_WIDTH - NG_OFF) // cw):
        ng_ref[:, j * cw:(j + 1) * cw] = _silu(chunk(NG_OFF + j * cw))

    @pl.when(i == pl.num_programs(0) - 1)
    def _():
        hs = _rms_rows(xs_ref, g_ref)
        bgs_ref[...] = _nt(hs, wbg_ref[...])
        for j in range(ZA_WIDTH // cw):
            zs_ref[:, j * cw:(j + 1) * cw] = _nt(hs, w_ref[j * cw:(j + 1) * cw, :])


def _proj_conv(x2, g, w_a, wbg, w_conv, tabs, xs2, *, tm, seq_len):
    m, d = x2.shape
    ms = xs2.shape[0]
    assert seq_len % tm == 0 and CONV_W == 3
    blocks_per_seq = seq_len // tm
    nq, nng = NG_OFF - Q_OFF, ZA_WIDTH - NG_OFF
    resident = lambda shape: pl.BlockSpec(shape, lambda i: (0,) * len(shape), pipeline_mode=pl.Buffered(1))
    row = lambda width: pl.BlockSpec((tm, width), lambda i: (i, 0))
    tspec = pl.BlockSpec((tm, LANE), lambda i: (i % blocks_per_seq, 0))
    return pl.pallas_call(
        functools.partial(_proj_conv_kernel, blocks_per_seq=blocks_per_seq),
        out_shape=[jax.ShapeDtypeStruct((m, nq), BF16), jax.ShapeDtypeStruct((m, nq), BF16),
                   jax.ShapeDtypeStruct((m, nng), F32), jax.ShapeDtypeStruct((m, LANE), F32),
                   jax.ShapeDtypeStruct((m, CONV_DIM), BF16),
                   jax.ShapeDtypeStruct((m // seq_len, CONV_W - 1, CONV_DIM), F32),
                   jax.ShapeDtypeStruct((ms, ZA_WIDTH), F32), jax.ShapeDtypeStruct((ms, LANE), F32)],
        grid=(m // tm,),
        in_specs=[row(d), resident((1, d)), resident(w_a.shape), resident(wbg.shape), resident(w_conv.shape),
                  tspec, tspec, tspec, resident((ms, d))],
        out_specs=[row(nq), row(nq), row(nng), row(LANE), row(CONV_DIM),
                   pl.BlockSpec((1, CONV_W - 1, CONV_DIM), lambda i: (i // blocks_per_seq, 0, 0)),
                   pl.BlockSpec((ms, ZA_WIDTH), lambda i: (0, 0)), pl.BlockSpec((ms, LANE), lambda i: (0, 0))],
        scratch_shapes=[pltpu.VMEM((tm, d), BF16), pltpu.VMEM((tm + 8, CONV_DIM), F32)],
        compiler_params=_params(("arbitrary",)),
        name="proj_conv",
    )(x2, g.reshape(1, d), w_a, wbg, w_conv, *tabs, xs2)


def _proj_kv_kernel(x_ref, g_ref, w_ref, c_ref, s1_ref, s2_ref, a_ref, xs_ref,
                    kvn_ref, kvb_ref, win_ref, pool_ref, zm_ref, zs_ref, h_scr, *, blocks_per_seq):
    tm = x_ref.shape[0]
    cw = 2 * NSA_KV * LANE
    h_scr[...] = _rms_rows(x_ref, g_ref)
    chunk = lambda off: _nt(h_scr[...], w_ref[off:off + cw, :])
    c, s1, s2 = c_ref[...], s1_ref[...], s2_ref[...]
    kv0 = chunk(KV_OFF)
    kv1 = chunk(KV_OFF + cw)
    kv2 = chunk(KV_OFF + 2 * cw)
    n_kv, n_w, half = 4 * NSA_KV, 2 * NSA_KV, NSA_KV * LANE
    for g in range(NSA_KV):
        lo, hi = g * LANE, (g + 1) * LANE
        ks = _rope(kv1[:, lo:hi], c, s1, s2)
        kw = _rope(kv2[:, lo:hi], c, s1, s2)
        vs = kv1[:, half + lo:half + hi]
        vw = kv2[:, half + lo:half + hi]
        kvn_ref[pl.ds(g, tm, stride=n_kv), :] = kv0[:, lo:hi]
        kvn_ref[pl.ds(NSA_KV + g, tm, stride=n_kv), :] = kv0[:, half + lo:half + hi]
        kvn_ref[pl.ds(2 * NSA_KV + g, tm, stride=n_kv), :] = ks
        kvn_ref[pl.ds(3 * NSA_KV + g, tm, stride=n_kv), :] = vs
        kvb_ref[:, lo:hi] = ks.astype(BF16)
        kvb_ref[:, half + lo:half + hi] = vs.astype(BF16)
        kvb_ref[:, 2 * half + lo:2 * half + hi] = kw.astype(BF16)
        kvb_ref[:, 3 * half + lo:3 * half + hi] = vw.astype(BF16)
        win_ref[pl.ds(g, tm, stride=n_w), :] = kw
        win_ref[pl.ds(NSA_KV + g, tm, stride=n_w), :] = vw

    pool_ref[...] = jnp.sum(kv0.reshape(tm // CMP_BLOCK, CMP_BLOCK, cw) * a_ref[...][None], axis=1)
    for j in range(zm_ref.shape[1] // cw):
        zm_ref[:, j * cw:(j + 1) * cw] = chunk(MQ_OFF + j * cw)

    @pl.when(pl.program_id(0) == pl.num_programs(0) - 1)
    def _():
        hs = _rms_rows(xs_ref, g_ref)
        for j in range(ZB_WIDTH // cw):
            zs_ref[:, j * cw:(j + 1) * cw] = _nt(hs, w_ref[j * cw:(j + 1) * cw, :])


def _proj_kv(x2, g, w_b, tabs, a4, xs2, *, tm, seq_len):
    m, d = x2.shape
    ms = xs2.shape[0]
    assert seq_len % tm == 0
    blocks_per_seq = seq_len // tm
    n_kv, n_w = 4 * NSA_KV, 2 * NSA_KV
    nm = ZB_WIDTH - MQ_OFF
    resident = lambda shape: pl.BlockSpec(shape, lambda i: (0,) * len(shape), pipeline_mode=pl.Buffered(1))
    row = lambda width: pl.BlockSpec((tm, width), lambda i: (i, 0))
    tspec = pl.BlockSpec((tm, LANE), lambda i: (i % blocks_per_seq, 0))
    return pl.pallas_call(
        functools.partial(_proj_kv_kernel, blocks_per_seq=blocks_per_seq),
        out_shape=[jax.ShapeDtypeStruct((m * n_kv, LANE), F32),
                   jax.ShapeDtypeStruct((m, n_kv * LANE), BF16),
                   jax.ShapeDtypeStruct((m // seq_len * tm * n_w, LANE), F32),
                   jax.ShapeDtypeStruct((m // CMP_BLOCK, 2 * NSA_KV * LANE), F32),
                   jax.ShapeDtypeStruct((m, nm), F32), jax.ShapeDtypeStruct((ms, ZB_WIDTH), F32)],
        grid=(m // tm,),
        in_specs=[row(d), resident((1, d)), resident(w_b.shape), tspec, tspec, tspec, resident(a4.shape),
                  resident((ms, d))],
        out_specs=[pl.BlockSpec((tm * n_kv, LANE), lambda i: (i, 0)), row(n_kv * LANE),
                   pl.BlockSpec((tm * n_w, LANE), lambda i: (i // blocks_per_seq, 0)),
                   pl.BlockSpec((tm // CMP_BLOCK, 2 * NSA_KV * LANE), lambda i: (i, 0)), row(nm),
                   pl.BlockSpec((ms, ZB_WIDTH), lambda i: (0, 0))],
        scratch_shapes=[pltpu.VMEM((tm, d), BF16)],
        compiler_params=_params(("arbitrary",)),
        name="proj_kv",
    )(x2, g.reshape(1, d), w_b, *tabs, a4, xs2)


def _cast_rows_kernel(w_ref, o_ref):
    o_ref[...] = w_ref[0].astype(BF16)


def _pack_gate_kernel(w_ref, o_ref):
    o_ref[...] = jnp.zeros(o_ref.shape, BF16)
    o_ref[0:w_ref.shape[1], :] = w_ref[0].astype(BF16)


def _pack_in_proj_weights(w_t3, layer):
    _, n_in, d = w_t3.shape
    tn = 512
    assert n_in == BG_SRC + BG_N + ZB_WIDTH and BG_SRC % tn == 0 and ZB_WIDTH % tn == 0 and BG_N % 8 == 0

    def cast_rows(first_row, n_rows, name):
        return pl.pallas_call(
            _cast_rows_kernel,
            out_shape=jax.ShapeDtypeStruct((n_rows, d), BF16),
            grid=(n_rows // tn,),
            in_specs=[pl.BlockSpec((pl.Element(1), pl.Element(tn), pl.Element(d)),
                                   lambda j: (layer, pl.multiple_of(first_row + j * tn, 8), 0))],
            out_specs=pl.BlockSpec((tn, d), lambda j: (j, 0)),
            compiler_params=_params(("parallel",)),
            name=name,
        )(w_t3)

    wa = cast_rows(0, ZA_WIDTH, "pack_w_a")
    wb = cast_rows(BG_SRC + BG_N, ZB_WIDTH, "pack_w_b")
    wbg = pl.pallas_call(
        _pack_gate_kernel,
        out_shape=jax.ShapeDtypeStruct((LANE, d), BF16),
        grid=(1,),
        in_specs=[pl.BlockSpec((pl.Element(1), pl.Element(BG_N), pl.Element(d)), lambda j: (layer, BG_SRC, 0))],
        out_specs=pl.BlockSpec((LANE, d), lambda j: (0, 0)),
        compiler_params=_params(("arbitrary",)),
        name="pack_w_gate",
    )(w_t3)
    return wa, wb, wbg


def _pool_pages_kernel(pt_sm, a_ref, cache_ref, o_ref, buf, sem, *, pages_per_step, page_base):
    step = pl.program_id(0)
    n_steps = pl.num_programs(0)
    slot = step % 2
    n_cols = buf.shape[3]

    def page_copy(step_idx, p, to_slot):
        page = pt_sm[step_idx * pages_per_step + p] + page_base
        return pltpu.make_async_copy(cache_ref.at[page, :, pl.ds(0, n_cols), :], buf.at[to_slot, p], sem.at[to_slot])

    @pl.when(step == 0)
    def _():
        for p in range(pages_per_step):
            page_copy(0, p, 0).start()

    @pl.when(step + 1 < n_steps)
    def _():
        for p in range(pages_per_step):
            page_copy(step + 1, p, 1 - slot).start()

    for p in range(pages_per_step):
        page_copy(step, p, slot).wait()

    a = a_ref[...]
    per = buf.shape[2] // CMP_BLOCK
    for p in range(pages_per_step):
        for k in range(per):
            x = buf[slot, p, pl.ds(k * CMP_BLOCK, CMP_BLOCK)]
            o_ref[0, p, k] = jnp.sum(x * a, axis=0)


def _pool_pages(cache4, page_flat, a3, *, bs, n_pages, page_base, pages_per_step=16):
    page = cache4.shape[1]
    per = page // CMP_BLOCK
    n_cols = 2 * NSA_KV
    total = bs * n_pages
    pages_per_step = min(pages_per_step, total)
    assert total % pages_per_step == 0
    n_steps = total // pages_per_step
    grid_spec = pltpu.PrefetchScalarGridSpec(
        num_scalar_prefetch=1, grid=(n_steps,),
        in_specs=[pl.BlockSpec((CMP_BLOCK, n_cols, LANE), lambda si, pt: (0, 0, 0)),
                  pl.BlockSpec(memory_space=pl.ANY)],
        out_specs=pl.BlockSpec((1, pages_per_step, per, n_cols, LANE), lambda si, pt: (si, 0, 0, 0, 0)),
        scratch_shapes=[pltpu.VMEM((2, pages_per_step, page, n_cols, LANE), F32),
                        pltpu.SemaphoreType.DMA((2,))])
    out = pl.pallas_call(
        functools.partial(_pool_pages_kernel, pages_per_step=pages_per_step, page_base=page_base),
        out_shape=jax.ShapeDtypeStruct((n_steps, pages_per_step, per, n_cols, LANE), F32),
        grid_spec=grid_spec,
        compiler_params=_params(("arbitrary",)),
        name="pool_pages",
    )(page_flat, a3, cache4)
    return out.reshape(bs, n_pages * per, n_cols * LANE)


def _cmp_proj_kernel(p_ref, w_ref, kc_ref, vc_ref):
    pooled = p_ref[0]
    n = pooled.shape[0]
    n_pad = kc_ref.shape[2]
    for c in range(4):
        r = _nn(pooled[:, c * LANE:(c + 1) * LANE].astype(BF16), w_ref[c]).astype(BF16)
        dst = kc_ref if c < 2 else vc_ref
        if n_pad > n:
            dst[0, c % 2] = jnp.zeros((n_pad, LANE), BF16)
        dst[0, c % 2, 0:n, :] = r


def _cmp_proj(pooled, w4):
    b, n, _ = pooled.shape
    n_pad = -(-n // LANE) * LANE
    spec = pl.BlockSpec((1, NSA_KV, n_pad, LANE), lambda bi: (bi, 0, 0, 0))
    return pl.pallas_call(
        _cmp_proj_kernel,
        out_shape=[jax.ShapeDtypeStruct((b, NSA_KV, n_pad, LANE), BF16)] * 2,
        grid=(b,),
        in_specs=[pl.BlockSpec((1, n, 512), lambda bi: (bi, 0, 0)),
                  pl.BlockSpec((4, LANE, LANE), lambda bi: (0, 0, 0))],
        out_specs=[spec, spec],
        compiler_params=_params(("parallel",)),
        name="cmp_proj",
    )(pooled, w4)


def _lane_parts(x):
    return [x[:, j * LANE:(j + 1) * LANE] for j in range(x.shape[1] // LANE)]


def _nsa_prompt_kernel(*refs, t_len, tq, tc, tw, n_sel, top, sub, pool_pages, page_base):
    if pool_pages:
        (pt_sm, qc_ref, qr_ref, ng_ref, bg_ref, ksel_ref, vsel_ref, kwin_ref, vwin_ref, kc_ref, vc_ref,
         pa_ref, cache_ref, o_ref, pool_ref,
         qc_scr, qr_scr, s_scr, p_scr, a_scr, m_scr, l_scr, acc_scr, oc_scr, b_scr, pbuf, psem) = refs
        step = (pl.program_id(0) * pl.num_programs(1) + pl.program_id(1)) * pl.num_programs(2) + pl.program_id(2)
        n_steps = pl.num_programs(0) * pl.num_programs(1) * pl.num_programs(2)
        slot = step % 2
        half_rows, n_cols = pbuf.shape[2], 2 * NSA_KV

        def page_copies(step_idx, p, to_slot):
            page = pt_sm[step_idx * pool_pages + p] + page_base
            return [pltpu.make_async_copy(cache_ref.at[page, pl.ds(hh * half_rows, half_rows), pl.ds(0, n_cols), :],
                                          pbuf.at[to_slot, p, :, pl.ds(hh * n_cols, n_cols), :], psem.at[to_slot])
                    for hh in range(2)]

        @pl.when(step == 0)
        def _():
            for p in range(pool_pages):
                for cp in page_copies(0, p, 0):
                    cp.start()

        @pl.when(step + 1 < n_steps)
        def _():
            for p in range(pool_pages):
                for cp in page_copies(step + 1, p, 1 - slot):
                    cp.start()
    else:
        (qc_ref, qr_ref, ng_ref, bg_ref, ksel_ref, vsel_ref, kwin_ref, vwin_ref, kc_ref, vc_ref,
         o_ref, qc_scr, qr_scr, s_scr, p_scr, a_scr, m_scr, l_scr, acc_scr, oc_scr, b_scr) = refs
    i = pl.program_id(2)
    hq = NSA_HPG
    for h in range(hq):
        qc_scr[pl.ds(h * tq, tq), :] = qc_ref[0, :, h * LANE:(h + 1) * LANE]
        qr_scr[pl.ds(h * tq, tq), 0:LANE] = qr_ref[0, :, h * LANE:(h + 1) * LANE]

    kc = kc_ref[0, 0]
    npad = kc.shape[0]
    s_scr[:, 0:npad] = _nt(qc_scr[...], kc)
    tpos = i * tq + lax.broadcasted_iota(jnp.int32, (tq, npad), 0)
    ncol = lax.broadcasted_iota(jnp.int32, (tq, npad), 1)
    cmask = (ncol + 1) * CMP_BLOCK <= tpos + 1
    imp = jnp.zeros((tq, npad), F32)
    for h in range(hq):
        r = pl.ds(h * tq, tq)
        s = jnp.where(cmask, s_scr[r, 0:npad], MASK_NEG)
        e = jnp.where(cmask, jnp.exp2(s - jnp.max(s, axis=-1, keepdims=True)), 0.0)
        p = e / jnp.maximum(jnp.sum(e, axis=-1, keepdims=True), 1e-30)
        p_scr[r, 0:npad] = p.astype(BF16)
        imp = imp + p
    oc_scr[...] = _nn(p_scr[:, 0:npad], vc_ref[0, 0])

    rows = min(npad, -(-n_sel // 8) * 8)
    imp_t = imp.T[0:rows]
    blk = lax.broadcasted_iota(jnp.int32, (rows, tq), 0)
    cur = lax.shift_right_logical(i * tq + lax.broadcasted_iota(jnp.int32, (rows, tq), 1), SEL_SHIFT)
    imp_t = jnp.where((blk == 0) | (blk == cur) | (blk == cur - 1), FORCE, imp_t)
    imp_t = jnp.where(blk > cur, -1.0, imp_t)
    imp_t = jnp.where(blk >= n_sel, -2.0, imp_t)
    rank = jnp.zeros((rows, tq), F32)
    for j in range(n_sel):
        a = imp_t[j:j + 1, :]
        ahead = (a > imp_t) | ((a == imp_t) & (blk > j))
        rank = rank + jnp.where(ahead, 1.0, 0.0)
    neg_t = jnp.where((rank < top) & (blk < n_sel) & (blk <= cur), 0.0, MASK_NEG)
    if npad > rows:
        neg_t = jnp.concatenate([neg_t, jnp.zeros((npad - rows, tq), F32)], axis=0)
    neg = neg_t.T.astype(BF16)
    for h in range(hq):
        qr_scr[pl.ds(h * tq, tq), LANE:2 * LANE] = neg

    m_scr[...] = jnp.full(m_scr.shape, MASK_NEG, F32)
    l_scr[...] = jnp.zeros(l_scr.shape, F32)
    acc_scr[...] = jnp.zeros(acc_scr.shape, F32)
    n_chunks = (i * tq + tq + tc - 1) // tc

    def chunk(ci, diagonal):
        k0 = pl.multiple_of(ci * tc, tc)
        v = vsel_ref[0, pl.ds(k0, tc), :]
        kb = lax.shift_right_logical(k0 + lax.broadcasted_iota(jnp.int32, (tc, npad), 0), SEL_SHIFT)
        jb = lax.broadcasted_iota(jnp.int32, (tc, npad), 1)
        k_aug = jnp.concatenate([ksel_ref[0, pl.ds(k0, tc), :], jnp.where(kb == jb, 1.0, 0.0).astype(BF16)], axis=1)
        s_scr[:, 0:tc] = _nt(qr_scr[...], k_aug)
        if diagonal:
            kp = k0 + lax.broadcasted_iota(jnp.int32, (tq, tc), 1)
            tp = i * tq + lax.broadcasted_iota(jnp.int32, (tq, tc), 0)
            b_scr[:, 0:tc] = jnp.where(kp <= tp, 0.0, MASK_NEG)
        for r0 in range(0, hq * tq, sub):
            r = pl.ds(r0, sub)
            sc = s_scr[r, 0:tc]
            if diagonal:
                sc = sc + b_scr[pl.ds(r0 % tq, sub), 0:tc]
            parts = _lane_parts(sc)
            m_prev = m_scr[r, :]
            m_new = jnp.maximum(m_prev, jnp.max(functools.reduce(jnp.maximum, parts), axis=-1, keepdims=True))
            alpha = jnp.exp2(m_prev - m_new)
            pes = [jnp.exp2(x - m_new) for x in parts]
            l_scr[r, :] = alpha * l_scr[r, :] + functools.reduce(jnp.add, pes)
            p_scr[r, 0:tc] = jnp.concatenate(pes, axis=1).astype(BF16)
            a_scr[r, :] = alpha
            m_scr[r, :] = m_new
        acc_scr[...] = a_scr[...] * acc_scr[...] + _nn(p_scr[:, 0:tc], v)

    def full_chunk(ci, carry):
        chunk(ci, False)
        return carry

    lax.fori_loop(0, n_chunks - 1, full_chunk, 0)
    chunk(n_chunks - 1, True)

    w0 = pl.multiple_of(jnp.clip(i * tq + tq - tw, 0, t_len - tw), LANE)
    kp = w0 + lax.broadcasted_iota(jnp.int32, (tq, tw), 1)
    tp = i * tq + lax.broadcasted_iota(jnp.int32, (tq, tw), 0)
    b_scr[:, 0:tw] = jnp.where((kp <= tp) & (kp > tp - WINDOW), 0.0, MASK_NEG)
    s_scr[:, 0:tw] = _nt(qr_scr[:, 0:LANE], kwin_ref[0, pl.ds(w0, tw), :])
    for r0 in range(0, hq * tq, sub):
        r = pl.ds(r0, sub)
        parts = _lane_parts(s_scr[r, 0:tw] + b_scr[pl.ds(r0 % tq, sub), 0:tw])
        m = jnp.max(functools.reduce(jnp.maximum, parts), axis=-1, keepdims=True)
        ews = [jnp.exp2(x - m) for x in parts]
        row_sum = jnp.sum(functools.reduce(jnp.add, ews), axis=-1, keepdims=True)
        a_scr[r, :] = jnp.broadcast_to(1.0 / row_sum, (sub, LANE))
        p_scr[r, 0:tw] = jnp.concatenate(ews, axis=1).astype(BF16)
    o_w = _nn(p_scr[:, 0:tw], vwin_ref[0, pl.ds(w0, tw), :])

    gate = jax.nn.sigmoid(bg_ref[0])
    for g in range(1, NSA_KV):
        gate = jnp.where(pl.program_id(1) == g, pltpu.roll(gate, LANE - g * hq * N_BRANCH, 1), gate)
    ng = ng_ref[0]
    for h in range(hq):
        r = slice(h * tq, (h + 1) * tq)
        o_s = acc_scr[r, :] * (1.0 / jnp.sum(l_scr[r, :], axis=-1, keepdims=True))
        o = (gate[:, 3 * h:3 * h + 1] * oc_scr[r, :] + gate[:, 3 * h + 1:3 * h + 2] * o_s
             + gate[:, 3 * h + 2:3 * h + 3] * (o_w[r] * a_scr[r, :]))
        o_ref[0, :, h * LANE:(h + 1) * LANE] = (ng[:, h * LANE:(h + 1) * LANE] * o).astype(o_ref.dtype)

    if pool_pages:
        for p in range(pool_pages):
            for cp in page_copies(step, p, slot):
                cp.wait()
        pa = pa_ref[...]
        for p in range(pool_pages):
            pool_ref[0, p] = jnp.sum(pbuf[slot, p] * pa, axis=0)


def _nsa_prompt(qc3, qr3, ng3, bg3, kvb, kc, vc, *, tq=128, tc=512, sub=64, pool=None):
    b, t, _ = qc3.shape
    nq = t // tq
    tc = min(tc, t)
    tw = min(WINDOW + tq, t)
    n_sel = t // SEL_BLOCK
    top = min(TOP_N, n_sel)
    npad = kc.shape[2]
    gw = NSA_HPG * LANE
    rows = NSA_HPG * tq
    wide = max(tc, tw, npad)
    kvspec = lambda k: pl.BlockSpec((1, t, LANE), lambda bi, gi, qi, *_, k=k: (bi, 0, k + gi))
    cspec = pl.BlockSpec((1, 1, npad, LANE), lambda bi, gi, qi, *_: (bi, gi, 0, 0))
    gspec = pl.BlockSpec((1, tq, gw), lambda bi, gi, qi, *_: (bi, qi, gi))
    in_specs = [gspec, gspec, gspec,
                pl.BlockSpec((1, tq, LANE), lambda bi, gi, qi, *_: (bi, qi, 0)),
                kvspec(0), kvspec(2), kvspec(4), kvspec(6), cspec, cspec]
    out_shape = [jax.ShapeDtypeStruct((b, t, NSA_HEADS * LANE), BF16)]
    out_specs = [pl.BlockSpec((1, tq, gw), lambda bi, gi, qi, *_: (bi, qi, gi))]
    assert tc % tq == 0 and npad == LANE
    scratch = ([pltpu.VMEM((rows, LANE), BF16), pltpu.VMEM((rows, 2 * LANE), BF16)]
               + [pltpu.VMEM((rows, wide), F32), pltpu.VMEM((rows, wide), BF16)]
               + [pltpu.VMEM((rows, LANE), F32)] * 5 + [pltpu.VMEM((tq, wide), F32)])
    args = [qc3, qr3, ng3, bg3, kvb, kvb, kvb, kvb, kc, vc]
    n_steps = b * NSA_KV * nq
    pool_pages, page_base, prefetch = 0, 0, []
    if pool is not None:
        cache4, page_flat, a3, page_base = pool
        page, n_cols = cache4.shape[1], 2 * NSA_KV
        if page_flat.shape[0] % n_steps == 0 and page == 2 * CMP_BLOCK and 2 * n_cols == 8:
            pool_pages = page_flat.shape[0] // n_steps
            prefetch = [page_flat]
            in_specs += [pl.BlockSpec((CMP_BLOCK, 2 * n_cols, LANE), lambda bi, gi, qi, *_: (0, 0, 0)),
                         pl.BlockSpec(memory_space=pl.ANY)]
            args += [jnp.concatenate([a3, a3], axis=1), cache4]
            out_shape.append(jax.ShapeDtypeStruct((n_steps, pool_pages, 2 * n_cols, LANE), F32))
            out_specs.append(pl.BlockSpec((1, pool_pages, 2 * n_cols, LANE),
                                          lambda bi, gi, qi, *_: ((bi * NSA_KV + gi) * nq + qi, 0, 0, 0)))
            scratch += [pltpu.VMEM((2, pool_pages, CMP_BLOCK, 2 * n_cols, LANE), F32), pltpu.SemaphoreType.DMA((2,))]
    kern = functools.partial(_nsa_prompt_kernel, t_len=t, tq=tq, tc=tc, tw=tw, n_sel=n_sel, top=top,
                             sub=min(sub, tq), pool_pages=pool_pages, page_base=page_base)
    res = pl.pallas_call(
        kern,
        out_shape=out_shape,
        grid_spec=pltpu.PrefetchScalarGridSpec(
            num_scalar_prefetch=len(prefetch), grid=(b, NSA_KV, nq),
            in_specs=in_specs, out_specs=out_specs, scratch_shapes=scratch),
        compiler_params=_params(("arbitrary", "arbitrary", "arbitrary")),
        name="nsa_prompt",
    )(*prefetch, *args)
    return (res[0], res[1]) if pool_pages else (res[0], None)


def _conv_kernel(h_ref, b_ref, c_ref, g_ref, prev_ref, w_ref, y_ref, st_ref, up_scr, *, t_real):
    u = c_ref[...] * h_ref[...]
    t = u.shape[1]
    up_scr[:, pl.ds(8 - (CONV_W - 1), CONV_W - 1), :] = prev_ref[...]
    up_scr[:, pl.ds(8, t), :] = u
    w = w_ref[...]
    y = w[0:1, :][None] * up_scr[:, pl.ds(6, t), :]
    y = y + w[1:2, :][None] * up_scr[:, pl.ds(7, t), :]
    y = y + w[2:3, :][None] * u
    y = b_ref[...] * y
    y_ref[...] = (_silu(g_ref[...]) * y).astype(y_ref.dtype)
    st_ref[...] = up_scr[:, pl.ds(6 + t_real, CONV_W - 1), :]


def _conv_mixer(z3, prev, w_conv, *, t_real):
    b, t, _ = z3.shape
    nc = CONV_DIM // LANE
    zspec = lambda off: pl.BlockSpec((b, t, LANE), lambda ci, off=off: (0, 0, off // LANE + ci))
    return pl.pallas_call(
        functools.partial(_conv_kernel, t_real=t_real),
        out_shape=[jax.ShapeDtypeStruct((b, t, CONV_DIM), BF16),
                   jax.ShapeDtypeStruct((b, CONV_W - 1, CONV_DIM), F32)],
        grid=(nc,),
        in_specs=[zspec(C_H), zspec(C_B), zspec(C_C), zspec(C_G),
                  pl.BlockSpec((b, CONV_W - 1, LANE), lambda ci: (0, 0, ci)),
                  pl.BlockSpec((CONV_W, LANE), lambda ci: (0, ci))],
        out_specs=[pl.BlockSpec((b, t, LANE), lambda ci: (0, 0, ci)),
                   pl.BlockSpec((b, CONV_W - 1, LANE), lambda ci: (0, 0, ci))],
        scratch_shapes=[pltpu.VMEM((b, t + 8, LANE), F32)],
        compiler_params=_params(("parallel",)),
        name="conv_mixer",
    )(z3, z3, z3, z3, prev, w_conv)


def _mem_attn_kernel(q_ref, mg_ref, kv_ref, o_ref, *, interleaved):
    q = q_ref[0]
    mg = mg_ref[0]
    half = MEM_HEADS * LANE
    for h in range(MEM_HEADS):
        lo, hi = h * LANE, (h + 1) * LANE
        if interleaved:
            nm = kv_ref.shape[1] // (2 * MEM_HEADS)
            k = kv_ref[0, pl.ds(h, nm, stride=2 * MEM_HEADS), :].astype(BF16)
            v = kv_ref[0, pl.ds(MEM_HEADS + h, nm, stride=2 * MEM_HEADS), :].astype(BF16)
        else:
            k = kv_ref[0, :, lo:hi].astype(BF16)
            v = kv_ref[0, :, half + lo:half + hi].astype(BF16)
        s = _nt((q[:, lo:hi] * ATTN_SCALE).astype(BF16), k)
        e = jnp.exp(s - jnp.max(s, axis=-1, keepdims=True))
        o = _nn(e.astype(BF16), v) / jnp.sum(e, axis=-1, keepdims=True)
        o_ref[0, :, lo:hi] = (_silu(mg[:, lo:hi]) * o).astype(o_ref.dtype)


def _mem_attn(zb3, mkv, *, mq_off, mg_off, tq, interleaved, kv_base=0):
    b, t, _ = zb3.shape
    wq = MEM_HEADS * LANE
    return pl.pallas_call(
        functools.partial(_mem_attn_kernel, interleaved=interleaved),
        out_shape=jax.ShapeDtypeStruct((b, t, wq), BF16),
        grid=(b, t // tq),
        in_specs=[pl.BlockSpec((1, tq, wq), lambda bi, ti: (bi, ti, mq_off // wq)),
                  pl.BlockSpec((1, tq, wq), lambda bi, ti: (bi, ti, mg_off // wq)),
                  pl.BlockSpec((1,) + mkv.shape[1:], lambda bi, ti: (bi + kv_base, 0, 0))],
        out_specs=pl.BlockSpec((1, tq, wq), lambda bi, ti: (bi, ti, 0)),
        compiler_params=_params(("parallel", "parallel")),
        name="mem_attn",
    )(zb3, zb3, mkv)


def _out_proj_kernel(x_ref, ya_ref, yb_ref, ym_ref, w_ref, fg_ref, o_ref, *, final):
    a, bw = CONV_DIM, CONV_DIM + NSA_HEADS * LANE
    acc = _nn(ya_ref[...], w_ref[0:a, :])
    acc = acc + _nn(yb_ref[...], w_ref[a:bw, :])
    acc = acc + _nn(ym_ref[...], w_ref[bw:, :])
    r = x_ref[...] + acc
    if final:
        r = r * lax.rsqrt(jnp.mean(r * r, axis=-1, keepdims=True) + NORM_EPS) * fg_ref[...]
    o_ref[...] = r


def _out_proj(x, ya, yb, ym, w, fg, *, tm, final):
    m, d = x.shape
    row = lambda width: pl.BlockSpec((tm, width), lambda i: (i, 0))
    return pl.pallas_call(
        functools.partial(_out_proj_kernel, final=final),
        out_shape=jax.ShapeDtypeStruct((m, d), F32),
        grid=(m // tm,),
        in_specs=[row(d), row(ya.shape[1]), row(yb.shape[1]), row(ym.shape[1]),
                  pl.BlockSpec(w.shape, lambda i: (0, 0), pipeline_mode=pl.Buffered(1)),
                  pl.BlockSpec((1, d), lambda i: (0, 0), pipeline_mode=pl.Buffered(1))],
        out_specs=row(d),
        compiler_params=_params(("parallel",)),
        name="out_proj",
    )(x, ya, yb, ym, w, fg.reshape(1, d))


def _sample_pre_kernel(q_ref, ng_ref, bg_ref, kv0_ref, kv1_ref, kv2_ref, c_ref, s1_ref, s2_ref, kc_ref, vc_ref,
                       kvn_ref, wn_ref, qr_ref, oc_ref, ngo_ref, gate_ref, val_ref, *, past):
    tp = SAMPLE_T_PAD
    hq = NSA_HPG
    c, s1, s2 = c_ref[...], s1_ref[...], s2_ref[...]
    kv0, kv1, kv2 = kv0_ref[0], kv1_ref[0], kv2_ref[0]
    kvn_ref[0, :, 0:512] = kv0
    kvn_ref[0, :, 768:1024] = kv1[:, 256:512]
    wn_ref[0, :, 256:512] = kv2[:, 256:512]
    for g in range(NSA_KV):
        lo, hi = g * LANE, (g + 1) * LANE
        kvn_ref[0, :, 512 + lo:512 + hi] = _rope(kv1[:, lo:hi], c, s1, s2)
        wn_ref[0, :, lo:hi] = _rope(kv2[:, lo:hi], c, s1, s2)

    q = q_ref[0]
    ng = ng_ref[0]
    gates = jax.nn.sigmoid(bg_ref[0])
    for g in range(NSA_KV):
        qc_l, qr_l = [], []
        for h in range(hq):
            lo = (g * hq + h) * LANE
            qh = q[:, lo:lo + LANE]
            qc_l.append(qh * ATTN_SCALE)
            qr_l.append(_rope(qh, c, s1, s2) * ATTN_SCALE)
            ngo_ref[0, g, h * tp:(h + 1) * tp, :] = ng[:, lo:lo + LANE]
            for br in range(N_BRANCH):
                col = (g * hq + h) * N_BRANCH + br
                gate_ref[0, g, br, h * tp:(h + 1) * tp, :] = jnp.broadcast_to(gates[:, col:col + 1], (tp, LANE))
        qc = jnp.concatenate(qc_l, axis=0)
        qr_ref[0, g] = jnp.concatenate(qr_l, axis=0)

        kc = kc_ref[0, g]
        npad = kc.shape[0]
        s = _nt(qc.astype(BF16), kc)
        trow = lax.broadcasted_iota(jnp.int32, (hq * tp, npad), 0) % tp
        ncol = lax.broadcasted_iota(jnp.int32, (hq * tp, npad), 1)
        cmask = (ncol + 1) * CMP_BLOCK <= past + trow + 1
        s = jnp.where(cmask, s, MASK_NEG)
        e = jnp.where(cmask, jnp.exp(s - jnp.max(s, axis=-1, keepdims=True)), 0.0)
        p = e / jnp.maximum(jnp.sum(e, axis=-1, keepdims=True), 1e-30)
        oc_ref[0, g] = _nn(p.astype(BF16), vc_ref[0, g])
        imp = jnp.sum(p.reshape(hq, tp, npad), axis=0)

        blk = lax.broadcasted_iota(jnp.int32, (tp, npad), 1)
        cur = (past + lax.broadcasted_iota(jnp.int32, (tp, npad), 0)) // SEL_BLOCK
        val = jnp.where((blk == 0) | (blk == cur) | (blk == cur - 1), FORCE, imp)
        val = jnp.where(blk > cur, -1.0, val)
        val = jnp.where(blk >= past // SEL_BLOCK, -2.0, val)
        val_ref[0, g] = val


def _topk_kernel(val_ref, idx_ref, *, n_top):
    val = val_ref[...]
    rows, n = val.shape
    blk = lax.broadcasted_iota(jnp.int32, (rows, n), 1)
    lane = lax.broadcasted_iota(jnp.int32, (rows, LANE), 1)
    idx = jnp.zeros((rows, LANE), jnp.int32)
    for r in range(n_top):
        best = jnp.max(val, axis=-1, keepdims=True)
        j = jnp.min(jnp.where(val == best, blk, n), axis=-1, keepdims=True)
        idx = jnp.where(lane == r, j, idx)
        val = jnp.where(blk == j, -3e38, val)
    idx_ref[...] = idx


def _topk(val2, *, n_top):
    rows, n = val2.shape
    return pl.pallas_call(
        functools.partial(_topk_kernel, n_top=n_top),
        out_shape=jax.ShapeDtypeStruct((rows, LANE), jnp.int32),
        grid=(1,),
        in_specs=[pl.BlockSpec((rows, n), lambda i: (0, 0))],
        out_specs=pl.BlockSpec((rows, LANE), lambda i: (0, 0)),
        compiler_params=_params(("arbitrary",)),
        name="sample_topk",
    )(val2)


def _sample_pre(za3, zb3, bg3, tabs, kc, vc, *, past):
    bs, tp, _ = za3.shape
    npad = kc.shape[2]
    qw = NSA_HEADS * LANE
    kvblk = KV_OFF // 512
    zspec = lambda k: pl.BlockSpec((1, tp, 512), lambda bi, k=k: (bi, 0, kvblk + k))
    tspec = pl.BlockSpec((tp, LANE), lambda bi: (0, 0))
    cspec = pl.BlockSpec((1, NSA_KV, npad, LANE), lambda bi: (bi, 0, 0, 0))
    rows = NSA_HPG * tp
    gspec = pl.BlockSpec((1, NSA_KV, rows, LANE), lambda bi: (bi, 0, 0, 0))
    gshape = jax.ShapeDtypeStruct((bs, NSA_KV, rows, LANE), F32)
    return pl.pallas_call(
        functools.partial(_sample_pre_kernel, past=past),
        out_shape=[jax.ShapeDtypeStruct((bs, tp, 1024), F32),
                   jax.ShapeDtypeStruct((bs, tp, 512), F32),
                   gshape, gshape, gshape,
                   jax.ShapeDtypeStruct((bs, NSA_KV, N_BRANCH, rows, LANE), F32),
                   jax.ShapeDtypeStruct((bs, NSA_KV, tp, npad), F32)],
        grid=(bs,),
        in_specs=[pl.BlockSpec((1, tp, qw), lambda bi: (bi, 0, Q_OFF // qw)),
                  pl.BlockSpec((1, tp, qw), lambda bi: (bi, 0, NG_OFF // qw)),
                  pl.BlockSpec((1, tp, LANE), lambda bi: (bi, 0, 0)),
                  zspec(0), zspec(1), zspec(2), tspec, tspec, tspec, cspec, cspec],
        out_specs=[pl.BlockSpec((1, tp, 1024), lambda bi: (bi, 0, 0)),
                   pl.BlockSpec((1, tp, 512), lambda bi: (bi, 0, 0)),
                   gspec, gspec, gspec,
                   pl.BlockSpec((1, NSA_KV, N_BRANCH, rows, LANE), lambda bi: (bi, 0, 0, 0, 0)),
                   pl.BlockSpec((1, NSA_KV, tp, npad), lambda bi: (bi, 0, 0, 0))],
        compiler_params=_params(("parallel",)),
        name="sample_pre",
    )(za3, za3, bg3, zb3, zb3, zb3, *tabs, kc, vc)


def _sample_attn_kernel(idx_sm, pt_sm, qr_ref, oc_ref, ng_ref, gate_ref, ksn_ref, vsn_ref,
                        wc_ref, kwn_ref, vwn_ref, cache_ref, o_ref,
                        kbuf, vbuf, kw_scr, vw_scr, sem, *, ts, n_top, n_pages, page_base, per_page, wb):
    tp = SAMPLE_T_PAD
    hq = NSA_HPG
    b = pl.program_id(0)
    g = pl.program_id(1)
    n_gath = n_top * SEL_BLOCK
    ks_rows = kbuf.shape[2]
    step = b * NSA_KV + g
    n_steps = pl.num_programs(0) * NSA_KV
    slot = step % 2

    def gather_copies(step_idx, t, r, to_slot):
        bb, gg = step_idx // NSA_KV, step_idx % NSA_KV
        blk = idx_sm[(step_idx * ts + t) * n_top + r]
        page = pt_sm[bb * n_pages + blk // per_page] + page_base
        row0 = (blk % per_page) * SEL_BLOCK
        src_k = cache_ref.at[page, pl.ds(row0, SEL_BLOCK), 2 * NSA_KV + gg]
        src_v = cache_ref.at[page, pl.ds(row0, SEL_BLOCK), 3 * NSA_KV + gg]
        dst = pl.ds(r * SEL_BLOCK, SEL_BLOCK)
        return (pltpu.make_async_copy(src_k, kbuf.at[to_slot, t, dst], sem.at[to_slot, 0]),
                pltpu.make_async_copy(src_v, vbuf.at[to_slot, t, dst], sem.at[to_slot, 1]))

    def start_gathers(step_idx, to_slot):
        for t in range(ts):
            for r in range(n_top):
                ck, cv = gather_copies(step_idx, t, r, to_slot)
                ck.start()
                cv.start()

    @pl.when(step == 0)
    def _():
        start_gathers(0, 0)

    @pl.when(step + 1 < n_steps)
    def _():
        start_gathers(step + 1, 1 - slot)

    qr = qr_ref[0, 0].astype(BF16)
    trow = lax.broadcasted_iota(jnp.int32, (hq * tp, 1), 0) % tp

    ww = kw_scr.shape[0]
    kw_scr[pl.ds(0, wb), :] = wc_ref[0, pl.ds(g, wb, stride=2 * NSA_KV), :]
    vw_scr[pl.ds(0, wb), :] = wc_ref[0, pl.ds(NSA_KV + g, wb, stride=2 * NSA_KV), :]
    kw_scr[pl.ds(wb, tp), :] = kwn_ref[0]
    vw_scr[pl.ds(wb, tp), :] = vwn_ref[0]
    kw_scr[pl.ds(wb + tp, ww - wb - tp), :] = jnp.zeros((ww - wb - tp, LANE), F32)
    vw_scr[pl.ds(wb + tp, ww - wb - tp), :] = jnp.zeros((ww - wb - tp, LANE), F32)
    sw = _nt(qr, kw_scr[...].astype(BF16))
    jw = lax.broadcasted_iota(jnp.int32, (hq * tp, ww), 1)
    rel = jw - wb
    okw = (rel <= trow) & (rel > trow - WINDOW) & (jw < wb + ts)
    sw = jnp.where(okw, sw, MASK_NEG)
    ew = jnp.where(okw, jnp.exp(sw - jnp.max(sw, axis=-1, keepdims=True)), 0.0)
    o_w = _nn(ew.astype(BF16), vw_scr[...].astype(BF16)) / jnp.sum(ew, axis=-1, keepdims=True)

    for t in range(ts):
        for r in range(n_top):
            ck, cv = gather_copies(step, t, r, slot)
            ck.wait()
            cv.wait()

    js = lax.broadcasted_iota(jnp.int32, (hq * tp, ks_rows), 1)
    o_s = jnp.zeros((hq * tp, LANE), F32)
    for t in range(ts):
        kbuf[slot, t, pl.ds(n_gath, tp), :] = ksn_ref[0]
        vbuf[slot, t, pl.ds(n_gath, tp), :] = vsn_ref[0]
        kbuf[slot, t, pl.ds(n_gath + tp, ks_rows - n_gath - tp), :] = jnp.zeros((ks_rows - n_gath - tp, LANE), F32)
        vbuf[slot, t, pl.ds(n_gath + tp, ks_rows - n_gath - tp), :] = jnp.zeros((ks_rows - n_gath - tp, LANE), F32)
        ss = _nt(qr, kbuf[slot, t].astype(BF16))
        oks = (js < n_gath) | ((js - n_gath <= t) & (js < n_gath + ts))
        ss = jnp.where(oks, ss, MASK_NEG)
        es = jnp.where(oks, jnp.exp(ss - jnp.max(ss, axis=-1, keepdims=True)), 0.0)
        ot = _nn(es.astype(BF16), vbuf[slot, t].astype(BF16)) / jnp.sum(es, axis=-1, keepdims=True)
        o_s = jnp.where(trow == t, ot, o_s)

    o = gate_ref[0, 0, 0] * oc_ref[0, 0] + gate_ref[0, 0, 1] * o_s + gate_ref[0, 0, 2] * o_w
    y = _silu(ng_ref[0, 0]) * o
    for h in range(hq):
        o_ref[0, :, h * LANE:(h + 1) * LANE] = y[h * tp:(h + 1) * tp].astype(o_ref.dtype)


def _sample_attn(idx_flat, page_flat, qr, oc, ngo, gates, kvn, cache_win_rows, wn, cache4, *,
                 ts, n_top, n_pages, page_base, win_base):
    bs = qr.shape[0]
    tp = SAMPLE_T_PAD
    rows = NSA_HPG * tp
    wb = cache_win_rows.shape[1] // (2 * NSA_KV)
    per_page = cache4.shape[1] // SEL_BLOCK
    ks_rows = -(-(n_top * SEL_BLOCK + tp) // LANE) * LANE
    ww = -(-(wb + tp) // LANE) * LANE
    gspec = pl.BlockSpec((1, 1, rows, LANE), lambda bi, gi, *_: (bi, gi, 0, 0))
    newspec = lambda k: pl.BlockSpec((1, tp, LANE), lambda bi, gi, *_, k=k: (bi, 0, k + gi))
    grid_spec = pltpu.PrefetchScalarGridSpec(
        num_scalar_prefetch=2, grid=(bs, NSA_KV),
        in_specs=[gspec, gspec, gspec,
                  pl.BlockSpec((1, 1, N_BRANCH, rows, LANE), lambda bi, gi, *_: (bi, gi, 0, 0, 0)),
                  newspec(2 * NSA_KV), newspec(3 * NSA_KV),
                  pl.BlockSpec((1,) + cache_win_rows.shape[1:], lambda bi, gi, *_: (bi + win_base, 0, 0)),
                  newspec(0), newspec(NSA_KV),
                  pl.BlockSpec(memory_space=pl.ANY)],
        out_specs=pl.BlockSpec((1, tp, NSA_HPG * LANE), lambda bi, gi, *_: (bi, 0, gi)),
        scratch_shapes=[pltpu.VMEM((2, ts, ks_rows, LANE), F32), pltpu.VMEM((2, ts, ks_rows, LANE), F32),
                        pltpu.VMEM((ww, LANE), F32), pltpu.VMEM((ww, LANE), F32),
                        pltpu.SemaphoreType.DMA((2, 2))])
    kern = functools.partial(_sample_attn_kernel, ts=ts, n_top=n_top, n_pages=n_pages,
                             page_base=page_base, per_page=per_page, wb=wb)
    return pl.pallas_call(
        kern,
        out_shape=jax.ShapeDtypeStruct((bs, tp, NSA_HEADS * LANE), BF16),
        grid_spec=grid_spec,
        compiler_params=_params(("arbitrary", "arbitrary")),
        name="sample_attn",
    )(idx_flat, page_flat, qr, oc, ngo, gates, kvn, kvn, cache_win_rows, wn, wn, cache4)


def _rope_tables(pos, rows):
    freqs = jnp.power(ROPE_THETA, -jnp.arange(ROPE_HALF, dtype=F32) * (2.0 / ROPE_DIM))
    ang = pos.astype(F32)[:, None] * freqs[None, :]
    cos, sin = jnp.cos(ang), jnp.sin(ang)
    n = pos.shape[0]
    z16 = jnp.zeros((n, ROPE_HALF), F32)
    rest = LANE - ROPE_DIM
    c = jnp.concatenate([cos, cos, jnp.ones((n, rest), F32)], axis=1)
    s1 = jnp.concatenate([z16, sin, jnp.zeros((n, rest), F32)], axis=1)
    s2 = jnp.concatenate([-sin, z16, jnp.zeros((n, rest), F32)], axis=1)
    pad = lambda a: jnp.pad(a, ((0, rows - n), (0, 0)))
    return pad(c), pad(s1), pad(s2)


def _layer_weights(norm_g, w_in3, layer, w_conv, a_cmp, w_cmp, w_out):
    w_a, w_b, wbg = _pack_in_proj_weights(jnp.swapaxes(w_in3, 1, 2), layer)
    a4 = jnp.concatenate([a_cmp[0], a_cmp[0], a_cmp[1], a_cmp[1]], axis=1)
    a3 = jnp.stack([a_cmp[0], a_cmp[0], a_cmp[1], a_cmp[1]], axis=1)
    w4 = jnp.stack([w_cmp[0], w_cmp[0], w_cmp[1], w_cmp[1]]).astype(BF16)
    return dict(norm_g=norm_g, w_a=w_a, w_b=w_b, wbg=wbg, w_conv=w_conv, a4=a4, a3=a3, w4=w4,
                w_out=w_out.astype(BF16))


def _prompt_layer(xp, xs2, mem_prompt, mem_norm_g, w_mem, lw, final_g, final, pool):
    b, t, d = xp.shape
    m = b * t
    x2 = xp.reshape(m, d)
    tabs = _rope_tables(jnp.arange(t, dtype=jnp.int32), t)
    wb = min(WINDOW, t)
    qc, qr, ng, bg, ya, conv_new, za_s, bg_s = _proj_conv(x2, lw["norm_g"], lw["w_a"], lw["wbg"], lw["w_conv"], tabs,
                                                          xs2, tm=min(512, t), seq_len=t)
    kvn, kvb, win_rows, pooled_prompt, zm, zb_s = _proj_kv(x2, lw["norm_g"], lw["w_b"], tabs, lw["a4"], xs2,
                                                           tm=wb, seq_len=t)
    nm = mem_prompt.shape[1]
    mkv = _norm_matmul(mem_prompt.reshape(b * nm, d), mem_norm_g, w_mem, tm=min(512, b * nm), tn=512)
    mkv3 = mkv.reshape(b, nm, 2 * MEM_HEADS * LANE)
    kc, vc = _cmp_proj(pooled_prompt.reshape(b, t // CMP_BLOCK, -1), lw["w4"])
    r3 = lambda a: a.reshape(b, t, -1)
    yb, pooled = _nsa_prompt(r3(qc), r3(qr), r3(ng), r3(bg), r3(kvb), kc, vc, tq=min(256, t), pool=pool)
    ym = _mem_attn(zm.reshape(b, t, -1), mkv3, mq_off=0, mg_off=MG_OFF - MQ_OFF, tq=min(512, t), interleaved=False)
    out = _out_proj(x2, ya, yb.reshape(m, -1), ym.reshape(m, -1),
                    lw["w_out"], final_g, tm=min(512, m), final=final)
    kv_new = kvn.reshape(b, t, 4, NSA_KV, HEAD_DIM)
    win_new = win_rows.reshape(b, wb, 2, NSA_KV, HEAD_DIM)
    mem_kv = mkv.reshape(b, nm, 2, MEM_HEADS, HEAD_DIM)
    return out.reshape(b, t, d), kv_new, win_new, conv_new, mem_kv, pooled, (za_s, zb_s, bg_s)


def _sample_layer(xs_p, proj, ts, layer, cache4, page_flat, pooled, cache_win, state_conv, cache_mem, lw, final_g,
                  final):
    bs, tp, d = xs_p.shape
    depth = cache_win.shape[0]
    pool, page = cache4.shape[0] // depth, cache4.shape[1]
    n_pages = page_flat.shape[0] // bs
    past = n_pages * page
    assert past % SEL_BLOCK == 0 and ts <= SEL_BLOCK and ts <= tp
    n_past = past // SEL_BLOCK
    n_top = min(TOP_N, n_past + 1) - 1
    m = bs * tp
    za, zb, bg = proj
    za3 = za.reshape(bs, tp, ZA_WIDTH)
    zb3 = zb.reshape(bs, tp, ZB_WIDTH)
    bg3 = bg.reshape(bs, tp, LANE)
    tabs = _rope_tables(past + jnp.arange(tp, dtype=jnp.int32), tp)
    if pooled is None:
        pooled = _pool_pages(cache4, page_flat, lw["a3"], bs=bs, n_pages=n_pages, page_base=layer * pool)
    else:
        pooled = pooled.reshape(bs, n_pages * (page // CMP_BLOCK), 2 * NSA_KV * LANE)
    kc, vc = _cmp_proj(pooled, lw["w4"])
    kvn, wn, qr, oc, ngo, gates, val = _sample_pre(za3, zb3, bg3, tabs, kc, vc, past=past)
    idx = _topk(val.reshape(bs * NSA_KV * tp, val.shape[-1]), n_top=n_top)
    idx_flat = idx.reshape(bs, NSA_KV, tp, LANE)[:, :, :ts, :n_top].reshape(-1)
    wbuf = cache_win.shape[2]
    cache_win_rows = cache_win.reshape(depth * bs, wbuf * 2 * NSA_KV, HEAD_DIM)
    yb = _sample_attn(idx_flat, page_flat, qr, oc, ngo, gates, kvn, cache_win_rows, wn, cache4,
                      ts=ts, n_top=n_top, n_pages=n_pages, page_base=layer * pool, win_base=layer * bs)
    ya, conv_new = _conv_mixer(za3, state_conv[layer], lw["w_conv"], t_real=ts)
    nm = cache_mem.shape[2]
    mem_rows = cache_mem.reshape(depth * bs, nm * 2 * MEM_HEADS, HEAD_DIM)
    ym = _mem_attn(zb3, mem_rows, mq_off=MQ_OFF, mg_off=MG_OFF, tq=tp, interleaved=True, kv_base=layer * bs)
    out = _out_proj(xs_p.reshape(m, d), ya.reshape(m, -1), yb.reshape(m, -1), ym.reshape(m, -1),
                    lw["w_out"], final_g, tm=m, final=final)
    kv_new = kvn[:, :ts].reshape(bs, ts, 4, NSA_KV, HEAD_DIM)
    win_rows = wn[:, :ts].reshape(bs, ts, 2, NSA_KV, HEAD_DIM)
    win_state = jnp.concatenate([cache_win[layer], win_rows], axis=1)[:, ts:]
    return out.reshape(bs, tp, d), kv_new, win_state, conv_new


def kernel(x_prompt, x_sample, cache_kv, cache_win, state_conv, cache_mem, page_table, mem_prompt,
           norm_g, w_in, w_conv, a_cmp, w_cmp, mem_norm_g, w_mem_kv, w_out, final_g):
    depth = w_in.shape[0]
    ts = x_sample.shape[1]
    xp = x_prompt
    xs = jnp.pad(x_sample, ((0, 0), (0, SAMPLE_T_PAD - ts), (0, 0)))
    pool_size, page = cache_kv.shape[1], cache_kv.shape[2]
    cache4 = cache_kv.reshape(depth * pool_size, page, 4 * NSA_KV, HEAD_DIM)
    page_flat = page_table.reshape(-1).astype(jnp.int32)
    kv_p, win_p, conv_p, mem_p, kv_s, win_s, conv_s = [], [], [], [], [], [], []
    for l in range(depth):
        lw = _layer_weights(norm_g[l], w_in, l, w_conv[l], a_cmp[l], w_cmp[l], w_out[l])
        final = l == depth - 1
        xp, kvn, winn, convn, mkv, pooled, proj_s = _prompt_layer(
            xp, xs.reshape(-1, xs.shape[-1]), mem_prompt, mem_norm_g[l], w_mem_kv[l].T.astype(BF16), lw, final_g, final,
            pool=(cache4, page_flat, lw["a3"], l * pool_size))
        kv_p.append(kvn)
        win_p.append(winn)
        conv_p.append(convn)
        mem_p.append(mkv)
        xs, kvn, winn, convn = _sample_layer(xs, proj_s, ts, l, cache4, page_flat, pooled, cache_win, state_conv,
                                             cache_mem, lw, final_g, final)
        kv_s.append(kvn)
        win_s.append(winn)
        conv_s.append(convn)
    return (xp, xs[:, :ts], jnp.stack(kv_p), jnp.stack(win_p), jnp.stack(conv_p), jnp.stack(mem_p),
            jnp.stack(kv_s), jnp.stack(win_s), jnp.stack(conv_s))
```

```python
import functools

import jax
import jax.numpy as jnp
from jax import lax
from jax.experimental import pallas as pl
from jax.experimental.pallas import tpu as pltpu

F32 = jnp.float32
BF16 = jnp.bfloat16

HEAD_DIM = 128
CONV_DIM = 512
CONV_W = 3
NSA_HEADS = 8
NSA_KV = 2
NSA_HPG = NSA_HEADS // NSA_KV
MEM_HEADS = 4
N_BRANCH = 3
ROPE_DIM = HEAD_DIM // 4
ROPE_HALF = ROPE_DIM // 2
ROPE_THETA = 500000.0
CMP_BLOCK = 64
SEL_BLOCK = 64
SEL_SHIFT = 6
TOP_N = 16
WINDOW = 512
NORM_EPS = 1e-6
MASK_NEG = -1e30
FORCE = 1e9
ATTN_SCALE = HEAD_DIM ** -0.5
SCALE_LOG2 = ATTN_SCALE * 1.4426950408889634

C_H, C_B, C_C, C_G, Q_OFF, NG_OFF = 0, 512, 1024, 1536, 2048, 3072
ZA_WIDTH = 4096
KV_OFF, MQ_OFF, MG_OFF = 0, 1536, 2048
ZB_WIDTH = 2560
BG_SRC = 4096
BG_N = NSA_HEADS * N_BRANCH
LANE = 128
SAMPLE_T_PAD = 8
VMEM_LIMIT = 56 * 1024 * 1024


def _nt(a, b):
    return lax.dot_general(a, b, (((1,), (1,)), ((), ())), preferred_element_type=F32)


def _nn(a, b):
    return jnp.dot(a, b, preferred_element_type=F32)


def _params(sem, vmem=VMEM_LIMIT):
    return pltpu.CompilerParams(dimension_semantics=sem, vmem_limit_bytes=vmem)


def _rope(x, c, s1, s2):
    return x * c + pltpu.roll(x, ROPE_HALF, 1) * s1 + pltpu.roll(x, LANE - ROPE_HALF, 1) * s2


def _silu(x):
    return x * jax.nn.sigmoid(x)


def _norm_matmul_kernel(x_ref, g_ref, w_ref, *rest, with_gate, tn):
    if with_gate:
        wbg_ref, z_ref, bg_ref, h_scr = rest
    else:
        z_ref, h_scr = rest
    x = x_ref[...]
    y = x * lax.rsqrt(jnp.mean(x * x, axis=-1, keepdims=True) + NORM_EPS) * g_ref[...]
    h_scr[...] = y.astype(BF16)
    if with_gate:
        bg_ref[...] = _nt(h_scr[...], wbg_ref[...])
    for j in range(w_ref.shape[0] // tn):
        z_ref[:, j * tn:(j + 1) * tn] = _nt(h_scr[...], w_ref[j * tn:(j + 1) * tn, :])


def _norm_matmul(x, g, w, wbg=None, *, tm, tn):
    m, d = x.shape
    n = w.shape[0]
    assert n % tn == 0 and m % tm == 0 and w.shape[1] == d
    with_gate = wbg is not None
    resident = lambda shape: pl.BlockSpec(shape, lambda i: (0, 0), pipeline_mode=pl.Buffered(1))
    in_specs = [pl.BlockSpec((tm, d), lambda i: (i, 0)), resident((1, d)), resident((n, d))]
    out_shape = [jax.ShapeDtypeStruct((m, n), F32)]
    out_specs = [pl.BlockSpec((tm, n), lambda i: (i, 0))]
    args = [x, g.reshape(1, d), w]
    if with_gate:
        nb = wbg.shape[0]
        in_specs.append(resident((nb, d)))
        out_shape.append(jax.ShapeDtypeStruct((m, nb), F32))
        out_specs.append(pl.BlockSpec((tm, nb), lambda i: (i, 0)))
        args.append(wbg)
    res = pl.pallas_call(
        functools.partial(_norm_matmul_kernel, with_gate=with_gate, tn=tn),
        out_shape=out_shape, grid=(m // tm,), in_specs=in_specs, out_specs=out_specs,
        scratch_shapes=[pltpu.VMEM((tm, d), BF16)],
        compiler_params=_params(("parallel",)),
        name="norm_matmul_gate" if with_gate else "norm_matmul",
    )(*args)
    return res if with_gate else res[0]


def _rms_rows(x_ref, g_ref):
    x = x_ref[...]
    return (x * lax.rsqrt(jnp.mean(x * x, axis=-1, keepdims=True) + NORM_EPS) * g_ref[...]).astype(BF16)


def _proj_conv_kernel(x_ref, g_ref, w_ref, wbg_ref, wc_ref, c_ref, s1_ref, s2_ref, xs_ref,
                      qc_ref, qr_ref, ng_ref, bg_ref, ya_ref, st_ref, zs_ref, bgs_ref, h_scr, up_scr,
                      *, blocks_per_seq):
    tm = x_ref.shape[0]
    cw = CONV_DIM
    i = pl.program_id(0)
    first = i % blocks_per_seq == 0
    h_scr[...] = _rms_rows(x_ref, g_ref)
    bg_ref[...] = _nt(h_scr[...], wbg_ref[...])
    chunk = lambda off: _nt(h_scr[...], w_ref[off:off + cw, :])

    @pl.when(i == 0)
    def _():
        up_scr[...] = jnp.zeros(up_scr.shape, F32)

    carry = up_scr[pl.ds(8 + tm - (CONV_W - 1), CONV_W - 1), :]
    up_scr[pl.ds(8 - (CONV_W - 1), CONV_W - 1), :] = jnp.where(first, 0.0, carry)
    u = chunk(C_C) * chunk(C_H)
    up_scr[pl.ds(8, tm), :] = u
    wc = wc_ref[...]
    y = wc[0:1, :] * up_scr[pl.ds(6, tm), :]
    y = y + wc[1:2, :] * up_scr[pl.ds(7, tm), :]
    y = y + wc[2:3, :] * u
    y = chunk(C_B) * y
    ya_ref[...] = (_silu(chunk(C_G)) * y).astype(ya_ref.dtype)

    st_ref[0] = up_scr[pl.ds(8 + tm - (CONV_W - 1), CONV_W - 1), :]

    c, s1, s2 = c_ref[...], s1_ref[...], s2_ref[...]
    for j in range((NG_OFF - Q_OFF) // cw):
        qv = chunk(Q_OFF + j * cw)
        for k in range(cw // LANE):
            lo = j * cw + k * LANE
            qh = qv[:, k * LANE:(k + 1) * LANE]
            qc_ref[:, lo:lo + LANE] = (qh * SCALE_LOG2).astype(BF16)
            qr_ref[:, lo:lo + LANE] = (_rope(qh, c, s1, s2) * SCALE_LOG2).astype(BF16)
    for j in range((ZA_WIDTH - NG_OFF) // cw):
        ng_ref[:, j * cw:(j + 1) * cw] = _silu(chunk(NG_OFF + j * cw))

    @pl.when(i == pl.num_programs(0) - 1)
    def _():
        hs = _rms_rows(xs_ref, g_ref)
        bgs_ref[...] = _nt(hs, wbg_ref[...])
        for j in range(ZA_WIDTH // cw):
            zs_ref[:, j * cw:(j + 1) * cw] = _nt(hs, w_ref[j * cw:(j + 1) * cw, :])


def _proj_conv(x2, g, w_a, wbg, w_conv, tabs, xs2, *, tm, seq_len):
    m, d = x2.shape
    ms = xs2.shape[0]
    assert seq_len % tm == 0 and CONV_W == 3
    blocks_per_seq = seq_len // tm
    nq, nng = NG_OFF - Q_OFF, ZA_WIDTH - NG_OFF
    resident = lambda shape: pl.BlockSpec(shape, lambda i: (0,) * len(shape), pipeline_mode=pl.Buffered(1))
    row = lambda width: pl.BlockSpec((tm, width), lambda i: (i, 0))
    tspec = pl.BlockSpec((tm, LANE), lambda i: (i % blocks_per_seq, 0))
    return pl.pallas_call(
        functools.partial(_proj_conv_kernel, blocks_per_seq=blocks_per_seq),
        out_shape=[jax.ShapeDtypeStruct((m, nq), BF16), jax.ShapeDtypeStruct((m, nq), BF16),
                   jax.ShapeDtypeStruct((m, nng), F32), jax.ShapeDtypeStruct((m, LANE), F32),
                   jax.ShapeDtypeStruct((m, CONV_DIM), BF16),
                   jax.ShapeDtypeStruct((m // seq_len, CONV_W - 1, CONV_DIM), F32),
                   jax.ShapeDtypeStruct((ms, ZA_WIDTH), F32), jax.ShapeDtypeStruct((ms, LANE), F32)],
        grid=(m // tm,),
        in_specs=[row(d), resident((1, d)), resident(w_a.shape), resident(wbg.shape), resident(w_conv.shape),
                  tspec, tspec, tspec, resident((ms, d))],
        out_specs=[row(nq), row(nq), row(nng), row(LANE), row(CONV_DIM),
                   pl.BlockSpec((1, CONV_W - 1, CONV_DIM), lambda i: (i // blocks_per_seq, 0, 0)),
                   pl.BlockSpec((ms, ZA_WIDTH), lambda i: (0, 0)), pl.BlockSpec((ms, LANE), lambda i: (0, 0))],
        scratch_shapes=[pltpu.VMEM((tm, d), BF16), pltpu.VMEM((tm + 8, CONV_DIM), F32)],
        compiler_params=_params(("arbitrary",)),
        name="proj_conv",
    )(x2, g.reshape(1, d), w_a, wbg, w_conv, *tabs, xs2)


def _proj_kv_kernel(x_ref, g_ref, w_ref, c_ref, s1_ref, s2_ref, a_ref, xs_ref,
                    kvn_ref, kvb_ref, win_ref, pool_ref, zm_ref, zs_ref, h_scr, *, blocks_per_seq):
    tm = x_ref.shape[0]
    cw = 2 * NSA_KV * LANE
    h_scr[...] = _rms_rows(x_ref, g_ref)
    chunk = lambda off: _nt(h_scr[...], w_ref[off:off + cw, :])
    c, s1, s2 = c_ref[...], s1_ref[...], s2_ref[...]
    kv0 = chunk(KV_OFF)
    kv1 = chunk(KV_OFF + cw)
    kv2 = chunk(KV_OFF + 2 * cw)
    n_kv, n_w, half = 4 * NSA_KV, 2 * NSA_KV, NSA_KV * LANE
    for g in range(NSA_KV):
        lo, hi = g * LANE, (g + 1) * LANE
        ks = _rope(kv1[:, lo:hi], c, s1, s2)
        kw = _rope(kv2[:, lo:hi], c, s1, s2)
        vs = kv1[:, half + lo:half + hi]
        vw = kv2[:, half + lo:half + hi]
        kvn_ref[pl.ds(g, tm, stride=n_kv), :] = kv0[:, lo:hi]
        kvn_ref[pl.ds(NSA_KV + g, tm, stride=n_kv), :] = kv0[:, half + lo:half + hi]
        kvn_ref[pl.ds(2 * NSA_KV + g, tm, stride=n_kv), :] = ks
        kvn_ref[pl.ds(3 * NSA_KV + g, tm, stride=n_kv), :] = vs
        kvb_ref[:, lo:hi] = ks.astype(BF16)
        kvb_ref[:, half + lo:half + hi] = vs.astype(BF16)
        kvb_ref[:, 2 * half + lo:2 * half + hi] = kw.astype(BF16)
        kvb_ref[:, 3 * half + lo:3 * half + hi] = vw.astype(BF16)
        win_ref[pl.ds(g, tm, stride=n_w), :] = kw
        win_ref[pl.ds(NSA_KV + g, tm, stride=n_w), :] = vw

    pool_ref[...] = jnp.sum(kv0.reshape(tm // CMP_BLOCK, CMP_BLOCK, cw) * a_ref[...][None], axis=1)
    for j in range(zm_ref.shape[1] // cw):
        zm_ref[:, j * cw:(j + 1) * cw] = chunk(MQ_OFF + j * cw)

    @pl.when(pl.program_id(0) == pl.num_programs(0) - 1)
    def _():
        hs = _rms_rows(xs_ref, g_ref)
        for j in range(ZB_WIDTH // cw):
            zs_ref[:, j * cw:(j + 1) * cw] = _nt(hs, w_ref[j * cw:(j + 1) * cw, :])


def _proj_kv(x2, g, w_b, tabs, a4, xs2, *, tm, seq_len):
    m, d = x2.shape
    ms = xs2.shape[0]
    assert seq_len % tm == 0
    blocks_per_seq = seq_len // tm
    n_kv, n_w = 4 * NSA_KV, 2 * NSA_KV
    nm = ZB_WIDTH - MQ_OFF
    resident = lambda shape: pl.BlockSpec(shape, lambda i: (0,) * len(shape), pipeline_mode=pl.Buffered(1))
    row = lambda width: pl.BlockSpec((tm, width), lambda i: (i, 0))
    tspec = pl.BlockSpec((tm, LANE), lambda i: (i % blocks_per_seq, 0))
    return pl.pallas_call(
        functools.partial(_proj_kv_kernel, blocks_per_seq=blocks_per_seq),
        out_shape=[jax.ShapeDtypeStruct((m * n_kv, LANE), F32),
                   jax.ShapeDtypeStruct((m, n_kv * LANE), BF16),
                   jax.ShapeDtypeStruct((m // seq_len * tm * n_w, LANE), F32),
                   jax.ShapeDtypeStruct((m // CMP_BLOCK, 2 * NSA_KV * LANE), F32),
                   jax.ShapeDtypeStruct((m, nm), F32), jax.ShapeDtypeStruct((ms, ZB_WIDTH), F32)],
        grid=(m // tm,),
        in_specs=[row(d), resident((1, d)), resident(w_b.shape), tspec, tspec, tspec, resident(a4.shape),
                  resident((ms, d))],
        out_specs=[pl.BlockSpec((tm * n_kv, LANE), lambda i: (i, 0)), row(n_kv * LANE),
                   pl.BlockSpec((tm * n_w, LANE), lambda i: (i // blocks_per_seq, 0)),
                   pl.BlockSpec((tm // CMP_BLOCK, 2 * NSA_KV * LANE), lambda i: (i, 0)), row(nm),
                   pl.BlockSpec((ms, ZB_WIDTH), lambda i: (0, 0))],
        scratch_shapes=[pltpu.VMEM((tm, d), BF16)],
        compiler_params=_params(("arbitrary",)),
        name="proj_kv",
    )(x2, g.reshape(1, d), w_b, *tabs, a4, xs2)


def _cast_rows_kernel(w_ref, o_ref):
    o_ref[...] = w_ref[0].astype(BF16)


def _pack_gate_kernel(w_ref, o_ref):
    o_ref[...] = jnp.zeros(o_ref.shape, BF16)
    o_ref[0:w_ref.shape[1], :] = w_ref[0].astype(BF16)


def _pack_in_proj_weights(w_t3, layer):
    _, n_in, d = w_t3.shape
    tn = 512
    assert n_in == BG_SRC + BG_N + ZB_WIDTH and BG_SRC % tn == 0 and ZB_WIDTH % tn == 0 and BG_N % 8 == 0

    def cast_rows(first_row, n_rows, name):
        return pl.pallas_call(
            _cast_rows_kernel,
            out_shape=jax.ShapeDtypeStruct((n_rows, d), BF16),
            grid=(n_rows // tn,),
            in_specs=[pl.BlockSpec((pl.Element(1), pl.Element(tn), pl.Element(d)),
                                   lambda j: (layer, pl.multiple_of(first_row + j * tn, 8), 0))],
            out_specs=pl.BlockSpec((tn, d), lambda j: (j, 0)),
            compiler_params=_params(("parallel",)),
            name=name,
        )(w_t3)

    wa = cast_rows(0, ZA_WIDTH, "pack_w_a")
    wb = cast_rows(BG_SRC + BG_N, ZB_WIDTH, "pack_w_b")
    wbg = pl.pallas_call(
        _pack_gate_kernel,
        out_shape=jax.ShapeDtypeStruct((LANE, d), BF16),
        grid=(1,),
        in_specs=[pl.BlockSpec((pl.Element(1), pl.Element(BG_N), pl.Element(d)), lambda j: (layer, BG_SRC, 0))],
        out_specs=pl.BlockSpec((LANE, d), lambda j: (0, 0)),
        compiler_params=_params(("arbitrary",)),
        name="pack_w_gate",
    )(w_t3)
    return wa, wb, wbg


def _pool_pages_kernel(pt_sm, a_ref, cache_ref, o_ref, buf, sem, *, pages_per_step, page_base):
    step = pl.program_id(0)
    n_steps = pl.num_programs(0)
    slot = step % 2
    n_cols = buf.shape[3]

    def page_copy(step_idx, p, to_slot):
        page = pt_sm[step_idx * pages_per_step + p] + page_base
        return pltpu.make_async_copy(cache_ref.at[page, :, pl.ds(0, n_cols), :], buf.at[to_slot, p], sem.at[to_slot])

    @pl.when(step == 0)
    def _():
        for p in range(pages_per_step):
            page_copy(0, p, 0).start()

    @pl.when(step + 1 < n_steps)
    def _():
        for p in range(pages_per_step):
            page_copy(step + 1, p, 1 - slot).start()

    for p in range(pages_per_step):
        page_copy(step, p, slot).wait()

    a = a_ref[...]
    per = buf.shape[2] // CMP_BLOCK
    for p in range(pages_per_step):
        for k in range(per):
            x = buf[slot, p, pl.ds(k * CMP_BLOCK, CMP_BLOCK)]
            o_ref[0, p, k] = jnp.sum(x * a, axis=0)


def _pool_pages(cache4, page_flat, a3, *, bs, n_pages, page_base, pages_per_step=16):
    page = cache4.shape[1]
    per = page // CMP_BLOCK
    n_cols = 2 * NSA_KV
    total = bs * n_pages
    pages_per_step = min(pages_per_step, total)
    assert total % pages_per_step == 0
    n_steps = total // pages_per_step
    grid_spec = pltpu.PrefetchScalarGridSpec(
        num_scalar_prefetch=1, grid=(n_steps,),
        in_specs=[pl.BlockSpec((CMP_BLOCK, n_cols, LANE), lambda si, pt: (0, 0, 0)),
                  pl.BlockSpec(memory_space=pl.ANY)],
        out_specs=pl.BlockSpec((1, pages_per_step, per, n_cols, LANE), lambda si, pt: (si, 0, 0, 0, 0)),
        scratch_shapes=[pltpu.VMEM((2, pages_per_step, page, n_cols, LANE), F32),
                        pltpu.SemaphoreType.DMA((2,))])
    out = pl.pallas_call(
        functools.partial(_pool_pages_kernel, pages_per_step=pages_per_step, page_base=page_base),
        out_shape=jax.ShapeDtypeStruct((n_steps, pages_per_step, per, n_cols, LANE), F32),
        grid_spec=grid_spec,
        compiler_params=_params(("arbitrary",)),
        name="pool_pages",
    )(page_flat, a3, cache4)
    return out.reshape(bs, n_pages * per, n_cols * LANE)


def _cmp_proj_kernel(p_ref, w_ref, kc_ref, vc_ref):
    pooled = p_ref[0]
    n = pooled.shape[0]
    n_pad = kc_ref.shape[2]
    for c in range(4):
        r = _nn(pooled[:, c * LANE:(c + 1) * LANE].astype(BF16), w_ref[c]).astype(BF16)
        dst = kc_ref if c < 2 else vc_ref
        if n_pad > n:
            dst[0, c % 2] = jnp.zeros((n_pad, LANE), BF16)
        dst[0, c % 2, 0:n, :] = r


def _cmp_proj(pooled, w4):
    b, n, _ = pooled.shape
    n_pad = -(-n // LANE) * LANE
    spec = pl.BlockSpec((1, NSA_KV, n_pad, LANE), lambda bi: (bi, 0, 0, 0))
    return pl.pallas_call(
        _cmp_proj_kernel,
        out_shape=[jax.ShapeDtypeStruct((b, NSA_KV, n_pad, LANE), BF16)] * 2,
        grid=(b,),
        in_specs=[pl.BlockSpec((1, n, 512), lambda bi: (bi, 0, 0)),
                  pl.BlockSpec((4, LANE, LANE), lambda bi: (0, 0, 0))],
        out_specs=[spec, spec],
        compiler_params=_params(("parallel",)),
        name="cmp_proj",
    )(pooled, w4)


def _lane_parts(x):
    return [x[:, j * LANE:(j + 1) * LANE] for j in range(x.shape[1] // LANE)]


def _nsa_prompt_kernel(*refs, t_len, tq, tc, tw, n_sel, top, sub, pool_pages, page_base):
    if pool_pages:
        (pt_sm, qc_ref, qr_ref, ng_ref, bg_ref, ksel_ref, vsel_ref, kwin_ref, vwin_ref, kc_ref, vc_ref,
         pa_ref, cache_ref, o_ref, pool_ref,
         qc_scr, qr_scr, s_scr, p_scr, a_scr, m_scr, l_scr, acc_scr, oc_scr, b_scr,
         sw_scr, pw_scr, bw_scr, ow_scr, wl_scr, pbuf, psem) = refs
        step = (pl.program_id(0) * pl.num_programs(1) + pl.program_id(1)) * pl.num_programs(2) + pl.program_id(2)
        n_steps = pl.num_programs(0) * pl.num_programs(1) * pl.num_programs(2)
        slot = step % 2
        half_rows, n_cols = pbuf.shape[2], 2 * NSA_KV

        def page_copies(step_idx, p, to_slot):
            page = pt_sm[step_idx * pool_pages + p] + page_base
            return [pltpu.make_async_copy(cache_ref.at[page, pl.ds(hh * half_rows, half_rows), pl.ds(0, n_cols), :],
                                          pbuf.at[to_slot, p, :, pl.ds(hh * n_cols, n_cols), :], psem.at[to_slot])
                    for hh in range(2)]

        @pl.when(step == 0)
        def _():
            for p in range(pool_pages):
                for cp in page_copies(0, p, 0):
                    cp.start()

        @pl.when(step + 1 < n_steps)
        def _():
            for p in range(pool_pages):
                for cp in page_copies(step + 1, p, 1 - slot):
                    cp.start()

        for p in range(pool_pages):
            for cp in page_copies(step, p, slot):
                cp.wait()
        pa = pa_ref[...]
        for p in range(pool_pages):
            pool_ref[0, p] = jnp.sum(pbuf[slot, p] * pa, axis=0)
    else:
        (qc_ref, qr_ref, ng_ref, bg_ref, ksel_ref, vsel_ref, kwin_ref, vwin_ref, kc_ref, vc_ref,
         o_ref, qc_scr, qr_scr, s_scr, p_scr, a_scr, m_scr, l_scr, acc_scr, oc_scr, b_scr,
         sw_scr, pw_scr, bw_scr, ow_scr, wl_scr) = refs
    i = pl.program_id(2)
    hq = NSA_HPG
    for h in range(hq):
        qc_scr[pl.ds(h * tq, tq), :] = qc_ref[0, :, h * LANE:(h + 1) * LANE]
        qr_scr[pl.ds(h * tq, tq), 0:LANE] = qr_ref[0, :, h * LANE:(h + 1) * LANE]

    w0 = pl.multiple_of(jnp.clip(i * tq + tq - tw, 0, t_len - tw), LANE)
    kp = w0 + lax.broadcasted_iota(jnp.int32, (tq, tw), 1)
    tp = i * tq + lax.broadcasted_iota(jnp.int32, (tq, tw), 0)
    bw_scr[...] = jnp.where((kp <= tp) & (kp > tp - WINDOW), 0.0, MASK_NEG)
    sw_scr[...] = _nt(qr_scr[:, 0:LANE], kwin_ref[0, pl.ds(w0, tw), :])
    for r0 in range(0, hq * tq, sub):
        r = pl.ds(r0, sub)
        parts = _lane_parts(sw_scr[r, :] + bw_scr[pl.ds(r0 % tq, sub), :])
        m = jnp.max(functools.reduce(jnp.maximum, parts), axis=-1, keepdims=True)
        ews = [jnp.exp2(x - m) for x in parts]
        row_sum = jnp.sum(functools.reduce(jnp.add, ews), axis=-1, keepdims=True)
        wl_scr[r, :] = jnp.broadcast_to(1.0 / row_sum, (sub, LANE))
        pw_scr[r, :] = jnp.concatenate(ews, axis=1).astype(BF16)
    ow_scr[...] = _nn(pw_scr[...], vwin_ref[0, pl.ds(w0, tw), :])

    kc = kc_ref[0, 0]
    npad = kc.shape[0]
    s_scr[:, 0:npad] = _nt(qc_scr[...], kc)
    tpos = i * tq + lax.broadcasted_iota(jnp.int32, (tq, npad), 0)
    ncol = lax.broadcasted_iota(jnp.int32, (tq, npad), 1)
    cmask = (ncol + 1) * CMP_BLOCK <= tpos + 1
    imp = jnp.zeros((tq, npad), F32)
    for h in range(hq):
        r = pl.ds(h * tq, tq)
        s = jnp.where(cmask, s_scr[r, 0:npad], MASK_NEG)
        e = jnp.where(cmask, jnp.exp2(s - jnp.max(s, axis=-1, keepdims=True)), 0.0)
        p = e / jnp.maximum(jnp.sum(e, axis=-1, keepdims=True), 1e-30)
        p_scr[r, 0:npad] = p.astype(BF16)
        imp = imp + p
    oc_scr[...] = _nn(p_scr[:, 0:npad], vc_ref[0, 0])

    rows = min(npad, -(-n_sel // 8) * 8)
    imp_t = imp.T[0:rows]
    blk = lax.broadcasted_iota(jnp.int32, (rows, tq), 0)
    cur = lax.shift_right_logical(i * tq + lax.broadcasted_iota(jnp.int32, (rows, tq), 1), SEL_SHIFT)
    imp_t = jnp.where((blk == 0) | (blk == cur) | (blk == cur - 1), FORCE, imp_t)
    imp_t = jnp.where(blk > cur, -1.0, imp_t)
    imp_t = jnp.where(blk >= n_sel, -2.0, imp_t)
    rank = jnp.zeros((rows, tq), F32)
    for j in range(n_sel):
        a = imp_t[j:j + 1, :]
        ahead = (a > imp_t) | ((a == imp_t) & (blk > j))
        rank = rank + jnp.where(ahead, 1.0, 0.0)
    neg_t = jnp.where((rank < top) & (blk < n_sel) & (blk <= cur), 0.0, MASK_NEG)
    if npad > rows:
        neg_t = jnp.concatenate([neg_t, jnp.zeros((npad - rows, tq), F32)], axis=0)
    neg = neg_t.T.astype(BF16)
    for h in range(hq):
        qr_scr[pl.ds(h * tq, tq), LANE:2 * LANE] = neg

    m_scr[...] = jnp.full(m_scr.shape, MASK_NEG, F32)
    l_scr[...] = jnp.zeros(l_scr.shape, F32)
    acc_scr[...] = jnp.zeros(acc_scr.shape, F32)
    n_chunks = (i * tq + tq + tc - 1) // tc

    def chunk(ci, diagonal):
        k0 = pl.multiple_of(ci * tc, tc)
        v = vsel_ref[0, pl.ds(k0, tc), :]
        kb = lax.shift_right_logical(k0 + lax.broadcasted_iota(jnp.int32, (tc, npad), 0), SEL_SHIFT)
        jb = lax.broadcasted_iota(jnp.int32, (tc, npad), 1)
        k_aug = jnp.concatenate([ksel_ref[0, pl.ds(k0, tc), :], jnp.where(kb == jb, 1.0, 0.0).astype(BF16)], axis=1)
        s_scr[:, 0:tc] = _nt(qr_scr[...], k_aug)
        if diagonal:
            kp = k0 + lax.broadcasted_iota(jnp.int32, (tq, tc), 1)
            tp = i * tq + lax.broadcasted_iota(jnp.int32, (tq, tc), 0)
            b_scr[:, 0:tc] = jnp.where(kp <= tp, 0.0, MASK_NEG)
        for r0 in range(0, hq * tq, sub):
            r = pl.ds(r0, sub)
            sc = s_scr[r, 0:tc]
            if diagonal:
                sc = sc + b_scr[pl.ds(r0 % tq, sub), 0:tc]
            parts = _lane_parts(sc)
            m_prev = m_scr[r, :]
            m_new = jnp.maximum(m_prev, jnp.max(functools.reduce(jnp.maximum, parts), axis=-1, keepdims=True))
            alpha = jnp.exp2(m_prev - m_new)
            pes = [jnp.exp2(x - m_new) for x in parts]
            l_scr[r, :] = alpha * l_scr[r, :] + functools.reduce(jnp.add, pes)
            p_scr[r, 0:tc] = jnp.concatenate(pes, axis=1).astype(BF16)
            a_scr[r, :] = alpha
            m_scr[r, :] = m_new
        acc_scr[...] = a_scr[...] * acc_scr[...] + _nn(p_scr[:, 0:tc], v)

    def full_chunk(ci, carry):
        chunk(ci, False)
        return carry

    lax.fori_loop(0, n_chunks - 1, full_chunk, 0)
    chunk(n_chunks - 1, True)

    gate = jax.nn.sigmoid(bg_ref[0])
    for g in range(1, NSA_KV):
        gate = jnp.where(pl.program_id(1) == g, pltpu.roll(gate, LANE - g * hq * N_BRANCH, 1), gate)
    ng = ng_ref[0]
    for h in range(hq):
        r = slice(h * tq, (h + 1) * tq)
        o_s = acc_scr[r, :] * (1.0 / jnp.sum(l_scr[r, :], axis=-1, keepdims=True))
        o = (gate[:, 3 * h:3 * h + 1] * oc_scr[r, :] + gate[:, 3 * h + 1:3 * h + 2] * o_s
             + gate[:, 3 * h + 2:3 * h + 3] * (ow_scr[r, :] * wl_scr[r, :]))
        o_ref[0, :, h * LANE:(h + 1) * LANE] = (ng[:, h * LANE:(h + 1) * LANE] * o).astype(o_ref.dtype)


def _nsa_prompt(qc3, qr3, ng3, bg3, kvb, kc, vc, *, tq=128, tc=512, sub=64, pool=None):
    b, t, _ = qc3.shape
    nq = t // tq
    tc = min(tc, t)
    tw = min(WINDOW + tq, t)
    n_sel = t // SEL_BLOCK
    top = min(TOP_N, n_sel)
    npad = kc.shape[2]
    gw = NSA_HPG * LANE
    rows = NSA_HPG * tq
    wide = max(tc, npad)
    kvspec = lambda k: pl.BlockSpec((1, t, LANE), lambda bi, gi, qi, *_, k=k: (bi, 0, k + gi))
    cspec = pl.BlockSpec((1, 1, npad, LANE), lambda bi, gi, qi, *_: (bi, gi, 0, 0))
    gspec = pl.BlockSpec((1, tq, gw), lambda bi, gi, qi, *_: (bi, qi, gi))
    in_specs = [gspec, gspec, gspec,
                pl.BlockSpec((1, tq, LANE), lambda bi, gi, qi, *_: (bi, qi, 0)),
                kvspec(0), kvspec(2), kvspec(4), kvspec(6), cspec, cspec]
    out_shape = [jax.ShapeDtypeStruct((b, t, NSA_HEADS * LANE), BF16)]
    out_specs = [pl.BlockSpec((1, tq, gw), lambda bi, gi, qi, *_: (bi, qi, gi))]
    assert tc % tq == 0 and npad == LANE
    scratch = ([pltpu.VMEM((rows, LANE), BF16), pltpu.VMEM((rows, 2 * LANE), BF16)]
               + [pltpu.VMEM((rows, wide), F32), pltpu.VMEM((rows, wide), BF16)]
               + [pltpu.VMEM((rows, LANE), F32)] * 5 + [pltpu.VMEM((tq, wide), F32)]
               + [pltpu.VMEM((rows, tw), F32), pltpu.VMEM((rows, tw), BF16), pltpu.VMEM((tq, tw), F32)]
               + [pltpu.VMEM((rows, LANE), F32)] * 2)
    args = [qc3, qr3, ng3, bg3, kvb, kvb, kvb, kvb, kc, vc]
    n_steps = b * NSA_KV * nq
    pool_pages, page_base, prefetch = 0, 0, []
    if pool is not None:
        cache4, page_flat, a3, page_base = pool
        page, n_cols = cache4.shape[1], 2 * NSA_KV
        if page_flat.shape[0] % n_steps == 0 and page == 2 * CMP_BLOCK and 2 * n_cols == 8:
            pool_pages = page_flat.shape[0] // n_steps
            prefetch = [page_flat]
            in_specs += [pl.BlockSpec((CMP_BLOCK, 2 * n_cols, LANE), lambda bi, gi, qi, *_: (0, 0, 0)),
                         pl.BlockSpec(memory_space=pl.ANY)]
            args += [jnp.concatenate([a3, a3], axis=1), cache4]
            out_shape.append(jax.ShapeDtypeStruct((n_steps, pool_pages, 2 * n_cols, LANE), F32))
            out_specs.append(pl.BlockSpec((1, pool_pages, 2 * n_cols, LANE),
                                          lambda bi, gi, qi, *_: ((bi * NSA_KV + gi) * nq + qi, 0, 0, 0)))
            scratch += [pltpu.VMEM((2, pool_pages, CMP_BLOCK, 2 * n_cols, LANE), F32), pltpu.SemaphoreType.DMA((2,))]
    kern = functools.partial(_nsa_prompt_kernel, t_len=t, tq=tq, tc=tc, tw=tw, n_sel=n_sel, top=top,
                             sub=min(sub, tq), pool_pages=pool_pages, page_base=page_base)
    res = pl.pallas_call(
        kern,
        out_shape=out_shape,
        grid_spec=pltpu.PrefetchScalarGridSpec(
            num_scalar_prefetch=len(prefetch), grid=(b, NSA_KV, nq),
            in_specs=in_specs, out_specs=out_specs, scratch_shapes=scratch),
        compiler_params=_params(("arbitrary", "arbitrary", "arbitrary")),
        name="nsa_prompt",
    )(*prefetch, *args)
    return (res[0], res[1]) if pool_pages else (res[0], None)


def _conv_kernel(h_ref, b_ref, c_ref, g_ref, prev_ref, w_ref, y_ref, st_ref, up_scr, *, t_real):
    u = c_ref[...] * h_ref[...]
    t = u.shape[1]
    up_scr[:, pl.ds(8 - (CONV_W - 1), CONV_W - 1), :] = prev_ref[...]
    up_scr[:, pl.ds(8, t), :] = u
    w = w_ref[...]
    y = w[0:1, :][None] * up_scr[:, pl.ds(6, t), :]
    y = y + w[1:2, :][None] * up_scr[:, pl.ds(7, t), :]
    y = y + w[2:3, :][None] * u
    y = b_ref[...] * y
    y_ref[...] = (_silu(g_ref[...]) * y).astype(y_ref.dtype)
    st_ref[...] = up_scr[:, pl.ds(6 + t_real, CONV_W - 1), :]


def _conv_mixer(z3, prev, w_conv, *, t_real):
    b, t, _ = z3.shape
    nc = CONV_DIM // LANE
    zspec = lambda off: pl.BlockSpec((b, t, LANE), lambda ci, off=off: (0, 0, off // LANE + ci))
    return pl.pallas_call(
        functools.partial(_conv_kernel, t_real=t_real),
        out_shape=[jax.ShapeDtypeStruct((b, t, CONV_DIM), BF16),
                   jax.ShapeDtypeStruct((b, CONV_W - 1, CONV_DIM), F32)],
        grid=(nc,),
        in_specs=[zspec(C_H), zspec(C_B), zspec(C_C), zspec(C_G),
                  pl.BlockSpec((b, CONV_W - 1, LANE), lambda ci: (0, 0, ci)),
                  pl.BlockSpec((CONV_W, LANE), lambda ci: (0, ci))],
        out_specs=[pl.BlockSpec((b, t, LANE), lambda ci: (0, 0, ci)),
                   pl.BlockSpec((b, CONV_W - 1, LANE), lambda ci: (0, 0, ci))],
        scratch_shapes=[pltpu.VMEM((b, t + 8, LANE), F32)],
        compiler_params=_params(("parallel",)),
        name="conv_mixer",
    )(z3, z3, z3, z3, prev, w_conv)


def _mem_attn_kernel(q_ref, mg_ref, kv_ref, o_ref, *, interleaved):
    q = q_ref[0]
    mg = mg_ref[0]
    half = MEM_HEADS * LANE
    for h in range(MEM_HEADS):
        lo, hi = h * LANE, (h + 1) * LANE
        if interleaved:
            nm = kv_ref.shape[1] // (2 * MEM_HEADS)
            k = kv_ref[0, pl.ds(h, nm, stride=2 * MEM_HEADS), :].astype(BF16)
            v = kv_ref[0, pl.ds(MEM_HEADS + h, nm, stride=2 * MEM_HEADS), :].astype(BF16)
        else:
            k = kv_ref[0, :, lo:hi].astype(BF16)
            v = kv_ref[0, :, half + lo:half + hi].astype(BF16)
        s = _nt((q[:, lo:hi] * ATTN_SCALE).astype(BF16), k)
        e = jnp.exp(s - jnp.max(s, axis=-1, keepdims=True))
        o = _nn(e.astype(BF16), v) / jnp.sum(e, axis=-1, keepdims=True)
        o_ref[0, :, lo:hi] = (_silu(mg[:, lo:hi]) * o).astype(o_ref.dtype)


def _mem_attn(zb3, mkv, *, mq_off, mg_off, tq, interleaved, kv_base=0):
    b, t, _ = zb3.shape
    wq = MEM_HEADS * LANE
    return pl.pallas_call(
        functools.partial(_mem_attn_kernel, interleaved=interleaved),
        out_shape=jax.ShapeDtypeStruct((b, t, wq), BF16),
        grid=(b, t // tq),
        in_specs=[pl.BlockSpec((1, tq, wq), lambda bi, ti: (bi, ti, mq_off // wq)),
                  pl.BlockSpec((1, tq, wq), lambda bi, ti: (bi, ti, mg_off // wq)),
                  pl.BlockSpec((1,) + mkv.shape[1:], lambda bi, ti: (bi + kv_base, 0, 0))],
        out_specs=pl.BlockSpec((1, tq, wq), lambda bi, ti: (bi, ti, 0)),
        compiler_params=_params(("parallel", "parallel")),
        name="mem_attn",
    )(zb3, zb3, mkv)


def _out_proj_kernel(x_ref, ya_ref, yb_ref, ym_ref, w_ref, fg_ref, o_ref, *, final):
    a, bw = CONV_DIM, CONV_DIM + NSA_HEADS * LANE
    acc = _nn(ya_ref[...], w_ref[0:a, :])
    acc = acc + _nn(yb_ref[...], w_ref[a:bw, :])
    acc = acc + _nn(ym_ref[...], w_ref[bw:, :])
    r = x_ref[...] + acc
    if final:
        r = r * lax.rsqrt(jnp.mean(r * r, axis=-1, keepdims=True) + NORM_EPS) * fg_ref[...]
    o_ref[...] = r


def _out_proj(x, ya, yb, ym, w, fg, *, tm, final):
    m, d = x.shape
    row = lambda width: pl.BlockSpec((tm, width), lambda i: (i, 0))
    return pl.pallas_call(
        functools.partial(_out_proj_kernel, final=final),
        out_shape=jax.ShapeDtypeStruct((m, d), F32),
        grid=(m // tm,),
        in_specs=[row(d), row(ya.shape[1]), row(yb.shape[1]), row(ym.shape[1]),
                  pl.BlockSpec(w.shape, lambda i: (0, 0), pipeline_mode=pl.Buffered(1)),
                  pl.BlockSpec((1, d), lambda i: (0, 0), pipeline_mode=pl.Buffered(1))],
        out_specs=row(d),
        compiler_params=_params(("parallel",)),
        name="out_proj",
    )(x, ya, yb, ym, w, fg.reshape(1, d))


def _sample_pre_kernel(q_ref, ng_ref, bg_ref, kv0_ref, kv1_ref, kv2_ref, c_ref, s1_ref, s2_ref, kc_ref, vc_ref,
                       kvn_ref, wn_ref, qr_ref, oc_ref, ngo_ref, gate_ref, val_ref, *, past):
    tp = SAMPLE_T_PAD
    hq = NSA_HPG
    c, s1, s2 = c_ref[...], s1_ref[...], s2_ref[...]
    kv0, kv1, kv2 = kv0_ref[0], kv1_ref[0], kv2_ref[0]
    kvn_ref[0, :, 0:512] = kv0
    kvn_ref[0, :, 768:1024] = kv1[:, 256:512]
    wn_ref[0, :, 256:512] = kv2[:, 256:512]
    for g in range(NSA_KV):
        lo, hi = g * LANE, (g + 1) * LANE
        kvn_ref[0, :, 512 + lo:512 + hi] = _rope(kv1[:, lo:hi], c, s1, s2)
        wn_ref[0, :, lo:hi] = _rope(kv2[:, lo:hi], c, s1, s2)

    q = q_ref[0]
    ng = ng_ref[0]
    gates = jax.nn.sigmoid(bg_ref[0])
    for g in range(NSA_KV):
        qc_l, qr_l = [], []
        for h in range(hq):
            lo = (g * hq + h) * LANE
            qh = q[:, lo:lo + LANE]
            qc_l.append(qh * ATTN_SCALE)
            qr_l.append(_rope(qh, c, s1, s2) * ATTN_SCALE)
            ngo_ref[0, g, h * tp:(h + 1) * tp, :] = ng[:, lo:lo + LANE]
            for br in range(N_BRANCH):
                col = (g * hq + h) * N_BRANCH + br
                gate_ref[0, g, br, h * tp:(h + 1) * tp, :] = jnp.broadcast_to(gates[:, col:col + 1], (tp, LANE))
        qc = jnp.concatenate(qc_l, axis=0)
        qr_ref[0, g] = jnp.concatenate(qr_l, axis=0)

        kc = kc_ref[0, g]
        npad = kc.shape[0]
        s = _nt(qc.astype(BF16), kc)
        trow = lax.broadcasted_iota(jnp.int32, (hq * tp, npad), 0) % tp
        ncol = lax.broadcasted_iota(jnp.int32, (hq * tp, npad), 1)
        cmask = (ncol + 1) * CMP_BLOCK <= past + trow + 1
        s = jnp.where(cmask, s, MASK_NEG)
        e = jnp.where(cmask, jnp.exp(s - jnp.max(s, axis=-1, keepdims=True)), 0.0)
        p = e / jnp.maximum(jnp.sum(e, axis=-1, keepdims=True), 1e-30)
        oc_ref[0, g] = _nn(p.astype(BF16), vc_ref[0, g])
        imp = jnp.sum(p.reshape(hq, tp, npad), axis=0)

        blk = lax.broadcasted_iota(jnp.int32, (tp, npad), 1)
        cur = (past + lax.broadcasted_iota(jnp.int32, (tp, npad), 0)) // SEL_BLOCK
        val = jnp.where((blk == 0) | (blk == cur) | (blk == cur - 1), FORCE, imp)
        val = jnp.where(blk > cur, -1.0, val)
        val = jnp.where(blk >= past // SEL_BLOCK, -2.0, val)
        val_ref[0, g] = val


def _topk_kernel(val_ref, idx_ref, *, n_top):
    val = val_ref[...]
    rows, n = val.shape
    blk = lax.broadcasted_iota(jnp.int32, (rows, n), 1)
    lane = lax.broadcasted_iota(jnp.int32, (rows, LANE), 1)
    idx = jnp.zeros((rows, LANE), jnp.int32)
    for r in range(n_top):
        best = jnp.max(val, axis=-1, keepdims=True)
        j = jnp.min(jnp.where(val == best, blk, n), axis=-1, keepdims=True)
        idx = jnp.where(lane == r, j, idx)
        val = jnp.where(blk == j, -3e38, val)
    idx_ref[...] = idx


def _topk(val2, *, n_top):
    rows, n = val2.shape
    return pl.pallas_call(
        functools.partial(_topk_kernel, n_top=n_top),
        out_shape=jax.ShapeDtypeStruct((rows, LANE), jnp.int32),
        grid=(1,),
        in_specs=[pl.BlockSpec((rows, n), lambda i: (0, 0))],
        out_specs=pl.BlockSpec((rows, LANE), lambda i: (0, 0)),
        compiler_params=_params(("arbitrary",)),
        name="sample_topk",
    )(val2)


def _sample_pre(za3, zb3, bg3, tabs, kc, vc, *, past):
    bs, tp, _ = za3.shape
    npad = kc.shape[2]
    qw = NSA_HEADS * LANE
    kvblk = KV_OFF // 512
    zspec = lambda k: pl.BlockSpec((1, tp, 512), lambda bi, k=k: (bi, 0, kvblk + k))
    tspec = pl.BlockSpec((tp, LANE), lambda bi: (0, 0))
    cspec = pl.BlockSpec((1, NSA_KV, npad, LANE), lambda bi: (bi, 0, 0, 0))
    rows = NSA_HPG * tp
    gspec = pl.BlockSpec((1, NSA_KV, rows, LANE), lambda bi: (bi, 0, 0, 0))
    gshape = jax.ShapeDtypeStruct((bs, NSA_KV, rows, LANE), F32)
    return pl.pallas_call(
        functools.partial(_sample_pre_kernel, past=past),
        out_shape=[jax.ShapeDtypeStruct((bs, tp, 1024), F32),
                   jax.ShapeDtypeStruct((bs, tp, 512), F32),
                   gshape, gshape, gshape,
                   jax.ShapeDtypeStruct((bs, NSA_KV, N_BRANCH, rows, LANE), F32),
                   jax.ShapeDtypeStruct((bs, NSA_KV, tp, npad), F32)],
        grid=(bs,),
        in_specs=[pl.BlockSpec((1, tp, qw), lambda bi: (bi, 0, Q_OFF // qw)),
                  pl.BlockSpec((1, tp, qw), lambda bi: (bi, 0, NG_OFF // qw)),
                  pl.BlockSpec((1, tp, LANE), lambda bi: (bi, 0, 0)),
                  zspec(0), zspec(1), zspec(2), tspec, tspec, tspec, cspec, cspec],
        out_specs=[pl.BlockSpec((1, tp, 1024), lambda bi: (bi, 0, 0)),
                   pl.BlockSpec((1, tp, 512), lambda bi: (bi, 0, 0)),
                   gspec, gspec, gspec,
                   pl.BlockSpec((1, NSA_KV, N_BRANCH, rows, LANE), lambda bi: (bi, 0, 0, 0, 0)),
                   pl.BlockSpec((1, NSA_KV, tp, npad), lambda bi: (bi, 0, 0, 0))],
        compiler_params=_params(("parallel",)),
        name="sample_pre",
    )(za3, za3, bg3, zb3, zb3, zb3, *tabs, kc, vc)


def _sample_attn_kernel(idx_sm, pt_sm, qr_ref, oc_ref, ng_ref, gate_ref, ksn_ref, vsn_ref,
                        wc_ref, kwn_ref, vwn_ref, cache_ref, o_ref,
                        kbuf, vbuf, kw_scr, vw_scr, sem, *, ts, n_top, n_pages, page_base, per_page, wb):
    tp = SAMPLE_T_PAD
    hq = NSA_HPG
    b = pl.program_id(0)
    g = pl.program_id(1)
    n_gath = n_top * SEL_BLOCK
    ks_rows = kbuf.shape[2]
    step = b * NSA_KV + g
    n_steps = pl.num_programs(0) * NSA_KV
    slot = step % 2

    def gather_copies(step_idx, t, r, to_slot):
        bb, gg = step_idx // NSA_KV, step_idx % NSA_KV
        blk = idx_sm[(step_idx * ts + t) * n_top + r]
        page = pt_sm[bb * n_pages + blk // per_page] + page_base
        row0 = (blk % per_page) * SEL_BLOCK
        src_k = cache_ref.at[page, pl.ds(row0, SEL_BLOCK), 2 * NSA_KV + gg]
        src_v = cache_ref.at[page, pl.ds(row0, SEL_BLOCK), 3 * NSA_KV + gg]
        dst = pl.ds(r * SEL_BLOCK, SEL_BLOCK)
        return (pltpu.make_async_copy(src_k, kbuf.at[to_slot, t, dst], sem.at[to_slot, 0]),
                pltpu.make_async_copy(src_v, vbuf.at[to_slot, t, dst], sem.at[to_slot, 1]))

    def start_gathers(step_idx, to_slot):
        for t in range(ts):
            for r in range(n_top):
                ck, cv = gather_copies(step_idx, t, r, to_slot)
                ck.start()
                cv.start()

    @pl.when(step == 0)
    def _():
        start_gathers(0, 0)

    @pl.when(step + 1 < n_steps)
    def _():
        start_gathers(step + 1, 1 - slot)

    qr = qr_ref[0, 0].astype(BF16)
    trow = lax.broadcasted_iota(jnp.int32, (hq * tp, 1), 0) % tp

    ww = kw_scr.shape[0]
    kw_scr[pl.ds(0, wb), :] = wc_ref[0, pl.ds(g, wb, stride=2 * NSA_KV), :]
    vw_scr[pl.ds(0, wb), :] = wc_ref[0, pl.ds(NSA_KV + g, wb, stride=2 * NSA_KV), :]
    kw_scr[pl.ds(wb, tp), :] = kwn_ref[0]
    vw_scr[pl.ds(wb, tp), :] = vwn_ref[0]
    kw_scr[pl.ds(wb + tp, ww - wb - tp), :] = jnp.zeros((ww - wb - tp, LANE), F32)
    vw_scr[pl.ds(wb + tp, ww - wb - tp), :] = jnp.zeros((ww - wb - tp, LANE), F32)
    sw = _nt(qr, kw_scr[...].astype(BF16))
    jw = lax.broadcasted_iota(jnp.int32, (hq * tp, ww), 1)
    rel = jw - wb
    okw = (rel <= trow) & (rel > trow - WINDOW) & (jw < wb + ts)
    sw = jnp.where(okw, sw, MASK_NEG)
    ew = jnp.where(okw, jnp.exp(sw - jnp.max(sw, axis=-1, keepdims=True)), 0.0)
    o_w = _nn(ew.astype(BF16), vw_scr[...].astype(BF16)) / jnp.sum(ew, axis=-1, keepdims=True)

    for t in range(ts):
        for r in range(n_top):
            ck, cv = gather_copies(step, t, r, slot)
            ck.wait()
            cv.wait()

    js = lax.broadcasted_iota(jnp.int32, (hq * tp, ks_rows), 1)
    o_s = jnp.zeros((hq * tp, LANE), F32)
    for t in range(ts):
        kbuf[slot, t, pl.ds(n_gath, tp), :] = ksn_ref[0]
        vbuf[slot, t, pl.ds(n_gath, tp), :] = vsn_ref[0]
        kbuf[slot, t, pl.ds(n_gath + tp, ks_rows - n_gath - tp), :] = jnp.zeros((ks_rows - n_gath - tp, LANE), F32)
        vbuf[slot, t, pl.ds(n_gath + tp, ks_rows - n_gath - tp), :] = jnp.zeros((ks_rows - n_gath - tp, LANE), F32)
        ss = _nt(qr, kbuf[slot, t].astype(BF16))
        oks = (js < n_gath) | ((js - n_gath <= t) & (js < n_gath + ts))
        ss = jnp.where(oks, ss, MASK_NEG)
        es = jnp.where(oks, jnp.exp(ss - jnp.max(ss, axis=-1, keepdims=True)), 0.0)
        ot = _nn(es.astype(BF16), vbuf[slot, t].astype(BF16)) / jnp.sum(es, axis=-1, keepdims=True)
        o_s = jnp.where(trow == t, ot, o_s)

    o = gate_ref[0, 0, 0] * oc_ref[0, 0] + gate_ref[0, 0, 1] * o_s + gate_ref[0, 0, 2] * o_w
    y = _silu(ng_ref[0, 0]) * o
    for h in range(hq):
        o_ref[0, :, h * LANE:(h + 1) * LANE] = y[h * tp:(h + 1) * tp].astype(o_ref.dtype)


def _sample_attn(idx_flat, page_flat, qr, oc, ngo, gates, kvn, cache_win_rows, wn, cache4, *,
                 ts, n_top, n_pages, page_base, win_base):
    bs = qr.shape[0]
    tp = SAMPLE_T_PAD
    rows = NSA_HPG * tp
    wb = cache_win_rows.shape[1] // (2 * NSA_KV)
    per_page = cache4.shape[1] // SEL_BLOCK
    ks_rows = -(-(n_top * SEL_BLOCK + tp) // LANE) * LANE
    ww = -(-(wb + tp) // LANE) * LANE
    gspec = pl.BlockSpec((1, 1, rows, LANE), lambda bi, gi, *_: (bi, gi, 0, 0))
    newspec = lambda k: pl.BlockSpec((1, tp, LANE), lambda bi, gi, *_, k=k: (bi, 0, k + gi))
    grid_spec = pltpu.PrefetchScalarGridSpec(
        num_scalar_prefetch=2, grid=(bs, NSA_KV),
        in_specs=[gspec, gspec, gspec,
                  pl.BlockSpec((1, 1, N_BRANCH, rows, LANE), lambda bi, gi, *_: (bi, gi, 0, 0, 0)),
                  newspec(2 * NSA_KV), newspec(3 * NSA_KV),
                  pl.BlockSpec((1,) + cache_win_rows.shape[1:], lambda bi, gi, *_: (bi + win_base, 0, 0)),
                  newspec(0), newspec(NSA_KV),
                  pl.BlockSpec(memory_space=pl.ANY)],
        out_specs=pl.BlockSpec((1, tp, NSA_HPG * LANE), lambda bi, gi, *_: (bi, 0, gi)),
        scratch_shapes=[pltpu.VMEM((2, ts, ks_rows, LANE), F32), pltpu.VMEM((2, ts, ks_rows, LANE), F32),
                        pltpu.VMEM((ww, LANE), F32), pltpu.VMEM((ww, LANE), F32),
                        pltpu.SemaphoreType.DMA((2, 2))])
    kern = functools.partial(_sample_attn_kernel, ts=ts, n_top=n_top, n_pages=n_pages,
                             page_base=page_base, per_page=per_page, wb=wb)
    return pl.pallas_call(
        kern,
        out_shape=jax.ShapeDtypeStruct((bs, tp, NSA_HEADS * LANE), BF16),
        grid_spec=grid_spec,
        compiler_params=_params(("arbitrary", "arbitrary")),
        name="sample_attn",
    )(idx_flat, page_flat, qr, oc, ngo, gates, kvn, kvn, cache_win_rows, wn, wn, cache4)


def _rope_tables(pos, rows):
    freqs = jnp.power(ROPE_THETA, -jnp.arange(ROPE_HALF, dtype=F32) * (2.0 / ROPE_DIM))
    ang = pos.astype(F32)[:, None] * freqs[None, :]
    cos, sin = jnp.cos(ang), jnp.sin(ang)
    n = pos.shape[0]
    z16 = jnp.zeros((n, ROPE_HALF), F32)
    rest = LANE - ROPE_DIM
    c = jnp.concatenate([cos, cos, jnp.ones((n, rest), F32)], axis=1)
    s1 = jnp.concatenate([z16, sin, jnp.zeros((n, rest), F32)], axis=1)
    s2 = jnp.concatenate([-sin, z16, jnp.zeros((n, rest), F32)], axis=1)
    pad = lambda a: jnp.pad(a, ((0, rows - n), (0, 0)))
    return pad(c), pad(s1), pad(s2)


def _layer_weights(norm_g, w_in3, layer, w_conv, a_cmp, w_cmp, w_out):
    w_a, w_b, wbg = _pack_in_proj_weights(jnp.swapaxes(w_in3, 1, 2), layer)
    a4 = jnp.concatenate([a_cmp[0], a_cmp[0], a_cmp[1], a_cmp[1]], axis=1)
    a3 = jnp.stack([a_cmp[0], a_cmp[0], a_cmp[1], a_cmp[1]], axis=1)
    w4 = jnp.stack([w_cmp[0], w_cmp[0], w_cmp[1], w_cmp[1]]).astype(BF16)
    return dict(norm_g=norm_g, w_a=w_a, w_b=w_b, wbg=wbg, w_conv=w_conv, a4=a4, a3=a3, w4=w4,
                w_out=w_out.astype(BF16))


def _prompt_layer(xp, xs2, mem_prompt, mem_norm_g, w_mem, lw, final_g, final, pool):
    b, t, d = xp.shape
    m = b * t
    x2 = xp.reshape(m, d)
    tabs = _rope_tables(jnp.arange(t, dtype=jnp.int32), t)
    wb = min(WINDOW, t)
    qc, qr, ng, bg, ya, conv_new, za_s, bg_s = _proj_conv(x2, lw["norm_g"], lw["w_a"], lw["wbg"], lw["w_conv"], tabs,
                                                          xs2, tm=min(512, t), seq_len=t)
    kvn, kvb, win_rows, pooled_prompt, zm, zb_s = _proj_kv(x2, lw["norm_g"], lw["w_b"], tabs, lw["a4"], xs2,
                                                           tm=wb, seq_len=t)
    nm = mem_prompt.shape[1]
    mkv = _norm_matmul(mem_prompt.reshape(b * nm, d), mem_norm_g, w_mem, tm=min(512, b * nm), tn=512)
    mkv3 = mkv.reshape(b, nm, 2 * MEM_HEADS * LANE)
    kc, vc = _cmp_proj(pooled_prompt.reshape(b, t // CMP_BLOCK, -1), lw["w4"])
    r3 = lambda a: a.reshape(b, t, -1)
    yb, pooled = _nsa_prompt(r3(qc), r3(qr), r3(ng), r3(bg), r3(kvb), kc, vc, tq=min(256, t), pool=pool)
    ym = _mem_attn(zm.reshape(b, t, -1), mkv3, mq_off=0, mg_off=MG_OFF - MQ_OFF, tq=min(1024, t), interleaved=False)
    out = _out_proj(x2, ya, yb.reshape(m, -1), ym.reshape(m, -1),
                    lw["w_out"], final_g, tm=min(512, m), final=final)
    kv_new = kvn.reshape(b, t, 4, NSA_KV, HEAD_DIM)
    win_new = win_rows.reshape(b, wb, 2, NSA_KV, HEAD_DIM)
    mem_kv = mkv.reshape(b, nm, 2, MEM_HEADS, HEAD_DIM)
    return out.reshape(b, t, d), kv_new, win_new, conv_new, mem_kv, pooled, (za_s, zb_s, bg_s)


def _sample_layer(xs_p, proj, ts, layer, cache4, page_flat, pooled, cache_win, state_conv, cache_mem, lw, final_g,
                  final):
    bs, tp, d = xs_p.shape
    depth = cache_win.shape[0]
    pool, page = cache4.shape[0] // depth, cache4.shape[1]
    n_pages = page_flat.shape[0] // bs
    past = n_pages * page
    assert past % SEL_BLOCK == 0 and ts <= SEL_BLOCK and ts <= tp
    n_past = past // SEL_BLOCK
    n_top = min(TOP_N, n_past + 1) - 1
    m = bs * tp
    za, zb, bg = proj
    za3 = za.reshape(bs, tp, ZA_WIDTH)
    zb3 = zb.reshape(bs, tp, ZB_WIDTH)
    bg3 = bg.reshape(bs, tp, LANE)
    tabs = _rope_tables(past + jnp.arange(tp, dtype=jnp.int32), tp)
    if pooled is None:
        pooled = _pool_pages(cache4, page_flat, lw["a3"], bs=bs, n_pages=n_pages, page_base=layer * pool)
    else:
        pooled = pooled.reshape(bs, n_pages * (page // CMP_BLOCK), 2 * NSA_KV * LANE)
    kc, vc = _cmp_proj(pooled, lw["w4"])
    kvn, wn, qr, oc, ngo, gates, val = _sample_pre(za3, zb3, bg3, tabs, kc, vc, past=past)
    idx = _topk(val.reshape(bs * NSA_KV * tp, val.shape[-1]), n_top=n_top)
    idx_flat = idx.reshape(bs, NSA_KV, tp, LANE)[:, :, :ts, :n_top].reshape(-1)
    wbuf = cache_win.shape[2]
    cache_win_rows = cache_win.reshape(depth * bs, wbuf * 2 * NSA_KV, HEAD_DIM)
    yb = _sample_attn(idx_flat, page_flat, qr, oc, ngo, gates, kvn, cache_win_rows, wn, cache4,
                      ts=ts, n_top=n_top, n_pages=n_pages, page_base=layer * pool, win_base=layer * bs)
    ya, conv_new = _conv_mixer(za3, state_conv[layer], lw["w_conv"], t_real=ts)
    nm = cache_mem.shape[2]
    mem_rows = cache_mem.reshape(depth * bs, nm * 2 * MEM_HEADS, HEAD_DIM)
    ym = _mem_attn(zb3, mem_rows, mq_off=MQ_OFF, mg_off=MG_OFF, tq=tp, interleaved=True, kv_base=layer * bs)
    out = _out_proj(xs_p.reshape(m, d), ya.reshape(m, -1), yb.reshape(m, -1), ym.reshape(m, -1),
                    lw["w_out"], final_g, tm=m, final=final)
    kv_new = kvn[:, :ts].reshape(bs, ts, 4, NSA_KV, HEAD_DIM)
    win_rows = wn[:, :ts].reshape(bs, ts, 2, NSA_KV, HEAD_DIM)
    win_state = jnp.concatenate([cache_win[layer], win_rows], axis=1)[:, ts:]
    return out.reshape(bs, tp, d), kv_new, win_state, conv_new


def kernel(x_prompt, x_sample, cache_kv, cache_win, state_conv, cache_mem, page_table, mem_prompt,
           norm_g, w_in, w_conv, a_cmp, w_cmp, mem_norm_g, w_mem_kv, w_out, final_g):
    depth = w_in.shape[0]
    ts = x_sample.shape[1]
    xp = x_prompt
    xs = jnp.pad(x_sample, ((0, 0), (0, SAMPLE_T_PAD - ts), (0, 0)))
    pool_size, page = cache_kv.shape[1], cache_kv.shape[2]
    cache4 = cache_kv.reshape(depth * pool_size, page, 4 * NSA_KV, HEAD_DIM)
    page_flat = page_table.reshape(-1).astype(jnp.int32)
    kv_p, win_p, conv_p, mem_p, kv_s, win_s, conv_s = [], [], [], [], [], [], []
    for l in range(depth):
        lw = _layer_weights(norm_g[l], w_in, l, w_conv[l], a_cmp[l], w_cmp[l], w_out[l])
        final = l == depth - 1
        xp, kvn, winn, convn, mkv, pooled, proj_s = _prompt_layer(
            xp, xs.reshape(-1, xs.shape[-1]), mem_prompt, mem_norm_g[l], w_mem_kv[l].T.astype(BF16), lw, final_g, final,
            pool=(cache4, page_flat, lw["a3"], l * pool_size))
        kv_p.append(kvn)
        win_p.append(winn)
        conv_p.append(convn)
        mem_p.append(mkv)
        xs, kvn, winn, convn = _sample_layer(xs, proj_s, ts, l, cache4, page_flat, pooled, cache_win, state_conv,
                                             cache_mem, lw, final_g, final)
        kv_s.append(kvn)
        win_s.append(winn)
        conv_s.append(convn)
    return (xp, xs[:, :ts], jnp.stack(kv_p), jnp.stack(win_p), jnp.stack(conv_p), jnp.stack(mem_p),
            jnp.stack(kv_s), jnp.stack(win_s), jnp.stack(conv_s))
```

```python
import functools

import jax
import jax.numpy as jnp
from jax import lax
from jax.experimental import pallas as pl
from jax.experimental.pallas import tpu as pltpu

F32 = jnp.float32
BF16 = jnp.bfloat16

HEAD_DIM = 128
CONV_DIM = 512
CONV_W = 3
NSA_HEADS = 8
NSA_KV = 2
NSA_HPG = NSA_HEADS // NSA_KV
MEM_HEADS = 4
N_BRANCH = 3
ROPE_DIM = HEAD_DIM // 4
ROPE_HALF = ROPE_DIM // 2
ROPE_THETA = 500000.0
CMP_BLOCK = 64
SEL_BLOCK = 64
SEL_SHIFT = 6
TOP_N = 16
WINDOW = 512
NORM_EPS = 1e-6
MASK_NEG = -1e30
FORCE = 1e9
ATTN_SCALE = HEAD_DIM ** -0.5
SCALE_LOG2 = ATTN_SCALE * 1.4426950408889634

C_H, C_B, C_C, C_G, Q_OFF, NG_OFF = 0, 512, 1024, 1536, 2048, 3072
ZA_WIDTH = 4096
KV_OFF, MQ_OFF, MG_OFF = 0, 1536, 2048
ZB_WIDTH = 2560
BG_SRC = 4096
BG_N = NSA_HEADS * N_BRANCH
LANE = 128
SAMPLE_T_PAD = 8
VMEM_LIMIT = 56 * 1024 * 1024


def _nt(a, b):
    return lax.dot_general(a, b, (((1,), (1,)), ((), ())), preferred_element_type=F32)


def _nn(a, b):
    return jnp.dot(a, b, preferred_element_type=F32)


def _params(sem, vmem=VMEM_LIMIT):
    return pltpu.CompilerParams(dimension_semantics=sem, vmem_limit_bytes=vmem)


def _rope(x, c, s1, s2):
    return x * c + pltpu.roll(x, ROPE_HALF, 1) * s1 + pltpu.roll(x, LANE - ROPE_HALF, 1) * s2


def _silu(x):
    return x * jax.nn.sigmoid(x)


def _rms_rows(x_ref, g_ref):
    x = x_ref[...]
    return (x * lax.rsqrt(jnp.mean(x * x, axis=-1, keepdims=True) + NORM_EPS) * g_ref[...]).astype(BF16)


def _norm_matmul_kernel(x_ref, g_ref, w_ref, z_ref, h_scr, *, tn):
    h_scr[...] = _rms_rows(x_ref, g_ref)
    for j in range(w_ref.shape[1] // tn):
        z_ref[:, j * tn:(j + 1) * tn] = _nn(h_scr[...], w_ref[:, j * tn:(j + 1) * tn].astype(BF16))


def _norm_matmul(x, g, w, *, tm, tn):
    m, d = x.shape
    n = w.shape[1]
    assert n % tn == 0 and m % tm == 0 and w.shape[0] == d
    resident = lambda shape: pl.BlockSpec(shape, lambda i: (0, 0), pipeline_mode=pl.Buffered(1))
    return pl.pallas_call(
        functools.partial(_norm_matmul_kernel, tn=tn),
        out_shape=jax.ShapeDtypeStruct((m, n), F32),
        grid=(m // tm,),
        in_specs=[pl.BlockSpec((tm, d), lambda i: (i, 0)), resident((1, d)), resident((d, n))],
        out_specs=pl.BlockSpec((tm, n), lambda i: (i, 0)),
        scratch_shapes=[pltpu.VMEM((tm, d), BF16)],
        compiler_params=_params(("parallel",)),
        name="norm_matmul",
    )(x, g.reshape(1, d), w)


def _proj_conv_kernel(x_ref, g_ref, w_ref, wbg_ref, wc_ref, c_ref, s1_ref, s2_ref, xs_ref,
                      qc_ref, qr_ref, ng_ref, bg_ref, ya_ref, st_ref, zs_ref, bgs_ref, h_scr, up_scr,
                      *, blocks_per_seq):
    tm = x_ref.shape[0]
    cw = CONV_DIM
    i = pl.program_id(0)
    first = i % blocks_per_seq == 0
    h_scr[...] = _rms_rows(x_ref, g_ref)
    bg_ref[...] = _nt(h_scr[...], wbg_ref[...])
    chunk = lambda off: _nt(h_scr[...], w_ref[off:off + cw, :])

    @pl.when(i == 0)
    def _():
        up_scr[...] = jnp.zeros(up_scr.shape, F32)

    carry = up_scr[pl.ds(8 + tm - (CONV_W - 1), CONV_W - 1), :]
    up_scr[pl.ds(8 - (CONV_W - 1), CONV_W - 1), :] = jnp.where(first, 0.0, carry)
    u = chunk(C_C) * chunk(C_H)
    up_scr[pl.ds(8, tm), :] = u
    wc = wc_ref[...]
    y = wc[0:1, :] * up_scr[pl.ds(6, tm), :]
    y = y + wc[1:2, :] * up_scr[pl.ds(7, tm), :]
    y = y + wc[2:3, :] * u
    y = chunk(C_B) * y
    ya_ref[...] = (_silu(chunk(C_G)) * y).astype(ya_ref.dtype)

    st_ref[0] = up_scr[pl.ds(8 + tm - (CONV_W - 1), CONV_W - 1), :]

    c, s1, s2 = c_ref[...], s1_ref[...], s2_ref[...]
    for j in range((NG_OFF - Q_OFF) // cw):
        qv = chunk(Q_OFF + j * cw)
        for k in range(cw // LANE):
            lo = j * cw + k * LANE
            qh = qv[:, k * LANE:(k + 1) * LANE]
            qc_ref[:, lo:lo + LANE] = (qh * SCALE_LOG2).astype(BF16)
            qr_ref[:, lo:lo + LANE] = (_rope(qh, c, s1, s2) * SCALE_LOG2).astype(BF16)
    for j in range((ZA_WIDTH - NG_OFF) // cw):
        ng_ref[:, j * cw:(j + 1) * cw] = _silu(chunk(NG_OFF + j * cw))

    @pl.when(i == pl.num_programs(0) - 1)
    def _():
        hs = _rms_rows(xs_ref, g_ref)
        bgs_ref[...] = _nt(hs, wbg_ref[...])
        for j in range(ZA_WIDTH // cw):
            zs_ref[:, j * cw:(j + 1) * cw] = _nt(hs, w_ref[j * cw:(j + 1) * cw, :])


def _proj_conv(x2, g, w_a, wbg, w_conv, tabs, xs2, *, tm, seq_len):
    m, d = x2.shape
    ms = xs2.shape[0]
    assert seq_len % tm == 0 and CONV_W == 3
    blocks_per_seq = seq_len // tm
    nq, nng = NG_OFF - Q_OFF, ZA_WIDTH - NG_OFF
    resident = lambda shape: pl.BlockSpec(shape, lambda i: (0,) * len(shape), pipeline_mode=pl.Buffered(1))
    row = lambda width: pl.BlockSpec((tm, width), lambda i: (i, 0))
    tspec = pl.BlockSpec((tm, LANE), lambda i: (i % blocks_per_seq, 0))
    return pl.pallas_call(
        functools.partial(_proj_conv_kernel, blocks_per_seq=blocks_per_seq),
        out_shape=[jax.ShapeDtypeStruct((m, nq), BF16), jax.ShapeDtypeStruct((m, nq), BF16),
                   jax.ShapeDtypeStruct((m, nng), F32), jax.ShapeDtypeStruct((m, LANE), F32),
                   jax.ShapeDtypeStruct((m, CONV_DIM), BF16),
                   jax.ShapeDtypeStruct((m // seq_len, CONV_W - 1, CONV_DIM), F32),
                   jax.ShapeDtypeStruct((ms, ZA_WIDTH), F32), jax.ShapeDtypeStruct((ms, LANE), F32)],
        grid=(m // tm,),
        in_specs=[row(d), resident((1, d)), resident(w_a.shape), resident(wbg.shape), resident(w_conv.shape),
                  tspec, tspec, tspec, resident((ms, d))],
        out_specs=[row(nq), row(nq), row(nng), row(LANE), row(CONV_DIM),
                   pl.BlockSpec((1, CONV_W - 1, CONV_DIM), lambda i: (i // blocks_per_seq, 0, 0)),
                   pl.BlockSpec((ms, ZA_WIDTH), lambda i: (0, 0)), pl.BlockSpec((ms, LANE), lambda i: (0, 0))],
        scratch_shapes=[pltpu.VMEM((tm, d), BF16), pltpu.VMEM((tm + 8, CONV_DIM), F32)],
        compiler_params=_params(("arbitrary",)),
        name="proj_conv",
    )(x2, g.reshape(1, d), w_a, wbg, w_conv, *tabs, xs2)


def _proj_kv_kernel(x_ref, g_ref, w_ref, c_ref, s1_ref, s2_ref, a_ref, xs_ref,
                    kvn_ref, kvb_ref, win_ref, pool_ref, zm_ref, zs_ref, h_scr, *, blocks_per_seq):
    tm = x_ref.shape[0]
    cw = 2 * NSA_KV * LANE
    h_scr[...] = _rms_rows(x_ref, g_ref)
    chunk = lambda off: _nt(h_scr[...], w_ref[off:off + cw, :])
    c, s1, s2 = c_ref[...], s1_ref[...], s2_ref[...]
    kv0 = chunk(KV_OFF)
    kv1 = chunk(KV_OFF + cw)
    kv2 = chunk(KV_OFF + 2 * cw)
    n_kv, n_w, half = 4 * NSA_KV, 2 * NSA_KV, NSA_KV * LANE
    for g in range(NSA_KV):
        lo, hi = g * LANE, (g + 1) * LANE
        ks = _rope(kv1[:, lo:hi], c, s1, s2)
        kw = _rope(kv2[:, lo:hi], c, s1, s2)
        vs = kv1[:, half + lo:half + hi]
        vw = kv2[:, half + lo:half + hi]
        kvn_ref[pl.ds(g, tm, stride=n_kv), :] = kv0[:, lo:hi]
        kvn_ref[pl.ds(NSA_KV + g, tm, stride=n_kv), :] = kv0[:, half + lo:half + hi]
        kvn_ref[pl.ds(2 * NSA_KV + g, tm, stride=n_kv), :] = ks
        kvn_ref[pl.ds(3 * NSA_KV + g, tm, stride=n_kv), :] = vs
        kvb_ref[:, lo:hi] = ks.astype(BF16)
        kvb_ref[:, half + lo:half + hi] = vs.astype(BF16)
        kvb_ref[:, 2 * half + lo:2 * half + hi] = kw.astype(BF16)
        kvb_ref[:, 3 * half + lo:3 * half + hi] = vw.astype(BF16)
        win_ref[pl.ds(g, tm, stride=n_w), :] = kw
        win_ref[pl.ds(NSA_KV + g, tm, stride=n_w), :] = vw

    pool_ref[...] = jnp.sum(kv0.reshape(tm // CMP_BLOCK, CMP_BLOCK, cw) * a_ref[...][None], axis=1)
    for j in range(zm_ref.shape[1] // cw):
        zm_ref[:, j * cw:(j + 1) * cw] = chunk(MQ_OFF + j * cw)

    @pl.when(pl.program_id(0) == pl.num_programs(0) - 1)
    def _():
        hs = _rms_rows(xs_ref, g_ref)
        for j in range(ZB_WIDTH // cw):
            zs_ref[:, j * cw:(j + 1) * cw] = _nt(hs, w_ref[j * cw:(j + 1) * cw, :])


def _proj_kv(x2, g, w_b, tabs, a4, xs2, *, tm, seq_len):
    m, d = x2.shape
    ms = xs2.shape[0]
    assert seq_len % tm == 0
    blocks_per_seq = seq_len // tm
    n_kv, n_w = 4 * NSA_KV, 2 * NSA_KV
    nm = ZB_WIDTH - MQ_OFF
    resident = lambda shape: pl.BlockSpec(shape, lambda i: (0,) * len(shape), pipeline_mode=pl.Buffered(1))
    row = lambda width: pl.BlockSpec((tm, width), lambda i: (i, 0))
    tspec = pl.BlockSpec((tm, LANE), lambda i: (i % blocks_per_seq, 0))
    return pl.pallas_call(
        functools.partial(_proj_kv_kernel, blocks_per_seq=blocks_per_seq),
        out_shape=[jax.ShapeDtypeStruct((m * n_kv, LANE), F32),
                   jax.ShapeDtypeStruct((m, n_kv * LANE), BF16),
                   jax.ShapeDtypeStruct((m // seq_len * tm * n_w, LANE), F32),
                   jax.ShapeDtypeStruct((m // CMP_BLOCK, 2 * NSA_KV * LANE), F32),
                   jax.ShapeDtypeStruct((m, nm), F32), jax.ShapeDtypeStruct((ms, ZB_WIDTH), F32)],
        grid=(m // tm,),
        in_specs=[row(d), resident((1, d)), resident(w_b.shape), tspec, tspec, tspec, resident(a4.shape),
                  resident((ms, d))],
        out_specs=[pl.BlockSpec((tm * n_kv, LANE), lambda i: (i, 0)), row(n_kv * LANE),
                   pl.BlockSpec((tm * n_w, LANE), lambda i: (i // blocks_per_seq, 0)),
                   pl.BlockSpec((tm // CMP_BLOCK, 2 * NSA_KV * LANE), lambda i: (i, 0)), row(nm),
                   pl.BlockSpec((ms, ZB_WIDTH), lambda i: (0, 0))],
        scratch_shapes=[pltpu.VMEM((tm, d), BF16)],
        compiler_params=_params(("arbitrary",)),
        name="proj_kv",
    )(x2, g.reshape(1, d), w_b, *tabs, a4, xs2)


def _cast_rows_kernel(w_ref, o_ref):
    o_ref[...] = w_ref[0].astype(BF16)


def _pack_gate_kernel(w_ref, o_ref):
    o_ref[...] = jnp.zeros(o_ref.shape, BF16)
    o_ref[0:w_ref.shape[1], :] = w_ref[0].astype(BF16)


def _pack_in_proj_weights(w_t3, layer):
    _, n_in, d = w_t3.shape
    tn = 512
    assert n_in == BG_SRC + BG_N + ZB_WIDTH and BG_SRC % tn == 0 and ZB_WIDTH % tn == 0 and BG_N % 8 == 0

    def cast_rows(first_row, n_rows, name):
        return pl.pallas_call(
            _cast_rows_kernel,
            out_shape=jax.ShapeDtypeStruct((n_rows, d), BF16),
            grid=(n_rows // tn,),
            in_specs=[pl.BlockSpec((pl.Element(1), pl.Element(tn), pl.Element(d)),
                                   lambda j: (layer, pl.multiple_of(first_row + j * tn, 8), 0))],
            out_specs=pl.BlockSpec((tn, d), lambda j: (j, 0)),
            compiler_params=_params(("parallel",)),
            name=name,
        )(w_t3)

    wa = cast_rows(0, ZA_WIDTH, "pack_w_a")
    wb = cast_rows(BG_SRC + BG_N, ZB_WIDTH, "pack_w_b")
    wbg = pl.pallas_call(
        _pack_gate_kernel,
        out_shape=jax.ShapeDtypeStruct((LANE, d), BF16),
        grid=(1,),
        in_specs=[pl.BlockSpec((pl.Element(1), pl.Element(BG_N), pl.Element(d)), lambda j: (layer, BG_SRC, 0))],
        out_specs=pl.BlockSpec((LANE, d), lambda j: (0, 0)),
        compiler_params=_params(("arbitrary",)),
        name="pack_w_gate",
    )(w_t3)
    return wa, wb, wbg


def _pool_pages_kernel(pt_sm, a_ref, cache_ref, o_ref, buf, sem, *, pages_per_step, page_base):
    step = pl.program_id(0)
    n_steps = pl.num_programs(0)
    slot = step % 2
    n_cols = buf.shape[3]

    def page_copy(step_idx, p, to_slot):
        page = pt_sm[step_idx * pages_per_step + p] + page_base
        return pltpu.make_async_copy(cache_ref.at[page, :, pl.ds(0, n_cols), :], buf.at[to_slot, p], sem.at[to_slot])

    @pl.when(step == 0)
    def _():
        for p in range(pages_per_step):
            page_copy(0, p, 0).start()

    @pl.when(step + 1 < n_steps)
    def _():
        for p in range(pages_per_step):
            page_copy(step + 1, p, 1 - slot).start()

    for p in range(pages_per_step):
        page_copy(step, p, slot).wait()

    a = a_ref[...]
    per = buf.shape[2] // CMP_BLOCK
    for p in range(pages_per_step):
        for k in range(per):
            x = buf[slot, p, pl.ds(k * CMP_BLOCK, CMP_BLOCK)]
            o_ref[0, p, k] = jnp.sum(x * a, axis=0)


def _pool_pages(cache4, page_flat, a3, *, bs, n_pages, page_base, pages_per_step=16):
    page = cache4.shape[1]
    per = page // CMP_BLOCK
    n_cols = 2 * NSA_KV
    total = bs * n_pages
    pages_per_step = min(pages_per_step, total)
    assert total % pages_per_step == 0
    n_steps = total // pages_per_step
    grid_spec = pltpu.PrefetchScalarGridSpec(
        num_scalar_prefetch=1, grid=(n_steps,),
        in_specs=[pl.BlockSpec((CMP_BLOCK, n_cols, LANE), lambda si, pt: (0, 0, 0)),
                  pl.BlockSpec(memory_space=pl.ANY)],
        out_specs=pl.BlockSpec((1, pages_per_step, per, n_cols, LANE), lambda si, pt: (si, 0, 0, 0, 0)),
        scratch_shapes=[pltpu.VMEM((2, pages_per_step, page, n_cols, LANE), F32),
                        pltpu.SemaphoreType.DMA((2,))])
    out = pl.pallas_call(
        functools.partial(_pool_pages_kernel, pages_per_step=pages_per_step, page_base=page_base),
        out_shape=jax.ShapeDtypeStruct((n_steps, pages_per_step, per, n_cols, LANE), F32),
        grid_spec=grid_spec,
        compiler_params=_params(("arbitrary",)),
        name="pool_pages",
    )(page_flat, a3, cache4)
    return out.reshape(bs, n_pages * per, n_cols * LANE)


def _cmp_proj_kernel(p_ref, w_ref, kc_ref, vc_ref):
    pooled = p_ref[0]
    n = pooled.shape[0]
    n_pad = kc_ref.shape[2]
    for c in range(4):
        r = _nn(pooled[:, c * LANE:(c + 1) * LANE].astype(BF16), w_ref[c]).astype(BF16)
        dst = kc_ref if c < 2 else vc_ref
        if n_pad > n:
            dst[0, c % 2] = jnp.zeros((n_pad, LANE), BF16)
        dst[0, c % 2, 0:n, :] = r


def _cmp_proj(pooled, w4):
    b, n, _ = pooled.shape
    n_pad = -(-n // LANE) * LANE
    spec = pl.BlockSpec((1, NSA_KV, n_pad, LANE), lambda bi: (bi, 0, 0, 0))
    return pl.pallas_call(
        _cmp_proj_kernel,
        out_shape=[jax.ShapeDtypeStruct((b, NSA_KV, n_pad, LANE), BF16)] * 2,
        grid=(b,),
        in_specs=[pl.BlockSpec((1, n, 512), lambda bi: (bi, 0, 0)),
                  pl.BlockSpec((4, LANE, LANE), lambda bi: (0, 0, 0))],
        out_specs=[spec, spec],
        compiler_params=_params(("parallel",)),
        name="cmp_proj",
    )(pooled, w4)


def _lane_parts(x):
    return [x[:, j * LANE:(j + 1) * LANE] for j in range(x.shape[1] // LANE)]


def _nsa_prompt_kernel(*refs, t_len, tq, tc, tw, n_sel, top, sub, pool_pages, page_base):
    if pool_pages:
        (pt_sm, qc_ref, qr_ref, ng_ref, bg_ref, ksel_ref, vsel_ref, kwin_ref, vwin_ref, kc_ref, vc_ref,
         pa_ref, cache_ref, o_ref, pool_ref,
         qc_scr, qr_scr, s_scr, p_scr, a_scr, m_scr, l_scr, acc_scr, oc_scr, b_scr,
         sw_scr, pw_scr, bw_scr, ow_scr, wl_scr, pbuf, psem) = refs
        step = (pl.program_id(0) * pl.num_programs(1) + pl.program_id(1)) * pl.num_programs(2) + pl.program_id(2)
        n_steps = pl.num_programs(0) * pl.num_programs(1) * pl.num_programs(2)
        slot = step % 2
        half_rows, n_cols = pbuf.shape[2], 2 * NSA_KV

        def page_copies(step_idx, p, to_slot):
            page = pt_sm[step_idx * pool_pages + p] + page_base
            return [pltpu.make_async_copy(cache_ref.at[page, pl.ds(hh * half_rows, half_rows), pl.ds(0, n_cols), :],
                                          pbuf.at[to_slot, p, :, pl.ds(hh * n_cols, n_cols), :], psem.at[to_slot])
                    for hh in range(2)]

        @pl.when(step == 0)
        def _():
            for p in range(pool_pages):
                for cp in page_copies(0, p, 0):
                    cp.start()

        @pl.when(step + 1 < n_steps)
        def _():
            for p in range(pool_pages):
                for cp in page_copies(step + 1, p, 1 - slot):
                    cp.start()

        for p in range(pool_pages):
            for cp in page_copies(step, p, slot):
                cp.wait()
        pa = pa_ref[...]
        for p in range(pool_pages):
            pool_ref[0, p] = jnp.sum(pbuf[slot, p] * pa, axis=0)
    else:
        (qc_ref, qr_ref, ng_ref, bg_ref, ksel_ref, vsel_ref, kwin_ref, vwin_ref, kc_ref, vc_ref,
         o_ref, qc_scr, qr_scr, s_scr, p_scr, a_scr, m_scr, l_scr, acc_scr, oc_scr, b_scr,
         sw_scr, pw_scr, bw_scr, ow_scr, wl_scr) = refs
    i = pl.program_id(2)
    hq = NSA_HPG
    for h in range(hq):
        qc_scr[pl.ds(h * tq, tq), :] = qc_ref[0, :, h * LANE:(h + 1) * LANE]
        qr_scr[pl.ds(h * tq, tq), 0:LANE] = qr_ref[0, :, h * LANE:(h + 1) * LANE]

    w0 = pl.multiple_of(jnp.clip(i * tq + tq - tw, 0, t_len - tw), LANE)
    kp = w0 + lax.broadcasted_iota(jnp.int32, (tq, tw), 1)
    tp = i * tq + lax.broadcasted_iota(jnp.int32, (tq, tw), 0)
    bw_scr[...] = jnp.where((kp <= tp) & (kp > tp - WINDOW), 0.0, MASK_NEG)
    sw_scr[...] = _nt(qr_scr[:, 0:LANE], kwin_ref[0, pl.ds(w0, tw), :])
    for r0 in range(0, hq * tq, sub):
        r = pl.ds(r0, sub)
        parts = _lane_parts(sw_scr[r, :] + bw_scr[pl.ds(r0 % tq, sub), :])
        m = jnp.max(functools.reduce(jnp.maximum, parts), axis=-1, keepdims=True)
        ews = [jnp.exp2(x - m) for x in parts]
        row_sum = jnp.sum(functools.reduce(jnp.add, ews), axis=-1, keepdims=True)
        wl_scr[r, :] = jnp.broadcast_to(1.0 / row_sum, (sub, LANE))
        pw_scr[r, :] = jnp.concatenate(ews, axis=1).astype(BF16)
    ow_scr[...] = _nn(pw_scr[...], vwin_ref[0, pl.ds(w0, tw), :])

    kc = kc_ref[0, 0]
    npad = kc.shape[0]
    s_scr[:, 0:npad] = _nt(qc_scr[...], kc)
    tpos = i * tq + lax.broadcasted_iota(jnp.int32, (tq, npad), 0)
    ncol = lax.broadcasted_iota(jnp.int32, (tq, npad), 1)
    cmask = (ncol + 1) * CMP_BLOCK <= tpos + 1
    imp = jnp.zeros((tq, npad), F32)
    for h in range(hq):
        r = pl.ds(h * tq, tq)
        s = jnp.where(cmask, s_scr[r, 0:npad], MASK_NEG)
        e = jnp.where(cmask, jnp.exp2(s - jnp.max(s, axis=-1, keepdims=True)), 0.0)
        p = e / jnp.maximum(jnp.sum(e, axis=-1, keepdims=True), 1e-30)
        p_scr[r, 0:npad] = p.astype(BF16)
        imp = imp + p
    oc_scr[...] = _nn(p_scr[:, 0:npad], vc_ref[0, 0])

    rows = min(npad, -(-n_sel // 8) * 8)
    imp_t = imp.T[0:rows]
    blk = lax.broadcasted_iota(jnp.int32, (rows, tq), 0)
    cur = lax.shift_right_logical(i * tq + lax.broadcasted_iota(jnp.int32, (rows, tq), 1), SEL_SHIFT)
    imp_t = jnp.where((blk == 0) | (blk == cur) | (blk == cur - 1), FORCE, imp_t)
    imp_t = jnp.where(blk > cur, -1.0, imp_t)
    imp_t = jnp.where(blk >= n_sel, -2.0, imp_t)
    rank = jnp.zeros((rows, tq), F32)
    for j in range(n_sel):
        a = imp_t[j:j + 1, :]
        ahead = (a > imp_t) | ((a == imp_t) & (blk > j))
        rank = rank + jnp.where(ahead, 1.0, 0.0)
    neg_t = jnp.where((rank < top) & (blk < n_sel) & (blk <= cur), 0.0, MASK_NEG)
    if npad > rows:
        neg_t = jnp.concatenate([neg_t, jnp.zeros((npad - rows, tq), F32)], axis=0)
    neg = neg_t.T.astype(BF16)
    for h in range(hq):
        qr_scr[pl.ds(h * tq, tq), LANE:2 * LANE] = neg

    m_scr[...] = jnp.full(m_scr.shape, MASK_NEG, F32)
    l_scr[...] = jnp.zeros(l_scr.shape, F32)
    acc_scr[...] = jnp.zeros(acc_scr.shape, F32)
    n_chunks = (i * tq + tq + tc - 1) // tc

    def chunk(ci, diagonal):
        k0 = pl.multiple_of(ci * tc, tc)
        v = vsel_ref[0, pl.ds(k0, tc), :]
        kb = lax.shift_right_logical(k0 + lax.broadcasted_iota(jnp.int32, (tc, npad), 0), SEL_SHIFT)
        jb = lax.broadcasted_iota(jnp.int32, (tc, npad), 1)
        k_aug = jnp.concatenate([ksel_ref[0, pl.ds(k0, tc), :], jnp.where(kb == jb, 1.0, 0.0).astype(BF16)], axis=1)
        s_scr[:, 0:tc] = _nt(qr_scr[...], k_aug)
        if diagonal:
            kp = k0 + lax.broadcasted_iota(jnp.int32, (tq, tc), 1)
            tp = i * tq + lax.broadcasted_iota(jnp.int32, (tq, tc), 0)
            b_scr[:, 0:tc] = jnp.where(kp <= tp, 0.0, MASK_NEG)
        for r0 in range(0, hq * tq, sub):
            r = pl.ds(r0, sub)
            sc = s_scr[r, 0:tc]
            if diagonal:
                sc = sc + b_scr[pl.ds(r0 % tq, sub), 0:tc]
            parts = _lane_parts(sc)
            m_prev = m_scr[r, :]
            m_new = jnp.maximum(m_prev, jnp.max(functools.reduce(jnp.maximum, parts), axis=-1, keepdims=True))
            alpha = jnp.exp2(m_prev - m_new)
            pes = [jnp.exp2(x - m_new) for x in parts]
            l_scr[r, :] = alpha * l_scr[r, :] + functools.reduce(jnp.add, pes)
            p_scr[r, 0:tc] = jnp.concatenate(pes, axis=1).astype(BF16)
            a_scr[r, :] = alpha
            m_scr[r, :] = m_new
        acc_scr[...] = a_scr[...] * acc_scr[...] + _nn(p_scr[:, 0:tc], v)

    def full_chunk(ci, carry):
        chunk(ci, False)
        return carry

    lax.fori_loop(0, n_chunks - 1, full_chunk, 0)
    chunk(n_chunks - 1, True)

    gate = jax.nn.sigmoid(bg_ref[0])
    for g in range(1, NSA_KV):
        gate = jnp.where(pl.program_id(1) == g, pltpu.roll(gate, LANE - g * hq * N_BRANCH, 1), gate)
    ng = ng_ref[0]
    for h in range(hq):
        r = slice(h * tq, (h + 1) * tq)
        o_s = acc_scr[r, :] * (1.0 / jnp.sum(l_scr[r, :], axis=-1, keepdims=True))
        o = (gate[:, 3 * h:3 * h + 1] * oc_scr[r, :] + gate[:, 3 * h + 1:3 * h + 2] * o_s
             + gate[:, 3 * h + 2:3 * h + 3] * (ow_scr[r, :] * wl_scr[r, :]))
        o_ref[0, :, h * LANE:(h + 1) * LANE] = (ng[:, h * LANE:(h + 1) * LANE] * o).astype(o_ref.dtype)


def _nsa_prompt(qc3, qr3, ng3, bg3, kvb, kc, vc, *, tq=128, tc=512, sub=64, pool=None):
    b, t, _ = qc3.shape
    nq = t // tq
    tc = min(tc, t)
    tw = min(WINDOW + tq, t)
    n_sel = t // SEL_BLOCK
    top = min(TOP_N, n_sel)
    npad = kc.shape[2]
    gw = NSA_HPG * LANE
    rows = NSA_HPG * tq
    wide = max(tc, npad)
    kvspec = lambda k: pl.BlockSpec((1, t, LANE), lambda bi, gi, qi, *_, k=k: (bi, 0, k + gi))
    cspec = pl.BlockSpec((1, 1, npad, LANE), lambda bi, gi, qi, *_: (bi, gi, 0, 0))
    gspec = pl.BlockSpec((1, tq, gw), lambda bi, gi, qi, *_: (bi, qi, gi))
    in_specs = [gspec, gspec, gspec,
                pl.BlockSpec((1, tq, LANE), lambda bi, gi, qi, *_: (bi, qi, 0)),
                kvspec(0), kvspec(2), kvspec(4), kvspec(6), cspec, cspec]
    out_shape = [jax.ShapeDtypeStruct((b, t, NSA_HEADS * LANE), BF16)]
    out_specs = [pl.BlockSpec((1, tq, gw), lambda bi, gi, qi, *_: (bi, qi, gi))]
    assert tc % tq == 0 and npad == LANE
    scratch = ([pltpu.VMEM((rows, LANE), BF16), pltpu.VMEM((rows, 2 * LANE), BF16)]
               + [pltpu.VMEM((rows, wide), F32), pltpu.VMEM((rows, wide), BF16)]
               + [pltpu.VMEM((rows, LANE), F32)] * 5 + [pltpu.VMEM((tq, wide), F32)]
               + [pltpu.VMEM((rows, tw), F32), pltpu.VMEM((rows, tw), BF16), pltpu.VMEM((tq, tw), F32)]
               + [pltpu.VMEM((rows, LANE), F32)] * 2)
    args = [qc3, qr3, ng3, bg3, kvb, kvb, kvb, kvb, kc, vc]
    n_steps = b * NSA_KV * nq
    pool_pages, page_base, prefetch = 0, 0, []
    if pool is not None:
        cache4, page_flat, a3, page_base = pool
        page, n_cols = cache4.shape[1], 2 * NSA_KV
        if page_flat.shape[0] % n_steps == 0 and page == 2 * CMP_BLOCK and 2 * n_cols == 8:
            pool_pages = page_flat.shape[0] // n_steps
            prefetch = [page_flat]
            in_specs += [pl.BlockSpec((CMP_BLOCK, 2 * n_cols, LANE), lambda bi, gi, qi, *_: (0, 0, 0)),
                         pl.BlockSpec(memory_space=pl.ANY)]
            args += [jnp.concatenate([a3, a3], axis=1), cache4]
            out_shape.append(jax.ShapeDtypeStruct((n_steps, pool_pages, 2 * n_cols, LANE), F32))
            out_specs.append(pl.BlockSpec((1, pool_pages, 2 * n_cols, LANE),
                                          lambda bi, gi, qi, *_: ((bi * NSA_KV + gi) * nq + qi, 0, 0, 0)))
            scratch += [pltpu.VMEM((2, pool_pages, CMP_BLOCK, 2 * n_cols, LANE), F32), pltpu.SemaphoreType.DMA((2,))]
    kern = functools.partial(_nsa_prompt_kernel, t_len=t, tq=tq, tc=tc, tw=tw, n_sel=n_sel, top=top,
                             sub=min(sub, tq), pool_pages=pool_pages, page_base=page_base)
    res = pl.pallas_call(
        kern,
        out_shape=out_shape,
        grid_spec=pltpu.PrefetchScalarGridSpec(
            num_scalar_prefetch=len(prefetch), grid=(b, NSA_KV, nq),
            in_specs=in_specs, out_specs=out_specs, scratch_shapes=scratch),
        compiler_params=_params(("arbitrary", "arbitrary", "arbitrary")),
        name="nsa_prompt",
    )(*prefetch, *args)
    return (res[0], res[1]) if pool_pages else (res[0], None)


def _conv_kernel(h_ref, b_ref, c_ref, g_ref, prev_ref, w_ref, y_ref, st_ref, up_scr, *, t_real):
    u = c_ref[...] * h_ref[...]
    t = u.shape[1]
    up_scr[:, pl.ds(8 - (CONV_W - 1), CONV_W - 1), :] = prev_ref[...]
    up_scr[:, pl.ds(8, t), :] = u
    w = w_ref[...]
    y = w[0:1, :][None] * up_scr[:, pl.ds(6, t), :]
    y = y + w[1:2, :][None] * up_scr[:, pl.ds(7, t), :]
    y = y + w[2:3, :][None] * u
    y = b_ref[...] * y
    y_ref[...] = (_silu(g_ref[...]) * y).astype(y_ref.dtype)
    st_ref[...] = up_scr[:, pl.ds(6 + t_real, CONV_W - 1), :]


def _conv_mixer(z3, prev, w_conv, *, t_real):
    b, t, _ = z3.shape
    nc = CONV_DIM // LANE
    zspec = lambda off: pl.BlockSpec((b, t, LANE), lambda ci, off=off: (0, 0, off // LANE + ci))
    return pl.pallas_call(
        functools.partial(_conv_kernel, t_real=t_real),
        out_shape=[jax.ShapeDtypeStruct((b, t, CONV_DIM), BF16),
                   jax.ShapeDtypeStruct((b, CONV_W - 1, CONV_DIM), F32)],
        grid=(nc,),
        in_specs=[zspec(C_H), zspec(C_B), zspec(C_C), zspec(C_G),
                  pl.BlockSpec((b, CONV_W - 1, LANE), lambda ci: (0, 0, ci)),
                  pl.BlockSpec((CONV_W, LANE), lambda ci: (0, ci))],
        out_specs=[pl.BlockSpec((b, t, LANE), lambda ci: (0, 0, ci)),
                   pl.BlockSpec((b, CONV_W - 1, LANE), lambda ci: (0, 0, ci))],
        scratch_shapes=[pltpu.VMEM((b, t + 8, LANE), F32)],
        compiler_params=_params(("parallel",)),
        name="conv_mixer",
    )(z3, z3, z3, z3, prev, w_conv)


def _mem_attn_kernel(q_ref, mg_ref, kv_ref, o_ref, *, interleaved):
    q = q_ref[0]
    mg = mg_ref[0]
    half = MEM_HEADS * LANE
    for h in range(MEM_HEADS):
        lo, hi = h * LANE, (h + 1) * LANE
        if interleaved:
            nm = kv_ref.shape[1] // (2 * MEM_HEADS)
            k = kv_ref[0, pl.ds(h, nm, stride=2 * MEM_HEADS), :].astype(BF16)
            v = kv_ref[0, pl.ds(MEM_HEADS + h, nm, stride=2 * MEM_HEADS), :].astype(BF16)
        else:
            k = kv_ref[0, :, lo:hi].astype(BF16)
            v = kv_ref[0, :, half + lo:half + hi].astype(BF16)
        s = _nt((q[:, lo:hi] * ATTN_SCALE).astype(BF16), k)
        e = jnp.exp(s - jnp.max(s, axis=-1, keepdims=True))
        o = _nn(e.astype(BF16), v) / jnp.sum(e, axis=-1, keepdims=True)
        o_ref[0, :, lo:hi] = (_silu(mg[:, lo:hi]) * o).astype(o_ref.dtype)


def _mem_attn(zb3, mkv, *, mq_off, mg_off, tq, interleaved, kv_base=0):
    b, t, _ = zb3.shape
    wq = MEM_HEADS * LANE
    return pl.pallas_call(
        functools.partial(_mem_attn_kernel, interleaved=interleaved),
        out_shape=jax.ShapeDtypeStruct((b, t, wq), BF16),
        grid=(b, t // tq),
        in_specs=[pl.BlockSpec((1, tq, wq), lambda bi, ti: (bi, ti, mq_off // wq)),
                  pl.BlockSpec((1, tq, wq), lambda bi, ti: (bi, ti, mg_off // wq)),
                  pl.BlockSpec((1,) + mkv.shape[1:], lambda bi, ti: (bi + kv_base, 0, 0))],
        out_specs=pl.BlockSpec((1, tq, wq), lambda bi, ti: (bi, ti, 0)),
        compiler_params=_params(("parallel", "parallel")),
        name="mem_attn",
    )(zb3, zb3, mkv)


def _out_proj_kernel(x_ref, ya_ref, yb_ref, ym_ref, w_ref, fg_ref, o_ref, w_scr, *, final):
    @pl.when(pl.program_id(0) == 0)
    def _():
        w_scr[...] = w_ref[...].astype(BF16)

    a, bw = CONV_DIM, CONV_DIM + NSA_HEADS * LANE
    acc = _nn(ya_ref[...], w_scr[0:a, :])
    acc = acc + _nn(yb_ref[...], w_scr[a:bw, :])
    acc = acc + _nn(ym_ref[...], w_scr[bw:, :])
    r = x_ref[...] + acc
    if final:
        r = r * lax.rsqrt(jnp.mean(r * r, axis=-1, keepdims=True) + NORM_EPS) * fg_ref[...]
    o_ref[...] = r


def _out_proj(x, ya, yb, ym, w, fg, *, tm, final):
    m, d = x.shape
    row = lambda width: pl.BlockSpec((tm, width), lambda i: (i, 0))
    return pl.pallas_call(
        functools.partial(_out_proj_kernel, final=final),
        out_shape=jax.ShapeDtypeStruct((m, d), F32),
        grid=(m // tm,),
        in_specs=[row(d), row(ya.shape[1]), row(yb.shape[1]), row(ym.shape[1]),
                  pl.BlockSpec(w.shape, lambda i: (0, 0), pipeline_mode=pl.Buffered(1)),
                  pl.BlockSpec((1, d), lambda i: (0, 0), pipeline_mode=pl.Buffered(1))],
        out_specs=row(d),
        scratch_shapes=[pltpu.VMEM(w.shape, BF16)],
        compiler_params=_params(("arbitrary",)),
        name="out_proj",
    )(x, ya, yb, ym, w, fg.reshape(1, d))


def _sample_pre_kernel(q_ref, ng_ref, bg_ref, kv0_ref, kv1_ref, kv2_ref, c_ref, s1_ref, s2_ref, kc_ref, vc_ref,
                       kvn_ref, wn_ref, qr_ref, oc_ref, ngo_ref, gate_ref, val_ref, *, past):
    tp = SAMPLE_T_PAD
    hq = NSA_HPG
    c, s1, s2 = c_ref[...], s1_ref[...], s2_ref[...]
    kv0, kv1, kv2 = kv0_ref[0], kv1_ref[0], kv2_ref[0]
    kvn_ref[0, :, 0:512] = kv0
    kvn_ref[0, :, 768:1024] = kv1[:, 256:512]
    wn_ref[0, :, 256:512] = kv2[:, 256:512]
    for g in range(NSA_KV):
        lo, hi = g * LANE, (g + 1) * LANE
        kvn_ref[0, :, 512 + lo:512 + hi] = _rope(kv1[:, lo:hi], c, s1, s2)
        wn_ref[0, :, lo:hi] = _rope(kv2[:, lo:hi], c, s1, s2)

    q = q_ref[0]
    ng = ng_ref[0]
    gates = jax.nn.sigmoid(bg_ref[0])
    for g in range(NSA_KV):
        qc_l, qr_l = [], []
        for h in range(hq):
            lo = (g * hq + h) * LANE
            qh = q[:, lo:lo + LANE]
            qc_l.append(qh * ATTN_SCALE)
            qr_l.append(_rope(qh, c, s1, s2) * ATTN_SCALE)
            ngo_ref[0, g, h * tp:(h + 1) * tp, :] = ng[:, lo:lo + LANE]
            for br in range(N_BRANCH):
                col = (g * hq + h) * N_BRANCH + br
                gate_ref[0, g, br, h * tp:(h + 1) * tp, :] = jnp.broadcast_to(gates[:, col:col + 1], (tp, LANE))
        qc = jnp.concatenate(qc_l, axis=0)
        qr_ref[0, g] = jnp.concatenate(qr_l, axis=0)

        kc = kc_ref[0, g]
        npad = kc.shape[0]
        s = _nt(qc.astype(BF16), kc)
        trow = lax.broadcasted_iota(jnp.int32, (hq * tp, npad), 0) % tp
        ncol = lax.broadcasted_iota(jnp.int32, (hq * tp, npad), 1)
        cmask = (ncol + 1) * CMP_BLOCK <= past + trow + 1
        s = jnp.where(cmask, s, MASK_NEG)
        e = jnp.where(cmask, jnp.exp(s - jnp.max(s, axis=-1, keepdims=True)), 0.0)
        p = e / jnp.maximum(jnp.sum(e, axis=-1, keepdims=True), 1e-30)
        oc_ref[0, g] = _nn(p.astype(BF16), vc_ref[0, g])
        imp = jnp.sum(p.reshape(hq, tp, npad), axis=0)

        blk = lax.broadcasted_iota(jnp.int32, (tp, npad), 1)
        cur = (past + lax.broadcasted_iota(jnp.int32, (tp, npad), 0)) // SEL_BLOCK
        val = jnp.where((blk == 0) | (blk == cur) | (blk == cur - 1), FORCE, imp)
        val = jnp.where(blk > cur, -1.0, val)
        val = jnp.where(blk >= past // SEL_BLOCK, -2.0, val)
        val_ref[0, g] = val


def _topk_kernel(val_ref, idx_ref, *, n_top):
    val = val_ref[...]
    rows, n = val.shape
    blk = lax.broadcasted_iota(jnp.int32, (rows, n), 1)
    lane = lax.broadcasted_iota(jnp.int32, (rows, LANE), 1)
    idx = jnp.zeros((rows, LANE), jnp.int32)
    for r in range(n_top):
        best = jnp.max(val, axis=-1, keepdims=True)
        j = jnp.min(jnp.where(val == best, blk, n), axis=-1, keepdims=True)
        idx = jnp.where(lane == r, j, idx)
        val = jnp.where(blk == j, -3e38, val)
    idx_ref[...] = idx


def _topk(val2, *, n_top):
    rows, n = val2.shape
    return pl.pallas_call(
        functools.partial(_topk_kernel, n_top=n_top),
        out_shape=jax.ShapeDtypeStruct((rows, LANE), jnp.int32),
        grid=(1,),
        in_specs=[pl.BlockSpec((rows, n), lambda i: (0, 0))],
        out_specs=pl.BlockSpec((rows, LANE), lambda i: (0, 0)),
        compiler_params=_params(("arbitrary",)),
        name="sample_topk",
    )(val2)


def _sample_pre(za3, zb3, bg3, tabs, kc, vc, *, past):
    bs, tp, _ = za3.shape
    npad = kc.shape[2]
    qw = NSA_HEADS * LANE
    kvblk = KV_OFF // 512
    zspec = lambda k: pl.BlockSpec((1, tp, 512), lambda bi, k=k: (bi, 0, kvblk + k))
    tspec = pl.BlockSpec((tp, LANE), lambda bi: (0, 0))
    cspec = pl.BlockSpec((1, NSA_KV, npad, LANE), lambda bi: (bi, 0, 0, 0))
    rows = NSA_HPG * tp
    gspec = pl.BlockSpec((1, NSA_KV, rows, LANE), lambda bi: (bi, 0, 0, 0))
    gshape = jax.ShapeDtypeStruct((bs, NSA_KV, rows, LANE), F32)
    return pl.pallas_call(
        functools.partial(_sample_pre_kernel, past=past),
        out_shape=[jax.ShapeDtypeStruct((bs, tp, 1024), F32),
                   jax.ShapeDtypeStruct((bs, tp, 512), F32),
                   gshape, gshape, gshape,
                   jax.ShapeDtypeStruct((bs, NSA_KV, N_BRANCH, rows, LANE), F32),
                   jax.ShapeDtypeStruct((bs, NSA_KV, tp, npad), F32)],
        grid=(bs,),
        in_specs=[pl.BlockSpec((1, tp, qw), lambda bi: (bi, 0, Q_OFF // qw)),
                  pl.BlockSpec((1, tp, qw), lambda bi: (bi, 0, NG_OFF // qw)),
                  pl.BlockSpec((1, tp, LANE), lambda bi: (bi, 0, 0)),
                  zspec(0), zspec(1), zspec(2), tspec, tspec, tspec, cspec, cspec],
        out_specs=[pl.BlockSpec((1, tp, 1024), lambda bi: (bi, 0, 0)),
                   pl.BlockSpec((1, tp, 512), lambda bi: (bi, 0, 0)),
                   gspec, gspec, gspec,
                   pl.BlockSpec((1, NSA_KV, N_BRANCH, rows, LANE), lambda bi: (bi, 0, 0, 0, 0)),
                   pl.BlockSpec((1, NSA_KV, tp, npad), lambda bi: (bi, 0, 0, 0))],
        compiler_params=_params(("parallel",)),
        name="sample_pre",
    )(za3, za3, bg3, zb3, zb3, zb3, *tabs, kc, vc)


def _sample_attn_kernel(idx_sm, pt_sm, qr_ref, oc_ref, ng_ref, gate_ref, ksn_ref, vsn_ref,
                        wc_ref, kwn_ref, vwn_ref, cache_ref, o_ref,
                        kbuf, vbuf, kw_scr, vw_scr, sem, *, ts, n_top, n_pages, page_base, per_page, wb):
    tp = SAMPLE_T_PAD
    hq = NSA_HPG
    b = pl.program_id(0)
    g = pl.program_id(1)
    n_gath = n_top * SEL_BLOCK
    ks_rows = kbuf.shape[2]
    step = b * NSA_KV + g
    n_steps = pl.num_programs(0) * NSA_KV
    slot = step % 2

    def gather_copies(step_idx, t, r, to_slot):
        bb, gg = step_idx // NSA_KV, step_idx % NSA_KV
        blk = idx_sm[(step_idx * ts + t) * n_top + r]
        page = pt_sm[bb * n_pages + blk // per_page] + page_base
        row0 = (blk % per_page) * SEL_BLOCK
        src_k = cache_ref.at[page, pl.ds(row0, SEL_BLOCK), 2 * NSA_KV + gg]
        src_v = cache_ref.at[page, pl.ds(row0, SEL_BLOCK), 3 * NSA_KV + gg]
        dst = pl.ds(r * SEL_BLOCK, SEL_BLOCK)
        return (pltpu.make_async_copy(src_k, kbuf.at[to_slot, t, dst], sem.at[to_slot, 0]),
                pltpu.make_async_copy(src_v, vbuf.at[to_slot, t, dst], sem.at[to_slot, 1]))

    def start_gathers(step_idx, to_slot):
        for t in range(ts):
            for r in range(n_top):
                ck, cv = gather_copies(step_idx, t, r, to_slot)
                ck.start()
                cv.start()

    @pl.when(step == 0)
    def _():
        start_gathers(0, 0)

    @pl.when(step + 1 < n_steps)
    def _():
        start_gathers(step + 1, 1 - slot)

    qr = qr_ref[0, 0].astype(BF16)
    trow = lax.broadcasted_iota(jnp.int32, (hq * tp, 1), 0) % tp

    ww = kw_scr.shape[0]
    kw_scr[pl.ds(0, wb), :] = wc_ref[0, pl.ds(g, wb, stride=2 * NSA_KV), :]
    vw_scr[pl.ds(0, wb), :] = wc_ref[0, pl.ds(NSA_KV + g, wb, stride=2 * NSA_KV), :]
    kw_scr[pl.ds(wb, tp), :] = kwn_ref[0]
    vw_scr[pl.ds(wb, tp), :] = vwn_ref[0]
    kw_scr[pl.ds(wb + tp, ww - wb - tp), :] = jnp.zeros((ww - wb - tp, LANE), F32)
    vw_scr[pl.ds(wb + tp, ww - wb - tp), :] = jnp.zeros((ww - wb - tp, LANE), F32)
    sw = _nt(qr, kw_scr[...].astype(BF16))
    jw = lax.broadcasted_iota(jnp.int32, (hq * tp, ww), 1)
    rel = jw - wb
    okw = (rel <= trow) & (rel > trow - WINDOW) & (jw < wb + ts)
    sw = jnp.where(okw, sw, MASK_NEG)
    ew = jnp.where(okw, jnp.exp(sw - jnp.max(sw, axis=-1, keepdims=True)), 0.0)
    o_w = _nn(ew.astype(BF16), vw_scr[...].astype(BF16)) / jnp.sum(ew, axis=-1, keepdims=True)

    for t in range(ts):
        for r in range(n_top):
            ck, cv = gather_copies(step, t, r, slot)
            ck.wait()
            cv.wait()

    js = lax.broadcasted_iota(jnp.int32, (hq * tp, ks_rows), 1)
    o_s = jnp.zeros((hq * tp, LANE), F32)
    for t in range(ts):
        kbuf[slot, t, pl.ds(n_gath, tp), :] = ksn_ref[0]
        vbuf[slot, t, pl.ds(n_gath, tp), :] = vsn_ref[0]
        kbuf[slot, t, pl.ds(n_gath + tp, ks_rows - n_gath - tp), :] = jnp.zeros((ks_rows - n_gath - tp, LANE), F32)
        vbuf[slot, t, pl.ds(n_gath + tp, ks_rows - n_gath - tp), :] = jnp.zeros((ks_rows - n_gath - tp, LANE), F32)
        ss = _nt(qr, kbuf[slot, t].astype(BF16))
        oks = (js < n_gath) | ((js - n_gath <= t) & (js < n_gath + ts))
        ss = jnp.where(oks, ss, MASK_NEG)
        es = jnp.where(oks, jnp.exp(ss - jnp.max(ss, axis=-1, keepdims=True)), 0.0)
        ot = _nn(es.astype(BF16), vbuf[slot, t].astype(BF16)) / jnp.sum(es, axis=-1, keepdims=True)
        o_s = jnp.where(trow == t, ot, o_s)

    o = gate_ref[0, 0, 0] * oc_ref[0, 0] + gate_ref[0, 0, 1] * o_s + gate_ref[0, 0, 2] * o_w
    y = _silu(ng_ref[0, 0]) * o
    for h in range(hq):
        o_ref[0, :, h * LANE:(h + 1) * LANE] = y[h * tp:(h + 1) * tp].astype(o_ref.dtype)


def _sample_attn(idx_flat, page_flat, qr, oc, ngo, gates, kvn, cache_win_rows, wn, cache4, *,
                 ts, n_top, n_pages, page_base, win_base):
    bs = qr.shape[0]
    tp = SAMPLE_T_PAD
    rows = NSA_HPG * tp
    wb = cache_win_rows.shape[1] // (2 * NSA_KV)
    per_page = cache4.shape[1] // SEL_BLOCK
    ks_rows = -(-(n_top * SEL_BLOCK + tp) // LANE) * LANE
    ww = -(-(wb + tp) // LANE) * LANE
    gspec = pl.BlockSpec((1, 1, rows, LANE), lambda bi, gi, *_: (bi, gi, 0, 0))
    newspec = lambda k: pl.BlockSpec((1, tp, LANE), lambda bi, gi, *_, k=k: (bi, 0, k + gi))
    grid_spec = pltpu.PrefetchScalarGridSpec(
        num_scalar_prefetch=2, grid=(bs, NSA_KV),
        in_specs=[gspec, gspec, gspec,
                  pl.BlockSpec((1, 1, N_BRANCH, rows, LANE), lambda bi, gi, *_: (bi, gi, 0, 0, 0)),
                  newspec(2 * NSA_KV), newspec(3 * NSA_KV),
                  pl.BlockSpec((1,) + cache_win_rows.shape[1:], lambda bi, gi, *_: (bi + win_base, 0, 0)),
                  newspec(0), newspec(NSA_KV),
                  pl.BlockSpec(memory_space=pl.ANY)],
        out_specs=pl.BlockSpec((1, tp, NSA_HPG * LANE), lambda bi, gi, *_: (bi, 0, gi)),
        scratch_shapes=[pltpu.VMEM((2, ts, ks_rows, LANE), F32), pltpu.VMEM((2, ts, ks_rows, LANE), F32),
                        pltpu.VMEM((ww, LANE), F32), pltpu.VMEM((ww, LANE), F32),
                        pltpu.SemaphoreType.DMA((2, 2))])
    kern = functools.partial(_sample_attn_kernel, ts=ts, n_top=n_top, n_pages=n_pages,
                             page_base=page_base, per_page=per_page, wb=wb)
    return pl.pallas_call(
        kern,
        out_shape=jax.ShapeDtypeStruct((bs, tp, NSA_HEADS * LANE), BF16),
        grid_spec=grid_spec,
        compiler_params=_params(("arbitrary", "arbitrary")),
        name="sample_attn",
    )(idx_flat, page_flat, qr, oc, ngo, gates, kvn, kvn, cache_win_rows, wn, wn, cache4)


def _rope_tables(pos, rows):
    freqs = jnp.power(ROPE_THETA, -jnp.arange(ROPE_HALF, dtype=F32) * (2.0 / ROPE_DIM))
    ang = pos.astype(F32)[:, None] * freqs[None, :]
    cos, sin = jnp.cos(ang), jnp.sin(ang)
    n = pos.shape[0]
    z16 = jnp.zeros((n, ROPE_HALF), F32)
    rest = LANE - ROPE_DIM
    c = jnp.concatenate([cos, cos, jnp.ones((n, rest), F32)], axis=1)
    s1 = jnp.concatenate([z16, sin, jnp.zeros((n, rest), F32)], axis=1)
    s2 = jnp.concatenate([-sin, z16, jnp.zeros((n, rest), F32)], axis=1)
    pad = lambda a: jnp.pad(a, ((0, rows - n), (0, 0)))
    return pad(c), pad(s1), pad(s2)


def _layer_weights(norm_g, w_in3, layer, w_conv, a_cmp, w_cmp, w_out):
    w_a, w_b, wbg = _pack_in_proj_weights(jnp.swapaxes(w_in3, 1, 2), layer)
    a4 = jnp.concatenate([a_cmp[0], a_cmp[0], a_cmp[1], a_cmp[1]], axis=1)
    a3 = jnp.stack([a_cmp[0], a_cmp[0], a_cmp[1], a_cmp[1]], axis=1)
    w4 = jnp.stack([w_cmp[0], w_cmp[0], w_cmp[1], w_cmp[1]]).astype(BF16)
    return dict(norm_g=norm_g, w_a=w_a, w_b=w_b, wbg=wbg, w_conv=w_conv, a4=a4, a3=a3, w4=w4,
                w_out=w_out)


def _prompt_layer(xp, xs2, mem_prompt, mem_norm_g, w_mem, lw, final_g, final, pool):
    b, t, d = xp.shape
    m = b * t
    x2 = xp.reshape(m, d)
    tabs = _rope_tables(jnp.arange(t, dtype=jnp.int32), t)
    wb = min(WINDOW, t)
    qc, qr, ng, bg, ya, conv_new, za_s, bg_s = _proj_conv(x2, lw["norm_g"], lw["w_a"], lw["wbg"], lw["w_conv"], tabs,
                                                          xs2, tm=min(512, t), seq_len=t)
    kvn, kvb, win_rows, pooled_prompt, zm, zb_s = _proj_kv(x2, lw["norm_g"], lw["w_b"], tabs, lw["a4"], xs2,
                                                           tm=wb, seq_len=t)
    nm = mem_prompt.shape[1]
    mkv = _norm_matmul(mem_prompt.reshape(b * nm, d), mem_norm_g, w_mem, tm=min(512, b * nm), tn=512)
    mkv3 = mkv.reshape(b, nm, 2 * MEM_HEADS * LANE)
    kc, vc = _cmp_proj(pooled_prompt.reshape(b, t // CMP_BLOCK, -1), lw["w4"])
    r3 = lambda a: a.reshape(b, t, -1)
    yb, pooled = _nsa_prompt(r3(qc), r3(qr), r3(ng), r3(bg), r3(kvb), kc, vc, tq=min(256, t), pool=pool)
    ym = _mem_attn(zm.reshape(b, t, -1), mkv3, mq_off=0, mg_off=MG_OFF - MQ_OFF, tq=min(1024, t), interleaved=False)
    out = _out_proj(x2, ya, yb.reshape(m, -1), ym.reshape(m, -1),
                    lw["w_out"], final_g, tm=min(512, m), final=final)
    kv_new = kvn.reshape(b, t, 4, NSA_KV, HEAD_DIM)
    win_new = win_rows.reshape(b, wb, 2, NSA_KV, HEAD_DIM)
    mem_kv = mkv.reshape(b, nm, 2, MEM_HEADS, HEAD_DIM)
    return out.reshape(b, t, d), kv_new, win_new, conv_new, mem_kv, pooled, (za_s, zb_s, bg_s)


def _sample_layer(xs_p, proj, ts, layer, cache4, page_flat, pooled, cache_win, state_conv, cache_mem, lw, final_g,
                  final):
    bs, tp, d = xs_p.shape
    depth = cache_win.shape[0]
    pool, page = cache4.shape[0] // depth, cache4.shape[1]
    n_pages = page_flat.shape[0] // bs
    past = n_pages * page
    assert past % SEL_BLOCK == 0 and ts <= SEL_BLOCK and ts <= tp
    n_past = past // SEL_BLOCK
    n_top = min(TOP_N, n_past + 1) - 1
    m = bs * tp
    za, zb, bg = proj
    za3 = za.reshape(bs, tp, ZA_WIDTH)
    zb3 = zb.reshape(bs, tp, ZB_WIDTH)
    bg3 = bg.reshape(bs, tp, LANE)
    tabs = _rope_tables(past + jnp.arange(tp, dtype=jnp.int32), tp)
    if pooled is None:
        pooled = _pool_pages(cache4, page_flat, lw["a3"], bs=bs, n_pages=n_pages, page_base=layer * pool)
    else:
        pooled = pooled.reshape(bs, n_pages * (page // CMP_BLOCK), 2 * NSA_KV * LANE)
    kc, vc = _cmp_proj(pooled, lw["w4"])
    kvn, wn, qr, oc, ngo, gates, val = _sample_pre(za3, zb3, bg3, tabs, kc, vc, past=past)
    idx = _topk(val.reshape(bs * NSA_KV * tp, val.shape[-1]), n_top=n_top)
    idx_flat = idx.reshape(bs, NSA_KV, tp, LANE)[:, :, :ts, :n_top].reshape(-1)
    wbuf = cache_win.shape[2]
    cache_win_rows = cache_win.reshape(depth * bs, wbuf * 2 * NSA_KV, HEAD_DIM)
    yb = _sample_attn(idx_flat, page_flat, qr, oc, ngo, gates, kvn, cache_win_rows, wn, cache4,
                      ts=ts, n_top=n_top, n_pages=n_pages, page_base=layer * pool, win_base=layer * bs)
    ya, conv_new = _conv_mixer(za3, state_conv[layer], lw["w_conv"], t_real=ts)
    nm = cache_mem.shape[2]
    mem_rows = cache_mem.reshape(depth * bs, nm * 2 * MEM_HEADS, HEAD_DIM)
    ym = _mem_attn(zb3, mem_rows, mq_off=MQ_OFF, mg_off=MG_OFF, tq=tp, interleaved=True, kv_base=layer * bs)
    out = _out_proj(xs_p.reshape(m, d), ya.reshape(m, -1), yb.reshape(m, -1), ym.reshape(m, -1),
                    lw["w_out"], final_g, tm=m, final=final)
    kv_new = kvn[:, :ts].reshape(bs, ts, 4, NSA_KV, HEAD_DIM)
    win_rows = wn[:, :ts].reshape(bs, ts, 2, NSA_KV, HEAD_DIM)
    win_state = jnp.concatenate([cache_win[layer], win_rows], axis=1)[:, ts:]
    return out.reshape(bs, tp, d), kv_new, win_state, conv_new


def kernel(x_prompt, x_sample, cache_kv, cache_win, state_conv, cache_mem, page_table, mem_prompt,
           norm_g, w_in, w_conv, a_cmp, w_cmp, mem_norm_g, w_mem_kv, w_out, final_g):
    depth = w_in.shape[0]
    ts = x_sample.shape[1]
    xp = x_prompt
    xs = jnp.pad(x_sample, ((0, 0), (0, SAMPLE_T_PAD - ts), (0, 0)))
    pool_size, page = cache_kv.shape[1], cache_kv.shape[2]
    cache4 = cache_kv.reshape(depth * pool_size, page, 4 * NSA_KV, HEAD_DIM)
    page_flat = page_table.reshape(-1).astype(jnp.int32)
    kv_p, win_p, conv_p, mem_p, kv_s, win_s, conv_s = [], [], [], [], [], [], []
    for l in range(depth):
        lw = _layer_weights(norm_g[l], w_in, l, w_conv[l], a_cmp[l], w_cmp[l], w_out[l])
        final = l == depth - 1
        xp, kvn, winn, convn, mkv, pooled, proj_s = _prompt_layer(
            xp, xs.reshape(-1, xs.shape[-1]), mem_prompt, mem_norm_g[l], w_mem_kv[l], lw, final_g, final,
            pool=(cache4, page_flat, lw["a3"], l * pool_size))
        kv_p.append(kvn)
        win_p.append(winn)
        conv_p.append(convn)
        mem_p.append(mkv)
        xs, kvn, winn, convn = _sample_layer(xs, proj_s, ts, l, cache4, page_flat, pooled, cache_win, state_conv,
                                             cache_mem, lw, final_g, final)
        kv_s.append(kvn)
        win_s.append(winn)
        conv_s.append(convn)
    return (xp, xs[:, :ts], jnp.stack(kv_p), jnp.stack(win_p), jnp.stack(conv_p), jnp.stack(mem_p),
            jnp.stack(kv_s), jnp.stack(win_s), jnp.stack(conv_s))
```

```python
import functools

import jax
import jax.numpy as jnp
from jax import lax
from jax.experimental import pallas as pl
from jax.experimental.pallas import tpu as pltpu

F32 = jnp.float32
BF16 = jnp.bfloat16

HEAD_DIM = 128
CONV_DIM = 512
CONV_W = 3
NSA_HEADS = 8
NSA_KV = 2
NSA_HPG = NSA_HEADS // NSA_KV
MEM_HEADS = 4
N_BRANCH = 3
ROPE_DIM = HEAD_DIM // 4
ROPE_HALF = ROPE_DIM // 2
ROPE_THETA = 500000.0
CMP_BLOCK = 64
SEL_BLOCK = 64
SEL_SHIFT = 6
TOP_N = 16
WINDOW = 512
NORM_EPS = 1e-6
MASK_NEG = -1e30
FORCE = 1e9
ATTN_SCALE = HEAD_DIM ** -0.5
SCALE_LOG2 = ATTN_SCALE * 1.4426950408889634

C_H, C_B, C_C, C_G, Q_OFF, NG_OFF = 0, 512, 1024, 1536, 2048, 3072
ZA_WIDTH = 4096
KV_OFF, MQ_OFF, MG_OFF = 0, 1536, 2048
ZB_WIDTH = 2560
BG_SRC = 4096
BG_N = NSA_HEADS * N_BRANCH
LANE = 128
SAMPLE_T_PAD = 8
VMEM_LIMIT = 56 * 1024 * 1024
W_CHUNK = 512


def _nt(a, b):
    return lax.dot_general(a, b, (((1,), (1,)), ((), ())), preferred_element_type=F32)


def _nn(a, b):
    return jnp.dot(a, b, preferred_element_type=F32)


def _params(sem, vmem=VMEM_LIMIT):
    return pltpu.CompilerParams(dimension_semantics=sem, vmem_limit_bytes=vmem)


def _rope(x, c, s1, s2):
    return x * c + pltpu.roll(x, ROPE_HALF, 1) * s1 + pltpu.roll(x, LANE - ROPE_HALF, 1) * s2


def _silu(x):
    return x * jax.nn.sigmoid(x)


def _rms_rows(x_ref, g_ref):
    x = x_ref[...]
    return (x * lax.rsqrt(jnp.mean(x * x, axis=-1, keepdims=True) + NORM_EPS) * g_ref[...]).astype(BF16)


def _norm_matmul_kernel(x_ref, g_ref, w_ref, z_ref, h_scr, *, tn):
    h_scr[...] = _rms_rows(x_ref, g_ref)
    for j in range(w_ref.shape[1] // tn):
        z_ref[:, j * tn:(j + 1) * tn] = _nn(h_scr[...], w_ref[:, j * tn:(j + 1) * tn].astype(BF16))


def _norm_matmul(x, g, w, *, tm, tn):
    m, d = x.shape
    n = w.shape[1]
    assert n % tn == 0 and m % tm == 0 and w.shape[0] == d
    resident = lambda shape: pl.BlockSpec(shape, lambda i: (0, 0), pipeline_mode=pl.Buffered(1))
    return pl.pallas_call(
        functools.partial(_norm_matmul_kernel, tn=tn),
        out_shape=jax.ShapeDtypeStruct((m, n), F32),
        grid=(m // tm,),
        in_specs=[pl.BlockSpec((tm, d), lambda i: (i, 0)), resident((1, d)), resident((d, n))],
        out_specs=pl.BlockSpec((tm, n), lambda i: (i, 0)),
        scratch_shapes=[pltpu.VMEM((tm, d), BF16)],
        compiler_params=_params(("parallel",)),
        name="norm_matmul",
    )(x, g.reshape(1, d), w)


def _fetch_weight_rows(w_hbm, layer, first_row, w_scr, stage, sem):
    ch = stage.shape[1]
    n_chunks = w_scr.shape[0] // ch
    assert n_chunks * ch == w_scr.shape[0]

    def chunk_copy(c):
        return pltpu.make_async_copy(w_hbm.at[layer, pl.ds(first_row + c * ch, ch), :], stage.at[c % 2], sem.at[c % 2])

    chunk_copy(0).start()
    for c in range(n_chunks):
        if c + 1 < n_chunks:
            chunk_copy(c + 1).start()
        chunk_copy(c).wait()
        w_scr[c * ch:(c + 1) * ch, :] = stage[c % 2].astype(BF16)


def _fetch_gate_rows(w_hbm, layer, wbg_scr, stage, sem):
    cp = pltpu.make_async_copy(w_hbm.at[layer, pl.ds(BG_SRC, BG_N), :], stage.at[0, pl.ds(0, BG_N)], sem.at[0])
    cp.start()
    cp.wait()
    wbg_scr[...] = jnp.zeros(wbg_scr.shape, BF16)
    wbg_scr[0:BG_N, :] = stage[0, 0:BG_N, :].astype(BF16)


def _proj_conv_kernel(x_ref, g_ref, w_hbm, wc_ref, c_ref, s1_ref, s2_ref, xs_ref,
                      qc_ref, qr_ref, ng_ref, bg_ref, ya_ref, st_ref, zs_ref, bgs_ref,
                      h_scr, up_scr, w_ref, wbg_ref, stage, sem, *, blocks_per_seq, layer):
    tm = x_ref.shape[0]
    cw = CONV_DIM
    i = pl.program_id(0)
    first = i % blocks_per_seq == 0
    @pl.when(i == 0)
    def _():
        _fetch_weight_rows(w_hbm, layer, 0, w_ref, stage, sem)
        _fetch_gate_rows(w_hbm, layer, wbg_ref, stage, sem)
        up_scr[...] = jnp.zeros(up_scr.shape, F32)

    h_scr[...] = _rms_rows(x_ref, g_ref)
    bg_ref[...] = _nt(h_scr[...], wbg_ref[...])
    chunk = lambda off: _nt(h_scr[...], w_ref[off:off + cw, :])


    carry = up_scr[pl.ds(8 + tm - (CONV_W - 1), CONV_W - 1), :]
    up_scr[pl.ds(8 - (CONV_W - 1), CONV_W - 1), :] = jnp.where(first, 0.0, carry)
    u = chunk(C_C) * chunk(C_H)
    up_scr[pl.ds(8, tm), :] = u
    wc = wc_ref[...]
    y = wc[0:1, :] * up_scr[pl.ds(6, tm), :]
    y = y + wc[1:2, :] * up_scr[pl.ds(7, tm), :]
    y = y + wc[2:3, :] * u
    y = chunk(C_B) * y
    ya_ref[...] = (_silu(chunk(C_G)) * y).astype(ya_ref.dtype)

    st_ref[0] = up_scr[pl.ds(8 + tm - (CONV_W - 1), CONV_W - 1), :]

    c, s1, s2 = c_ref[...], s1_ref[...], s2_ref[...]
    for j in range((NG_OFF - Q_OFF) // cw):
        qv = chunk(Q_OFF + j * cw)
        for k in range(cw // LANE):
            lo = j * cw + k * LANE
            qh = qv[:, k * LANE:(k + 1) * LANE]
            qc_ref[:, lo:lo + LANE] = (qh * SCALE_LOG2).astype(BF16)
            qr_ref[:, lo:lo + LANE] = (_rope(qh, c, s1, s2) * SCALE_LOG2).astype(BF16)
    for j in range((ZA_WIDTH - NG_OFF) // cw):
        ng_ref[:, j * cw:(j + 1) * cw] = _silu(chunk(NG_OFF + j * cw))

    @pl.when(i == pl.num_programs(0) - 1)
    def _():
        hs = _rms_rows(xs_ref, g_ref)
        bgs_ref[...] = _nt(hs, wbg_ref[...])
        for j in range(ZA_WIDTH // cw):
            zs_ref[:, j * cw:(j + 1) * cw] = _nt(hs, w_ref[j * cw:(j + 1) * cw, :])


def _proj_conv(x2, g, w_t3, layer, w_conv, tabs, xs2, *, tm, seq_len):
    m, d = x2.shape
    ms = xs2.shape[0]
    assert seq_len % tm == 0 and CONV_W == 3
    blocks_per_seq = seq_len // tm
    nq, nng = NG_OFF - Q_OFF, ZA_WIDTH - NG_OFF
    resident = lambda shape: pl.BlockSpec(shape, lambda i: (0,) * len(shape), pipeline_mode=pl.Buffered(1))
    row = lambda width: pl.BlockSpec((tm, width), lambda i: (i, 0))
    tspec = pl.BlockSpec((tm, LANE), lambda i: (i % blocks_per_seq, 0))
    return pl.pallas_call(
        functools.partial(_proj_conv_kernel, blocks_per_seq=blocks_per_seq, layer=layer),
        out_shape=[jax.ShapeDtypeStruct((m, nq), BF16), jax.ShapeDtypeStruct((m, nq), BF16),
                   jax.ShapeDtypeStruct((m, nng), F32), jax.ShapeDtypeStruct((m, LANE), F32),
                   jax.ShapeDtypeStruct((m, CONV_DIM), BF16),
                   jax.ShapeDtypeStruct((m // seq_len, CONV_W - 1, CONV_DIM), F32),
                   jax.ShapeDtypeStruct((ms, ZA_WIDTH), F32), jax.ShapeDtypeStruct((ms, LANE), F32)],
        grid=(m // tm,),
        in_specs=[row(d), resident((1, d)), pl.BlockSpec(memory_space=pl.ANY), resident(w_conv.shape),
                  tspec, tspec, tspec, resident((ms, d))],
        out_specs=[row(nq), row(nq), row(nng), row(LANE), row(CONV_DIM),
                   pl.BlockSpec((1, CONV_W - 1, CONV_DIM), lambda i: (i // blocks_per_seq, 0, 0)),
                   pl.BlockSpec((ms, ZA_WIDTH), lambda i: (0, 0)), pl.BlockSpec((ms, LANE), lambda i: (0, 0))],
        scratch_shapes=[pltpu.VMEM((tm, d), BF16), pltpu.VMEM((tm + 8, CONV_DIM), F32),
                        pltpu.VMEM((ZA_WIDTH, d), BF16), pltpu.VMEM((LANE, d), BF16),
                        pltpu.VMEM((2, W_CHUNK, d), F32), pltpu.SemaphoreType.DMA((2,))],
        compiler_params=_params(("arbitrary",)),
        name="proj_conv",
    )(x2, g.reshape(1, d), w_t3, w_conv, *tabs, xs2)


def _proj_kv_kernel(x_ref, g_ref, w_hbm, c_ref, s1_ref, s2_ref, a_ref, xs_ref,
                    kvn_ref, kvb_ref, win_ref, pool_ref, zm_ref, zs_ref, h_scr, w_ref, stage, sem,
                    *, blocks_per_seq, layer):
    tm = x_ref.shape[0]
    cw = 2 * NSA_KV * LANE

    @pl.when(pl.program_id(0) == 0)
    def _():
        _fetch_weight_rows(w_hbm, layer, BG_SRC + BG_N, w_ref, stage, sem)

    h_scr[...] = _rms_rows(x_ref, g_ref)
    chunk = lambda off: _nt(h_scr[...], w_ref[off:off + cw, :])
    c, s1, s2 = c_ref[...], s1_ref[...], s2_ref[...]
    kv0 = chunk(KV_OFF)
    kv1 = chunk(KV_OFF + cw)
    kv2 = chunk(KV_OFF + 2 * cw)
    n_kv, n_w, half = 4 * NSA_KV, 2 * NSA_KV, NSA_KV * LANE
    for g in range(NSA_KV):
        lo, hi = g * LANE, (g + 1) * LANE
        ks = _rope(kv1[:, lo:hi], c, s1, s2)
        kw = _rope(kv2[:, lo:hi], c, s1, s2)
        vs = kv1[:, half + lo:half + hi]
        vw = kv2[:, half + lo:half + hi]
        kvn_ref[pl.ds(g, tm, stride=n_kv), :] = kv0[:, lo:hi]
        kvn_ref[pl.ds(NSA_KV + g, tm, stride=n_kv), :] = kv0[:, half + lo:half + hi]
        kvn_ref[pl.ds(2 * NSA_KV + g, tm, stride=n_kv), :] = ks
        kvn_ref[pl.ds(3 * NSA_KV + g, tm, stride=n_kv), :] = vs
        kvb_ref[:, lo:hi] = ks.astype(BF16)
        kvb_ref[:, half + lo:half + hi] = vs.astype(BF16)
        kvb_ref[:, 2 * half + lo:2 * half + hi] = kw.astype(BF16)
        kvb_ref[:, 3 * half + lo:3 * half + hi] = vw.astype(BF16)
        win_ref[pl.ds(g, tm, stride=n_w), :] = kw
        win_ref[pl.ds(NSA_KV + g, tm, stride=n_w), :] = vw

    pool_ref[...] = jnp.sum(kv0.reshape(tm // CMP_BLOCK, CMP_BLOCK, cw) * a_ref[...][None], axis=1)
    for j in range(zm_ref.shape[1] // cw):
        zm_ref[:, j * cw:(j + 1) * cw] = chunk(MQ_OFF + j * cw)

    @pl.when(pl.program_id(0) == pl.num_programs(0) - 1)
    def _():
        hs = _rms_rows(xs_ref, g_ref)
        for j in range(ZB_WIDTH // cw):
            zs_ref[:, j * cw:(j + 1) * cw] = _nt(hs, w_ref[j * cw:(j + 1) * cw, :])


def _proj_kv(x2, g, w_t3, layer, tabs, a4, xs2, *, tm, seq_len):
    m, d = x2.shape
    ms = xs2.shape[0]
    assert seq_len % tm == 0
    blocks_per_seq = seq_len // tm
    n_kv, n_w = 4 * NSA_KV, 2 * NSA_KV
    nm = ZB_WIDTH - MQ_OFF
    resident = lambda shape: pl.BlockSpec(shape, lambda i: (0,) * len(shape), pipeline_mode=pl.Buffered(1))
    row = lambda width: pl.BlockSpec((tm, width), lambda i: (i, 0))
    tspec = pl.BlockSpec((tm, LANE), lambda i: (i % blocks_per_seq, 0))
    return pl.pallas_call(
        functools.partial(_proj_kv_kernel, blocks_per_seq=blocks_per_seq, layer=layer),
        out_shape=[jax.ShapeDtypeStruct((m * n_kv, LANE), F32),
                   jax.ShapeDtypeStruct((m, n_kv * LANE), BF16),
                   jax.ShapeDtypeStruct((m // seq_len * tm * n_w, LANE), F32),
                   jax.ShapeDtypeStruct((m // CMP_BLOCK, 2 * NSA_KV * LANE), F32),
                   jax.ShapeDtypeStruct((m, nm), F32), jax.ShapeDtypeStruct((ms, ZB_WIDTH), F32)],
        grid=(m // tm,),
        in_specs=[row(d), resident((1, d)), pl.BlockSpec(memory_space=pl.ANY), tspec, tspec, tspec,
                  resident(a4.shape), resident((ms, d))],
        out_specs=[pl.BlockSpec((tm * n_kv, LANE), lambda i: (i, 0)), row(n_kv * LANE),
                   pl.BlockSpec((tm * n_w, LANE), lambda i: (i // blocks_per_seq, 0)),
                   pl.BlockSpec((tm // CMP_BLOCK, 2 * NSA_KV * LANE), lambda i: (i, 0)), row(nm),
                   pl.BlockSpec((ms, ZB_WIDTH), lambda i: (0, 0))],
        scratch_shapes=[pltpu.VMEM((tm, d), BF16), pltpu.VMEM((ZB_WIDTH, d), BF16),
                        pltpu.VMEM((2, W_CHUNK, d), F32), pltpu.SemaphoreType.DMA((2,))],
        compiler_params=_params(("arbitrary",)),
        name="proj_kv",
    )(x2, g.reshape(1, d), w_t3, *tabs, a4, xs2)


def _pool_pages_kernel(pt_sm, a_ref, cache_ref, o_ref, buf, sem, *, pages_per_step, page_base):
    step = pl.program_id(0)
    n_steps = pl.num_programs(0)
    slot = step % 2
    n_cols = buf.shape[3]

    def page_copy(step_idx, p, to_slot):
        page = pt_sm[step_idx * pages_per_step + p] + page_base
        return pltpu.make_async_copy(cache_ref.at[page, :, pl.ds(0, n_cols), :], buf.at[to_slot, p], sem.at[to_slot])

    @pl.when(step == 0)
    def _():
        for p in range(pages_per_step):
            page_copy(0, p, 0).start()

    @pl.when(step + 1 < n_steps)
    def _():
        for p in range(pages_per_step):
            page_copy(step + 1, p, 1 - slot).start()

    for p in range(pages_per_step):
        page_copy(step, p, slot).wait()

    a = a_ref[...]
    per = buf.shape[2] // CMP_BLOCK
    for p in range(pages_per_step):
        for k in range(per):
            x = buf[slot, p, pl.ds(k * CMP_BLOCK, CMP_BLOCK)]
            o_ref[0, p, k] = jnp.sum(x * a, axis=0)


def _pool_pages(cache4, page_flat, a3, *, bs, n_pages, page_base, pages_per_step=16):
    page = cache4.shape[1]
    per = page // CMP_BLOCK
    n_cols = 2 * NSA_KV
    total = bs * n_pages
    pages_per_step = min(pages_per_step, total)
    assert total % pages_per_step == 0
    n_steps = total // pages_per_step
    grid_spec = pltpu.PrefetchScalarGridSpec(
        num_scalar_prefetch=1, grid=(n_steps,),
        in_specs=[pl.BlockSpec((CMP_BLOCK, n_cols, LANE), lambda si, pt: (0, 0, 0)),
                  pl.BlockSpec(memory_space=pl.ANY)],
        out_specs=pl.BlockSpec((1, pages_per_step, per, n_cols, LANE), lambda si, pt: (si, 0, 0, 0, 0)),
        scratch_shapes=[pltpu.VMEM((2, pages_per_step, page, n_cols, LANE), F32),
                        pltpu.SemaphoreType.DMA((2,))])
    out = pl.pallas_call(
        functools.partial(_pool_pages_kernel, pages_per_step=pages_per_step, page_base=page_base),
        out_shape=jax.ShapeDtypeStruct((n_steps, pages_per_step, per, n_cols, LANE), F32),
        grid_spec=grid_spec,
        compiler_params=_params(("arbitrary",)),
        name="pool_pages",
    )(page_flat, a3, cache4)
    return out.reshape(bs, n_pages * per, n_cols * LANE)


def _cmp_proj_kernel(p_ref, w_ref, kc_ref, vc_ref):
    pooled = p_ref[0]
    n = pooled.shape[0]
    n_pad = kc_ref.shape[2]
    for c in range(4):
        r = _nn(pooled[:, c * LANE:(c + 1) * LANE].astype(BF16), w_ref[c]).astype(BF16)
        dst = kc_ref if c < 2 else vc_ref
        if n_pad > n:
            dst[0, c % 2] = jnp.zeros((n_pad, LANE), BF16)
        dst[0, c % 2, 0:n, :] = r


def _cmp_proj(pooled, w4):
    b, n, _ = pooled.shape
    n_pad = -(-n // LANE) * LANE
    spec = pl.BlockSpec((1, NSA_KV, n_pad, LANE), lambda bi: (bi, 0, 0, 0))
    return pl.pallas_call(
        _cmp_proj_kernel,
        out_shape=[jax.ShapeDtypeStruct((b, NSA_KV, n_pad, LANE), BF16)] * 2,
        grid=(b,),
        in_specs=[pl.BlockSpec((1, n, 512), lambda bi: (bi, 0, 0)),
                  pl.BlockSpec((4, LANE, LANE), lambda bi: (0, 0, 0))],
        out_specs=[spec, spec],
        compiler_params=_params(("parallel",)),
        name="cmp_proj",
    )(pooled, w4)


def _lane_parts(x):
    return [x[:, j * LANE:(j + 1) * LANE] for j in range(x.shape[1] // LANE)]


def _nsa_prompt_kernel(*refs, t_len, tq, tc, tw, n_sel, top, sub, pool_pages, page_base):
    if pool_pages:
        (pt_sm, qc_ref, qr_ref, ng_ref, bg_ref, ksel_ref, vsel_ref, kwin_ref, vwin_ref, kc_ref, vc_ref,
         pa_ref, cache_ref, o_ref, pool_ref,
         qc_scr, qr_scr, s_scr, p_scr, a_scr, m_scr, l_scr, acc_scr, oc_scr, b_scr,
         sw_scr, pw_scr, bw_scr, ow_scr, wl_scr, pbuf, psem) = refs
        step = (pl.program_id(0) * pl.num_programs(1) + pl.program_id(1)) * pl.num_programs(2) + pl.program_id(2)
        n_steps = pl.num_programs(0) * pl.num_programs(1) * pl.num_programs(2)
        slot = step % 2
        half_rows, n_cols = pbuf.shape[2], 2 * NSA_KV

        def page_copies(step_idx, p, to_slot):
            page = pt_sm[step_idx * pool_pages + p] + page_base
            return [pltpu.make_async_copy(cache_ref.at[page, pl.ds(hh * half_rows, half_rows), pl.ds(0, n_cols), :],
                                          pbuf.at[to_slot, p, :, pl.ds(hh * n_cols, n_cols), :], psem.at[to_slot])
                    for hh in range(2)]

        @pl.when(step == 0)
        def _():
            for p in range(pool_pages):
                for cp in page_copies(0, p, 0):
                    cp.start()

        @pl.when(step + 1 < n_steps)
        def _():
            for p in range(pool_pages):
                for cp in page_copies(step + 1, p, 1 - slot):
                    cp.start()

        for p in range(pool_pages):
            for cp in page_copies(step, p, slot):
                cp.wait()
        pa = pa_ref[...]
        for p in range(pool_pages):
            pool_ref[0, p] = jnp.sum(pbuf[slot, p] * pa, axis=0)
    else:
        (qc_ref, qr_ref, ng_ref, bg_ref, ksel_ref, vsel_ref, kwin_ref, vwin_ref, kc_ref, vc_ref,
         o_ref, qc_scr, qr_scr, s_scr, p_scr, a_scr, m_scr, l_scr, acc_scr, oc_scr, b_scr,
         sw_scr, pw_scr, bw_scr, ow_scr, wl_scr) = refs
    i = pl.program_id(2)
    hq = NSA_HPG
    for h in range(hq):
        qc_scr[pl.ds(h * tq, tq), :] = qc_ref[0, :, h * LANE:(h + 1) * LANE]
        qr_scr[pl.ds(h * tq, tq), 0:LANE] = qr_ref[0, :, h * LANE:(h + 1) * LANE]

    w0 = pl.multiple_of(jnp.clip(i * tq + tq - tw, 0, t_len - tw), LANE)
    kp = w0 + lax.broadcasted_iota(jnp.int32, (tq, tw), 1)
    tp = i * tq + lax.broadcasted_iota(jnp.int32, (tq, tw), 0)
    bw_scr[...] = jnp.where((kp <= tp) & (kp > tp - WINDOW), 0.0, MASK_NEG)
    sw_scr[...] = _nt(qr_scr[:, 0:LANE], kwin_ref[0, pl.ds(w0, tw), :])
    for r0 in range(0, hq * tq, sub):
        r = pl.ds(r0, sub)
        parts = _lane_parts(sw_scr[r, :] + bw_scr[pl.ds(r0 % tq, sub), :])
        m = jnp.max(functools.reduce(jnp.maximum, parts), axis=-1, keepdims=True)
        ews = [jnp.exp2(x - m) for x in parts]
        row_sum = jnp.sum(functools.reduce(jnp.add, ews), axis=-1, keepdims=True)
        wl_scr[r, :] = jnp.broadcast_to(1.0 / row_sum, (sub, LANE))
        pw_scr[r, :] = jnp.concatenate(ews, axis=1).astype(BF16)
    ow_scr[...] = _nn(pw_scr[...], vwin_ref[0, pl.ds(w0, tw), :])

    kc = kc_ref[0, 0]
    npad = kc.shape[0]
    s_scr[:, 0:npad] = _nt(qc_scr[...], kc)
    tpos = i * tq + lax.broadcasted_iota(jnp.int32, (tq, npad), 0)
    ncol = lax.broadcasted_iota(jnp.int32, (tq, npad), 1)
    cmask = (ncol + 1) * CMP_BLOCK <= tpos + 1
    imp = jnp.zeros((tq, npad), F32)
    for h in range(hq):
        r = pl.ds(h * tq, tq)
        s = jnp.where(cmask, s_scr[r, 0:npad], MASK_NEG)
        e = jnp.where(cmask, jnp.exp2(s - jnp.max(s, axis=-1, keepdims=True)), 0.0)
        p = e / jnp.maximum(jnp.sum(e, axis=-1, keepdims=True), 1e-30)
        p_scr[r, 0:npad] = p.astype(BF16)
        imp = imp + p
    oc_scr[...] = _nn(p_scr[:, 0:npad], vc_ref[0, 0])

    rows = min(npad, -(-n_sel // 8) * 8)
    imp_t = imp.T[0:rows]
    blk = lax.broadcasted_iota(jnp.int32, (rows, tq), 0)
    cur = lax.shift_right_logical(i * tq + lax.broadcasted_iota(jnp.int32, (rows, tq), 1), SEL_SHIFT)
    imp_t = jnp.where((blk == 0) | (blk == cur) | (blk == cur - 1), FORCE, imp_t)
    imp_t = jnp.where(blk > cur, -1.0, imp_t)
    imp_t = jnp.where(blk >= n_sel, -2.0, imp_t)
    rank = jnp.zeros((rows, tq), F32)
    for j in range(n_sel):
        a = imp_t[j:j + 1, :]
        ahead = (a > imp_t) | ((a == imp_t) & (blk > j))
        rank = rank + jnp.where(ahead, 1.0, 0.0)
    neg_t = jnp.where((rank < top) & (blk < n_sel) & (blk <= cur), 0.0, MASK_NEG)
    if npad > rows:
        neg_t = jnp.concatenate([neg_t, jnp.zeros((npad - rows, tq), F32)], axis=0)
    neg = neg_t.T.astype(BF16)
    for h in range(hq):
        qr_scr[pl.ds(h * tq, tq), LANE:2 * LANE] = neg

    m_scr[...] = jnp.full(m_scr.shape, MASK_NEG, F32)
    l_scr[...] = jnp.zeros(l_scr.shape, F32)
    acc_scr[...] = jnp.zeros(acc_scr.shape, F32)
    n_chunks = (i * tq + tq + tc - 1) // tc

    def chunk(ci, diagonal):
        k0 = pl.multiple_of(ci * tc, tc)
        v = vsel_ref[0, pl.ds(k0, tc), :]
        kb = lax.shift_right_logical(k0 + lax.broadcasted_iota(jnp.int32, (tc, npad), 0), SEL_SHIFT)
        jb = lax.broadcasted_iota(jnp.int32, (tc, npad), 1)
        k_aug = jnp.concatenate([ksel_ref[0, pl.ds(k0, tc), :], jnp.where(kb == jb, 1.0, 0.0).astype(BF16)], axis=1)
        s_scr[:, 0:tc] = _nt(qr_scr[...], k_aug)
        if diagonal:
            kp = k0 + lax.broadcasted_iota(jnp.int32, (tq, tc), 1)
            tp = i * tq + lax.broadcasted_iota(jnp.int32, (tq, tc), 0)
            b_scr[:, 0:tc] = jnp.where(kp <= tp, 0.0, MASK_NEG)
        for r0 in range(0, hq * tq, sub):
            r = pl.ds(r0, sub)
            sc = s_scr[r, 0:tc]
            if diagonal:
                sc = sc + b_scr[pl.ds(r0 % tq, sub), 0:tc]
            parts = _lane_parts(sc)
            m_prev = m_scr[r, :]
            m_new = jnp.maximum(m_prev, jnp.max(functools.reduce(jnp.maximum, parts), axis=-1, keepdims=True))
            alpha = jnp.exp2(m_prev - m_new)
            pes = [jnp.exp2(x - m_new) for x in parts]
            l_scr[r, :] = alpha * l_scr[r, :] + functools.reduce(jnp.add, pes)
            p_scr[r, 0:tc] = jnp.concatenate(pes, axis=1).astype(BF16)
            a_scr[r, :] = alpha
            m_scr[r, :] = m_new
        acc_scr[...] = a_scr[...] * acc_scr[...] + _nn(p_scr[:, 0:tc], v)

    def full_chunk(ci, carry):
        chunk(ci, False)
        return carry

    lax.fori_loop(0, n_chunks - 1, full_chunk, 0)
    chunk(n_chunks - 1, True)

    gate = jax.nn.sigmoid(bg_ref[0])
    for g in range(1, NSA_KV):
        gate = jnp.where(pl.program_id(1) == g, pltpu.roll(gate, LANE - g * hq * N_BRANCH, 1), gate)
    ng = ng_ref[0]
    for h in range(hq):
        r = slice(h * tq, (h + 1) * tq)
        o_s = acc_scr[r, :] * (1.0 / jnp.sum(l_scr[r, :], axis=-1, keepdims=True))
        o = (gate[:, 3 * h:3 * h + 1] * oc_scr[r, :] + gate[:, 3 * h + 1:3 * h + 2] * o_s
             + gate[:, 3 * h + 2:3 * h + 3] * (ow_scr[r, :] * wl_scr[r, :]))
        o_ref[0, :, h * LANE:(h + 1) * LANE] = (ng[:, h * LANE:(h + 1) * LANE] * o).astype(o_ref.dtype)


def _nsa_prompt(qc3, qr3, ng3, bg3, kvb, kc, vc, *, tq=128, tc=512, sub=64, pool=None):
    b, t, _ = qc3.shape
    nq = t // tq
    tc = min(tc, t)
    tw = min(WINDOW + tq, t)
    n_sel = t // SEL_BLOCK
    top = min(TOP_N, n_sel)
    npad = kc.shape[2]
    gw = NSA_HPG * LANE
    rows = NSA_HPG * tq
    wide = max(tc, npad)
    kvspec = lambda k: pl.BlockSpec((1, t, LANE), lambda bi, gi, qi, *_, k=k: (bi, 0, k + gi))
    cspec = pl.BlockSpec((1, 1, npad, LANE), lambda bi, gi, qi, *_: (bi, gi, 0, 0))
    gspec = pl.BlockSpec((1, tq, gw), lambda bi, gi, qi, *_: (bi, qi, gi))
    in_specs = [gspec, gspec, gspec,
                pl.BlockSpec((1, tq, LANE), lambda bi, gi, qi, *_: (bi, qi, 0)),
                kvspec(0), kvspec(2), kvspec(4), kvspec(6), cspec, cspec]
    out_shape = [jax.ShapeDtypeStruct((b, t, NSA_HEADS * LANE), BF16)]
    out_specs = [pl.BlockSpec((1, tq, gw), lambda bi, gi, qi, *_: (bi, qi, gi))]
    assert tc % tq == 0 and npad == LANE
    scratch = ([pltpu.VMEM((rows, LANE), BF16), pltpu.VMEM((rows, 2 * LANE), BF16)]
               + [pltpu.VMEM((rows, wide), F32), pltpu.VMEM((rows, wide), BF16)]
               + [pltpu.VMEM((rows, LANE), F32)] * 5 + [pltpu.VMEM((tq, wide), F32)]
               + [pltpu.VMEM((rows, tw), F32), pltpu.VMEM((rows, tw), BF16), pltpu.VMEM((tq, tw), F32)]
               + [pltpu.VMEM((rows, LANE), F32)] * 2)
    args = [qc3, qr3, ng3, bg3, kvb, kvb, kvb, kvb, kc, vc]
    n_steps = b * NSA_KV * nq
    pool_pages, page_base, prefetch = 0, 0, []
    if pool is not None:
        cache4, page_flat, a3, page_base = pool
        page, n_cols = cache4.shape[1], 2 * NSA_KV
        if page_flat.shape[0] % n_steps == 0 and page == 2 * CMP_BLOCK and 2 * n_cols == 8:
            pool_pages = page_flat.shape[0] // n_steps
            prefetch = [page_flat]
            in_specs += [pl.BlockSpec((CMP_BLOCK, 2 * n_cols, LANE), lambda bi, gi, qi, *_: (0, 0, 0)),
                         pl.BlockSpec(memory_space=pl.ANY)]
            args += [jnp.concatenate([a3, a3], axis=1), cache4]
            out_shape.append(jax.ShapeDtypeStruct((n_steps, pool_pages, 2 * n_cols, LANE), F32))
            out_specs.append(pl.BlockSpec((1, pool_pages, 2 * n_cols, LANE),
                                          lambda bi, gi, qi, *_: ((bi * NSA_KV + gi) * nq + qi, 0, 0, 0)))
            scratch += [pltpu.VMEM((2, pool_pages, CMP_BLOCK, 2 * n_cols, LANE), F32), pltpu.SemaphoreType.DMA((2,))]
    kern = functools.partial(_nsa_prompt_kernel, t_len=t, tq=tq, tc=tc, tw=tw, n_sel=n_sel, top=top,
                             sub=min(sub, tq), pool_pages=pool_pages, page_base=page_base)
    res = pl.pallas_call(
        kern,
        out_shape=out_shape,
        grid_spec=pltpu.PrefetchScalarGridSpec(
            num_scalar_prefetch=len(prefetch), grid=(b, NSA_KV, nq),
            in_specs=in_specs, out_specs=out_specs, scratch_shapes=scratch),
        compiler_params=_params(("arbitrary", "arbitrary", "arbitrary")),
        name="nsa_prompt",
    )(*prefetch, *args)
    return (res[0], res[1]) if pool_pages else (res[0], None)


def _conv_kernel(h_ref, b_ref, c_ref, g_ref, prev_ref, w_ref, y_ref, st_ref, up_scr, *, t_real):
    u = c_ref[...] * h_ref[...]
    t = u.shape[1]
    up_scr[:, pl.ds(8 - (CONV_W - 1), CONV_W - 1), :] = prev_ref[...]
    up_scr[:, pl.ds(8, t), :] = u
    w = w_ref[...]
    y = w[0:1, :][None] * up_scr[:, pl.ds(6, t), :]
    y = y + w[1:2, :][None] * up_scr[:, pl.ds(7, t), :]
    y = y + w[2:3, :][None] * u
    y = b_ref[...] * y
    y_ref[...] = (_silu(g_ref[...]) * y).astype(y_ref.dtype)
    st_ref[...] = up_scr[:, pl.ds(6 + t_real, CONV_W - 1), :]


def _conv_mixer(z3, prev, w_conv, *, t_real):
    b, t, _ = z3.shape
    nc = CONV_DIM // LANE
    zspec = lambda off: pl.BlockSpec((b, t, LANE), lambda ci, off=off: (0, 0, off // LANE + ci))
    return pl.pallas_call(
        functools.partial(_conv_kernel, t_real=t_real),
        out_shape=[jax.ShapeDtypeStruct((b, t, CONV_DIM), BF16),
                   jax.ShapeDtypeStruct((b, CONV_W - 1, CONV_DIM), F32)],
        grid=(nc,),
        in_specs=[zspec(C_H), zspec(C_B), zspec(C_C), zspec(C_G),
                  pl.BlockSpec((b, CONV_W - 1, LANE), lambda ci: (0, 0, ci)),
                  pl.BlockSpec((CONV_W, LANE), lambda ci: (0, ci))],
        out_specs=[pl.BlockSpec((b, t, LANE), lambda ci: (0, 0, ci)),
                   pl.BlockSpec((b, CONV_W - 1, LANE), lambda ci: (0, 0, ci))],
        scratch_shapes=[pltpu.VMEM((b, t + 8, LANE), F32)],
        compiler_params=_params(("parallel",)),
        name="conv_mixer",
    )(z3, z3, z3, z3, prev, w_conv)


def _mem_attn_kernel(q_ref, mg_ref, kv_ref, o_ref, *, interleaved):
    q = q_ref[0]
    mg = mg_ref[0]
    half = MEM_HEADS * LANE
    for h in range(MEM_HEADS):
        lo, hi = h * LANE, (h + 1) * LANE
        if interleaved:
            nm = kv_ref.shape[1] // (2 * MEM_HEADS)
            k = kv_ref[0, pl.ds(h, nm, stride=2 * MEM_HEADS), :].astype(BF16)
            v = kv_ref[0, pl.ds(MEM_HEADS + h, nm, stride=2 * MEM_HEADS), :].astype(BF16)
        else:
            k = kv_ref[0, :, lo:hi].astype(BF16)
            v = kv_ref[0, :, half + lo:half + hi].astype(BF16)
        s = _nt((q[:, lo:hi] * ATTN_SCALE).astype(BF16), k)
        e = jnp.exp(s - jnp.max(s, axis=-1, keepdims=True))
        o = _nn(e.astype(BF16), v) / jnp.sum(e, axis=-1, keepdims=True)
        o_ref[0, :, lo:hi] = (_silu(mg[:, lo:hi]) * o).astype(o_ref.dtype)


def _mem_attn(zb3, mkv, *, mq_off, mg_off, tq, interleaved, kv_base=0):
    b, t, _ = zb3.shape
    wq = MEM_HEADS * LANE
    return pl.pallas_call(
        functools.partial(_mem_attn_kernel, interleaved=interleaved),
        out_shape=jax.ShapeDtypeStruct((b, t, wq), BF16),
        grid=(b, t // tq),
        in_specs=[pl.BlockSpec((1, tq, wq), lambda bi, ti: (bi, ti, mq_off // wq)),
                  pl.BlockSpec((1, tq, wq), lambda bi, ti: (bi, ti, mg_off // wq)),
                  pl.BlockSpec((1,) + mkv.shape[1:], lambda bi, ti: (bi + kv_base, 0, 0))],
        out_specs=pl.BlockSpec((1, tq, wq), lambda bi, ti: (bi, ti, 0)),
        compiler_params=_params(("parallel", "parallel")),
        name="mem_attn",
    )(zb3, zb3, mkv)


def _out_proj_kernel(x_ref, ya_ref, yb_ref, ym_ref, w_ref, fg_ref, o_ref, w_scr, *, final):
    @pl.when(pl.program_id(0) == 0)
    def _():
        w_scr[...] = w_ref[...].astype(BF16)

    a, bw = CONV_DIM, CONV_DIM + NSA_HEADS * LANE
    acc = _nn(ya_ref[...], w_scr[0:a, :])
    acc = acc + _nn(yb_ref[...], w_scr[a:bw, :])
    acc = acc + _nn(ym_ref[...], w_scr[bw:, :])
    r = x_ref[...] + acc
    if final:
        r = r * lax.rsqrt(jnp.mean(r * r, axis=-1, keepdims=True) + NORM_EPS) * fg_ref[...]
    o_ref[...] = r


def _out_proj(x, ya, yb, ym, w, fg, *, tm, final):
    m, d = x.shape
    row = lambda width: pl.BlockSpec((tm, width), lambda i: (i, 0))
    return pl.pallas_call(
        functools.partial(_out_proj_kernel, final=final),
        out_shape=jax.ShapeDtypeStruct((m, d), F32),
        grid=(m // tm,),
        in_specs=[row(d), row(ya.shape[1]), row(yb.shape[1]), row(ym.shape[1]),
                  pl.BlockSpec(w.shape, lambda i: (0, 0), pipeline_mode=pl.Buffered(1)),
                  pl.BlockSpec((1, d), lambda i: (0, 0), pipeline_mode=pl.Buffered(1))],
        out_specs=row(d),
        scratch_shapes=[pltpu.VMEM(w.shape, BF16)],
        compiler_params=_params(("arbitrary",)),
        name="out_proj",
    )(x, ya, yb, ym, w, fg.reshape(1, d))


def _sample_pre_kernel(q_ref, ng_ref, bg_ref, kv0_ref, kv1_ref, kv2_ref, c_ref, s1_ref, s2_ref, kc_ref, vc_ref,
                       kvn_ref, wn_ref, qr_ref, oc_ref, ngo_ref, gate_ref, val_ref, *, past):
    tp = SAMPLE_T_PAD
    hq = NSA_HPG
    c, s1, s2 = c_ref[...], s1_ref[...], s2_ref[...]
    kv0, kv1, kv2 = kv0_ref[0], kv1_ref[0], kv2_ref[0]
    kvn_ref[0, :, 0:512] = kv0
    kvn_ref[0, :, 768:1024] = kv1[:, 256:512]
    wn_ref[0, :, 256:512] = kv2[:, 256:512]
    for g in range(NSA_KV):
        lo, hi = g * LANE, (g + 1) * LANE
        kvn_ref[0, :, 512 + lo:512 + hi] = _rope(kv1[:, lo:hi], c, s1, s2)
        wn_ref[0, :, lo:hi] = _rope(kv2[:, lo:hi], c, s1, s2)

    q = q_ref[0]
    ng = ng_ref[0]
    gates = jax.nn.sigmoid(bg_ref[0])
    for g in range(NSA_KV):
        qc_l, qr_l = [], []
        for h in range(hq):
            lo = (g * hq + h) * LANE
            qh = q[:, lo:lo + LANE]
            qc_l.append(qh * ATTN_SCALE)
            qr_l.append(_rope(qh, c, s1, s2) * ATTN_SCALE)
            ngo_ref[0, g, h * tp:(h + 1) * tp, :] = ng[:, lo:lo + LANE]
            for br in range(N_BRANCH):
                col = (g * hq + h) * N_BRANCH + br
                gate_ref[0, g, br, h * tp:(h + 1) * tp, :] = jnp.broadcast_to(gates[:, col:col + 1], (tp, LANE))
        qc = jnp.concatenate(qc_l, axis=0)
        qr_ref[0, g] = jnp.concatenate(qr_l, axis=0)

        kc = kc_ref[0, g]
        npad = kc.shape[0]
        s = _nt(qc.astype(BF16), kc)
        trow = lax.broadcasted_iota(jnp.int32, (hq * tp, npad), 0) % tp
        ncol = lax.broadcasted_iota(jnp.int32, (hq * tp, npad), 1)
        cmask = (ncol + 1) * CMP_BLOCK <= past + trow + 1
        s = jnp.where(cmask, s, MASK_NEG)
        e = jnp.where(cmask, jnp.exp(s - jnp.max(s, axis=-1, keepdims=True)), 0.0)
        p = e / jnp.maximum(jnp.sum(e, axis=-1, keepdims=True), 1e-30)
        oc_ref[0, g] = _nn(p.astype(BF16), vc_ref[0, g])
        imp = jnp.sum(p.reshape(hq, tp, npad), axis=0)

        blk = lax.broadcasted_iota(jnp.int32, (tp, npad), 1)
        cur = (past + lax.broadcasted_iota(jnp.int32, (tp, npad), 0)) // SEL_BLOCK
        val = jnp.where((blk == 0) | (blk == cur) | (blk == cur - 1), FORCE, imp)
        val = jnp.where(blk > cur, -1.0, val)
        val = jnp.where(blk >= past // SEL_BLOCK, -2.0, val)
        val_ref[0, g] = val


def _topk_kernel(val_ref, idx_ref, *, n_top):
    val = val_ref[...]
    rows, n = val.shape
    blk = lax.broadcasted_iota(jnp.int32, (rows, n), 1)
    lane = lax.broadcasted_iota(jnp.int32, (rows, LANE), 1)
    idx = jnp.zeros((rows, LANE), jnp.int32)
    for r in range(n_top):
        best = jnp.max(val, axis=-1, keepdims=True)
        j = jnp.min(jnp.where(val == best, blk, n), axis=-1, keepdims=True)
        idx = jnp.where(lane == r, j, idx)
        val = jnp.where(blk == j, -3e38, val)
    idx_ref[...] = idx


def _topk(val2, *, n_top):
    rows, n = val2.shape
    return pl.pallas_call(
        functools.partial(_topk_kernel, n_top=n_top),
        out_shape=jax.ShapeDtypeStruct((rows, LANE), jnp.int32),
        grid=(1,),
        in_specs=[pl.BlockSpec((rows, n), lambda i: (0, 0))],
        out_specs=pl.BlockSpec((rows, LANE), lambda i: (0, 0)),
        compiler_params=_params(("arbitrary",)),
        name="sample_topk",
    )(val2)


def _sample_pre(za3, zb3, bg3, tabs, kc, vc, *, past):
    bs, tp, _ = za3.shape
    npad = kc.shape[2]
    qw = NSA_HEADS * LANE
    kvblk = KV_OFF // 512
    zspec = lambda k: pl.BlockSpec((1, tp, 512), lambda bi, k=k: (bi, 0, kvblk + k))
    tspec = pl.BlockSpec((tp, LANE), lambda bi: (0, 0))
    cspec = pl.BlockSpec((1, NSA_KV, npad, LANE), lambda bi: (bi, 0, 0, 0))
    rows = NSA_HPG * tp
    gspec = pl.BlockSpec((1, NSA_KV, rows, LANE), lambda bi: (bi, 0, 0, 0))
    gshape = jax.ShapeDtypeStruct((bs, NSA_KV, rows, LANE), F32)
    return pl.pallas_call(
        functools.partial(_sample_pre_kernel, past=past),
        out_shape=[jax.ShapeDtypeStruct((bs, tp, 1024), F32),
                   jax.ShapeDtypeStruct((bs, tp, 512), F32),
                   gshape, gshape, gshape,
                   jax.ShapeDtypeStruct((bs, NSA_KV, N_BRANCH, rows, LANE), F32),
                   jax.ShapeDtypeStruct((bs, NSA_KV, tp, npad), F32)],
        grid=(bs,),
        in_specs=[pl.BlockSpec((1, tp, qw), lambda bi: (bi, 0, Q_OFF // qw)),
                  pl.BlockSpec((1, tp, qw), lambda bi: (bi, 0, NG_OFF // qw)),
                  pl.BlockSpec((1, tp, LANE), lambda bi: (bi, 0, 0)),
                  zspec(0), zspec(1), zspec(2), tspec, tspec, tspec, cspec, cspec],
        out_specs=[pl.BlockSpec((1, tp, 1024), lambda bi: (bi, 0, 0)),
                   pl.BlockSpec((1, tp, 512), lambda bi: (bi, 0, 0)),
                   gspec, gspec, gspec,
                   pl.BlockSpec((1, NSA_KV, N_BRANCH, rows, LANE), lambda bi: (bi, 0, 0, 0, 0)),
                   pl.BlockSpec((1, NSA_KV, tp, npad), lambda bi: (bi, 0, 0, 0))],
        compiler_params=_params(("parallel",)),
        name="sample_pre",
    )(za3, za3, bg3, zb3, zb3, zb3, *tabs, kc, vc)


def _sample_attn_kernel(idx_sm, pt_sm, qr_ref, oc_ref, ng_ref, gate_ref, ksn_ref, vsn_ref,
                        wc_ref, kwn_ref, vwn_ref, cache_ref, o_ref,
                        kbuf, vbuf, kw_scr, vw_scr, sem, *, ts, n_top, n_pages, page_base, per_page, wb):
    tp = SAMPLE_T_PAD
    hq = NSA_HPG
    b = pl.program_id(0)
    g = pl.program_id(1)
    n_gath = n_top * SEL_BLOCK
    ks_rows = kbuf.shape[2]
    step = b * NSA_KV + g
    n_steps = pl.num_programs(0) * NSA_KV
    slot = step % 2

    def gather_copies(step_idx, t, r, to_slot):
        bb, gg = step_idx // NSA_KV, step_idx % NSA_KV
        blk = idx_sm[(step_idx * ts + t) * n_top + r]
        page = pt_sm[bb * n_pages + blk // per_page] + page_base
        row0 = (blk % per_page) * SEL_BLOCK
        src_k = cache_ref.at[page, pl.ds(row0, SEL_BLOCK), 2 * NSA_KV + gg]
        src_v = cache_ref.at[page, pl.ds(row0, SEL_BLOCK), 3 * NSA_KV + gg]
        dst = pl.ds(r * SEL_BLOCK, SEL_BLOCK)
        return (pltpu.make_async_copy(src_k, kbuf.at[to_slot, t, dst], sem.at[to_slot, 0]),
                pltpu.make_async_copy(src_v, vbuf.at[to_slot, t, dst], sem.at[to_slot, 1]))

    def start_gathers(step_idx, to_slot):
        for t in range(ts):
            for r in range(n_top):
                ck, cv = gather_copies(step_idx, t, r, to_slot)
                ck.start()
                cv.start()

    @pl.when(step == 0)
    def _():
        start_gathers(0, 0)

    @pl.when(step + 1 < n_steps)
    def _():
        start_gathers(step + 1, 1 - slot)

    qr = qr_ref[0, 0].astype(BF16)
    trow = lax.broadcasted_iota(jnp.int32, (hq * tp, 1), 0) % tp

    ww = kw_scr.shape[0]
    kw_scr[pl.ds(0, wb), :] = wc_ref[0, pl.ds(g, wb, stride=2 * NSA_KV), :]
    vw_scr[pl.ds(0, wb), :] = wc_ref[0, pl.ds(NSA_KV + g, wb, stride=2 * NSA_KV), :]
    kw_scr[pl.ds(wb, tp), :] = kwn_ref[0]
    vw_scr[pl.ds(wb, tp), :] = vwn_ref[0]
    kw_scr[pl.ds(wb + tp, ww - wb - tp), :] = jnp.zeros((ww - wb - tp, LANE), F32)
    vw_scr[pl.ds(wb + tp, ww - wb - tp), :] = jnp.zeros((ww - wb - tp, LANE), F32)
    sw = _nt(qr, kw_scr[...].astype(BF16))
    jw = lax.broadcasted_iota(jnp.int32, (hq * tp, ww), 1)
    rel = jw - wb
    okw = (rel <= trow) & (rel > trow - WINDOW) & (jw < wb + ts)
    sw = jnp.where(okw, sw, MASK_NEG)
    ew = jnp.where(okw, jnp.exp(sw - jnp.max(sw, axis=-1, keepdims=True)), 0.0)
    o_w = _nn(ew.astype(BF16), vw_scr[...].astype(BF16)) / jnp.sum(ew, axis=-1, keepdims=True)

    for t in range(ts):
        for r in range(n_top):
            ck, cv = gather_copies(step, t, r, slot)
            ck.wait()
            cv.wait()

    js = lax.broadcasted_iota(jnp.int32, (hq * tp, ks_rows), 1)
    o_s = jnp.zeros((hq * tp, LANE), F32)
    for t in range(ts):
        kbuf[slot, t, pl.ds(n_gath, tp), :] = ksn_ref[0]
        vbuf[slot, t, pl.ds(n_gath, tp), :] = vsn_ref[0]
        kbuf[slot, t, pl.ds(n_gath + tp, ks_rows - n_gath - tp), :] = jnp.zeros((ks_rows - n_gath - tp, LANE), F32)
        vbuf[slot, t, pl.ds(n_gath + tp, ks_rows - n_gath - tp), :] = jnp.zeros((ks_rows - n_gath - tp, LANE), F32)
        ss = _nt(qr, kbuf[slot, t].astype(BF16))
        oks = (js < n_gath) | ((js - n_gath <= t) & (js < n_gath + ts))
        ss = jnp.where(oks, ss, MASK_NEG)
        es = jnp.where(oks, jnp.exp(ss - jnp.max(ss, axis=-1, keepdims=True)), 0.0)
        ot = _nn(es.astype(BF16), vbuf[slot, t].astype(BF16)) / jnp.sum(es, axis=-1, keepdims=True)
        o_s = jnp.where(trow == t, ot, o_s)

    o = gate_ref[0, 0, 0] * oc_ref[0, 0] + gate_ref[0, 0, 1] * o_s + gate_ref[0, 0, 2] * o_w
    y = _silu(ng_ref[0, 0]) * o
    for h in range(hq):
        o_ref[0, :, h * LANE:(h + 1) * LANE] = y[h * tp:(h + 1) * tp].astype(o_ref.dtype)


def _sample_attn(idx_flat, page_flat, qr, oc, ngo, gates, kvn, cache_win_rows, wn, cache4, *,
                 ts, n_top, n_pages, page_base, win_base):
    bs = qr.shape[0]
    tp = SAMPLE_T_PAD
    rows = NSA_HPG * tp
    wb = cache_win_rows.shape[1] // (2 * NSA_KV)
    per_page = cache4.shape[1] // SEL_BLOCK
    ks_rows = -(-(n_top * SEL_BLOCK + tp) // LANE) * LANE
    ww = -(-(wb + tp) // LANE) * LANE
    gspec = pl.BlockSpec((1, 1, rows, LANE), lambda bi, gi, *_: (bi, gi, 0, 0))
    newspec = lambda k: pl.BlockSpec((1, tp, LANE), lambda bi, gi, *_, k=k: (bi, 0, k + gi))
    grid_spec = pltpu.PrefetchScalarGridSpec(
        num_scalar_prefetch=2, grid=(bs, NSA_KV),
        in_specs=[gspec, gspec, gspec,
                  pl.BlockSpec((1, 1, N_BRANCH, rows, LANE), lambda bi, gi, *_: (bi, gi, 0, 0, 0)),
                  newspec(2 * NSA_KV), newspec(3 * NSA_KV),
                  pl.BlockSpec((1,) + cache_win_rows.shape[1:], lambda bi, gi, *_: (bi + win_base, 0, 0)),
                  newspec(0), newspec(NSA_KV),
                  pl.BlockSpec(memory_space=pl.ANY)],
        out_specs=pl.BlockSpec((1, tp, NSA_HPG * LANE), lambda bi, gi, *_: (bi, 0, gi)),
        scratch_shapes=[pltpu.VMEM((2, ts, ks_rows, LANE), F32), pltpu.VMEM((2, ts, ks_rows, LANE), F32),
                        pltpu.VMEM((ww, LANE), F32), pltpu.VMEM((ww, LANE), F32),
                        pltpu.SemaphoreType.DMA((2, 2))])
    kern = functools.partial(_sample_attn_kernel, ts=ts, n_top=n_top, n_pages=n_pages,
                             page_base=page_base, per_page=per_page, wb=wb)
    return pl.pallas_call(
        kern,
        out_shape=jax.ShapeDtypeStruct((bs, tp, NSA_HEADS * LANE), BF16),
        grid_spec=grid_spec,
        compiler_params=_params(("arbitrary", "arbitrary")),
        name="sample_attn",
    )(idx_flat, page_flat, qr, oc, ngo, gates, kvn, kvn, cache_win_rows, wn, wn, cache4)


def _rope_tables(pos, rows):
    freqs = jnp.power(ROPE_THETA, -jnp.arange(ROPE_HALF, dtype=F32) * (2.0 / ROPE_DIM))
    ang = pos.astype(F32)[:, None] * freqs[None, :]
    cos, sin = jnp.cos(ang), jnp.sin(ang)
    n = pos.shape[0]
    z16 = jnp.zeros((n, ROPE_HALF), F32)
    rest = LANE - ROPE_DIM
    c = jnp.concatenate([cos, cos, jnp.ones((n, rest), F32)], axis=1)
    s1 = jnp.concatenate([z16, sin, jnp.zeros((n, rest), F32)], axis=1)
    s2 = jnp.concatenate([-sin, z16, jnp.zeros((n, rest), F32)], axis=1)
    pad = lambda a: jnp.pad(a, ((0, rows - n), (0, 0)))
    return pad(c), pad(s1), pad(s2)


def _layer_weights(norm_g, w_in3, layer, w_conv, a_cmp, w_cmp, w_out):
    w_t3 = jnp.swapaxes(w_in3, 1, 2)
    assert w_t3.shape[1] == BG_SRC + BG_N + ZB_WIDTH and BG_SRC == ZA_WIDTH and BG_N % 8 == 0
    a4 = jnp.concatenate([a_cmp[0], a_cmp[0], a_cmp[1], a_cmp[1]], axis=1)
    a3 = jnp.stack([a_cmp[0], a_cmp[0], a_cmp[1], a_cmp[1]], axis=1)
    w4 = jnp.stack([w_cmp[0], w_cmp[0], w_cmp[1], w_cmp[1]]).astype(BF16)
    return dict(norm_g=norm_g, w_t3=w_t3, layer=layer, w_conv=w_conv, a4=a4, a3=a3, w4=w4,
                w_out=w_out)


def _prompt_layer(xp, xs2, mem_prompt, mem_norm_g, w_mem, lw, final_g, final, pool):
    b, t, d = xp.shape
    m = b * t
    x2 = xp.reshape(m, d)
    tabs = _rope_tables(jnp.arange(t, dtype=jnp.int32), t)
    wb = min(WINDOW, t)
    qc, qr, ng, bg, ya, conv_new, za_s, bg_s = _proj_conv(x2, lw["norm_g"], lw["w_t3"], lw["layer"], lw["w_conv"],
                                                          tabs, xs2, tm=min(512, t), seq_len=t)
    kvn, kvb, win_rows, pooled_prompt, zm, zb_s = _proj_kv(x2, lw["norm_g"], lw["w_t3"], lw["layer"], tabs, lw["a4"],
                                                           xs2, tm=wb, seq_len=t)
    nm = mem_prompt.shape[1]
    mkv = _norm_matmul(mem_prompt.reshape(b * nm, d), mem_norm_g, w_mem, tm=min(512, b * nm), tn=512)
    mkv3 = mkv.reshape(b, nm, 2 * MEM_HEADS * LANE)
    kc, vc = _cmp_proj(pooled_prompt.reshape(b, t // CMP_BLOCK, -1), lw["w4"])
    r3 = lambda a: a.reshape(b, t, -1)
    yb, pooled = _nsa_prompt(r3(qc), r3(qr), r3(ng), r3(bg), r3(kvb), kc, vc, tq=min(256, t), pool=pool)
    ym = _mem_attn(zm.reshape(b, t, -1), mkv3, mq_off=0, mg_off=MG_OFF - MQ_OFF, tq=min(1024, t), interleaved=False)
    out = _out_proj(x2, ya, yb.reshape(m, -1), ym.reshape(m, -1),
                    lw["w_out"], final_g, tm=min(512, m), final=final)
    kv_new = kvn.reshape(b, t, 4, NSA_KV, HEAD_DIM)
    win_new = win_rows.reshape(b, wb, 2, NSA_KV, HEAD_DIM)
    mem_kv = mkv.reshape(b, nm, 2, MEM_HEADS, HEAD_DIM)
    return out.reshape(b, t, d), kv_new, win_new, conv_new, mem_kv, pooled, (za_s, zb_s, bg_s)


def _sample_layer(xs_p, proj, ts, layer, cache4, page_flat, pooled, cache_win, state_conv, cache_mem, lw, final_g,
                  final):
    bs, tp, d = xs_p.shape
    depth = cache_win.shape[0]
    pool, page = cache4.shape[0] // depth, cache4.shape[1]
    n_pages = page_flat.shape[0] // bs
    past = n_pages * page
    assert past % SEL_BLOCK == 0 and ts <= SEL_BLOCK and ts <= tp
    n_past = past // SEL_BLOCK
    n_top = min(TOP_N, n_past + 1) - 1
    m = bs * tp
    za, zb, bg = proj
    za3 = za.reshape(bs, tp, ZA_WIDTH)
    zb3 = zb.reshape(bs, tp, ZB_WIDTH)
    bg3 = bg.reshape(bs, tp, LANE)
    tabs = _rope_tables(past + jnp.arange(tp, dtype=jnp.int32), tp)
    if pooled is None:
        pooled = _pool_pages(cache4, page_flat, lw["a3"], bs=bs, n_pages=n_pages, page_base=layer * pool)
    else:
        pooled = pooled.reshape(bs, n_pages * (page // CMP_BLOCK), 2 * NSA_KV * LANE)
    kc, vc = _cmp_proj(pooled, lw["w4"])
    kvn, wn, qr, oc, ngo, gates, val = _sample_pre(za3, zb3, bg3, tabs, kc, vc, past=past)
    idx = _topk(val.reshape(bs * NSA_KV * tp, val.shape[-1]), n_top=n_top)
    idx_flat = idx.reshape(bs, NSA_KV, tp, LANE)[:, :, :ts, :n_top].reshape(-1)
    wbuf = cache_win.shape[2]
    cache_win_rows = cache_win.reshape(depth * bs, wbuf * 2 * NSA_KV, HEAD_DIM)
    yb = _sample_attn(idx_flat, page_flat, qr, oc, ngo, gates, kvn, cache_win_rows, wn, cache4,
                      ts=ts, n_top=n_top, n_pages=n_pages, page_base=layer * pool, win_base=layer * bs)
    ya, conv_new = _conv_mixer(za3, state_conv[layer], lw["w_conv"], t_real=ts)
    nm = cache_mem.shape[2]
    mem_rows = cache_mem.reshape(depth * bs, nm * 2 * MEM_HEADS, HEAD_DIM)
    ym = _mem_attn(zb3, mem_rows, mq_off=MQ_OFF, mg_off=MG_OFF, tq=tp, interleaved=True, kv_base=layer * bs)
    out = _out_proj(xs_p.reshape(m, d), ya.reshape(m, -1), yb.reshape(m, -1), ym.reshape(m, -1),
                    lw["w_out"], final_g, tm=m, final=final)
    kv_new = kvn[:, :ts].reshape(bs, ts, 4, NSA_KV, HEAD_DIM)
    win_rows = wn[:, :ts].reshape(bs, ts, 2, NSA_KV, HEAD_DIM)
    win_state = jnp.concatenate([cache_win[layer], win_rows], axis=1)[:, ts:]
    return out.reshape(bs, tp, d), kv_new, win_state, conv_new


def kernel(x_prompt, x_sample, cache_kv, cache_win, state_conv, cache_mem, page_table, mem_prompt,
           norm_g, w_in, w_conv, a_cmp, w_cmp, mem_norm_g, w_mem_kv, w_out, final_g):
    depth = w_in.shape[0]
    ts = x_sample.shape[1]
    xp = x_prompt
    xs = jnp.pad(x_sample, ((0, 0), (0, SAMPLE_T_PAD - ts), (0, 0)))
    pool_size, page = cache_kv.shape[1], cache_kv.shape[2]
    cache4 = cache_kv.reshape(depth * pool_size, page, 4 * NSA_KV, HEAD_DIM)
    page_flat = page_table.reshape(-1).astype(jnp.int32)
    kv_p, win_p, conv_p, mem_p, kv_s, win_s, conv_s = [], [], [], [], [], [], []
    for l in range(depth):
        lw = _layer_weights(norm_g[l], w_in, l, w_conv[l], a_cmp[l], w_cmp[l], w_out[l])
        final = l == depth - 1
        xp, kvn, winn, convn, mkv, pooled, proj_s = _prompt_layer(
            xp, xs.reshape(-1, xs.shape[-1]), mem_prompt, mem_norm_g[l], w_mem_kv[l], lw, final_g, final,
            pool=(cache4, page_flat, lw["a3"], l * pool_size))
        kv_p.append(kvn)
        win_p.append(winn)
        conv_p.append(convn)
        mem_p.append(mkv)
        xs, kvn, winn, convn = _sample_layer(xs, proj_s, ts, l, cache4, page_flat, pooled, cache_win, state_conv,
                                             cache_mem, lw, final_g, final)
        kv_s.append(kvn)
        win_s.append(winn)
        conv_s.append(convn)
    return (xp, xs[:, :ts], jnp.stack(kv_p), jnp.stack(win_p), jnp.stack(conv_p), jnp.stack(mem_p),
            jnp.stack(kv_s), jnp.stack(win_s), jnp.stack(conv_s))
```

```python
import functools

import jax
import jax.numpy as jnp
from jax import lax
from jax.experimental import pallas as pl
from jax.experimental.pallas import tpu as pltpu

F32 = jnp.float32
BF16 = jnp.bfloat16

HEAD_DIM = 128
CONV_DIM = 512
CONV_W = 3
NSA_HEADS = 8
NSA_KV = 2
NSA_HPG = NSA_HEADS // NSA_KV
MEM_HEADS = 4
N_BRANCH = 3
ROPE_DIM = HEAD_DIM // 4
ROPE_HALF = ROPE_DIM // 2
ROPE_THETA = 500000.0
CMP_BLOCK = 64
SEL_BLOCK = 64
SEL_SHIFT = 6
TOP_N = 16
WINDOW = 512
NORM_EPS = 1e-6
MASK_NEG = -1e30
FORCE = 1e9
ATTN_SCALE = HEAD_DIM ** -0.5
SCALE_LOG2 = ATTN_SCALE * 1.4426950408889634

C_H, C_B, C_C, C_G, Q_OFF, NG_OFF = 0, 512, 1024, 1536, 2048, 3072
ZA_WIDTH = 4096
KV_OFF, MQ_OFF, MG_OFF = 0, 1536, 2048
ZB_WIDTH = 2560
BG_SRC = 4096
BG_N = NSA_HEADS * N_BRANCH
LANE = 128
SAMPLE_T_PAD = 8
VMEM_LIMIT = 56 * 1024 * 1024
W_CHUNK = 512


def _nt(a, b):
    return lax.dot_general(a, b, (((1,), (1,)), ((), ())), preferred_element_type=F32)


def _nn(a, b):
    return jnp.dot(a, b, preferred_element_type=F32)


def _params(sem, vmem=VMEM_LIMIT):
    return pltpu.CompilerParams(dimension_semantics=sem, vmem_limit_bytes=vmem)


def _rope(x, c, s1, s2):
    return x * c + pltpu.roll(x, ROPE_HALF, 1) * s1 + pltpu.roll(x, LANE - ROPE_HALF, 1) * s2


def _silu(x):
    return x * jax.nn.sigmoid(x)


def _rms_rows(x_ref, g_ref):
    x = x_ref[...]
    return (x * lax.rsqrt(jnp.mean(x * x, axis=-1, keepdims=True) + NORM_EPS) * g_ref[...]).astype(BF16)


def _norm_matmul_kernel(x_ref, g_ref, w_ref, z_ref, h_scr, *, tn):
    h_scr[...] = _rms_rows(x_ref, g_ref)
    for j in range(w_ref.shape[1] // tn):
        z_ref[:, j * tn:(j + 1) * tn] = _nn(h_scr[...], w_ref[:, j * tn:(j + 1) * tn].astype(BF16))


def _norm_matmul(x, g, w, *, tm, tn):
    m, d = x.shape
    n = w.shape[1]
    assert n % tn == 0 and m % tm == 0 and w.shape[0] == d
    resident = lambda shape: pl.BlockSpec(shape, lambda i: (0, 0), pipeline_mode=pl.Buffered(1))
    return pl.pallas_call(
        functools.partial(_norm_matmul_kernel, tn=tn),
        out_shape=jax.ShapeDtypeStruct((m, n), F32),
        grid=(m // tm,),
        in_specs=[pl.BlockSpec((tm, d), lambda i: (i, 0)), resident((1, d)), resident((d, n))],
        out_specs=pl.BlockSpec((tm, n), lambda i: (i, 0)),
        scratch_shapes=[pltpu.VMEM((tm, d), BF16)],
        compiler_params=_params(("parallel",)),
        name="norm_matmul",
    )(x, g.reshape(1, d), w)


def _fetch_weight_rows(w_hbm, layer, first_row, w_scr, stage, sem):
    ch = stage.shape[1]
    n_chunks = w_scr.shape[0] // ch
    assert n_chunks * ch == w_scr.shape[0]

    def chunk_copy(c):
        return pltpu.make_async_copy(w_hbm.at[layer, pl.ds(first_row + c * ch, ch), :], stage.at[c % 2], sem.at[c % 2])

    chunk_copy(0).start()
    for c in range(n_chunks):
        if c + 1 < n_chunks:
            chunk_copy(c + 1).start()
        chunk_copy(c).wait()
        w_scr[c * ch:(c + 1) * ch, :] = stage[c % 2].astype(BF16)


def _fetch_gate_rows(w_hbm, layer, wbg_scr, stage, sem):
    cp = pltpu.make_async_copy(w_hbm.at[layer, pl.ds(BG_SRC, BG_N), :], stage.at[0, pl.ds(0, BG_N)], sem.at[0])
    cp.start()
    cp.wait()
    wbg_scr[...] = jnp.zeros(wbg_scr.shape, BF16)
    wbg_scr[0:BG_N, :] = stage[0, 0:BG_N, :].astype(BF16)


def _proj_conv_kernel(x_ref, g_ref, w_hbm, wc_ref, c_ref, s1_ref, s2_ref, xs_ref,
                      qc_ref, qr_ref, ng_ref, bg_ref, ya_ref, st_ref, zs_ref, bgs_ref,
                      h_scr, up_scr, w_ref, wbg_ref, stage, sem, *, blocks_per_seq, layer):
    tm = x_ref.shape[0]
    cw = CONV_DIM
    i = pl.program_id(0)
    first = i % blocks_per_seq == 0
    @pl.when(i == 0)
    def _():
        _fetch_weight_rows(w_hbm, layer, 0, w_ref, stage, sem)
        _fetch_gate_rows(w_hbm, layer, wbg_ref, stage, sem)
        up_scr[...] = jnp.zeros(up_scr.shape, F32)

    h_scr[...] = _rms_rows(x_ref, g_ref)
    bg_ref[...] = _nt(h_scr[...], wbg_ref[...])
    chunk = lambda off: _nt(h_scr[...], w_ref[off:off + cw, :])


    carry = up_scr[pl.ds(8 + tm - (CONV_W - 1), CONV_W - 1), :]
    up_scr[pl.ds(8 - (CONV_W - 1), CONV_W - 1), :] = jnp.where(first, 0.0, carry)
    u = chunk(C_C) * chunk(C_H)
    up_scr[pl.ds(8, tm), :] = u
    wc = wc_ref[...]
    y = wc[0:1, :] * up_scr[pl.ds(6, tm), :]
    y = y + wc[1:2, :] * up_scr[pl.ds(7, tm), :]
    y = y + wc[2:3, :] * u
    y = chunk(C_B) * y
    ya_ref[...] = (_silu(chunk(C_G)) * y).astype(ya_ref.dtype)

    st_ref[0] = up_scr[pl.ds(8 + tm - (CONV_W - 1), CONV_W - 1), :]

    c, s1, s2 = c_ref[...], s1_ref[...], s2_ref[...]
    for j in range((NG_OFF - Q_OFF) // cw):
        qv = chunk(Q_OFF + j * cw)
        for k in range(cw // LANE):
            lo = j * cw + k * LANE
            qh = qv[:, k * LANE:(k + 1) * LANE]
            qc_ref[:, lo:lo + LANE] = (qh * SCALE_LOG2).astype(BF16)
            qr_ref[:, lo:lo + LANE] = (_rope(qh, c, s1, s2) * SCALE_LOG2).astype(BF16)
    for j in range((ZA_WIDTH - NG_OFF) // cw):
        ng_ref[:, j * cw:(j + 1) * cw] = _silu(chunk(NG_OFF + j * cw))

    @pl.when(i == pl.num_programs(0) - 1)
    def _():
        hs = _rms_rows(xs_ref, g_ref)
        bgs_ref[...] = _nt(hs, wbg_ref[...])
        for j in range(ZA_WIDTH // cw):
            zs_ref[:, j * cw:(j + 1) * cw] = _nt(hs, w_ref[j * cw:(j + 1) * cw, :])


def _proj_conv(x2, g, w_t3, layer, w_conv, tabs, xs2, *, tm, seq_len):
    m, d = x2.shape
    ms = xs2.shape[0]
    assert seq_len % tm == 0 and CONV_W == 3
    blocks_per_seq = seq_len // tm
    nq, nng = NG_OFF - Q_OFF, ZA_WIDTH - NG_OFF
    resident = lambda shape: pl.BlockSpec(shape, lambda i: (0,) * len(shape), pipeline_mode=pl.Buffered(1))
    row = lambda width: pl.BlockSpec((tm, width), lambda i: (i, 0))
    tspec = pl.BlockSpec((tm, LANE), lambda i: (i % blocks_per_seq, 0))
    return pl.pallas_call(
        functools.partial(_proj_conv_kernel, blocks_per_seq=blocks_per_seq, layer=layer),
        out_shape=[jax.ShapeDtypeStruct((m, nq), BF16), jax.ShapeDtypeStruct((m, nq), BF16),
                   jax.ShapeDtypeStruct((m, nng), F32), jax.ShapeDtypeStruct((m, LANE), F32),
                   jax.ShapeDtypeStruct((m, CONV_DIM), BF16),
                   jax.ShapeDtypeStruct((m // seq_len, CONV_W - 1, CONV_DIM), F32),
                   jax.ShapeDtypeStruct((ms, ZA_WIDTH), F32), jax.ShapeDtypeStruct((ms, LANE), F32)],
        grid=(m // tm,),
        in_specs=[row(d), resident((1, d)), pl.BlockSpec(memory_space=pl.ANY), resident(w_conv.shape),
                  tspec, tspec, tspec, resident((ms, d))],
        out_specs=[row(nq), row(nq), row(nng), row(LANE), row(CONV_DIM),
                   pl.BlockSpec((1, CONV_W - 1, CONV_DIM), lambda i: (i // blocks_per_seq, 0, 0)),
                   pl.BlockSpec((ms, ZA_WIDTH), lambda i: (0, 0)), pl.BlockSpec((ms, LANE), lambda i: (0, 0))],
        scratch_shapes=[pltpu.VMEM((tm, d), BF16), pltpu.VMEM((tm + 8, CONV_DIM), F32),
                        pltpu.VMEM((ZA_WIDTH, d), BF16), pltpu.VMEM((LANE, d), BF16),
                        pltpu.VMEM((2, W_CHUNK, d), F32), pltpu.SemaphoreType.DMA((2,))],
        compiler_params=_params(("arbitrary",)),
        name="proj_conv",
    )(x2, g.reshape(1, d), w_t3, w_conv, *tabs, xs2)


def _proj_kv_kernel(x_ref, g_ref, w_hbm, c_ref, s1_ref, s2_ref, a_ref, xs_ref,
                    kvn_ref, kvb_ref, win_ref, pool_ref, zm_ref, zs_ref, h_scr, w_ref, stage, sem,
                    *, blocks_per_seq, layer):
    tm = x_ref.shape[0]
    cw = 2 * NSA_KV * LANE

    @pl.when(pl.program_id(0) == 0)
    def _():
        _fetch_weight_rows(w_hbm, layer, BG_SRC + BG_N, w_ref, stage, sem)

    h_scr[...] = _rms_rows(x_ref, g_ref)
    chunk = lambda off: _nt(h_scr[...], w_ref[off:off + cw, :])
    c, s1, s2 = c_ref[...], s1_ref[...], s2_ref[...]
    kv0 = chunk(KV_OFF)
    kv1 = chunk(KV_OFF + cw)
    kv2 = chunk(KV_OFF + 2 * cw)
    n_kv, n_w, half = 4 * NSA_KV, 2 * NSA_KV, NSA_KV * LANE
    for g in range(NSA_KV):
        lo, hi = g * LANE, (g + 1) * LANE
        ks = _rope(kv1[:, lo:hi], c, s1, s2)
        kw = _rope(kv2[:, lo:hi], c, s1, s2)
        vs = kv1[:, half + lo:half + hi]
        vw = kv2[:, half + lo:half + hi]
        kvn_ref[pl.ds(g, tm, stride=n_kv), :] = kv0[:, lo:hi]
        kvn_ref[pl.ds(NSA_KV + g, tm, stride=n_kv), :] = kv0[:, half + lo:half + hi]
        kvn_ref[pl.ds(2 * NSA_KV + g, tm, stride=n_kv), :] = ks
        kvn_ref[pl.ds(3 * NSA_KV + g, tm, stride=n_kv), :] = vs
        kvb_ref[:, lo:hi] = ks.astype(BF16)
        kvb_ref[:, half + lo:half + hi] = vs.astype(BF16)
        kvb_ref[:, 2 * half + lo:2 * half + hi] = kw.astype(BF16)
        kvb_ref[:, 3 * half + lo:3 * half + hi] = vw.astype(BF16)
        win_ref[pl.ds(g, tm, stride=n_w), :] = kw
        win_ref[pl.ds(NSA_KV + g, tm, stride=n_w), :] = vw

    pool_ref[...] = jnp.sum(kv0.reshape(tm // CMP_BLOCK, CMP_BLOCK, cw) * a_ref[...][None], axis=1)
    for j in range(zm_ref.shape[1] // cw):
        zm_ref[:, j * cw:(j + 1) * cw] = chunk(MQ_OFF + j * cw)

    @pl.when(pl.program_id(0) == pl.num_programs(0) - 1)
    def _():
        hs = _rms_rows(xs_ref, g_ref)
        for j in range(ZB_WIDTH // cw):
            zs_ref[:, j * cw:(j + 1) * cw] = _nt(hs, w_ref[j * cw:(j + 1) * cw, :])


def _proj_kv(x2, g, w_t3, layer, tabs, a4, xs2, *, tm, seq_len):
    m, d = x2.shape
    ms = xs2.shape[0]
    assert seq_len % tm == 0
    blocks_per_seq = seq_len // tm
    n_kv, n_w = 4 * NSA_KV, 2 * NSA_KV
    nm = ZB_WIDTH - MQ_OFF
    resident = lambda shape: pl.BlockSpec(shape, lambda i: (0,) * len(shape), pipeline_mode=pl.Buffered(1))
    row = lambda width: pl.BlockSpec((tm, width), lambda i: (i, 0))
    tspec = pl.BlockSpec((tm, LANE), lambda i: (i % blocks_per_seq, 0))
    return pl.pallas_call(
        functools.partial(_proj_kv_kernel, blocks_per_seq=blocks_per_seq, layer=layer),
        out_shape=[jax.ShapeDtypeStruct((m * n_kv, LANE), F32),
                   jax.ShapeDtypeStruct((m, n_kv * LANE), BF16),
                   jax.ShapeDtypeStruct((m // seq_len * tm * n_w, LANE), F32),
                   jax.ShapeDtypeStruct((m // CMP_BLOCK, 2 * NSA_KV * LANE), F32),
                   jax.ShapeDtypeStruct((m, nm), F32), jax.ShapeDtypeStruct((ms, ZB_WIDTH), F32)],
        grid=(m // tm,),
        in_specs=[row(d), resident((1, d)), pl.BlockSpec(memory_space=pl.ANY), tspec, tspec, tspec,
                  resident(a4.shape), resident((ms, d))],
        out_specs=[pl.BlockSpec((tm * n_kv, LANE), lambda i: (i, 0)), row(n_kv * LANE),
                   pl.BlockSpec((tm * n_w, LANE), lambda i: (i // blocks_per_seq, 0)),
                   pl.BlockSpec((tm // CMP_BLOCK, 2 * NSA_KV * LANE), lambda i: (i, 0)), row(nm),
                   pl.BlockSpec((ms, ZB_WIDTH), lambda i: (0, 0))],
        scratch_shapes=[pltpu.VMEM((tm, d), BF16), pltpu.VMEM((ZB_WIDTH, d), BF16),
                        pltpu.VMEM((2, W_CHUNK, d), F32), pltpu.SemaphoreType.DMA((2,))],
        compiler_params=_params(("arbitrary",)),
        name="proj_kv",
    )(x2, g.reshape(1, d), w_t3, *tabs, a4, xs2)


def _pool_pages_kernel(pt_sm, a_ref, cache_ref, o_ref, buf, sem, *, pages_per_step, page_base):
    step = pl.program_id(0)
    n_steps = pl.num_programs(0)
    slot = step % 2
    n_cols = buf.shape[3]

    def page_copy(step_idx, p, to_slot):
        page = pt_sm[step_idx * pages_per_step + p] + page_base
        return pltpu.make_async_copy(cache_ref.at[page, :, pl.ds(0, n_cols), :], buf.at[to_slot, p], sem.at[to_slot])

    @pl.when(step == 0)
    def _():
        for p in range(pages_per_step):
            page_copy(0, p, 0).start()

    @pl.when(step + 1 < n_steps)
    def _():
        for p in range(pages_per_step):
            page_copy(step + 1, p, 1 - slot).start()

    for p in range(pages_per_step):
        page_copy(step, p, slot).wait()

    a = a_ref[...]
    per = buf.shape[2] // CMP_BLOCK
    for p in range(pages_per_step):
        for k in range(per):
            x = buf[slot, p, pl.ds(k * CMP_BLOCK, CMP_BLOCK)]
            o_ref[0, p, k] = jnp.sum(x * a, axis=0)


def _pool_pages(cache4, page_flat, a3, *, bs, n_pages, page_base, pages_per_step=16):
    page = cache4.shape[1]
    per = page // CMP_BLOCK
    n_cols = 2 * NSA_KV
    total = bs * n_pages
    pages_per_step = min(pages_per_step, total)
    assert total % pages_per_step == 0
    n_steps = total // pages_per_step
    grid_spec = pltpu.PrefetchScalarGridSpec(
        num_scalar_prefetch=1, grid=(n_steps,),
        in_specs=[pl.BlockSpec((CMP_BLOCK, n_cols, LANE), lambda si, pt: (0, 0, 0)),
                  pl.BlockSpec(memory_space=pl.ANY)],
        out_specs=pl.BlockSpec((1, pages_per_step, per, n_cols, LANE), lambda si, pt: (si, 0, 0, 0, 0)),
        scratch_shapes=[pltpu.VMEM((2, pages_per_step, page, n_cols, LANE), F32),
                        pltpu.SemaphoreType.DMA((2,))])
    out = pl.pallas_call(
        functools.partial(_pool_pages_kernel, pages_per_step=pages_per_step, page_base=page_base),
        out_shape=jax.ShapeDtypeStruct((n_steps, pages_per_step, per, n_cols, LANE), F32),
        grid_spec=grid_spec,
        compiler_params=_params(("arbitrary",)),
        name="pool_pages",
    )(page_flat, a3, cache4)
    return out.reshape(bs, n_pages * per, n_cols * LANE)


def _cmp_proj_kernel(p_ref, w_ref, kc_ref, vc_ref):
    pooled = p_ref[0]
    n = pooled.shape[0]
    n_pad = kc_ref.shape[2]
    for c in range(4):
        r = _nn(pooled[:, c * LANE:(c + 1) * LANE].astype(BF16), w_ref[c]).astype(BF16)
        dst = kc_ref if c < 2 else vc_ref
        if n_pad > n:
            dst[0, c % 2] = jnp.zeros((n_pad, LANE), BF16)
        dst[0, c % 2, 0:n, :] = r


def _cmp_proj(pooled, w4):
    b, n, _ = pooled.shape
    n_pad = -(-n // LANE) * LANE
    spec = pl.BlockSpec((1, NSA_KV, n_pad, LANE), lambda bi: (bi, 0, 0, 0))
    return pl.pallas_call(
        _cmp_proj_kernel,
        out_shape=[jax.ShapeDtypeStruct((b, NSA_KV, n_pad, LANE), BF16)] * 2,
        grid=(b,),
        in_specs=[pl.BlockSpec((1, n, 512), lambda bi: (bi, 0, 0)),
                  pl.BlockSpec((4, LANE, LANE), lambda bi: (0, 0, 0))],
        out_specs=[spec, spec],
        compiler_params=_params(("parallel",)),
        name="cmp_proj",
    )(pooled, w4)


def _lane_parts(x):
    return [x[:, j * LANE:(j + 1) * LANE] for j in range(x.shape[1] // LANE)]


def _nsa_prompt_kernel(*refs, t_len, tq, tc, tw, n_sel, top, sub, pool_pages, page_base):
    if pool_pages:
        (pt_sm, qc_ref, qr_ref, ng_ref, bg_ref, ksel_ref, vsel_ref, kwin_ref, vwin_ref, kc_ref, vc_ref,
         pa_ref, cache_ref, o_ref, pool_ref,
         qc_scr, qr_scr, s_scr, p_scr, a_scr, m_scr, l_scr, acc_scr, oc_scr, b_scr,
         sw_scr, pw_scr, bw_scr, ow_scr, wl_scr, pbuf, psem) = refs
        step = (pl.program_id(0) * pl.num_programs(1) + pl.program_id(1)) * pl.num_programs(2) + pl.program_id(2)
        n_steps = pl.num_programs(0) * pl.num_programs(1) * pl.num_programs(2)
        slot = step % 2
        half_rows, n_cols = pbuf.shape[2], 2 * NSA_KV

        def page_copies(step_idx, p, to_slot):
            page = pt_sm[step_idx * pool_pages + p] + page_base
            return [pltpu.make_async_copy(cache_ref.at[page, pl.ds(hh * half_rows, half_rows), pl.ds(0, n_cols), :],
                                          pbuf.at[to_slot, p, :, pl.ds(hh * n_cols, n_cols), :], psem.at[to_slot])
                    for hh in range(2)]

        @pl.when(step == 0)
        def _():
            for p in range(pool_pages):
                for cp in page_copies(0, p, 0):
                    cp.start()

        @pl.when(step + 1 < n_steps)
        def _():
            for p in range(pool_pages):
                for cp in page_copies(step + 1, p, 1 - slot):
                    cp.start()

        for p in range(pool_pages):
            for cp in page_copies(step, p, slot):
                cp.wait()
        pa = pa_ref[...]
        for p in range(pool_pages):
            pool_ref[0, p] = jnp.sum(pbuf[slot, p] * pa, axis=0)
    else:
        (qc_ref, qr_ref, ng_ref, bg_ref, ksel_ref, vsel_ref, kwin_ref, vwin_ref, kc_ref, vc_ref,
         o_ref, qc_scr, qr_scr, s_scr, p_scr, a_scr, m_scr, l_scr, acc_scr, oc_scr, b_scr,
         sw_scr, pw_scr, bw_scr, ow_scr, wl_scr) = refs
    i = pl.program_id(2)
    hq = NSA_HPG
    for h in range(hq):
        qc_scr[pl.ds(h * tq, tq), :] = qc_ref[0, :, h * LANE:(h + 1) * LANE]
        qr_scr[pl.ds(h * tq, tq), 0:LANE] = qr_ref[0, :, h * LANE:(h + 1) * LANE]

    w0 = pl.multiple_of(jnp.clip(i * tq + tq - tw, 0, t_len - tw), LANE)
    kp = w0 + lax.broadcasted_iota(jnp.int32, (tq, tw), 1)
    tp = i * tq + lax.broadcasted_iota(jnp.int32, (tq, tw), 0)
    bw_scr[...] = jnp.where((kp <= tp) & (kp > tp - WINDOW), 0.0, MASK_NEG)
    sw_scr[...] = _nt(qr_scr[:, 0:LANE], kwin_ref[0, pl.ds(w0, tw), :])
    for r0 in range(0, hq * tq, sub):
        r = pl.ds(r0, sub)
        parts = _lane_parts(sw_scr[r, :] + bw_scr[pl.ds(r0 % tq, sub), :])
        m = jnp.max(functools.reduce(jnp.maximum, parts), axis=-1, keepdims=True)
        ews = [jnp.exp2(x - m) for x in parts]
        row_sum = jnp.sum(functools.reduce(jnp.add, ews), axis=-1, keepdims=True)
        wl_scr[r, :] = jnp.broadcast_to(1.0 / row_sum, (sub, LANE))
        pw_scr[r, :] = jnp.concatenate(ews, axis=1).astype(BF16)
    ow_scr[...] = _nn(pw_scr[...], vwin_ref[0, pl.ds(w0, tw), :])

    kc = kc_ref[0, 0]
    npad = kc.shape[0]
    s_scr[:, 0:npad] = _nt(qc_scr[...], kc)
    tpos = i * tq + lax.broadcasted_iota(jnp.int32, (tq, npad), 0)
    ncol = lax.broadcasted_iota(jnp.int32, (tq, npad), 1)
    cmask = (ncol + 1) * CMP_BLOCK <= tpos + 1
    imp = jnp.zeros((tq, npad), F32)
    for h in range(hq):
        r = pl.ds(h * tq, tq)
        s = jnp.where(cmask, s_scr[r, 0:npad], MASK_NEG)
        e = jnp.where(cmask, jnp.exp2(s - jnp.max(s, axis=-1, keepdims=True)), 0.0)
        p = e / jnp.maximum(jnp.sum(e, axis=-1, keepdims=True), 1e-30)
        p_scr[r, 0:npad] = p.astype(BF16)
        imp = imp + p
    oc_scr[...] = _nn(p_scr[:, 0:npad], vc_ref[0, 0])

    rows = min(npad, -(-n_sel // 8) * 8)
    imp_t = imp.T[0:rows]
    blk = lax.broadcasted_iota(jnp.int32, (rows, tq), 0)
    cur = lax.shift_right_logical(i * tq + lax.broadcasted_iota(jnp.int32, (rows, tq), 1), SEL_SHIFT)
    imp_t = jnp.where((blk == 0) | (blk == cur) | (blk == cur - 1), FORCE, imp_t)
    imp_t = jnp.where(blk > cur, -1.0, imp_t)
    imp_t = jnp.where(blk >= n_sel, -2.0, imp_t)
    rank = jnp.zeros((rows, tq), F32)
    for j in range(n_sel):
        a = imp_t[j:j + 1, :]
        ahead = (a > imp_t) | ((a == imp_t) & (blk > j))
        rank = rank + jnp.where(ahead, 1.0, 0.0)
    neg_t = jnp.where((rank < top) & (blk < n_sel) & (blk <= cur), 0.0, MASK_NEG)
    if npad > rows:
        neg_t = jnp.concatenate([neg_t, jnp.zeros((npad - rows, tq), F32)], axis=0)
    neg = neg_t.T.astype(BF16)
    for h in range(hq):
        qr_scr[pl.ds(h * tq, tq), LANE:2 * LANE] = neg

    m_scr[...] = jnp.full(m_scr.shape, MASK_NEG, F32)
    l_scr[...] = jnp.zeros(l_scr.shape, F32)
    acc_scr[...] = jnp.zeros(acc_scr.shape, F32)
    n_chunks = (i * tq + tq + tc - 1) // tc

    def chunk(ci, diagonal):
        k0 = pl.multiple_of(ci * tc, tc)
        v = vsel_ref[0, pl.ds(k0, tc), :]
        kb = lax.shift_right_logical(k0 + lax.broadcasted_iota(jnp.int32, (tc, npad), 0), SEL_SHIFT)
        jb = lax.broadcasted_iota(jnp.int32, (tc, npad), 1)
        k_aug = jnp.concatenate([ksel_ref[0, pl.ds(k0, tc), :], jnp.where(kb == jb, 1.0, 0.0).astype(BF16)], axis=1)
        s_scr[:, 0:tc] = _nt(qr_scr[...], k_aug)
        if diagonal:
            kp = k0 + lax.broadcasted_iota(jnp.int32, (tq, tc), 1)
            tp = i * tq + lax.broadcasted_iota(jnp.int32, (tq, tc), 0)
            b_scr[:, 0:tc] = jnp.where(kp <= tp, 0.0, MASK_NEG)
        for r0 in range(0, hq * tq, sub):
            r = pl.ds(r0, sub)
            sc = s_scr[r, 0:tc]
            if diagonal:
                sc = sc + b_scr[pl.ds(r0 % tq, sub), 0:tc]
            parts = _lane_parts(sc)
            m_prev = m_scr[r, :]
            m_new = jnp.maximum(m_prev, jnp.max(functools.reduce(jnp.maximum, parts), axis=-1, keepdims=True))
            alpha = jnp.exp2(m_prev - m_new)
            pes = [jnp.exp2(x - m_new) for x in parts]
            l_scr[r, :] = alpha * l_scr[r, :] + functools.reduce(jnp.add, pes)
            p_scr[r, 0:tc] = jnp.concatenate(pes, axis=1).astype(BF16)
            a_scr[r, :] = alpha
            m_scr[r, :] = m_new
        acc_scr[...] = a_scr[...] * acc_scr[...] + _nn(p_scr[:, 0:tc], v)

    def full_chunk(ci, carry):
        chunk(ci, False)
        return carry

    lax.fori_loop(0, n_chunks - 1, full_chunk, 0)
    chunk(n_chunks - 1, True)

    gate = jax.nn.sigmoid(bg_ref[0])
    for g in range(1, NSA_KV):
        gate = jnp.where(pl.program_id(1) == g, pltpu.roll(gate, LANE - g * hq * N_BRANCH, 1), gate)
    ng = ng_ref[0]
    for h in range(hq):
        r = slice(h * tq, (h + 1) * tq)
        o_s = acc_scr[r, :] * (1.0 / jnp.sum(l_scr[r, :], axis=-1, keepdims=True))
        o = (gate[:, 3 * h:3 * h + 1] * oc_scr[r, :] + gate[:, 3 * h + 1:3 * h + 2] * o_s
             + gate[:, 3 * h + 2:3 * h + 3] * (ow_scr[r, :] * wl_scr[r, :]))
        o_ref[0, :, h * LANE:(h + 1) * LANE] = (ng[:, h * LANE:(h + 1) * LANE] * o).astype(o_ref.dtype)


def _nsa_prompt(qc3, qr3, ng3, bg3, kvb, kc, vc, *, tq=128, tc=512, sub=64, pool=None):
    b, t, _ = qc3.shape
    nq = t // tq
    tc = min(tc, t)
    tw = min(WINDOW + tq, t)
    n_sel = t // SEL_BLOCK
    top = min(TOP_N, n_sel)
    npad = kc.shape[2]
    gw = NSA_HPG * LANE
    rows = NSA_HPG * tq
    wide = max(tc, npad)
    kvspec = lambda k: pl.BlockSpec((1, t, LANE), lambda bi, gi, qi, *_, k=k: (bi, 0, k + gi))
    cspec = pl.BlockSpec((1, 1, npad, LANE), lambda bi, gi, qi, *_: (bi, gi, 0, 0))
    gspec = pl.BlockSpec((1, tq, gw), lambda bi, gi, qi, *_: (bi, qi, gi))
    in_specs = [gspec, gspec, gspec,
                pl.BlockSpec((1, tq, LANE), lambda bi, gi, qi, *_: (bi, qi, 0)),
                kvspec(0), kvspec(2), kvspec(4), kvspec(6), cspec, cspec]
    out_shape = [jax.ShapeDtypeStruct((b, t, NSA_HEADS * LANE), BF16)]
    out_specs = [pl.BlockSpec((1, tq, gw), lambda bi, gi, qi, *_: (bi, qi, gi))]
    assert tc % tq == 0 and npad == LANE
    scratch = ([pltpu.VMEM((rows, LANE), BF16), pltpu.VMEM((rows, 2 * LANE), BF16)]
               + [pltpu.VMEM((rows, wide), F32), pltpu.VMEM((rows, wide), BF16)]
               + [pltpu.VMEM((rows, LANE), F32)] * 5 + [pltpu.VMEM((tq, wide), F32)]
               + [pltpu.VMEM((rows, tw), F32), pltpu.VMEM((rows, tw), BF16), pltpu.VMEM((tq, tw), F32)]
               + [pltpu.VMEM((rows, LANE), F32)] * 2)
    args = [qc3, qr3, ng3, bg3, kvb, kvb, kvb, kvb, kc, vc]
    n_steps = b * NSA_KV * nq
    pool_pages, page_base, prefetch = 0, 0, []
    if pool is not None:
        cache4, page_flat, a3, page_base = pool
        page, n_cols = cache4.shape[1], 2 * NSA_KV
        if page_flat.shape[0] % n_steps == 0 and page == 2 * CMP_BLOCK and 2 * n_cols == 8:
            pool_pages = page_flat.shape[0] // n_steps
            prefetch = [page_flat]
            in_specs += [pl.BlockSpec((CMP_BLOCK, 2 * n_cols, LANE), lambda bi, gi, qi, *_: (0, 0, 0)),
                         pl.BlockSpec(memory_space=pl.ANY)]
            args += [jnp.concatenate([a3, a3], axis=1), cache4]
            out_shape.append(jax.ShapeDtypeStruct((n_steps, pool_pages, 2 * n_cols, LANE), F32))
            out_specs.append(pl.BlockSpec((1, pool_pages, 2 * n_cols, LANE),
                                          lambda bi, gi, qi, *_: ((bi * NSA_KV + gi) * nq + qi, 0, 0, 0)))
            scratch += [pltpu.VMEM((2, pool_pages, CMP_BLOCK, 2 * n_cols, LANE), F32), pltpu.SemaphoreType.DMA((2,))]
    kern = functools.partial(_nsa_prompt_kernel, t_len=t, tq=tq, tc=tc, tw=tw, n_sel=n_sel, top=top,
                             sub=min(sub, tq), pool_pages=pool_pages, page_base=page_base)
    res = pl.pallas_call(
        kern,
        out_shape=out_shape,
        grid_spec=pltpu.PrefetchScalarGridSpec(
            num_scalar_prefetch=len(prefetch), grid=(b, NSA_KV, nq),
            in_specs=in_specs, out_specs=out_specs, scratch_shapes=scratch),
        compiler_params=_params(("arbitrary", "arbitrary", "arbitrary")),
        name="nsa_prompt",
    )(*prefetch, *args)
    return (res[0], res[1]) if pool_pages else (res[0], None)


def _conv_kernel(h_ref, b_ref, c_ref, g_ref, prev_ref, w_ref, y_ref, st_ref, up_scr, *, t_real):
    u = c_ref[...] * h_ref[...]
    t = u.shape[1]
    up_scr[:, pl.ds(8 - (CONV_W - 1), CONV_W - 1), :] = prev_ref[...]
    up_scr[:, pl.ds(8, t), :] = u
    w = w_ref[...]
    y = w[0:1, :][None] * up_scr[:, pl.ds(6, t), :]
    y = y + w[1:2, :][None] * up_scr[:, pl.ds(7, t), :]
    y = y + w[2:3, :][None] * u
    y = b_ref[...] * y
    y_ref[...] = (_silu(g_ref[...]) * y).astype(y_ref.dtype)
    st_ref[...] = up_scr[:, pl.ds(6 + t_real, CONV_W - 1), :]


def _conv_mixer(z3, prev, w_conv, *, t_real):
    b, t, _ = z3.shape
    nc = CONV_DIM // LANE
    zspec = lambda off: pl.BlockSpec((b, t, LANE), lambda ci, off=off: (0, 0, off // LANE + ci))
    return pl.pallas_call(
        functools.partial(_conv_kernel, t_real=t_real),
        out_shape=[jax.ShapeDtypeStruct((b, t, CONV_DIM), BF16),
                   jax.ShapeDtypeStruct((b, CONV_W - 1, CONV_DIM), F32)],
        grid=(nc,),
        in_specs=[zspec(C_H), zspec(C_B), zspec(C_C), zspec(C_G),
                  pl.BlockSpec((b, CONV_W - 1, LANE), lambda ci: (0, 0, ci)),
                  pl.BlockSpec((CONV_W, LANE), lambda ci: (0, ci))],
        out_specs=[pl.BlockSpec((b, t, LANE), lambda ci: (0, 0, ci)),
                   pl.BlockSpec((b, CONV_W - 1, LANE), lambda ci: (0, 0, ci))],
        scratch_shapes=[pltpu.VMEM((b, t + 8, LANE), F32)],
        compiler_params=_params(("parallel",)),
        name="conv_mixer",
    )(z3, z3, z3, z3, prev, w_conv)


def _mem_attn_kernel(q_ref, mg_ref, kv_ref, o_ref, *, interleaved):
    q = q_ref[0]
    mg = mg_ref[0]
    half = MEM_HEADS * LANE
    for h in range(MEM_HEADS):
        lo, hi = h * LANE, (h + 1) * LANE
        if interleaved:
            nm = kv_ref.shape[1] // (2 * MEM_HEADS)
            k = kv_ref[0, pl.ds(h, nm, stride=2 * MEM_HEADS), :].astype(BF16)
            v = kv_ref[0, pl.ds(MEM_HEADS + h, nm, stride=2 * MEM_HEADS), :].astype(BF16)
        else:
            k = kv_ref[0, :, lo:hi].astype(BF16)
            v = kv_ref[0, :, half + lo:half + hi].astype(BF16)
        s = _nt((q[:, lo:hi] * ATTN_SCALE).astype(BF16), k)
        e = jnp.exp(s - jnp.max(s, axis=-1, keepdims=True))
        o = _nn(e.astype(BF16), v) / jnp.sum(e, axis=-1, keepdims=True)
        o_ref[0, :, lo:hi] = (_silu(mg[:, lo:hi]) * o).astype(o_ref.dtype)


def _mem_attn(zb3, mkv, *, mq_off, mg_off, tq, interleaved, kv_base=0):
    b, t, _ = zb3.shape
    wq = MEM_HEADS * LANE
    return pl.pallas_call(
        functools.partial(_mem_attn_kernel, interleaved=interleaved),
        out_shape=jax.ShapeDtypeStruct((b, t, wq), BF16),
        grid=(b, t // tq),
        in_specs=[pl.BlockSpec((1, tq, wq), lambda bi, ti: (bi, ti, mq_off // wq)),
                  pl.BlockSpec((1, tq, wq), lambda bi, ti: (bi, ti, mg_off // wq)),
                  pl.BlockSpec((1,) + mkv.shape[1:], lambda bi, ti: (bi + kv_base, 0, 0))],
        out_specs=pl.BlockSpec((1, tq, wq), lambda bi, ti: (bi, ti, 0)),
        compiler_params=_params(("parallel", "parallel")),
        name="mem_attn",
    )(zb3, zb3, mkv)


def _out_proj_kernel(x_ref, ya_ref, yb_ref, ym_ref, xs_ref, yas_ref, ybs_ref, yms_ref, w_ref, fg_ref,
                     o_ref, os_ref, w_scr, *, final):
    @pl.when(pl.program_id(0) == 0)
    def _():
        w_scr[...] = w_ref[...].astype(BF16)

    a, bw = CONV_DIM, CONV_DIM + NSA_HEADS * LANE

    def mix(x, ya, yb, ym):
        acc = _nn(ya, w_scr[0:a, :])
        acc = acc + _nn(yb, w_scr[a:bw, :])
        acc = acc + _nn(ym, w_scr[bw:, :])
        r = x + acc
        if final:
            r = r * lax.rsqrt(jnp.mean(r * r, axis=-1, keepdims=True) + NORM_EPS) * fg_ref[...]
        return r

    o_ref[...] = mix(x_ref[...], ya_ref[...], yb_ref[...], ym_ref[...])

    @pl.when(pl.program_id(0) == pl.num_programs(0) - 1)
    def _():
        os_ref[...] = mix(xs_ref[...], yas_ref[...], ybs_ref[...], yms_ref[...])


def _out_proj(x, ya, yb, ym, sample, w, fg, *, tm, final):
    m, d = x.shape
    row = lambda width: pl.BlockSpec((tm, width), lambda i: (i, 0))
    whole = lambda a: pl.BlockSpec(a.shape, lambda i: (0, 0), pipeline_mode=pl.Buffered(1))
    return pl.pallas_call(
        functools.partial(_out_proj_kernel, final=final),
        out_shape=[jax.ShapeDtypeStruct((m, d), F32), jax.ShapeDtypeStruct(sample[0].shape, F32)],
        grid=(m // tm,),
        in_specs=[row(d), row(ya.shape[1]), row(yb.shape[1]), row(ym.shape[1])] + [whole(a) for a in sample]
                 + [whole(w), pl.BlockSpec((1, d), lambda i: (0, 0), pipeline_mode=pl.Buffered(1))],
        out_specs=[row(d), pl.BlockSpec(sample[0].shape, lambda i: (0, 0))],
        scratch_shapes=[pltpu.VMEM(w.shape, BF16)],
        compiler_params=_params(("arbitrary",)),
        name="out_proj",
    )(x, ya, yb, ym, *sample, w, fg.reshape(1, d))


def _sample_pre_kernel(q_ref, ng_ref, bg_ref, kv0_ref, kv1_ref, kv2_ref, c_ref, s1_ref, s2_ref, kc_ref, vc_ref,
                       kvn_ref, wn_ref, qr_ref, oc_ref, ngo_ref, gate_ref, val_ref, *, past):
    tp = SAMPLE_T_PAD
    hq = NSA_HPG
    c, s1, s2 = c_ref[...], s1_ref[...], s2_ref[...]
    kv0, kv1, kv2 = kv0_ref[0], kv1_ref[0], kv2_ref[0]
    kvn_ref[0, :, 0:512] = kv0
    kvn_ref[0, :, 768:1024] = kv1[:, 256:512]
    wn_ref[0, :, 256:512] = kv2[:, 256:512]
    for g in range(NSA_KV):
        lo, hi = g * LANE, (g + 1) * LANE
        kvn_ref[0, :, 512 + lo:512 + hi] = _rope(kv1[:, lo:hi], c, s1, s2)
        wn_ref[0, :, lo:hi] = _rope(kv2[:, lo:hi], c, s1, s2)

    q = q_ref[0]
    ng = ng_ref[0]
    gates = jax.nn.sigmoid(bg_ref[0])
    for g in range(NSA_KV):
        qc_l, qr_l = [], []
        for h in range(hq):
            lo = (g * hq + h) * LANE
            qh = q[:, lo:lo + LANE]
            qc_l.append(qh * ATTN_SCALE)
            qr_l.append(_rope(qh, c, s1, s2) * ATTN_SCALE)
            ngo_ref[0, g, h * tp:(h + 1) * tp, :] = ng[:, lo:lo + LANE]
            for br in range(N_BRANCH):
                col = (g * hq + h) * N_BRANCH + br
                gate_ref[0, g, br, h * tp:(h + 1) * tp, :] = jnp.broadcast_to(gates[:, col:col + 1], (tp, LANE))
        qc = jnp.concatenate(qc_l, axis=0)
        qr_ref[0, g] = jnp.concatenate(qr_l, axis=0)

        kc = kc_ref[0, g]
        npad = kc.shape[0]
        s = _nt(qc.astype(BF16), kc)
        trow = lax.broadcasted_iota(jnp.int32, (hq * tp, npad), 0) % tp
        ncol = lax.broadcasted_iota(jnp.int32, (hq * tp, npad), 1)
        cmask = (ncol + 1) * CMP_BLOCK <= past + trow + 1
        s = jnp.where(cmask, s, MASK_NEG)
        e = jnp.where(cmask, jnp.exp(s - jnp.max(s, axis=-1, keepdims=True)), 0.0)
        p = e / jnp.maximum(jnp.sum(e, axis=-1, keepdims=True), 1e-30)
        oc_ref[0, g] = _nn(p.astype(BF16), vc_ref[0, g])
        imp = jnp.sum(p.reshape(hq, tp, npad), axis=0)

        blk = lax.broadcasted_iota(jnp.int32, (tp, npad), 1)
        cur = (past + lax.broadcasted_iota(jnp.int32, (tp, npad), 0)) // SEL_BLOCK
        val = jnp.where((blk == 0) | (blk == cur) | (blk == cur - 1), FORCE, imp)
        val = jnp.where(blk > cur, -1.0, val)
        val = jnp.where(blk >= past // SEL_BLOCK, -2.0, val)
        val_ref[0, g] = val


def _topk_kernel(val_ref, idx_ref, *, n_top):
    val = val_ref[...]
    rows, n = val.shape
    blk = lax.broadcasted_iota(jnp.int32, (rows, n), 1)
    lane = lax.broadcasted_iota(jnp.int32, (rows, LANE), 1)
    idx = jnp.zeros((rows, LANE), jnp.int32)
    for r in range(n_top):
        best = jnp.max(val, axis=-1, keepdims=True)
        j = jnp.min(jnp.where(val == best, blk, n), axis=-1, keepdims=True)
        idx = jnp.where(lane == r, j, idx)
        val = jnp.where(blk == j, -3e38, val)
    idx_ref[...] = idx


def _topk(val2, *, n_top):
    rows, n = val2.shape
    return pl.pallas_call(
        functools.partial(_topk_kernel, n_top=n_top),
        out_shape=jax.ShapeDtypeStruct((rows, LANE), jnp.int32),
        grid=(1,),
        in_specs=[pl.BlockSpec((rows, n), lambda i: (0, 0))],
        out_specs=pl.BlockSpec((rows, LANE), lambda i: (0, 0)),
        compiler_params=_params(("arbitrary",)),
        name="sample_topk",
    )(val2)


def _sample_pre(za3, zb3, bg3, tabs, kc, vc, *, past):
    bs, tp, _ = za3.shape
    npad = kc.shape[2]
    qw = NSA_HEADS * LANE
    kvblk = KV_OFF // 512
    zspec = lambda k: pl.BlockSpec((1, tp, 512), lambda bi, k=k: (bi, 0, kvblk + k))
    tspec = pl.BlockSpec((tp, LANE), lambda bi: (0, 0))
    cspec = pl.BlockSpec((1, NSA_KV, npad, LANE), lambda bi: (bi, 0, 0, 0))
    rows = NSA_HPG * tp
    gspec = pl.BlockSpec((1, NSA_KV, rows, LANE), lambda bi: (bi, 0, 0, 0))
    gshape = jax.ShapeDtypeStruct((bs, NSA_KV, rows, LANE), F32)
    return pl.pallas_call(
        functools.partial(_sample_pre_kernel, past=past),
        out_shape=[jax.ShapeDtypeStruct((bs, tp, 1024), F32),
                   jax.ShapeDtypeStruct((bs, tp, 512), F32),
                   gshape, gshape, gshape,
                   jax.ShapeDtypeStruct((bs, NSA_KV, N_BRANCH, rows, LANE), F32),
                   jax.ShapeDtypeStruct((bs, NSA_KV, tp, npad), F32)],
        grid=(bs,),
        in_specs=[pl.BlockSpec((1, tp, qw), lambda bi: (bi, 0, Q_OFF // qw)),
                  pl.BlockSpec((1, tp, qw), lambda bi: (bi, 0, NG_OFF // qw)),
                  pl.BlockSpec((1, tp, LANE), lambda bi: (bi, 0, 0)),
                  zspec(0), zspec(1), zspec(2), tspec, tspec, tspec, cspec, cspec],
        out_specs=[pl.BlockSpec((1, tp, 1024), lambda bi: (bi, 0, 0)),
                   pl.BlockSpec((1, tp, 512), lambda bi: (bi, 0, 0)),
                   gspec, gspec, gspec,
                   pl.BlockSpec((1, NSA_KV, N_BRANCH, rows, LANE), lambda bi: (bi, 0, 0, 0, 0)),
                   pl.BlockSpec((1, NSA_KV, tp, npad), lambda bi: (bi, 0, 0, 0))],
        compiler_params=_params(("parallel",)),
        name="sample_pre",
    )(za3, za3, bg3, zb3, zb3, zb3, *tabs, kc, vc)


def _sample_attn_kernel(idx_sm, pt_sm, qr_ref, oc_ref, ng_ref, gate_ref, ksn_ref, vsn_ref,
                        wc_ref, kwn_ref, vwn_ref, cache_ref, o_ref,
                        kbuf, vbuf, kw_scr, vw_scr, sem, *, ts, n_top, n_pages, page_base, per_page, wb):
    tp = SAMPLE_T_PAD
    hq = NSA_HPG
    b = pl.program_id(0)
    g = pl.program_id(1)
    n_gath = n_top * SEL_BLOCK
    ks_rows = kbuf.shape[2]
    step = b * NSA_KV + g
    n_steps = pl.num_programs(0) * NSA_KV
    slot = step % 2

    def gather_copies(step_idx, t, r, to_slot):
        bb, gg = step_idx // NSA_KV, step_idx % NSA_KV
        blk = idx_sm[(step_idx * ts + t) * n_top + r]
        page = pt_sm[bb * n_pages + blk // per_page] + page_base
        row0 = (blk % per_page) * SEL_BLOCK
        src_k = cache_ref.at[page, pl.ds(row0, SEL_BLOCK), 2 * NSA_KV + gg]
        src_v = cache_ref.at[page, pl.ds(row0, SEL_BLOCK), 3 * NSA_KV + gg]
        dst = pl.ds(r * SEL_BLOCK, SEL_BLOCK)
        return (pltpu.make_async_copy(src_k, kbuf.at[to_slot, t, dst], sem.at[to_slot, 0]),
                pltpu.make_async_copy(src_v, vbuf.at[to_slot, t, dst], sem.at[to_slot, 1]))

    def start_gathers(step_idx, to_slot):
        for t in range(ts):
            for r in range(n_top):
                ck, cv = gather_copies(step_idx, t, r, to_slot)
                ck.start()
                cv.start()

    @pl.when(step == 0)
    def _():
        start_gathers(0, 0)

    @pl.when(step + 1 < n_steps)
    def _():
        start_gathers(step + 1, 1 - slot)

    qr = qr_ref[0, 0].astype(BF16)
    trow = lax.broadcasted_iota(jnp.int32, (hq * tp, 1), 0) % tp

    ww = kw_scr.shape[0]
    kw_scr[pl.ds(0, wb), :] = wc_ref[0, pl.ds(g, wb, stride=2 * NSA_KV), :]
    vw_scr[pl.ds(0, wb), :] = wc_ref[0, pl.ds(NSA_KV + g, wb, stride=2 * NSA_KV), :]
    kw_scr[pl.ds(wb, tp), :] = kwn_ref[0]
    vw_scr[pl.ds(wb, tp), :] = vwn_ref[0]
    kw_scr[pl.ds(wb + tp, ww - wb - tp), :] = jnp.zeros((ww - wb - tp, LANE), F32)
    vw_scr[pl.ds(wb + tp, ww - wb - tp), :] = jnp.zeros((ww - wb - tp, LANE), F32)
    sw = _nt(qr, kw_scr[...].astype(BF16))
    jw = lax.broadcasted_iota(jnp.int32, (hq * tp, ww), 1)
    rel = jw - wb
    okw = (rel <= trow) & (rel > trow - WINDOW) & (jw < wb + ts)
    sw = jnp.where(okw, sw, MASK_NEG)
    ew = jnp.where(okw, jnp.exp(sw - jnp.max(sw, axis=-1, keepdims=True)), 0.0)
    o_w = _nn(ew.astype(BF16), vw_scr[...].astype(BF16)) / jnp.sum(ew, axis=-1, keepdims=True)

    for t in range(ts):
        for r in range(n_top):
            ck, cv = gather_copies(step, t, r, slot)
            ck.wait()
            cv.wait()

    js = lax.broadcasted_iota(jnp.int32, (hq * tp, ks_rows), 1)
    o_s = jnp.zeros((hq * tp, LANE), F32)
    for t in range(ts):
        kbuf[slot, t, pl.ds(n_gath, tp), :] = ksn_ref[0]
        vbuf[slot, t, pl.ds(n_gath, tp), :] = vsn_ref[0]
        kbuf[slot, t, pl.ds(n_gath + tp, ks_rows - n_gath - tp), :] = jnp.zeros((ks_rows - n_gath - tp, LANE), F32)
        vbuf[slot, t, pl.ds(n_gath + tp, ks_rows - n_gath - tp), :] = jnp.zeros((ks_rows - n_gath - tp, LANE), F32)
        ss = _nt(qr, kbuf[slot, t].astype(BF16))
        oks = (js < n_gath) | ((js - n_gath <= t) & (js < n_gath + ts))
        ss = jnp.where(oks, ss, MASK_NEG)
        es = jnp.where(oks, jnp.exp(ss - jnp.max(ss, axis=-1, keepdims=True)), 0.0)
        ot = _nn(es.astype(BF16), vbuf[slot, t].astype(BF16)) / jnp.sum(es, axis=-1, keepdims=True)
        o_s = jnp.where(trow == t, ot, o_s)

    o = gate_ref[0, 0, 0] * oc_ref[0, 0] + gate_ref[0, 0, 1] * o_s + gate_ref[0, 0, 2] * o_w
    y = _silu(ng_ref[0, 0]) * o
    for h in range(hq):
        o_ref[0, :, h * LANE:(h + 1) * LANE] = y[h * tp:(h + 1) * tp].astype(o_ref.dtype)


def _sample_attn(idx_flat, page_flat, qr, oc, ngo, gates, kvn, cache_win_rows, wn, cache4, *,
                 ts, n_top, n_pages, page_base, win_base):
    bs = qr.shape[0]
    tp = SAMPLE_T_PAD
    rows = NSA_HPG * tp
    wb = cache_win_rows.shape[1] // (2 * NSA_KV)
    per_page = cache4.shape[1] // SEL_BLOCK
    ks_rows = -(-(n_top * SEL_BLOCK + tp) // LANE) * LANE
    ww = -(-(wb + tp) // LANE) * LANE
    gspec = pl.BlockSpec((1, 1, rows, LANE), lambda bi, gi, *_: (bi, gi, 0, 0))
    newspec = lambda k: pl.BlockSpec((1, tp, LANE), lambda bi, gi, *_, k=k: (bi, 0, k + gi))
    grid_spec = pltpu.PrefetchScalarGridSpec(
        num_scalar_prefetch=2, grid=(bs, NSA_KV),
        in_specs=[gspec, gspec, gspec,
                  pl.BlockSpec((1, 1, N_BRANCH, rows, LANE), lambda bi, gi, *_: (bi, gi, 0, 0, 0)),
                  newspec(2 * NSA_KV), newspec(3 * NSA_KV),
                  pl.BlockSpec((1,) + cache_win_rows.shape[1:], lambda bi, gi, *_: (bi + win_base, 0, 0)),
                  newspec(0), newspec(NSA_KV),
                  pl.BlockSpec(memory_space=pl.ANY)],
        out_specs=pl.BlockSpec((1, tp, NSA_HPG * LANE), lambda bi, gi, *_: (bi, 0, gi)),
        scratch_shapes=[pltpu.VMEM((2, ts, ks_rows, LANE), F32), pltpu.VMEM((2, ts, ks_rows, LANE), F32),
                        pltpu.VMEM((ww, LANE), F32), pltpu.VMEM((ww, LANE), F32),
                        pltpu.SemaphoreType.DMA((2, 2))])
    kern = functools.partial(_sample_attn_kernel, ts=ts, n_top=n_top, n_pages=n_pages,
                             page_base=page_base, per_page=per_page, wb=wb)
    return pl.pallas_call(
        kern,
        out_shape=jax.ShapeDtypeStruct((bs, tp, NSA_HEADS * LANE), BF16),
        grid_spec=grid_spec,
        compiler_params=_params(("arbitrary", "arbitrary")),
        name="sample_attn",
    )(idx_flat, page_flat, qr, oc, ngo, gates, kvn, kvn, cache_win_rows, wn, wn, cache4)


def _rope_tables(pos, rows):
    freqs = jnp.power(ROPE_THETA, -jnp.arange(ROPE_HALF, dtype=F32) * (2.0 / ROPE_DIM))
    ang = pos.astype(F32)[:, None] * freqs[None, :]
    cos, sin = jnp.cos(ang), jnp.sin(ang)
    n = pos.shape[0]
    z16 = jnp.zeros((n, ROPE_HALF), F32)
    rest = LANE - ROPE_DIM
    c = jnp.concatenate([cos, cos, jnp.ones((n, rest), F32)], axis=1)
    s1 = jnp.concatenate([z16, sin, jnp.zeros((n, rest), F32)], axis=1)
    s2 = jnp.concatenate([-sin, z16, jnp.zeros((n, rest), F32)], axis=1)
    pad = lambda a: jnp.pad(a, ((0, rows - n), (0, 0)))
    return pad(c), pad(s1), pad(s2)


def _layer_weights(norm_g, w_in3, layer, w_conv, a_cmp, w_cmp, w_out):
    w_t3 = jnp.swapaxes(w_in3, 1, 2)
    assert w_t3.shape[1] == BG_SRC + BG_N + ZB_WIDTH and BG_SRC == ZA_WIDTH and BG_N % 8 == 0
    a4 = jnp.concatenate([a_cmp[0], a_cmp[0], a_cmp[1], a_cmp[1]], axis=1)
    a3 = jnp.stack([a_cmp[0], a_cmp[0], a_cmp[1], a_cmp[1]], axis=1)
    w4 = jnp.stack([w_cmp[0], w_cmp[0], w_cmp[1], w_cmp[1]]).astype(BF16)
    return dict(norm_g=norm_g, w_t3=w_t3, layer=layer, w_conv=w_conv, a4=a4, a3=a3, w4=w4,
                w_out=w_out)


def _prompt_layer(xp, xs2, mem_prompt, mem_norm_g, w_mem, lw, pool):
    b, t, d = xp.shape
    m = b * t
    x2 = xp.reshape(m, d)
    tabs = _rope_tables(jnp.arange(t, dtype=jnp.int32), t)
    wb = min(WINDOW, t)
    qc, qr, ng, bg, ya, conv_new, za_s, bg_s = _proj_conv(x2, lw["norm_g"], lw["w_t3"], lw["layer"], lw["w_conv"],
                                                          tabs, xs2, tm=min(512, t), seq_len=t)
    kvn, kvb, win_rows, pooled_prompt, zm, zb_s = _proj_kv(x2, lw["norm_g"], lw["w_t3"], lw["layer"], tabs, lw["a4"],
                                                           xs2, tm=wb, seq_len=t)
    nm = mem_prompt.shape[1]
    mkv = _norm_matmul(mem_prompt.reshape(b * nm, d), mem_norm_g, w_mem, tm=min(512, b * nm), tn=512)
    mkv3 = mkv.reshape(b, nm, 2 * MEM_HEADS * LANE)
    kc, vc = _cmp_proj(pooled_prompt.reshape(b, t // CMP_BLOCK, -1), lw["w4"])
    r3 = lambda a: a.reshape(b, t, -1)
    yb, pooled = _nsa_prompt(r3(qc), r3(qr), r3(ng), r3(bg), r3(kvb), kc, vc, tq=min(256, t), pool=pool)
    ym = _mem_attn(zm.reshape(b, t, -1), mkv3, mq_off=0, mg_off=MG_OFF - MQ_OFF, tq=min(1024, t), interleaved=False)
    mix_in = (x2, ya, yb.reshape(m, -1), ym.reshape(m, -1))
    kv_new = kvn.reshape(b, t, 4, NSA_KV, HEAD_DIM)
    win_new = win_rows.reshape(b, wb, 2, NSA_KV, HEAD_DIM)
    mem_kv = mkv.reshape(b, nm, 2, MEM_HEADS, HEAD_DIM)
    return mix_in, kv_new, win_new, conv_new, mem_kv, pooled, (za_s, zb_s, bg_s)


def _sample_layer(xs_p, proj, ts, layer, cache4, page_flat, pooled, cache_win, state_conv, cache_mem, lw):
    bs, tp, d = xs_p.shape
    depth = cache_win.shape[0]
    pool, page = cache4.shape[0] // depth, cache4.shape[1]
    n_pages = page_flat.shape[0] // bs
    past = n_pages * page
    assert past % SEL_BLOCK == 0 and ts <= SEL_BLOCK and ts <= tp
    n_past = past // SEL_BLOCK
    n_top = min(TOP_N, n_past + 1) - 1
    m = bs * tp
    za, zb, bg = proj
    za3 = za.reshape(bs, tp, ZA_WIDTH)
    zb3 = zb.reshape(bs, tp, ZB_WIDTH)
    bg3 = bg.reshape(bs, tp, LANE)
    tabs = _rope_tables(past + jnp.arange(tp, dtype=jnp.int32), tp)
    if pooled is None:
        pooled = _pool_pages(cache4, page_flat, lw["a3"], bs=bs, n_pages=n_pages, page_base=layer * pool)
    else:
        pooled = pooled.reshape(bs, n_pages * (page // CMP_BLOCK), 2 * NSA_KV * LANE)
    kc, vc = _cmp_proj(pooled, lw["w4"])
    kvn, wn, qr, oc, ngo, gates, val = _sample_pre(za3, zb3, bg3, tabs, kc, vc, past=past)
    idx = _topk(val.reshape(bs * NSA_KV * tp, val.shape[-1]), n_top=n_top)
    idx_flat = idx.reshape(bs, NSA_KV, tp, LANE)[:, :, :ts, :n_top].reshape(-1)
    wbuf = cache_win.shape[2]
    cache_win_rows = cache_win.reshape(depth * bs, wbuf * 2 * NSA_KV, HEAD_DIM)
    yb = _sample_attn(idx_flat, page_flat, qr, oc, ngo, gates, kvn, cache_win_rows, wn, cache4,
                      ts=ts, n_top=n_top, n_pages=n_pages, page_base=layer * pool, win_base=layer * bs)
    ya, conv_new = _conv_mixer(za3, state_conv[layer], lw["w_conv"], t_real=ts)
    nm = cache_mem.shape[2]
    mem_rows = cache_mem.reshape(depth * bs, nm * 2 * MEM_HEADS, HEAD_DIM)
    ym = _mem_attn(zb3, mem_rows, mq_off=MQ_OFF, mg_off=MG_OFF, tq=tp, interleaved=True, kv_base=layer * bs)
    mix_in = (xs_p.reshape(m, d), ya.reshape(m, -1), yb.reshape(m, -1), ym.reshape(m, -1))
    kv_new = kvn[:, :ts].reshape(bs, ts, 4, NSA_KV, HEAD_DIM)
    win_rows = wn[:, :ts].reshape(bs, ts, 2, NSA_KV, HEAD_DIM)
    win_state = jnp.concatenate([cache_win[layer], win_rows], axis=1)[:, ts:]
    return mix_in, kv_new, win_state, conv_new


def kernel(x_prompt, x_sample, cache_kv, cache_win, state_conv, cache_mem, page_table, mem_prompt,
           norm_g, w_in, w_conv, a_cmp, w_cmp, mem_norm_g, w_mem_kv, w_out, final_g):
    depth = w_in.shape[0]
    ts = x_sample.shape[1]
    xp = x_prompt
    xs = jnp.pad(x_sample, ((0, 0), (0, SAMPLE_T_PAD - ts), (0, 0)))
    pool_size, page = cache_kv.shape[1], cache_kv.shape[2]
    cache4 = cache_kv.reshape(depth * pool_size, page, 4 * NSA_KV, HEAD_DIM)
    page_flat = page_table.reshape(-1).astype(jnp.int32)
    kv_p, win_p, conv_p, mem_p, kv_s, win_s, conv_s = [], [], [], [], [], [], []
    for l in range(depth):
        lw = _layer_weights(norm_g[l], w_in, l, w_conv[l], a_cmp[l], w_cmp[l], w_out[l])
        final = l == depth - 1
        mix_p, kvn, winn, convn, mkv, pooled, proj_s = _prompt_layer(
            xp, xs.reshape(-1, xs.shape[-1]), mem_prompt, mem_norm_g[l], w_mem_kv[l], lw,
            pool=(cache4, page_flat, lw["a3"], l * pool_size))
        kv_p.append(kvn)
        win_p.append(winn)
        conv_p.append(convn)
        mem_p.append(mkv)
        mix_s, kvn, winn, convn = _sample_layer(xs, proj_s, ts, l, cache4, page_flat, pooled, cache_win, state_conv,
                                                cache_mem, lw)
        kv_s.append(kvn)
        win_s.append(winn)
        conv_s.append(convn)
        out_p, out_s = _out_proj(*mix_p, mix_s, lw["w_out"], final_g, tm=min(512, mix_p[0].shape[0]), final=final)
        xp = out_p.reshape(xp.shape)
        xs = out_s.reshape(xs.shape)
    return (xp, xs[:, :ts], jnp.stack(kv_p), jnp.stack(win_p), jnp.stack(conv_p), jnp.stack(mem_p),
            jnp.stack(kv_s), jnp.stack(win_s), jnp.stack(conv_s))
```

```python
import functools

import jax
import jax.numpy as jnp
from jax import lax
from jax.experimental import pallas as pl
from jax.experimental.pallas import tpu as pltpu

F32 = jnp.float32
BF16 = jnp.bfloat16

HEAD_DIM = 128
CONV_DIM = 512
CONV_W = 3
NSA_HEADS = 8
NSA_KV = 2
NSA_HPG = NSA_HEADS // NSA_KV
MEM_HEADS = 4
N_BRANCH = 3
ROPE_DIM = HEAD_DIM // 4
ROPE_HALF = ROPE_DIM // 2
ROPE_THETA = 500000.0
CMP_BLOCK = 64
SEL_BLOCK = 64
SEL_SHIFT = 6
TOP_N = 16
WINDOW = 512
NORM_EPS = 1e-6
MASK_NEG = -1e30
FORCE = 1e9
ATTN_SCALE = HEAD_DIM ** -0.5
SCALE_LOG2 = ATTN_SCALE * 1.4426950408889634

C_H, C_B, C_C, C_G, Q_OFF, NG_OFF = 0, 512, 1024, 1536, 2048, 3072
ZA_WIDTH = 4096
KV_OFF, MQ_OFF, MG_OFF = 0, 1536, 2048
ZB_WIDTH = 2560
BG_SRC = 4096
BG_N = NSA_HEADS * N_BRANCH
LANE = 128
SAMPLE_T_PAD = 8
VMEM_LIMIT = 56 * 1024 * 1024
W_CHUNK = 512


def _nt(a, b):
    return lax.dot_general(a, b, (((1,), (1,)), ((), ())), preferred_element_type=F32)


def _nn(a, b):
    return jnp.dot(a, b, preferred_element_type=F32)


def _params(sem, vmem=VMEM_LIMIT):
    return pltpu.CompilerParams(dimension_semantics=sem, vmem_limit_bytes=vmem)


def _rope(x, c, s1, s2):
    return x * c + pltpu.roll(x, ROPE_HALF, 1) * s1 + pltpu.roll(x, LANE - ROPE_HALF, 1) * s2


def _silu(x):
    return x * jax.nn.sigmoid(x)


def _rms_rows(x_ref, g_ref):
    x = x_ref[...]
    return (x * lax.rsqrt(jnp.mean(x * x, axis=-1, keepdims=True) + NORM_EPS) * g_ref[...]).astype(BF16)


def _norm_matmul_kernel(x_ref, g_ref, w_ref, z_ref, h_scr, *, tn):
    h_scr[...] = _rms_rows(x_ref, g_ref)
    for j in range(w_ref.shape[1] // tn):
        z_ref[:, j * tn:(j + 1) * tn] = _nn(h_scr[...], w_ref[:, j * tn:(j + 1) * tn].astype(BF16))


def _norm_matmul(x, g, w, *, tm, tn):
    m, d = x.shape
    n = w.shape[1]
    assert n % tn == 0 and m % tm == 0 and w.shape[0] == d
    resident = lambda shape: pl.BlockSpec(shape, lambda i: (0, 0), pipeline_mode=pl.Buffered(1))
    return pl.pallas_call(
        functools.partial(_norm_matmul_kernel, tn=tn),
        out_shape=jax.ShapeDtypeStruct((m, n), F32),
        grid=(m // tm,),
        in_specs=[pl.BlockSpec((tm, d), lambda i: (i, 0)), resident((1, d)), resident((d, n))],
        out_specs=pl.BlockSpec((tm, n), lambda i: (i, 0)),
        scratch_shapes=[pltpu.VMEM((tm, d), BF16)],
        compiler_params=_params(("parallel",)),
        name="norm_matmul",
    )(x, g.reshape(1, d), w)


def _fetch_weight_rows(w_hbm, layer, first_row, w_scr, stage, sem):
    ch = stage.shape[1]
    n_chunks = w_scr.shape[0] // ch
    assert n_chunks * ch == w_scr.shape[0]

    def chunk_copy(c):
        return pltpu.make_async_copy(w_hbm.at[layer, pl.ds(first_row + c * ch, ch), :], stage.at[c % 2], sem.at[c % 2])

    chunk_copy(0).start()
    for c in range(n_chunks):
        if c + 1 < n_chunks:
            chunk_copy(c + 1).start()
        chunk_copy(c).wait()
        w_scr[c * ch:(c + 1) * ch, :] = stage[c % 2].astype(BF16)


def _fetch_gate_rows(w_hbm, layer, wbg_scr, stage, sem):
    cp = pltpu.make_async_copy(w_hbm.at[layer, pl.ds(BG_SRC, BG_N), :], stage.at[0, pl.ds(0, BG_N)], sem.at[0])
    cp.start()
    cp.wait()
    wbg_scr[...] = jnp.zeros(wbg_scr.shape, BF16)
    wbg_scr[0:BG_N, :] = stage[0, 0:BG_N, :].astype(BF16)


def _proj_conv_kernel(x_ref, g_ref, w_hbm, wc_ref, c_ref, s1_ref, s2_ref, xs_ref,
                      qc_ref, qr_ref, ng_ref, bg_ref, ya_ref, st_ref, zs_ref, bgs_ref,
                      h_scr, up_scr, w_ref, wbg_ref, stage, sem, *, blocks_per_seq, layer):
    tm = x_ref.shape[0]
    cw = CONV_DIM
    i = pl.program_id(0)
    first = i % blocks_per_seq == 0
    @pl.when(i == 0)
    def _():
        _fetch_weight_rows(w_hbm, layer, 0, w_ref, stage, sem)
        _fetch_gate_rows(w_hbm, layer, wbg_ref, stage, sem)
        up_scr[...] = jnp.zeros(up_scr.shape, F32)

    h_scr[...] = _rms_rows(x_ref, g_ref)
    bg_ref[...] = _nt(h_scr[...], wbg_ref[...])
    chunk = lambda off: _nt(h_scr[...], w_ref[off:off + cw, :])


    carry = up_scr[pl.ds(8 + tm - (CONV_W - 1), CONV_W - 1), :]
    up_scr[pl.ds(8 - (CONV_W - 1), CONV_W - 1), :] = jnp.where(first, 0.0, carry)
    u = chunk(C_C) * chunk(C_H)
    up_scr[pl.ds(8, tm), :] = u
    wc = wc_ref[...]
    y = wc[0:1, :] * up_scr[pl.ds(6, tm), :]
    y = y + wc[1:2, :] * up_scr[pl.ds(7, tm), :]
    y = y + wc[2:3, :] * u
    y = chunk(C_B) * y
    ya_ref[...] = (_silu(chunk(C_G)) * y).astype(ya_ref.dtype)

    st_ref[0] = up_scr[pl.ds(8 + tm - (CONV_W - 1), CONV_W - 1), :]

    c, s1, s2 = c_ref[...], s1_ref[...], s2_ref[...]
    for j in range((NG_OFF - Q_OFF) // cw):
        qv = chunk(Q_OFF + j * cw)
        for k in range(cw // LANE):
            lo = j * cw + k * LANE
            qh = qv[:, k * LANE:(k + 1) * LANE]
            qc_ref[:, lo:lo + LANE] = (qh * SCALE_LOG2).astype(BF16)
            qr_ref[:, lo:lo + LANE] = (_rope(qh, c, s1, s2) * SCALE_LOG2).astype(BF16)
    for j in range((ZA_WIDTH - NG_OFF) // cw):
        ng_ref[:, j * cw:(j + 1) * cw] = _silu(chunk(NG_OFF + j * cw))

    @pl.when(i == pl.num_programs(0) - 1)
    def _():
        hs = _rms_rows(xs_ref, g_ref)
        bgs_ref[...] = _nt(hs, wbg_ref[...])
        for j in range(ZA_WIDTH // cw):
            zs_ref[:, j * cw:(j + 1) * cw] = _nt(hs, w_ref[j * cw:(j + 1) * cw, :])


def _proj_conv(x2, g, w_t3, layer, w_conv, tabs, xs2, *, tm, seq_len):
    m, d = x2.shape
    ms = xs2.shape[0]
    assert seq_len % tm == 0 and CONV_W == 3
    blocks_per_seq = seq_len // tm
    nq, nng = NG_OFF - Q_OFF, ZA_WIDTH - NG_OFF
    resident = lambda shape: pl.BlockSpec(shape, lambda i: (0,) * len(shape), pipeline_mode=pl.Buffered(1))
    row = lambda width: pl.BlockSpec((tm, width), lambda i: (i, 0))
    tspec = pl.BlockSpec((tm, LANE), lambda i: (i % blocks_per_seq, 0))
    return pl.pallas_call(
        functools.partial(_proj_conv_kernel, blocks_per_seq=blocks_per_seq, layer=layer),
        out_shape=[jax.ShapeDtypeStruct((m, nq), BF16), jax.ShapeDtypeStruct((m, nq), BF16),
                   jax.ShapeDtypeStruct((m, nng), F32), jax.ShapeDtypeStruct((m, LANE), F32),
                   jax.ShapeDtypeStruct((m, CONV_DIM), BF16),
                   jax.ShapeDtypeStruct((m // seq_len, CONV_W - 1, CONV_DIM), F32),
                   jax.ShapeDtypeStruct((ms, ZA_WIDTH), F32), jax.ShapeDtypeStruct((ms, LANE), F32)],
        grid=(m // tm,),
        in_specs=[row(d), resident((1, d)), pl.BlockSpec(memory_space=pl.ANY), resident(w_conv.shape),
                  tspec, tspec, tspec, resident((ms, d))],
        out_specs=[row(nq), row(nq), row(nng), row(LANE), row(CONV_DIM),
                   pl.BlockSpec((1, CONV_W - 1, CONV_DIM), lambda i: (i // blocks_per_seq, 0, 0)),
                   pl.BlockSpec((ms, ZA_WIDTH), lambda i: (0, 0)), pl.BlockSpec((ms, LANE), lambda i: (0, 0))],
        scratch_shapes=[pltpu.VMEM((tm, d), BF16), pltpu.VMEM((tm + 8, CONV_DIM), F32),
                        pltpu.VMEM((ZA_WIDTH, d), BF16), pltpu.VMEM((LANE, d), BF16),
                        pltpu.VMEM((2, W_CHUNK, d), F32), pltpu.SemaphoreType.DMA((2,))],
        compiler_params=_params(("arbitrary",)),
        name="proj_conv",
    )(x2, g.reshape(1, d), w_t3, w_conv, *tabs, xs2)


def _proj_kv_kernel(x_ref, g_ref, w_hbm, c_ref, s1_ref, s2_ref, a_ref, xs_ref, mkv_ref,
                    kvn_ref, kvb_ref, win_ref, pool_ref, ym_ref, zs_ref, h_scr, w_ref, stage, sem,
                    *, blocks_per_seq, layer):
    tm = x_ref.shape[0]
    cw = 2 * NSA_KV * LANE

    @pl.when(pl.program_id(0) == 0)
    def _():
        _fetch_weight_rows(w_hbm, layer, BG_SRC + BG_N, w_ref, stage, sem)

    h_scr[...] = _rms_rows(x_ref, g_ref)
    chunk = lambda off: _nt(h_scr[...], w_ref[off:off + cw, :])
    c, s1, s2 = c_ref[...], s1_ref[...], s2_ref[...]
    kv0 = chunk(KV_OFF)
    kv1 = chunk(KV_OFF + cw)
    kv2 = chunk(KV_OFF + 2 * cw)
    n_kv, n_w, half = 4 * NSA_KV, 2 * NSA_KV, NSA_KV * LANE
    for g in range(NSA_KV):
        lo, hi = g * LANE, (g + 1) * LANE
        ks = _rope(kv1[:, lo:hi], c, s1, s2)
        kw = _rope(kv2[:, lo:hi], c, s1, s2)
        vs = kv1[:, half + lo:half + hi]
        vw = kv2[:, half + lo:half + hi]
        kvn_ref[pl.ds(g, tm, stride=n_kv), :] = kv0[:, lo:hi]
        kvn_ref[pl.ds(NSA_KV + g, tm, stride=n_kv), :] = kv0[:, half + lo:half + hi]
        kvn_ref[pl.ds(2 * NSA_KV + g, tm, stride=n_kv), :] = ks
        kvn_ref[pl.ds(3 * NSA_KV + g, tm, stride=n_kv), :] = vs
        kvb_ref[:, lo:hi] = ks.astype(BF16)
        kvb_ref[:, half + lo:half + hi] = vs.astype(BF16)
        kvb_ref[:, 2 * half + lo:2 * half + hi] = kw.astype(BF16)
        kvb_ref[:, 3 * half + lo:3 * half + hi] = vw.astype(BF16)
        win_ref[pl.ds(g, tm, stride=n_w), :] = kw
        win_ref[pl.ds(NSA_KV + g, tm, stride=n_w), :] = vw

    pool_ref[...] = jnp.sum(kv0.reshape(tm // CMP_BLOCK, CMP_BLOCK, cw) * a_ref[...][None], axis=1)

    def store(lo, hi, y):
        ym_ref[:, lo:hi] = y.astype(ym_ref.dtype)

    _mem_heads(chunk(MQ_OFF), chunk(MG_OFF), mkv_ref, store, interleaved=False)

    @pl.when(pl.program_id(0) == pl.num_programs(0) - 1)
    def _():
        hs = _rms_rows(xs_ref, g_ref)
        for j in range(ZB_WIDTH // cw):
            zs_ref[:, j * cw:(j + 1) * cw] = _nt(hs, w_ref[j * cw:(j + 1) * cw, :])


def _proj_kv(x2, g, w_t3, layer, tabs, a4, xs2, mkv3, *, tm, seq_len):
    m, d = x2.shape
    ms = xs2.shape[0]
    wm = MEM_HEADS * LANE
    assert seq_len % tm == 0 and MG_OFF == MQ_OFF + wm and ZB_WIDTH == MG_OFF + wm
    blocks_per_seq = seq_len // tm
    n_kv, n_w = 4 * NSA_KV, 2 * NSA_KV
    resident = lambda shape: pl.BlockSpec(shape, lambda i: (0,) * len(shape), pipeline_mode=pl.Buffered(1))
    row = lambda width: pl.BlockSpec((tm, width), lambda i: (i, 0))
    tspec = pl.BlockSpec((tm, LANE), lambda i: (i % blocks_per_seq, 0))
    return pl.pallas_call(
        functools.partial(_proj_kv_kernel, blocks_per_seq=blocks_per_seq, layer=layer),
        out_shape=[jax.ShapeDtypeStruct((m * n_kv, LANE), F32),
                   jax.ShapeDtypeStruct((m, n_kv * LANE), BF16),
                   jax.ShapeDtypeStruct((m // seq_len * tm * n_w, LANE), F32),
                   jax.ShapeDtypeStruct((m // CMP_BLOCK, 2 * NSA_KV * LANE), F32),
                   jax.ShapeDtypeStruct((m, wm), BF16), jax.ShapeDtypeStruct((ms, ZB_WIDTH), F32)],
        grid=(m // tm,),
        in_specs=[row(d), resident((1, d)), pl.BlockSpec(memory_space=pl.ANY), tspec, tspec, tspec,
                  resident(a4.shape), resident((ms, d)),
                  pl.BlockSpec((1,) + mkv3.shape[1:], lambda i: (i // blocks_per_seq, 0, 0))],
        out_specs=[pl.BlockSpec((tm * n_kv, LANE), lambda i: (i, 0)), row(n_kv * LANE),
                   pl.BlockSpec((tm * n_w, LANE), lambda i: (i // blocks_per_seq, 0)),
                   pl.BlockSpec((tm // CMP_BLOCK, 2 * NSA_KV * LANE), lambda i: (i, 0)), row(wm),
                   pl.BlockSpec((ms, ZB_WIDTH), lambda i: (0, 0))],
        scratch_shapes=[pltpu.VMEM((tm, d), BF16), pltpu.VMEM((ZB_WIDTH, d), BF16),
                        pltpu.VMEM((2, W_CHUNK, d), F32), pltpu.SemaphoreType.DMA((2,))],
        compiler_params=_params(("arbitrary",)),
        name="proj_kv",
    )(x2, g.reshape(1, d), w_t3, *tabs, a4, xs2, mkv3)


def _pool_pages_kernel(pt_sm, a_ref, cache_ref, o_ref, buf, sem, *, pages_per_step, page_base):
    step = pl.program_id(0)
    n_steps = pl.num_programs(0)
    slot = step % 2
    n_cols = buf.shape[3]

    def page_copy(step_idx, p, to_slot):
        page = pt_sm[step_idx * pages_per_step + p] + page_base
        return pltpu.make_async_copy(cache_ref.at[page, :, pl.ds(0, n_cols), :], buf.at[to_slot, p], sem.at[to_slot])

    @pl.when(step == 0)
    def _():
        for p in range(pages_per_step):
            page_copy(0, p, 0).start()

    @pl.when(step + 1 < n_steps)
    def _():
        for p in range(pages_per_step):
            page_copy(step + 1, p, 1 - slot).start()

    for p in range(pages_per_step):
        page_copy(step, p, slot).wait()

    a = a_ref[...]
    per = buf.shape[2] // CMP_BLOCK
    for p in range(pages_per_step):
        for k in range(per):
            x = buf[slot, p, pl.ds(k * CMP_BLOCK, CMP_BLOCK)]
            o_ref[0, p, k] = jnp.sum(x * a, axis=0)


def _pool_pages(cache4, page_flat, a3, *, bs, n_pages, page_base, pages_per_step=16):
    page = cache4.shape[1]
    per = page // CMP_BLOCK
    n_cols = 2 * NSA_KV
    total = bs * n_pages
    pages_per_step = min(pages_per_step, total)
    assert total % pages_per_step == 0
    n_steps = total // pages_per_step
    grid_spec = pltpu.PrefetchScalarGridSpec(
        num_scalar_prefetch=1, grid=(n_steps,),
        in_specs=[pl.BlockSpec((CMP_BLOCK, n_cols, LANE), lambda si, pt: (0, 0, 0)),
                  pl.BlockSpec(memory_space=pl.ANY)],
        out_specs=pl.BlockSpec((1, pages_per_step, per, n_cols, LANE), lambda si, pt: (si, 0, 0, 0, 0)),
        scratch_shapes=[pltpu.VMEM((2, pages_per_step, page, n_cols, LANE), F32),
                        pltpu.SemaphoreType.DMA((2,))])
    out = pl.pallas_call(
        functools.partial(_pool_pages_kernel, pages_per_step=pages_per_step, page_base=page_base),
        out_shape=jax.ShapeDtypeStruct((n_steps, pages_per_step, per, n_cols, LANE), F32),
        grid_spec=grid_spec,
        compiler_params=_params(("arbitrary",)),
        name="pool_pages",
    )(page_flat, a3, cache4)
    return out.reshape(bs, n_pages * per, n_cols * LANE)


def _cmp_proj_kernel(p_ref, w_ref, kc_ref, vc_ref):
    pooled = p_ref[0]
    n = pooled.shape[0]
    n_pad = kc_ref.shape[2]
    for c in range(4):
        r = _nn(pooled[:, c * LANE:(c + 1) * LANE].astype(BF16), w_ref[c]).astype(BF16)
        dst = kc_ref if c < 2 else vc_ref
        if n_pad > n:
            dst[0, c % 2] = jnp.zeros((n_pad, LANE), BF16)
        dst[0, c % 2, 0:n, :] = r


def _cmp_proj(pooled, w4):
    b, n, _ = pooled.shape
    n_pad = -(-n // LANE) * LANE
    spec = pl.BlockSpec((1, NSA_KV, n_pad, LANE), lambda bi: (bi, 0, 0, 0))
    return pl.pallas_call(
        _cmp_proj_kernel,
        out_shape=[jax.ShapeDtypeStruct((b, NSA_KV, n_pad, LANE), BF16)] * 2,
        grid=(b,),
        in_specs=[pl.BlockSpec((1, n, 512), lambda bi: (bi, 0, 0)),
                  pl.BlockSpec((4, LANE, LANE), lambda bi: (0, 0, 0))],
        out_specs=[spec, spec],
        compiler_params=_params(("parallel",)),
        name="cmp_proj",
    )(pooled, w4)


def _lane_parts(x):
    return [x[:, j * LANE:(j + 1) * LANE] for j in range(x.shape[1] // LANE)]


def _nsa_prompt_kernel(*refs, t_len, tq, tc, tw, n_sel, top, sub, pool_pages, page_base):
    if pool_pages:
        (pt_sm, qc_ref, qr_ref, ng_ref, bg_ref, ksel_ref, vsel_ref, kwin_ref, vwin_ref, kc_ref, vc_ref,
         pa_ref, cache_ref, o_ref, pool_ref,
         qc_scr, qr_scr, s_scr, p_scr, a_scr, m_scr, l_scr, acc_scr, oc_scr, b_scr,
         sw_scr, pw_scr, bw_scr, ow_scr, wl_scr, pbuf, psem) = refs
        step = (pl.program_id(0) * pl.num_programs(1) + pl.program_id(1)) * pl.num_programs(2) + pl.program_id(2)
        n_steps = pl.num_programs(0) * pl.num_programs(1) * pl.num_programs(2)
        slot = step % 2
        half_rows, n_cols = pbuf.shape[2], 2 * NSA_KV

        def page_copies(step_idx, p, to_slot):
            page = pt_sm[step_idx * pool_pages + p] + page_base
            return [pltpu.make_async_copy(cache_ref.at[page, pl.ds(hh * half_rows, half_rows), pl.ds(0, n_cols), :],
                                          pbuf.at[to_slot, p, :, pl.ds(hh * n_cols, n_cols), :], psem.at[to_slot])
                    for hh in range(2)]

        @pl.when(step == 0)
        def _():
            for p in range(pool_pages):
                for cp in page_copies(0, p, 0):
                    cp.start()

        @pl.when(step + 1 < n_steps)
        def _():
            for p in range(pool_pages):
                for cp in page_copies(step + 1, p, 1 - slot):
                    cp.start()

        for p in range(pool_pages):
            for cp in page_copies(step, p, slot):
                cp.wait()
        pa = pa_ref[...]
        for p in range(pool_pages):
            pool_ref[0, p] = jnp.sum(pbuf[slot, p] * pa, axis=0)
    else:
        (qc_ref, qr_ref, ng_ref, bg_ref, ksel_ref, vsel_ref, kwin_ref, vwin_ref, kc_ref, vc_ref,
         o_ref, qc_scr, qr_scr, s_scr, p_scr, a_scr, m_scr, l_scr, acc_scr, oc_scr, b_scr,
         sw_scr, pw_scr, bw_scr, ow_scr, wl_scr) = refs
    i = pl.program_id(2)
    hq = NSA_HPG
    for h in range(hq):
        qc_scr[pl.ds(h * tq, tq), :] = qc_ref[0, :, h * LANE:(h + 1) * LANE]
        qr_scr[pl.ds(h * tq, tq), 0:LANE] = qr_ref[0, :, h * LANE:(h + 1) * LANE]

    w0 = pl.multiple_of(jnp.clip(i * tq + tq - tw, 0, t_len - tw), LANE)
    kp = w0 + lax.broadcasted_iota(jnp.int32, (tq, tw), 1)
    tp = i * tq + lax.broadcasted_iota(jnp.int32, (tq, tw), 0)
    bw_scr[...] = jnp.where((kp <= tp) & (kp > tp - WINDOW), 0.0, MASK_NEG)
    sw_scr[...] = _nt(qr_scr[:, 0:LANE], kwin_ref[0, pl.ds(w0, tw), :])
    for r0 in range(0, hq * tq, sub):
        r = pl.ds(r0, sub)
        parts = _lane_parts(sw_scr[r, :] + bw_scr[pl.ds(r0 % tq, sub), :])
        m = jnp.max(functools.reduce(jnp.maximum, parts), axis=-1, keepdims=True)
        ews = [jnp.exp2(x - m) for x in parts]
        row_sum = jnp.sum(functools.reduce(jnp.add, ews), axis=-1, keepdims=True)
        wl_scr[r, :] = jnp.broadcast_to(1.0 / row_sum, (sub, LANE))
        pw_scr[r, :] = jnp.concatenate(ews, axis=1).astype(BF16)
    ow_scr[...] = _nn(pw_scr[...], vwin_ref[0, pl.ds(w0, tw), :])

    kc = kc_ref[0, 0]
    npad = kc.shape[0]
    s_scr[:, 0:npad] = _nt(qc_scr[...], kc)
    tpos = i * tq + lax.broadcasted_iota(jnp.int32, (tq, npad), 0)
    ncol = lax.broadcasted_iota(jnp.int32, (tq, npad), 1)
    cmask = (ncol + 1) * CMP_BLOCK <= tpos + 1
    imp = jnp.zeros((tq, npad), F32)
    for h in range(hq):
        r = pl.ds(h * tq, tq)
        s = jnp.where(cmask, s_scr[r, 0:npad], MASK_NEG)
        e = jnp.where(cmask, jnp.exp2(s - jnp.max(s, axis=-1, keepdims=True)), 0.0)
        p = e / jnp.maximum(jnp.sum(e, axis=-1, keepdims=True), 1e-30)
        p_scr[r, 0:npad] = p.astype(BF16)
        imp = imp + p
    oc_scr[...] = _nn(p_scr[:, 0:npad], vc_ref[0, 0])

    rows = min(npad, -(-n_sel // 8) * 8)
    imp_t = imp.T[0:rows]
    blk = lax.broadcasted_iota(jnp.int32, (rows, tq), 0)
    cur = lax.shift_right_logical(i * tq + lax.broadcasted_iota(jnp.int32, (rows, tq), 1), SEL_SHIFT)
    imp_t = jnp.where((blk == 0) | (blk == cur) | (blk == cur - 1), FORCE, imp_t)
    imp_t = jnp.where(blk > cur, -1.0, imp_t)
    imp_t = jnp.where(blk >= n_sel, -2.0, imp_t)
    rank = jnp.zeros((rows, tq), F32)
    for j in range(n_sel):
        a = imp_t[j:j + 1, :]
        ahead = (a > imp_t) | ((a == imp_t) & (blk > j))
        rank = rank + jnp.where(ahead, 1.0, 0.0)
    neg_t = jnp.where((rank < top) & (blk < n_sel) & (blk <= cur), 0.0, MASK_NEG)
    if npad > rows:
        neg_t = jnp.concatenate([neg_t, jnp.zeros((npad - rows, tq), F32)], axis=0)
    neg = neg_t.T.astype(BF16)
    for h in range(hq):
        qr_scr[pl.ds(h * tq, tq), LANE:2 * LANE] = neg

    m_scr[...] = jnp.full(m_scr.shape, MASK_NEG, F32)
    l_scr[...] = jnp.zeros(l_scr.shape, F32)
    acc_scr[...] = jnp.zeros(acc_scr.shape, F32)
    n_chunks = (i * tq + tq + tc - 1) // tc

    def chunk(ci, diagonal):
        k0 = pl.multiple_of(ci * tc, tc)
        v = vsel_ref[0, pl.ds(k0, tc), :]
        kb = lax.shift_right_logical(k0 + lax.broadcasted_iota(jnp.int32, (tc, npad), 0), SEL_SHIFT)
        jb = lax.broadcasted_iota(jnp.int32, (tc, npad), 1)
        k_aug = jnp.concatenate([ksel_ref[0, pl.ds(k0, tc), :], jnp.where(kb == jb, 1.0, 0.0).astype(BF16)], axis=1)
        s_scr[:, 0:tc] = _nt(qr_scr[...], k_aug)
        if diagonal:
            kp = k0 + lax.broadcasted_iota(jnp.int32, (tq, tc), 1)
            tp = i * tq + lax.broadcasted_iota(jnp.int32, (tq, tc), 0)
            b_scr[:, 0:tc] = jnp.where(kp <= tp, 0.0, MASK_NEG)
        for r0 in range(0, hq * tq, sub):
            r = pl.ds(r0, sub)
            sc = s_scr[r, 0:tc]
            if diagonal:
                sc = sc + b_scr[pl.ds(r0 % tq, sub), 0:tc]
            parts = _lane_parts(sc)
            m_prev = m_scr[r, :]
            m_new = jnp.maximum(m_prev, jnp.max(functools.reduce(jnp.maximum, parts), axis=-1, keepdims=True))
            alpha = jnp.exp2(m_prev - m_new)
            pes = [jnp.exp2(x - m_new) for x in parts]
            l_scr[r, :] = alpha * l_scr[r, :] + functools.reduce(jnp.add, pes)
            p_scr[r, 0:tc] = jnp.concatenate(pes, axis=1).astype(BF16)
            a_scr[r, :] = alpha
            m_scr[r, :] = m_new
        acc_scr[...] = a_scr[...] * acc_scr[...] + _nn(p_scr[:, 0:tc], v)

    def full_chunk(ci, carry):
        chunk(ci, False)
        return carry

    lax.fori_loop(0, n_chunks - 1, full_chunk, 0)
    chunk(n_chunks - 1, True)

    gate = jax.nn.sigmoid(bg_ref[0])
    for g in range(1, NSA_KV):
        gate = jnp.where(pl.program_id(1) == g, pltpu.roll(gate, LANE - g * hq * N_BRANCH, 1), gate)
    ng = ng_ref[0]
    for h in range(hq):
        r = slice(h * tq, (h + 1) * tq)
        o_s = acc_scr[r, :] * (1.0 / jnp.sum(l_scr[r, :], axis=-1, keepdims=True))
        o = (gate[:, 3 * h:3 * h + 1] * oc_scr[r, :] + gate[:, 3 * h + 1:3 * h + 2] * o_s
             + gate[:, 3 * h + 2:3 * h + 3] * (ow_scr[r, :] * wl_scr[r, :]))
        o_ref[0, :, h * LANE:(h + 1) * LANE] = (ng[:, h * LANE:(h + 1) * LANE] * o).astype(o_ref.dtype)


def _nsa_prompt(qc3, qr3, ng3, bg3, kvb, kc, vc, *, tq=128, tc=512, sub=64, pool=None):
    b, t, _ = qc3.shape
    nq = t // tq
    tc = min(tc, t)
    tw = min(WINDOW + tq, t)
    n_sel = t // SEL_BLOCK
    top = min(TOP_N, n_sel)
    npad = kc.shape[2]
    gw = NSA_HPG * LANE
    rows = NSA_HPG * tq
    wide = max(tc, npad)
    kvspec = lambda k: pl.BlockSpec((1, t, LANE), lambda bi, gi, qi, *_, k=k: (bi, 0, k + gi))
    cspec = pl.BlockSpec((1, 1, npad, LANE), lambda bi, gi, qi, *_: (bi, gi, 0, 0))
    gspec = pl.BlockSpec((1, tq, gw), lambda bi, gi, qi, *_: (bi, qi, gi))
    in_specs = [gspec, gspec, gspec,
                pl.BlockSpec((1, tq, LANE), lambda bi, gi, qi, *_: (bi, qi, 0)),
                kvspec(0), kvspec(2), kvspec(4), kvspec(6), cspec, cspec]
    out_shape = [jax.ShapeDtypeStruct((b, t, NSA_HEADS * LANE), BF16)]
    out_specs = [pl.BlockSpec((1, tq, gw), lambda bi, gi, qi, *_: (bi, qi, gi))]
    assert tc % tq == 0 and npad == LANE
    scratch = ([pltpu.VMEM((rows, LANE), BF16), pltpu.VMEM((rows, 2 * LANE), BF16)]
               + [pltpu.VMEM((rows, wide), F32), pltpu.VMEM((rows, wide), BF16)]
               + [pltpu.VMEM((rows, LANE), F32)] * 5 + [pltpu.VMEM((tq, wide), F32)]
               + [pltpu.VMEM((rows, tw), F32), pltpu.VMEM((rows, tw), BF16), pltpu.VMEM((tq, tw), F32)]
               + [pltpu.VMEM((rows, LANE), F32)] * 2)
    args = [qc3, qr3, ng3, bg3, kvb, kvb, kvb, kvb, kc, vc]
    n_steps = b * NSA_KV * nq
    pool_pages, page_base, prefetch = 0, 0, []
    if pool is not None:
        cache4, page_flat, a3, page_base = pool
        page, n_cols = cache4.shape[1], 2 * NSA_KV
        if page_flat.shape[0] % n_steps == 0 and page == 2 * CMP_BLOCK and 2 * n_cols == 8:
            pool_pages = page_flat.shape[0] // n_steps
            prefetch = [page_flat]
            in_specs += [pl.BlockSpec((CMP_BLOCK, 2 * n_cols, LANE), lambda bi, gi, qi, *_: (0, 0, 0)),
                         pl.BlockSpec(memory_space=pl.ANY)]
            args += [jnp.concatenate([a3, a3], axis=1), cache4]
            out_shape.append(jax.ShapeDtypeStruct((n_steps, pool_pages, 2 * n_cols, LANE), F32))
            out_specs.append(pl.BlockSpec((1, pool_pages, 2 * n_cols, LANE),
                                          lambda bi, gi, qi, *_: ((bi * NSA_KV + gi) * nq + qi, 0, 0, 0)))
            scratch += [pltpu.VMEM((2, pool_pages, CMP_BLOCK, 2 * n_cols, LANE), F32), pltpu.SemaphoreType.DMA((2,))]
    kern = functools.partial(_nsa_prompt_kernel, t_len=t, tq=tq, tc=tc, tw=tw, n_sel=n_sel, top=top,
                             sub=min(sub, tq), pool_pages=pool_pages, page_base=page_base)
    res = pl.pallas_call(
        kern,
        out_shape=out_shape,
        grid_spec=pltpu.PrefetchScalarGridSpec(
            num_scalar_prefetch=len(prefetch), grid=(b, NSA_KV, nq),
            in_specs=in_specs, out_specs=out_specs, scratch_shapes=scratch),
        compiler_params=_params(("arbitrary", "arbitrary", "arbitrary")),
        name="nsa_prompt",
    )(*prefetch, *args)
    return (res[0], res[1]) if pool_pages else (res[0], None)


def _conv_kernel(h_ref, b_ref, c_ref, g_ref, prev_ref, w_ref, y_ref, st_ref, up_scr, *, t_real):
    u = c_ref[...] * h_ref[...]
    t = u.shape[1]
    up_scr[:, pl.ds(8 - (CONV_W - 1), CONV_W - 1), :] = prev_ref[...]
    up_scr[:, pl.ds(8, t), :] = u
    w = w_ref[...]
    y = w[0:1, :][None] * up_scr[:, pl.ds(6, t), :]
    y = y + w[1:2, :][None] * up_scr[:, pl.ds(7, t), :]
    y = y + w[2:3, :][None] * u
    y = b_ref[...] * y
    y_ref[...] = (_silu(g_ref[...]) * y).astype(y_ref.dtype)
    st_ref[...] = up_scr[:, pl.ds(6 + t_real, CONV_W - 1), :]


def _conv_mixer(z3, prev, w_conv, *, t_real):
    b, t, _ = z3.shape
    nc = CONV_DIM // LANE
    zspec = lambda off: pl.BlockSpec((b, t, LANE), lambda ci, off=off: (0, 0, off // LANE + ci))
    return pl.pallas_call(
        functools.partial(_conv_kernel, t_real=t_real),
        out_shape=[jax.ShapeDtypeStruct((b, t, CONV_DIM), BF16),
                   jax.ShapeDtypeStruct((b, CONV_W - 1, CONV_DIM), F32)],
        grid=(nc,),
        in_specs=[zspec(C_H), zspec(C_B), zspec(C_C), zspec(C_G),
                  pl.BlockSpec((b, CONV_W - 1, LANE), lambda ci: (0, 0, ci)),
                  pl.BlockSpec((CONV_W, LANE), lambda ci: (0, ci))],
        out_specs=[pl.BlockSpec((b, t, LANE), lambda ci: (0, 0, ci)),
                   pl.BlockSpec((b, CONV_W - 1, LANE), lambda ci: (0, 0, ci))],
        scratch_shapes=[pltpu.VMEM((b, t + 8, LANE), F32)],
        compiler_params=_params(("parallel",)),
        name="conv_mixer",
    )(z3, z3, z3, z3, prev, w_conv)


def _mem_heads(q, mg, kv_ref, store, *, interleaved):
    half = MEM_HEADS * LANE
    for h in range(MEM_HEADS):
        lo, hi = h * LANE, (h + 1) * LANE
        if interleaved:
            nm = kv_ref.shape[1] // (2 * MEM_HEADS)
            k = kv_ref[0, pl.ds(h, nm, stride=2 * MEM_HEADS), :].astype(BF16)
            v = kv_ref[0, pl.ds(MEM_HEADS + h, nm, stride=2 * MEM_HEADS), :].astype(BF16)
        else:
            k = kv_ref[0, :, lo:hi].astype(BF16)
            v = kv_ref[0, :, half + lo:half + hi].astype(BF16)
        s = _nt((q[:, lo:hi] * ATTN_SCALE).astype(BF16), k)
        e = jnp.exp(s - jnp.max(s, axis=-1, keepdims=True))
        o = _nn(e.astype(BF16), v) / jnp.sum(e, axis=-1, keepdims=True)
        store(lo, hi, _silu(mg[:, lo:hi]) * o)


def _mem_attn_kernel(q_ref, mg_ref, kv_ref, o_ref, *, interleaved):
    def store(lo, hi, y):
        o_ref[0, :, lo:hi] = y.astype(o_ref.dtype)

    _mem_heads(q_ref[0], mg_ref[0], kv_ref, store, interleaved=interleaved)


def _mem_attn(zb3, mkv, *, mq_off, mg_off, tq, interleaved, kv_base=0):
    b, t, _ = zb3.shape
    wq = MEM_HEADS * LANE
    return pl.pallas_call(
        functools.partial(_mem_attn_kernel, interleaved=interleaved),
        out_shape=jax.ShapeDtypeStruct((b, t, wq), BF16),
        grid=(b, t // tq),
        in_specs=[pl.BlockSpec((1, tq, wq), lambda bi, ti: (bi, ti, mq_off // wq)),
                  pl.BlockSpec((1, tq, wq), lambda bi, ti: (bi, ti, mg_off // wq)),
                  pl.BlockSpec((1,) + mkv.shape[1:], lambda bi, ti: (bi + kv_base, 0, 0))],
        out_specs=pl.BlockSpec((1, tq, wq), lambda bi, ti: (bi, ti, 0)),
        compiler_params=_params(("parallel", "parallel")),
        name="mem_attn",
    )(zb3, zb3, mkv)


def _out_proj_kernel(x_ref, ya_ref, yb_ref, ym_ref, xs_ref, yas_ref, ybs_ref, yms_ref, w_ref, fg_ref,
                     o_ref, os_ref, w_scr, *, final):
    @pl.when(pl.program_id(0) == 0)
    def _():
        w_scr[...] = w_ref[...].astype(BF16)

    a, bw = CONV_DIM, CONV_DIM + NSA_HEADS * LANE

    def mix(x, ya, yb, ym):
        acc = _nn(ya, w_scr[0:a, :])
        acc = acc + _nn(yb, w_scr[a:bw, :])
        acc = acc + _nn(ym, w_scr[bw:, :])
        r = x + acc
        if final:
            r = r * lax.rsqrt(jnp.mean(r * r, axis=-1, keepdims=True) + NORM_EPS) * fg_ref[...]
        return r

    o_ref[...] = mix(x_ref[...], ya_ref[...], yb_ref[...], ym_ref[...])

    @pl.when(pl.program_id(0) == pl.num_programs(0) - 1)
    def _():
        os_ref[...] = mix(xs_ref[...], yas_ref[...], ybs_ref[...], yms_ref[...])


def _out_proj(x, ya, yb, ym, sample, w, fg, *, tm, final):
    m, d = x.shape
    row = lambda width: pl.BlockSpec((tm, width), lambda i: (i, 0))
    whole = lambda a: pl.BlockSpec(a.shape, lambda i: (0, 0), pipeline_mode=pl.Buffered(1))
    return pl.pallas_call(
        functools.partial(_out_proj_kernel, final=final),
        out_shape=[jax.ShapeDtypeStruct((m, d), F32), jax.ShapeDtypeStruct(sample[0].shape, F32)],
        grid=(m // tm,),
        in_specs=[row(d), row(ya.shape[1]), row(yb.shape[1]), row(ym.shape[1])] + [whole(a) for a in sample]
                 + [whole(w), pl.BlockSpec((1, d), lambda i: (0, 0), pipeline_mode=pl.Buffered(1))],
        out_specs=[row(d), pl.BlockSpec(sample[0].shape, lambda i: (0, 0))],
        scratch_shapes=[pltpu.VMEM(w.shape, BF16)],
        compiler_params=_params(("arbitrary",)),
        name="out_proj",
    )(x, ya, yb, ym, *sample, w, fg.reshape(1, d))


def _sample_pre_kernel(q_ref, ng_ref, bg_ref, kv0_ref, kv1_ref, kv2_ref, c_ref, s1_ref, s2_ref, kc_ref, vc_ref,
                       kvn_ref, wn_ref, qr_ref, oc_ref, ngo_ref, gate_ref, val_ref, *, past):
    tp = SAMPLE_T_PAD
    hq = NSA_HPG
    c, s1, s2 = c_ref[...], s1_ref[...], s2_ref[...]
    kv0, kv1, kv2 = kv0_ref[0], kv1_ref[0], kv2_ref[0]
    kvn_ref[0, :, 0:512] = kv0
    kvn_ref[0, :, 768:1024] = kv1[:, 256:512]
    wn_ref[0, :, 256:512] = kv2[:, 256:512]
    for g in range(NSA_KV):
        lo, hi = g * LANE, (g + 1) * LANE
        kvn_ref[0, :, 512 + lo:512 + hi] = _rope(kv1[:, lo:hi], c, s1, s2)
        wn_ref[0, :, lo:hi] = _rope(kv2[:, lo:hi], c, s1, s2)

    q = q_ref[0]
    ng = ng_ref[0]
    gates = jax.nn.sigmoid(bg_ref[0])
    for g in range(NSA_KV):
        qc_l, qr_l = [], []
        for h in range(hq):
            lo = (g * hq + h) * LANE
            qh = q[:, lo:lo + LANE]
            qc_l.append(qh * ATTN_SCALE)
            qr_l.append(_rope(qh, c, s1, s2) * ATTN_SCALE)
            ngo_ref[0, g, h * tp:(h + 1) * tp, :] = ng[:, lo:lo + LANE]
            for br in range(N_BRANCH):
                col = (g * hq + h) * N_BRANCH + br
                gate_ref[0, g, br, h * tp:(h + 1) * tp, :] = jnp.broadcast_to(gates[:, col:col + 1], (tp, LANE))
        qc = jnp.concatenate(qc_l, axis=0)
        qr_ref[0, g] = jnp.concatenate(qr_l, axis=0)

        kc = kc_ref[0, g]
        npad = kc.shape[0]
        s = _nt(qc.astype(BF16), kc)
        trow = lax.broadcasted_iota(jnp.int32, (hq * tp, npad), 0) % tp
        ncol = lax.broadcasted_iota(jnp.int32, (hq * tp, npad), 1)
        cmask = (ncol + 1) * CMP_BLOCK <= past + trow + 1
        s = jnp.where(cmask, s, MASK_NEG)
        e = jnp.where(cmask, jnp.exp(s - jnp.max(s, axis=-1, keepdims=True)), 0.0)
        p = e / jnp.maximum(jnp.sum(e, axis=-1, keepdims=True), 1e-30)
        oc_ref[0, g] = _nn(p.astype(BF16), vc_ref[0, g])
        imp = jnp.sum(p.reshape(hq, tp, npad), axis=0)

        blk = lax.broadcasted_iota(jnp.int32, (tp, npad), 1)
        cur = (past + lax.broadcasted_iota(jnp.int32, (tp, npad), 0)) // SEL_BLOCK
        val = jnp.where((blk == 0) | (blk == cur) | (blk == cur - 1), FORCE, imp)
        val = jnp.where(blk > cur, -1.0, val)
        val = jnp.where(blk >= past // SEL_BLOCK, -2.0, val)
        val_ref[0, g] = val


def _topk_kernel(val_ref, idx_ref, *, n_top):
    val = val_ref[...]
    rows, n = val.shape
    blk = lax.broadcasted_iota(jnp.int32, (rows, n), 1)
    lane = lax.broadcasted_iota(jnp.int32, (rows, LANE), 1)
    idx = jnp.zeros((rows, LANE), jnp.int32)
    for r in range(n_top):
        best = jnp.max(val, axis=-1, keepdims=True)
        j = jnp.min(jnp.where(val == best, blk, n), axis=-1, keepdims=True)
        idx = jnp.where(lane == r, j, idx)
        val = jnp.where(blk == j, -3e38, val)
    idx_ref[...] = idx


def _topk(val2, *, n_top):
    rows, n = val2.shape
    return pl.pallas_call(
        functools.partial(_topk_kernel, n_top=n_top),
        out_shape=jax.ShapeDtypeStruct((rows, LANE), jnp.int32),
        grid=(1,),
        in_specs=[pl.BlockSpec((rows, n), lambda i: (0, 0))],
        out_specs=pl.BlockSpec((rows, LANE), lambda i: (0, 0)),
        compiler_params=_params(("arbitrary",)),
        name="sample_topk",
    )(val2)


def _sample_pre(za3, zb3, bg3, tabs, kc, vc, *, past):
    bs, tp, _ = za3.shape
    npad = kc.shape[2]
    qw = NSA_HEADS * LANE
    kvblk = KV_OFF // 512
    zspec = lambda k: pl.BlockSpec((1, tp, 512), lambda bi, k=k: (bi, 0, kvblk + k))
    tspec = pl.BlockSpec((tp, LANE), lambda bi: (0, 0))
    cspec = pl.BlockSpec((1, NSA_KV, npad, LANE), lambda bi: (bi, 0, 0, 0))
    rows = NSA_HPG * tp
    gspec = pl.BlockSpec((1, NSA_KV, rows, LANE), lambda bi: (bi, 0, 0, 0))
    gshape = jax.ShapeDtypeStruct((bs, NSA_KV, rows, LANE), F32)
    return pl.pallas_call(
        functools.partial(_sample_pre_kernel, past=past),
        out_shape=[jax.ShapeDtypeStruct((bs, tp, 1024), F32),
                   jax.ShapeDtypeStruct((bs, tp, 512), F32),
                   gshape, gshape, gshape,
                   jax.ShapeDtypeStruct((bs, NSA_KV, N_BRANCH, rows, LANE), F32),
                   jax.ShapeDtypeStruct((bs, NSA_KV, tp, npad), F32)],
        grid=(bs,),
        in_specs=[pl.BlockSpec((1, tp, qw), lambda bi: (bi, 0, Q_OFF // qw)),
                  pl.BlockSpec((1, tp, qw), lambda bi: (bi, 0, NG_OFF // qw)),
                  pl.BlockSpec((1, tp, LANE), lambda bi: (bi, 0, 0)),
                  zspec(0), zspec(1), zspec(2), tspec, tspec, tspec, cspec, cspec],
        out_specs=[pl.BlockSpec((1, tp, 1024), lambda bi: (bi, 0, 0)),
                   pl.BlockSpec((1, tp, 512), lambda bi: (bi, 0, 0)),
                   gspec, gspec, gspec,
                   pl.BlockSpec((1, NSA_KV, N_BRANCH, rows, LANE), lambda bi: (bi, 0, 0, 0, 0)),
                   pl.BlockSpec((1, NSA_KV, tp, npad), lambda bi: (bi, 0, 0, 0))],
        compiler_params=_params(("parallel",)),
        name="sample_pre",
    )(za3, za3, bg3, zb3, zb3, zb3, *tabs, kc, vc)


def _sample_attn_kernel(idx_sm, pt_sm, qr_ref, oc_ref, ng_ref, gate_ref, ksn_ref, vsn_ref,
                        wc_ref, kwn_ref, vwn_ref, cache_ref, o_ref,
                        kbuf, vbuf, kw_scr, vw_scr, sem, *, ts, n_top, n_pages, page_base, per_page, wb):
    tp = SAMPLE_T_PAD
    hq = NSA_HPG
    b = pl.program_id(0)
    g = pl.program_id(1)
    n_gath = n_top * SEL_BLOCK
    ks_rows = kbuf.shape[2]
    step = b * NSA_KV + g
    n_steps = pl.num_programs(0) * NSA_KV
    slot = step % 2

    def gather_copies(step_idx, t, r, to_slot):
        bb, gg = step_idx // NSA_KV, step_idx % NSA_KV
        blk = idx_sm[(step_idx * ts + t) * n_top + r]
        page = pt_sm[bb * n_pages + blk // per_page] + page_base
        row0 = (blk % per_page) * SEL_BLOCK
        src_k = cache_ref.at[page, pl.ds(row0, SEL_BLOCK), 2 * NSA_KV + gg]
        src_v = cache_ref.at[page, pl.ds(row0, SEL_BLOCK), 3 * NSA_KV + gg]
        dst = pl.ds(r * SEL_BLOCK, SEL_BLOCK)
        return (pltpu.make_async_copy(src_k, kbuf.at[to_slot, t, dst], sem.at[to_slot, 0]),
                pltpu.make_async_copy(src_v, vbuf.at[to_slot, t, dst], sem.at[to_slot, 1]))

    def start_gathers(step_idx, to_slot):
        for t in range(ts):
            for r in range(n_top):
                ck, cv = gather_copies(step_idx, t, r, to_slot)
                ck.start()
                cv.start()

    @pl.when(step == 0)
    def _():
        start_gathers(0, 0)

    @pl.when(step + 1 < n_steps)
    def _():
        start_gathers(step + 1, 1 - slot)

    qr = qr_ref[0, 0].astype(BF16)
    trow = lax.broadcasted_iota(jnp.int32, (hq * tp, 1), 0) % tp

    ww = kw_scr.shape[0]
    kw_scr[pl.ds(0, wb), :] = wc_ref[0, pl.ds(g, wb, stride=2 * NSA_KV), :]
    vw_scr[pl.ds(0, wb), :] = wc_ref[0, pl.ds(NSA_KV + g, wb, stride=2 * NSA_KV), :]
    kw_scr[pl.ds(wb, tp), :] = kwn_ref[0]
    vw_scr[pl.ds(wb, tp), :] = vwn_ref[0]
    kw_scr[pl.ds(wb + tp, ww - wb - tp), :] = jnp.zeros((ww - wb - tp, LANE), F32)
    vw_scr[pl.ds(wb + tp, ww - wb - tp), :] = jnp.zeros((ww - wb - tp, LANE), F32)
    sw = _nt(qr, kw_scr[...].astype(BF16))
    jw = lax.broadcasted_iota(jnp.int32, (hq * tp, ww), 1)
    rel = jw - wb
    okw = (rel <= trow) & (rel > trow - WINDOW) & (jw < wb + ts)
    sw = jnp.where(okw, sw, MASK_NEG)
    ew = jnp.where(okw, jnp.exp(sw - jnp.max(sw, axis=-1, keepdims=True)), 0.0)
    o_w = _nn(ew.astype(BF16), vw_scr[...].astype(BF16)) / jnp.sum(ew, axis=-1, keepdims=True)

    for t in range(ts):
        for r in range(n_top):
            ck, cv = gather_copies(step, t, r, slot)
            ck.wait()
            cv.wait()

    js = lax.broadcasted_iota(jnp.int32, (hq * tp, ks_rows), 1)
    o_s = jnp.zeros((hq * tp, LANE), F32)
    for t in range(ts):
        kbuf[slot, t, pl.ds(n_gath, tp), :] = ksn_ref[0]
        vbuf[slot, t, pl.ds(n_gath, tp), :] = vsn_ref[0]
        kbuf[slot, t, pl.ds(n_gath + tp, ks_rows - n_gath - tp), :] = jnp.zeros((ks_rows - n_gath - tp, LANE), F32)
        vbuf[slot, t, pl.ds(n_gath + tp, ks_rows - n_gath - tp), :] = jnp.zeros((ks_rows - n_gath - tp, LANE), F32)
        ss = _nt(qr, kbuf[slot, t].astype(BF16))
        oks = (js < n_gath) | ((js - n_gath <= t) & (js < n_gath + ts))
        ss = jnp.where(oks, ss, MASK_NEG)
        es = jnp.where(oks, jnp.exp(ss - jnp.max(ss, axis=-1, keepdims=True)), 0.0)
        ot = _nn(es.astype(BF16), vbuf[slot, t].astype(BF16)) / jnp.sum(es, axis=-1, keepdims=True)
        o_s = jnp.where(trow == t, ot, o_s)

    o = gate_ref[0, 0, 0] * oc_ref[0, 0] + gate_ref[0, 0, 1] * o_s + gate_ref[0, 0, 2] * o_w
    y = _silu(ng_ref[0, 0]) * o
    for h in range(hq):
        o_ref[0, :, h * LANE:(h + 1) * LANE] = y[h * tp:(h + 1) * tp].astype(o_ref.dtype)


def _sample_attn(idx_flat, page_flat, qr, oc, ngo, gates, kvn, cache_win_rows, wn, cache4, *,
                 ts, n_top, n_pages, page_base, win_base):
    bs = qr.shape[0]
    tp = SAMPLE_T_PAD
    rows = NSA_HPG * tp
    wb = cache_win_rows.shape[1] // (2 * NSA_KV)
    per_page = cache4.shape[1] // SEL_BLOCK
    ks_rows = -(-(n_top * SEL_BLOCK + tp) // LANE) * LANE
    ww = -(-(wb + tp) // LANE) * LANE
    gspec = pl.BlockSpec((1, 1, rows, LANE), lambda bi, gi, *_: (bi, gi, 0, 0))
    newspec = lambda k: pl.BlockSpec((1, tp, LANE), lambda bi, gi, *_, k=k: (bi, 0, k + gi))
    grid_spec = pltpu.PrefetchScalarGridSpec(
        num_scalar_prefetch=2, grid=(bs, NSA_KV),
        in_specs=[gspec, gspec, gspec,
                  pl.BlockSpec((1, 1, N_BRANCH, rows, LANE), lambda bi, gi, *_: (bi, gi, 0, 0, 0)),
                  newspec(2 * NSA_KV), newspec(3 * NSA_KV),
                  pl.BlockSpec((1,) + cache_win_rows.shape[1:], lambda bi, gi, *_: (bi + win_base, 0, 0)),
                  newspec(0), newspec(NSA_KV),
                  pl.BlockSpec(memory_space=pl.ANY)],
        out_specs=pl.BlockSpec((1, tp, NSA_HPG * LANE), lambda bi, gi, *_: (bi, 0, gi)),
        scratch_shapes=[pltpu.VMEM((2, ts, ks_rows, LANE), F32), pltpu.VMEM((2, ts, ks_rows, LANE), F32),
                        pltpu.VMEM((ww, LANE), F32), pltpu.VMEM((ww, LANE), F32),
                        pltpu.SemaphoreType.DMA((2, 2))])
    kern = functools.partial(_sample_attn_kernel, ts=ts, n_top=n_top, n_pages=n_pages,
                             page_base=page_base, per_page=per_page, wb=wb)
    return pl.pallas_call(
        kern,
        out_shape=jax.ShapeDtypeStruct((bs, tp, NSA_HEADS * LANE), BF16),
        grid_spec=grid_spec,
        compiler_params=_params(("arbitrary", "arbitrary")),
        name="sample_attn",
    )(idx_flat, page_flat, qr, oc, ngo, gates, kvn, kvn, cache_win_rows, wn, wn, cache4)


def _rope_tables(pos, rows):
    freqs = jnp.power(ROPE_THETA, -jnp.arange(ROPE_HALF, dtype=F32) * (2.0 / ROPE_DIM))
    ang = pos.astype(F32)[:, None] * freqs[None, :]
    cos, sin = jnp.cos(ang), jnp.sin(ang)
    n = pos.shape[0]
    z16 = jnp.zeros((n, ROPE_HALF), F32)
    rest = LANE - ROPE_DIM
    c = jnp.concatenate([cos, cos, jnp.ones((n, rest), F32)], axis=1)
    s1 = jnp.concatenate([z16, sin, jnp.zeros((n, rest), F32)], axis=1)
    s2 = jnp.concatenate([-sin, z16, jnp.zeros((n, rest), F32)], axis=1)
    pad = lambda a: jnp.pad(a, ((0, rows - n), (0, 0)))
    return pad(c), pad(s1), pad(s2)


def _layer_weights(norm_g, w_in3, layer, w_conv, a_cmp, w_cmp, w_out):
    w_t3 = jnp.swapaxes(w_in3, 1, 2)
    assert w_t3.shape[1] == BG_SRC + BG_N + ZB_WIDTH and BG_SRC == ZA_WIDTH and BG_N % 8 == 0
    a4 = jnp.concatenate([a_cmp[0], a_cmp[0], a_cmp[1], a_cmp[1]], axis=1)
    a3 = jnp.stack([a_cmp[0], a_cmp[0], a_cmp[1], a_cmp[1]], axis=1)
    w4 = jnp.stack([w_cmp[0], w_cmp[0], w_cmp[1], w_cmp[1]]).astype(BF16)
    return dict(norm_g=norm_g, w_t3=w_t3, layer=layer, w_conv=w_conv, a4=a4, a3=a3, w4=w4,
                w_out=w_out)


def _prompt_layer(xp, xs2, mem_prompt, mem_norm_g, w_mem, lw, pool):
    b, t, d = xp.shape
    m = b * t
    x2 = xp.reshape(m, d)
    tabs = _rope_tables(jnp.arange(t, dtype=jnp.int32), t)
    wb = min(WINDOW, t)
    qc, qr, ng, bg, ya, conv_new, za_s, bg_s = _proj_conv(x2, lw["norm_g"], lw["w_t3"], lw["layer"], lw["w_conv"],
                                                          tabs, xs2, tm=min(512, t), seq_len=t)
    nm = mem_prompt.shape[1]
    mkv = _norm_matmul(mem_prompt.reshape(b * nm, d), mem_norm_g, w_mem, tm=min(512, b * nm), tn=512)
    mkv3 = mkv.reshape(b, nm, 2 * MEM_HEADS * LANE)
    kvn, kvb, win_rows, pooled_prompt, ym, zb_s = _proj_kv(x2, lw["norm_g"], lw["w_t3"], lw["layer"], tabs, lw["a4"],
                                                           xs2, mkv3, tm=wb, seq_len=t)
    kc, vc = _cmp_proj(pooled_prompt.reshape(b, t // CMP_BLOCK, -1), lw["w4"])
    r3 = lambda a: a.reshape(b, t, -1)
    yb, pooled = _nsa_prompt(r3(qc), r3(qr), r3(ng), r3(bg), r3(kvb), kc, vc, tq=min(256, t), pool=pool)
    mix_in = (x2, ya, yb.reshape(m, -1), ym)
    kv_new = kvn.reshape(b, t, 4, NSA_KV, HEAD_DIM)
    win_new = win_rows.reshape(b, wb, 2, NSA_KV, HEAD_DIM)
    mem_kv = mkv.reshape(b, nm, 2, MEM_HEADS, HEAD_DIM)
    return mix_in, kv_new, win_new, conv_new, mem_kv, pooled, (za_s, zb_s, bg_s)


def _sample_layer(xs_p, proj, ts, layer, cache4, page_flat, pooled, cache_win, state_conv, cache_mem, lw):
    bs, tp, d = xs_p.shape
    depth = cache_win.shape[0]
    pool, page = cache4.shape[0] // depth, cache4.shape[1]
    n_pages = page_flat.shape[0] // bs
    past = n_pages * page
    assert past % SEL_BLOCK == 0 and ts <= SEL_BLOCK and ts <= tp
    n_past = past // SEL_BLOCK
    n_top = min(TOP_N, n_past + 1) - 1
    m = bs * tp
    za, zb, bg = proj
    za3 = za.reshape(bs, tp, ZA_WIDTH)
    zb3 = zb.reshape(bs, tp, ZB_WIDTH)
    bg3 = bg.reshape(bs, tp, LANE)
    tabs = _rope_tables(past + jnp.arange(tp, dtype=jnp.int32), tp)
    if pooled is None:
        pooled = _pool_pages(cache4, page_flat, lw["a3"], bs=bs, n_pages=n_pages, page_base=layer * pool)
    else:
        pooled = pooled.reshape(bs, n_pages * (page // CMP_BLOCK), 2 * NSA_KV * LANE)
    kc, vc = _cmp_proj(pooled, lw["w4"])
    kvn, wn, qr, oc, ngo, gates, val = _sample_pre(za3, zb3, bg3, tabs, kc, vc, past=past)
    idx = _topk(val.reshape(bs * NSA_KV * tp, val.shape[-1]), n_top=n_top)
    idx_flat = idx.reshape(bs, NSA_KV, tp, LANE)[:, :, :ts, :n_top].reshape(-1)
    wbuf = cache_win.shape[2]
    cache_win_rows = cache_win.reshape(depth * bs, wbuf * 2 * NSA_KV, HEAD_DIM)
    yb = _sample_attn(idx_flat, page_flat, qr, oc, ngo, gates, kvn, cache_win_rows, wn, cache4,
                      ts=ts, n_top=n_top, n_pages=n_pages, page_base=layer * pool, win_base=layer * bs)
    ya, conv_new = _conv_mixer(za3, state_conv[layer], lw["w_conv"], t_real=ts)
    nm = cache_mem.shape[2]
    mem_rows = cache_mem.reshape(depth * bs, nm * 2 * MEM_HEADS, HEAD_DIM)
    ym = _mem_attn(zb3, mem_rows, mq_off=MQ_OFF, mg_off=MG_OFF, tq=tp, interleaved=True, kv_base=layer * bs)
    mix_in = (xs_p.reshape(m, d), ya.reshape(m, -1), yb.reshape(m, -1), ym.reshape(m, -1))
    kv_new = kvn[:, :ts].reshape(bs, ts, 4, NSA_KV, HEAD_DIM)
    win_rows = wn[:, :ts].reshape(bs, ts, 2, NSA_KV, HEAD_DIM)
    win_state = jnp.concatenate([cache_win[layer], win_rows], axis=1)[:, ts:]
    return mix_in, kv_new, win_state, conv_new


def kernel(x_prompt, x_sample, cache_kv, cache_win, state_conv, cache_mem, page_table, mem_prompt,
           norm_g, w_in, w_conv, a_cmp, w_cmp, mem_norm_g, w_mem_kv, w_out, final_g):
    depth = w_in.shape[0]
    ts = x_sample.shape[1]
    xp = x_prompt
    xs = jnp.pad(x_sample, ((0, 0), (0, SAMPLE_T_PAD - ts), (0, 0)))
    pool_size, page = cache_kv.shape[1], cache_kv.shape[2]
    cache4 = cache_kv.reshape(depth * pool_size, page, 4 * NSA_KV, HEAD_DIM)
    page_flat = page_table.reshape(-1).astype(jnp.int32)
    kv_p, win_p, conv_p, mem_p, kv_s, win_s, conv_s = [], [], [], [], [], [], []
    for l in range(depth):
        lw = _layer_weights(norm_g[l], w_in, l, w_conv[l], a_cmp[l], w_cmp[l], w_out[l])
        final = l == depth - 1
        mix_p, kvn, winn, convn, mkv, pooled, proj_s = _prompt_layer(
            xp, xs.reshape(-1, xs.shape[-1]), mem_prompt, mem_norm_g[l], w_mem_kv[l], lw,
            pool=(cache4, page_flat, lw["a3"], l * pool_size))
        kv_p.append(kvn)
        win_p.append(winn)
        conv_p.append(convn)
        mem_p.append(mkv)
        mix_s, kvn, winn, convn = _sample_layer(xs, proj_s, ts, l, cache4, page_flat, pooled, cache_win, state_conv,
                                                cache_mem, lw)
        kv_s.append(kvn)
        win_s.append(winn)
        conv_s.append(convn)
        out_p, out_s = _out_proj(*mix_p, mix_s, lw["w_out"], final_g, tm=min(512, mix_p[0].shape[0]), final=final)
        xp = out_p.reshape(xp.shape)
        xs = out_s.reshape(xs.shape)
    return (xp, xs[:, :ts], jnp.stack(kv_p), jnp.stack(win_p), jnp.stack(conv_p), jnp.stack(mem_p),
            jnp.stack(kv_s), jnp.stack(win_s), jnp.stack(conv_s))
```

```python
import functools

import jax
import jax.numpy as jnp
import numpy as np
from jax import lax
from jax.experimental import pallas as pl
from jax.experimental.pallas import tpu as pltpu

F32 = jnp.float32
BF16 = jnp.bfloat16

HEAD_DIM = 128
CONV_DIM = 512
CONV_W = 3
NSA_HEADS = 8
NSA_KV = 2
NSA_HPG = NSA_HEADS // NSA_KV
MEM_HEADS = 4
N_BRANCH = 3
ROPE_DIM = HEAD_DIM // 4
ROPE_HALF = ROPE_DIM // 2
ROPE_THETA = 500000.0
CMP_BLOCK = 64
SEL_BLOCK = 64
SEL_SHIFT = 6
TOP_N = 16
WINDOW = 512
NORM_EPS = 1e-6
MASK_NEG = -1e30
FORCE = 1e9
ATTN_SCALE = HEAD_DIM ** -0.5
SCALE_LOG2 = ATTN_SCALE * 1.4426950408889634

C_H, C_B, C_C, C_G, Q_OFF, NG_OFF = 0, 512, 1024, 1536, 2048, 3072
ZA_WIDTH = 4096
KV_OFF, MQ_OFF, MG_OFF = 0, 1536, 2048
ZB_WIDTH = 2560
BG_SRC = 4096
BG_N = NSA_HEADS * N_BRANCH
LANE = 128
SAMPLE_T_PAD = 8
VMEM_LIMIT = 56 * 1024 * 1024
W_CHUNK = 512


def _nt(a, b):
    return lax.dot_general(a, b, (((1,), (1,)), ((), ())), preferred_element_type=F32)


def _nn(a, b):
    return jnp.dot(a, b, preferred_element_type=F32)


def _params(sem, vmem=VMEM_LIMIT):
    return pltpu.CompilerParams(dimension_semantics=sem, vmem_limit_bytes=vmem)


def _rope(x, c, s1, s2):
    return x * c + pltpu.roll(x, ROPE_HALF, 1) * s1 + pltpu.roll(x, LANE - ROPE_HALF, 1) * s2


def _silu(x):
    return x * jax.nn.sigmoid(x)


def _rms_rows(x_ref, g_ref):
    x = x_ref[...]
    return (x * lax.rsqrt(jnp.mean(x * x, axis=-1, keepdims=True) + NORM_EPS) * g_ref[...]).astype(BF16)


def _norm_matmul_kernel(x_ref, g_ref, w_ref, z_ref, h_scr, *, tn):
    tm = x_ref.shape[0]
    parts = w_ref.shape[1] // LANE
    h_scr[...] = _rms_rows(x_ref, g_ref)
    for j in range(w_ref.shape[1] // tn):
        z = _nn(h_scr[...], w_ref[:, j * tn:(j + 1) * tn].astype(BF16))
        for k in range(tn // LANE):
            z_ref[pl.ds(j * (tn // LANE) + k, tm, stride=parts), :] = z[:, k * LANE:(k + 1) * LANE]


def _norm_matmul(x, g, w, *, tm, tn):
    m, d = x.shape
    n = w.shape[1]
    assert n % tn == 0 and tn % LANE == 0 and m % tm == 0 and w.shape[0] == d
    resident = lambda shape: pl.BlockSpec(shape, lambda i: (0, 0), pipeline_mode=pl.Buffered(1))
    return pl.pallas_call(
        functools.partial(_norm_matmul_kernel, tn=tn),
        out_shape=jax.ShapeDtypeStruct((m * (n // LANE), LANE), F32),
        grid=(m // tm,),
        in_specs=[pl.BlockSpec((tm, d), lambda i: (i, 0)), resident((1, d)), resident((d, n))],
        out_specs=pl.BlockSpec((tm * (n // LANE), LANE), lambda i: (i, 0)),
        scratch_shapes=[pltpu.VMEM((tm, d), BF16)],
        compiler_params=_params(("parallel",)),
        name="norm_matmul",
    )(x, g.reshape(1, d), w)


def _fetch_weight_rows(w_hbm, layer, first_row, w_scr, stage, sem):
    ch = stage.shape[1]
    n_chunks = w_scr.shape[0] // ch
    assert n_chunks * ch == w_scr.shape[0]

    def chunk_copy(c):
        return pltpu.make_async_copy(w_hbm.at[layer, pl.ds(first_row + c * ch, ch), :], stage.at[c % 2], sem.at[c % 2])

    chunk_copy(0).start()
    for c in range(n_chunks):
        if c + 1 < n_chunks:
            chunk_copy(c + 1).start()
        chunk_copy(c).wait()
        w_scr[c * ch:(c + 1) * ch, :] = stage[c % 2].astype(BF16)


def _fetch_gate_rows(w_hbm, layer, wbg_scr, stage, sem):
    cp = pltpu.make_async_copy(w_hbm.at[layer, pl.ds(BG_SRC, BG_N), :], stage.at[0, pl.ds(0, BG_N)], sem.at[0])
    cp.start()
    cp.wait()
    wbg_scr[...] = jnp.zeros(wbg_scr.shape, BF16)
    wbg_scr[0:BG_N, :] = stage[0, 0:BG_N, :].astype(BF16)


def _proj_conv_kernel(x_ref, g_ref, w_hbm, wc_ref, c_ref, s1_ref, s2_ref, xs_ref,
                      qc_ref, qr_ref, ng_ref, bg_ref, ya_ref, st_ref, zs_ref, bgs_ref,
                      h_scr, up_scr, w_ref, wbg_ref, stage, sem, *, blocks_per_seq, layer):
    tm = x_ref.shape[0]
    cw = CONV_DIM
    i = pl.program_id(0)
    first = i % blocks_per_seq == 0
    @pl.when(i == 0)
    def _():
        _fetch_weight_rows(w_hbm, layer, 0, w_ref, stage, sem)
        _fetch_gate_rows(w_hbm, layer, wbg_ref, stage, sem)
        up_scr[...] = jnp.zeros(up_scr.shape, F32)

    h_scr[...] = _rms_rows(x_ref, g_ref)
    bg_ref[...] = _nt(h_scr[...], wbg_ref[...])
    chunk = lambda off: _nt(h_scr[...], w_ref[off:off + cw, :])


    carry = up_scr[pl.ds(8 + tm - (CONV_W - 1), CONV_W - 1), :]
    up_scr[pl.ds(8 - (CONV_W - 1), CONV_W - 1), :] = jnp.where(first, 0.0, carry)
    u = chunk(C_C) * chunk(C_H)
    up_scr[pl.ds(8, tm), :] = u
    wc = wc_ref[...]
    y = wc[0:1, :] * up_scr[pl.ds(6, tm), :]
    y = y + wc[1:2, :] * up_scr[pl.ds(7, tm), :]
    y = y + wc[2:3, :] * u
    y = chunk(C_B) * y
    ya_ref[...] = (_silu(chunk(C_G)) * y).astype(ya_ref.dtype)

    st_ref[0] = up_scr[pl.ds(8 + tm - (CONV_W - 1), CONV_W - 1), :]

    c, s1, s2 = c_ref[...], s1_ref[...], s2_ref[...]
    for j in range((NG_OFF - Q_OFF) // cw):
        qv = chunk(Q_OFF + j * cw)
        for k in range(cw // LANE):
            lo = j * cw + k * LANE
            qh = qv[:, k * LANE:(k + 1) * LANE]
            qc_ref[:, lo:lo + LANE] = (qh * SCALE_LOG2).astype(BF16)
            qr_ref[:, lo:lo + LANE] = (_rope(qh, c, s1, s2) * SCALE_LOG2).astype(BF16)
    for j in range((ZA_WIDTH - NG_OFF) // cw):
        ng_ref[:, j * cw:(j + 1) * cw] = _silu(chunk(NG_OFF + j * cw))

    @pl.when(i == pl.num_programs(0) - 1)
    def _():
        hs = _rms_rows(xs_ref, g_ref)
        bgs_ref[...] = _nt(hs, wbg_ref[...])
        for j in range(ZA_WIDTH // cw):
            zs_ref[:, j * cw:(j + 1) * cw] = _nt(hs, w_ref[j * cw:(j + 1) * cw, :])


def _proj_conv(x2, g, w_t3, layer, w_conv, tabs, xs2, *, tm, seq_len):
    m, d = x2.shape
    ms = xs2.shape[0]
    assert seq_len % tm == 0 and CONV_W == 3
    blocks_per_seq = seq_len // tm
    nq, nng = NG_OFF - Q_OFF, ZA_WIDTH - NG_OFF
    resident = lambda shape: pl.BlockSpec(shape, lambda i: (0,) * len(shape), pipeline_mode=pl.Buffered(1))
    row = lambda width: pl.BlockSpec((tm, width), lambda i: (i, 0))
    tspec = pl.BlockSpec((tm, LANE), lambda i: (i % blocks_per_seq, 0))
    return pl.pallas_call(
        functools.partial(_proj_conv_kernel, blocks_per_seq=blocks_per_seq, layer=layer),
        out_shape=[jax.ShapeDtypeStruct((m, nq), BF16), jax.ShapeDtypeStruct((m, nq), BF16),
                   jax.ShapeDtypeStruct((m, nng), F32), jax.ShapeDtypeStruct((m, LANE), F32),
                   jax.ShapeDtypeStruct((m, CONV_DIM), BF16),
                   jax.ShapeDtypeStruct((m // seq_len, CONV_W - 1, CONV_DIM), F32),
                   jax.ShapeDtypeStruct((ms, ZA_WIDTH), F32), jax.ShapeDtypeStruct((ms, LANE), F32)],
        grid=(m // tm,),
        in_specs=[row(d), resident((1, d)), pl.BlockSpec(memory_space=pl.ANY), resident(w_conv.shape),
                  tspec, tspec, tspec, resident((ms, d))],
        out_specs=[row(nq), row(nq), row(nng), row(LANE), row(CONV_DIM),
                   pl.BlockSpec((1, CONV_W - 1, CONV_DIM), lambda i: (i // blocks_per_seq, 0, 0)),
                   pl.BlockSpec((ms, ZA_WIDTH), lambda i: (0, 0)), pl.BlockSpec((ms, LANE), lambda i: (0, 0))],
        scratch_shapes=[pltpu.VMEM((tm, d), BF16), pltpu.VMEM((tm + 8, CONV_DIM), F32),
                        pltpu.VMEM((ZA_WIDTH, d), BF16), pltpu.VMEM((LANE, d), BF16),
                        pltpu.VMEM((2, W_CHUNK, d), F32), pltpu.SemaphoreType.DMA((2,))],
        compiler_params=_params(("arbitrary",)),
        name="proj_conv",
    )(x2, g.reshape(1, d), w_t3, w_conv, *tabs, xs2)


def _proj_kv_kernel(x_ref, g_ref, w_hbm, c_ref, s1_ref, s2_ref, a_ref, xs_ref, mkv_ref,
                    kvn_ref, kvb_ref, win_ref, pool_ref, ym_ref, zs_ref, h_scr, w_ref, stage, sem,
                    *, blocks_per_seq, layer):
    tm = x_ref.shape[0]
    cw = 2 * NSA_KV * LANE

    @pl.when(pl.program_id(0) == 0)
    def _():
        _fetch_weight_rows(w_hbm, layer, BG_SRC + BG_N, w_ref, stage, sem)

    h_scr[...] = _rms_rows(x_ref, g_ref)
    chunk = lambda off: _nt(h_scr[...], w_ref[off:off + cw, :])
    c, s1, s2 = c_ref[...], s1_ref[...], s2_ref[...]
    kv0 = chunk(KV_OFF)
    kv1 = chunk(KV_OFF + cw)
    kv2 = chunk(KV_OFF + 2 * cw)
    n_kv, n_w, half = 4 * NSA_KV, 2 * NSA_KV, NSA_KV * LANE
    for g in range(NSA_KV):
        lo, hi = g * LANE, (g + 1) * LANE
        ks = _rope(kv1[:, lo:hi], c, s1, s2)
        kw = _rope(kv2[:, lo:hi], c, s1, s2)
        vs = kv1[:, half + lo:half + hi]
        vw = kv2[:, half + lo:half + hi]
        kvn_ref[pl.ds(g, tm, stride=n_kv), :] = kv0[:, lo:hi]
        kvn_ref[pl.ds(NSA_KV + g, tm, stride=n_kv), :] = kv0[:, half + lo:half + hi]
        kvn_ref[pl.ds(2 * NSA_KV + g, tm, stride=n_kv), :] = ks
        kvn_ref[pl.ds(3 * NSA_KV + g, tm, stride=n_kv), :] = vs
        kvb_ref[:, lo:hi] = ks.astype(BF16)
        kvb_ref[:, half + lo:half + hi] = vs.astype(BF16)
        kvb_ref[:, 2 * half + lo:2 * half + hi] = kw.astype(BF16)
        kvb_ref[:, 3 * half + lo:3 * half + hi] = vw.astype(BF16)
        win_ref[pl.ds(g, tm, stride=n_w), :] = kw
        win_ref[pl.ds(NSA_KV + g, tm, stride=n_w), :] = vw

    pool_ref[...] = jnp.sum(kv0.reshape(tm // CMP_BLOCK, CMP_BLOCK, cw) * a_ref[...][None], axis=1)

    def store(lo, hi, y):
        ym_ref[:, lo:hi] = y.astype(ym_ref.dtype)

    _mem_heads(chunk(MQ_OFF), chunk(MG_OFF), mkv_ref, store)

    @pl.when(pl.program_id(0) == pl.num_programs(0) - 1)
    def _():
        hs = _rms_rows(xs_ref, g_ref)
        for j in range(ZB_WIDTH // cw):
            zs_ref[:, j * cw:(j + 1) * cw] = _nt(hs, w_ref[j * cw:(j + 1) * cw, :])


def _proj_kv(x2, g, w_t3, layer, tabs, a4, xs2, mkv3, *, tm, seq_len):
    m, d = x2.shape
    ms = xs2.shape[0]
    wm = MEM_HEADS * LANE
    assert seq_len % tm == 0 and MG_OFF == MQ_OFF + wm and ZB_WIDTH == MG_OFF + wm
    blocks_per_seq = seq_len // tm
    n_kv, n_w = 4 * NSA_KV, 2 * NSA_KV
    resident = lambda shape: pl.BlockSpec(shape, lambda i: (0,) * len(shape), pipeline_mode=pl.Buffered(1))
    row = lambda width: pl.BlockSpec((tm, width), lambda i: (i, 0))
    tspec = pl.BlockSpec((tm, LANE), lambda i: (i % blocks_per_seq, 0))
    return pl.pallas_call(
        functools.partial(_proj_kv_kernel, blocks_per_seq=blocks_per_seq, layer=layer),
        out_shape=[jax.ShapeDtypeStruct((m * n_kv, LANE), F32),
                   jax.ShapeDtypeStruct((m, n_kv * LANE), BF16),
                   jax.ShapeDtypeStruct((m // seq_len * tm * n_w, LANE), F32),
                   jax.ShapeDtypeStruct((m // CMP_BLOCK, 2 * NSA_KV * LANE), F32),
                   jax.ShapeDtypeStruct((m, wm), BF16), jax.ShapeDtypeStruct((ms, ZB_WIDTH), F32)],
        grid=(m // tm,),
        in_specs=[row(d), resident((1, d)), pl.BlockSpec(memory_space=pl.ANY), tspec, tspec, tspec,
                  resident(a4.shape), resident((ms, d)),
                  pl.BlockSpec((1,) + mkv3.shape[1:], lambda i: (i // blocks_per_seq, 0, 0))],
        out_specs=[pl.BlockSpec((tm * n_kv, LANE), lambda i: (i, 0)), row(n_kv * LANE),
                   pl.BlockSpec((tm * n_w, LANE), lambda i: (i // blocks_per_seq, 0)),
                   pl.BlockSpec((tm // CMP_BLOCK, 2 * NSA_KV * LANE), lambda i: (i, 0)), row(wm),
                   pl.BlockSpec((ms, ZB_WIDTH), lambda i: (0, 0))],
        scratch_shapes=[pltpu.VMEM((tm, d), BF16), pltpu.VMEM((ZB_WIDTH, d), BF16),
                        pltpu.VMEM((2, W_CHUNK, d), F32), pltpu.SemaphoreType.DMA((2,))],
        compiler_params=_params(("arbitrary",)),
        name="proj_kv",
    )(x2, g.reshape(1, d), w_t3, *tabs, a4, xs2, mkv3)


def _pool_pages_kernel(pt_sm, a_ref, cache_ref, o_ref, buf, sem, *, pages_per_step, page_base):
    step = pl.program_id(0)
    n_steps = pl.num_programs(0)
    slot = step % 2
    n_cols = buf.shape[3]

    def page_copy(step_idx, p, to_slot):
        page = pt_sm[step_idx * pages_per_step + p] + page_base
        return pltpu.make_async_copy(cache_ref.at[page, :, pl.ds(0, n_cols), :], buf.at[to_slot, p], sem.at[to_slot])

    @pl.when(step == 0)
    def _():
        for p in range(pages_per_step):
            page_copy(0, p, 0).start()

    @pl.when(step + 1 < n_steps)
    def _():
        for p in range(pages_per_step):
            page_copy(step + 1, p, 1 - slot).start()

    for p in range(pages_per_step):
        page_copy(step, p, slot).wait()

    a = a_ref[...]
    per = buf.shape[2] // CMP_BLOCK
    for p in range(pages_per_step):
        for k in range(per):
            x = buf[slot, p, pl.ds(k * CMP_BLOCK, CMP_BLOCK)]
            o_ref[0, p, k] = jnp.sum(x * a, axis=0)


def _pool_pages(cache4, page_flat, a3, *, bs, n_pages, page_base, pages_per_step=16):
    page = cache4.shape[1]
    per = page // CMP_BLOCK
    n_cols = 2 * NSA_KV
    total = bs * n_pages
    pages_per_step = min(pages_per_step, total)
    assert total % pages_per_step == 0
    n_steps = total // pages_per_step
    grid_spec = pltpu.PrefetchScalarGridSpec(
        num_scalar_prefetch=1, grid=(n_steps,),
        in_specs=[pl.BlockSpec((CMP_BLOCK, n_cols, LANE), lambda si, pt: (0, 0, 0)),
                  pl.BlockSpec(memory_space=pl.ANY)],
        out_specs=pl.BlockSpec((1, pages_per_step, per, n_cols, LANE), lambda si, pt: (si, 0, 0, 0, 0)),
        scratch_shapes=[pltpu.VMEM((2, pages_per_step, page, n_cols, LANE), F32),
                        pltpu.SemaphoreType.DMA((2,))])
    out = pl.pallas_call(
        functools.partial(_pool_pages_kernel, pages_per_step=pages_per_step, page_base=page_base),
        out_shape=jax.ShapeDtypeStruct((n_steps, pages_per_step, per, n_cols, LANE), F32),
        grid_spec=grid_spec,
        compiler_params=_params(("arbitrary",)),
        name="pool_pages",
    )(page_flat, a3, cache4)
    return out.reshape(bs, n_pages * per, n_cols * LANE)


def _cmp_proj_kernel(p_ref, w_ref, kc_ref, vc_ref):
    pooled = p_ref[0]
    n = pooled.shape[0]
    n_pad = kc_ref.shape[2]
    for c in range(4):
        r = _nn(pooled[:, c * LANE:(c + 1) * LANE].astype(BF16), w_ref[c]).astype(BF16)
        dst = kc_ref if c < 2 else vc_ref
        if n_pad > n:
            dst[0, c % 2] = jnp.zeros((n_pad, LANE), BF16)
        dst[0, c % 2, 0:n, :] = r


def _cmp_proj(pooled, w4):
    b, n, _ = pooled.shape
    n_pad = -(-n // LANE) * LANE
    spec = pl.BlockSpec((1, NSA_KV, n_pad, LANE), lambda bi: (bi, 0, 0, 0))
    return pl.pallas_call(
        _cmp_proj_kernel,
        out_shape=[jax.ShapeDtypeStruct((b, NSA_KV, n_pad, LANE), BF16)] * 2,
        grid=(b,),
        in_specs=[pl.BlockSpec((1, n, 512), lambda bi: (bi, 0, 0)),
                  pl.BlockSpec((4, LANE, LANE), lambda bi: (0, 0, 0))],
        out_specs=[spec, spec],
        compiler_params=_params(("parallel",)),
        name="cmp_proj",
    )(pooled, w4)


def _lane_parts(x):
    return [x[:, j * LANE:(j + 1) * LANE] for j in range(x.shape[1] // LANE)]


def _nsa_prompt_kernel(*refs, t_len, tq, tc, tw, n_sel, top, sub, pool_pages, page_base):
    if pool_pages:
        (pt_sm, qc_ref, qr_ref, ng_ref, bg_ref, ksel_ref, vsel_ref, kwin_ref, vwin_ref, kc_ref, vc_ref,
         pa_ref, cache_ref, o_ref, pool_ref,
         qc_scr, qr_scr, s_scr, p_scr, a_scr, m_scr, l_scr, acc_scr, oc_scr, b_scr,
         sw_scr, pw_scr, bw_scr, ow_scr, wl_scr, pbuf, prow_scr, psem) = refs
        step = (pl.program_id(0) * pl.num_programs(1) + pl.program_id(1)) * pl.num_programs(2) + pl.program_id(2)
        n_steps = pl.num_programs(0) * pl.num_programs(1) * pl.num_programs(2)
        slot = step % 2
        half_rows, n_cols = pbuf.shape[2], 2 * NSA_KV

        def page_copies(step_idx, p, to_slot):
            page = pt_sm[step_idx * pool_pages + p] + page_base
            return [pltpu.make_async_copy(cache_ref.at[page, pl.ds(hh * half_rows, half_rows), pl.ds(0, n_cols), :],
                                          pbuf.at[to_slot, p, :, pl.ds(hh * n_cols, n_cols), :], psem.at[to_slot])
                    for hh in range(2)]

        @pl.when(step == 0)
        def _():
            for p in range(pool_pages):
                for cp in page_copies(0, p, 0):
                    cp.start()

        @pl.when(step + 1 < n_steps)
        def _():
            for p in range(pool_pages):
                for cp in page_copies(step + 1, p, 1 - slot):
                    cp.start()

        for p in range(pool_pages):
            for cp in page_copies(step, p, slot):
                cp.wait()
        pa = pa_ref[...]
        for p in range(pool_pages):
            prow_scr[p * 2 * n_cols:(p + 1) * 2 * n_cols, :] = jnp.sum(pbuf[slot, p] * pa, axis=0)
        for col in range(n_cols):
            pool_ref[0, :, col * LANE:(col + 1) * LANE] = prow_scr[pl.ds(col, 2 * pool_pages, stride=n_cols), :]
    else:
        (qc_ref, qr_ref, ng_ref, bg_ref, ksel_ref, vsel_ref, kwin_ref, vwin_ref, kc_ref, vc_ref,
         o_ref, qc_scr, qr_scr, s_scr, p_scr, a_scr, m_scr, l_scr, acc_scr, oc_scr, b_scr,
         sw_scr, pw_scr, bw_scr, ow_scr, wl_scr) = refs
    i = pl.program_id(2)
    hq = NSA_HPG
    for h in range(hq):
        qc_scr[pl.ds(h * tq, tq), :] = qc_ref[0, :, h * LANE:(h + 1) * LANE]
        qr_scr[pl.ds(h * tq, tq), 0:LANE] = qr_ref[0, :, h * LANE:(h + 1) * LANE]

    w0 = pl.multiple_of(jnp.clip(i * tq + tq - tw, 0, t_len - tw), LANE)
    kp = w0 + lax.broadcasted_iota(jnp.int32, (tq, tw), 1)
    tp = i * tq + lax.broadcasted_iota(jnp.int32, (tq, tw), 0)
    bw_scr[...] = jnp.where((kp <= tp) & (kp > tp - WINDOW), 0.0, MASK_NEG)
    sw_scr[...] = _nt(qr_scr[:, 0:LANE], kwin_ref[0, pl.ds(w0, tw), :])
    for r0 in range(0, hq * tq, sub):
        r = pl.ds(r0, sub)
        parts = _lane_parts(sw_scr[r, :] + bw_scr[pl.ds(r0 % tq, sub), :])
        m = jnp.max(functools.reduce(jnp.maximum, parts), axis=-1, keepdims=True)
        ews = [jnp.exp2(x - m) for x in parts]
        row_sum = jnp.sum(functools.reduce(jnp.add, ews), axis=-1, keepdims=True)
        wl_scr[r, :] = jnp.broadcast_to(1.0 / row_sum, (sub, LANE))
        pw_scr[r, :] = jnp.concatenate(ews, axis=1).astype(BF16)
    ow_scr[...] = _nn(pw_scr[...], vwin_ref[0, pl.ds(w0, tw), :])

    kc = kc_ref[0, 0]
    npad = kc.shape[0]
    s_scr[:, 0:npad] = _nt(qc_scr[...], kc)
    tpos = i * tq + lax.broadcasted_iota(jnp.int32, (tq, npad), 0)
    ncol = lax.broadcasted_iota(jnp.int32, (tq, npad), 1)
    cmask = (ncol + 1) * CMP_BLOCK <= tpos + 1
    imp = jnp.zeros((tq, npad), F32)
    for h in range(hq):
        r = pl.ds(h * tq, tq)
        s = jnp.where(cmask, s_scr[r, 0:npad], MASK_NEG)
        e = jnp.where(cmask, jnp.exp2(s - jnp.max(s, axis=-1, keepdims=True)), 0.0)
        p = e / jnp.maximum(jnp.sum(e, axis=-1, keepdims=True), 1e-30)
        p_scr[r, 0:npad] = p.astype(BF16)
        imp = imp + p
    oc_scr[...] = _nn(p_scr[:, 0:npad], vc_ref[0, 0])

    rows = min(npad, -(-n_sel // 8) * 8)
    imp_t = imp.T[0:rows]
    blk = lax.broadcasted_iota(jnp.int32, (rows, tq), 0)
    cur = lax.shift_right_logical(i * tq + lax.broadcasted_iota(jnp.int32, (rows, tq), 1), SEL_SHIFT)
    imp_t = jnp.where((blk == 0) | (blk == cur) | (blk == cur - 1), FORCE, imp_t)
    imp_t = jnp.where(blk > cur, -1.0, imp_t)
    imp_t = jnp.where(blk >= n_sel, -2.0, imp_t)
    rank = jnp.zeros((rows, tq), F32)
    for j in range(n_sel):
        a = imp_t[j:j + 1, :]
        ahead = (a > imp_t) | ((a == imp_t) & (blk > j))
        rank = rank + jnp.where(ahead, 1.0, 0.0)
    neg_t = jnp.where((rank < top) & (blk < n_sel) & (blk <= cur), 0.0, MASK_NEG)
    if npad > rows:
        neg_t = jnp.concatenate([neg_t, jnp.zeros((npad - rows, tq), F32)], axis=0)
    neg = neg_t.T.astype(BF16)
    for h in range(hq):
        qr_scr[pl.ds(h * tq, tq), LANE:2 * LANE] = neg

    m_scr[...] = jnp.full(m_scr.shape, MASK_NEG, F32)
    l_scr[...] = jnp.zeros(l_scr.shape, F32)
    acc_scr[...] = jnp.zeros(acc_scr.shape, F32)
    n_chunks = (i * tq + tq + tc - 1) // tc

    def chunk(ci, diagonal):
        k0 = pl.multiple_of(ci * tc, tc)
        v = vsel_ref[0, pl.ds(k0, tc), :]
        kb = lax.shift_right_logical(k0 + lax.broadcasted_iota(jnp.int32, (tc, npad), 0), SEL_SHIFT)
        jb = lax.broadcasted_iota(jnp.int32, (tc, npad), 1)
        k_aug = jnp.concatenate([ksel_ref[0, pl.ds(k0, tc), :], jnp.where(kb == jb, 1.0, 0.0).astype(BF16)], axis=1)
        s_scr[:, 0:tc] = _nt(qr_scr[...], k_aug)
        if diagonal:
            kp = k0 + lax.broadcasted_iota(jnp.int32, (tq, tc), 1)
            tp = i * tq + lax.broadcasted_iota(jnp.int32, (tq, tc), 0)
            b_scr[:, 0:tc] = jnp.where(kp <= tp, 0.0, MASK_NEG)
        for r0 in range(0, hq * tq, sub):
            r = pl.ds(r0, sub)
            sc = s_scr[r, 0:tc]
            if diagonal:
                sc = sc + b_scr[pl.ds(r0 % tq, sub), 0:tc]
            parts = _lane_parts(sc)
            m_prev = m_scr[r, :]
            m_new = jnp.maximum(m_prev, jnp.max(functools.reduce(jnp.maximum, parts), axis=-1, keepdims=True))
            alpha = jnp.exp2(m_prev - m_new)
            pes = [jnp.exp2(x - m_new) for x in parts]
            l_scr[r, :] = alpha * l_scr[r, :] + functools.reduce(jnp.add, pes)
            p_scr[r, 0:tc] = jnp.concatenate(pes, axis=1).astype(BF16)
            a_scr[r, :] = alpha
            m_scr[r, :] = m_new
        acc_scr[...] = a_scr[...] * acc_scr[...] + _nn(p_scr[:, 0:tc], v)

    def full_chunk(ci, carry):
        chunk(ci, False)
        return carry

    lax.fori_loop(0, n_chunks - 1, full_chunk, 0)
    chunk(n_chunks - 1, True)

    gate = jax.nn.sigmoid(bg_ref[0])
    for g in range(1, NSA_KV):
        gate = jnp.where(pl.program_id(1) == g, pltpu.roll(gate, LANE - g * hq * N_BRANCH, 1), gate)
    ng = ng_ref[0]
    for h in range(hq):
        r = slice(h * tq, (h + 1) * tq)
        o_s = acc_scr[r, :] * (1.0 / jnp.sum(l_scr[r, :], axis=-1, keepdims=True))
        o = (gate[:, 3 * h:3 * h + 1] * oc_scr[r, :] + gate[:, 3 * h + 1:3 * h + 2] * o_s
             + gate[:, 3 * h + 2:3 * h + 3] * (ow_scr[r, :] * wl_scr[r, :]))
        o_ref[0, :, h * LANE:(h + 1) * LANE] = (ng[:, h * LANE:(h + 1) * LANE] * o).astype(o_ref.dtype)


def _nsa_prompt(qc3, qr3, ng3, bg3, kvb, kc, vc, *, tq=128, tc=512, sub=64, pool=None):
    b, t, _ = qc3.shape
    nq = t // tq
    tc = min(tc, t)
    tw = min(WINDOW + tq, t)
    n_sel = t // SEL_BLOCK
    top = min(TOP_N, n_sel)
    npad = kc.shape[2]
    gw = NSA_HPG * LANE
    rows = NSA_HPG * tq
    wide = max(tc, npad)
    kvspec = lambda k: pl.BlockSpec((1, t, LANE), lambda bi, gi, qi, *_, k=k: (bi, 0, k + gi))
    cspec = pl.BlockSpec((1, 1, npad, LANE), lambda bi, gi, qi, *_: (bi, gi, 0, 0))
    gspec = pl.BlockSpec((1, tq, gw), lambda bi, gi, qi, *_: (bi, qi, gi))
    in_specs = [gspec, gspec, gspec,
                pl.BlockSpec((1, tq, LANE), lambda bi, gi, qi, *_: (bi, qi, 0)),
                kvspec(0), kvspec(2), kvspec(4), kvspec(6), cspec, cspec]
    out_shape = [jax.ShapeDtypeStruct((b, t, NSA_HEADS * LANE), BF16)]
    out_specs = [pl.BlockSpec((1, tq, gw), lambda bi, gi, qi, *_: (bi, qi, gi))]
    assert tc % tq == 0 and npad == LANE
    scratch = ([pltpu.VMEM((rows, LANE), BF16), pltpu.VMEM((rows, 2 * LANE), BF16)]
               + [pltpu.VMEM((rows, wide), F32), pltpu.VMEM((rows, wide), BF16)]
               + [pltpu.VMEM((rows, LANE), F32)] * 5 + [pltpu.VMEM((tq, wide), F32)]
               + [pltpu.VMEM((rows, tw), F32), pltpu.VMEM((rows, tw), BF16), pltpu.VMEM((tq, tw), F32)]
               + [pltpu.VMEM((rows, LANE), F32)] * 2)
    args = [qc3, qr3, ng3, bg3, kvb, kvb, kvb, kvb, kc, vc]
    n_steps = b * NSA_KV * nq
    pool_pages, page_base, prefetch = 0, 0, []
    if pool is not None:
        cache4, page_flat, a3, page_base = pool
        page, n_cols = cache4.shape[1], 2 * NSA_KV
        if page_flat.shape[0] % n_steps == 0 and page == 2 * CMP_BLOCK and 2 * n_cols == 8:
            pool_pages = page_flat.shape[0] // n_steps
            prefetch = [page_flat]
            in_specs += [pl.BlockSpec((CMP_BLOCK, 2 * n_cols, LANE), lambda bi, gi, qi, *_: (0, 0, 0)),
                         pl.BlockSpec(memory_space=pl.ANY)]
            args += [jnp.concatenate([a3, a3], axis=1), cache4]
            out_shape.append(jax.ShapeDtypeStruct((n_steps, 2 * pool_pages, n_cols * LANE), F32))
            out_specs.append(pl.BlockSpec((1, 2 * pool_pages, n_cols * LANE),
                                          lambda bi, gi, qi, *_: ((bi * NSA_KV + gi) * nq + qi, 0, 0)))
            scratch += [pltpu.VMEM((2, pool_pages, CMP_BLOCK, 2 * n_cols, LANE), F32),
                        pltpu.VMEM((pool_pages * 2 * n_cols, LANE), F32), pltpu.SemaphoreType.DMA((2,))]
    kern = functools.partial(_nsa_prompt_kernel, t_len=t, tq=tq, tc=tc, tw=tw, n_sel=n_sel, top=top,
                             sub=min(sub, tq), pool_pages=pool_pages, page_base=page_base)
    res = pl.pallas_call(
        kern,
        out_shape=out_shape,
        grid_spec=pltpu.PrefetchScalarGridSpec(
            num_scalar_prefetch=len(prefetch), grid=(b, NSA_KV, nq),
            in_specs=in_specs, out_specs=out_specs, scratch_shapes=scratch),
        compiler_params=_params(("arbitrary", "arbitrary", "arbitrary")),
        name="nsa_prompt",
    )(*prefetch, *args)
    return (res[0], res[1]) if pool_pages else (res[0], None)


def _conv_kernel(h_ref, b_ref, c_ref, g_ref, prev_ref, w_ref, y_ref, st_ref, up_scr, *, t_real):
    u = c_ref[...] * h_ref[...]
    t = u.shape[1]
    up_scr[:, pl.ds(8 - (CONV_W - 1), CONV_W - 1), :] = prev_ref[...]
    up_scr[:, pl.ds(8, t), :] = u
    w = w_ref[...]
    y = w[0:1, :][None] * up_scr[:, pl.ds(6, t), :]
    y = y + w[1:2, :][None] * up_scr[:, pl.ds(7, t), :]
    y = y + w[2:3, :][None] * u
    y = b_ref[...] * y
    y_ref[...] = (_silu(g_ref[...]) * y).astype(y_ref.dtype)
    st_ref[...] = up_scr[:, pl.ds(6 + t_real, CONV_W - 1), :]


def _conv_mixer(z3, prev, w_conv, *, t_real):
    b, t, _ = z3.shape
    nc = CONV_DIM // LANE
    zspec = lambda off: pl.BlockSpec((b, t, LANE), lambda ci, off=off: (0, 0, off // LANE + ci))
    return pl.pallas_call(
        functools.partial(_conv_kernel, t_real=t_real),
        out_shape=[jax.ShapeDtypeStruct((b, t, CONV_DIM), BF16),
                   jax.ShapeDtypeStruct((b, CONV_W - 1, CONV_DIM), F32)],
        grid=(nc,),
        in_specs=[zspec(C_H), zspec(C_B), zspec(C_C), zspec(C_G),
                  pl.BlockSpec((b, CONV_W - 1, LANE), lambda ci: (0, 0, ci)),
                  pl.BlockSpec((CONV_W, LANE), lambda ci: (0, ci))],
        out_specs=[pl.BlockSpec((b, t, LANE), lambda ci: (0, 0, ci)),
                   pl.BlockSpec((b, CONV_W - 1, LANE), lambda ci: (0, 0, ci))],
        scratch_shapes=[pltpu.VMEM((b, t + 8, LANE), F32)],
        compiler_params=_params(("parallel",)),
        name="conv_mixer",
    )(z3, z3, z3, z3, prev, w_conv)


def _mem_heads(q, mg, kv_ref, store):
    nm = kv_ref.shape[1] // (2 * MEM_HEADS)
    for h in range(MEM_HEADS):
        lo, hi = h * LANE, (h + 1) * LANE
        k = kv_ref[0, pl.ds(h, nm, stride=2 * MEM_HEADS), :].astype(BF16)
        v = kv_ref[0, pl.ds(MEM_HEADS + h, nm, stride=2 * MEM_HEADS), :].astype(BF16)
        s = _nt((q[:, lo:hi] * ATTN_SCALE).astype(BF16), k)
        e = jnp.exp(s - jnp.max(s, axis=-1, keepdims=True))
        o = _nn(e.astype(BF16), v) / jnp.sum(e, axis=-1, keepdims=True)
        store(lo, hi, _silu(mg[:, lo:hi]) * o)


def _mem_attn_kernel(q_ref, mg_ref, kv_ref, o_ref):
    def store(lo, hi, y):
        o_ref[0, :, lo:hi] = y.astype(o_ref.dtype)

    _mem_heads(q_ref[0], mg_ref[0], kv_ref, store)


def _mem_attn(zb3, mkv, *, mq_off, mg_off, tq, kv_base=0):
    b, t, _ = zb3.shape
    wq = MEM_HEADS * LANE
    return pl.pallas_call(
        _mem_attn_kernel,
        out_shape=jax.ShapeDtypeStruct((b, t, wq), BF16),
        grid=(b, t // tq),
        in_specs=[pl.BlockSpec((1, tq, wq), lambda bi, ti: (bi, ti, mq_off // wq)),
                  pl.BlockSpec((1, tq, wq), lambda bi, ti: (bi, ti, mg_off // wq)),
                  pl.BlockSpec((1,) + mkv.shape[1:], lambda bi, ti: (bi + kv_base, 0, 0))],
        out_specs=pl.BlockSpec((1, tq, wq), lambda bi, ti: (bi, ti, 0)),
        compiler_params=_params(("parallel", "parallel")),
        name="mem_attn",
    )(zb3, zb3, mkv)


def _out_proj_kernel(x_ref, ya_ref, yb_ref, ym_ref, xs_ref, yas_ref, ybs_ref, yms_ref, w_ref, fg_ref,
                     o_ref, os_ref, w_scr, *, final):
    @pl.when(pl.program_id(0) == 0)
    def _():
        w_scr[...] = w_ref[...].astype(BF16)

    a, bw = CONV_DIM, CONV_DIM + NSA_HEADS * LANE

    def mix(x, ya, yb, ym):
        acc = _nn(ya, w_scr[0:a, :])
        acc = acc + _nn(yb, w_scr[a:bw, :])
        acc = acc + _nn(ym, w_scr[bw:, :])
        r = x + acc
        if final:
            r = r * lax.rsqrt(jnp.mean(r * r, axis=-1, keepdims=True) + NORM_EPS) * fg_ref[...]
        return r

    o_ref[...] = mix(x_ref[...], ya_ref[...], yb_ref[...], ym_ref[...])

    @pl.when(pl.program_id(0) == pl.num_programs(0) - 1)
    def _():
        os_ref[...] = mix(xs_ref[...], yas_ref[...], ybs_ref[...], yms_ref[...])


def _out_proj(x, ya, yb, ym, sample, w, fg, *, tm, final):
    m, d = x.shape
    row = lambda width: pl.BlockSpec((tm, width), lambda i: (i, 0))
    whole = lambda a: pl.BlockSpec(a.shape, lambda i: (0, 0), pipeline_mode=pl.Buffered(1))
    return pl.pallas_call(
        functools.partial(_out_proj_kernel, final=final),
        out_shape=[jax.ShapeDtypeStruct((m, d), F32), jax.ShapeDtypeStruct(sample[0].shape, F32)],
        grid=(m // tm,),
        in_specs=[row(d), row(ya.shape[1]), row(yb.shape[1]), row(ym.shape[1])] + [whole(a) for a in sample]
                 + [whole(w), pl.BlockSpec((1, d), lambda i: (0, 0), pipeline_mode=pl.Buffered(1))],
        out_specs=[row(d), pl.BlockSpec(sample[0].shape, lambda i: (0, 0))],
        scratch_shapes=[pltpu.VMEM(w.shape, BF16)],
        compiler_params=_params(("arbitrary",)),
        name="out_proj",
    )(x, ya, yb, ym, *sample, w, fg.reshape(1, d))


def _sample_pre_kernel(q_ref, ng_ref, bg_ref, kv0_ref, kv1_ref, kv2_ref, c_ref, s1_ref, s2_ref, kc_ref, vc_ref,
                       kvn_ref, wn_ref, qr_ref, oc_ref, ngo_ref, gate_ref, val_ref, *, past):
    tp = SAMPLE_T_PAD
    hq = NSA_HPG
    c, s1, s2 = c_ref[...], s1_ref[...], s2_ref[...]
    kv0, kv1, kv2 = kv0_ref[0], kv1_ref[0], kv2_ref[0]
    kvn_ref[0, :, 0:512] = kv0
    kvn_ref[0, :, 768:1024] = kv1[:, 256:512]
    wn_ref[0, :, 256:512] = kv2[:, 256:512]
    for g in range(NSA_KV):
        lo, hi = g * LANE, (g + 1) * LANE
        kvn_ref[0, :, 512 + lo:512 + hi] = _rope(kv1[:, lo:hi], c, s1, s2)
        wn_ref[0, :, lo:hi] = _rope(kv2[:, lo:hi], c, s1, s2)

    q = q_ref[0]
    ng = ng_ref[0]
    gates = jax.nn.sigmoid(bg_ref[0])
    for g in range(NSA_KV):
        qc_l, qr_l = [], []
        for h in range(hq):
            lo = (g * hq + h) * LANE
            qh = q[:, lo:lo + LANE]
            qc_l.append(qh * ATTN_SCALE)
            qr_l.append(_rope(qh, c, s1, s2) * ATTN_SCALE)
            ngo_ref[0, g, h * tp:(h + 1) * tp, :] = ng[:, lo:lo + LANE]
            for br in range(N_BRANCH):
                col = (g * hq + h) * N_BRANCH + br
                gate_ref[0, g, br, h * tp:(h + 1) * tp, :] = jnp.broadcast_to(gates[:, col:col + 1], (tp, LANE))
        qc = jnp.concatenate(qc_l, axis=0)
        qr_ref[0, g] = jnp.concatenate(qr_l, axis=0)

        kc = kc_ref[0, g]
        npad = kc.shape[0]
        s = _nt(qc.astype(BF16), kc)
        trow = lax.broadcasted_iota(jnp.int32, (hq * tp, npad), 0) % tp
        ncol = lax.broadcasted_iota(jnp.int32, (hq * tp, npad), 1)
        cmask = (ncol + 1) * CMP_BLOCK <= past + trow + 1
        s = jnp.where(cmask, s, MASK_NEG)
        e = jnp.where(cmask, jnp.exp(s - jnp.max(s, axis=-1, keepdims=True)), 0.0)
        p = e / jnp.maximum(jnp.sum(e, axis=-1, keepdims=True), 1e-30)
        oc_ref[0, g] = _nn(p.astype(BF16), vc_ref[0, g])
        imp = jnp.sum(p.reshape(hq, tp, npad), axis=0)

        blk = lax.broadcasted_iota(jnp.int32, (tp, npad), 1)
        cur = (past + lax.broadcasted_iota(jnp.int32, (tp, npad), 0)) // SEL_BLOCK
        val = jnp.where((blk == 0) | (blk == cur) | (blk == cur - 1), FORCE, imp)
        val = jnp.where(blk > cur, -1.0, val)
        val = jnp.where(blk >= past // SEL_BLOCK, -2.0, val)
        val_ref[0, g] = val


def _topk_kernel(val_ref, idx_ref, *, n_top):
    val = val_ref[...]
    rows, n = val.shape
    blk = lax.broadcasted_iota(jnp.int32, (rows, n), 1)
    lane = lax.broadcasted_iota(jnp.int32, (rows, LANE), 1)
    idx = jnp.zeros((rows, LANE), jnp.int32)
    for r in range(n_top):
        best = jnp.max(val, axis=-1, keepdims=True)
        j = jnp.min(jnp.where(val == best, blk, n), axis=-1, keepdims=True)
        idx = jnp.where(lane == r, j, idx)
        val = jnp.where(blk == j, -3e38, val)
    idx_ref[...] = idx


def _topk(val2, *, n_top):
    rows, n = val2.shape
    return pl.pallas_call(
        functools.partial(_topk_kernel, n_top=n_top),
        out_shape=jax.ShapeDtypeStruct((rows, LANE), jnp.int32),
        grid=(1,),
        in_specs=[pl.BlockSpec((rows, n), lambda i: (0, 0))],
        out_specs=pl.BlockSpec((rows, LANE), lambda i: (0, 0)),
        compiler_params=_params(("arbitrary",)),
        name="sample_topk",
    )(val2)


def _sample_pre(za3, zb3, bg3, tabs, kc, vc, *, past):
    bs, tp, _ = za3.shape
    npad = kc.shape[2]
    qw = NSA_HEADS * LANE
    kvblk = KV_OFF // 512
    zspec = lambda k: pl.BlockSpec((1, tp, 512), lambda bi, k=k: (bi, 0, kvblk + k))
    tspec = pl.BlockSpec((tp, LANE), lambda bi: (0, 0))
    cspec = pl.BlockSpec((1, NSA_KV, npad, LANE), lambda bi: (bi, 0, 0, 0))
    rows = NSA_HPG * tp
    gspec = pl.BlockSpec((1, NSA_KV, rows, LANE), lambda bi: (bi, 0, 0, 0))
    gshape = jax.ShapeDtypeStruct((bs, NSA_KV, rows, LANE), F32)
    return pl.pallas_call(
        functools.partial(_sample_pre_kernel, past=past),
        out_shape=[jax.ShapeDtypeStruct((bs, tp, 1024), F32),
                   jax.ShapeDtypeStruct((bs, tp, 512), F32),
                   gshape, gshape, gshape,
                   jax.ShapeDtypeStruct((bs, NSA_KV, N_BRANCH, rows, LANE), F32),
                   jax.ShapeDtypeStruct((bs, NSA_KV, tp, npad), F32)],
        grid=(bs,),
        in_specs=[pl.BlockSpec((1, tp, qw), lambda bi: (bi, 0, Q_OFF // qw)),
                  pl.BlockSpec((1, tp, qw), lambda bi: (bi, 0, NG_OFF // qw)),
                  pl.BlockSpec((1, tp, LANE), lambda bi: (bi, 0, 0)),
                  zspec(0), zspec(1), zspec(2), tspec, tspec, tspec, cspec, cspec],
        out_specs=[pl.BlockSpec((1, tp, 1024), lambda bi: (bi, 0, 0)),
                   pl.BlockSpec((1, tp, 512), lambda bi: (bi, 0, 0)),
                   gspec, gspec, gspec,
                   pl.BlockSpec((1, NSA_KV, N_BRANCH, rows, LANE), lambda bi: (bi, 0, 0, 0, 0)),
                   pl.BlockSpec((1, NSA_KV, tp, npad), lambda bi: (bi, 0, 0, 0))],
        compiler_params=_params(("parallel",)),
        name="sample_pre",
    )(za3, za3, bg3, zb3, zb3, zb3, *tabs, kc, vc)


def _sample_attn_kernel(idx_sm, pt_sm, qr_ref, oc_ref, ng_ref, gate_ref, ksn_ref, vsn_ref,
                        wc_ref, kwn_ref, vwn_ref, cache_ref, o_ref,
                        kbuf, vbuf, kw_scr, vw_scr, sem, *, ts, n_top, n_pages, page_base, per_page, wb):
    tp = SAMPLE_T_PAD
    hq = NSA_HPG
    b = pl.program_id(0)
    g = pl.program_id(1)
    n_gath = n_top * SEL_BLOCK
    ks_rows = kbuf.shape[2]
    step = b * NSA_KV + g
    n_steps = pl.num_programs(0) * NSA_KV
    slot = step % 2

    def gather_copies(step_idx, t, r, to_slot):
        bb, gg = step_idx // NSA_KV, step_idx % NSA_KV
        blk = idx_sm[(step_idx * ts + t) * n_top + r]
        page = pt_sm[bb * n_pages + blk // per_page] + page_base
        row0 = (blk % per_page) * SEL_BLOCK
        src_k = cache_ref.at[page, pl.ds(row0, SEL_BLOCK), 2 * NSA_KV + gg]
        src_v = cache_ref.at[page, pl.ds(row0, SEL_BLOCK), 3 * NSA_KV + gg]
        dst = pl.ds(r * SEL_BLOCK, SEL_BLOCK)
        return (pltpu.make_async_copy(src_k, kbuf.at[to_slot, t, dst], sem.at[to_slot, 0]),
                pltpu.make_async_copy(src_v, vbuf.at[to_slot, t, dst], sem.at[to_slot, 1]))

    def start_gathers(step_idx, to_slot):
        for t in range(ts):
            for r in range(n_top):
                ck, cv = gather_copies(step_idx, t, r, to_slot)
                ck.start()
                cv.start()

    @pl.when(step == 0)
    def _():
        start_gathers(0, 0)

    @pl.when(step + 1 < n_steps)
    def _():
        start_gathers(step + 1, 1 - slot)

    qr = qr_ref[0, 0].astype(BF16)
    trow = lax.broadcasted_iota(jnp.int32, (hq * tp, 1), 0) % tp

    ww = kw_scr.shape[0]
    kw_scr[pl.ds(0, wb), :] = wc_ref[0, pl.ds(g, wb, stride=2 * NSA_KV), :]
    vw_scr[pl.ds(0, wb), :] = wc_ref[0, pl.ds(NSA_KV + g, wb, stride=2 * NSA_KV), :]
    kw_scr[pl.ds(wb, tp), :] = kwn_ref[0]
    vw_scr[pl.ds(wb, tp), :] = vwn_ref[0]
    kw_scr[pl.ds(wb + tp, ww - wb - tp), :] = jnp.zeros((ww - wb - tp, LANE), F32)
    vw_scr[pl.ds(wb + tp, ww - wb - tp), :] = jnp.zeros((ww - wb - tp, LANE), F32)
    sw = _nt(qr, kw_scr[...].astype(BF16))
    jw = lax.broadcasted_iota(jnp.int32, (hq * tp, ww), 1)
    rel = jw - wb
    okw = (rel <= trow) & (rel > trow - WINDOW) & (jw < wb + ts)
    sw = jnp.where(okw, sw, MASK_NEG)
    ew = jnp.where(okw, jnp.exp(sw - jnp.max(sw, axis=-1, keepdims=True)), 0.0)
    o_w = _nn(ew.astype(BF16), vw_scr[...].astype(BF16)) / jnp.sum(ew, axis=-1, keepdims=True)

    for t in range(ts):
        for r in range(n_top):
            ck, cv = gather_copies(step, t, r, slot)
            ck.wait()
            cv.wait()

    js = lax.broadcasted_iota(jnp.int32, (hq * tp, ks_rows), 1)
    o_s = jnp.zeros((hq * tp, LANE), F32)
    for t in range(ts):
        kbuf[slot, t, pl.ds(n_gath, tp), :] = ksn_ref[0]
        vbuf[slot, t, pl.ds(n_gath, tp), :] = vsn_ref[0]
        kbuf[slot, t, pl.ds(n_gath + tp, ks_rows - n_gath - tp), :] = jnp.zeros((ks_rows - n_gath - tp, LANE), F32)
        vbuf[slot, t, pl.ds(n_gath + tp, ks_rows - n_gath - tp), :] = jnp.zeros((ks_rows - n_gath - tp, LANE), F32)
        ss = _nt(qr, kbuf[slot, t].astype(BF16))
        oks = (js < n_gath) | ((js - n_gath <= t) & (js < n_gath + ts))
        ss = jnp.where(oks, ss, MASK_NEG)
        es = jnp.where(oks, jnp.exp(ss - jnp.max(ss, axis=-1, keepdims=True)), 0.0)
        ot = _nn(es.astype(BF16), vbuf[slot, t].astype(BF16)) / jnp.sum(es, axis=-1, keepdims=True)
        o_s = jnp.where(trow == t, ot, o_s)

    o = gate_ref[0, 0, 0] * oc_ref[0, 0] + gate_ref[0, 0, 1] * o_s + gate_ref[0, 0, 2] * o_w
    y = _silu(ng_ref[0, 0]) * o
    for h in range(hq):
        o_ref[0, :, h * LANE:(h + 1) * LANE] = y[h * tp:(h + 1) * tp].astype(o_ref.dtype)


def _sample_attn(idx_flat, page_flat, qr, oc, ngo, gates, kvn, cache_win_rows, wn, cache4, *,
                 ts, n_top, n_pages, page_base, win_base):
    bs = qr.shape[0]
    tp = SAMPLE_T_PAD
    rows = NSA_HPG * tp
    wb = cache_win_rows.shape[1] // (2 * NSA_KV)
    per_page = cache4.shape[1] // SEL_BLOCK
    ks_rows = -(-(n_top * SEL_BLOCK + tp) // LANE) * LANE
    ww = -(-(wb + tp) // LANE) * LANE
    gspec = pl.BlockSpec((1, 1, rows, LANE), lambda bi, gi, *_: (bi, gi, 0, 0))
    newspec = lambda k: pl.BlockSpec((1, tp, LANE), lambda bi, gi, *_, k=k: (bi, 0, k + gi))
    grid_spec = pltpu.PrefetchScalarGridSpec(
        num_scalar_prefetch=2, grid=(bs, NSA_KV),
        in_specs=[gspec, gspec, gspec,
                  pl.BlockSpec((1, 1, N_BRANCH, rows, LANE), lambda bi, gi, *_: (bi, gi, 0, 0, 0)),
                  newspec(2 * NSA_KV), newspec(3 * NSA_KV),
                  pl.BlockSpec((1,) + cache_win_rows.shape[1:], lambda bi, gi, *_: (bi + win_base, 0, 0)),
                  newspec(0), newspec(NSA_KV),
                  pl.BlockSpec(memory_space=pl.ANY)],
        out_specs=pl.BlockSpec((1, tp, NSA_HPG * LANE), lambda bi, gi, *_: (bi, 0, gi)),
        scratch_shapes=[pltpu.VMEM((2, ts, ks_rows, LANE), F32), pltpu.VMEM((2, ts, ks_rows, LANE), F32),
                        pltpu.VMEM((ww, LANE), F32), pltpu.VMEM((ww, LANE), F32),
                        pltpu.SemaphoreType.DMA((2, 2))])
    kern = functools.partial(_sample_attn_kernel, ts=ts, n_top=n_top, n_pages=n_pages,
                             page_base=page_base, per_page=per_page, wb=wb)
    return pl.pallas_call(
        kern,
        out_shape=jax.ShapeDtypeStruct((bs, tp, NSA_HEADS * LANE), BF16),
        grid_spec=grid_spec,
        compiler_params=_params(("arbitrary", "arbitrary")),
        name="sample_attn",
    )(idx_flat, page_flat, qr, oc, ngo, gates, kvn, kvn, cache_win_rows, wn, wn, cache4)


def _rope_tables(first, n):
    f32 = np.float32
    freqs = np.power(f32(ROPE_THETA), -np.arange(ROPE_HALF, dtype=f32) * f32(2.0 / ROPE_DIM)).astype(f32)
    ang = (np.arange(first, first + n).astype(f32)[:, None] * freqs[None, :]).astype(f32)
    cos, sin = np.cos(ang).astype(f32), np.sin(ang).astype(f32)
    z16 = np.zeros((n, ROPE_HALF), f32)
    rest = LANE - ROPE_DIM
    c = np.concatenate([cos, cos, np.ones((n, rest), f32)], axis=1)
    s1 = np.concatenate([z16, sin, np.zeros((n, rest), f32)], axis=1)
    s2 = np.concatenate([-sin, z16, np.zeros((n, rest), f32)], axis=1)
    return jnp.asarray(c), jnp.asarray(s1), jnp.asarray(s2)


def _layer_weights(norm_g, w_in3, layer, w_conv, a_cmp, w_cmp, w_out):
    w_t3 = jnp.swapaxes(w_in3, 1, 2)
    assert w_t3.shape[1] == BG_SRC + BG_N + ZB_WIDTH and BG_SRC == ZA_WIDTH and BG_N % 8 == 0
    a4 = jnp.concatenate([a_cmp[0], a_cmp[0], a_cmp[1], a_cmp[1]], axis=1)
    a3 = jnp.stack([a_cmp[0], a_cmp[0], a_cmp[1], a_cmp[1]], axis=1)
    w4 = jnp.stack([w_cmp[0], w_cmp[0], w_cmp[1], w_cmp[1]]).astype(BF16)
    return dict(norm_g=norm_g, w_t3=w_t3, layer=layer, w_conv=w_conv, a4=a4, a3=a3, w4=w4,
                w_out=w_out)


def _prompt_layer(xp, xs2, mem_prompt, mem_norm_g, w_mem, lw, pool):
    b, t, d = xp.shape
    m = b * t
    x2 = xp.reshape(m, d)
    tabs = _rope_tables(0, t)
    wb = min(WINDOW, t)
    qc, qr, ng, bg, ya, conv_new, za_s, bg_s = _proj_conv(x2, lw["norm_g"], lw["w_t3"], lw["layer"], lw["w_conv"],
                                                          tabs, xs2, tm=min(512, t), seq_len=t)
    nm = mem_prompt.shape[1]
    mkv = _norm_matmul(mem_prompt.reshape(b * nm, d), mem_norm_g, w_mem, tm=min(512, b * nm), tn=512)
    mkv3 = mkv.reshape(b, nm * 2 * MEM_HEADS, LANE)
    kvn, kvb, win_rows, pooled_prompt, ym, zb_s = _proj_kv(x2, lw["norm_g"], lw["w_t3"], lw["layer"], tabs, lw["a4"],
                                                           xs2, mkv3, tm=wb, seq_len=t)
    kc, vc = _cmp_proj(pooled_prompt.reshape(b, t // CMP_BLOCK, -1), lw["w4"])
    r3 = lambda a: a.reshape(b, t, -1)
    yb, pooled = _nsa_prompt(r3(qc), r3(qr), r3(ng), r3(bg), r3(kvb), kc, vc, tq=min(256, t), pool=pool)
    mix_in = (x2, ya, yb.reshape(m, -1), ym)
    kv_new = kvn.reshape(b, t, 4, NSA_KV, HEAD_DIM)
    win_new = win_rows.reshape(b, wb, 2, NSA_KV, HEAD_DIM)
    mem_kv = mkv.reshape(b, nm, 2, MEM_HEADS, HEAD_DIM)
    return mix_in, kv_new, win_new, conv_new, mem_kv, pooled, (za_s, zb_s, bg_s)


def _sample_layer(xs_p, proj, ts, layer, cache4, page_flat, pooled, cache_win, state_conv, cache_mem, lw):
    bs, tp, d = xs_p.shape
    depth = cache_win.shape[0]
    pool, page = cache4.shape[0] // depth, cache4.shape[1]
    n_pages = page_flat.shape[0] // bs
    past = n_pages * page
    assert past % SEL_BLOCK == 0 and ts <= SEL_BLOCK and ts <= tp
    n_past = past // SEL_BLOCK
    n_top = min(TOP_N, n_past + 1) - 1
    m = bs * tp
    za, zb, bg = proj
    za3 = za.reshape(bs, tp, ZA_WIDTH)
    zb3 = zb.reshape(bs, tp, ZB_WIDTH)
    bg3 = bg.reshape(bs, tp, LANE)
    tabs = _rope_tables(past, tp)
    if pooled is None:
        pooled = _pool_pages(cache4, page_flat, lw["a3"], bs=bs, n_pages=n_pages, page_base=layer * pool)
    else:
        pooled = pooled.reshape(bs, n_pages * (page // CMP_BLOCK), 2 * NSA_KV * LANE)
    kc, vc = _cmp_proj(pooled, lw["w4"])
    kvn, wn, qr, oc, ngo, gates, val = _sample_pre(za3, zb3, bg3, tabs, kc, vc, past=past)
    idx = _topk(val.reshape(bs * NSA_KV * tp, val.shape[-1]), n_top=n_top)
    idx_flat = idx.reshape(bs, NSA_KV, tp, LANE)[:, :, :ts, :n_top].reshape(-1)
    wbuf = cache_win.shape[2]
    cache_win_rows = cache_win.reshape(depth * bs, wbuf * 2 * NSA_KV, HEAD_DIM)
    yb = _sample_attn(idx_flat, page_flat, qr, oc, ngo, gates, kvn, cache_win_rows, wn, cache4,
                      ts=ts, n_top=n_top, n_pages=n_pages, page_base=layer * pool, win_base=layer * bs)
    ya, conv_new = _conv_mixer(za3, state_conv[layer], lw["w_conv"], t_real=ts)
    nm = cache_mem.shape[2]
    mem_rows = cache_mem.reshape(depth * bs, nm * 2 * MEM_HEADS, HEAD_DIM)
    ym = _mem_attn(zb3, mem_rows, mq_off=MQ_OFF, mg_off=MG_OFF, tq=tp, kv_base=layer * bs)
    mix_in = (xs_p.reshape(m, d), ya.reshape(m, -1), yb.reshape(m, -1), ym.reshape(m, -1))
    kv_new = kvn[:, :ts].reshape(bs, ts, 4, NSA_KV, HEAD_DIM)
    win_rows = wn[:, :ts].reshape(bs, ts, 2, NSA_KV, HEAD_DIM)
    win_state = jnp.concatenate([cache_win[layer], win_rows], axis=1)[:, ts:]
    return mix_in, kv_new, win_state, conv_new


def kernel(x_prompt, x_sample, cache_kv, cache_win, state_conv, cache_mem, page_table, mem_prompt,
           norm_g, w_in, w_conv, a_cmp, w_cmp, mem_norm_g, w_mem_kv, w_out, final_g):
    depth = w_in.shape[0]
    ts = x_sample.shape[1]
    xp = x_prompt
    xs = jnp.pad(x_sample, ((0, 0), (0, SAMPLE_T_PAD - ts), (0, 0)))
    pool_size, page = cache_kv.shape[1], cache_kv.shape[2]
    cache4 = cache_kv.reshape(depth * pool_size, page, 4 * NSA_KV, HEAD_DIM)
    page_flat = page_table.reshape(-1).astype(jnp.int32)
    kv_p, win_p, conv_p, mem_p, kv_s, win_s, conv_s = [], [], [], [], [], [], []
    for l in range(depth):
        lw = _layer_weights(norm_g[l], w_in, l, w_conv[l], a_cmp[l], w_cmp[l], w_out[l])
        final = l == depth - 1
        mix_p, kvn, winn, convn, mkv, pooled, proj_s = _prompt_layer(
            xp, xs.reshape(-1, xs.shape[-1]), mem_prompt, mem_norm_g[l], w_mem_kv[l], lw,
            pool=(cache4, page_flat, lw["a3"], l * pool_size))
        kv_p.append(kvn)
        win_p.append(winn)
        conv_p.append(convn)
        mem_p.append(mkv)
        mix_s, kvn, winn, convn = _sample_layer(xs, proj_s, ts, l, cache4, page_flat, pooled, cache_win, state_conv,
                                                cache_mem, lw)
        kv_s.append(kvn)
        win_s.append(winn)
        conv_s.append(convn)
        out_p, out_s = _out_proj(*mix_p, mix_s, lw["w_out"], final_g, tm=min(512, mix_p[0].shape[0]), final=final)
        xp = out_p.reshape(xp.shape)
        xs = out_s.reshape(xs.shape)
    return (xp, xs[:, :ts], jnp.stack(kv_p), jnp.stack(win_p), jnp.stack(conv_p), jnp.stack(mem_p),
            jnp.stack(kv_s), jnp.stack(win_s), jnp.stack(conv_s))
```

```python
import functools

import jax
import jax.numpy as jnp
import numpy as np
from jax import lax
from jax.experimental import pallas as pl
from jax.experimental.pallas import tpu as pltpu

F32 = jnp.float32
BF16 = jnp.bfloat16

HEAD_DIM = 128
CONV_DIM = 512
CONV_W = 3
NSA_HEADS = 8
NSA_KV = 2
NSA_HPG = NSA_HEADS // NSA_KV
MEM_HEADS = 4
N_BRANCH = 3
ROPE_DIM = HEAD_DIM // 4
ROPE_HALF = ROPE_DIM // 2
ROPE_THETA = 500000.0
CMP_BLOCK = 64
SEL_BLOCK = 64
SEL_SHIFT = 6
TOP_N = 16
WINDOW = 512
NORM_EPS = 1e-6
MASK_NEG = -1e30
FORCE = 1e9
ATTN_SCALE = HEAD_DIM ** -0.5
SCALE_LOG2 = ATTN_SCALE * 1.4426950408889634

C_H, C_B, C_C, C_G, Q_OFF, NG_OFF = 0, 512, 1024, 1536, 2048, 3072
ZA_WIDTH = 4096
KV_OFF, MQ_OFF, MG_OFF = 0, 1536, 2048
ZB_WIDTH = 2560
BG_SRC = 4096
BG_N = NSA_HEADS * N_BRANCH
LANE = 128
SAMPLE_T_PAD = 8
VMEM_LIMIT = 56 * 1024 * 1024
W_CHUNK = 512


def _nt(a, b):
    return lax.dot_general(a, b, (((1,), (1,)), ((), ())), preferred_element_type=F32)


def _nn(a, b):
    return jnp.dot(a, b, preferred_element_type=F32)


def _params(sem, vmem=VMEM_LIMIT):
    return pltpu.CompilerParams(dimension_semantics=sem, vmem_limit_bytes=vmem)


def _rope(x, c, s1, s2):
    return x * c + pltpu.roll(x, ROPE_HALF, 1) * s1 + pltpu.roll(x, LANE - ROPE_HALF, 1) * s2


def _silu(x):
    return x * jax.nn.sigmoid(x)


def _rms_rows(x_ref, g_ref):
    x = x_ref[...]
    return (x * lax.rsqrt(jnp.mean(x * x, axis=-1, keepdims=True) + NORM_EPS) * g_ref[...]).astype(BF16)


def _norm_matmul_kernel(x_ref, g_ref, w_ref, z_ref, h_scr, *, tn):
    tm = x_ref.shape[0]
    parts = w_ref.shape[1] // LANE
    h_scr[...] = _rms_rows(x_ref, g_ref)
    for j in range(w_ref.shape[1] // tn):
        z = _nn(h_scr[...], w_ref[:, j * tn:(j + 1) * tn].astype(BF16))
        for k in range(tn // LANE):
            z_ref[pl.ds(j * (tn // LANE) + k, tm, stride=parts), :] = z[:, k * LANE:(k + 1) * LANE]


def _norm_matmul(x, g, w, *, tm, tn):
    m, d = x.shape
    n = w.shape[1]
    assert n % tn == 0 and tn % LANE == 0 and m % tm == 0 and w.shape[0] == d
    resident = lambda shape: pl.BlockSpec(shape, lambda i: (0, 0), pipeline_mode=pl.Buffered(1))
    return pl.pallas_call(
        functools.partial(_norm_matmul_kernel, tn=tn),
        out_shape=jax.ShapeDtypeStruct((m * (n // LANE), LANE), F32),
        grid=(m // tm,),
        in_specs=[pl.BlockSpec((tm, d), lambda i: (i, 0)), resident((1, d)), resident((d, n))],
        out_specs=pl.BlockSpec((tm * (n // LANE), LANE), lambda i: (i, 0)),
        scratch_shapes=[pltpu.VMEM((tm, d), BF16)],
        compiler_params=_params(("parallel",)),
        name="norm_matmul",
    )(x, g.reshape(1, d), w)


def _fetch_weight_rows(w_hbm, layer, first_row, w_scr, stage, sem):
    ch = stage.shape[1]
    n_chunks = w_scr.shape[0] // ch
    assert n_chunks * ch == w_scr.shape[0]

    def chunk_copy(c):
        return pltpu.make_async_copy(w_hbm.at[layer, pl.ds(first_row + c * ch, ch), :], stage.at[c % 2], sem.at[c % 2])

    chunk_copy(0).start()
    for c in range(n_chunks):
        if c + 1 < n_chunks:
            chunk_copy(c + 1).start()
        chunk_copy(c).wait()
        w_scr[c * ch:(c + 1) * ch, :] = stage[c % 2].astype(BF16)


def _fetch_gate_rows(w_hbm, layer, wbg_scr, stage, sem):
    cp = pltpu.make_async_copy(w_hbm.at[layer, pl.ds(BG_SRC, BG_N), :], stage.at[0, pl.ds(0, BG_N)], sem.at[0])
    cp.start()
    cp.wait()
    wbg_scr[...] = jnp.zeros(wbg_scr.shape, BF16)
    wbg_scr[0:BG_N, :] = stage[0, 0:BG_N, :].astype(BF16)


def _proj_conv_kernel(x_ref, g_ref, w_hbm, wc_ref, c_ref, s1_ref, s2_ref, xs_ref,
                      qc_ref, qr_ref, ng_ref, bg_ref, ya_ref, st_ref, zs_ref, bgs_ref,
                      h_scr, up_scr, w_ref, wbg_ref, stage, sem, *, blocks_per_seq, layer):
    tm = x_ref.shape[0]
    cw = CONV_DIM
    i = pl.program_id(0)
    first = i % blocks_per_seq == 0
    @pl.when(i == 0)
    def _():
        _fetch_weight_rows(w_hbm, layer, 0, w_ref, stage, sem)
        _fetch_gate_rows(w_hbm, layer, wbg_ref, stage, sem)
        up_scr[...] = jnp.zeros(up_scr.shape, F32)

    h_scr[...] = _rms_rows(x_ref, g_ref)
    bg_ref[...] = _nt(h_scr[...], wbg_ref[...])
    chunk = lambda off: _nt(h_scr[...], w_ref[off:off + cw, :])


    carry = up_scr[pl.ds(8 + tm - (CONV_W - 1), CONV_W - 1), :]
    up_scr[pl.ds(8 - (CONV_W - 1), CONV_W - 1), :] = jnp.where(first, 0.0, carry)
    u = chunk(C_C) * chunk(C_H)
    up_scr[pl.ds(8, tm), :] = u
    wc = wc_ref[...]
    y = wc[0:1, :] * up_scr[pl.ds(6, tm), :]
    y = y + wc[1:2, :] * up_scr[pl.ds(7, tm), :]
    y = y + wc[2:3, :] * u
    y = chunk(C_B) * y
    ya_ref[...] = (_silu(chunk(C_G)) * y).astype(ya_ref.dtype)

    st_ref[0] = up_scr[pl.ds(8 + tm - (CONV_W - 1), CONV_W - 1), :]

    c, s1, s2 = c_ref[...], s1_ref[...], s2_ref[...]
    for j in range((NG_OFF - Q_OFF) // cw):
        qv = chunk(Q_OFF + j * cw)
        for k in range(cw // LANE):
            lo = j * cw + k * LANE
            qh = qv[:, k * LANE:(k + 1) * LANE]
            qc_ref[:, lo:lo + LANE] = (qh * SCALE_LOG2).astype(BF16)
            qr_ref[:, lo:lo + LANE] = (_rope(qh, c, s1, s2) * SCALE_LOG2).astype(BF16)
    for j in range((ZA_WIDTH - NG_OFF) // cw):
        ng_ref[:, j * cw:(j + 1) * cw] = _silu(chunk(NG_OFF + j * cw))

    @pl.when(i == pl.num_programs(0) - 1)
    def _():
        hs = _rms_rows(xs_ref, g_ref)
        bgs_ref[...] = _nt(hs, wbg_ref[...])
        for j in range(ZA_WIDTH // cw):
            zs_ref[:, j * cw:(j + 1) * cw] = _nt(hs, w_ref[j * cw:(j + 1) * cw, :])


def _proj_conv(x2, g, w_t3, layer, w_conv, tabs, xs2, *, tm, seq_len):
    m, d = x2.shape
    ms = xs2.shape[0]
    assert seq_len % tm == 0 and CONV_W == 3
    blocks_per_seq = seq_len // tm
    nq, nng = NG_OFF - Q_OFF, ZA_WIDTH - NG_OFF
    resident = lambda shape: pl.BlockSpec(shape, lambda i: (0,) * len(shape), pipeline_mode=pl.Buffered(1))
    row = lambda width: pl.BlockSpec((tm, width), lambda i: (i, 0))
    tspec = pl.BlockSpec((tm, LANE), lambda i: (i % blocks_per_seq, 0))
    return pl.pallas_call(
        functools.partial(_proj_conv_kernel, blocks_per_seq=blocks_per_seq, layer=layer),
        out_shape=[jax.ShapeDtypeStruct((m, nq), BF16), jax.ShapeDtypeStruct((m, nq), BF16),
                   jax.ShapeDtypeStruct((m, nng), F32), jax.ShapeDtypeStruct((m, LANE), F32),
                   jax.ShapeDtypeStruct((m, CONV_DIM), BF16),
                   jax.ShapeDtypeStruct((m // seq_len, CONV_W - 1, CONV_DIM), F32),
                   jax.ShapeDtypeStruct((ms, ZA_WIDTH), F32), jax.ShapeDtypeStruct((ms, LANE), F32)],
        grid=(m // tm,),
        in_specs=[row(d), resident((1, d)), pl.BlockSpec(memory_space=pl.ANY), resident(w_conv.shape),
                  tspec, tspec, tspec, resident((ms, d))],
        out_specs=[row(nq), row(nq), row(nng), row(LANE), row(CONV_DIM),
                   pl.BlockSpec((1, CONV_W - 1, CONV_DIM), lambda i: (i // blocks_per_seq, 0, 0)),
                   pl.BlockSpec((ms, ZA_WIDTH), lambda i: (0, 0)), pl.BlockSpec((ms, LANE), lambda i: (0, 0))],
        scratch_shapes=[pltpu.VMEM((tm, d), BF16), pltpu.VMEM((tm + 8, CONV_DIM), F32),
                        pltpu.VMEM((ZA_WIDTH, d), BF16), pltpu.VMEM((LANE, d), BF16),
                        pltpu.VMEM((2, W_CHUNK, d), F32), pltpu.SemaphoreType.DMA((2,))],
        compiler_params=_params(("arbitrary",)),
        name="proj_conv",
    )(x2, g.reshape(1, d), w_t3, w_conv, *tabs, xs2)


def _proj_kv_kernel(x_ref, g_ref, w_hbm, c_ref, s1_ref, s2_ref, a_ref, xs_ref, mkv_ref,
                    kvn_ref, kvb_ref, win_ref, pool_ref, ym_ref, zs_ref, h_scr, w_ref, stage, sem,
                    *, blocks_per_seq, layer):
    tm = x_ref.shape[0]
    cw = 2 * NSA_KV * LANE

    @pl.when(pl.program_id(0) == 0)
    def _():
        _fetch_weight_rows(w_hbm, layer, BG_SRC + BG_N, w_ref, stage, sem)

    h_scr[...] = _rms_rows(x_ref, g_ref)
    chunk = lambda off: _nt(h_scr[...], w_ref[off:off + cw, :])
    c, s1, s2 = c_ref[...], s1_ref[...], s2_ref[...]
    kv0 = chunk(KV_OFF)
    kv1 = chunk(KV_OFF + cw)
    kv2 = chunk(KV_OFF + 2 * cw)
    n_kv, n_w, half = 4 * NSA_KV, 2 * NSA_KV, NSA_KV * LANE
    for g in range(NSA_KV):
        lo, hi = g * LANE, (g + 1) * LANE
        ks = _rope(kv1[:, lo:hi], c, s1, s2)
        kw = _rope(kv2[:, lo:hi], c, s1, s2)
        vs = kv1[:, half + lo:half + hi]
        vw = kv2[:, half + lo:half + hi]
        kvn_ref[pl.ds(g, tm, stride=n_kv), :] = kv0[:, lo:hi]
        kvn_ref[pl.ds(NSA_KV + g, tm, stride=n_kv), :] = kv0[:, half + lo:half + hi]
        kvn_ref[pl.ds(2 * NSA_KV + g, tm, stride=n_kv), :] = ks
        kvn_ref[pl.ds(3 * NSA_KV + g, tm, stride=n_kv), :] = vs
        kvb_ref[:, lo:hi] = ks.astype(BF16)
        kvb_ref[:, half + lo:half + hi] = vs.astype(BF16)
        kvb_ref[:, 2 * half + lo:2 * half + hi] = kw.astype(BF16)
        kvb_ref[:, 3 * half + lo:3 * half + hi] = vw.astype(BF16)
        win_ref[pl.ds(g, tm, stride=n_w), :] = kw
        win_ref[pl.ds(NSA_KV + g, tm, stride=n_w), :] = vw

    pool_ref[...] = jnp.sum(kv0.reshape(tm // CMP_BLOCK, CMP_BLOCK, cw) * a_ref[...][None], axis=1)

    def store(lo, hi, y):
        ym_ref[:, lo:hi] = y.astype(ym_ref.dtype)

    _mem_heads(chunk(MQ_OFF), chunk(MG_OFF), mkv_ref, store)

    @pl.when(pl.program_id(0) == pl.num_programs(0) - 1)
    def _():
        hs = _rms_rows(xs_ref, g_ref)
        for j in range(ZB_WIDTH // cw):
            zs_ref[:, j * cw:(j + 1) * cw] = _nt(hs, w_ref[j * cw:(j + 1) * cw, :])


def _proj_kv(x2, g, w_t3, layer, tabs, a4, xs2, mkv3, *, tm, seq_len):
    m, d = x2.shape
    ms = xs2.shape[0]
    wm = MEM_HEADS * LANE
    assert seq_len % tm == 0 and MG_OFF == MQ_OFF + wm and ZB_WIDTH == MG_OFF + wm
    blocks_per_seq = seq_len // tm
    n_kv, n_w = 4 * NSA_KV, 2 * NSA_KV
    resident = lambda shape: pl.BlockSpec(shape, lambda i: (0,) * len(shape), pipeline_mode=pl.Buffered(1))
    row = lambda width: pl.BlockSpec((tm, width), lambda i: (i, 0))
    tspec = pl.BlockSpec((tm, LANE), lambda i: (i % blocks_per_seq, 0))
    return pl.pallas_call(
        functools.partial(_proj_kv_kernel, blocks_per_seq=blocks_per_seq, layer=layer),
        out_shape=[jax.ShapeDtypeStruct((m * n_kv, LANE), F32),
                   jax.ShapeDtypeStruct((m, n_kv * LANE), BF16),
                   jax.ShapeDtypeStruct((m // seq_len * tm * n_w, LANE), F32),
                   jax.ShapeDtypeStruct((m // CMP_BLOCK, 2 * NSA_KV * LANE), F32),
                   jax.ShapeDtypeStruct((m, wm), BF16), jax.ShapeDtypeStruct((ms, ZB_WIDTH), F32)],
        grid=(m // tm,),
        in_specs=[row(d), resident((1, d)), pl.BlockSpec(memory_space=pl.ANY), tspec, tspec, tspec,
                  resident(a4.shape), resident((ms, d)),
                  pl.BlockSpec((1,) + mkv3.shape[1:], lambda i: (i // blocks_per_seq, 0, 0))],
        out_specs=[pl.BlockSpec((tm * n_kv, LANE), lambda i: (i, 0)), row(n_kv * LANE),
                   pl.BlockSpec((tm * n_w, LANE), lambda i: (i // blocks_per_seq, 0)),
                   pl.BlockSpec((tm // CMP_BLOCK, 2 * NSA_KV * LANE), lambda i: (i, 0)), row(wm),
                   pl.BlockSpec((ms, ZB_WIDTH), lambda i: (0, 0))],
        scratch_shapes=[pltpu.VMEM((tm, d), BF16), pltpu.VMEM((ZB_WIDTH, d), BF16),
                        pltpu.VMEM((2, W_CHUNK, d), F32), pltpu.SemaphoreType.DMA((2,))],
        compiler_params=_params(("arbitrary",)),
        name="proj_kv",
    )(x2, g.reshape(1, d), w_t3, *tabs, a4, xs2, mkv3)


def _pool_pages_kernel(pt_sm, a_ref, cache_ref, o_ref, buf, sem, *, pages_per_step, page_base):
    step = pl.program_id(0)
    n_steps = pl.num_programs(0)
    slot = step % 2
    n_cols = buf.shape[3]

    def page_copy(step_idx, p, to_slot):
        page = pt_sm[step_idx * pages_per_step + p] + page_base
        return pltpu.make_async_copy(cache_ref.at[page, :, pl.ds(0, n_cols), :], buf.at[to_slot, p], sem.at[to_slot])

    @pl.when(step == 0)
    def _():
        for p in range(pages_per_step):
            page_copy(0, p, 0).start()

    @pl.when(step + 1 < n_steps)
    def _():
        for p in range(pages_per_step):
            page_copy(step + 1, p, 1 - slot).start()

    for p in range(pages_per_step):
        page_copy(step, p, slot).wait()

    a = a_ref[...]
    per = buf.shape[2] // CMP_BLOCK
    for p in range(pages_per_step):
        for k in range(per):
            x = buf[slot, p, pl.ds(k * CMP_BLOCK, CMP_BLOCK)]
            o_ref[0, p, k] = jnp.sum(x * a, axis=0)


def _pool_pages(cache4, page_flat, a3, *, bs, n_pages, page_base, pages_per_step=16):
    page = cache4.shape[1]
    per = page // CMP_BLOCK
    n_cols = 2 * NSA_KV
    total = bs * n_pages
    pages_per_step = min(pages_per_step, total)
    assert total % pages_per_step == 0
    n_steps = total // pages_per_step
    grid_spec = pltpu.PrefetchScalarGridSpec(
        num_scalar_prefetch=1, grid=(n_steps,),
        in_specs=[pl.BlockSpec((CMP_BLOCK, n_cols, LANE), lambda si, pt: (0, 0, 0)),
                  pl.BlockSpec(memory_space=pl.ANY)],
        out_specs=pl.BlockSpec((1, pages_per_step, per, n_cols, LANE), lambda si, pt: (si, 0, 0, 0, 0)),
        scratch_shapes=[pltpu.VMEM((2, pages_per_step, page, n_cols, LANE), F32),
                        pltpu.SemaphoreType.DMA((2,))])
    out = pl.pallas_call(
        functools.partial(_pool_pages_kernel, pages_per_step=pages_per_step, page_base=page_base),
        out_shape=jax.ShapeDtypeStruct((n_steps, pages_per_step, per, n_cols, LANE), F32),
        grid_spec=grid_spec,
        compiler_params=_params(("arbitrary",)),
        name="pool_pages",
    )(page_flat, a3, cache4)
    return out.reshape(bs, n_pages * per, n_cols * LANE)


def _cmp_proj_kernel(p_ref, w_ref, kc_ref, vc_ref):
    pooled = p_ref[0]
    n = pooled.shape[0]
    n_pad = kc_ref.shape[2]
    for c in range(4):
        r = _nn(pooled[:, c * LANE:(c + 1) * LANE].astype(BF16), w_ref[c]).astype(BF16)
        dst = kc_ref if c < 2 else vc_ref
        if n_pad > n:
            dst[0, c % 2] = jnp.zeros((n_pad, LANE), BF16)
        dst[0, c % 2, 0:n, :] = r


def _cmp_proj(pooled, w4):
    b, n, _ = pooled.shape
    n_pad = -(-n // LANE) * LANE
    spec = pl.BlockSpec((1, NSA_KV, n_pad, LANE), lambda bi: (bi, 0, 0, 0))
    return pl.pallas_call(
        _cmp_proj_kernel,
        out_shape=[jax.ShapeDtypeStruct((b, NSA_KV, n_pad, LANE), BF16)] * 2,
        grid=(b,),
        in_specs=[pl.BlockSpec((1, n, 512), lambda bi: (bi, 0, 0)),
                  pl.BlockSpec((4, LANE, LANE), lambda bi: (0, 0, 0))],
        out_specs=[spec, spec],
        compiler_params=_params(("parallel",)),
        name="cmp_proj",
    )(pooled, w4)


def _lane_parts(x):
    return [x[:, j * LANE:(j + 1) * LANE] for j in range(x.shape[1] // LANE)]


def _nsa_prompt_kernel(*refs, t_len, tq, tc, tw, n_sel, top, sub, pool_pages, page_base):
    if pool_pages:
        (pt_sm, qc_ref, qr_ref, ng_ref, bg_ref, ksel_ref, vsel_ref, kwin_ref, vwin_ref, kc_ref, vc_ref,
         pa_ref, cache_ref, o_ref, pool_ref,
         qc_scr, qr_scr, s_scr, p_scr, a_scr, m_scr, l_scr, acc_scr, oc_scr, b_scr,
         sw_scr, pw_scr, bw_scr, ow_scr, wl_scr, pbuf, prow_scr, psem) = refs
        step = (pl.program_id(0) * pl.num_programs(1) + pl.program_id(1)) * pl.num_programs(2) + pl.program_id(2)
        n_steps = pl.num_programs(0) * pl.num_programs(1) * pl.num_programs(2)
        slot = step % 2
        half_rows, n_cols = pbuf.shape[2], 2 * NSA_KV

        def page_copies(step_idx, p, to_slot):
            page = pt_sm[step_idx * pool_pages + p] + page_base
            return [pltpu.make_async_copy(cache_ref.at[page, pl.ds(hh * half_rows, half_rows), pl.ds(0, n_cols), :],
                                          pbuf.at[to_slot, p, :, pl.ds(hh * n_cols, n_cols), :], psem.at[to_slot])
                    for hh in range(2)]

        @pl.when(step == 0)
        def _():
            for p in range(pool_pages):
                for cp in page_copies(0, p, 0):
                    cp.start()

        @pl.when(step + 1 < n_steps)
        def _():
            for p in range(pool_pages):
                for cp in page_copies(step + 1, p, 1 - slot):
                    cp.start()

        for p in range(pool_pages):
            for cp in page_copies(step, p, slot):
                cp.wait()
        pa = pa_ref[...]
        for p in range(pool_pages):
            prow_scr[p * 2 * n_cols:(p + 1) * 2 * n_cols, :] = jnp.sum(pbuf[slot, p] * pa, axis=0)
        for col in range(n_cols):
            pool_ref[0, :, col * LANE:(col + 1) * LANE] = prow_scr[pl.ds(col, 2 * pool_pages, stride=n_cols), :]
    else:
        (qc_ref, qr_ref, ng_ref, bg_ref, ksel_ref, vsel_ref, kwin_ref, vwin_ref, kc_ref, vc_ref,
         o_ref, qc_scr, qr_scr, s_scr, p_scr, a_scr, m_scr, l_scr, acc_scr, oc_scr, b_scr,
         sw_scr, pw_scr, bw_scr, ow_scr, wl_scr) = refs
    i = pl.program_id(2)
    hq = NSA_HPG
    for h in range(hq):
        qc_scr[pl.ds(h * tq, tq), :] = qc_ref[0, :, h * LANE:(h + 1) * LANE]
        qr_scr[pl.ds(h * tq, tq), 0:LANE] = qr_ref[0, :, h * LANE:(h + 1) * LANE]

    w0 = pl.multiple_of(jnp.clip(i * tq + tq - tw, 0, t_len - tw), LANE)
    kp = w0 + lax.broadcasted_iota(jnp.int32, (tq, tw), 1)
    tp = i * tq + lax.broadcasted_iota(jnp.int32, (tq, tw), 0)
    bw_scr[...] = jnp.where((kp <= tp) & (kp > tp - WINDOW), 0.0, MASK_NEG)
    sw_scr[...] = _nt(qr_scr[:, 0:LANE], kwin_ref[0, pl.ds(w0, tw), :])
    for r0 in range(0, hq * tq, sub):
        r = pl.ds(r0, sub)
        parts = _lane_parts(sw_scr[r, :] + bw_scr[pl.ds(r0 % tq, sub), :])
        m = jnp.max(functools.reduce(jnp.maximum, parts), axis=-1, keepdims=True)
        ews = [jnp.exp2(x - m) for x in parts]
        row_sum = jnp.sum(functools.reduce(jnp.add, ews), axis=-1, keepdims=True)
        wl_scr[r, :] = jnp.broadcast_to(1.0 / row_sum, (sub, LANE))
        pw_scr[r, :] = jnp.concatenate(ews, axis=1).astype(BF16)
    ow_scr[...] = _nn(pw_scr[...], vwin_ref[0, pl.ds(w0, tw), :])

    kc = kc_ref[0, 0]
    npad = kc.shape[0]
    s_scr[:, 0:npad] = _nt(qc_scr[...], kc)
    tpos = i * tq + lax.broadcasted_iota(jnp.int32, (tq, npad), 0)
    ncol = lax.broadcasted_iota(jnp.int32, (tq, npad), 1)
    cmask = (ncol + 1) * CMP_BLOCK <= tpos + 1
    imp = jnp.zeros((tq, npad), F32)
    for h in range(hq):
        r = pl.ds(h * tq, tq)
        s = jnp.where(cmask, s_scr[r, 0:npad], MASK_NEG)
        e = jnp.where(cmask, jnp.exp2(s - jnp.max(s, axis=-1, keepdims=True)), 0.0)
        p = e / jnp.maximum(jnp.sum(e, axis=-1, keepdims=True), 1e-30)
        p_scr[r, 0:npad] = p.astype(BF16)
        imp = imp + p
    oc_scr[...] = _nn(p_scr[:, 0:npad], vc_ref[0, 0])

    rows = min(npad, -(-n_sel // 8) * 8)
    imp_t = imp.T[0:rows]
    blk = lax.broadcasted_iota(jnp.int32, (rows, tq), 0)
    cur = lax.shift_right_logical(i * tq + lax.broadcasted_iota(jnp.int32, (rows, tq), 1), SEL_SHIFT)
    imp_t = jnp.where((blk == 0) | (blk == cur) | (blk == cur - 1), FORCE, imp_t)
    imp_t = jnp.where(blk > cur, -1.0, imp_t)
    imp_t = jnp.where(blk >= n_sel, -2.0, imp_t)
    rank = jnp.zeros((rows, tq), F32)
    for j in range(n_sel):
        a = imp_t[j:j + 1, :]
        ahead = (a > imp_t) | ((a == imp_t) & (blk > j))
        rank = rank + jnp.where(ahead, 1.0, 0.0)
    neg_t = jnp.where((rank < top) & (blk < n_sel) & (blk <= cur), 0.0, MASK_NEG)
    if npad > rows:
        neg_t = jnp.concatenate([neg_t, jnp.zeros((npad - rows, tq), F32)], axis=0)
    neg = neg_t.T.astype(BF16)
    for h in range(hq):
        qr_scr[pl.ds(h * tq, tq), LANE:2 * LANE] = neg

    m_scr[...] = jnp.full(m_scr.shape, MASK_NEG, F32)
    l_scr[...] = jnp.zeros(l_scr.shape, F32)
    acc_scr[...] = jnp.zeros(acc_scr.shape, F32)
    n_chunks = (i * tq + tq + tc - 1) // tc

    def chunk(ci, diagonal):
        k0 = pl.multiple_of(ci * tc, tc)
        v = vsel_ref[0, pl.ds(k0, tc), :]
        kb = lax.shift_right_logical(k0 + lax.broadcasted_iota(jnp.int32, (tc, npad), 0), SEL_SHIFT)
        jb = lax.broadcasted_iota(jnp.int32, (tc, npad), 1)
        k_aug = jnp.concatenate([ksel_ref[0, pl.ds(k0, tc), :], jnp.where(kb == jb, 1.0, 0.0).astype(BF16)], axis=1)
        s_scr[:, 0:tc] = _nt(qr_scr[...], k_aug)
        if diagonal:
            kp = k0 + lax.broadcasted_iota(jnp.int32, (tq, tc), 1)
            tp = i * tq + lax.broadcasted_iota(jnp.int32, (tq, tc), 0)
            b_scr[:, 0:tc] = jnp.where(kp <= tp, 0.0, MASK_NEG)
        for r0 in range(0, hq * tq, sub):
            r = pl.ds(r0, sub)
            sc = s_scr[r, 0:tc]
            if diagonal:
                sc = sc + b_scr[pl.ds(r0 % tq, sub), 0:tc]
            parts = _lane_parts(sc)
            m_prev = m_scr[r, :]
            m_new = jnp.maximum(m_prev, jnp.max(functools.reduce(jnp.maximum, parts), axis=-1, keepdims=True))
            alpha = jnp.exp2(m_prev - m_new)
            pes = [jnp.exp2(x - m_new) for x in parts]
            l_scr[r, :] = alpha * l_scr[r, :] + functools.reduce(jnp.add, pes)
            p_scr[r, 0:tc] = jnp.concatenate(pes, axis=1).astype(BF16)
            a_scr[r, :] = alpha
            m_scr[r, :] = m_new
        acc_scr[...] = a_scr[...] * acc_scr[...] + _nn(p_scr[:, 0:tc], v)

    def full_chunk(ci, carry):
        chunk(ci, False)
        return carry

    lax.fori_loop(0, n_chunks - 1, full_chunk, 0)
    chunk(n_chunks - 1, True)

    gate = jax.nn.sigmoid(bg_ref[0])
    for g in range(1, NSA_KV):
        gate = jnp.where(pl.program_id(1) == g, pltpu.roll(gate, LANE - g * hq * N_BRANCH, 1), gate)
    ng = ng_ref[0]
    for h in range(hq):
        r = slice(h * tq, (h + 1) * tq)
        o_s = acc_scr[r, :] * (1.0 / jnp.sum(l_scr[r, :], axis=-1, keepdims=True))
        o = (gate[:, 3 * h:3 * h + 1] * oc_scr[r, :] + gate[:, 3 * h + 1:3 * h + 2] * o_s
             + gate[:, 3 * h + 2:3 * h + 3] * (ow_scr[r, :] * wl_scr[r, :]))
        o_ref[0, :, h * LANE:(h + 1) * LANE] = (ng[:, h * LANE:(h + 1) * LANE] * o).astype(o_ref.dtype)


def _nsa_prompt(qc3, qr3, ng3, bg3, kvb, kc, vc, *, tq=128, tc=512, sub=64, pool=None):
    b, t, _ = qc3.shape
    nq = t // tq
    tc = min(tc, t)
    tw = min(WINDOW + tq, t)
    n_sel = t // SEL_BLOCK
    top = min(TOP_N, n_sel)
    npad = kc.shape[2]
    gw = NSA_HPG * LANE
    rows = NSA_HPG * tq
    wide = max(tc, npad)
    kvspec = lambda k: pl.BlockSpec((1, t, LANE), lambda bi, gi, qi, *_, k=k: (bi, 0, k + gi))
    cspec = pl.BlockSpec((1, 1, npad, LANE), lambda bi, gi, qi, *_: (bi, gi, 0, 0))
    gspec = pl.BlockSpec((1, tq, gw), lambda bi, gi, qi, *_: (bi, qi, gi))
    in_specs = [gspec, gspec, gspec,
                pl.BlockSpec((1, tq, LANE), lambda bi, gi, qi, *_: (bi, qi, 0)),
                kvspec(0), kvspec(2), kvspec(4), kvspec(6), cspec, cspec]
    out_shape = [jax.ShapeDtypeStruct((b, t, NSA_HEADS * LANE), BF16)]
    out_specs = [pl.BlockSpec((1, tq, gw), lambda bi, gi, qi, *_: (bi, qi, gi))]
    assert tc % tq == 0 and npad == LANE
    scratch = ([pltpu.VMEM((rows, LANE), BF16), pltpu.VMEM((rows, 2 * LANE), BF16)]
               + [pltpu.VMEM((rows, wide), F32), pltpu.VMEM((rows, wide), BF16)]
               + [pltpu.VMEM((rows, LANE), F32)] * 5 + [pltpu.VMEM((tq, wide), F32)]
               + [pltpu.VMEM((rows, tw), F32), pltpu.VMEM((rows, tw), BF16), pltpu.VMEM((tq, tw), F32)]
               + [pltpu.VMEM((rows, LANE), F32)] * 2)
    args = [qc3, qr3, ng3, bg3, kvb, kvb, kvb, kvb, kc, vc]
    n_steps = b * NSA_KV * nq
    pool_pages, page_base, prefetch = 0, 0, []
    if pool is not None:
        cache4, page_flat, a3, page_base = pool
        page, n_cols = cache4.shape[1], 2 * NSA_KV
        if page_flat.shape[0] % n_steps == 0 and page == 2 * CMP_BLOCK and 2 * n_cols == 8:
            pool_pages = page_flat.shape[0] // n_steps
            prefetch = [page_flat]
            in_specs += [pl.BlockSpec((CMP_BLOCK, 2 * n_cols, LANE), lambda bi, gi, qi, *_: (0, 0, 0)),
                         pl.BlockSpec(memory_space=pl.ANY)]
            args += [jnp.concatenate([a3, a3], axis=1), cache4]
            out_shape.append(jax.ShapeDtypeStruct((n_steps, 2 * pool_pages, n_cols * LANE), F32))
            out_specs.append(pl.BlockSpec((1, 2 * pool_pages, n_cols * LANE),
                                          lambda bi, gi, qi, *_: ((bi * NSA_KV + gi) * nq + qi, 0, 0)))
            scratch += [pltpu.VMEM((2, pool_pages, CMP_BLOCK, 2 * n_cols, LANE), F32),
                        pltpu.VMEM((pool_pages * 2 * n_cols, LANE), F32), pltpu.SemaphoreType.DMA((2,))]
    kern = functools.partial(_nsa_prompt_kernel, t_len=t, tq=tq, tc=tc, tw=tw, n_sel=n_sel, top=top,
                             sub=min(sub, tq), pool_pages=pool_pages, page_base=page_base)
    res = pl.pallas_call(
        kern,
        out_shape=out_shape,
        grid_spec=pltpu.PrefetchScalarGridSpec(
            num_scalar_prefetch=len(prefetch), grid=(b, NSA_KV, nq),
            in_specs=in_specs, out_specs=out_specs, scratch_shapes=scratch),
        compiler_params=_params(("arbitrary", "arbitrary", "arbitrary")),
        name="nsa_prompt",
    )(*prefetch, *args)
    return (res[0], res[1]) if pool_pages else (res[0], None)


def _conv_kernel(h_ref, b_ref, c_ref, g_ref, prev_ref, w_ref, y_ref, st_ref, up_scr, *, t_real):
    u = c_ref[...] * h_ref[...]
    t = u.shape[1]
    up_scr[:, pl.ds(8 - (CONV_W - 1), CONV_W - 1), :] = prev_ref[...]
    up_scr[:, pl.ds(8, t), :] = u
    w = w_ref[...]
    y = w[0:1, :][None] * up_scr[:, pl.ds(6, t), :]
    y = y + w[1:2, :][None] * up_scr[:, pl.ds(7, t), :]
    y = y + w[2:3, :][None] * u
    y = b_ref[...] * y
    y_ref[...] = (_silu(g_ref[...]) * y).astype(y_ref.dtype)
    st_ref[...] = up_scr[:, pl.ds(6 + t_real, CONV_W - 1), :]


def _conv_mixer(z3, prev, w_conv, *, t_real):
    b, t, _ = z3.shape
    nc = CONV_DIM // LANE
    zspec = lambda off: pl.BlockSpec((b, t, LANE), lambda ci, off=off: (0, 0, off // LANE + ci))
    return pl.pallas_call(
        functools.partial(_conv_kernel, t_real=t_real),
        out_shape=[jax.ShapeDtypeStruct((b, t, CONV_DIM), BF16),
                   jax.ShapeDtypeStruct((b, CONV_W - 1, CONV_DIM), F32)],
        grid=(nc,),
        in_specs=[zspec(C_H), zspec(C_B), zspec(C_C), zspec(C_G),
                  pl.BlockSpec((b, CONV_W - 1, LANE), lambda ci: (0, 0, ci)),
                  pl.BlockSpec((CONV_W, LANE), lambda ci: (0, ci))],
        out_specs=[pl.BlockSpec((b, t, LANE), lambda ci: (0, 0, ci)),
                   pl.BlockSpec((b, CONV_W - 1, LANE), lambda ci: (0, 0, ci))],
        scratch_shapes=[pltpu.VMEM((b, t + 8, LANE), F32)],
        compiler_params=_params(("parallel",)),
        name="conv_mixer",
    )(z3, z3, z3, z3, prev, w_conv)


def _mem_heads(q, mg, kv_ref, store):
    nm = kv_ref.shape[1] // (2 * MEM_HEADS)
    for h in range(MEM_HEADS):
        lo, hi = h * LANE, (h + 1) * LANE
        k = kv_ref[0, pl.ds(h, nm, stride=2 * MEM_HEADS), :].astype(BF16)
        v = kv_ref[0, pl.ds(MEM_HEADS + h, nm, stride=2 * MEM_HEADS), :].astype(BF16)
        s = _nt((q[:, lo:hi] * ATTN_SCALE).astype(BF16), k)
        e = jnp.exp(s - jnp.max(s, axis=-1, keepdims=True))
        o = _nn(e.astype(BF16), v) / jnp.sum(e, axis=-1, keepdims=True)
        store(lo, hi, _silu(mg[:, lo:hi]) * o)


def _mem_attn_kernel(q_ref, mg_ref, kv_ref, o_ref):
    def store(lo, hi, y):
        o_ref[0, :, lo:hi] = y.astype(o_ref.dtype)

    _mem_heads(q_ref[0], mg_ref[0], kv_ref, store)


def _mem_attn(zb3, mkv, *, mq_off, mg_off, tq, kv_base=0):
    b, t, _ = zb3.shape
    wq = MEM_HEADS * LANE
    return pl.pallas_call(
        _mem_attn_kernel,
        out_shape=jax.ShapeDtypeStruct((b, t, wq), BF16),
        grid=(b, t // tq),
        in_specs=[pl.BlockSpec((1, tq, wq), lambda bi, ti: (bi, ti, mq_off // wq)),
                  pl.BlockSpec((1, tq, wq), lambda bi, ti: (bi, ti, mg_off // wq)),
                  pl.BlockSpec((1,) + mkv.shape[1:], lambda bi, ti: (bi + kv_base, 0, 0))],
        out_specs=pl.BlockSpec((1, tq, wq), lambda bi, ti: (bi, ti, 0)),
        compiler_params=_params(("parallel", "parallel")),
        name="mem_attn",
    )(zb3, zb3, mkv)


def _out_proj_kernel(x_ref, ya_ref, yb_ref, ym_ref, xs_ref, yas_ref, ybs_ref, yms_ref, w_ref, fg_ref,
                     o_ref, os_ref, w_scr, *, final):
    @pl.when(pl.program_id(0) == 0)
    def _():
        w_scr[...] = w_ref[...].astype(BF16)

    a, bw = CONV_DIM, CONV_DIM + NSA_HEADS * LANE

    def mix(x, ya, yb, ym):
        acc = _nn(ya, w_scr[0:a, :])
        acc = acc + _nn(yb, w_scr[a:bw, :])
        acc = acc + _nn(ym, w_scr[bw:, :])
        r = x + acc
        if final:
            r = r * lax.rsqrt(jnp.mean(r * r, axis=-1, keepdims=True) + NORM_EPS) * fg_ref[...]
        return r

    o_ref[...] = mix(x_ref[...], ya_ref[...], yb_ref[...], ym_ref[...])

    @pl.when(pl.program_id(0) == pl.num_programs(0) - 1)
    def _():
        os_ref[...] = mix(xs_ref[...], yas_ref[...], ybs_ref[...], yms_ref[...])


def _out_proj(x, ya, yb, ym, sample, w, fg, *, tm, final):
    m, d = x.shape
    row = lambda width: pl.BlockSpec((tm, width), lambda i: (i, 0))
    whole = lambda a: pl.BlockSpec(a.shape, lambda i: (0, 0), pipeline_mode=pl.Buffered(1))
    return pl.pallas_call(
        functools.partial(_out_proj_kernel, final=final),
        out_shape=[jax.ShapeDtypeStruct((m, d), F32), jax.ShapeDtypeStruct(sample[0].shape, F32)],
        grid=(m // tm,),
        in_specs=[row(d), row(ya.shape[1]), row(yb.shape[1]), row(ym.shape[1])] + [whole(a) for a in sample]
                 + [whole(w), pl.BlockSpec((1, d), lambda i: (0, 0), pipeline_mode=pl.Buffered(1))],
        out_specs=[row(d), pl.BlockSpec(sample[0].shape, lambda i: (0, 0))],
        scratch_shapes=[pltpu.VMEM(w.shape, BF16)],
        compiler_params=_params(("arbitrary",)),
        name="out_proj",
    )(x, ya, yb, ym, *sample, w, fg.reshape(1, d))


def _sample_pre_kernel(q_ref, ng_ref, bg_ref, kv0_ref, kv1_ref, kv2_ref, c_ref, s1_ref, s2_ref, kc_ref, vc_ref,
                       kvn_ref, wn_ref, qr_ref, oc_ref, ngo_ref, gate_ref, val_ref, *, past):
    tp = SAMPLE_T_PAD
    hq = NSA_HPG
    c, s1, s2 = c_ref[...], s1_ref[...], s2_ref[...]
    kv0, kv1, kv2 = kv0_ref[0], kv1_ref[0], kv2_ref[0]
    kvn_ref[0, :, 0:512] = kv0
    kvn_ref[0, :, 768:1024] = kv1[:, 256:512]
    wn_ref[0, :, 256:512] = kv2[:, 256:512]
    for g in range(NSA_KV):
        lo, hi = g * LANE, (g + 1) * LANE
        kvn_ref[0, :, 512 + lo:512 + hi] = _rope(kv1[:, lo:hi], c, s1, s2)
        wn_ref[0, :, lo:hi] = _rope(kv2[:, lo:hi], c, s1, s2)

    q = q_ref[0]
    ng = ng_ref[0]
    gates = jax.nn.sigmoid(bg_ref[0])
    for g in range(NSA_KV):
        qc_l, qr_l = [], []
        for h in range(hq):
            lo = (g * hq + h) * LANE
            qh = q[:, lo:lo + LANE]
            qc_l.append(qh * ATTN_SCALE)
            qr_l.append(_rope(qh, c, s1, s2) * ATTN_SCALE)
            ngo_ref[0, g, h * tp:(h + 1) * tp, :] = ng[:, lo:lo + LANE]
            for br in range(N_BRANCH):
                col = (g * hq + h) * N_BRANCH + br
                gate_ref[0, g, br, h * tp:(h + 1) * tp, :] = jnp.broadcast_to(gates[:, col:col + 1], (tp, LANE))
        qc = jnp.concatenate(qc_l, axis=0)
        qr_ref[0, g] = jnp.concatenate(qr_l, axis=0)

        kc = kc_ref[0, g]
        npad = kc.shape[0]
        s = _nt(qc.astype(BF16), kc)
        trow = lax.broadcasted_iota(jnp.int32, (hq * tp, npad), 0) % tp
        ncol = lax.broadcasted_iota(jnp.int32, (hq * tp, npad), 1)
        cmask = (ncol + 1) * CMP_BLOCK <= past + trow + 1
        s = jnp.where(cmask, s, MASK_NEG)
        e = jnp.where(cmask, jnp.exp(s - jnp.max(s, axis=-1, keepdims=True)), 0.0)
        p = e / jnp.maximum(jnp.sum(e, axis=-1, keepdims=True), 1e-30)
        oc_ref[0, g] = _nn(p.astype(BF16), vc_ref[0, g])
        imp = jnp.sum(p.reshape(hq, tp, npad), axis=0)

        blk = lax.broadcasted_iota(jnp.int32, (tp, npad), 1)
        cur = (past + lax.broadcasted_iota(jnp.int32, (tp, npad), 0)) // SEL_BLOCK
        val = jnp.where((blk == 0) | (blk == cur) | (blk == cur - 1), FORCE, imp)
        val = jnp.where(blk > cur, -1.0, val)
        val = jnp.where(blk >= past // SEL_BLOCK, -2.0, val)
        val_ref[0, g] = val


def _topk_kernel(val_ref, idx_ref, *, n_top):
    val = val_ref[...]
    rows, n = val.shape
    blk = lax.broadcasted_iota(jnp.int32, (rows, n), 1)
    lane = lax.broadcasted_iota(jnp.int32, (rows, LANE), 1)
    idx = jnp.zeros((rows, LANE), jnp.int32)
    for r in range(n_top):
        best = jnp.max(val, axis=-1, keepdims=True)
        j = jnp.min(jnp.where(val == best, blk, n), axis=-1, keepdims=True)
        idx = jnp.where(lane == r, j, idx)
        val = jnp.where(blk == j, -3e38, val)
    idx_ref[...] = idx


def _topk(val2, *, n_top):
    rows, n = val2.shape
    return pl.pallas_call(
        functools.partial(_topk_kernel, n_top=n_top),
        out_shape=jax.ShapeDtypeStruct((rows, LANE), jnp.int32),
        grid=(1,),
        in_specs=[pl.BlockSpec((rows, n), lambda i: (0, 0))],
        out_specs=pl.BlockSpec((rows, LANE), lambda i: (0, 0)),
        compiler_params=_params(("arbitrary",)),
        name="sample_topk",
    )(val2)


def _sample_pre(za3, zb3, bg3, tabs, kc, vc, *, past):
    bs, tp, _ = za3.shape
    npad = kc.shape[2]
    qw = NSA_HEADS * LANE
    kvblk = KV_OFF // 512
    zspec = lambda k: pl.BlockSpec((1, tp, 512), lambda bi, k=k: (bi, 0, kvblk + k))
    tspec = pl.BlockSpec((tp, LANE), lambda bi: (0, 0))
    cspec = pl.BlockSpec((1, NSA_KV, npad, LANE), lambda bi: (bi, 0, 0, 0))
    rows = NSA_HPG * tp
    gspec = pl.BlockSpec((1, NSA_KV, rows, LANE), lambda bi: (bi, 0, 0, 0))
    gshape = jax.ShapeDtypeStruct((bs, NSA_KV, rows, LANE), F32)
    return pl.pallas_call(
        functools.partial(_sample_pre_kernel, past=past),
        out_shape=[jax.ShapeDtypeStruct((bs, tp, 1024), F32),
                   jax.ShapeDtypeStruct((bs, tp, 512), F32),
                   gshape, gshape, gshape,
                   jax.ShapeDtypeStruct((bs, NSA_KV, N_BRANCH, rows, LANE), F32),
                   jax.ShapeDtypeStruct((bs, NSA_KV, tp, npad), F32)],
        grid=(bs,),
        in_specs=[pl.BlockSpec((1, tp, qw), lambda bi: (bi, 0, Q_OFF // qw)),
                  pl.BlockSpec((1, tp, qw), lambda bi: (bi, 0, NG_OFF // qw)),
                  pl.BlockSpec((1, tp, LANE), lambda bi: (bi, 0, 0)),
                  zspec(0), zspec(1), zspec(2), tspec, tspec, tspec, cspec, cspec],
        out_specs=[pl.BlockSpec((1, tp, 1024), lambda bi: (bi, 0, 0)),
                   pl.BlockSpec((1, tp, 512), lambda bi: (bi, 0, 0)),
                   gspec, gspec, gspec,
                   pl.BlockSpec((1, NSA_KV, N_BRANCH, rows, LANE), lambda bi: (bi, 0, 0, 0, 0)),
                   pl.BlockSpec((1, NSA_KV, tp, npad), lambda bi: (bi, 0, 0, 0))],
        compiler_params=_params(("parallel",)),
        name="sample_pre",
    )(za3, za3, bg3, zb3, zb3, zb3, *tabs, kc, vc)


def _sample_attn_kernel(idx_sm, pt_sm, qr_ref, oc_ref, ng_ref, gate_ref, ksn_ref, vsn_ref,
                        wc_ref, kwn_ref, vwn_ref, wn_ref, cache_ref, o_ref, win_ref,
                        kbuf, vbuf, kw_scr, vw_scr, sem, *, ts, n_top, n_pages, page_base, per_page, wb):
    tp = SAMPLE_T_PAD
    hq = NSA_HPG
    b = pl.program_id(0)
    g = pl.program_id(1)
    n_gath = n_top * SEL_BLOCK
    ks_rows = kbuf.shape[2]
    step = b * NSA_KV + g
    n_steps = pl.num_programs(0) * NSA_KV
    slot = step % 2

    def gather_copies(step_idx, t, r, to_slot):
        bb, gg = step_idx // NSA_KV, step_idx % NSA_KV
        blk = idx_sm[(step_idx * ts + t) * n_top + r]
        page = pt_sm[bb * n_pages + blk // per_page] + page_base
        row0 = (blk % per_page) * SEL_BLOCK
        src_k = cache_ref.at[page, pl.ds(row0, SEL_BLOCK), 2 * NSA_KV + gg]
        src_v = cache_ref.at[page, pl.ds(row0, SEL_BLOCK), 3 * NSA_KV + gg]
        dst = pl.ds(r * SEL_BLOCK, SEL_BLOCK)
        return (pltpu.make_async_copy(src_k, kbuf.at[to_slot, t, dst], sem.at[to_slot, 0]),
                pltpu.make_async_copy(src_v, vbuf.at[to_slot, t, dst], sem.at[to_slot, 1]))

    def start_gathers(step_idx, to_slot):
        for t in range(ts):
            for r in range(n_top):
                ck, cv = gather_copies(step_idx, t, r, to_slot)
                ck.start()
                cv.start()

    @pl.when(step == 0)
    def _():
        start_gathers(0, 0)

    @pl.when(step + 1 < n_steps)
    def _():
        start_gathers(step + 1, 1 - slot)

    @pl.when(g == 0)
    def _():
        n_w = 2 * NSA_KV
        kept = (wb - ts) * n_w
        win_ref[0, pl.ds(0, kept), :] = wc_ref[0, pl.ds(ts * n_w, kept), :]
        for part in range(n_w):
            win_ref[0, pl.ds(kept + part, ts, stride=n_w), :] = wn_ref[0, 0:ts, part * LANE:(part + 1) * LANE]

    qr = qr_ref[0, 0].astype(BF16)
    trow = lax.broadcasted_iota(jnp.int32, (hq * tp, 1), 0) % tp

    ww = kw_scr.shape[0]
    kw_scr[pl.ds(0, wb), :] = wc_ref[0, pl.ds(g, wb, stride=2 * NSA_KV), :]
    vw_scr[pl.ds(0, wb), :] = wc_ref[0, pl.ds(NSA_KV + g, wb, stride=2 * NSA_KV), :]
    kw_scr[pl.ds(wb, tp), :] = kwn_ref[0]
    vw_scr[pl.ds(wb, tp), :] = vwn_ref[0]
    kw_scr[pl.ds(wb + tp, ww - wb - tp), :] = jnp.zeros((ww - wb - tp, LANE), F32)
    vw_scr[pl.ds(wb + tp, ww - wb - tp), :] = jnp.zeros((ww - wb - tp, LANE), F32)
    sw = _nt(qr, kw_scr[...].astype(BF16))
    jw = lax.broadcasted_iota(jnp.int32, (hq * tp, ww), 1)
    rel = jw - wb
    okw = (rel <= trow) & (rel > trow - WINDOW) & (jw < wb + ts)
    sw = jnp.where(okw, sw, MASK_NEG)
    ew = jnp.where(okw, jnp.exp(sw - jnp.max(sw, axis=-1, keepdims=True)), 0.0)
    o_w = _nn(ew.astype(BF16), vw_scr[...].astype(BF16)) / jnp.sum(ew, axis=-1, keepdims=True)

    for t in range(ts):
        for r in range(n_top):
            ck, cv = gather_copies(step, t, r, slot)
            ck.wait()
            cv.wait()

    js = lax.broadcasted_iota(jnp.int32, (hq * tp, ks_rows), 1)
    o_s = jnp.zeros((hq * tp, LANE), F32)
    for t in range(ts):
        kbuf[slot, t, pl.ds(n_gath, tp), :] = ksn_ref[0]
        vbuf[slot, t, pl.ds(n_gath, tp), :] = vsn_ref[0]
        kbuf[slot, t, pl.ds(n_gath + tp, ks_rows - n_gath - tp), :] = jnp.zeros((ks_rows - n_gath - tp, LANE), F32)
        vbuf[slot, t, pl.ds(n_gath + tp, ks_rows - n_gath - tp), :] = jnp.zeros((ks_rows - n_gath - tp, LANE), F32)
        ss = _nt(qr, kbuf[slot, t].astype(BF16))
        oks = (js < n_gath) | ((js - n_gath <= t) & (js < n_gath + ts))
        ss = jnp.where(oks, ss, MASK_NEG)
        es = jnp.where(oks, jnp.exp(ss - jnp.max(ss, axis=-1, keepdims=True)), 0.0)
        ot = _nn(es.astype(BF16), vbuf[slot, t].astype(BF16)) / jnp.sum(es, axis=-1, keepdims=True)
        o_s = jnp.where(trow == t, ot, o_s)

    o = gate_ref[0, 0, 0] * oc_ref[0, 0] + gate_ref[0, 0, 1] * o_s + gate_ref[0, 0, 2] * o_w
    y = _silu(ng_ref[0, 0]) * o
    for h in range(hq):
        o_ref[0, :, h * LANE:(h + 1) * LANE] = y[h * tp:(h + 1) * tp].astype(o_ref.dtype)


def _sample_attn(idx_flat, page_flat, qr, oc, ngo, gates, kvn, cache_win_rows, wn, cache4, *,
                 ts, n_top, n_pages, page_base, win_base):
    bs = qr.shape[0]
    tp = SAMPLE_T_PAD
    rows = NSA_HPG * tp
    wb = cache_win_rows.shape[1] // (2 * NSA_KV)
    assert ts <= wb and (ts * 2 * NSA_KV) % 8 == 0
    per_page = cache4.shape[1] // SEL_BLOCK
    ks_rows = -(-(n_top * SEL_BLOCK + tp) // LANE) * LANE
    ww = -(-(wb + tp) // LANE) * LANE
    gspec = pl.BlockSpec((1, 1, rows, LANE), lambda bi, gi, *_: (bi, gi, 0, 0))
    newspec = lambda k: pl.BlockSpec((1, tp, LANE), lambda bi, gi, *_, k=k: (bi, 0, k + gi))
    grid_spec = pltpu.PrefetchScalarGridSpec(
        num_scalar_prefetch=2, grid=(bs, NSA_KV),
        in_specs=[gspec, gspec, gspec,
                  pl.BlockSpec((1, 1, N_BRANCH, rows, LANE), lambda bi, gi, *_: (bi, gi, 0, 0, 0)),
                  newspec(2 * NSA_KV), newspec(3 * NSA_KV),
                  pl.BlockSpec((1,) + cache_win_rows.shape[1:], lambda bi, gi, *_: (bi + win_base, 0, 0)),
                  newspec(0), newspec(NSA_KV),
                  pl.BlockSpec((1, tp, 2 * NSA_KV * LANE), lambda bi, gi, *_: (bi, 0, 0)),
                  pl.BlockSpec(memory_space=pl.ANY)],
        out_specs=[pl.BlockSpec((1, tp, NSA_HPG * LANE), lambda bi, gi, *_: (bi, 0, gi)),
                   pl.BlockSpec((1,) + cache_win_rows.shape[1:], lambda bi, gi, *_: (bi, 0, 0))],
        scratch_shapes=[pltpu.VMEM((2, ts, ks_rows, LANE), F32), pltpu.VMEM((2, ts, ks_rows, LANE), F32),
                        pltpu.VMEM((ww, LANE), F32), pltpu.VMEM((ww, LANE), F32),
                        pltpu.SemaphoreType.DMA((2, 2))])
    kern = functools.partial(_sample_attn_kernel, ts=ts, n_top=n_top, n_pages=n_pages,
                             page_base=page_base, per_page=per_page, wb=wb)
    return pl.pallas_call(
        kern,
        out_shape=[jax.ShapeDtypeStruct((bs, tp, NSA_HEADS * LANE), BF16),
                   jax.ShapeDtypeStruct((bs,) + cache_win_rows.shape[1:], F32)],
        grid_spec=grid_spec,
        compiler_params=_params(("arbitrary", "arbitrary")),
        name="sample_attn",
    )(idx_flat, page_flat, qr, oc, ngo, gates, kvn, kvn, cache_win_rows, wn, wn, wn, cache4)


def _rope_tables(first, n):
    f32 = np.float32
    freqs = np.power(f32(ROPE_THETA), -np.arange(ROPE_HALF, dtype=f32) * f32(2.0 / ROPE_DIM)).astype(f32)
    ang = (np.arange(first, first + n).astype(f32)[:, None] * freqs[None, :]).astype(f32)
    cos, sin = np.cos(ang).astype(f32), np.sin(ang).astype(f32)
    z16 = np.zeros((n, ROPE_HALF), f32)
    rest = LANE - ROPE_DIM
    c = np.concatenate([cos, cos, np.ones((n, rest), f32)], axis=1)
    s1 = np.concatenate([z16, sin, np.zeros((n, rest), f32)], axis=1)
    s2 = np.concatenate([-sin, z16, np.zeros((n, rest), f32)], axis=1)
    return jnp.asarray(c), jnp.asarray(s1), jnp.asarray(s2)


def _layer_weights(norm_g, w_in3, layer, w_conv, a_cmp, w_cmp, w_out):
    w_t3 = jnp.swapaxes(w_in3, 1, 2)
    assert w_t3.shape[1] == BG_SRC + BG_N + ZB_WIDTH and BG_SRC == ZA_WIDTH and BG_N % 8 == 0
    a4 = jnp.concatenate([a_cmp[0], a_cmp[0], a_cmp[1], a_cmp[1]], axis=1)
    a3 = jnp.stack([a_cmp[0], a_cmp[0], a_cmp[1], a_cmp[1]], axis=1)
    w4 = jnp.stack([w_cmp[0], w_cmp[0], w_cmp[1], w_cmp[1]]).astype(BF16)
    return dict(norm_g=norm_g, w_t3=w_t3, layer=layer, w_conv=w_conv, a4=a4, a3=a3, w4=w4,
                w_out=w_out)


def _prompt_layer(xp, xs2, mem_prompt, mem_norm_g, w_mem, lw, pool):
    b, t, d = xp.shape
    m = b * t
    x2 = xp.reshape(m, d)
    tabs = _rope_tables(0, t)
    wb = min(WINDOW, t)
    qc, qr, ng, bg, ya, conv_new, za_s, bg_s = _proj_conv(x2, lw["norm_g"], lw["w_t3"], lw["layer"], lw["w_conv"],
                                                          tabs, xs2, tm=min(512, t), seq_len=t)
    nm = mem_prompt.shape[1]
    mkv = _norm_matmul(mem_prompt.reshape(b * nm, d), mem_norm_g, w_mem, tm=min(512, b * nm), tn=512)
    mkv3 = mkv.reshape(b, nm * 2 * MEM_HEADS, LANE)
    kvn, kvb, win_rows, pooled_prompt, ym, zb_s = _proj_kv(x2, lw["norm_g"], lw["w_t3"], lw["layer"], tabs, lw["a4"],
                                                           xs2, mkv3, tm=wb, seq_len=t)
    kc, vc = _cmp_proj(pooled_prompt.reshape(b, t // CMP_BLOCK, -1), lw["w4"])
    r3 = lambda a: a.reshape(b, t, -1)
    yb, pooled = _nsa_prompt(r3(qc), r3(qr), r3(ng), r3(bg), r3(kvb), kc, vc, tq=min(256, t), pool=pool)
    mix_in = (x2, ya, yb.reshape(m, -1), ym)
    kv_new = kvn.reshape(b, t, 4, NSA_KV, HEAD_DIM)
    win_new = win_rows.reshape(b, wb, 2, NSA_KV, HEAD_DIM)
    mem_kv = mkv.reshape(b, nm, 2, MEM_HEADS, HEAD_DIM)
    return mix_in, kv_new, win_new, conv_new, mem_kv, pooled, (za_s, zb_s, bg_s)


def _sample_layer(xs_p, proj, ts, layer, cache4, page_flat, pooled, cache_win, state_conv, cache_mem, lw):
    bs, tp, d = xs_p.shape
    depth = cache_win.shape[0]
    pool, page = cache4.shape[0] // depth, cache4.shape[1]
    n_pages = page_flat.shape[0] // bs
    past = n_pages * page
    assert past % SEL_BLOCK == 0 and ts <= SEL_BLOCK and ts <= tp
    n_past = past // SEL_BLOCK
    n_top = min(TOP_N, n_past + 1) - 1
    m = bs * tp
    za, zb, bg = proj
    za3 = za.reshape(bs, tp, ZA_WIDTH)
    zb3 = zb.reshape(bs, tp, ZB_WIDTH)
    bg3 = bg.reshape(bs, tp, LANE)
    tabs = _rope_tables(past, tp)
    if pooled is None:
        pooled = _pool_pages(cache4, page_flat, lw["a3"], bs=bs, n_pages=n_pages, page_base=layer * pool)
    else:
        pooled = pooled.reshape(bs, n_pages * (page // CMP_BLOCK), 2 * NSA_KV * LANE)
    kc, vc = _cmp_proj(pooled, lw["w4"])
    kvn, wn, qr, oc, ngo, gates, val = _sample_pre(za3, zb3, bg3, tabs, kc, vc, past=past)
    idx = _topk(val.reshape(bs * NSA_KV * tp, val.shape[-1]), n_top=n_top)
    idx_flat = idx.reshape(bs, NSA_KV, tp, LANE)[:, :, :ts, :n_top].reshape(-1)
    wbuf = cache_win.shape[2]
    cache_win_rows = cache_win.reshape(depth * bs, wbuf * 2 * NSA_KV, HEAD_DIM)
    yb, win_rows = _sample_attn(idx_flat, page_flat, qr, oc, ngo, gates, kvn, cache_win_rows, wn, cache4,
                                ts=ts, n_top=n_top, n_pages=n_pages, page_base=layer * pool, win_base=layer * bs)
    ya, conv_new = _conv_mixer(za3, state_conv[layer], lw["w_conv"], t_real=ts)
    nm = cache_mem.shape[2]
    mem_rows = cache_mem.reshape(depth * bs, nm * 2 * MEM_HEADS, HEAD_DIM)
    ym = _mem_attn(zb3, mem_rows, mq_off=MQ_OFF, mg_off=MG_OFF, tq=tp, kv_base=layer * bs)
    mix_in = (xs_p.reshape(m, d), ya.reshape(m, -1), yb.reshape(m, -1), ym.reshape(m, -1))
    kv_new = kvn[:, :ts].reshape(bs, ts, 4, NSA_KV, HEAD_DIM)
    win_state = win_rows.reshape(bs, wbuf, 2, NSA_KV, HEAD_DIM)
    return mix_in, kv_new, win_state, conv_new


def kernel(x_prompt, x_sample, cache_kv, cache_win, state_conv, cache_mem, page_table, mem_prompt,
           norm_g, w_in, w_conv, a_cmp, w_cmp, mem_norm_g, w_mem_kv, w_out, final_g):
    depth = w_in.shape[0]
    ts = x_sample.shape[1]
    xp = x_prompt
    xs = jnp.pad(x_sample, ((0, 0), (0, SAMPLE_T_PAD - ts), (0, 0)))
    pool_size, page = cache_kv.shape[1], cache_kv.shape[2]
    cache4 = cache_kv.reshape(depth * pool_size, page, 4 * NSA_KV, HEAD_DIM)
    page_flat = page_table.reshape(-1).astype(jnp.int32)
    kv_p, win_p, conv_p, mem_p, kv_s, win_s, conv_s = [], [], [], [], [], [], []
    for l in range(depth):
        lw = _layer_weights(norm_g[l], w_in, l, w_conv[l], a_cmp[l], w_cmp[l], w_out[l])
        final = l == depth - 1
        mix_p, kvn, winn, convn, mkv, pooled, proj_s = _prompt_layer(
            xp, xs.reshape(-1, xs.shape[-1]), mem_prompt, mem_norm_g[l], w_mem_kv[l], lw,
            pool=(cache4, page_flat, lw["a3"], l * pool_size))
        kv_p.append(kvn)
        win_p.append(winn)
        conv_p.append(convn)
        mem_p.append(mkv)
        mix_s, kvn, winn, convn = _sample_layer(xs, proj_s, ts, l, cache4, page_flat, pooled, cache_win, state_conv,
                                                cache_mem, lw)
        kv_s.append(kvn)
        win_s.append(winn)
        conv_s.append(convn)
        out_p, out_s = _out_proj(*mix_p, mix_s, lw["w_out"], final_g, tm=min(512, mix_p[0].shape[0]), final=final)
        xp = out_p.reshape(xp.shape)
        xs = out_s.reshape(xs.shape)
    return (xp, xs[:, :ts], jnp.stack(kv_p), jnp.stack(win_p), jnp.stack(conv_p), jnp.stack(mem_p),
            jnp.stack(kv_s), jnp.stack(win_s), jnp.stack(conv_s))
```

```python
import functools

import jax
import jax.numpy as jnp
import numpy as np
from jax import lax
from jax.experimental import pallas as pl
from jax.experimental.pallas import tpu as pltpu

F32 = jnp.float32
BF16 = jnp.bfloat16

HEAD_DIM = 128
CONV_DIM = 512
CONV_W = 3
NSA_HEADS = 8
NSA_KV = 2
NSA_HPG = NSA_HEADS // NSA_KV
MEM_HEADS = 4
N_BRANCH = 3
ROPE_DIM = HEAD_DIM // 4
ROPE_HALF = ROPE_DIM // 2
ROPE_THETA = 500000.0
CMP_BLOCK = 64
SEL_BLOCK = 64
SEL_SHIFT = 6
TOP_N = 16
WINDOW = 512
NORM_EPS = 1e-6
MASK_NEG = -1e30
FORCE = 1e9
ATTN_SCALE = HEAD_DIM ** -0.5
SCALE_LOG2 = ATTN_SCALE * 1.4426950408889634

C_H, C_B, C_C, C_G, Q_OFF, NG_OFF = 0, 512, 1024, 1536, 2048, 3072
ZA_WIDTH = 4096
KV_OFF, MQ_OFF, MG_OFF = 0, 1536, 2048
ZB_WIDTH = 2560
BG_SRC = 4096
BG_N = NSA_HEADS * N_BRANCH
LANE = 128
SAMPLE_T_PAD = 8
VMEM_LIMIT = 56 * 1024 * 1024
W_CHUNK = 512


def _nt(a, b):
    return lax.dot_general(a, b, (((1,), (1,)), ((), ())), preferred_element_type=F32)


def _nn(a, b):
    return jnp.dot(a, b, preferred_element_type=F32)


def _params(sem, vmem=VMEM_LIMIT):
    return pltpu.CompilerParams(dimension_semantics=sem, vmem_limit_bytes=vmem)


def _rope(x, c, s1, s2):
    return x * c + pltpu.roll(x, ROPE_HALF, 1) * s1 + pltpu.roll(x, LANE - ROPE_HALF, 1) * s2


def _silu(x):
    return x * jax.nn.sigmoid(x)


def _rms_rows(x_ref, g_ref):
    x = x_ref[...]
    return (x * lax.rsqrt(jnp.mean(x * x, axis=-1, keepdims=True) + NORM_EPS) * g_ref[...]).astype(BF16)


def _norm_matmul_kernel(x_ref, g_ref, w_ref, z_ref, h_scr, *, tn):
    tm = x_ref.shape[0]
    parts = w_ref.shape[1] // LANE
    h_scr[...] = _rms_rows(x_ref, g_ref)
    for j in range(w_ref.shape[1] // tn):
        z = _nn(h_scr[...], w_ref[:, j * tn:(j + 1) * tn].astype(BF16))
        for k in range(tn // LANE):
            z_ref[pl.ds(j * (tn // LANE) + k, tm, stride=parts), :] = z[:, k * LANE:(k + 1) * LANE]


def _norm_matmul(x, g, w, *, tm, tn):
    m, d = x.shape
    n = w.shape[1]
    assert n % tn == 0 and tn % LANE == 0 and m % tm == 0 and w.shape[0] == d
    resident = lambda shape: pl.BlockSpec(shape, lambda i: (0, 0), pipeline_mode=pl.Buffered(1))
    return pl.pallas_call(
        functools.partial(_norm_matmul_kernel, tn=tn),
        out_shape=jax.ShapeDtypeStruct((m * (n // LANE), LANE), F32),
        grid=(m // tm,),
        in_specs=[pl.BlockSpec((tm, d), lambda i: (i, 0)), resident((1, d)), resident((d, n))],
        out_specs=pl.BlockSpec((tm * (n // LANE), LANE), lambda i: (i, 0)),
        scratch_shapes=[pltpu.VMEM((tm, d), BF16)],
        compiler_params=_params(("parallel",)),
        name="norm_matmul",
    )(x, g.reshape(1, d), w)


def _fetch_weight_rows(w_hbm, layer, first_row, w_scr, stage, sem):
    ch = stage.shape[1]
    n_chunks = w_scr.shape[0] // ch
    assert n_chunks * ch == w_scr.shape[0]

    def chunk_copy(c):
        return pltpu.make_async_copy(w_hbm.at[layer, pl.ds(first_row + c * ch, ch), :], stage.at[c % 2], sem.at[c % 2])

    chunk_copy(0).start()
    for c in range(n_chunks):
        if c + 1 < n_chunks:
            chunk_copy(c + 1).start()
        chunk_copy(c).wait()
        w_scr[c * ch:(c + 1) * ch, :] = stage[c % 2].astype(BF16)


def _fetch_gate_rows(w_hbm, layer, wbg_scr, stage, sem):
    cp = pltpu.make_async_copy(w_hbm.at[layer, pl.ds(BG_SRC, BG_N), :], stage.at[0, pl.ds(0, BG_N)], sem.at[0])
    cp.start()
    cp.wait()
    wbg_scr[...] = jnp.zeros(wbg_scr.shape, BF16)
    wbg_scr[0:BG_N, :] = stage[0, 0:BG_N, :].astype(BF16)


def _proj_conv_kernel(x_ref, g_ref, w_hbm, wc_ref, c_ref, s1_ref, s2_ref, xs_ref,
                      qc_ref, qr_ref, ng_ref, bg_ref, ya_ref, st_ref, zs_ref, bgs_ref,
                      h_scr, up_scr, w_ref, wbg_ref, stage, sem, *, blocks_per_seq, layer):
    tm = x_ref.shape[0]
    cw = CONV_DIM
    i = pl.program_id(0)
    first = i % blocks_per_seq == 0
    @pl.when(i == 0)
    def _():
        _fetch_weight_rows(w_hbm, layer, 0, w_ref, stage, sem)
        _fetch_gate_rows(w_hbm, layer, wbg_ref, stage, sem)
        up_scr[...] = jnp.zeros(up_scr.shape, F32)

    h_scr[...] = _rms_rows(x_ref, g_ref)
    bg_ref[...] = _nt(h_scr[...], wbg_ref[...])
    chunk = lambda off: _nt(h_scr[...], w_ref[off:off + cw, :])


    carry = up_scr[pl.ds(8 + tm - (CONV_W - 1), CONV_W - 1), :]
    up_scr[pl.ds(8 - (CONV_W - 1), CONV_W - 1), :] = jnp.where(first, 0.0, carry)
    u = chunk(C_C) * chunk(C_H)
    up_scr[pl.ds(8, tm), :] = u
    wc = wc_ref[...]
    y = wc[0:1, :] * up_scr[pl.ds(6, tm), :]
    y = y + wc[1:2, :] * up_scr[pl.ds(7, tm), :]
    y = y + wc[2:3, :] * u
    y = chunk(C_B) * y
    ya_ref[...] = (_silu(chunk(C_G)) * y).astype(ya_ref.dtype)

    st_ref[0] = up_scr[pl.ds(8 + tm - (CONV_W - 1), CONV_W - 1), :]

    c, s1, s2 = c_ref[...], s1_ref[...], s2_ref[...]
    for j in range((NG_OFF - Q_OFF) // cw):
        qv = chunk(Q_OFF + j * cw)
        for k in range(cw // LANE):
            lo = j * cw + k * LANE
            qh = qv[:, k * LANE:(k + 1) * LANE]
            qc_ref[:, lo:lo + LANE] = (qh * SCALE_LOG2).astype(BF16)
            qr_ref[:, lo:lo + LANE] = (_rope(qh, c, s1, s2) * SCALE_LOG2).astype(BF16)
    for j in range((ZA_WIDTH - NG_OFF) // cw):
        ng_ref[:, j * cw:(j + 1) * cw] = _silu(chunk(NG_OFF + j * cw))

    @pl.when(i == pl.num_programs(0) - 1)
    def _():
        hs = _rms_rows(xs_ref, g_ref)
        bgs_ref[...] = _nt(hs, wbg_ref[...])
        for j in range(ZA_WIDTH // cw):
            zs_ref[:, j * cw:(j + 1) * cw] = _nt(hs, w_ref[j * cw:(j + 1) * cw, :])


def _proj_conv(x2, g, w_t3, layer, w_conv, tabs, xs2, *, tm, seq_len):
    m, d = x2.shape
    ms = xs2.shape[0]
    assert seq_len % tm == 0 and CONV_W == 3
    blocks_per_seq = seq_len // tm
    nq, nng = NG_OFF - Q_OFF, ZA_WIDTH - NG_OFF
    resident = lambda shape: pl.BlockSpec(shape, lambda i: (0,) * len(shape), pipeline_mode=pl.Buffered(1))
    row = lambda width: pl.BlockSpec((tm, width), lambda i: (i, 0))
    tspec = pl.BlockSpec((tm, LANE), lambda i: (i % blocks_per_seq, 0))
    return pl.pallas_call(
        functools.partial(_proj_conv_kernel, blocks_per_seq=blocks_per_seq, layer=layer),
        out_shape=[jax.ShapeDtypeStruct((m, nq), BF16), jax.ShapeDtypeStruct((m, nq), BF16),
                   jax.ShapeDtypeStruct((m, nng), F32), jax.ShapeDtypeStruct((m, LANE), F32),
                   jax.ShapeDtypeStruct((m, CONV_DIM), BF16),
                   jax.ShapeDtypeStruct((m // seq_len, CONV_W - 1, CONV_DIM), F32),
                   jax.ShapeDtypeStruct((ms, ZA_WIDTH), F32), jax.ShapeDtypeStruct((ms, LANE), F32)],
        grid=(m // tm,),
        in_specs=[row(d), resident((1, d)), pl.BlockSpec(memory_space=pl.ANY), resident(w_conv.shape),
                  tspec, tspec, tspec, resident((ms, d))],
        out_specs=[row(nq), row(nq), row(nng), row(LANE), row(CONV_DIM),
                   pl.BlockSpec((1, CONV_W - 1, CONV_DIM), lambda i: (i // blocks_per_seq, 0, 0)),
                   pl.BlockSpec((ms, ZA_WIDTH), lambda i: (0, 0)), pl.BlockSpec((ms, LANE), lambda i: (0, 0))],
        scratch_shapes=[pltpu.VMEM((tm, d), BF16), pltpu.VMEM((tm + 8, CONV_DIM), F32),
                        pltpu.VMEM((ZA_WIDTH, d), BF16), pltpu.VMEM((LANE, d), BF16),
                        pltpu.VMEM((2, W_CHUNK, d), F32), pltpu.SemaphoreType.DMA((2,))],
        compiler_params=_params(("arbitrary",)),
        name="proj_conv",
    )(x2, g.reshape(1, d), w_t3, w_conv, *tabs, xs2)


def _proj_kv_kernel(x_ref, g_ref, w_hbm, c_ref, s1_ref, s2_ref, a_ref, xs_ref, mkv_ref,
                    kvn_ref, kvb_ref, win_ref, pool_ref, ym_ref, zs_ref, h_scr, w_ref, stage, sem,
                    *, blocks_per_seq, layer):
    tm = x_ref.shape[0]
    cw = 2 * NSA_KV * LANE

    @pl.when(pl.program_id(0) == 0)
    def _():
        _fetch_weight_rows(w_hbm, layer, BG_SRC + BG_N, w_ref, stage, sem)

    h_scr[...] = _rms_rows(x_ref, g_ref)
    chunk = lambda off: _nt(h_scr[...], w_ref[off:off + cw, :])
    c, s1, s2 = c_ref[...], s1_ref[...], s2_ref[...]
    kv0 = chunk(KV_OFF)
    kv1 = chunk(KV_OFF + cw)
    kv2 = chunk(KV_OFF + 2 * cw)
    n_kv, n_w, half = 4 * NSA_KV, 2 * NSA_KV, NSA_KV * LANE
    for g in range(NSA_KV):
        lo, hi = g * LANE, (g + 1) * LANE
        ks = _rope(kv1[:, lo:hi], c, s1, s2)
        kw = _rope(kv2[:, lo:hi], c, s1, s2)
        vs = kv1[:, half + lo:half + hi]
        vw = kv2[:, half + lo:half + hi]
        kvn_ref[pl.ds(g, tm, stride=n_kv), :] = kv0[:, lo:hi]
        kvn_ref[pl.ds(NSA_KV + g, tm, stride=n_kv), :] = kv0[:, half + lo:half + hi]
        kvn_ref[pl.ds(2 * NSA_KV + g, tm, stride=n_kv), :] = ks
        kvn_ref[pl.ds(3 * NSA_KV + g, tm, stride=n_kv), :] = vs
        kvb_ref[:, lo:hi] = ks.astype(BF16)
        kvb_ref[:, half + lo:half + hi] = vs.astype(BF16)
        kvb_ref[:, 2 * half + lo:2 * half + hi] = kw.astype(BF16)
        kvb_ref[:, 3 * half + lo:3 * half + hi] = vw.astype(BF16)
        win_ref[pl.ds(g, tm, stride=n_w), :] = kw
        win_ref[pl.ds(NSA_KV + g, tm, stride=n_w), :] = vw

    pool_ref[...] = jnp.sum(kv0.reshape(tm // CMP_BLOCK, CMP_BLOCK, cw) * a_ref[...][None], axis=1)

    def store(lo, hi, y):
        ym_ref[:, lo:hi] = y.astype(ym_ref.dtype)

    _mem_heads(chunk(MQ_OFF), chunk(MG_OFF), mkv_ref, store)

    @pl.when(pl.program_id(0) == pl.num_programs(0) - 1)
    def _():
        hs = _rms_rows(xs_ref, g_ref)
        for j in range(ZB_WIDTH // cw):
            zs_ref[:, j * cw:(j + 1) * cw] = _nt(hs, w_ref[j * cw:(j + 1) * cw, :])


def _proj_kv(x2, g, w_t3, layer, tabs, a4, xs2, mkv3, *, tm, seq_len):
    m, d = x2.shape
    ms = xs2.shape[0]
    wm = MEM_HEADS * LANE
    assert seq_len % tm == 0 and MG_OFF == MQ_OFF + wm and ZB_WIDTH == MG_OFF + wm
    blocks_per_seq = seq_len // tm
    n_kv, n_w = 4 * NSA_KV, 2 * NSA_KV
    resident = lambda shape: pl.BlockSpec(shape, lambda i: (0,) * len(shape), pipeline_mode=pl.Buffered(1))
    row = lambda width: pl.BlockSpec((tm, width), lambda i: (i, 0))
    tspec = pl.BlockSpec((tm, LANE), lambda i: (i % blocks_per_seq, 0))
    return pl.pallas_call(
        functools.partial(_proj_kv_kernel, blocks_per_seq=blocks_per_seq, layer=layer),
        out_shape=[jax.ShapeDtypeStruct((m * n_kv, LANE), F32),
                   jax.ShapeDtypeStruct((m, n_kv * LANE), BF16),
                   jax.ShapeDtypeStruct((m // seq_len * tm * n_w, LANE), F32),
                   jax.ShapeDtypeStruct((m // CMP_BLOCK, 2 * NSA_KV * LANE), F32),
                   jax.ShapeDtypeStruct((m, wm), BF16), jax.ShapeDtypeStruct((ms, ZB_WIDTH), F32)],
        grid=(m // tm,),
        in_specs=[row(d), resident((1, d)), pl.BlockSpec(memory_space=pl.ANY), tspec, tspec, tspec,
                  resident(a4.shape), resident((ms, d)),
                  pl.BlockSpec((1,) + mkv3.shape[1:], lambda i: (i // blocks_per_seq, 0, 0))],
        out_specs=[pl.BlockSpec((tm * n_kv, LANE), lambda i: (i, 0)), row(n_kv * LANE),
                   pl.BlockSpec((tm * n_w, LANE), lambda i: (i // blocks_per_seq, 0)),
                   pl.BlockSpec((tm // CMP_BLOCK, 2 * NSA_KV * LANE), lambda i: (i, 0)), row(wm),
                   pl.BlockSpec((ms, ZB_WIDTH), lambda i: (0, 0))],
        scratch_shapes=[pltpu.VMEM((tm, d), BF16), pltpu.VMEM((ZB_WIDTH, d), BF16),
                        pltpu.VMEM((2, W_CHUNK, d), F32), pltpu.SemaphoreType.DMA((2,))],
        compiler_params=_params(("arbitrary",)),
        name="proj_kv",
    )(x2, g.reshape(1, d), w_t3, *tabs, a4, xs2, mkv3)


def _pool_pages_kernel(pt_sm, a_ref, cache_ref, o_ref, buf, sem, *, pages_per_step, page_base):
    step = pl.program_id(0)
    n_steps = pl.num_programs(0)
    slot = step % 2
    n_cols = buf.shape[3]

    def page_copy(step_idx, p, to_slot):
        page = pt_sm[step_idx * pages_per_step + p] + page_base
        return pltpu.make_async_copy(cache_ref.at[page, :, pl.ds(0, n_cols), :], buf.at[to_slot, p], sem.at[to_slot])

    @pl.when(step == 0)
    def _():
        for p in range(pages_per_step):
            page_copy(0, p, 0).start()

    @pl.when(step + 1 < n_steps)
    def _():
        for p in range(pages_per_step):
            page_copy(step + 1, p, 1 - slot).start()

    for p in range(pages_per_step):
        page_copy(step, p, slot).wait()

    a = a_ref[...]
    per = buf.shape[2] // CMP_BLOCK
    for p in range(pages_per_step):
        for k in range(per):
            x = buf[slot, p, pl.ds(k * CMP_BLOCK, CMP_BLOCK)]
            o_ref[0, p, k] = jnp.sum(x * a, axis=0)


def _pool_pages(cache4, page_flat, a3, *, bs, n_pages, page_base, pages_per_step=16):
    page = cache4.shape[1]
    per = page // CMP_BLOCK
    n_cols = 2 * NSA_KV
    total = bs * n_pages
    pages_per_step = min(pages_per_step, total)
    assert total % pages_per_step == 0
    n_steps = total // pages_per_step
    grid_spec = pltpu.PrefetchScalarGridSpec(
        num_scalar_prefetch=1, grid=(n_steps,),
        in_specs=[pl.BlockSpec((CMP_BLOCK, n_cols, LANE), lambda si, pt: (0, 0, 0)),
                  pl.BlockSpec(memory_space=pl.ANY)],
        out_specs=pl.BlockSpec((1, pages_per_step, per, n_cols, LANE), lambda si, pt: (si, 0, 0, 0, 0)),
        scratch_shapes=[pltpu.VMEM((2, pages_per_step, page, n_cols, LANE), F32),
                        pltpu.SemaphoreType.DMA((2,))])
    out = pl.pallas_call(
        functools.partial(_pool_pages_kernel, pages_per_step=pages_per_step, page_base=page_base),
        out_shape=jax.ShapeDtypeStruct((n_steps, pages_per_step, per, n_cols, LANE), F32),
        grid_spec=grid_spec,
        compiler_params=_params(("arbitrary",)),
        name="pool_pages",
    )(page_flat, a3, cache4)
    return out.reshape(bs, n_pages * per, n_cols * LANE)


def _cmp_proj_kernel(p_ref, w_ref, kc_ref, vc_ref):
    pooled = p_ref[0]
    n = pooled.shape[0]
    n_pad = kc_ref.shape[2]
    for c in range(4):
        r = _nn(pooled[:, c * LANE:(c + 1) * LANE].astype(BF16), w_ref[c]).astype(BF16)
        dst = kc_ref if c < 2 else vc_ref
        if n_pad > n:
            dst[0, c % 2] = jnp.zeros((n_pad, LANE), BF16)
        dst[0, c % 2, 0:n, :] = r


def _cmp_proj(pooled, w4):
    b, n, _ = pooled.shape
    n_pad = -(-n // LANE) * LANE
    spec = pl.BlockSpec((1, NSA_KV, n_pad, LANE), lambda bi: (bi, 0, 0, 0))
    return pl.pallas_call(
        _cmp_proj_kernel,
        out_shape=[jax.ShapeDtypeStruct((b, NSA_KV, n_pad, LANE), BF16)] * 2,
        grid=(b,),
        in_specs=[pl.BlockSpec((1, n, 512), lambda bi: (bi, 0, 0)),
                  pl.BlockSpec((4, LANE, LANE), lambda bi: (0, 0, 0))],
        out_specs=[spec, spec],
        compiler_params=_params(("parallel",)),
        name="cmp_proj",
    )(pooled, w4)


def _lane_parts(x):
    return [x[:, j * LANE:(j + 1) * LANE] for j in range(x.shape[1] // LANE)]


def _nsa_prompt_kernel(*refs, t_len, tq, tc, tw, n_sel, top, sub, pool_pages, page_base):
    if pool_pages:
        (pt_sm, qc_ref, qr_ref, ng_ref, bg_ref, ksel_ref, vsel_ref, kwin_ref, vwin_ref, kc_ref, vc_ref,
         pa_ref, cache_ref, o_ref, pool_ref,
         qc_scr, qr_scr, s_scr, p_scr, a_scr, m_scr, l_scr, acc_scr, oc_scr, b_scr,
         sw_scr, pw_scr, bw_scr, ow_scr, wl_scr, pbuf, prow_scr, psem) = refs
        step = (pl.program_id(0) * pl.num_programs(1) + pl.program_id(1)) * pl.num_programs(2) + pl.program_id(2)
        n_steps = pl.num_programs(0) * pl.num_programs(1) * pl.num_programs(2)
        slot = step % 2
        half_rows, n_cols = pbuf.shape[2], 2 * NSA_KV

        def page_copies(step_idx, p, to_slot):
            page = pt_sm[step_idx * pool_pages + p] + page_base
            return [pltpu.make_async_copy(cache_ref.at[page, pl.ds(hh * half_rows, half_rows), pl.ds(0, n_cols), :],
                                          pbuf.at[to_slot, p, :, pl.ds(hh * n_cols, n_cols), :], psem.at[to_slot])
                    for hh in range(2)]

        @pl.when(step == 0)
        def _():
            for p in range(pool_pages):
                for cp in page_copies(0, p, 0):
                    cp.start()

        @pl.when(step + 1 < n_steps)
        def _():
            for p in range(pool_pages):
                for cp in page_copies(step + 1, p, 1 - slot):
                    cp.start()

        for p in range(pool_pages):
            for cp in page_copies(step, p, slot):
                cp.wait()
        pa = pa_ref[...]
        for p in range(pool_pages):
            prow_scr[p * 2 * n_cols:(p + 1) * 2 * n_cols, :] = jnp.sum(pbuf[slot, p] * pa, axis=0)
        for col in range(n_cols):
            pool_ref[0, :, col * LANE:(col + 1) * LANE] = prow_scr[pl.ds(col, 2 * pool_pages, stride=n_cols), :]
    else:
        (qc_ref, qr_ref, ng_ref, bg_ref, ksel_ref, vsel_ref, kwin_ref, vwin_ref, kc_ref, vc_ref,
         o_ref, qc_scr, qr_scr, s_scr, p_scr, a_scr, m_scr, l_scr, acc_scr, oc_scr, b_scr,
         sw_scr, pw_scr, bw_scr, ow_scr, wl_scr) = refs
    i = pl.program_id(2)
    hq = NSA_HPG
    for h in range(hq):
        qc_scr[pl.ds(h * tq, tq), :] = qc_ref[0, :, h * LANE:(h + 1) * LANE]
        qr_scr[pl.ds(h * tq, tq), 0:LANE] = qr_ref[0, :, h * LANE:(h + 1) * LANE]

    w0 = pl.multiple_of(jnp.clip(i * tq + tq - tw, 0, t_len - tw), LANE)
    kp = w0 + lax.broadcasted_iota(jnp.int32, (tq, tw), 1)
    tp = i * tq + lax.broadcasted_iota(jnp.int32, (tq, tw), 0)
    bw_scr[...] = jnp.where((kp <= tp) & (kp > tp - WINDOW), 0.0, MASK_NEG)
    sw_scr[...] = _nt(qr_scr[:, 0:LANE], kwin_ref[0, pl.ds(w0, tw), :])
    for r0 in range(0, hq * tq, sub):
        r = pl.ds(r0, sub)
        parts = _lane_parts(sw_scr[r, :] + bw_scr[pl.ds(r0 % tq, sub), :])
        m = jnp.max(functools.reduce(jnp.maximum, parts), axis=-1, keepdims=True)
        ews = [jnp.exp2(x - m) for x in parts]
        row_sum = jnp.sum(functools.reduce(jnp.add, ews), axis=-1, keepdims=True)
        wl_scr[r, :] = jnp.broadcast_to(1.0 / row_sum, (sub, LANE))
        pw_scr[r, :] = jnp.concatenate(ews, axis=1).astype(BF16)
    ow_scr[...] = _nn(pw_scr[...], vwin_ref[0, pl.ds(w0, tw), :])

    kc = kc_ref[0, 0]
    npad = kc.shape[0]
    s_scr[:, 0:npad] = _nt(qc_scr[...], kc)
    tpos = i * tq + lax.broadcasted_iota(jnp.int32, (tq, npad), 0)
    ncol = lax.broadcasted_iota(jnp.int32, (tq, npad), 1)
    cmask = (ncol + 1) * CMP_BLOCK <= tpos + 1
    imp = jnp.zeros((tq, npad), F32)
    for h in range(hq):
        r = pl.ds(h * tq, tq)
        s = jnp.where(cmask, s_scr[r, 0:npad], MASK_NEG)
        e = jnp.where(cmask, jnp.exp2(s - jnp.max(s, axis=-1, keepdims=True)), 0.0)
        p = e / jnp.maximum(jnp.sum(e, axis=-1, keepdims=True), 1e-30)
        p_scr[r, 0:npad] = p.astype(BF16)
        imp = imp + p
    oc_scr[...] = _nn(p_scr[:, 0:npad], vc_ref[0, 0])

    rows = min(npad, -(-n_sel // 8) * 8)
    imp_t = imp.T[0:rows]
    blk = lax.broadcasted_iota(jnp.int32, (rows, tq), 0)
    cur = lax.shift_right_logical(i * tq + lax.broadcasted_iota(jnp.int32, (rows, tq), 1), SEL_SHIFT)
    imp_t = jnp.where((blk == 0) | (blk == cur) | (blk == cur - 1), FORCE, imp_t)
    imp_t = jnp.where(blk > cur, -1.0, imp_t)
    imp_t = jnp.where(blk >= n_sel, -2.0, imp_t)
    rank = jnp.zeros((rows, tq), F32)
    for j in range(n_sel):
        a = imp_t[j:j + 1, :]
        ahead = (a > imp_t) | ((a == imp_t) & (blk > j))
        rank = rank + jnp.where(ahead, 1.0, 0.0)
    neg_t = jnp.where((rank < top) & (blk < n_sel) & (blk <= cur), 0.0, MASK_NEG)
    if npad > rows:
        neg_t = jnp.concatenate([neg_t, jnp.zeros((npad - rows, tq), F32)], axis=0)
    neg = neg_t.T.astype(BF16)
    for h in range(hq):
        qr_scr[pl.ds(h * tq, tq), LANE:2 * LANE] = neg

    m_scr[...] = jnp.full(m_scr.shape, MASK_NEG, F32)
    l_scr[...] = jnp.zeros(l_scr.shape, F32)
    acc_scr[...] = jnp.zeros(acc_scr.shape, F32)
    n_chunks = (i * tq + tq + tc - 1) // tc

    def chunk(ci, diagonal):
        k0 = pl.multiple_of(ci * tc, tc)
        v = vsel_ref[0, pl.ds(k0, tc), :]
        kb = lax.shift_right_logical(k0 + lax.broadcasted_iota(jnp.int32, (tc, npad), 0), SEL_SHIFT)
        jb = lax.broadcasted_iota(jnp.int32, (tc, npad), 1)
        k_aug = jnp.concatenate([ksel_ref[0, pl.ds(k0, tc), :], jnp.where(kb == jb, 1.0, 0.0).astype(BF16)], axis=1)
        s_scr[:, 0:tc] = _nt(qr_scr[...], k_aug)
        if diagonal:
            kp = k0 + lax.broadcasted_iota(jnp.int32, (tq, tc), 1)
            tp = i * tq + lax.broadcasted_iota(jnp.int32, (tq, tc), 0)
            b_scr[:, 0:tc] = jnp.where(kp <= tp, 0.0, MASK_NEG)
        for r0 in range(0, hq * tq, sub):
            r = pl.ds(r0, sub)
            sc = s_scr[r, 0:tc]
            if diagonal:
                sc = sc + b_scr[pl.ds(r0 % tq, sub), 0:tc]
            parts = _lane_parts(sc)
            m_prev = m_scr[r, :]
            m_new = jnp.maximum(m_prev, jnp.max(functools.reduce(jnp.maximum, parts), axis=-1, keepdims=True))
            alpha = jnp.exp2(m_prev - m_new)
            pes = [jnp.exp2(x - m_new) for x in parts]
            l_scr[r, :] = alpha * l_scr[r, :] + functools.reduce(jnp.add, pes)
            p_scr[r, 0:tc] = jnp.concatenate(pes, axis=1).astype(BF16)
            a_scr[r, :] = alpha
            m_scr[r, :] = m_new
        acc_scr[...] = a_scr[...] * acc_scr[...] + _nn(p_scr[:, 0:tc], v)

    def full_chunk(ci, carry):
        chunk(ci, False)
        return carry

    lax.fori_loop(0, n_chunks - 1, full_chunk, 0)
    chunk(n_chunks - 1, True)

    gate = jax.nn.sigmoid(bg_ref[0])
    for g in range(1, NSA_KV):
        gate = jnp.where(pl.program_id(1) == g, pltpu.roll(gate, LANE - g * hq * N_BRANCH, 1), gate)
    ng = ng_ref[0]
    for h in range(hq):
        r = slice(h * tq, (h + 1) * tq)
        o_s = acc_scr[r, :] * (1.0 / jnp.sum(l_scr[r, :], axis=-1, keepdims=True))
        o = (gate[:, 3 * h:3 * h + 1] * oc_scr[r, :] + gate[:, 3 * h + 1:3 * h + 2] * o_s
             + gate[:, 3 * h + 2:3 * h + 3] * (ow_scr[r, :] * wl_scr[r, :]))
        o_ref[0, :, h * LANE:(h + 1) * LANE] = (ng[:, h * LANE:(h + 1) * LANE] * o).astype(o_ref.dtype)


def _nsa_prompt(qc3, qr3, ng3, bg3, kvb, kc, vc, *, tq=128, tc=512, sub=64, pool=None):
    b, t, _ = qc3.shape
    nq = t // tq
    tc = min(tc, t)
    tw = min(WINDOW + tq, t)
    n_sel = t // SEL_BLOCK
    top = min(TOP_N, n_sel)
    npad = kc.shape[2]
    gw = NSA_HPG * LANE
    rows = NSA_HPG * tq
    wide = max(tc, npad)
    kvspec = lambda k: pl.BlockSpec((1, t, LANE), lambda bi, gi, qi, *_, k=k: (bi, 0, k + gi))
    cspec = pl.BlockSpec((1, 1, npad, LANE), lambda bi, gi, qi, *_: (bi, gi, 0, 0))
    gspec = pl.BlockSpec((1, tq, gw), lambda bi, gi, qi, *_: (bi, qi, gi))
    in_specs = [gspec, gspec, gspec,
                pl.BlockSpec((1, tq, LANE), lambda bi, gi, qi, *_: (bi, qi, 0)),
                kvspec(0), kvspec(2), kvspec(4), kvspec(6), cspec, cspec]
    out_shape = [jax.ShapeDtypeStruct((b, t, NSA_HEADS * LANE), BF16)]
    out_specs = [pl.BlockSpec((1, tq, gw), lambda bi, gi, qi, *_: (bi, qi, gi))]
    assert tc % tq == 0 and npad == LANE
    scratch = ([pltpu.VMEM((rows, LANE), BF16), pltpu.VMEM((rows, 2 * LANE), BF16)]
               + [pltpu.VMEM((rows, wide), F32), pltpu.VMEM((rows, wide), BF16)]
               + [pltpu.VMEM((rows, LANE), F32)] * 5 + [pltpu.VMEM((tq, wide), F32)]
               + [pltpu.VMEM((rows, tw), F32), pltpu.VMEM((rows, tw), BF16), pltpu.VMEM((tq, tw), F32)]
               + [pltpu.VMEM((rows, LANE), F32)] * 2)
    args = [qc3, qr3, ng3, bg3, kvb, kvb, kvb, kvb, kc, vc]
    n_steps = b * NSA_KV * nq
    pool_pages, page_base, prefetch = 0, 0, []
    if pool is not None:
        cache4, page_flat, a3, page_base = pool
        page, n_cols = cache4.shape[1], 2 * NSA_KV
        if page_flat.shape[0] % n_steps == 0 and page == 2 * CMP_BLOCK and 2 * n_cols == 8:
            pool_pages = page_flat.shape[0] // n_steps
            prefetch = [page_flat]
            in_specs += [pl.BlockSpec((CMP_BLOCK, 2 * n_cols, LANE), lambda bi, gi, qi, *_: (0, 0, 0)),
                         pl.BlockSpec(memory_space=pl.ANY)]
            args += [jnp.concatenate([a3, a3], axis=1), cache4]
            out_shape.append(jax.ShapeDtypeStruct((n_steps, 2 * pool_pages, n_cols * LANE), F32))
            out_specs.append(pl.BlockSpec((1, 2 * pool_pages, n_cols * LANE),
                                          lambda bi, gi, qi, *_: ((bi * NSA_KV + gi) * nq + qi, 0, 0)))
            scratch += [pltpu.VMEM((2, pool_pages, CMP_BLOCK, 2 * n_cols, LANE), F32),
                        pltpu.VMEM((pool_pages * 2 * n_cols, LANE), F32), pltpu.SemaphoreType.DMA((2,))]
    kern = functools.partial(_nsa_prompt_kernel, t_len=t, tq=tq, tc=tc, tw=tw, n_sel=n_sel, top=top,
                             sub=min(sub, tq), pool_pages=pool_pages, page_base=page_base)
    res = pl.pallas_call(
        kern,
        out_shape=out_shape,
        grid_spec=pltpu.PrefetchScalarGridSpec(
            num_scalar_prefetch=len(prefetch), grid=(b, NSA_KV, nq),
            in_specs=in_specs, out_specs=out_specs, scratch_shapes=scratch),
        compiler_params=_params(("arbitrary", "arbitrary", "arbitrary")),
        name="nsa_prompt",
    )(*prefetch, *args)
    return (res[0], res[1]) if pool_pages else (res[0], None)


def _conv_kernel(h_ref, b_ref, c_ref, g_ref, prev_ref, w_ref, y_ref, st_ref, up_scr, *, t_real):
    u = c_ref[...] * h_ref[...]
    t = u.shape[1]
    up_scr[:, pl.ds(8 - (CONV_W - 1), CONV_W - 1), :] = prev_ref[...]
    up_scr[:, pl.ds(8, t), :] = u
    w = w_ref[...]
    y = w[0:1, :][None] * up_scr[:, pl.ds(6, t), :]
    y = y + w[1:2, :][None] * up_scr[:, pl.ds(7, t), :]
    y = y + w[2:3, :][None] * u
    y = b_ref[...] * y
    y_ref[...] = (_silu(g_ref[...]) * y).astype(y_ref.dtype)
    st_ref[...] = up_scr[:, pl.ds(6 + t_real, CONV_W - 1), :]


def _conv_mixer(z3, prev, w_conv, *, t_real):
    b, t, _ = z3.shape
    nc = CONV_DIM // LANE
    zspec = lambda off: pl.BlockSpec((b, t, LANE), lambda ci, off=off: (0, 0, off // LANE + ci))
    return pl.pallas_call(
        functools.partial(_conv_kernel, t_real=t_real),
        out_shape=[jax.ShapeDtypeStruct((b, t, CONV_DIM), BF16),
                   jax.ShapeDtypeStruct((b, CONV_W - 1, CONV_DIM), F32)],
        grid=(nc,),
        in_specs=[zspec(C_H), zspec(C_B), zspec(C_C), zspec(C_G),
                  pl.BlockSpec((b, CONV_W - 1, LANE), lambda ci: (0, 0, ci)),
                  pl.BlockSpec((CONV_W, LANE), lambda ci: (0, ci))],
        out_specs=[pl.BlockSpec((b, t, LANE), lambda ci: (0, 0, ci)),
                   pl.BlockSpec((b, CONV_W - 1, LANE), lambda ci: (0, 0, ci))],
        scratch_shapes=[pltpu.VMEM((b, t + 8, LANE), F32)],
        compiler_params=_params(("parallel",)),
        name="conv_mixer",
    )(z3, z3, z3, z3, prev, w_conv)


def _mem_heads(q, mg, kv_ref, store):
    nm = kv_ref.shape[1] // (2 * MEM_HEADS)
    for h in range(MEM_HEADS):
        lo, hi = h * LANE, (h + 1) * LANE
        k = kv_ref[0, pl.ds(h, nm, stride=2 * MEM_HEADS), :].astype(BF16)
        v = kv_ref[0, pl.ds(MEM_HEADS + h, nm, stride=2 * MEM_HEADS), :].astype(BF16)
        s = _nt((q[:, lo:hi] * ATTN_SCALE).astype(BF16), k)
        e = jnp.exp(s - jnp.max(s, axis=-1, keepdims=True))
        o = _nn(e.astype(BF16), v) / jnp.sum(e, axis=-1, keepdims=True)
        store(lo, hi, _silu(mg[:, lo:hi]) * o)


def _mem_attn_kernel(q_ref, mg_ref, kv_ref, o_ref):
    def store(lo, hi, y):
        o_ref[0, :, lo:hi] = y.astype(o_ref.dtype)

    _mem_heads(q_ref[0], mg_ref[0], kv_ref, store)


def _mem_attn(zb3, mkv, *, mq_off, mg_off, tq, kv_base=0):
    b, t, _ = zb3.shape
    wq = MEM_HEADS * LANE
    return pl.pallas_call(
        _mem_attn_kernel,
        out_shape=jax.ShapeDtypeStruct((b, t, wq), BF16),
        grid=(b, t // tq),
        in_specs=[pl.BlockSpec((1, tq, wq), lambda bi, ti: (bi, ti, mq_off // wq)),
                  pl.BlockSpec((1, tq, wq), lambda bi, ti: (bi, ti, mg_off // wq)),
                  pl.BlockSpec((1,) + mkv.shape[1:], lambda bi, ti: (bi + kv_base, 0, 0))],
        out_specs=pl.BlockSpec((1, tq, wq), lambda bi, ti: (bi, ti, 0)),
        compiler_params=_params(("parallel", "parallel")),
        name="mem_attn",
    )(zb3, zb3, mkv)


def _out_proj_kernel(x_ref, ya_ref, yb_ref, ym_ref, xs_ref, yas_ref, ybs_ref, yms_ref, w_ref, fg_ref,
                     o_ref, os_ref, w_scr, *, final):
    @pl.when(pl.program_id(0) == 0)
    def _():
        w_scr[...] = w_ref[...].astype(BF16)

    a, bw = CONV_DIM, CONV_DIM + NSA_HEADS * LANE

    def mix(x, ya, yb, ym):
        acc = _nn(ya, w_scr[0:a, :])
        acc = acc + _nn(yb, w_scr[a:bw, :])
        acc = acc + _nn(ym, w_scr[bw:, :])
        r = x + acc
        if final:
            r = r * lax.rsqrt(jnp.mean(r * r, axis=-1, keepdims=True) + NORM_EPS) * fg_ref[...]
        return r

    o_ref[...] = mix(x_ref[...], ya_ref[...], yb_ref[...], ym_ref[...])

    @pl.when(pl.program_id(0) == pl.num_programs(0) - 1)
    def _():
        os_ref[...] = mix(xs_ref[...], yas_ref[...], ybs_ref[...], yms_ref[...])


def _out_proj(x, ya, yb, ym, sample, w, fg, *, tm, final):
    m, d = x.shape
    row = lambda width: pl.BlockSpec((tm, width), lambda i: (i, 0))
    whole = lambda a: pl.BlockSpec(a.shape, lambda i: (0, 0), pipeline_mode=pl.Buffered(1))
    return pl.pallas_call(
        functools.partial(_out_proj_kernel, final=final),
        out_shape=[jax.ShapeDtypeStruct((m, d), F32), jax.ShapeDtypeStruct(sample[0].shape, F32)],
        grid=(m // tm,),
        in_specs=[row(d), row(ya.shape[1]), row(yb.shape[1]), row(ym.shape[1])] + [whole(a) for a in sample]
                 + [whole(w), pl.BlockSpec((1, d), lambda i: (0, 0), pipeline_mode=pl.Buffered(1))],
        out_specs=[row(d), pl.BlockSpec(sample[0].shape, lambda i: (0, 0))],
        scratch_shapes=[pltpu.VMEM(w.shape, BF16)],
        compiler_params=_params(("arbitrary",)),
        name="out_proj",
    )(x, ya, yb, ym, *sample, w, fg.reshape(1, d))


def _sample_pre_kernel(q_ref, ng_ref, bg_ref, kv0_ref, kv1_ref, kv2_ref, c_ref, s1_ref, s2_ref, kc_ref, vc_ref,
                       kvn_ref, wn_ref, qr_ref, oc_ref, ngo_ref, gate_ref, val_ref, *, past):
    tp = SAMPLE_T_PAD
    hq = NSA_HPG
    c, s1, s2 = c_ref[...], s1_ref[...], s2_ref[...]
    kv0, kv1, kv2 = kv0_ref[0], kv1_ref[0], kv2_ref[0]
    kvn_ref[0, :, 0:512] = kv0
    kvn_ref[0, :, 768:1024] = kv1[:, 256:512]
    wn_ref[0, :, 256:512] = kv2[:, 256:512]
    for g in range(NSA_KV):
        lo, hi = g * LANE, (g + 1) * LANE
        kvn_ref[0, :, 512 + lo:512 + hi] = _rope(kv1[:, lo:hi], c, s1, s2)
        wn_ref[0, :, lo:hi] = _rope(kv2[:, lo:hi], c, s1, s2)

    q = q_ref[0]
    ng = ng_ref[0]
    gates = jax.nn.sigmoid(bg_ref[0])
    for g in range(NSA_KV):
        qc_l, qr_l = [], []
        for h in range(hq):
            lo = (g * hq + h) * LANE
            qh = q[:, lo:lo + LANE]
            qc_l.append(qh * ATTN_SCALE)
            qr_l.append(_rope(qh, c, s1, s2) * ATTN_SCALE)
            ngo_ref[0, g, h * tp:(h + 1) * tp, :] = ng[:, lo:lo + LANE]
            for br in range(N_BRANCH):
                col = (g * hq + h) * N_BRANCH + br
                gate_ref[0, g, br, h * tp:(h + 1) * tp, :] = jnp.broadcast_to(gates[:, col:col + 1], (tp, LANE))
        qc = jnp.concatenate(qc_l, axis=0)
        qr_ref[0, g] = jnp.concatenate(qr_l, axis=0)

        kc = kc_ref[0, g]
        npad = kc.shape[0]
        s = _nt(qc.astype(BF16), kc)
        trow = lax.broadcasted_iota(jnp.int32, (hq * tp, npad), 0) % tp
        ncol = lax.broadcasted_iota(jnp.int32, (hq * tp, npad), 1)
        cmask = (ncol + 1) * CMP_BLOCK <= past + trow + 1
        s = jnp.where(cmask, s, MASK_NEG)
        e = jnp.where(cmask, jnp.exp(s - jnp.max(s, axis=-1, keepdims=True)), 0.0)
        p = e / jnp.maximum(jnp.sum(e, axis=-1, keepdims=True), 1e-30)
        oc_ref[0, g] = _nn(p.astype(BF16), vc_ref[0, g])
        imp = jnp.sum(p.reshape(hq, tp, npad), axis=0)

        blk = lax.broadcasted_iota(jnp.int32, (tp, npad), 1)
        cur = (past + lax.broadcasted_iota(jnp.int32, (tp, npad), 0)) // SEL_BLOCK
        val = jnp.where((blk == 0) | (blk == cur) | (blk == cur - 1), FORCE, imp)
        val = jnp.where(blk > cur, -1.0, val)
        val = jnp.where(blk >= past // SEL_BLOCK, -2.0, val)
        val_ref[0, g] = val


def _topk_kernel(val_ref, idx_ref, *, n_top):
    val = val_ref[...]
    rows, n = val.shape
    blk = lax.broadcasted_iota(jnp.int32, (rows, n), 1)
    lane = lax.broadcasted_iota(jnp.int32, (rows, LANE), 1)
    idx = jnp.zeros((rows, LANE), jnp.int32)
    for r in range(n_top):
        best = jnp.max(val, axis=-1, keepdims=True)
        j = jnp.min(jnp.where(val == best, blk, n), axis=-1, keepdims=True)
        idx = jnp.where(lane == r, j, idx)
        val = jnp.where(blk == j, -3e38, val)
    idx_ref[...] = idx


def _topk(val2, *, n_top):
    rows, n = val2.shape
    return pl.pallas_call(
        functools.partial(_topk_kernel, n_top=n_top),
        out_shape=jax.ShapeDtypeStruct((rows, LANE), jnp.int32),
        grid=(1,),
        in_specs=[pl.BlockSpec((rows, n), lambda i: (0, 0))],
        out_specs=pl.BlockSpec((rows, LANE), lambda i: (0, 0)),
        compiler_params=_params(("arbitrary",)),
        name="sample_topk",
    )(val2)


def _sample_pre(za3, zb3, bg3, tabs, kc, vc, *, past):
    bs, tp, _ = za3.shape
    npad = kc.shape[2]
    qw = NSA_HEADS * LANE
    kvblk = KV_OFF // 512
    zspec = lambda k: pl.BlockSpec((1, tp, 512), lambda bi, k=k: (bi, 0, kvblk + k))
    tspec = pl.BlockSpec((tp, LANE), lambda bi: (0, 0))
    cspec = pl.BlockSpec((1, NSA_KV, npad, LANE), lambda bi: (bi, 0, 0, 0))
    rows = NSA_HPG * tp
    gspec = pl.BlockSpec((1, NSA_KV, rows, LANE), lambda bi: (bi, 0, 0, 0))
    gshape = jax.ShapeDtypeStruct((bs, NSA_KV, rows, LANE), F32)
    return pl.pallas_call(
        functools.partial(_sample_pre_kernel, past=past),
        out_shape=[jax.ShapeDtypeStruct((bs, tp, 1024), F32),
                   jax.ShapeDtypeStruct((bs, tp, 512), F32),
                   gshape, gshape, gshape,
                   jax.ShapeDtypeStruct((bs, NSA_KV, N_BRANCH, rows, LANE), F32),
                   jax.ShapeDtypeStruct((bs, NSA_KV, tp, npad), F32)],
        grid=(bs,),
        in_specs=[pl.BlockSpec((1, tp, qw), lambda bi: (bi, 0, Q_OFF // qw)),
                  pl.BlockSpec((1, tp, qw), lambda bi: (bi, 0, NG_OFF // qw)),
                  pl.BlockSpec((1, tp, LANE), lambda bi: (bi, 0, 0)),
                  zspec(0), zspec(1), zspec(2), tspec, tspec, tspec, cspec, cspec],
        out_specs=[pl.BlockSpec((1, tp, 1024), lambda bi: (bi, 0, 0)),
                   pl.BlockSpec((1, tp, 512), lambda bi: (bi, 0, 0)),
                   gspec, gspec, gspec,
                   pl.BlockSpec((1, NSA_KV, N_BRANCH, rows, LANE), lambda bi: (bi, 0, 0, 0, 0)),
                   pl.BlockSpec((1, NSA_KV, tp, npad), lambda bi: (bi, 0, 0, 0))],
        compiler_params=_params(("parallel",)),
        name="sample_pre",
    )(za3, za3, bg3, zb3, zb3, zb3, *tabs, kc, vc)


def _div_pow2(x, n):
    assert n & (n - 1) == 0
    return x // n if isinstance(x, int) else lax.shift_right_logical(x, n.bit_length() - 1)


def _mod_pow2(x, n):
    assert n & (n - 1) == 0
    return x % n if isinstance(x, int) else x & (n - 1)


def _sample_attn_kernel(idx_sm, pt_sm, qr_ref, oc_ref, ng_ref, gate_ref, ksn_ref, vsn_ref,
                        wc_ref, kwn_ref, vwn_ref, wn_ref, cache_ref, o_ref, win_ref,
                        kbuf, vbuf, kw_scr, vw_scr, sem, *, ts, n_top, n_pages, page_base, per_page, wb):
    tp = SAMPLE_T_PAD
    hq = NSA_HPG
    b = pl.program_id(0)
    g = pl.program_id(1)
    n_gath = n_top * SEL_BLOCK
    ks_rows = kbuf.shape[2]
    step = b * NSA_KV + g
    n_steps = pl.num_programs(0) * NSA_KV
    slot = step % 2

    def gather_copies(step_idx, t, r, to_slot):
        bb, gg = _div_pow2(step_idx, NSA_KV), _mod_pow2(step_idx, NSA_KV)
        blk = idx_sm[(step_idx * ts + t) * n_top + r]
        page = pt_sm[bb * n_pages + _div_pow2(blk, per_page)] + page_base
        row0 = _mod_pow2(blk, per_page) * SEL_BLOCK
        src_k = cache_ref.at[page, pl.ds(row0, SEL_BLOCK), 2 * NSA_KV + gg]
        src_v = cache_ref.at[page, pl.ds(row0, SEL_BLOCK), 3 * NSA_KV + gg]
        dst = pl.ds(r * SEL_BLOCK, SEL_BLOCK)
        return (pltpu.make_async_copy(src_k, kbuf.at[to_slot, t, dst], sem.at[to_slot, 0]),
                pltpu.make_async_copy(src_v, vbuf.at[to_slot, t, dst], sem.at[to_slot, 1]))

    def start_gathers(step_idx, to_slot):
        for t in range(ts):
            for r in range(n_top):
                ck, cv = gather_copies(step_idx, t, r, to_slot)
                ck.start(priority=0)
                cv.start(priority=1)

    @pl.when(step == 0)
    def _():
        start_gathers(0, 0)

    @pl.when(step + 1 < n_steps)
    def _():
        start_gathers(step + 1, 1 - slot)

    @pl.when(g == 0)
    def _():
        n_w = 2 * NSA_KV
        kept = (wb - ts) * n_w
        win_ref[0, pl.ds(0, kept), :] = wc_ref[0, pl.ds(ts * n_w, kept), :]
        for part in range(n_w):
            win_ref[0, pl.ds(kept + part, ts, stride=n_w), :] = wn_ref[0, 0:ts, part * LANE:(part + 1) * LANE]

    qr = qr_ref[0, 0].astype(BF16)
    trow = lax.broadcasted_iota(jnp.int32, (hq * tp, 1), 0) % tp

    ww = kw_scr.shape[0]
    kw_scr[pl.ds(0, wb), :] = wc_ref[0, pl.ds(g, wb, stride=2 * NSA_KV), :]
    vw_scr[pl.ds(0, wb), :] = wc_ref[0, pl.ds(NSA_KV + g, wb, stride=2 * NSA_KV), :]
    kw_scr[pl.ds(wb, tp), :] = kwn_ref[0]
    vw_scr[pl.ds(wb, tp), :] = vwn_ref[0]
    kw_scr[pl.ds(wb + tp, ww - wb - tp), :] = jnp.zeros((ww - wb - tp, LANE), F32)
    vw_scr[pl.ds(wb + tp, ww - wb - tp), :] = jnp.zeros((ww - wb - tp, LANE), F32)
    sw = _nt(qr, kw_scr[...].astype(BF16))
    jw = lax.broadcasted_iota(jnp.int32, (hq * tp, ww), 1)
    rel = jw - wb
    okw = (rel <= trow) & (rel > trow - WINDOW) & (jw < wb + ts)
    sw = jnp.where(okw, sw, MASK_NEG)
    ew = jnp.where(okw, jnp.exp(sw - jnp.max(sw, axis=-1, keepdims=True)), 0.0)
    o_w = _nn(ew.astype(BF16), vw_scr[...].astype(BF16)) / jnp.sum(ew, axis=-1, keepdims=True)

    for t in range(ts):
        for r in range(n_top):
            ck, cv = gather_copies(step, t, r, slot)
            ck.wait()
            cv.wait()

    js = lax.broadcasted_iota(jnp.int32, (hq * tp, ks_rows), 1)
    o_s = jnp.zeros((hq * tp, LANE), F32)
    for t in range(ts):
        kbuf[slot, t, pl.ds(n_gath, tp), :] = ksn_ref[0]
        vbuf[slot, t, pl.ds(n_gath, tp), :] = vsn_ref[0]
        kbuf[slot, t, pl.ds(n_gath + tp, ks_rows - n_gath - tp), :] = jnp.zeros((ks_rows - n_gath - tp, LANE), F32)
        vbuf[slot, t, pl.ds(n_gath + tp, ks_rows - n_gath - tp), :] = jnp.zeros((ks_rows - n_gath - tp, LANE), F32)
        ss = _nt(qr, kbuf[slot, t].astype(BF16))
        oks = (js < n_gath) | ((js - n_gath <= t) & (js < n_gath + ts))
        ss = jnp.where(oks, ss, MASK_NEG)
        es = jnp.where(oks, jnp.exp(ss - jnp.max(ss, axis=-1, keepdims=True)), 0.0)
        ot = _nn(es.astype(BF16), vbuf[slot, t].astype(BF16)) / jnp.sum(es, axis=-1, keepdims=True)
        o_s = jnp.where(trow == t, ot, o_s)

    o = gate_ref[0, 0, 0] * oc_ref[0, 0] + gate_ref[0, 0, 1] * o_s + gate_ref[0, 0, 2] * o_w
    y = _silu(ng_ref[0, 0]) * o
    for h in range(hq):
        o_ref[0, :, h * LANE:(h + 1) * LANE] = y[h * tp:(h + 1) * tp].astype(o_ref.dtype)


def _sample_attn(idx_flat, page_flat, qr, oc, ngo, gates, kvn, cache_win_rows, wn, cache4, *,
                 ts, n_top, n_pages, page_base, win_base):
    bs = qr.shape[0]
    tp = SAMPLE_T_PAD
    rows = NSA_HPG * tp
    wb = cache_win_rows.shape[1] // (2 * NSA_KV)
    assert ts <= wb and (ts * 2 * NSA_KV) % 8 == 0
    per_page = cache4.shape[1] // SEL_BLOCK
    ks_rows = -(-(n_top * SEL_BLOCK + tp) // LANE) * LANE
    ww = -(-(wb + tp) // LANE) * LANE
    gspec = pl.BlockSpec((1, 1, rows, LANE), lambda bi, gi, *_: (bi, gi, 0, 0))
    newspec = lambda k: pl.BlockSpec((1, tp, LANE), lambda bi, gi, *_, k=k: (bi, 0, k + gi))
    grid_spec = pltpu.PrefetchScalarGridSpec(
        num_scalar_prefetch=2, grid=(bs, NSA_KV),
        in_specs=[gspec, gspec, gspec,
                  pl.BlockSpec((1, 1, N_BRANCH, rows, LANE), lambda bi, gi, *_: (bi, gi, 0, 0, 0)),
                  newspec(2 * NSA_KV), newspec(3 * NSA_KV),
                  pl.BlockSpec((1,) + cache_win_rows.shape[1:], lambda bi, gi, *_: (bi + win_base, 0, 0)),
                  newspec(0), newspec(NSA_KV),
                  pl.BlockSpec((1, tp, 2 * NSA_KV * LANE), lambda bi, gi, *_: (bi, 0, 0)),
                  pl.BlockSpec(memory_space=pl.ANY)],
        out_specs=[pl.BlockSpec((1, tp, NSA_HPG * LANE), lambda bi, gi, *_: (bi, 0, gi)),
                   pl.BlockSpec((1,) + cache_win_rows.shape[1:], lambda bi, gi, *_: (bi, 0, 0))],
        scratch_shapes=[pltpu.VMEM((2, ts, ks_rows, LANE), F32), pltpu.VMEM((2, ts, ks_rows, LANE), F32),
                        pltpu.VMEM((ww, LANE), F32), pltpu.VMEM((ww, LANE), F32),
                        pltpu.SemaphoreType.DMA((2, 2))])
    kern = functools.partial(_sample_attn_kernel, ts=ts, n_top=n_top, n_pages=n_pages,
                             page_base=page_base, per_page=per_page, wb=wb)
    return pl.pallas_call(
        kern,
        out_shape=[jax.ShapeDtypeStruct((bs, tp, NSA_HEADS * LANE), BF16),
                   jax.ShapeDtypeStruct((bs,) + cache_win_rows.shape[1:], F32)],
        grid_spec=grid_spec,
        compiler_params=_params(("arbitrary", "arbitrary")),
        name="sample_attn",
    )(idx_flat, page_flat, qr, oc, ngo, gates, kvn, kvn, cache_win_rows, wn, wn, wn, cache4)


def _rope_tables(first, n):
    f32 = np.float32
    freqs = np.power(f32(ROPE_THETA), -np.arange(ROPE_HALF, dtype=f32) * f32(2.0 / ROPE_DIM)).astype(f32)
    ang = (np.arange(first, first + n).astype(f32)[:, None] * freqs[None, :]).astype(f32)
    cos, sin = np.cos(ang).astype(f32), np.sin(ang).astype(f32)
    z16 = np.zeros((n, ROPE_HALF), f32)
    rest = LANE - ROPE_DIM
    c = np.concatenate([cos, cos, np.ones((n, rest), f32)], axis=1)
    s1 = np.concatenate([z16, sin, np.zeros((n, rest), f32)], axis=1)
    s2 = np.concatenate([-sin, z16, np.zeros((n, rest), f32)], axis=1)
    return jnp.asarray(c), jnp.asarray(s1), jnp.asarray(s2)


def _layer_weights(norm_g, w_in3, layer, w_conv, a_cmp, w_cmp, w_out):
    w_t3 = jnp.swapaxes(w_in3, 1, 2)
    assert w_t3.shape[1] == BG_SRC + BG_N + ZB_WIDTH and BG_SRC == ZA_WIDTH and BG_N % 8 == 0
    a4 = jnp.concatenate([a_cmp[0], a_cmp[0], a_cmp[1], a_cmp[1]], axis=1)
    a3 = jnp.stack([a_cmp[0], a_cmp[0], a_cmp[1], a_cmp[1]], axis=1)
    w4 = jnp.stack([w_cmp[0], w_cmp[0], w_cmp[1], w_cmp[1]]).astype(BF16)
    return dict(norm_g=norm_g, w_t3=w_t3, layer=layer, w_conv=w_conv, a4=a4, a3=a3, w4=w4,
                w_out=w_out)


def _prompt_layer(xp, xs2, mem_prompt, mem_norm_g, w_mem, lw, pool):
    b, t, d = xp.shape
    m = b * t
    x2 = xp.reshape(m, d)
    tabs = _rope_tables(0, t)
    wb = min(WINDOW, t)
    qc, qr, ng, bg, ya, conv_new, za_s, bg_s = _proj_conv(x2, lw["norm_g"], lw["w_t3"], lw["layer"], lw["w_conv"],
                                                          tabs, xs2, tm=min(512, t), seq_len=t)
    nm = mem_prompt.shape[1]
    mkv = _norm_matmul(mem_prompt.reshape(b * nm, d), mem_norm_g, w_mem, tm=min(512, b * nm), tn=512)
    mkv3 = mkv.reshape(b, nm * 2 * MEM_HEADS, LANE)
    kvn, kvb, win_rows, pooled_prompt, ym, zb_s = _proj_kv(x2, lw["norm_g"], lw["w_t3"], lw["layer"], tabs, lw["a4"],
                                                           xs2, mkv3, tm=wb, seq_len=t)
    kc, vc = _cmp_proj(pooled_prompt.reshape(b, t // CMP_BLOCK, -1), lw["w4"])
    r3 = lambda a: a.reshape(b, t, -1)
    yb, pooled = _nsa_prompt(r3(qc), r3(qr), r3(ng), r3(bg), r3(kvb), kc, vc, tq=min(256, t), pool=pool)
    mix_in = (x2, ya, yb.reshape(m, -1), ym)
    kv_new = kvn.reshape(b, t, 4, NSA_KV, HEAD_DIM)
    win_new = win_rows.reshape(b, wb, 2, NSA_KV, HEAD_DIM)
    mem_kv = mkv.reshape(b, nm, 2, MEM_HEADS, HEAD_DIM)
    return mix_in, kv_new, win_new, conv_new, mem_kv, pooled, (za_s, zb_s, bg_s)


def _sample_layer(xs_p, proj, ts, layer, cache4, page_flat, pooled, cache_win, state_conv, cache_mem, lw):
    bs, tp, d = xs_p.shape
    depth = cache_win.shape[0]
    pool, page = cache4.shape[0] // depth, cache4.shape[1]
    n_pages = page_flat.shape[0] // bs
    past = n_pages * page
    assert past % SEL_BLOCK == 0 and ts <= SEL_BLOCK and ts <= tp
    n_past = past // SEL_BLOCK
    n_top = min(TOP_N, n_past + 1) - 1
    m = bs * tp
    za, zb, bg = proj
    za3 = za.reshape(bs, tp, ZA_WIDTH)
    zb3 = zb.reshape(bs, tp, ZB_WIDTH)
    bg3 = bg.reshape(bs, tp, LANE)
    tabs = _rope_tables(past, tp)
    if pooled is None:
        pooled = _pool_pages(cache4, page_flat, lw["a3"], bs=bs, n_pages=n_pages, page_base=layer * pool)
    else:
        pooled = pooled.reshape(bs, n_pages * (page // CMP_BLOCK), 2 * NSA_KV * LANE)
    kc, vc = _cmp_proj(pooled, lw["w4"])
    kvn, wn, qr, oc, ngo, gates, val = _sample_pre(za3, zb3, bg3, tabs, kc, vc, past=past)
    idx = _topk(val.reshape(bs * NSA_KV * tp, val.shape[-1]), n_top=n_top)
    idx_flat = idx.reshape(bs, NSA_KV, tp, LANE)[:, :, :ts, :n_top].reshape(-1)
    wbuf = cache_win.shape[2]
    cache_win_rows = cache_win.reshape(depth * bs, wbuf * 2 * NSA_KV, HEAD_DIM)
    yb, win_rows = _sample_attn(idx_flat, page_flat, qr, oc, ngo, gates, kvn, cache_win_rows, wn, cache4,
                                ts=ts, n_top=n_top, n_pages=n_pages, page_base=layer * pool, win_base=layer * bs)
    ya, conv_new = _conv_mixer(za3, state_conv[layer], lw["w_conv"], t_real=ts)
    nm = cache_mem.shape[2]
    mem_rows = cache_mem.reshape(depth * bs, nm * 2 * MEM_HEADS, HEAD_DIM)
    ym = _mem_attn(zb3, mem_rows, mq_off=MQ_OFF, mg_off=MG_OFF, tq=tp, kv_base=layer * bs)
    mix_in = (xs_p.reshape(m, d), ya.reshape(m, -1), yb.reshape(m, -1), ym.reshape(m, -1))
    kv_new = kvn[:, :ts].reshape(bs, ts, 4, NSA_KV, HEAD_DIM)
    win_state = win_rows.reshape(bs, wbuf, 2, NSA_KV, HEAD_DIM)
    return mix_in, kv_new, win_state, conv_new


def kernel(x_prompt, x_sample, cache_kv, cache_win, state_conv, cache_mem, page_table, mem_prompt,
           norm_g, w_in, w_conv, a_cmp, w_cmp, mem_norm_g, w_mem_kv, w_out, final_g):
    depth = w_in.shape[0]
    ts = x_sample.shape[1]
    xp = x_prompt
    xs = jnp.pad(x_sample, ((0, 0), (0, SAMPLE_T_PAD - ts), (0, 0)))
    pool_size, page = cache_kv.shape[1], cache_kv.shape[2]
    cache4 = cache_kv.reshape(depth * pool_size, page, 4 * NSA_KV, HEAD_DIM)
    page_flat = page_table.reshape(-1).astype(jnp.int32)
    kv_p, win_p, conv_p, mem_p, kv_s, win_s, conv_s = [], [], [], [], [], [], []
    for l in range(depth):
        lw = _layer_weights(norm_g[l], w_in, l, w_conv[l], a_cmp[l], w_cmp[l], w_out[l])
        final = l == depth - 1
        mix_p, kvn, winn, convn, mkv, pooled, proj_s = _prompt_layer(
            xp, xs.reshape(-1, xs.shape[-1]), mem_prompt, mem_norm_g[l], w_mem_kv[l], lw,
            pool=(cache4, page_flat, lw["a3"], l * pool_size))
        kv_p.append(kvn)
        win_p.append(winn)
        conv_p.append(convn)
        mem_p.append(mkv)
        mix_s, kvn, winn, convn = _sample_layer(xs, proj_s, ts, l, cache4, page_flat, pooled, cache_win, state_conv,
                                                cache_mem, lw)
        kv_s.append(kvn)
        win_s.append(winn)
        conv_s.append(convn)
        out_p, out_s = _out_proj(*mix_p, mix_s, lw["w_out"], final_g, tm=min(512, mix_p[0].shape[0]), final=final)
        xp = out_p.reshape(xp.shape)
        xs = out_s.reshape(xs.shape)
    return (xp, xs[:, :ts], jnp.stack(kv_p), jnp.stack(win_p), jnp.stack(conv_p), jnp.stack(mem_p),
            jnp.stack(kv_s), jnp.stack(win_s), jnp.stack(conv_s))
```

```python
import functools

import jax
import jax.numpy as jnp
import numpy as np
from jax import lax
from jax.experimental import pallas as pl
from jax.experimental.pallas import tpu as pltpu

F32 = jnp.float32
BF16 = jnp.bfloat16

HEAD_DIM = 128
CONV_DIM = 512
CONV_W = 3
NSA_HEADS = 8
NSA_KV = 2
NSA_HPG = NSA_HEADS // NSA_KV
MEM_HEADS = 4
N_BRANCH = 3
ROPE_DIM = HEAD_DIM // 4
ROPE_HALF = ROPE_DIM // 2
ROPE_THETA = 500000.0
CMP_BLOCK = 64
SEL_BLOCK = 64
SEL_SHIFT = 6
TOP_N = 16
WINDOW = 512
NORM_EPS = 1e-6
MASK_NEG = -1e30
FORCE = 1e9
ATTN_SCALE = HEAD_DIM ** -0.5
SCALE_LOG2 = ATTN_SCALE * 1.4426950408889634

C_H, C_B, C_C, C_G, Q_OFF, NG_OFF = 0, 512, 1024, 1536, 2048, 3072
ZA_WIDTH = 4096
KV_OFF, MQ_OFF, MG_OFF = 0, 1536, 2048
ZB_WIDTH = 2560
BG_SRC = 4096
BG_N = NSA_HEADS * N_BRANCH
LANE = 128
SAMPLE_T_PAD = 8
VMEM_LIMIT = 56 * 1024 * 1024
W_CHUNK = 512


def _nt(a, b):
    return lax.dot_general(a, b, (((1,), (1,)), ((), ())), preferred_element_type=F32)


def _nn(a, b):
    return jnp.dot(a, b, preferred_element_type=F32)


def _params(sem, vmem=VMEM_LIMIT):
    return pltpu.CompilerParams(dimension_semantics=sem, vmem_limit_bytes=vmem)


def _rope(x, c, s1, s2):
    return x * c + pltpu.roll(x, ROPE_HALF, 1) * s1 + pltpu.roll(x, LANE - ROPE_HALF, 1) * s2


def _silu(x):
    return x * jax.nn.sigmoid(x)


def _rms_rows(x_ref, g_ref):
    x = x_ref[...]
    return (x * lax.rsqrt(jnp.mean(x * x, axis=-1, keepdims=True) + NORM_EPS) * g_ref[...]).astype(BF16)


def _norm_matmul_kernel(x_ref, g_ref, w_ref, z_ref, h_scr, *, tn):
    tm = x_ref.shape[0]
    parts = w_ref.shape[1] // LANE
    h_scr[...] = _rms_rows(x_ref, g_ref)
    for j in range(w_ref.shape[1] // tn):
        z = _nn(h_scr[...], w_ref[:, j * tn:(j + 1) * tn].astype(BF16))
        for k in range(tn // LANE):
            z_ref[pl.ds(j * (tn // LANE) + k, tm, stride=parts), :] = z[:, k * LANE:(k + 1) * LANE]


def _norm_matmul(x, g, w, *, tm, tn):
    m, d = x.shape
    n = w.shape[1]
    assert n % tn == 0 and tn % LANE == 0 and m % tm == 0 and w.shape[0] == d
    resident = lambda shape: pl.BlockSpec(shape, lambda i: (0, 0), pipeline_mode=pl.Buffered(1))
    return pl.pallas_call(
        functools.partial(_norm_matmul_kernel, tn=tn),
        out_shape=jax.ShapeDtypeStruct((m * (n // LANE), LANE), F32),
        grid=(m // tm,),
        in_specs=[pl.BlockSpec((tm, d), lambda i: (i, 0)), resident((1, d)), resident((d, n))],
        out_specs=pl.BlockSpec((tm * (n // LANE), LANE), lambda i: (i, 0)),
        scratch_shapes=[pltpu.VMEM((tm, d), BF16)],
        compiler_params=_params(("parallel",)),
        name="norm_matmul",
    )(x, g.reshape(1, d), w)


def _fetch_weight_rows(w_hbm, layer, first_row, w_scr, stage, sem):
    ch = stage.shape[1]
    n_chunks = w_scr.shape[0] // ch
    assert n_chunks * ch == w_scr.shape[0]

    def chunk_copy(c):
        return pltpu.make_async_copy(w_hbm.at[layer, pl.ds(first_row + c * ch, ch), :], stage.at[c % 2], sem.at[c % 2])

    chunk_copy(0).start()
    for c in range(n_chunks):
        if c + 1 < n_chunks:
            chunk_copy(c + 1).start()
        chunk_copy(c).wait()
        w_scr[c * ch:(c + 1) * ch, :] = stage[c % 2].astype(BF16)


def _fetch_gate_rows(w_hbm, layer, wbg_scr, stage, sem):
    cp = pltpu.make_async_copy(w_hbm.at[layer, pl.ds(BG_SRC, BG_N), :], stage.at[0, pl.ds(0, BG_N)], sem.at[0])
    cp.start()
    cp.wait()
    wbg_scr[...] = jnp.zeros(wbg_scr.shape, BF16)
    wbg_scr[0:BG_N, :] = stage[0, 0:BG_N, :].astype(BF16)


def _proj_conv_kernel(x_ref, g_ref, w_hbm, wc_ref, c_ref, s1_ref, s2_ref, xs_ref,
                      qc_ref, qr_ref, ng_ref, bg_ref, ya_ref, st_ref, zs_ref, bgs_ref,
                      h_scr, up_scr, w_ref, wbg_ref, stage, sem, *, blocks_per_seq, layer):
    tm = x_ref.shape[0]
    cw = CONV_DIM
    i = pl.program_id(0)
    first = i % blocks_per_seq == 0
    @pl.when(i == 0)
    def _():
        _fetch_weight_rows(w_hbm, layer, 0, w_ref, stage, sem)
        _fetch_gate_rows(w_hbm, layer, wbg_ref, stage, sem)
        up_scr[...] = jnp.zeros(up_scr.shape, F32)

    h_scr[...] = _rms_rows(x_ref, g_ref)
    bg_ref[...] = _nt(h_scr[...], wbg_ref[...])
    chunk = lambda off: _nt(h_scr[...], w_ref[off:off + cw, :])


    carry = up_scr[pl.ds(8 + tm - (CONV_W - 1), CONV_W - 1), :]
    up_scr[pl.ds(8 - (CONV_W - 1), CONV_W - 1), :] = jnp.where(first, 0.0, carry)
    u = chunk(C_C) * chunk(C_H)
    up_scr[pl.ds(8, tm), :] = u
    wc = wc_ref[...]
    y = wc[0:1, :] * up_scr[pl.ds(6, tm), :]
    y = y + wc[1:2, :] * up_scr[pl.ds(7, tm), :]
    y = y + wc[2:3, :] * u
    y = chunk(C_B) * y
    ya_ref[...] = (_silu(chunk(C_G)) * y).astype(ya_ref.dtype)

    st_ref[0] = up_scr[pl.ds(8 + tm - (CONV_W - 1), CONV_W - 1), :]

    c, s1, s2 = c_ref[...], s1_ref[...], s2_ref[...]
    for j in range((NG_OFF - Q_OFF) // cw):
        qv = chunk(Q_OFF + j * cw)
        for k in range(cw // LANE):
            lo = j * cw + k * LANE
            qh = qv[:, k * LANE:(k + 1) * LANE]
            qc_ref[:, lo:lo + LANE] = (qh * SCALE_LOG2).astype(BF16)
            qr_ref[:, lo:lo + LANE] = (_rope(qh, c, s1, s2) * SCALE_LOG2).astype(BF16)
    for j in range((ZA_WIDTH - NG_OFF) // cw):
        ng_ref[:, j * cw:(j + 1) * cw] = _silu(chunk(NG_OFF + j * cw))

    @pl.when(i == pl.num_programs(0) - 1)
    def _():
        hs = _rms_rows(xs_ref, g_ref)
        bgs_ref[...] = _nt(hs, wbg_ref[...])
        for j in range(ZA_WIDTH // cw):
            zs_ref[:, j * cw:(j + 1) * cw] = _nt(hs, w_ref[j * cw:(j + 1) * cw, :])


def _proj_conv(x2, g, w_t3, layer, w_conv, tabs, xs2, *, tm, seq_len):
    m, d = x2.shape
    ms = xs2.shape[0]
    assert seq_len % tm == 0 and CONV_W == 3
    blocks_per_seq = seq_len // tm
    nq, nng = NG_OFF - Q_OFF, ZA_WIDTH - NG_OFF
    resident = lambda shape: pl.BlockSpec(shape, lambda i: (0,) * len(shape), pipeline_mode=pl.Buffered(1))
    row = lambda width: pl.BlockSpec((tm, width), lambda i: (i, 0))
    tspec = pl.BlockSpec((tm, LANE), lambda i: (i % blocks_per_seq, 0))
    return pl.pallas_call(
        functools.partial(_proj_conv_kernel, blocks_per_seq=blocks_per_seq, layer=layer),
        out_shape=[jax.ShapeDtypeStruct((m, nq), BF16), jax.ShapeDtypeStruct((m, nq), BF16),
                   jax.ShapeDtypeStruct((m, nng), F32), jax.ShapeDtypeStruct((m, LANE), F32),
                   jax.ShapeDtypeStruct((m, CONV_DIM), BF16),
                   jax.ShapeDtypeStruct((m // seq_len, CONV_W - 1, CONV_DIM), F32),
                   jax.ShapeDtypeStruct((ms, ZA_WIDTH), F32), jax.ShapeDtypeStruct((ms, LANE), F32)],
        grid=(m // tm,),
        in_specs=[row(d), resident((1, d)), pl.BlockSpec(memory_space=pl.ANY), resident(w_conv.shape),
                  tspec, tspec, tspec, resident((ms, d))],
        out_specs=[row(nq), row(nq), row(nng), row(LANE), row(CONV_DIM),
                   pl.BlockSpec((1, CONV_W - 1, CONV_DIM), lambda i: (i // blocks_per_seq, 0, 0)),
                   pl.BlockSpec((ms, ZA_WIDTH), lambda i: (0, 0)), pl.BlockSpec((ms, LANE), lambda i: (0, 0))],
        scratch_shapes=[pltpu.VMEM((tm, d), BF16), pltpu.VMEM((tm + 8, CONV_DIM), F32),
                        pltpu.VMEM((ZA_WIDTH, d), BF16), pltpu.VMEM((LANE, d), BF16),
                        pltpu.VMEM((2, W_CHUNK, d), F32), pltpu.SemaphoreType.DMA((2,))],
        compiler_params=_params(("arbitrary",)),
        name="proj_conv",
    )(x2, g.reshape(1, d), w_t3, w_conv, *tabs, xs2)


def _proj_kv_kernel(x_ref, g_ref, w_hbm, c_ref, s1_ref, s2_ref, a_ref, xs_ref, mkv_ref,
                    kvn_ref, kvb_ref, win_ref, pool_ref, ym_ref, zs_ref, h_scr, w_ref, stage, sem,
                    *, blocks_per_seq, layer):
    tm = x_ref.shape[0]
    cw = 2 * NSA_KV * LANE

    @pl.when(pl.program_id(0) == 0)
    def _():
        _fetch_weight_rows(w_hbm, layer, BG_SRC + BG_N, w_ref, stage, sem)

    h_scr[...] = _rms_rows(x_ref, g_ref)
    chunk = lambda off: _nt(h_scr[...], w_ref[off:off + cw, :])
    c, s1, s2 = c_ref[...], s1_ref[...], s2_ref[...]
    kv0 = chunk(KV_OFF)
    kv1 = chunk(KV_OFF + cw)
    kv2 = chunk(KV_OFF + 2 * cw)
    n_kv, n_w, half = 4 * NSA_KV, 2 * NSA_KV, NSA_KV * LANE
    for g in range(NSA_KV):
        lo, hi = g * LANE, (g + 1) * LANE
        ks = _rope(kv1[:, lo:hi], c, s1, s2)
        kw = _rope(kv2[:, lo:hi], c, s1, s2)
        vs = kv1[:, half + lo:half + hi]
        vw = kv2[:, half + lo:half + hi]
        kvn_ref[pl.ds(g, tm, stride=n_kv), :] = kv0[:, lo:hi]
        kvn_ref[pl.ds(NSA_KV + g, tm, stride=n_kv), :] = kv0[:, half + lo:half + hi]
        kvn_ref[pl.ds(2 * NSA_KV + g, tm, stride=n_kv), :] = ks
        kvn_ref[pl.ds(3 * NSA_KV + g, tm, stride=n_kv), :] = vs
        kvb_ref[:, lo:hi] = ks.astype(BF16)
        kvb_ref[:, half + lo:half + hi] = vs.astype(BF16)
        kvb_ref[:, 2 * half + lo:2 * half + hi] = kw.astype(BF16)
        kvb_ref[:, 3 * half + lo:3 * half + hi] = vw.astype(BF16)
        win_ref[pl.ds(g, tm, stride=n_w), :] = kw
        win_ref[pl.ds(NSA_KV + g, tm, stride=n_w), :] = vw

    pool_ref[...] = jnp.sum(kv0.reshape(tm // CMP_BLOCK, CMP_BLOCK, cw) * a_ref[...][None], axis=1)

    def store(lo, hi, y):
        ym_ref[:, lo:hi] = y.astype(ym_ref.dtype)

    _mem_heads(chunk(MQ_OFF), chunk(MG_OFF), mkv_ref, store)

    @pl.when(pl.program_id(0) == pl.num_programs(0) - 1)
    def _():
        hs = _rms_rows(xs_ref, g_ref)
        for j in range(ZB_WIDTH // cw):
            zs_ref[:, j * cw:(j + 1) * cw] = _nt(hs, w_ref[j * cw:(j + 1) * cw, :])


def _proj_kv(x2, g, w_t3, layer, tabs, a4, xs2, mkv3, *, tm, seq_len):
    m, d = x2.shape
    ms = xs2.shape[0]
    wm = MEM_HEADS * LANE
    assert seq_len % tm == 0 and MG_OFF == MQ_OFF + wm and ZB_WIDTH == MG_OFF + wm
    blocks_per_seq = seq_len // tm
    n_kv, n_w = 4 * NSA_KV, 2 * NSA_KV
    resident = lambda shape: pl.BlockSpec(shape, lambda i: (0,) * len(shape), pipeline_mode=pl.Buffered(1))
    row = lambda width: pl.BlockSpec((tm, width), lambda i: (i, 0))
    tspec = pl.BlockSpec((tm, LANE), lambda i: (i % blocks_per_seq, 0))
    return pl.pallas_call(
        functools.partial(_proj_kv_kernel, blocks_per_seq=blocks_per_seq, layer=layer),
        out_shape=[jax.ShapeDtypeStruct((m * n_kv, LANE), F32),
                   jax.ShapeDtypeStruct((m, n_kv * LANE), BF16),
                   jax.ShapeDtypeStruct((m // seq_len * tm * n_w, LANE), F32),
                   jax.ShapeDtypeStruct((m // CMP_BLOCK, 2 * NSA_KV * LANE), F32),
                   jax.ShapeDtypeStruct((m, wm), BF16), jax.ShapeDtypeStruct((ms, ZB_WIDTH), F32)],
        grid=(m // tm,),
        in_specs=[row(d), resident((1, d)), pl.BlockSpec(memory_space=pl.ANY), tspec, tspec, tspec,
                  resident(a4.shape), resident((ms, d)),
                  pl.BlockSpec((1,) + mkv3.shape[1:], lambda i: (i // blocks_per_seq, 0, 0))],
        out_specs=[pl.BlockSpec((tm * n_kv, LANE), lambda i: (i, 0)), row(n_kv * LANE),
                   pl.BlockSpec((tm * n_w, LANE), lambda i: (i // blocks_per_seq, 0)),
                   pl.BlockSpec((tm // CMP_BLOCK, 2 * NSA_KV * LANE), lambda i: (i, 0)), row(wm),
                   pl.BlockSpec((ms, ZB_WIDTH), lambda i: (0, 0))],
        scratch_shapes=[pltpu.VMEM((tm, d), BF16), pltpu.VMEM((ZB_WIDTH, d), BF16),
                        pltpu.VMEM((2, W_CHUNK, d), F32), pltpu.SemaphoreType.DMA((2,))],
        compiler_params=_params(("arbitrary",)),
        name="proj_kv",
    )(x2, g.reshape(1, d), w_t3, *tabs, a4, xs2, mkv3)


def _pool_pages_kernel(pt_sm, a_ref, cache_ref, o_ref, buf, sem, *, pages_per_step, page_base):
    step = pl.program_id(0)
    n_steps = pl.num_programs(0)
    slot = step % 2
    n_cols = buf.shape[3]

    def page_copy(step_idx, p, to_slot):
        page = pt_sm[step_idx * pages_per_step + p] + page_base
        return pltpu.make_async_copy(cache_ref.at[page, :, pl.ds(0, n_cols), :], buf.at[to_slot, p], sem.at[to_slot])

    @pl.when(step == 0)
    def _():
        for p in range(pages_per_step):
            page_copy(0, p, 0).start()

    @pl.when(step + 1 < n_steps)
    def _():
        for p in range(pages_per_step):
            page_copy(step + 1, p, 1 - slot).start()

    for p in range(pages_per_step):
        page_copy(step, p, slot).wait()

    a = a_ref[...]
    per = buf.shape[2] // CMP_BLOCK
    for p in range(pages_per_step):
        for k in range(per):
            x = buf[slot, p, pl.ds(k * CMP_BLOCK, CMP_BLOCK)]
            o_ref[0, p, k] = jnp.sum(x * a, axis=0)


def _pool_pages(cache4, page_flat, a3, *, bs, n_pages, page_base, pages_per_step=16):
    page = cache4.shape[1]
    per = page // CMP_BLOCK
    n_cols = 2 * NSA_KV
    total = bs * n_pages
    pages_per_step = min(pages_per_step, total)
    assert total % pages_per_step == 0
    n_steps = total // pages_per_step
    grid_spec = pltpu.PrefetchScalarGridSpec(
        num_scalar_prefetch=1, grid=(n_steps,),
        in_specs=[pl.BlockSpec((CMP_BLOCK, n_cols, LANE), lambda si, pt: (0, 0, 0)),
                  pl.BlockSpec(memory_space=pl.ANY)],
        out_specs=pl.BlockSpec((1, pages_per_step, per, n_cols, LANE), lambda si, pt: (si, 0, 0, 0, 0)),
        scratch_shapes=[pltpu.VMEM((2, pages_per_step, page, n_cols, LANE), F32),
                        pltpu.SemaphoreType.DMA((2,))])
    out = pl.pallas_call(
        functools.partial(_pool_pages_kernel, pages_per_step=pages_per_step, page_base=page_base),
        out_shape=jax.ShapeDtypeStruct((n_steps, pages_per_step, per, n_cols, LANE), F32),
        grid_spec=grid_spec,
        compiler_params=_params(("arbitrary",)),
        name="pool_pages",
    )(page_flat, a3, cache4)
    return out.reshape(bs, n_pages * per, n_cols * LANE)


def _cmp_proj_kernel(p_ref, w_ref, kc_ref, vc_ref):
    pooled = p_ref[0]
    n = pooled.shape[0]
    n_pad = kc_ref.shape[2]
    for c in range(4):
        r = _nn(pooled[:, c * LANE:(c + 1) * LANE].astype(BF16), w_ref[c]).astype(BF16)
        dst = kc_ref if c < 2 else vc_ref
        if n_pad > n:
            dst[0, c % 2] = jnp.zeros((n_pad, LANE), BF16)
        dst[0, c % 2, 0:n, :] = r


def _cmp_proj(pooled, w4):
    b, n, _ = pooled.shape
    n_pad = -(-n // LANE) * LANE
    spec = pl.BlockSpec((1, NSA_KV, n_pad, LANE), lambda bi: (bi, 0, 0, 0))
    return pl.pallas_call(
        _cmp_proj_kernel,
        out_shape=[jax.ShapeDtypeStruct((b, NSA_KV, n_pad, LANE), BF16)] * 2,
        grid=(b,),
        in_specs=[pl.BlockSpec((1, n, 512), lambda bi: (bi, 0, 0)),
                  pl.BlockSpec((4, LANE, LANE), lambda bi: (0, 0, 0))],
        out_specs=[spec, spec],
        compiler_params=_params(("parallel",)),
        name="cmp_proj",
    )(pooled, w4)


def _lane_parts(x):
    return [x[:, j * LANE:(j + 1) * LANE] for j in range(x.shape[1] // LANE)]


def _nsa_prompt_kernel(*refs, t_len, tq, tc, tw, n_sel, top, sub, pool_pages, page_base):
    if pool_pages:
        (pt_sm, qc_ref, qr_ref, ng_ref, bg_ref, ksel_ref, vsel_ref, kwin_ref, vwin_ref, kc_ref, vc_ref,
         pa_ref, cache_ref, o_ref, pool_ref,
         qc_scr, qr_scr, s_scr, p_scr, a_scr, m_scr, l_scr, acc_scr, oc_scr, b_scr,
         sw_scr, pw_scr, bw_scr, ow_scr, wl_scr, pbuf, prow_scr, psem) = refs
        step = (pl.program_id(0) * pl.num_programs(1) + pl.program_id(1)) * pl.num_programs(2) + pl.program_id(2)
        n_steps = pl.num_programs(0) * pl.num_programs(1) * pl.num_programs(2)
        slot = step % 2
        half_rows, n_cols = pbuf.shape[2], 2 * NSA_KV

        def page_copies(step_idx, p, to_slot):
            page = pt_sm[step_idx * pool_pages + p] + page_base
            return [pltpu.make_async_copy(cache_ref.at[page, pl.ds(hh * half_rows, half_rows), pl.ds(0, n_cols), :],
                                          pbuf.at[to_slot, p, :, pl.ds(hh * n_cols, n_cols), :], psem.at[to_slot])
                    for hh in range(2)]

        @pl.when(step == 0)
        def _():
            for p in range(pool_pages):
                for cp in page_copies(0, p, 0):
                    cp.start()

        @pl.when(step + 1 < n_steps)
        def _():
            for p in range(pool_pages):
                for cp in page_copies(step + 1, p, 1 - slot):
                    cp.start()

        for p in range(pool_pages):
            for cp in page_copies(step, p, slot):
                cp.wait()
        pa = pa_ref[...]
        for p in range(pool_pages):
            prow_scr[p * 2 * n_cols:(p + 1) * 2 * n_cols, :] = jnp.sum(pbuf[slot, p] * pa, axis=0)
    else:
        (qc_ref, qr_ref, ng_ref, bg_ref, ksel_ref, vsel_ref, kwin_ref, vwin_ref, kc_ref, vc_ref,
         o_ref, qc_scr, qr_scr, s_scr, p_scr, a_scr, m_scr, l_scr, acc_scr, oc_scr, b_scr,
         sw_scr, pw_scr, bw_scr, ow_scr, wl_scr) = refs
    i = pl.program_id(2)
    hq = NSA_HPG
    for h in range(hq):
        qc_scr[pl.ds(h * tq, tq), :] = qc_ref[0, :, h * LANE:(h + 1) * LANE]
        qr_scr[pl.ds(h * tq, tq), 0:LANE] = qr_ref[0, :, h * LANE:(h + 1) * LANE]

    w0 = pl.multiple_of(jnp.clip(i * tq + tq - tw, 0, t_len - tw), LANE)
    kp = w0 + lax.broadcasted_iota(jnp.int32, (tq, tw), 1)
    tp = i * tq + lax.broadcasted_iota(jnp.int32, (tq, tw), 0)
    bw_scr[...] = jnp.where((kp <= tp) & (kp > tp - WINDOW), 0.0, MASK_NEG)
    sw_scr[...] = _nt(qr_scr[:, 0:LANE], kwin_ref[0, pl.ds(w0, tw), :])
    for r0 in range(0, hq * tq, sub):
        r = pl.ds(r0, sub)
        parts = _lane_parts(sw_scr[r, :] + bw_scr[pl.ds(r0 % tq, sub), :])
        m = jnp.max(functools.reduce(jnp.maximum, parts), axis=-1, keepdims=True)
        ews = [jnp.exp2(x - m) for x in parts]
        row_sum = jnp.sum(functools.reduce(jnp.add, ews), axis=-1, keepdims=True)
        wl_scr[r, :] = jnp.broadcast_to(1.0 / row_sum, (sub, LANE))
        pw_scr[r, :] = jnp.concatenate(ews, axis=1).astype(BF16)
    ow_scr[...] = _nn(pw_scr[...], vwin_ref[0, pl.ds(w0, tw), :])

    kc = kc_ref[0, 0]
    npad = kc.shape[0]
    s_scr[:, 0:npad] = _nt(qc_scr[...], kc)
    tpos = i * tq + lax.broadcasted_iota(jnp.int32, (tq, npad), 0)
    ncol = lax.broadcasted_iota(jnp.int32, (tq, npad), 1)
    cmask = (ncol + 1) * CMP_BLOCK <= tpos + 1
    imp = jnp.zeros((tq, npad), F32)
    for h in range(hq):
        r = pl.ds(h * tq, tq)
        s = jnp.where(cmask, s_scr[r, 0:npad], MASK_NEG)
        e = jnp.where(cmask, jnp.exp2(s - jnp.max(s, axis=-1, keepdims=True)), 0.0)
        p = e / jnp.maximum(jnp.sum(e, axis=-1, keepdims=True), 1e-30)
        p_scr[r, 0:npad] = p.astype(BF16)
        imp = imp + p
    oc_scr[...] = _nn(p_scr[:, 0:npad], vc_ref[0, 0])

    rows = min(npad, -(-n_sel // 8) * 8)
    imp_t = imp.T[0:rows]
    blk = lax.broadcasted_iota(jnp.int32, (rows, tq), 0)
    cur = lax.shift_right_logical(i * tq + lax.broadcasted_iota(jnp.int32, (rows, tq), 1), SEL_SHIFT)
    imp_t = jnp.where((blk == 0) | (blk == cur) | (blk == cur - 1), FORCE, imp_t)
    imp_t = jnp.where(blk > cur, -1.0, imp_t)
    imp_t = jnp.where(blk >= n_sel, -2.0, imp_t)
    rank = jnp.zeros((rows, tq), F32)
    for j in range(n_sel):
        a = imp_t[j:j + 1, :]
        ahead = (a > imp_t) | ((a == imp_t) & (blk > j))
        rank = rank + jnp.where(ahead, 1.0, 0.0)
    neg_t = jnp.where((rank < top) & (blk < n_sel) & (blk <= cur), 0.0, MASK_NEG)
    if npad > rows:
        neg_t = jnp.concatenate([neg_t, jnp.zeros((npad - rows, tq), F32)], axis=0)
    neg = neg_t.T.astype(BF16)
    for h in range(hq):
        qr_scr[pl.ds(h * tq, tq), LANE:2 * LANE] = neg

    m_scr[...] = jnp.full(m_scr.shape, MASK_NEG, F32)
    l_scr[...] = jnp.zeros(l_scr.shape, F32)
    acc_scr[...] = jnp.zeros(acc_scr.shape, F32)
    n_chunks = (i * tq + tq + tc - 1) // tc

    def chunk(ci, diagonal):
        k0 = pl.multiple_of(ci * tc, tc)
        v = vsel_ref[0, pl.ds(k0, tc), :]
        kb = lax.shift_right_logical(k0 + lax.broadcasted_iota(jnp.int32, (tc, npad), 0), SEL_SHIFT)
        jb = lax.broadcasted_iota(jnp.int32, (tc, npad), 1)
        k_aug = jnp.concatenate([ksel_ref[0, pl.ds(k0, tc), :], jnp.where(kb == jb, 1.0, 0.0).astype(BF16)], axis=1)
        s_scr[:, 0:tc] = _nt(qr_scr[...], k_aug)
        if diagonal:
            kp = k0 + lax.broadcasted_iota(jnp.int32, (tq, tc), 1)
            tp = i * tq + lax.broadcasted_iota(jnp.int32, (tq, tc), 0)
            b_scr[:, 0:tc] = jnp.where(kp <= tp, 0.0, MASK_NEG)
        for r0 in range(0, hq * tq, sub):
            r = pl.ds(r0, sub)
            sc = s_scr[r, 0:tc]
            if diagonal:
                sc = sc + b_scr[pl.ds(r0 % tq, sub), 0:tc]
            parts = _lane_parts(sc)
            m_prev = m_scr[r, :]
            m_new = jnp.maximum(m_prev, jnp.max(functools.reduce(jnp.maximum, parts), axis=-1, keepdims=True))
            alpha = jnp.exp2(m_prev - m_new)
            pes = [jnp.exp2(x - m_new) for x in parts]
            l_scr[r, :] = alpha * l_scr[r, :] + functools.reduce(jnp.add, pes)
            p_scr[r, 0:tc] = jnp.concatenate(pes, axis=1).astype(BF16)
            a_scr[r, :] = alpha
            m_scr[r, :] = m_new
        acc_scr[...] = a_scr[...] * acc_scr[...] + _nn(p_scr[:, 0:tc], v)

    def full_chunk(ci, carry):
        chunk(ci, False)
        return carry

    lax.fori_loop(0, n_chunks - 1, full_chunk, 0)
    chunk(n_chunks - 1, True)

    gate = jax.nn.sigmoid(bg_ref[0])
    for g in range(1, NSA_KV):
        gate = jnp.where(pl.program_id(1) == g, pltpu.roll(gate, LANE - g * hq * N_BRANCH, 1), gate)
    ng = ng_ref[0]
    for h in range(hq):
        r = slice(h * tq, (h + 1) * tq)
        o_s = acc_scr[r, :] * (1.0 / jnp.sum(l_scr[r, :], axis=-1, keepdims=True))
        o = (gate[:, 3 * h:3 * h + 1] * oc_scr[r, :] + gate[:, 3 * h + 1:3 * h + 2] * o_s
             + gate[:, 3 * h + 2:3 * h + 3] * (ow_scr[r, :] * wl_scr[r, :]))
        o_ref[0, :, h * LANE:(h + 1) * LANE] = (ng[:, h * LANE:(h + 1) * LANE] * o).astype(o_ref.dtype)

    if pool_pages:
        for col in range(n_cols):
            pool_ref[0, :, col * LANE:(col + 1) * LANE] = prow_scr[pl.ds(col, 2 * pool_pages, stride=n_cols), :]


def _nsa_prompt(qc3, qr3, ng3, bg3, kvb, kc, vc, *, tq=128, tc=512, sub=64, pool=None):
    b, t, _ = qc3.shape
    nq = t // tq
    tc = min(tc, t)
    tw = min(WINDOW + tq, t)
    n_sel = t // SEL_BLOCK
    top = min(TOP_N, n_sel)
    npad = kc.shape[2]
    gw = NSA_HPG * LANE
    rows = NSA_HPG * tq
    wide = max(tc, npad)
    kvspec = lambda k: pl.BlockSpec((1, t, LANE), lambda bi, gi, qi, *_, k=k: (bi, 0, k + gi))
    cspec = pl.BlockSpec((1, 1, npad, LANE), lambda bi, gi, qi, *_: (bi, gi, 0, 0))
    gspec = pl.BlockSpec((1, tq, gw), lambda bi, gi, qi, *_: (bi, qi, gi))
    in_specs = [gspec, gspec, gspec,
                pl.BlockSpec((1, tq, LANE), lambda bi, gi, qi, *_: (bi, qi, 0)),
                kvspec(0), kvspec(2), kvspec(4), kvspec(6), cspec, cspec]
    out_shape = [jax.ShapeDtypeStruct((b, t, NSA_HEADS * LANE), BF16)]
    out_specs = [pl.BlockSpec((1, tq, gw), lambda bi, gi, qi, *_: (bi, qi, gi))]
    assert tc % tq == 0 and npad == LANE
    scratch = ([pltpu.VMEM((rows, LANE), BF16), pltpu.VMEM((rows, 2 * LANE), BF16)]
               + [pltpu.VMEM((rows, wide), F32), pltpu.VMEM((rows, wide), BF16)]
               + [pltpu.VMEM((rows, LANE), F32)] * 5 + [pltpu.VMEM((tq, wide), F32)]
               + [pltpu.VMEM((rows, tw), F32), pltpu.VMEM((rows, tw), BF16), pltpu.VMEM((tq, tw), F32)]
               + [pltpu.VMEM((rows, LANE), F32)] * 2)
    args = [qc3, qr3, ng3, bg3, kvb, kvb, kvb, kvb, kc, vc]
    n_steps = b * NSA_KV * nq
    pool_pages, page_base, prefetch = 0, 0, []
    if pool is not None:
        cache4, page_flat, a3, page_base = pool
        page, n_cols = cache4.shape[1], 2 * NSA_KV
        if page_flat.shape[0] % n_steps == 0 and page == 2 * CMP_BLOCK and 2 * n_cols == 8:
            pool_pages = page_flat.shape[0] // n_steps
            prefetch = [page_flat]
            in_specs += [pl.BlockSpec((CMP_BLOCK, 2 * n_cols, LANE), lambda bi, gi, qi, *_: (0, 0, 0)),
                         pl.BlockSpec(memory_space=pl.ANY)]
            args += [jnp.concatenate([a3, a3], axis=1), cache4]
            out_shape.append(jax.ShapeDtypeStruct((n_steps, 2 * pool_pages, n_cols * LANE), F32))
            out_specs.append(pl.BlockSpec((1, 2 * pool_pages, n_cols * LANE),
                                          lambda bi, gi, qi, *_: ((bi * NSA_KV + gi) * nq + qi, 0, 0)))
            scratch += [pltpu.VMEM((2, pool_pages, CMP_BLOCK, 2 * n_cols, LANE), F32),
                        pltpu.VMEM((pool_pages * 2 * n_cols, LANE), F32), pltpu.SemaphoreType.DMA((2,))]
    kern = functools.partial(_nsa_prompt_kernel, t_len=t, tq=tq, tc=tc, tw=tw, n_sel=n_sel, top=top,
                             sub=min(sub, tq), pool_pages=pool_pages, page_base=page_base)
    res = pl.pallas_call(
        kern,
        out_shape=out_shape,
        grid_spec=pltpu.PrefetchScalarGridSpec(
            num_scalar_prefetch=len(prefetch), grid=(b, NSA_KV, nq),
            in_specs=in_specs, out_specs=out_specs, scratch_shapes=scratch),
        compiler_params=_params(("arbitrary", "arbitrary", "arbitrary")),
        name="nsa_prompt",
    )(*prefetch, *args)
    return (res[0], res[1]) if pool_pages else (res[0], None)


def _conv_kernel(h_ref, b_ref, c_ref, g_ref, prev_ref, w_ref, y_ref, st_ref, up_scr, *, t_real):
    u = c_ref[...] * h_ref[...]
    t = u.shape[1]
    up_scr[:, pl.ds(8 - (CONV_W - 1), CONV_W - 1), :] = prev_ref[...]
    up_scr[:, pl.ds(8, t), :] = u
    w = w_ref[...]
    y = w[0:1, :][None] * up_scr[:, pl.ds(6, t), :]
    y = y + w[1:2, :][None] * up_scr[:, pl.ds(7, t), :]
    y = y + w[2:3, :][None] * u
    y = b_ref[...] * y
    y_ref[...] = (_silu(g_ref[...]) * y).astype(y_ref.dtype)
    st_ref[...] = up_scr[:, pl.ds(6 + t_real, CONV_W - 1), :]


def _conv_mixer(z3, prev, w_conv, *, t_real):
    b, t, _ = z3.shape
    nc = CONV_DIM // LANE
    zspec = lambda off: pl.BlockSpec((b, t, LANE), lambda ci, off=off: (0, 0, off // LANE + ci))
    return pl.pallas_call(
        functools.partial(_conv_kernel, t_real=t_real),
        out_shape=[jax.ShapeDtypeStruct((b, t, CONV_DIM), BF16),
                   jax.ShapeDtypeStruct((b, CONV_W - 1, CONV_DIM), F32)],
        grid=(nc,),
        in_specs=[zspec(C_H), zspec(C_B), zspec(C_C), zspec(C_G),
                  pl.BlockSpec((b, CONV_W - 1, LANE), lambda ci: (0, 0, ci)),
                  pl.BlockSpec((CONV_W, LANE), lambda ci: (0, ci))],
        out_specs=[pl.BlockSpec((b, t, LANE), lambda ci: (0, 0, ci)),
                   pl.BlockSpec((b, CONV_W - 1, LANE), lambda ci: (0, 0, ci))],
        scratch_shapes=[pltpu.VMEM((b, t + 8, LANE), F32)],
        compiler_params=_params(("parallel",)),
        name="conv_mixer",
    )(z3, z3, z3, z3, prev, w_conv)


def _mem_heads(q, mg, kv_ref, store):
    nm = kv_ref.shape[1] // (2 * MEM_HEADS)
    for h in range(MEM_HEADS):
        lo, hi = h * LANE, (h + 1) * LANE
        k = kv_ref[0, pl.ds(h, nm, stride=2 * MEM_HEADS), :].astype(BF16)
        v = kv_ref[0, pl.ds(MEM_HEADS + h, nm, stride=2 * MEM_HEADS), :].astype(BF16)
        s = _nt((q[:, lo:hi] * ATTN_SCALE).astype(BF16), k)
        e = jnp.exp(s - jnp.max(s, axis=-1, keepdims=True))
        o = _nn(e.astype(BF16), v) / jnp.sum(e, axis=-1, keepdims=True)
        store(lo, hi, _silu(mg[:, lo:hi]) * o)


def _mem_attn_kernel(q_ref, mg_ref, kv_ref, o_ref):
    def store(lo, hi, y):
        o_ref[0, :, lo:hi] = y.astype(o_ref.dtype)

    _mem_heads(q_ref[0], mg_ref[0], kv_ref, store)


def _mem_attn(zb3, mkv, *, mq_off, mg_off, tq, kv_base=0):
    b, t, _ = zb3.shape
    wq = MEM_HEADS * LANE
    return pl.pallas_call(
        _mem_attn_kernel,
        out_shape=jax.ShapeDtypeStruct((b, t, wq), BF16),
        grid=(b, t // tq),
        in_specs=[pl.BlockSpec((1, tq, wq), lambda bi, ti: (bi, ti, mq_off // wq)),
                  pl.BlockSpec((1, tq, wq), lambda bi, ti: (bi, ti, mg_off // wq)),
                  pl.BlockSpec((1,) + mkv.shape[1:], lambda bi, ti: (bi + kv_base, 0, 0))],
        out_specs=pl.BlockSpec((1, tq, wq), lambda bi, ti: (bi, ti, 0)),
        compiler_params=_params(("parallel", "parallel")),
        name="mem_attn",
    )(zb3, zb3, mkv)


def _out_proj_kernel(x_ref, ya_ref, yb_ref, ym_ref, xs_ref, yas_ref, ybs_ref, yms_ref, w_ref, fg_ref,
                     o_ref, os_ref, w_scr, *, final):
    @pl.when(pl.program_id(0) == 0)
    def _():
        w_scr[...] = w_ref[...].astype(BF16)

    a, bw = CONV_DIM, CONV_DIM + NSA_HEADS * LANE

    def mix(x, ya, yb, ym):
        acc = _nn(ya, w_scr[0:a, :])
        acc = acc + _nn(yb, w_scr[a:bw, :])
        acc = acc + _nn(ym, w_scr[bw:, :])
        r = x + acc
        if final:
            r = r * lax.rsqrt(jnp.mean(r * r, axis=-1, keepdims=True) + NORM_EPS) * fg_ref[...]
        return r

    o_ref[...] = mix(x_ref[...], ya_ref[...], yb_ref[...], ym_ref[...])

    @pl.when(pl.program_id(0) == pl.num_programs(0) - 1)
    def _():
        os_ref[...] = mix(xs_ref[...], yas_ref[...], ybs_ref[...], yms_ref[...])


def _out_proj(x, ya, yb, ym, sample, w, fg, *, tm, final):
    m, d = x.shape
    row = lambda width: pl.BlockSpec((tm, width), lambda i: (i, 0))
    whole = lambda a: pl.BlockSpec(a.shape, lambda i: (0, 0), pipeline_mode=pl.Buffered(1))
    return pl.pallas_call(
        functools.partial(_out_proj_kernel, final=final),
        out_shape=[jax.ShapeDtypeStruct((m, d), F32), jax.ShapeDtypeStruct(sample[0].shape, F32)],
        grid=(m // tm,),
        in_specs=[row(d), row(ya.shape[1]), row(yb.shape[1]), row(ym.shape[1])] + [whole(a) for a in sample]
                 + [whole(w), pl.BlockSpec((1, d), lambda i: (0, 0), pipeline_mode=pl.Buffered(1))],
        out_specs=[row(d), pl.BlockSpec(sample[0].shape, lambda i: (0, 0))],
        scratch_shapes=[pltpu.VMEM(w.shape, BF16)],
        compiler_params=_params(("arbitrary",)),
        name="out_proj",
    )(x, ya, yb, ym, *sample, w, fg.reshape(1, d))


def _sample_pre_kernel(q_ref, ng_ref, bg_ref, kv0_ref, kv1_ref, kv2_ref, c_ref, s1_ref, s2_ref, kc_ref, vc_ref,
                       kvn_ref, wn_ref, qr_ref, oc_ref, ngo_ref, gate_ref, val_ref, *, past):
    tp = SAMPLE_T_PAD
    hq = NSA_HPG
    c, s1, s2 = c_ref[...], s1_ref[...], s2_ref[...]
    kv0, kv1, kv2 = kv0_ref[0], kv1_ref[0], kv2_ref[0]
    kvn_ref[0, :, 0:512] = kv0
    kvn_ref[0, :, 768:1024] = kv1[:, 256:512]
    wn_ref[0, :, 256:512] = kv2[:, 256:512]
    for g in range(NSA_KV):
        lo, hi = g * LANE, (g + 1) * LANE
        kvn_ref[0, :, 512 + lo:512 + hi] = _rope(kv1[:, lo:hi], c, s1, s2)
        wn_ref[0, :, lo:hi] = _rope(kv2[:, lo:hi], c, s1, s2)

    q = q_ref[0]
    ng = ng_ref[0]
    gates = jax.nn.sigmoid(bg_ref[0])
    for g in range(NSA_KV):
        qc_l, qr_l = [], []
        for h in range(hq):
            lo = (g * hq + h) * LANE
            qh = q[:, lo:lo + LANE]
            qc_l.append(qh * ATTN_SCALE)
            qr_l.append(_rope(qh, c, s1, s2) * ATTN_SCALE)
            ngo_ref[0, g, h * tp:(h + 1) * tp, :] = ng[:, lo:lo + LANE]
            for br in range(N_BRANCH):
                col = (g * hq + h) * N_BRANCH + br
                gate_ref[0, g, br, h * tp:(h + 1) * tp, :] = jnp.broadcast_to(gates[:, col:col + 1], (tp, LANE))
        qc = jnp.concatenate(qc_l, axis=0)
        qr_ref[0, g] = jnp.concatenate(qr_l, axis=0)

        kc = kc_ref[0, g]
        npad = kc.shape[0]
        s = _nt(qc.astype(BF16), kc)
        trow = lax.broadcasted_iota(jnp.int32, (hq * tp, npad), 0) % tp
        ncol = lax.broadcasted_iota(jnp.int32, (hq * tp, npad), 1)
        cmask = (ncol + 1) * CMP_BLOCK <= past + trow + 1
        s = jnp.where(cmask, s, MASK_NEG)
        e = jnp.where(cmask, jnp.exp(s - jnp.max(s, axis=-1, keepdims=True)), 0.0)
        p = e / jnp.maximum(jnp.sum(e, axis=-1, keepdims=True), 1e-30)
        oc_ref[0, g] = _nn(p.astype(BF16), vc_ref[0, g])
        imp = jnp.sum(p.reshape(hq, tp, npad), axis=0)

        blk = lax.broadcasted_iota(jnp.int32, (tp, npad), 1)
        cur = (past + lax.broadcasted_iota(jnp.int32, (tp, npad), 0)) // SEL_BLOCK
        val = jnp.where((blk == 0) | (blk == cur) | (blk == cur - 1), FORCE, imp)
        val = jnp.where(blk > cur, -1.0, val)
        val = jnp.where(blk >= past // SEL_BLOCK, -2.0, val)
        val_ref[0, g] = val


def _topk_kernel(val_ref, idx_ref, *, n_top):
    val = val_ref[...]
    rows, n = val.shape
    blk = lax.broadcasted_iota(jnp.int32, (rows, n), 1)
    lane = lax.broadcasted_iota(jnp.int32, (rows, LANE), 1)
    idx = jnp.zeros((rows, LANE), jnp.int32)
    for r in range(n_top):
        best = jnp.max(val, axis=-1, keepdims=True)
        j = jnp.min(jnp.where(val == best, blk, n), axis=-1, keepdims=True)
        idx = jnp.where(lane == r, j, idx)
        val = jnp.where(blk == j, -3e38, val)
    idx_ref[...] = idx


def _topk(val2, *, n_top):
    rows, n = val2.shape
    return pl.pallas_call(
        functools.partial(_topk_kernel, n_top=n_top),
        out_shape=jax.ShapeDtypeStruct((rows, LANE), jnp.int32),
        grid=(1,),
        in_specs=[pl.BlockSpec((rows, n), lambda i: (0, 0))],
        out_specs=pl.BlockSpec((rows, LANE), lambda i: (0, 0)),
        compiler_params=_params(("arbitrary",)),
        name="sample_topk",
    )(val2)


def _sample_pre(za3, zb3, bg3, tabs, kc, vc, *, past):
    bs, tp, _ = za3.shape
    npad = kc.shape[2]
    qw = NSA_HEADS * LANE
    kvblk = KV_OFF // 512
    zspec = lambda k: pl.BlockSpec((1, tp, 512), lambda bi, k=k: (bi, 0, kvblk + k))
    tspec = pl.BlockSpec((tp, LANE), lambda bi: (0, 0))
    cspec = pl.BlockSpec((1, NSA_KV, npad, LANE), lambda bi: (bi, 0, 0, 0))
    rows = NSA_HPG * tp
    gspec = pl.BlockSpec((1, NSA_KV, rows, LANE), lambda bi: (bi, 0, 0, 0))
    gshape = jax.ShapeDtypeStruct((bs, NSA_KV, rows, LANE), F32)
    return pl.pallas_call(
        functools.partial(_sample_pre_kernel, past=past),
        out_shape=[jax.ShapeDtypeStruct((bs, tp, 1024), F32),
                   jax.ShapeDtypeStruct((bs, tp, 512), F32),
                   gshape, gshape, gshape,
                   jax.ShapeDtypeStruct((bs, NSA_KV, N_BRANCH, rows, LANE), F32),
                   jax.ShapeDtypeStruct((bs, NSA_KV, tp, npad), F32)],
        grid=(bs,),
        in_specs=[pl.BlockSpec((1, tp, qw), lambda bi: (bi, 0, Q_OFF // qw)),
                  pl.BlockSpec((1, tp, qw), lambda bi: (bi, 0, NG_OFF // qw)),
                  pl.BlockSpec((1, tp, LANE), lambda bi: (bi, 0, 0)),
                  zspec(0), zspec(1), zspec(2), tspec, tspec, tspec, cspec, cspec],
        out_specs=[pl.BlockSpec((1, tp, 1024), lambda bi: (bi, 0, 0)),
                   pl.BlockSpec((1, tp, 512), lambda bi: (bi, 0, 0)),
                   gspec, gspec, gspec,
                   pl.BlockSpec((1, NSA_KV, N_BRANCH, rows, LANE), lambda bi: (bi, 0, 0, 0, 0)),
                   pl.BlockSpec((1, NSA_KV, tp, npad), lambda bi: (bi, 0, 0, 0))],
        compiler_params=_params(("parallel",)),
        name="sample_pre",
    )(za3, za3, bg3, zb3, zb3, zb3, *tabs, kc, vc)


def _div_pow2(x, n):
    assert n & (n - 1) == 0
    return x // n if isinstance(x, int) else lax.shift_right_logical(x, n.bit_length() - 1)


def _mod_pow2(x, n):
    assert n & (n - 1) == 0
    return x % n if isinstance(x, int) else x & (n - 1)


def _sample_attn_kernel(idx_sm, pt_sm, qr_ref, oc_ref, ng_ref, gate_ref, ksn_ref, vsn_ref,
                        wc_ref, kwn_ref, vwn_ref, wn_ref, cache_ref, o_ref, win_ref,
                        kbuf, vbuf, kw_scr, vw_scr, sem, *, ts, n_top, n_pages, page_base, per_page, wb):
    tp = SAMPLE_T_PAD
    hq = NSA_HPG
    b = pl.program_id(0)
    g = pl.program_id(1)
    n_gath = n_top * SEL_BLOCK
    ks_rows = kbuf.shape[2]
    step = b * NSA_KV + g
    n_steps = pl.num_programs(0) * NSA_KV
    slot = step % 2

    def gather_copies(step_idx, t, r, to_slot):
        bb, gg = _div_pow2(step_idx, NSA_KV), _mod_pow2(step_idx, NSA_KV)
        blk = idx_sm[(step_idx * ts + t) * n_top + r]
        page = pt_sm[bb * n_pages + _div_pow2(blk, per_page)] + page_base
        row0 = _mod_pow2(blk, per_page) * SEL_BLOCK
        src_k = cache_ref.at[page, pl.ds(row0, SEL_BLOCK), 2 * NSA_KV + gg]
        src_v = cache_ref.at[page, pl.ds(row0, SEL_BLOCK), 3 * NSA_KV + gg]
        dst = pl.ds(r * SEL_BLOCK, SEL_BLOCK)
        return (pltpu.make_async_copy(src_k, kbuf.at[to_slot, t, dst], sem.at[to_slot, 0]),
                pltpu.make_async_copy(src_v, vbuf.at[to_slot, t, dst], sem.at[to_slot, 1]))

    def start_gathers(step_idx, to_slot):
        for t in range(ts):
            for r in range(n_top):
                ck, cv = gather_copies(step_idx, t, r, to_slot)
                ck.start(priority=0)
                cv.start(priority=1)

    @pl.when(step == 0)
    def _():
        start_gathers(0, 0)

    @pl.when(step + 1 < n_steps)
    def _():
        start_gathers(step + 1, 1 - slot)

    @pl.when(g == 0)
    def _():
        n_w = 2 * NSA_KV
        kept = (wb - ts) * n_w
        win_ref[0, pl.ds(0, kept), :] = wc_ref[0, pl.ds(ts * n_w, kept), :]
        for part in range(n_w):
            win_ref[0, pl.ds(kept + part, ts, stride=n_w), :] = wn_ref[0, 0:ts, part * LANE:(part + 1) * LANE]

    qr = qr_ref[0, 0].astype(BF16)
    trow = lax.broadcasted_iota(jnp.int32, (hq * tp, 1), 0) % tp

    ww = kw_scr.shape[0]
    kw_scr[pl.ds(0, wb), :] = wc_ref[0, pl.ds(g, wb, stride=2 * NSA_KV), :]
    vw_scr[pl.ds(0, wb), :] = wc_ref[0, pl.ds(NSA_KV + g, wb, stride=2 * NSA_KV), :]
    kw_scr[pl.ds(wb, tp), :] = kwn_ref[0]
    vw_scr[pl.ds(wb, tp), :] = vwn_ref[0]
    kw_scr[pl.ds(wb + tp, ww - wb - tp), :] = jnp.zeros((ww - wb - tp, LANE), F32)
    vw_scr[pl.ds(wb + tp, ww - wb - tp), :] = jnp.zeros((ww - wb - tp, LANE), F32)
    sw = _nt(qr, kw_scr[...].astype(BF16))
    jw = lax.broadcasted_iota(jnp.int32, (hq * tp, ww), 1)
    rel = jw - wb
    okw = (rel <= trow) & (rel > trow - WINDOW) & (jw < wb + ts)
    sw = jnp.where(okw, sw, MASK_NEG)
    ew = jnp.where(okw, jnp.exp(sw - jnp.max(sw, axis=-1, keepdims=True)), 0.0)
    o_w = _nn(ew.astype(BF16), vw_scr[...].astype(BF16)) / jnp.sum(ew, axis=-1, keepdims=True)

    for t in range(ts):
        for r in range(n_top):
            ck, cv = gather_copies(step, t, r, slot)
            ck.wait()
            cv.wait()

    js = lax.broadcasted_iota(jnp.int32, (hq * tp, ks_rows), 1)
    o_s = jnp.zeros((hq * tp, LANE), F32)
    for t in range(ts):
        kbuf[slot, t, pl.ds(n_gath, tp), :] = ksn_ref[0]
        vbuf[slot, t, pl.ds(n_gath, tp), :] = vsn_ref[0]
        kbuf[slot, t, pl.ds(n_gath + tp, ks_rows - n_gath - tp), :] = jnp.zeros((ks_rows - n_gath - tp, LANE), F32)
        vbuf[slot, t, pl.ds(n_gath + tp, ks_rows - n_gath - tp), :] = jnp.zeros((ks_rows - n_gath - tp, LANE), F32)
        ss = _nt(qr, kbuf[slot, t].astype(BF16))
        oks = (js < n_gath) | ((js - n_gath <= t) & (js < n_gath + ts))
        ss = jnp.where(oks, ss, MASK_NEG)
        es = jnp.where(oks, jnp.exp(ss - jnp.max(ss, axis=-1, keepdims=True)), 0.0)
        ot = _nn(es.astype(BF16), vbuf[slot, t].astype(BF16)) / jnp.sum(es, axis=-1, keepdims=True)
        o_s = jnp.where(trow == t, ot, o_s)

    o = gate_ref[0, 0, 0] * oc_ref[0, 0] + gate_ref[0, 0, 1] * o_s + gate_ref[0, 0, 2] * o_w
    y = _silu(ng_ref[0, 0]) * o
    for h in range(hq):
        o_ref[0, :, h * LANE:(h + 1) * LANE] = y[h * tp:(h + 1) * tp].astype(o_ref.dtype)


def _sample_attn(idx_flat, page_flat, qr, oc, ngo, gates, kvn, cache_win_rows, wn, cache4, *,
                 ts, n_top, n_pages, page_base, win_base):
    bs = qr.shape[0]
    tp = SAMPLE_T_PAD
    rows = NSA_HPG * tp
    wb = cache_win_rows.shape[1] // (2 * NSA_KV)
    assert ts <= wb and (ts * 2 * NSA_KV) % 8 == 0
    per_page = cache4.shape[1] // SEL_BLOCK
    ks_rows = -(-(n_top * SEL_BLOCK + tp) // LANE) * LANE
    ww = -(-(wb + tp) // LANE) * LANE
    gspec = pl.BlockSpec((1, 1, rows, LANE), lambda bi, gi, *_: (bi, gi, 0, 0))
    newspec = lambda k: pl.BlockSpec((1, tp, LANE), lambda bi, gi, *_, k=k: (bi, 0, k + gi))
    grid_spec = pltpu.PrefetchScalarGridSpec(
        num_scalar_prefetch=2, grid=(bs, NSA_KV),
        in_specs=[gspec, gspec, gspec,
                  pl.BlockSpec((1, 1, N_BRANCH, rows, LANE), lambda bi, gi, *_: (bi, gi, 0, 0, 0)),
                  newspec(2 * NSA_KV), newspec(3 * NSA_KV),
                  pl.BlockSpec((1,) + cache_win_rows.shape[1:], lambda bi, gi, *_: (bi + win_base, 0, 0)),
                  newspec(0), newspec(NSA_KV),
                  pl.BlockSpec((1, tp, 2 * NSA_KV * LANE), lambda bi, gi, *_: (bi, 0, 0)),
                  pl.BlockSpec(memory_space=pl.ANY)],
        out_specs=[pl.BlockSpec((1, tp, NSA_HPG * LANE), lambda bi, gi, *_: (bi, 0, gi)),
                   pl.BlockSpec((1,) + cache_win_rows.shape[1:], lambda bi, gi, *_: (bi, 0, 0))],
        scratch_shapes=[pltpu.VMEM((2, ts, ks_rows, LANE), F32), pltpu.VMEM((2, ts, ks_rows, LANE), F32),
                        pltpu.VMEM((ww, LANE), F32), pltpu.VMEM((ww, LANE), F32),
                        pltpu.SemaphoreType.DMA((2, 2))])
    kern = functools.partial(_sample_attn_kernel, ts=ts, n_top=n_top, n_pages=n_pages,
                             page_base=page_base, per_page=per_page, wb=wb)
    return pl.pallas_call(
        kern,
        out_shape=[jax.ShapeDtypeStruct((bs, tp, NSA_HEADS * LANE), BF16),
                   jax.ShapeDtypeStruct((bs,) + cache_win_rows.shape[1:], F32)],
        grid_spec=grid_spec,
        compiler_params=_params(("arbitrary", "arbitrary")),
        name="sample_attn",
    )(idx_flat, page_flat, qr, oc, ngo, gates, kvn, kvn, cache_win_rows, wn, wn, wn, cache4)


def _rope_tables(first, n):
    f32 = np.float32
    freqs = np.power(f32(ROPE_THETA), -np.arange(ROPE_HALF, dtype=f32) * f32(2.0 / ROPE_DIM)).astype(f32)
    ang = (np.arange(first, first + n).astype(f32)[:, None] * freqs[None, :]).astype(f32)
    cos, sin = np.cos(ang).astype(f32), np.sin(ang).astype(f32)
    z16 = np.zeros((n, ROPE_HALF), f32)
    rest = LANE - ROPE_DIM
    c = np.concatenate([cos, cos, np.ones((n, rest), f32)], axis=1)
    s1 = np.concatenate([z16, sin, np.zeros((n, rest), f32)], axis=1)
    s2 = np.concatenate([-sin, z16, np.zeros((n, rest), f32)], axis=1)
    return jnp.asarray(c), jnp.asarray(s1), jnp.asarray(s2)


def _layer_weights(norm_g, w_in3, layer, w_conv, a_cmp, w_cmp, w_out):
    w_t3 = jnp.swapaxes(w_in3, 1, 2)
    assert w_t3.shape[1] == BG_SRC + BG_N + ZB_WIDTH and BG_SRC == ZA_WIDTH and BG_N % 8 == 0
    a4 = jnp.concatenate([a_cmp[0], a_cmp[0], a_cmp[1], a_cmp[1]], axis=1)
    a3 = jnp.stack([a_cmp[0], a_cmp[0], a_cmp[1], a_cmp[1]], axis=1)
    w4 = jnp.stack([w_cmp[0], w_cmp[0], w_cmp[1], w_cmp[1]]).astype(BF16)
    return dict(norm_g=norm_g, w_t3=w_t3, layer=layer, w_conv=w_conv, a4=a4, a3=a3, w4=w4,
                w_out=w_out)


def _prompt_layer(xp, xs2, mem_prompt, mem_norm_g, w_mem, lw, pool):
    b, t, d = xp.shape
    m = b * t
    x2 = xp.reshape(m, d)
    tabs = _rope_tables(0, t)
    wb = min(WINDOW, t)
    qc, qr, ng, bg, ya, conv_new, za_s, bg_s = _proj_conv(x2, lw["norm_g"], lw["w_t3"], lw["layer"], lw["w_conv"],
                                                          tabs, xs2, tm=min(512, t), seq_len=t)
    nm = mem_prompt.shape[1]
    mkv = _norm_matmul(mem_prompt.reshape(b * nm, d), mem_norm_g, w_mem, tm=min(512, b * nm), tn=512)
    mkv3 = mkv.reshape(b, nm * 2 * MEM_HEADS, LANE)
    kvn, kvb, win_rows, pooled_prompt, ym, zb_s = _proj_kv(x2, lw["norm_g"], lw["w_t3"], lw["layer"], tabs, lw["a4"],
                                                           xs2, mkv3, tm=wb, seq_len=t)
    kc, vc = _cmp_proj(pooled_prompt.reshape(b, t // CMP_BLOCK, -1), lw["w4"])
    r3 = lambda a: a.reshape(b, t, -1)
    yb, pooled = _nsa_prompt(r3(qc), r3(qr), r3(ng), r3(bg), r3(kvb), kc, vc, tq=min(256, t), pool=pool)
    mix_in = (x2, ya, yb.reshape(m, -1), ym)
    kv_new = kvn.reshape(b, t, 4, NSA_KV, HEAD_DIM)
    win_new = win_rows.reshape(b, wb, 2, NSA_KV, HEAD_DIM)
    mem_kv = mkv.reshape(b, nm, 2, MEM_HEADS, HEAD_DIM)
    return mix_in, kv_new, win_new, conv_new, mem_kv, pooled, (za_s, zb_s, bg_s)


def _sample_layer(xs_p, proj, ts, layer, cache4, page_flat, pooled, cache_win, state_conv, cache_mem, lw):
    bs, tp, d = xs_p.shape
    depth = cache_win.shape[0]
    pool, page = cache4.shape[0] // depth, cache4.shape[1]
    n_pages = page_flat.shape[0] // bs
    past = n_pages * page
    assert past % SEL_BLOCK == 0 and ts <= SEL_BLOCK and ts <= tp
    n_past = past // SEL_BLOCK
    n_top = min(TOP_N, n_past + 1) - 1
    m = bs * tp
    za, zb, bg = proj
    za3 = za.reshape(bs, tp, ZA_WIDTH)
    zb3 = zb.reshape(bs, tp, ZB_WIDTH)
    bg3 = bg.reshape(bs, tp, LANE)
    tabs = _rope_tables(past, tp)
    if pooled is None:
        pooled = _pool_pages(cache4, page_flat, lw["a3"], bs=bs, n_pages=n_pages, page_base=layer * pool)
    else:
        pooled = pooled.reshape(bs, n_pages * (page // CMP_BLOCK), 2 * NSA_KV * LANE)
    kc, vc = _cmp_proj(pooled, lw["w4"])
    kvn, wn, qr, oc, ngo, gates, val = _sample_pre(za3, zb3, bg3, tabs, kc, vc, past=past)
    idx = _topk(val.reshape(bs * NSA_KV * tp, val.shape[-1]), n_top=n_top)
    idx_flat = idx.reshape(bs, NSA_KV, tp, LANE)[:, :, :ts, :n_top].reshape(-1)
    wbuf = cache_win.shape[2]
    cache_win_rows = cache_win.reshape(depth * bs, wbuf * 2 * NSA_KV, HEAD_DIM)
    yb, win_rows = _sample_attn(idx_flat, page_flat, qr, oc, ngo, gates, kvn, cache_win_rows, wn, cache4,
                                ts=ts, n_top=n_top, n_pages=n_pages, page_base=layer * pool, win_base=layer * bs)
    ya, conv_new = _conv_mixer(za3, state_conv[layer], lw["w_conv"], t_real=ts)
    nm = cache_mem.shape[2]
    mem_rows = cache_mem.reshape(depth * bs, nm * 2 * MEM_HEADS, HEAD_DIM)
    ym = _mem_attn(zb3, mem_rows, mq_off=MQ_OFF, mg_off=MG_OFF, tq=tp, kv_base=layer * bs)
    mix_in = (xs_p.reshape(m, d), ya.reshape(m, -1), yb.reshape(m, -1), ym.reshape(m, -1))
    kv_new = kvn[:, :ts].reshape(bs, ts, 4, NSA_KV, HEAD_DIM)
    win_state = win_rows.reshape(bs, wbuf, 2, NSA_KV, HEAD_DIM)
    return mix_in, kv_new, win_state, conv_new


def kernel(x_prompt, x_sample, cache_kv, cache_win, state_conv, cache_mem, page_table, mem_prompt,
           norm_g, w_in, w_conv, a_cmp, w_cmp, mem_norm_g, w_mem_kv, w_out, final_g):
    depth = w_in.shape[0]
    ts = x_sample.shape[1]
    xp = x_prompt
    xs = jnp.pad(x_sample, ((0, 0), (0, SAMPLE_T_PAD - ts), (0, 0)))
    pool_size, page = cache_kv.shape[1], cache_kv.shape[2]
    cache4 = cache_kv.reshape(depth * pool_size, page, 4 * NSA_KV, HEAD_DIM)
    page_flat = page_table.reshape(-1).astype(jnp.int32)
    kv_p, win_p, conv_p, mem_p, kv_s, win_s, conv_s = [], [], [], [], [], [], []
    for l in range(depth):
        lw = _layer_weights(norm_g[l], w_in, l, w_conv[l], a_cmp[l], w_cmp[l], w_out[l])
        final = l == depth - 1
        mix_p, kvn, winn, convn, mkv, pooled, proj_s = _prompt_layer(
            xp, xs.reshape(-1, xs.shape[-1]), mem_prompt, mem_norm_g[l], w_mem_kv[l], lw,
            pool=(cache4, page_flat, lw["a3"], l * pool_size))
        kv_p.append(kvn)
        win_p.append(winn)
        conv_p.append(convn)
        mem_p.append(mkv)
        mix_s, kvn, winn, convn = _sample_layer(xs, proj_s, ts, l, cache4, page_flat, pooled, cache_win, state_conv,
                                                cache_mem, lw)
        kv_s.append(kvn)
        win_s.append(winn)
        conv_s.append(convn)
        out_p, out_s = _out_proj(*mix_p, mix_s, lw["w_out"], final_g, tm=min(512, mix_p[0].shape[0]), final=final)
        xp = out_p.reshape(xp.shape)
        xs = out_s.reshape(xs.shape)
    return (xp, xs[:, :ts], jnp.stack(kv_p), jnp.stack(win_p), jnp.stack(conv_p), jnp.stack(mem_p),
            jnp.stack(kv_s), jnp.stack(win_s), jnp.stack(conv_s))
```
